```python
import jax, jax.numpy as jnp
from jax import lax
import numpy as np

D_MODEL = 4096
BATCH = 8
SEQ = 4096
DEPTH = 1

D_RWKV = D_MODEL // 2
RWKV_HEAD_DIM = 64
RWKV_HEADS = D_RWKV // RWKV_HEAD_DIM
DECAY_LORA = max(32, int(round(1.8 * D_RWKV ** 0.5 / 32)) * 32)
ICLR_LORA = max(32, int(round(1.8 * D_RWKV ** 0.5 / 32)) * 32)
GATE_LORA = max(32, int(round(0.6 * D_RWKV ** 0.8 / 32)) * 32)
D_CONV = D_MODEL // 2
CONV_WIDTH = 3
D_FF = 4 * D_MODEL
RMS_EPS = 1e-5
LNX_EPS = 64e-5
L2_EPS = 1e-12

N_RWKV_COLS = 3 * D_RWKV + DECAY_LORA + ICLR_LORA + GATE_LORA
N_CONV_COLS = 3 * D_CONV
N_GATE_COLS = 2 * D_MODEL
N_IN_COLS = N_RWKV_COLS + N_CONV_COLS + N_GATE_COLS

kernel_name = "rwkv7_shortconv_gated_hybrid"


def rms_norm(x, g):
    xf = x.astype(jnp.float32)
    y = xf * lax.rsqrt(jnp.mean(xf * xf, axis=-1, keepdims=True) + RMS_EPS)
    return (y * g.astype(jnp.float32)).astype(x.dtype)


def token_shift(p):
    return jnp.pad(p[:, :-1], ((0, 0), (1, 0), (0, 0)))


def rwkv7_recurrence(r, decay, k, v, a, b):
    bsz, _, h, n = r.shape
    xs = tuple(jnp.swapaxes(t, 0, 1) for t in (r, decay, k, v, a, b))

    def step(S, inp):
        r_t, w_t, k_t, v_t, a_t, b_t = inp
        sa = jnp.einsum('bhij,bhj->bhi', S, a_t)
        S = S * w_t[:, :, None, :] + sa[..., None] * b_t[:, :, None, :] + v_t[..., None] * k_t[:, :, None, :]
        y_t = jnp.einsum('bhij,bhj->bhi', S, r_t)
        return S, y_t

    S0 = jnp.zeros((bsz, h, n, n), jnp.float32)
    _, ys = lax.scan(step, S0, xs)
    return jnp.swapaxes(ys, 0, 1)


def rwkv7_branch(p, shift_mu, w0, w_decay_up, a0, w_iclr_up, w_gate_up, k_k, k_a, r_k, lnx_w, lnx_b, w_out_a):
    bsz, t, _ = p.shape
    f32 = jnp.float32
    p_mix = p + (token_shift(p) - p) * shift_mu
    o1 = D_RWKV
    o2 = 2 * D_RWKV
    o3 = 3 * D_RWKV
    o4 = o3 + DECAY_LORA
    o5 = o4 + ICLR_LORA
    r, k, v, wd, ad, gd = jnp.split(p_mix, [o1, o2, o3, o4, o5], axis=-1)
    w_log = -jax.nn.softplus(-(w0 + jnp.tanh(wd) @ w_decay_up)) - 0.5
    decay = jnp.exp(-jnp.exp(w_log.astype(f32)))
    a = jax.nn.sigmoid(a0 + ad @ w_iclr_up)
    g = jax.nn.sigmoid(gd) @ w_gate_up
    hs = lambda z: z.astype(f32).reshape(bsz, t, RWKV_HEADS, RWKV_HEAD_DIM)
    kk = hs(k * k_k)
    kk = kk / jnp.maximum(jnp.sqrt(jnp.sum(kk * kk, axis=-1, keepdims=True)), L2_EPS)
    k = k * (1.0 + (a - 1.0) * k_a)
    r_h, k_h, v_h, a_h = hs(r), hs(k), hs(v), hs(a)
    decay_h = decay.reshape(bsz, t, RWKV_HEADS, RWKV_HEAD_DIM)
    y = rwkv7_recurrence(r_h, decay_h, k_h, v_h, -kk, kk * a_h)
    mu = jnp.mean(y, axis=-1, keepdims=True)
    var = jnp.mean(jnp.square(y - mu), axis=-1, keepdims=True)
    y = ((y - mu) * lax.rsqrt(var + LNX_EPS)).reshape(bsz, t, D_RWKV)
    y = y * lnx_w.astype(f32) + lnx_b.astype(f32)
    bonus = jnp.sum(r_h * k_h * r_k.astype(f32), axis=-1, keepdims=True) * v_h
    y = (y + bonus.reshape(bsz, t, D_RWKV)).astype(p.dtype)
    return (y * g) @ w_out_a


def short_conv_branch(p, conv_w, w_out_b):
    b_gate, c_gate, u = jnp.split(p, [D_CONV, 2 * D_CONV], axis=-1)
    z = c_gate * u
    z = lax.conv_general_dilated(
        z, conv_w[:, None, :], window_strides=(1,), padding=[(CONV_WIDTH - 1, 0)],
        dimension_numbers=('NWC', 'WIO', 'NWC'), feature_group_count=D_CONV)
    return (b_gate * z) @ w_out_b


def _fwd_setup_inputs(seed: int = 0) -> dict:
    key = jax.random.key(seed)
    ks = jax.random.split(key, 24)
    f32 = jnp.float32
    nrm = lambda k, shape, scale: (jax.random.normal(k, shape, f32) * scale)
    L = DEPTH
    return {
        "x": nrm(ks[0], (BATCH, SEQ, D_MODEL), 1.0),
        "norm_mix_w": 1.0 + nrm(ks[1], (L, D_MODEL), 0.02),
        "w_in": nrm(ks[2], (L, D_MODEL, N_IN_COLS), D_MODEL ** -0.5),
        "gate_bias": nrm(ks[3], (L, N_GATE_COLS), 0.1),
        "shift_mu": jax.random.uniform(ks[4], (L, N_RWKV_COLS), f32),
        "w0": -2.0 + nrm(ks[5], (L, D_RWKV), 0.5),
        "w_decay_up": nrm(ks[6], (L, DECAY_LORA, D_RWKV), 0.1 * DECAY_LORA ** -0.5),
        "a0": nrm(ks[7], (L, D_RWKV), 0.5),
        "w_iclr_up": nrm(ks[8], (L, ICLR_LORA, D_RWKV), 0.1 * ICLR_LORA ** -0.5),
        "w_gate_up": nrm(ks[9], (L, GATE_LORA, D_RWKV), GATE_LORA ** -0.5),
        "k_k": 0.85 + nrm(ks[10], (L, D_RWKV), 0.05),
        "k_a": 1.0 + nrm(ks[11], (L, D_RWKV), 0.05),
        "r_k": 0.5 + nrm(ks[12], (L, RWKV_HEADS, RWKV_HEAD_DIM), 0.1),
        "lnx_w": 1.0 + nrm(ks[13], (L, D_RWKV), 0.02),
        "lnx_b": nrm(ks[14], (L, D_RWKV), 0.02),
        "w_out_a": nrm(ks[15], (L, D_RWKV, D_MODEL), D_RWKV ** -0.5),
        "conv_w": nrm(ks[16], (L, CONV_WIDTH, D_CONV), CONV_WIDTH ** -0.5),
        "w_out_b": nrm(ks[17], (L, D_CONV, D_MODEL), D_CONV ** -0.5),
        "w_out": nrm(ks[18], (L, D_MODEL, D_MODEL), D_MODEL ** -0.5),
        "norm_mlp_w": 1.0 + nrm(ks[19], (L, D_MODEL), 0.02),
        "w_mlp_up": nrm(ks[20], (L, D_MODEL, D_FF), D_MODEL ** -0.5),
        "w_mlp_down": nrm(ks[21], (L, D_FF, D_MODEL), D_FF ** -0.5),
        "norm_final_w": 1.0 + nrm(ks[22], (D_MODEL,), 0.02),
    }


def _fwd_reference(x, norm_mix_w, w_in, gate_bias, shift_mu, w0, w_decay_up, a0, w_iclr_up, w_gate_up,
              k_k, k_a, r_k, lnx_w, lnx_b, w_out_a, conv_w, w_out_b, w_out,
              norm_mlp_w, w_mlp_up, w_mlp_down, norm_final_w):
    h = x
    for l in range(DEPTH):
        xn = rms_norm(h, norm_mix_w[l])
        p = xn @ w_in[l]
        p_rwkv, p_conv, p_gate = jnp.split(p, [N_RWKV_COLS, N_RWKV_COLS + N_CONV_COLS], axis=-1)
        y_a = rwkv7_branch(p_rwkv, shift_mu[l], w0[l], w_decay_up[l], a0[l], w_iclr_up[l], w_gate_up[l],
                           k_k[l], k_a[l], r_k[l], lnx_w[l], lnx_b[l], w_out_a[l])
        y_b = short_conv_branch(p_conv, conv_w[l], w_out_b[l])
        gates = jax.nn.sigmoid(p_gate + gate_bias[l])
        g_a, g_b = jnp.split(gates, 2, axis=-1)
        h = h + (g_a * y_a + g_b * y_b) @ w_out[l]
        hn = rms_norm(h, norm_mlp_w[l])
        h = h + jnp.square(jax.nn.relu(hn @ w_mlp_up[l])) @ w_mlp_down[l]
    return rms_norm(h, norm_final_w)


import jax as _jax
import jax.numpy as _jnp

TWIN_FORMAT = 'train_step'
FWD_PARAMS = ['x', 'norm_mix_w', 'w_in', 'gate_bias', 'shift_mu', 'w0', 'w_decay_up', 'a0', 'w_iclr_up', 'w_gate_up', 'k_k', 'k_a', 'r_k', 'lnx_w', 'lnx_b', 'w_out_a', 'conv_w', 'w_out_b', 'w_out', 'norm_mlp_w', 'w_mlp_up', 'w_mlp_down', 'norm_final_w']
TWIN_WEIGHTS = ['norm_mix_w', 'w_in', 'gate_bias', 'shift_mu', 'w0', 'w_decay_up', 'a0', 'w_iclr_up', 'w_gate_up', 'k_k', 'k_a', 'r_k', 'lnx_w', 'lnx_b', 'w_out_a', 'conv_w', 'w_out_b', 'w_out', 'norm_mlp_w', 'w_mlp_up', 'w_mlp_down', 'norm_final_w']
TWIN_DIFF_INPUT = 'x'
TWIN_INPUTS = ['x', 'norm_mix_w', 'w_in', 'gate_bias', 'shift_mu', 'w0', 'w_decay_up', 'a0', 'w_iclr_up', 'w_gate_up', 'k_k', 'k_a', 'r_k', 'lnx_w', 'lnx_b', 'w_out_a', 'conv_w', 'w_out_b', 'w_out', 'norm_mlp_w', 'w_mlp_up', 'w_mlp_down', 'norm_final_w', 'loss_target', 'm_norm_mix_w', 'm_w_in', 'm_gate_bias', 'm_shift_mu', 'm_w0', 'm_w_decay_up', 'm_a0', 'm_w_iclr_up', 'm_w_gate_up', 'm_k_k', 'm_k_a', 'm_r_k', 'm_lnx_w', 'm_lnx_b', 'm_w_out_a', 'm_conv_w', 'm_w_out_b', 'm_w_out', 'm_norm_mlp_w', 'm_w_mlp_up', 'm_w_mlp_down', 'm_norm_final_w', 'v_norm_mix_w', 'v_w_in', 'v_gate_bias', 'v_shift_mu', 'v_w0', 'v_w_decay_up', 'v_a0', 'v_w_iclr_up', 'v_w_gate_up', 'v_k_k', 'v_k_a', 'v_r_k', 'v_lnx_w', 'v_lnx_b', 'v_w_out_a', 'v_conv_w', 'v_w_out_b', 'v_w_out', 'v_norm_mlp_w', 'v_w_mlp_up', 'v_w_mlp_down', 'v_norm_final_w']
TWIN_OUTPUTS = ['loss', 'grad_x', 'grad_norm_mix_w', 'grad_w_in', 'grad_gate_bias', 'grad_shift_mu', 'grad_w0', 'grad_w_decay_up', 'grad_a0', 'grad_w_iclr_up', 'grad_w_gate_up', 'grad_k_k', 'grad_k_a', 'grad_r_k', 'grad_lnx_w', 'grad_lnx_b', 'grad_w_out_a', 'grad_conv_w', 'grad_w_out_b', 'grad_w_out', 'grad_norm_mlp_w', 'grad_w_mlp_up', 'grad_w_mlp_down', 'grad_norm_final_w', 'delta_norm_mix_w', 'delta_w_in', 'delta_gate_bias', 'delta_shift_mu', 'delta_w0', 'delta_w_decay_up', 'delta_a0', 'delta_w_iclr_up', 'delta_w_gate_up', 'delta_k_k', 'delta_k_a', 'delta_r_k', 'delta_lnx_w', 'delta_lnx_b', 'delta_w_out_a', 'delta_conv_w', 'delta_w_out_b', 'delta_w_out', 'delta_norm_mlp_w', 'delta_w_mlp_up', 'delta_w_mlp_down', 'delta_norm_final_w', 'new_m_norm_mix_w', 'new_m_w_in', 'new_m_gate_bias', 'new_m_shift_mu', 'new_m_w0', 'new_m_w_decay_up', 'new_m_a0', 'new_m_w_iclr_up', 'new_m_w_gate_up', 'new_m_k_k', 'new_m_k_a', 'new_m_r_k', 'new_m_lnx_w', 'new_m_lnx_b', 'new_m_w_out_a', 'new_m_conv_w', 'new_m_w_out_b', 'new_m_w_out', 'new_m_norm_mlp_w', 'new_m_w_mlp_up', 'new_m_w_mlp_down', 'new_m_norm_final_w', 'new_v_norm_mix_w', 'new_v_w_in', 'new_v_gate_bias', 'new_v_shift_mu', 'new_v_w0', 'new_v_w_decay_up', 'new_v_a0', 'new_v_w_iclr_up', 'new_v_w_gate_up', 'new_v_k_k', 'new_v_k_a', 'new_v_r_k', 'new_v_lnx_w', 'new_v_lnx_b', 'new_v_w_out_a', 'new_v_conv_w', 'new_v_w_out_b', 'new_v_w_out', 'new_v_norm_mlp_w', 'new_v_w_mlp_up', 'new_v_w_mlp_down', 'new_v_norm_final_w']
TWIN_LEAF_KINDS = {'loss': 'loss', 'grad_x': 'grad_x', 'grad_norm_mix_w': 'grad_w', 'grad_w_in': 'grad_w', 'grad_gate_bias': 'grad_w', 'grad_shift_mu': 'grad_w', 'grad_w0': 'grad_w', 'grad_w_decay_up': 'grad_w', 'grad_a0': 'grad_w', 'grad_w_iclr_up': 'grad_w', 'grad_w_gate_up': 'grad_w', 'grad_k_k': 'grad_w', 'grad_k_a': 'grad_w', 'grad_r_k': 'grad_w', 'grad_lnx_w': 'grad_w', 'grad_lnx_b': 'grad_w', 'grad_w_out_a': 'grad_w', 'grad_conv_w': 'grad_w', 'grad_w_out_b': 'grad_w', 'grad_w_out': 'grad_w', 'grad_norm_mlp_w': 'grad_w', 'grad_w_mlp_up': 'grad_w', 'grad_w_mlp_down': 'grad_w', 'grad_norm_final_w': 'grad_w', 'delta_norm_mix_w': 'delta_w', 'delta_w_in': 'delta_w', 'delta_gate_bias': 'delta_w', 'delta_shift_mu': 'delta_w', 'delta_w0': 'delta_w', 'delta_w_decay_up': 'delta_w', 'delta_a0': 'delta_w', 'delta_w_iclr_up': 'delta_w', 'delta_w_gate_up': 'delta_w', 'delta_k_k': 'delta_w', 'delta_k_a': 'delta_w', 'delta_r_k': 'delta_w', 'delta_lnx_w': 'delta_w', 'delta_lnx_b': 'delta_w', 'delta_w_out_a': 'delta_w', 'delta_conv_w': 'delta_w', 'delta_w_out_b': 'delta_w', 'delta_w_out': 'delta_w', 'delta_norm_mlp_w': 'delta_w', 'delta_w_mlp_up': 'delta_w', 'delta_w_mlp_down': 'delta_w', 'delta_norm_final_w': 'delta_w', 'new_m_norm_mix_w': 'new_m', 'new_m_w_in': 'new_m', 'new_m_gate_bias': 'new_m', 'new_m_shift_mu': 'new_m', 'new_m_w0': 'new_m', 'new_m_w_decay_up': 'new_m', 'new_m_a0': 'new_m', 'new_m_w_iclr_up': 'new_m', 'new_m_w_gate_up': 'new_m', 'new_m_k_k': 'new_m', 'new_m_k_a': 'new_m', 'new_m_r_k': 'new_m', 'new_m_lnx_w': 'new_m', 'new_m_lnx_b': 'new_m', 'new_m_w_out_a': 'new_m', 'new_m_conv_w': 'new_m', 'new_m_w_out_b': 'new_m', 'new_m_w_out': 'new_m', 'new_m_norm_mlp_w': 'new_m', 'new_m_w_mlp_up': 'new_m', 'new_m_w_mlp_down': 'new_m', 'new_m_norm_final_w': 'new_m', 'new_v_norm_mix_w': 'new_v', 'new_v_w_in': 'new_v', 'new_v_gate_bias': 'new_v', 'new_v_shift_mu': 'new_v', 'new_v_w0': 'new_v', 'new_v_w_decay_up': 'new_v', 'new_v_a0': 'new_v', 'new_v_w_iclr_up': 'new_v', 'new_v_w_gate_up': 'new_v', 'new_v_k_k': 'new_v', 'new_v_k_a': 'new_v', 'new_v_r_k': 'new_v', 'new_v_lnx_w': 'new_v', 'new_v_lnx_b': 'new_v', 'new_v_w_out_a': 'new_v', 'new_v_conv_w': 'new_v', 'new_v_w_out_b': 'new_v', 'new_v_w_out': 'new_v', 'new_v_norm_mlp_w': 'new_v', 'new_v_w_mlp_up': 'new_v', 'new_v_w_mlp_down': 'new_v', 'new_v_norm_final_w': 'new_v'}


def _forward(args):
    return _fwd_reference(*[args[k] for k in FWD_PARAMS])


def _output_shape():
    out = _jax.eval_shape(lambda: _forward(_fwd_setup_inputs(0)))
    return out.shape, out.dtype

N_MICROBATCH = 1
ADAM_LR = 0.001
ADAM_B1 = 0.9
ADAM_B2 = 0.999
ADAM_EPS = 1e-08
ADAM_WD = 0.01
ADAM_STEP = 10
PER_EXAMPLE_BATCH_AXIS = {'x': 0, 'loss_target': 0}
SHARED_INPUTS = []
_WEIGHT_DTYPES = {'norm_mix_w': _jnp.float32, 'w_in': _jnp.float32, 'gate_bias': _jnp.float32, 'shift_mu': _jnp.float32, 'w0': _jnp.float32, 'w_decay_up': _jnp.float32, 'a0': _jnp.float32, 'w_iclr_up': _jnp.float32, 'w_gate_up': _jnp.float32, 'k_k': _jnp.float32, 'k_a': _jnp.float32, 'r_k': _jnp.float32, 'lnx_w': _jnp.float32, 'lnx_b': _jnp.float32, 'w_out_a': _jnp.float32, 'conv_w': _jnp.float32, 'w_out_b': _jnp.float32, 'w_out': _jnp.float32, 'norm_mlp_w': _jnp.float32, 'w_mlp_up': _jnp.float32, 'w_mlp_down': _jnp.float32, 'norm_final_w': _jnp.float32}
MOMENT_SCALE = {'norm_mix_w': 5.010036e-02, 'w_in': 2.258233e-02, 'gate_bias': 7.959846e-03, 'shift_mu': 4.620360e-02, 'w0': 9.814372e-03, 'w_decay_up': 1.650154e-03, 'a0': 1.257980e-02, 'w_iclr_up': 1.021273e-02, 'w_gate_up': 2.872152e-02, 'k_k': 6.806974e-03, 'k_a': 2.824717e-02, 'r_k': 3.483485e-02, 'lnx_w': 1.548039e-02, 'lnx_b': 1.911097e-02, 'w_out_a': 1.912664e-02, 'conv_w': 3.003527e-02, 'w_out_b': 2.083811e-02, 'w_out': 2.825857e-02, 'norm_mlp_w': 3.692955e-02, 'w_mlp_up': 1.796115e-02, 'w_mlp_down': 3.402512e-02, 'norm_final_w': 8.035308e+00}


def _to_microbatches(a, axis):
    t = _jnp.moveaxis(a, axis, 0)
    t = t.reshape((N_MICROBATCH, t.shape[0] // N_MICROBATCH) + t.shape[1:])
    return _jnp.moveaxis(t, 1, axis + 1)


def setup_inputs(seed: int = 0) -> dict:
    inp = _fwd_setup_inputs(seed)
    key = _jax.random.fold_in(_jax.random.key(seed), 7919)
    shape, _ = _output_shape()
    out = dict(inp)
    out["loss_target"] = _jax.random.normal(_jax.random.fold_in(key, 0), shape, _jnp.float32)
    for i, name in enumerate(TWIN_WEIGHTS):
        w = inp[name].astype(_jnp.float32)
        if MOMENT_SCALE is None:
            s = _jnp.sqrt(_jnp.mean(_jnp.square(w)) + 1e-30)
        else:
            s = MOMENT_SCALE[name]
        km, kv = _jax.random.split(_jax.random.fold_in(key, i + 1))
        out[name] = w
        out["m_" + name] = s * _jax.random.normal(km, w.shape, _jnp.float32)
        out["v_" + name] = (s * s) * _jax.random.uniform(kv, w.shape, _jnp.float32, 0.5, 1.5)
    if N_MICROBATCH > 1:
        for name, axis in PER_EXAMPLE_BATCH_AXIS.items():
            out[name] = _to_microbatches(out[name], axis)
    return {'x': out['x'], 'norm_mix_w': out['norm_mix_w'], 'w_in': out['w_in'], 'gate_bias': out['gate_bias'], 'shift_mu': out['shift_mu'], 'w0': out['w0'], 'w_decay_up': out['w_decay_up'], 'a0': out['a0'], 'w_iclr_up': out['w_iclr_up'], 'w_gate_up': out['w_gate_up'], 'k_k': out['k_k'], 'k_a': out['k_a'], 'r_k': out['r_k'], 'lnx_w': out['lnx_w'], 'lnx_b': out['lnx_b'], 'w_out_a': out['w_out_a'], 'conv_w': out['conv_w'], 'w_out_b': out['w_out_b'], 'w_out': out['w_out'], 'norm_mlp_w': out['norm_mlp_w'], 'w_mlp_up': out['w_mlp_up'], 'w_mlp_down': out['w_mlp_down'], 'norm_final_w': out['norm_final_w'], 'loss_target': out['loss_target'], 'm_norm_mix_w': out['m_norm_mix_w'], 'm_w_in': out['m_w_in'], 'm_gate_bias': out['m_gate_bias'], 'm_shift_mu': out['m_shift_mu'], 'm_w0': out['m_w0'], 'm_w_decay_up': out['m_w_decay_up'], 'm_a0': out['m_a0'], 'm_w_iclr_up': out['m_w_iclr_up'], 'm_w_gate_up': out['m_w_gate_up'], 'm_k_k': out['m_k_k'], 'm_k_a': out['m_k_a'], 'm_r_k': out['m_r_k'], 'm_lnx_w': out['m_lnx_w'], 'm_lnx_b': out['m_lnx_b'], 'm_w_out_a': out['m_w_out_a'], 'm_conv_w': out['m_conv_w'], 'm_w_out_b': out['m_w_out_b'], 'm_w_out': out['m_w_out'], 'm_norm_mlp_w': out['m_norm_mlp_w'], 'm_w_mlp_up': out['m_w_mlp_up'], 'm_w_mlp_down': out['m_w_mlp_down'], 'm_norm_final_w': out['m_norm_final_w'], 'v_norm_mix_w': out['v_norm_mix_w'], 'v_w_in': out['v_w_in'], 'v_gate_bias': out['v_gate_bias'], 'v_shift_mu': out['v_shift_mu'], 'v_w0': out['v_w0'], 'v_w_decay_up': out['v_w_decay_up'], 'v_a0': out['v_a0'], 'v_w_iclr_up': out['v_w_iclr_up'], 'v_w_gate_up': out['v_w_gate_up'], 'v_k_k': out['v_k_k'], 'v_k_a': out['v_k_a'], 'v_r_k': out['v_r_k'], 'v_lnx_w': out['v_lnx_w'], 'v_lnx_b': out['v_lnx_b'], 'v_w_out_a': out['v_w_out_a'], 'v_conv_w': out['v_conv_w'], 'v_w_out_b': out['v_w_out_b'], 'v_w_out': out['v_w_out'], 'v_norm_mlp_w': out['v_norm_mlp_w'], 'v_w_mlp_up': out['v_w_mlp_up'], 'v_w_mlp_down': out['v_w_mlp_down'], 'v_norm_final_w': out['v_norm_final_w']}


def _loss(weights, diff, rest, loss_target):
    with _jax.named_scope("forward"):
        args = {**rest, TWIN_DIFF_INPUT: diff, **{k: w.astype(_WEIGHT_DTYPES[k]) for k, w in weights.items()}}
        y = _forward(args)
    with _jax.named_scope("loss_head"):
        err = _jnp.square(y.astype(_jnp.float32) - loss_target)
        return 0.5 * _jnp.sum(_jnp.mean(err, axis=-1)) if err.ndim else 0.5 * err


def _adamw(w, g, m, v):
    m = ADAM_B1 * m + (1.0 - ADAM_B1) * g
    v = ADAM_B2 * v + (1.0 - ADAM_B2) * _jnp.square(g)
    m_hat = m / (1.0 - ADAM_B1 ** ADAM_STEP)
    v_hat = v / (1.0 - ADAM_B2 ** ADAM_STEP)
    delta = -ADAM_LR * (m_hat / (_jnp.sqrt(v_hat) + ADAM_EPS) + ADAM_WD * w)
    return delta, m, v


def reference(x, norm_mix_w, w_in, gate_bias, shift_mu, w0, w_decay_up, a0, w_iclr_up, w_gate_up, k_k, k_a, r_k, lnx_w, lnx_b, w_out_a, conv_w, w_out_b, w_out, norm_mlp_w, w_mlp_up, w_mlp_down, norm_final_w, loss_target, m_norm_mix_w, m_w_in, m_gate_bias, m_shift_mu, m_w0, m_w_decay_up, m_a0, m_w_iclr_up, m_w_gate_up, m_k_k, m_k_a, m_r_k, m_lnx_w, m_lnx_b, m_w_out_a, m_conv_w, m_w_out_b, m_w_out, m_norm_mlp_w, m_w_mlp_up, m_w_mlp_down, m_norm_final_w, v_norm_mix_w, v_w_in, v_gate_bias, v_shift_mu, v_w0, v_w_decay_up, v_a0, v_w_iclr_up, v_w_gate_up, v_k_k, v_k_a, v_r_k, v_lnx_w, v_lnx_b, v_w_out_a, v_conv_w, v_w_out_b, v_w_out, v_norm_mlp_w, v_w_mlp_up, v_w_mlp_down, v_norm_final_w):
    given = dict(x=x, norm_mix_w=norm_mix_w, w_in=w_in, gate_bias=gate_bias, shift_mu=shift_mu, w0=w0, w_decay_up=w_decay_up, a0=a0, w_iclr_up=w_iclr_up, w_gate_up=w_gate_up, k_k=k_k, k_a=k_a, r_k=r_k, lnx_w=lnx_w, lnx_b=lnx_b, w_out_a=w_out_a, conv_w=conv_w, w_out_b=w_out_b, w_out=w_out, norm_mlp_w=norm_mlp_w, w_mlp_up=w_mlp_up, w_mlp_down=w_mlp_down, norm_final_w=norm_final_w, loss_target=loss_target, m_norm_mix_w=m_norm_mix_w, m_w_in=m_w_in, m_gate_bias=m_gate_bias, m_shift_mu=m_shift_mu, m_w0=m_w0, m_w_decay_up=m_w_decay_up, m_a0=m_a0, m_w_iclr_up=m_w_iclr_up, m_w_gate_up=m_w_gate_up, m_k_k=m_k_k, m_k_a=m_k_a, m_r_k=m_r_k, m_lnx_w=m_lnx_w, m_lnx_b=m_lnx_b, m_w_out_a=m_w_out_a, m_conv_w=m_conv_w, m_w_out_b=m_w_out_b, m_w_out=m_w_out, m_norm_mlp_w=m_norm_mlp_w, m_w_mlp_up=m_w_mlp_up, m_w_mlp_down=m_w_mlp_down, m_norm_final_w=m_norm_final_w, v_norm_mix_w=v_norm_mix_w, v_w_in=v_w_in, v_gate_bias=v_gate_bias, v_shift_mu=v_shift_mu, v_w0=v_w0, v_w_decay_up=v_w_decay_up, v_a0=v_a0, v_w_iclr_up=v_w_iclr_up, v_w_gate_up=v_w_gate_up, v_k_k=v_k_k, v_k_a=v_k_a, v_r_k=v_r_k, v_lnx_w=v_lnx_w, v_lnx_b=v_lnx_b, v_w_out_a=v_w_out_a, v_conv_w=v_conv_w, v_w_out_b=v_w_out_b, v_w_out=v_w_out, v_norm_mlp_w=v_norm_mlp_w, v_w_mlp_up=v_w_mlp_up, v_w_mlp_down=v_w_mlp_down, v_norm_final_w=v_norm_final_w)
    weights = {n: given[n] for n in TWIN_WEIGHTS}
    shared = {n: given[n] for n in SHARED_INPUTS}
    per_example = {n: given[n] for n in ['x']}
    grad_fn = _jax.value_and_grad(_loss, argnums=(0, 1))

    def one_microbatch(ex, loss_target):
        ex = dict(ex)
        diff = ex.pop(TWIN_DIFF_INPUT)
        return grad_fn(weights, diff, {**shared, **ex}, loss_target)

    if N_MICROBATCH == 1:
        loss, (grad_w, grad_x) = one_microbatch(per_example, given["loss_target"])
    else:
        def body(carry, xs):
            loss_sum, grad_sum = carry
            l_k, (gw_k, gx_k) = one_microbatch(xs[0], xs[1])
            with _jax.named_scope("update"):
                return (loss_sum + l_k, _jax.tree.map(_jnp.add, grad_sum, gw_k)), gx_k

        init = (_jnp.zeros((), _jnp.float32), _jax.tree.map(_jnp.zeros_like, weights))
        (loss, grad_w), grad_x = _jax.lax.scan(body, init, (per_example, given["loss_target"]))
    with _jax.named_scope("update"):
        delta_w, new_m, new_v = {}, {}, {}
        for n in TWIN_WEIGHTS:
            delta_w[n], new_m[n], new_v[n] = _adamw(weights[n], grad_w[n], given["m_" + n], given["v_" + n])
    return (loss, grad_x, *[grad_w[n] for n in TWIN_WEIGHTS], *[delta_w[n] for n in TWIN_WEIGHTS],
            *[new_m[n] for n in TWIN_WEIGHTS], *[new_v[n] for n in TWIN_WEIGHTS])
```

```python
import functools
import math

import jax
import jax.numpy as jnp
from jax import lax
from jax.experimental import pallas as pl
from jax.experimental.pallas import tpu as pltpu

F32 = jnp.float32
BF16 = jnp.bfloat16
HIGHEST = lax.Precision.HIGHEST
MESH = pl.DeviceIdType.MESH

N_DEV = 8
HEAD = 64
LANES = 128
SUBLANES = 8
CHUNK = 64
RMS_EPS = 1e-5
LNX_EPS = 64e-5
L2_EPS = 1e-12
ADAM_LR = 0.001
ADAM_B1 = 0.9
ADAM_B2 = 0.999
ADAM_EPS = 1e-08
ADAM_WD = 0.01
ADAM_STEP = 10
VMEM_LIMIT = 48 * 1024 * 1024


def _cparams(sem):
    return pltpu.CompilerParams(dimension_semantics=sem, vmem_limit_bytes=VMEM_LIMIT)


def _tile(dim, cands):
    for c in cands:
        if c <= dim and dim % c == 0:
            return c
    return dim


_DIMS = {"nn": ((1,), (0,)), "nt": ((1,), (1,)), "tn": ((0,), (0,))}


def _matmul(name, a, b, mode, out_dtypes, epi=None, extras=()):
    if mode == "nn":
        (m, k), n = a.shape, b.shape[1]
    elif mode == "nt":
        (m, k), n = a.shape, b.shape[0]
    else:
        (k, m), n = a.shape, b.shape[1]
    tm = _tile(m, (1024, 512, 256, 128, 64, 32, 16, 8))
    tn = _tile(n, (1024, 512, 256, 128))
    tk = _tile(k, (512, 256, 128))
    nk = k // tk
    a_spec = pl.BlockSpec((tk, tm), lambda i, j, q: (q, i)) if mode == "tn" else pl.BlockSpec((tm, tk), lambda i, j, q: (i, q))
    b_spec = pl.BlockSpec((tn, tk), lambda i, j, q: (j, q)) if mode == "nt" else pl.BlockSpec((tk, tn), lambda i, j, q: (q, j))
    mn_spec = pl.BlockSpec((tm, tn), lambda i, j, q: (i, j))
    ne, no = len(extras), len(out_dtypes)
    dims = (_DIMS[mode], ((), ()))

    def body(a_ref, b_ref, *rest):
        extra_refs, out_refs, acc = rest[:ne], rest[ne:ne + no], rest[ne + no]
        q = pl.program_id(2)

        @pl.when(q == 0)
        def _():
            acc[...] = jnp.zeros_like(acc)

        acc[...] += lax.dot_general(a_ref[...], b_ref[...], dims, preferred_element_type=F32)

        @pl.when(q == nk - 1)
        def _():
            r = acc[...]
            outs = (r,) if epi is None else epi(r, *[e[...] for e in extra_refs])
            for o_ref, o in zip(out_refs, outs):
                o_ref[...] = o.astype(o_ref.dtype)

    outs = pl.pallas_call(
        body, name=name,
        out_shape=[jax.ShapeDtypeStruct((m, n), dt) for dt in out_dtypes],
        grid=(m // tm, n // tn, nk),
        in_specs=[a_spec, b_spec] + [mn_spec] * ne,
        out_specs=[mn_spec] * no,
        scratch_shapes=[pltpu.VMEM((tm, tn), F32)],
        compiler_params=_cparams(("parallel", "parallel", "arbitrary")),
    )(a, b, *extras)
    return outs


@jax.custom_vjp
def _mm(a, w):
    return jnp.dot(a.astype(BF16), w.astype(BF16), preferred_element_type=F32)


def _mm_fwd(a, w):
    return _mm(a, w), (a, w)


def _mm_bwd(res, ct):
    a, w = res
    ctb = ct.astype(BF16)
    da = lax.dot_general(ctb, w.astype(BF16), (((1,), (1,)), ((), ())), preferred_element_type=F32)
    dw = lax.dot_general(a.astype(BF16), ctb, (((0,), (0,)), ((), ())), preferred_element_type=F32)
    return da, dw


_mm.defvjp(_mm_fwd, _mm_bwd)


def _split3(x):
    hi = x.astype(BF16)
    r1 = x - hi.astype(F32)
    mid = r1.astype(BF16)
    lo = (r1 - mid.astype(F32)).astype(BF16)
    return hi, mid, lo


def _head_ones(width):
    r = lax.broadcasted_iota(jnp.int32, (width, width), 0) // HEAD
    c = lax.broadcasted_iota(jnp.int32, (width, width), 1) // HEAD
    return (r == c).astype(BF16)


@jax.custom_vjp
def _segsum(x):
    ones = _head_ones(x.shape[-1])
    out = None
    for piece in _split3(x):
        t = jnp.dot(piece, ones, preferred_element_type=F32)
        out = t if out is None else out + t
    return out


_segsum.defvjp(lambda x: (_segsum(x), None), lambda _, ct: (_segsum(ct),))


def _softplus(z):
    return jnp.maximum(z, 0.0) + jnp.log(1.0 + jnp.exp(-jnp.abs(z)))


def _sigmoid(z):
    return 1.0 / (1.0 + jnp.exp(-z))


def _rms(x, w):
    ms = jnp.mean(x * x, axis=-1, keepdims=True)
    return x * lax.rsqrt(ms + RMS_EPS) * w


def _row(ref, i):
    return ref[pl.ds(i, 1), :]


def _shift_down(x, prev_ref, n, first):
    rolled = pltpu.roll(x, n, 0)
    rows = lax.broadcasted_iota(jnp.int32, x.shape, 0)
    for q in range(n):
        halo = jnp.where(first, 0.0, _row(prev_ref, SUBLANES - n + q))
        rolled = jnp.where(rows == q, halo, rolled)
    return rolled


def _shift_up(x, next_ref, n, last):
    t = x.shape[0]
    rolled = pltpu.roll(x, t - n, 0)
    rows = lax.broadcasted_iota(jnp.int32, x.shape, 0)
    for q in range(n):
        halo = jnp.where(last, 0.0, _row(next_ref, q))
        rolled = jnp.where(rows == t - n + q, halo, rolled)
    return rolled


def _acc_out(ref, val, first):
    @pl.when(first)
    def _():
        ref[...] = val

    @pl.when(jnp.logical_not(first))
    def _():
        ref[...] += val


def _prep_fn(k, plm, w0, a0, kkw, kaw, wd, wi, wg):
    w_log = -_softplus(-(w0 + _mm(jnp.tanh(plm), wd))) - 0.5
    lw = -jnp.exp(w_log)
    a_g = _sigmoid(a0 + _mm(plm, wi))
    g = _mm(_sigmoid(plm), wg)
    kk = k * kkw
    kk = kk / jnp.maximum(jnp.sqrt(_segsum(kk * kk)), L2_EPS)
    k2 = k * (1.0 + (a_g - 1.0) * kaw)
    return lw, k2, -kk, kk * a_g, g


def _post_fn(y, r, k2, v, g, lnw, lnb, rk):
    mu = _segsum(y) * (1.0 / HEAD)
    yc = y - mu
    var = _segsum(yc * yc) * (1.0 / HEAD)
    yn = yc * lax.rsqrt(var + LNX_EPS) * lnw + lnb
    bonus = _segsum(r * k2 * rk) * v
    return (yn + bonus) * g


def _merge_fn(pga, pgb, ba, bb, ya, yb):
    return _sigmoid(pga + ba) * ya + _sigmoid(pgb + bb) * yb


def _hdot(a, b, dims=((1,), (0,))):
    return lax.dot_general(a, b, (dims, ((), ())), precision=HIGHEST, preferred_element_type=F32)


def _chunk_fn(s, r, lw, k, v, a, b):
    c = r.shape[0]
    c2 = 2 * c
    ri = lax.broadcasted_iota(jnp.int32, (c, c), 0)
    ci = lax.broadcasted_iota(jnp.int32, (c, c), 1)
    tri = (ri >= ci).astype(F32)
    cum = _hdot(tri, lw)
    tot = jnp.sum(lw, axis=0, keepdims=True)
    g_in, g_inv, g_out = jnp.exp(cum), jnp.exp(-cum), jnp.exp(tot - cum)
    lane_head = lax.broadcasted_iota(jnp.int32, (2, 1, LANES), 2) // HEAD
    which = lax.broadcasted_iota(jnp.int32, (2, 1, LANES), 0)
    hmask = (lane_head == which).astype(F32)

    def st(x):
        return (x[None] * hmask).reshape(c2, LANES)

    r2, a2 = st(r * g_in), st(a * jnp.exp(cum - lw))
    b2, k2, v2 = st(b * g_inv), st(k * g_inv), st(v)
    bo2, ko2 = st(b * g_out), st(k * g_out)
    r2i = lax.broadcasted_iota(jnp.int32, (c2, c2), 0)
    c2i = lax.broadcasted_iota(jnp.int32, (c2, c2), 1)
    same = (r2i >= c) == (c2i >= c)
    strict = jnp.logical_and(same, r2i > c2i)
    incl = jnp.logical_and(same, r2i >= c2i)
    nt = ((1,), (1,))
    lab = jnp.where(strict, _hdot(a2, b2, nt), 0.0)
    lak = jnp.where(strict, _hdot(a2, k2, nt), 0.0)
    mrb = jnp.where(incl, _hdot(r2, b2, nt), 0.0)
    mrk = jnp.where(incl, _hdot(r2, k2, nt), 0.0)
    x2 = _hdot(a2, s, nt) + _hdot(lak, v2)
    eye = (r2i == c2i).astype(F32)
    tinv = eye + lab
    pw = lab
    for _ in range(int(math.log2(c)) - 1):
        pw = _hdot(pw, pw)
        tinv = tinv + _hdot(tinv, pw)
    u2 = _hdot(tinv, x2)
    y2 = _hdot(r2, s, nt) + _hdot(mrb, u2) + _hdot(mrk, v2)
    y = jnp.sum(y2.reshape(2, c, LANES), axis=0)
    tn = ((0,), (0,))
    s_new = s * jnp.exp(tot) + _hdot(u2, bo2, tn) + _hdot(v2, ko2, tn)
    return y, s_new


def _norm_fwd(name, x, add, w, want_sum):
    t, d = x.shape
    tt = _tile(t, (128, 64, 32, 16, 8))
    row = pl.BlockSpec((tt, d), lambda i: (i, 0))
    par = pl.BlockSpec((1, d), lambda i: (0, 0))
    has_add = add is not None

    def body(*refs):
        x_ref = refs[0]
        add_ref = refs[1] if has_add else None
        w_ref = refs[1 + has_add]
        outs = refs[2 + has_add:]
        h = x_ref[...] + add_ref[...] if has_add else x_ref[...]
        if want_sum:
            outs[0][...] = h
        outs[-1][...] = _rms(h, w_ref[...]).astype(BF16)

    out_shape = ([jax.ShapeDtypeStruct((t, d), F32)] if want_sum else []) + [jax.ShapeDtypeStruct((t, d), BF16)]
    return pl.pallas_call(
        body, name=name, out_shape=out_shape, grid=(t // tt,),
        in_specs=[row] + ([row] if has_add else []) + [par],
        out_specs=[row] * len(out_shape),
        compiler_params=_cparams(("arbitrary",)),
    )(*([x] + ([add] if has_add else []) + [w]))


def _norm_bwd(name, xin, dy, dres, w):
    t, d = xin.shape
    tt = _tile(t, (128, 64, 32, 16, 8))
    row = pl.BlockSpec((tt, d), lambda i: (i, 0))
    par = pl.BlockSpec((1, d), lambda i: (0, 0))

    def body(x_ref, dy_ref, dres_ref, w_ref, dx_ref, dw_ref):
        _, vjp = jax.vjp(_rms, x_ref[...], w_ref[...])
        dx, dw = vjp(dy_ref[...])
        dx_ref[...] = dx + dres_ref[...]
        _acc_out(dw_ref, dw, pl.program_id(0) == 0)

    return pl.pallas_call(
        body, name=name,
        out_shape=[jax.ShapeDtypeStruct((t, d), F32), jax.ShapeDtypeStruct((1, d), F32)],
        grid=(t // tt,), in_specs=[row, row, row, par], out_specs=[row, par],
        compiler_params=_cparams(("arbitrary",)),
    )(xin, dy, dres, w)


def _final(name, h1, md, target, w):
    t, d = h1.shape
    tt = _tile(t, (128, 64, 32, 16, 8))
    row = pl.BlockSpec((tt, d), lambda i: (i, 0))
    par = pl.BlockSpec((1, d), lambda i: (0, 0))
    one = pl.BlockSpec((1, LANES), lambda i: (0, 0))

    def body(h1_ref, md_ref, tg_ref, w_ref, loss_ref, dh_ref, dw_ref):
        tg = tg_ref[...]

        def f(h, wv):
            err = _rms(h, wv) - tg
            return 0.5 * jnp.sum(jnp.mean(err * err, axis=-1, keepdims=True), axis=0, keepdims=True)

        loss, vjp = jax.vjp(f, h1_ref[...] + md_ref[...], w_ref[...])
        dh, dw = vjp(jnp.ones((1, 1), F32))
        dh_ref[...] = dh
        first = pl.program_id(0) == 0
        _acc_out(dw_ref, dw, first)
        _acc_out(loss_ref, jnp.broadcast_to(loss, (1, LANES)), first)

    return pl.pallas_call(
        body, name=name,
        out_shape=[jax.ShapeDtypeStruct((1, LANES), F32), jax.ShapeDtypeStruct((t, d), F32),
                   jax.ShapeDtypeStruct((1, d), F32)],
        grid=(t // tt,), in_specs=[row, row, row, par], out_specs=[one, row, par],
        compiler_params=_cparams(("arbitrary",)),
    )(h1, md, target, w)


def _halo_specs(tt, cb, nrow8, col_of):
    prev = pl.BlockSpec((SUBLANES, cb), lambda i, j: (jnp.maximum(i * (tt // SUBLANES) - 1, 0), col_of(j)))
    nxt = pl.BlockSpec((SUBLANES, cb), lambda i, j: (jnp.minimum((i + 1) * (tt // SUBLANES), nrow8 - 1), col_of(j)))
    return prev, nxt


def _mix_fwd(name, p_all, mu, width, cb):
    t = p_all.shape[0]
    tt = _tile(t, (256, 128, 64, 32, 16, 8))
    main = pl.BlockSpec((tt, cb), lambda i, j: (i, j))
    prev, _ = _halo_specs(tt, cb, t // SUBLANES, lambda j: j)
    par = pl.BlockSpec((1, cb), lambda i, j: (0, j))

    def body(p_ref, prev_ref, mu_ref, o_ref):
        p = p_ref[...]
        o_ref[...] = p + (_shift_down(p, prev_ref, 1, pl.program_id(0) == 0) - p) * mu_ref[...]

    return pl.pallas_call(
        body, name=name, out_shape=jax.ShapeDtypeStruct((t, width), F32),
        grid=(t // tt, width // cb), in_specs=[main, prev, par], out_specs=main,
        compiler_params=_cparams(("arbitrary", "arbitrary")),
    )(p_all, p_all, mu)


def _mix_bwd(name, dpm_list, p_all, col0, mu, cb):
    t, width = dpm_list[0].shape
    tt = _tile(t, (256, 128, 64, 32, 16, 8))
    n8 = t // SUBLANES
    nl = len(dpm_list)
    main = pl.BlockSpec((tt, cb), lambda j, i: (i, j))
    nxt = pl.BlockSpec((SUBLANES, cb), lambda j, i: (jnp.minimum((i + 1) * (tt // SUBLANES), n8 - 1), j))
    p_main = pl.BlockSpec((tt, cb), lambda j, i: (i, col0 + j))
    p_prev = pl.BlockSpec((SUBLANES, cb), lambda j, i: (jnp.maximum(i * (tt // SUBLANES) - 1, 0), col0 + j))
    par = pl.BlockSpec((1, cb), lambda j, i: (0, j))
    nt_ = t // tt

    def body(*refs):
        d_refs, dn_refs = refs[:nl], refs[nl:2 * nl]
        p_ref, pp_ref, mu_ref, dp_ref, dmu_ref, nx_scr = refs[2 * nl:]
        i = pl.program_id(1)
        dpm = d_refs[0][...]
        nx = dn_refs[0][...]
        for q in range(1, nl):
            dpm = dpm + d_refs[q][...]
            nx = nx + dn_refs[q][...]
        nx_scr[...] = nx
        mu_v = mu_ref[...]
        up = _shift_up(dpm, nx_scr, 1, i == nt_ - 1)
        dp_ref[...] = (dpm * (1.0 - mu_v) + up * mu_v).astype(BF16)
        p = p_ref[...]
        diff = _shift_down(p, pp_ref, 1, i == 0) - p
        _acc_out(dmu_ref, jnp.sum(dpm * diff, axis=0, keepdims=True), i == 0)

    return pl.pallas_call(
        body, name=name,
        out_shape=[jax.ShapeDtypeStruct((t, width), BF16), jax.ShapeDtypeStruct((1, width), F32)],
        grid=(width // cb, nt_),
        in_specs=[main] * nl + [nxt] * nl + [p_main, p_prev, par],
        out_specs=[main, par],
        scratch_shapes=[pltpu.VMEM((SUBLANES, cb), F32)],
        compiler_params=_cparams(("arbitrary", "arbitrary")),
    )(*dpm_list, *dpm_list, p_all, p_all, mu)


def _prep_fwd(name, pm, cfg, w0, a0, kkw, kaw, wd, wi, wg):
    t = pm.shape[0]
    dr, lp, cb = cfg["dr"], cfg["lp"], cfg["cb"]
    tt = _tile(t, (256, 128, 64, 32, 16, 8))
    nj = dr // cb
    kspec = pl.BlockSpec((tt, cb), lambda j, i: (i, nj + j))
    lspec = pl.BlockSpec((tt, lp), lambda j, i: (i, 3 * dr // lp))
    par = pl.BlockSpec((1, cb), lambda j, i: (0, j))
    wspec = pl.BlockSpec((lp, cb), lambda j, i: (0, j))
    out = pl.BlockSpec((tt, cb), lambda j, i: (i, j))

    def body(k_ref, l_ref, w0_ref, a0_ref, kk_ref, ka_ref, wd_ref, wi_ref, wg_ref, *outs):
        vals = _prep_fn(k_ref[...], l_ref[...], w0_ref[...], a0_ref[...], kk_ref[...], ka_ref[...],
                        wd_ref[...], wi_ref[...], wg_ref[...])
        for o_ref, val in zip(outs, vals):
            o_ref[...] = val

    return pl.pallas_call(
        body, name=name, out_shape=[jax.ShapeDtypeStruct((t, dr), F32)] * 5,
        grid=(nj, t // tt), in_specs=[kspec, lspec, par, par, par, par, wspec, wspec, wspec],
        out_specs=[out] * 5, compiler_params=_cparams(("arbitrary", "arbitrary")),
    )(pm, pm, w0, a0, kkw, kaw, wd, wi, wg)


def _prep_bwd(name, pm, cfg, w0, a0, kkw, kaw, wd, wi, wg, cts, dr_parts, dv_parts):
    t = pm.shape[0]
    dr, lp, cb = cfg["dr"], cfg["lp"], cfg["cb"]
    tt = _tile(t, (256, 128, 64, 32, 16, 8))
    nj = dr // cb
    kspec = pl.BlockSpec((tt, cb), lambda j, i: (i, nj + j))
    lspec = pl.BlockSpec((tt, lp), lambda j, i: (i, 3 * dr // lp))
    par = pl.BlockSpec((1, cb), lambda j, i: (0, j))
    wspec = pl.BlockSpec((lp, cb), lambda j, i: (0, j))
    blk = pl.BlockSpec((tt, cb), lambda j, i: (i, j))
    dpl_spec = pl.BlockSpec((None, tt, lp), lambda j, i: (j, i, 0))

    def body(k_ref, l_ref, w0_ref, a0_ref, kk_ref, ka_ref, wd_ref, wi_ref, wg_ref,
             dlw_ref, dk2a_ref, dk2b_ref, da_ref, db_ref, dg_ref, dr0_ref, dr1_ref, dv0_ref, dv1_ref,
             dpr_ref, dpk_ref, dpv_ref, dpl_ref, dw0_ref, da0_ref, dkk_ref, dka_ref, dwd_ref, dwi_ref, dwg_ref):
        _, vjp = jax.vjp(_prep_fn, k_ref[...], l_ref[...], w0_ref[...], a0_ref[...], kk_ref[...], ka_ref[...],
                         wd_ref[...], wi_ref[...], wg_ref[...])
        dk, dpl, dw0, da0, dkk, dka, dwd, dwi, dwg = vjp(
            (dlw_ref[...], dk2a_ref[...] + dk2b_ref[...], da_ref[...], db_ref[...], dg_ref[...]))
        dpr_ref[...] = dr0_ref[...] + dr1_ref[...]
        dpv_ref[...] = dv0_ref[...] + dv1_ref[...]
        dpk_ref[...] = dk
        dpl_ref[...] = dpl
        first = pl.program_id(1) == 0
        for ref, val in ((dw0_ref, dw0), (da0_ref, da0), (dkk_ref, dkk), (dka_ref, dka),
                         (dwd_ref, dwd), (dwi_ref, dwi), (dwg_ref, dwg)):
            _acc_out(ref, val, first)

    out_shape = ([jax.ShapeDtypeStruct((t, dr), F32)] * 3 + [jax.ShapeDtypeStruct((nj, t, lp), F32)]
                 + [jax.ShapeDtypeStruct((1, dr), F32)] * 4 + [jax.ShapeDtypeStruct((lp, dr), F32)] * 3)
    return pl.pallas_call(
        body, name=name, out_shape=out_shape, grid=(nj, t // tt),
        in_specs=[kspec, lspec, par, par, par, par, wspec, wspec, wspec] + [blk] * 10,
        out_specs=[blk] * 3 + [dpl_spec] + [par] * 4 + [wspec] * 3,
        compiler_params=_cparams(("arbitrary", "arbitrary")),
    )(pm, pm, w0, a0, kkw, kaw, wd, wi, wg, *cts, *dr_parts, *dv_parts)


def _post_specs(t, cfg):
    dr, cb = cfg["dr"], cfg["cb"]
    tt = _tile(t, (256, 128, 64, 32, 16, 8))
    nj = dr // cb
    blk = pl.BlockSpec((tt, cb), lambda j, i: (i, j))
    rspec = pl.BlockSpec((tt, cb), lambda j, i: (i, j))
    vspec = pl.BlockSpec((tt, cb), lambda j, i: (i, 2 * nj + j))
    par = pl.BlockSpec((1, cb), lambda j, i: (0, j))
    return tt, nj, blk, rspec, vspec, par


def _post_fwd(name, y, pm, k2, g, lnw, lnb, rk, cfg):
    t = y.shape[0]
    tt, nj, blk, rspec, vspec, par = _post_specs(t, cfg)

    def body(y_ref, r_ref, k_ref, v_ref, g_ref, lw_ref, lb_ref, rk_ref, o_ref):
        o_ref[...] = _post_fn(y_ref[...], r_ref[...], k_ref[...], v_ref[...], g_ref[...],
                              lw_ref[...], lb_ref[...], rk_ref[...]).astype(BF16)

    return pl.pallas_call(
        body, name=name, out_shape=jax.ShapeDtypeStruct((t, cfg["dr"]), BF16), grid=(nj, t // tt),
        in_specs=[blk, rspec, blk, vspec, blk, par, par, par], out_specs=blk,
        compiler_params=_cparams(("arbitrary", "arbitrary")),
    )(y, pm, k2, pm, g, lnw, lnb, rk)


def _post_bwd(name, y, pm, k2, g, lnw, lnb, rk, dout, cfg):
    t = y.shape[0]
    tt, nj, blk, rspec, vspec, par = _post_specs(t, cfg)

    def body(y_ref, r_ref, k_ref, v_ref, g_ref, lw_ref, lb_ref, rk_ref, do_ref,
             dy_ref, dr_ref, dk_ref, dv_ref, dg_ref, dlw_ref, dlb_ref, drk_ref):
        _, vjp = jax.vjp(_post_fn, y_ref[...], r_ref[...], k_ref[...], v_ref[...], g_ref[...],
                         lw_ref[...], lb_ref[...], rk_ref[...])
        dy, dr, dk, dv, dg, dlw, dlb, drk = vjp(do_ref[...])
        for ref, val in ((dy_ref, dy), (dr_ref, dr), (dk_ref, dk), (dv_ref, dv), (dg_ref, dg)):
            ref[...] = val
        first = pl.program_id(1) == 0
        for ref, val in ((dlw_ref, dlw), (dlb_ref, dlb), (drk_ref, drk)):
            _acc_out(ref, val, first)

    dr = cfg["dr"]
    return pl.pallas_call(
        body, name=name,
        out_shape=[jax.ShapeDtypeStruct((t, dr), F32)] * 5 + [jax.ShapeDtypeStruct((1, dr), F32)] * 3,
        grid=(nj, t // tt),
        in_specs=[blk, rspec, blk, vspec, blk, par, par, par, blk],
        out_specs=[blk] * 5 + [par] * 3,
        compiler_params=_cparams(("arbitrary", "arbitrary")),
    )(y, pm, k2, pm, g, lnw, lnb, rk, dout)


def _conv_specs(t, cfg):
    dc, cb = cfg["dc"], cfg["cb"]
    tt = _tile(t, (256, 128, 64, 32, 16, 8))
    nj = dc // cb
    c0 = cfg["off_conv"] // cb
    n8 = t // SUBLANES

    def sect(s):
        col = lambda j: c0 + s * nj + j
        main = pl.BlockSpec((tt, cb), lambda j, i: (i, col(j)))
        prev = pl.BlockSpec((SUBLANES, cb), lambda j, i: (jnp.maximum(i * (tt // SUBLANES) - 1, 0), col(j)))
        nxt = pl.BlockSpec((SUBLANES, cb), lambda j, i: (jnp.minimum((i + 1) * (tt // SUBLANES), n8 - 1), col(j)))
        return main, prev, nxt

    blk = pl.BlockSpec((tt, cb), lambda j, i: (i, j))
    wspec = pl.BlockSpec((SUBLANES, cb), lambda j, i: (0, j))
    return tt, nj, n8, sect, blk, wspec


def _conv_fwd(name, p_all, cw8, cfg):
    t = p_all.shape[0]
    tt, nj, n8, sect, blk, wspec = _conv_specs(t, cfg)
    (bm, _, _), (cm, cp, _), (um, up, _) = sect(0), sect(1), sect(2)

    def body(b_ref, c_ref, cp_ref, u_ref, up_ref, w_ref, o_ref, zp_scr):
        first = pl.program_id(1) == 0
        z = c_ref[...] * u_ref[...]
        zp_scr[...] = cp_ref[...] * up_ref[...]
        o = _row(w_ref, 2) * z + _row(w_ref, 1) * _shift_down(z, zp_scr, 1, first) \
            + _row(w_ref, 0) * _shift_down(z, zp_scr, 2, first)
        o_ref[...] = (b_ref[...] * o).astype(BF16)

    return pl.pallas_call(
        body, name=name, out_shape=jax.ShapeDtypeStruct((t, cfg["dc"]), BF16), grid=(nj, t // tt),
        in_specs=[bm, cm, cp, um, up, wspec], out_specs=blk,
        scratch_shapes=[pltpu.VMEM((SUBLANES, blk.block_shape[1]), F32)],
        compiler_params=_cparams(("arbitrary", "arbitrary")),
    )(p_all, p_all, p_all, p_all, p_all, cw8)


def _conv_bwd(name, p_all, cw8, dyb, cfg):
    t = p_all.shape[0]
    tt, nj, n8, sect, blk, wspec = _conv_specs(t, cfg)
    (bm, _, bn), (cm, cp, _), (um, up, _) = sect(0), sect(1), sect(2)
    cb = blk.block_shape[1]
    dnxt = pl.BlockSpec((SUBLANES, cb), lambda j, i: (jnp.minimum((i + 1) * (tt // SUBLANES), n8 - 1), j))
    nt_ = t // tt

    def body(b_ref, bn_ref, c_ref, cp_ref, u_ref, up_ref, w_ref, d_ref, dn_ref,
             db_ref, dc_ref, du_ref, dw_ref, zp_scr, don_scr):
        i = pl.program_id(1)
        first, last = i == 0, i == nt_ - 1
        c, u, b, dy = c_ref[...], u_ref[...], b_ref[...], d_ref[...]
        z = c * u
        zp_scr[...] = cp_ref[...] * up_ref[...]
        z1 = _shift_down(z, zp_scr, 1, first)
        z2 = _shift_down(z, zp_scr, 2, first)
        w0, w1, w2 = _row(w_ref, 0), _row(w_ref, 1), _row(w_ref, 2)
        o = w2 * z + w1 * z1 + w0 * z2
        do = dy * b
        don_scr[...] = dn_ref[...] * bn_ref[...]
        dz = w2 * do + w1 * _shift_up(do, don_scr, 1, last) + w0 * _shift_up(do, don_scr, 2, last)
        db_ref[...] = (dy * o).astype(BF16)
        dc_ref[...] = (dz * u).astype(BF16)
        du_ref[...] = (dz * c).astype(BF16)
        rows = lax.broadcasted_iota(jnp.int32, (SUBLANES, cb), 0)
        s0 = jnp.sum(do * z2, axis=0, keepdims=True)
        s1 = jnp.sum(do * z1, axis=0, keepdims=True)
        s2 = jnp.sum(do * z, axis=0, keepdims=True)
        dw = jnp.where(rows == 0, s0, jnp.where(rows == 1, s1, jnp.where(rows == 2, s2, 0.0)))
        _acc_out(dw_ref, dw, first)

    dc = cfg["dc"]
    return pl.pallas_call(
        body, name=name,
        out_shape=[jax.ShapeDtypeStruct((t, dc), BF16)] * 3 + [jax.ShapeDtypeStruct((SUBLANES, dc), F32)],
        grid=(nj, nt_),
        in_specs=[bm, bn, cm, cp, um, up, wspec, blk, dnxt],
        out_specs=[blk] * 3 + [wspec],
        scratch_shapes=[pltpu.VMEM((SUBLANES, cb), F32), pltpu.VMEM((SUBLANES, cb), F32)],
        compiler_params=_cparams(("arbitrary", "arbitrary")),
    )(p_all, p_all, p_all, p_all, p_all, p_all, cw8, dyb, dyb)


def _merge_specs(t, cfg):
    d, cb = cfg["d"], cfg["cb"]
    tt = _tile(t, (256, 128, 64, 32, 16, 8))
    nj = d // cb
    g0 = cfg["off_gate"] // cb
    ga = pl.BlockSpec((tt, cb), lambda j, i: (i, g0 + j))
    gb = pl.BlockSpec((tt, cb), lambda j, i: (i, g0 + nj + j))
    ba = pl.BlockSpec((1, cb), lambda j, i: (0, j))
    bb = pl.BlockSpec((1, cb), lambda j, i: (0, nj + j))
    blk = pl.BlockSpec((tt, cb), lambda j, i: (i, j))
    return tt, nj, ga, gb, ba, bb, blk


def _merge_fwd(name, p_all, bias, ya, yb, cfg):
    t = p_all.shape[0]
    tt, nj, ga, gb, ba, bb, blk = _merge_specs(t, cfg)

    def body(ga_ref, gb_ref, ba_ref, bb_ref, ya_ref, yb_ref, o_ref):
        o_ref[...] = _merge_fn(ga_ref[...], gb_ref[...], ba_ref[...], bb_ref[...],
                               ya_ref[...], yb_ref[...]).astype(BF16)

    return pl.pallas_call(
        body, name=name, out_shape=jax.ShapeDtypeStruct((t, cfg["d"]), BF16), grid=(nj, t // tt),
        in_specs=[ga, gb, ba, bb, blk, blk], out_specs=blk,
        compiler_params=_cparams(("arbitrary", "arbitrary")),
    )(p_all, p_all, bias, bias, ya, yb)


def _merge_bwd(name, p_all, bias, ya, yb, dm, cfg):
    t = p_all.shape[0]
    tt, nj, ga, gb, ba, bb, blk = _merge_specs(t, cfg)

    def body(ga_ref, gb_ref, ba_ref, bb_ref, ya_ref, yb_ref, dm_ref,
             dga_ref, dgb_ref, dya_ref, dyb_ref, dba_ref, dbb_ref):
        _, vjp = jax.vjp(_merge_fn, ga_ref[...], gb_ref[...], ba_ref[...], bb_ref[...], ya_ref[...], yb_ref[...])
        dga, dgb, dba, dbb, dya, dyb = vjp(dm_ref[...])
        for ref, val in ((dga_ref, dga), (dgb_ref, dgb), (dya_ref, dya), (dyb_ref, dyb)):
            ref[...] = val.astype(BF16)
        first = pl.program_id(1) == 0
        _acc_out(dba_ref, dba, first)
        _acc_out(dbb_ref, dbb, first)

    d = cfg["d"]
    par = pl.BlockSpec((1, blk.block_shape[1]), lambda j, i: (0, j))
    return pl.pallas_call(
        body, name=name,
        out_shape=[jax.ShapeDtypeStruct((t, d), BF16)] * 4 + [jax.ShapeDtypeStruct((1, d), F32)] * 2,
        grid=(nj, t // tt),
        in_specs=[ga, gb, ba, bb, blk, blk, blk], out_specs=[blk] * 4 + [par] * 2,
        compiler_params=_cparams(("arbitrary", "arbitrary")),
    )(p_all, p_all, bias, bias, ya, yb, dm)


def _rec_specs(t, cfg, rev):
    dr = cfg["dr"]
    nc = t // CHUNK
    hp = dr // LANES
    ch = (lambda c: nc - 1 - c) if rev else (lambda c: c)
    slab = pl.BlockSpec((CHUNK, LANES), lambda h, c: (ch(c), h))
    rspec = pl.BlockSpec((CHUNK, LANES), lambda h, c: (ch(c), h))
    vspec = pl.BlockSpec((CHUNK, LANES), lambda h, c: (ch(c), 2 * hp + h))
    sspec = pl.BlockSpec((None, None, LANES, LANES), lambda h, c: (ch(c), h, 0, 0))
    return nc, hp, slab, rspec, vspec, sspec


def _rec_fwd(name, pm, lw, k2, a, b, cfg):
    t = pm.shape[0]
    nc, hp, slab, rspec, vspec, sspec = _rec_specs(t, cfg, False)

    def body(r_ref, lw_ref, k_ref, v_ref, a_ref, b_ref, y_ref, s_ref, s_scr):
        @pl.when(pl.program_id(1) == 0)
        def _():
            s_scr[...] = jnp.zeros_like(s_scr)

        s = s_scr[...]
        s_ref[...] = s
        y, s_new = _chunk_fn(s, r_ref[...], lw_ref[...], k_ref[...], v_ref[...], a_ref[...], b_ref[...])
        y_ref[...] = y
        s_scr[...] = s_new

    return pl.pallas_call(
        body, name=name,
        out_shape=[jax.ShapeDtypeStruct((t, cfg["dr"]), F32), jax.ShapeDtypeStruct((nc, hp, LANES, LANES), F32)],
        grid=(hp, nc), in_specs=[rspec, slab, slab, vspec, slab, slab], out_specs=[slab, sspec],
        scratch_shapes=[pltpu.VMEM((LANES, LANES), F32)],
        compiler_params=_cparams(("arbitrary", "arbitrary")),
    )(pm, lw, k2, pm, a, b)


def _rec_bwd(name, pm, lw, k2, a, b, s_chk, dy, cfg):
    t = pm.shape[0]
    nc, hp, slab, rspec, vspec, sspec = _rec_specs(t, cfg, True)

    def body(r_ref, lw_ref, k_ref, v_ref, a_ref, b_ref, s_ref, dy_ref,
             dr_ref, dlw_ref, dk_ref, dv_ref, da_ref, db_ref, ds_scr):
        @pl.when(pl.program_id(1) == 0)
        def _():
            ds_scr[...] = jnp.zeros_like(ds_scr)

        _, vjp = jax.vjp(_chunk_fn, s_ref[...], r_ref[...], lw_ref[...], k_ref[...], v_ref[...],
                         a_ref[...], b_ref[...])
        ds, dr, dlw, dk, dv, da, db = vjp((dy_ref[...], ds_scr[...]))
        ds_scr[...] = ds
        for ref, val in ((dr_ref, dr), (dlw_ref, dlw), (dk_ref, dk), (dv_ref, dv), (da_ref, da), (db_ref, db)):
            ref[...] = val

    return pl.pallas_call(
        body, name=name, out_shape=[jax.ShapeDtypeStruct((t, cfg["dr"]), F32)] * 6,
        grid=(hp, nc), in_specs=[rspec, slab, slab, vspec, slab, slab, sspec, slab], out_specs=[slab] * 6,
        scratch_shapes=[pltpu.VMEM((LANES, LANES), F32)],
        compiler_params=_cparams(("arbitrary", "arbitrary")),
    )(pm, lw, k2, pm, a, b, s_chk, dy)


def _my_pos():
    return lax.axis_index("x"), lax.axis_index("y"), lax.axis_index("c")


def _all_gather_hbm(name, arrs):
    n = len(arrs)
    hbm = pl.BlockSpec(memory_space=pl.ANY)

    def body(*refs):
        in_refs, out_refs = refs[:n], refs[n:2 * n]
        send_sems, recv_sems, local_sems = refs[2 * n:]
        x, y, c = _my_pos()
        me, sibling = (x, y, c), (x, y, 1 - c)
        chips = [(1 - x, y), (x, 1 - y), (1 - x, 1 - y)]

        def copy(ai, k, block, to, src=None):
            px, py, pc = block
            dst = out_refs[ai].at[4 * px + 2 * py + pc]
            return pltpu.make_async_remote_copy(
                src_ref=dst if src is None else src, dst_ref=dst,
                send_sem=send_sems.at[ai * 7 + k], recv_sem=recv_sems.at[ai * 7 + k],
                device_id=to, device_id_type=MESH)

        mine, first, passed = [], [], []
        for ai in range(n):
            cp = pltpu.make_async_copy(in_refs[ai], out_refs[ai].at[4 * x + 2 * y + c], local_sems.at[ai])
            cp.start()
            mine.append(cp)
            f = [copy(ai, 0, me, sibling, src=in_refs[ai])]
            f += [copy(ai, 1 + j, me, (*chip, c), src=in_refs[ai]) for j, chip in enumerate(chips)]
            for cp in f:
                cp.start()
            first += f
        for ai in range(n):
            for j, chip in enumerate(chips):
                copy(ai, 1 + j, (*chip, c), me).wait_recv()
                fwd = copy(ai, 4 + j, (*chip, c), sibling)
                fwd.start()
                passed.append(fwd)
        for ai in range(n):
            copy(ai, 0, sibling, me).wait_recv()
            for j, chip in enumerate(chips):
                copy(ai, 4 + j, (*chip, 1 - c), me).wait_recv()
        for cp in first + passed:
            cp.wait_send()
        for cp in mine:
            cp.wait()

    return pl.pallas_call(
        body, name=name,
        out_shape=[jax.ShapeDtypeStruct((N_DEV,) + a.shape, a.dtype) for a in arrs],
        in_specs=[hbm] * n, out_specs=[hbm] * n,
        scratch_shapes=[pltpu.SemaphoreType.DMA((7 * n,)), pltpu.SemaphoreType.DMA((7 * n,)),
                        pltpu.SemaphoreType.DMA((n,))],
    )(*arrs)


def _exchange_hbm(name, arrs):
    n = len(arrs)
    hbm = pl.BlockSpec(memory_space=pl.ANY)

    def body(*refs):
        in_refs, out_refs = refs[:n], refs[n:2 * n]
        send_sems, recv_sems = refs[2 * n:]
        x, y, c = _my_pos()
        copies = []
        for ai in range(n):
            for r in range(1, N_DEV):
                px = 1 - x if r & 4 else x
                py = 1 - y if r & 2 else y
                pc = 1 - c if r & 1 else c
                cp = pltpu.make_async_remote_copy(
                    src_ref=in_refs[ai].at[4 * px + 2 * py + pc], dst_ref=out_refs[ai].at[r - 1],
                    send_sem=send_sems.at[ai * 7 + r - 1], recv_sem=recv_sems.at[ai * 7 + r - 1],
                    device_id=(px, py, pc), device_id_type=MESH)
                cp.start()
                copies.append(cp)
        for cp in copies:
            cp.wait()

    return pl.pallas_call(
        body, name=name,
        out_shape=[jax.ShapeDtypeStruct((N_DEV - 1,) + a.shape[1:], a.dtype) for a in arrs],
        in_specs=[hbm] * n, out_specs=[hbm] * n,
        scratch_shapes=[pltpu.SemaphoreType.DMA((7 * n,)), pltpu.SemaphoreType.DMA((7 * n,))],
    )(*arrs)


def _all_reduce_small(name, v):
    rows = v.shape[0]
    vm = pl.BlockSpec(memory_space=pltpu.VMEM)

    def body(x_ref, out_ref, buf, send_sems, recv_sems):
        x, y, c = _my_pos()
        me, sibling = (x, y, c), (x, y, 1 - c)
        chips = [(1 - x, y), (x, 1 - y), (1 - x, 1 - y)]

        def copy(k, block, to, src=None):
            px, py, pc = block
            dst = buf.at[4 * px + 2 * py + pc]
            return pltpu.make_async_remote_copy(
                src_ref=dst if src is None else src, dst_ref=dst,
                send_sem=send_sems.at[k], recv_sem=recv_sems.at[k], device_id=to, device_id_type=MESH)

        buf[4 * x + 2 * y + c] = x_ref[...]
        first = [copy(0, me, sibling, src=x_ref)]
        first += [copy(1 + j, me, (*chip, c), src=x_ref) for j, chip in enumerate(chips)]
        for cp in first:
            cp.start()
        passed = [copy(4 + j, (*chip, c), sibling) for j, chip in enumerate(chips)]
        for j, chip in enumerate(chips):
            copy(1 + j, (*chip, c), me).wait_recv()
            passed[j].start()
        copy(0, sibling, me).wait_recv()
        for j, chip in enumerate(chips):
            copy(4 + j, (*chip, 1 - c), me).wait_recv()
        for cp in first + passed:
            cp.wait_send()
        acc = buf[0]
        for d in range(1, N_DEV):
            acc = acc + buf[d]
        out_ref[...] = acc

    return pl.pallas_call(
        body, name=name, out_shape=jax.ShapeDtypeStruct(v.shape, F32),
        in_specs=[vm], out_specs=vm,
        scratch_shapes=[pltpu.VMEM((N_DEV, rows, LANES), F32), pltpu.SemaphoreType.DMA((7,)),
                        pltpu.SemaphoreType.DMA((7,))],
    )(v)


def _adamw(name, w, m, v, g_own, g_recv=None):
    rows, cols = w.shape
    per_el = 4 * 3 + g_own.dtype.itemsize + (7 * g_recv.dtype.itemsize if g_recv is not None else 0) + 16
    rb = SUBLANES * 2
    while rb * 2 <= rows and rows % (rb * 2) == 0 and rb * 2 * cols * per_el * 2 <= VMEM_LIMIT // 2:
        rb *= 2
    if rows % rb:
        rb = rows
    blk = pl.BlockSpec((rb, cols), lambda i: (i, 0))
    rblk = pl.BlockSpec((N_DEV - 1, rb, cols), lambda i: (0, i, 0))
    has_r = g_recv is not None
    bc1 = 1.0 - ADAM_B1 ** ADAM_STEP
    bc2 = 1.0 - ADAM_B2 ** ADAM_STEP

    def body(*refs):
        w_ref, m_ref, v_ref, go_ref = refs[:4]
        gr_ref = refs[4] if has_r else None
        g_out, d_out, m_out, v_out = refs[4 + has_r:]
        g = go_ref[...].astype(F32)
        if has_r:
            for r in range(N_DEV - 1):
                g = g + gr_ref[r].astype(F32)
        mn = ADAM_B1 * m_ref[...] + (1.0 - ADAM_B1) * g
        vn = ADAM_B2 * v_ref[...] + (1.0 - ADAM_B2) * (g * g)
        m_hat = mn / bc1
        v_hat = vn / bc2
        g_out[...] = g
        d_out[...] = -ADAM_LR * (m_hat / (jnp.sqrt(v_hat) + ADAM_EPS) + ADAM_WD * w_ref[...])
        m_out[...] = mn
        v_out[...] = vn

    return pl.pallas_call(
        body, name=name, out_shape=[jax.ShapeDtypeStruct((rows, cols), F32)] * 4, grid=(rows // rb,),
        in_specs=[blk] * 4 + ([rblk] if has_r else []), out_specs=[blk] * 4,
        compiler_params=_cparams(("arbitrary",)),
    )(*([w, m, v, g_own] + ([g_recv] if has_r else [])))


def _round_up(n, q):
    return (n + q - 1) // q * q


def _local_step(x, target, wts, cfg):
    dr, dc, d, lp, cb = cfg["dr"], cfg["dc"], cfg["d"], cfg["lp"], cfg["cb"]
    wmix = 3 * dr + lp
    (xn,) = _norm_fwd("norm_mix_fwd", x, None, wts["norm_mix_w"], False)
    (p_all,) = _matmul("mm_in", xn, wts["w_all"], "nn", [F32])
    pm = _mix_fwd("mix_fwd", p_all, wts["mu_pad"], wmix, cb)
    prep_w = (wts["w0"], wts["a0"], wts["k_k"], wts["k_a"], wts["wd"], wts["wi"], wts["wg"])
    lw, k2, a_in, b_in, g = _prep_fwd("prep_fwd", pm, cfg, *prep_w)
    y_raw, s_chk = _rec_fwd("rec_fwd", pm, lw, k2, a_in, b_in, cfg)
    post_w = (wts["lnx_w"], wts["lnx_b"], wts["r_k"])
    ya_in = _post_fwd("post_fwd", y_raw, pm, k2, g, *post_w, cfg)
    (ya,) = _matmul("mm_out_a", ya_in, wts["w_out_a"], "nn", [F32])
    yb_in = _conv_fwd("conv_fwd", p_all, wts["conv_w8"], cfg)
    (yb,) = _matmul("mm_out_b", yb_in, wts["w_out_b"], "nn", [F32])
    mg = _merge_fwd("merge_fwd", p_all, wts["gate_bias"], ya, yb, cfg)
    (mo,) = _matmul("mm_out", mg, wts["w_out"], "nn", [F32])
    h1, hn = _norm_fwd("norm_mlp_fwd", x, mo, wts["norm_mlp_w"], True)
    u, act = _matmul("mm_up", hn, wts["w_up"], "nn", [F32, BF16],
                     epi=lambda r: (r, jnp.square(jnp.maximum(r, 0.0))))
    (md,) = _matmul("mm_down", act, wts["w_down"], "nn", [F32])
    loss, dh2, g_norm_final = _final("final", h1, md, target, wts["norm_final_w"])
    dh2b = dh2.astype(BF16)
    (du,) = _matmul("mm_down_dx", dh2b, wts["w_down"], "nt", [BF16],
                    epi=lambda r, uu: (r * (2.0 * jnp.maximum(uu, 0.0)),), extras=(u,))
    (g_down,) = _matmul("mm_down_dw", act, dh2b, "tn", [BF16])
    (dhn,) = _matmul("mm_up_dx", du, wts["w_up"], "nt", [F32])
    (g_up,) = _matmul("mm_up_dw", hn, du, "tn", [BF16])
    dh1, g_norm_mlp = _norm_bwd("norm_mlp_bwd", h1, dhn, dh2, wts["norm_mlp_w"])
    dh1b = dh1.astype(BF16)
    (dmg,) = _matmul("mm_out_dx", dh1b, wts["w_out"], "nt", [F32])
    (g_out,) = _matmul("mm_out_dw", mg, dh1b, "tn", [BF16])
    dpga, dpgb, dya, dyb, dba, dbb = _merge_bwd("merge_bwd", p_all, wts["gate_bias"], ya, yb, dmg, cfg)
    (dya_in,) = _matmul("mm_out_a_dx", dya, wts["w_out_a"], "nt", [F32])
    (g_out_a,) = _matmul("mm_out_a_dw", ya_in, dya, "tn", [BF16])
    (dyb_in,) = _matmul("mm_out_b_dx", dyb, wts["w_out_b"], "nt", [F32])
    (g_out_b,) = _matmul("mm_out_b_dw", yb_in, dyb, "tn", [BF16])
    dpb, dpc, dpu, g_conv8 = _conv_bwd("conv_bwd", p_all, wts["conv_w8"], dyb_in, cfg)
    dy_raw, dr_post, dk_post, dv_post, dg, g_lnw, g_lnb, g_rk = _post_bwd(
        "post_bwd", y_raw, pm, k2, g, *post_w, dya_in, cfg)
    dr_rec, dlw, dk_rec, dv_rec, da_in, db_in = _rec_bwd("rec_bwd", pm, lw, k2, a_in, b_in, s_chk, dy_raw, cfg)
    (dpm_r, dpm_k, dpm_v, dpl, g_w0, g_a0, g_kk, g_ka, g_wd, g_wi, g_wg) = _prep_bwd(
        "prep_bwd", pm, cfg, *prep_w, (dlw, dk_rec, dk_post, da_in, db_in, dg),
        (dr_rec, dr_post), (dv_rec, dv_post))
    mu = wts["mu_pad"]
    nb = dr // cb
    dps, dmus = [], []
    for s, dpm_s in enumerate((dpm_r, dpm_k, dpm_v)):
        dp_s, dmu_s = _mix_bwd("mix_bwd_%d" % s, [dpm_s], p_all, s * nb, mu[:, s * dr:(s + 1) * dr], cb)
        dps.append(dp_s)
        dmus.append(dmu_s)
    dp_l, dmu_l = _mix_bwd("mix_bwd_l", [dpl[j] for j in range(nb)], p_all, 3 * nb, mu[:, 3 * dr:], min(cb, lp))
    dp_all = jnp.concatenate(dps + [dp_l, dpb, dpc, dpu, dpga, dpgb], axis=1)
    (dxn,) = _matmul("mm_in_dx", dp_all, wts["w_all"], "nt", [F32])
    (g_all,) = _matmul("mm_in_dw", xn, dp_all, "tn", [BF16])
    grad_x, g_norm_mix = _norm_bwd("norm_mix_bwd", x, dxn, dh1, wts["norm_mix_w"])
    grads = dict(
        norm_mix_w=g_norm_mix, w_all=g_all, gate_bias=jnp.concatenate([dba, dbb], axis=1),
        mu_pad=jnp.concatenate(dmus + [dmu_l], axis=1), w0=g_w0, a0=g_a0, k_k=g_kk, k_a=g_ka,
        wd=g_wd, wi=g_wi, wg=g_wg, r_k=g_rk, lnx_w=g_lnw, lnx_b=g_lnb, w_out_a=g_out_a, conv_w8=g_conv8,
        w_out_b=g_out_b, w_out=g_out, norm_mlp_w=g_norm_mlp, w_up=g_up, w_down=g_down,
        norm_final_w=g_norm_final)
    return loss, grad_x, grads


_SMALL = ("norm_mix_w", "gate_bias", "shift_mu", "w0", "a0", "k_k", "k_a", "r_k", "lnx_w", "lnx_b",
          "norm_mlp_w", "norm_final_w")
_ORDER = ("norm_mix_w", "w_in", "gate_bias", "shift_mu", "w0", "w_decay_up", "a0", "w_iclr_up", "w_gate_up",
          "k_k", "k_a", "r_k", "lnx_w", "lnx_b", "w_out_a", "conv_w", "w_out_b", "w_out", "norm_mlp_w",
          "w_mlp_up", "w_mlp_down", "norm_final_w")


def _step(x, target, w, m, v):
    t, d = x.shape[1], x.shape[2]
    dr = w["w0"].shape[-1]
    ld, li, lg = w["w_decay_up"].shape[1], w["w_iclr_up"].shape[1], w["w_gate_up"].shape[1]
    lora = ld + li + lg
    lp = _round_up(lora, LANES)
    dc = w["conv_w"].shape[-1] * N_DEV
    dff = w["w_mlp_up"].shape[-1] * N_DEV
    cb = math.gcd(math.gcd(lp, dr), 512)
    cfg = dict(d=d, dr=dr, dc=dc, lp=lp, cb=cb, off_conv=3 * dr + lp, off_gate=3 * dr + lp + 3 * dc)
    n_in = 3 * dr + lora + 3 * dc + 2 * d
    x2, tg2 = x[0], target[0]

    small_sh = jnp.concatenate([w["w_decay_up"][0], w["w_iclr_up"][0], w["w_gate_up"][0], w["conv_w"][0]], axis=0)
    small_rows = small_sh.shape[0]
    small_sh = jnp.pad(small_sh, ((0, _round_up(small_rows, SUBLANES) - small_rows), (0, 0)))
    big = ("w_in", "w_out_a", "w_out_b", "w_out", "w_mlp_up", "w_mlp_down")
    gathered = _all_gather_hbm("gather_weights", [w[n][0].astype(BF16) for n in big] + [small_sh])
    gw = dict(zip(big, gathered[:-1]))
    gsm = gathered[-1]

    def cols(a8):
        return jnp.transpose(a8, (1, 0, 2)).reshape(a8.shape[1], -1)

    w_in = cols(gw["w_in"])
    zpad = jnp.zeros((d, lp - lora), BF16)
    w_all = jnp.concatenate([w_in[:, :3 * dr + lora], zpad, w_in[:, 3 * dr + lora:]], axis=1)
    sm = cols(gsm)
    lora_full = sm[:lora]

    def lora_pad(lo, hi):
        rows = lax.broadcasted_iota(jnp.int32, (lp, 1), 0)
        full = jnp.pad(lora_full, ((0, lp - lora), (0, 0)))
        return jnp.where(jnp.logical_and(rows >= lo, rows < hi), full, 0.0)

    conv_w8 = jnp.pad(sm[lora:lora + 3], ((0, SUBLANES - 3), (0, 0)))
    mu_pad = jnp.pad(w["shift_mu"], ((0, 0), (0, lp - lora)))
    wts = dict(
        w_all=w_all, w_out_a=cols(gw["w_out_a"]), w_out_b=cols(gw["w_out_b"]),
        w_out=gw["w_out"].reshape(d, d), w_up=cols(gw["w_mlp_up"]), w_down=gw["w_mlp_down"].reshape(dff, d),
        wd=lora_pad(0, ld), wi=lora_pad(ld, ld + li), wg=lora_pad(ld + li, lora), conv_w8=conv_w8,
        mu_pad=mu_pad, norm_mix_w=w["norm_mix_w"], gate_bias=w["gate_bias"], w0=w["w0"], a0=w["a0"],
        k_k=w["k_k"], k_a=w["k_a"], r_k=w["r_k"].reshape(1, dr), lnx_w=w["lnx_w"], lnx_b=w["lnx_b"],
        norm_mlp_w=w["norm_mlp_w"], norm_final_w=w["norm_final_w"].reshape(1, d))

    loss, grad_x, gr = _local_step(x2, tg2, wts, cfg)

    def col_slabs(a):
        r_, c_ = a.shape
        return jnp.transpose(a.reshape(r_, N_DEV, c_ // N_DEV), (1, 0, 2))

    g_all = gr["w_all"]
    g_in = jnp.concatenate([g_all[:, :3 * dr + lora], g_all[:, 3 * dr + lp:]], axis=1)
    g_small = jnp.concatenate([gr["wd"][:ld], gr["wi"][ld:ld + li], gr["wg"][ld + li:lora], gr["conv_w8"][:3]], axis=0)
    g_small = jnp.pad(g_small, ((0, _round_up(small_rows, SUBLANES) - small_rows), (0, 0)))
    slabs = dict(
        w_in=col_slabs(g_in), w_out_a=col_slabs(gr["w_out_a"]), w_out_b=col_slabs(gr["w_out_b"]),
        w_out=gr["w_out"].reshape(N_DEV, d // N_DEV, d), w_mlp_up=col_slabs(gr["w_up"]),
        w_mlp_down=gr["w_down"].reshape(N_DEV, dff // N_DEV, d), small=col_slabs(g_small))
    names = big + ("small",)
    received = dict(zip(names, _exchange_hbm("exchange_grads", [slabs[n] for n in names])))
    x_i, y_i, c_i = _my_pos()
    me = 4 * x_i + 2 * y_i + c_i
    own = {n: lax.dynamic_index_in_dim(slabs[n], me, axis=0, keepdims=False) for n in names}

    small_g = dict(norm_mix_w=gr["norm_mix_w"], gate_bias=gr["gate_bias"], shift_mu=gr["mu_pad"][:, :3 * dr + lora],
                   w0=gr["w0"], a0=gr["a0"], k_k=gr["k_k"], k_a=gr["k_a"], r_k=gr["r_k"], lnx_w=gr["lnx_w"],
                   lnx_b=gr["lnx_b"], norm_mlp_w=gr["norm_mlp_w"], norm_final_w=gr["norm_final_w"])
    sizes = [small_g[n].size for n in _SMALL]
    total = sum(sizes) + 1
    prow = _round_up(total, LANES * SUBLANES) // LANES

    def pack(parts):
        flat = jnp.concatenate([p.reshape(-1) for p in parts])
        return jnp.pad(flat, (0, prow * LANES - flat.size)).reshape(prow, LANES)

    g_packed = _all_reduce_small("reduce_small", pack([small_g[n] for n in _SMALL] + [loss[0, :1]]))
    one = jnp.zeros((1,), F32)
    packed = [pack([d_[n] for n in _SMALL] + [one]) for d_ in (w, m, v)]
    sm_out = _adamw("adamw_small", *packed, g_packed)
    loss_out = g_packed.reshape(-1)[total - 1]

    def unpack(flat2d):
        flat = flat2d.reshape(-1)
        out, o = {}, 0
        for n, s in zip(_SMALL, sizes):
            out[n] = flat[o:o + s].reshape(w[n].shape)
            o += s
        return out

    res = [unpack(a) for a in sm_out]

    def shard2d(a):
        return a.reshape(-1, a.shape[-1])

    for n in big:
        outs = _adamw("adamw_" + n, shard2d(w[n]), shard2d(m[n]), shard2d(v[n]), shard2d(own[n]),
                      received[n].reshape((N_DEV - 1,) + shard2d(own[n]).shape))
        for r_, o in zip(res, outs):
            r_[n] = o.reshape(w[n].shape)
    sm_names = ("w_decay_up", "w_iclr_up", "w_gate_up", "conv_w")
    stack = lambda d_: jnp.pad(jnp.concatenate([d_[n][0] for n in sm_names], axis=0),
                               ((0, _round_up(small_rows, SUBLANES) - small_rows), (0, 0)))
    outs = _adamw("adamw_stack", stack(w), stack(m), stack(v), own["small"], received["small"])
    bounds = (0, ld, ld + li, lora, lora + 3)
    for r_, o in zip(res, outs):
        for q, n in enumerate(sm_names):
            r_[n] = o[bounds[q]:bounds[q + 1]].reshape(w[n].shape)

    grad, delta, new_m, new_v = res
    return (loss_out, grad_x[None], *[grad[n] for n in _ORDER], *[delta[n] for n in _ORDER],
            *[new_m[n] for n in _ORDER], *[new_v[n] for n in _ORDER])


def kernel(x, norm_mix_w, w_in, gate_bias, shift_mu, w0, w_decay_up, a0, w_iclr_up, w_gate_up, k_k, k_a, r_k, lnx_w, lnx_b, w_out_a, conv_w, w_out_b, w_out, norm_mlp_w, w_mlp_up, w_mlp_down, norm_final_w, loss_target, m_norm_mix_w, m_w_in, m_gate_bias, m_shift_mu, m_w0, m_w_decay_up, m_a0, m_w_iclr_up, m_w_gate_up, m_k_k, m_k_a, m_r_k, m_lnx_w, m_lnx_b, m_w_out_a, m_conv_w, m_w_out_b, m_w_out, m_norm_mlp_w, m_w_mlp_up, m_w_mlp_down, m_norm_final_w, v_norm_mix_w, v_w_in, v_gate_bias, v_shift_mu, v_w0, v_w_decay_up, v_a0, v_w_iclr_up, v_w_gate_up, v_k_k, v_k_a, v_r_k, v_lnx_w, v_lnx_b, v_w_out_a, v_conv_w, v_w_out_b, v_w_out, v_norm_mlp_w, v_w_mlp_up, v_w_mlp_down, v_norm_final_w):
    w = dict(zip(_ORDER, (norm_mix_w, w_in, gate_bias, shift_mu, w0, w_decay_up, a0, w_iclr_up, w_gate_up, k_k, k_a,
                          r_k, lnx_w, lnx_b, w_out_a, conv_w, w_out_b, w_out, norm_mlp_w, w_mlp_up, w_mlp_down,
                          norm_final_w)))
    m = dict(zip(_ORDER, (m_norm_mix_w, m_w_in, m_gate_bias, m_shift_mu, m_w0, m_w_decay_up, m_a0, m_w_iclr_up,
                          m_w_gate_up, m_k_k, m_k_a, m_r_k, m_lnx_w, m_lnx_b, m_w_out_a, m_conv_w, m_w_out_b,
                          m_w_out, m_norm_mlp_w, m_w_mlp_up, m_w_mlp_down, m_norm_final_w)))
    v = dict(zip(_ORDER, (v_norm_mix_w, v_w_in, v_gate_bias, v_shift_mu, v_w0, v_w_decay_up, v_a0, v_w_iclr_up,
                          v_w_gate_up, v_k_k, v_k_a, v_r_k, v_lnx_w, v_lnx_b, v_w_out_a, v_conv_w, v_w_out_b,
                          v_w_out, v_norm_mlp_w, v_w_mlp_up, v_w_mlp_down, v_norm_final_w)))
    return _step(x, loss_target, w, m, v)
```

```python
import math

import jax
import jax.numpy as jnp
from jax import lax
from jax.experimental import pallas as pl
from jax.experimental.pallas import tpu as pltpu

F32 = jnp.float32
BF16 = jnp.bfloat16
MESH = pl.DeviceIdType.MESH

N_DEV = 8
HEAD = 64
LANES = 128
SUBLANES = 8
CHUNK = 64
RMS_EPS = 1e-5
LNX_EPS = 64e-5
L2_EPS = 1e-12
ADAM_LR = 0.001
ADAM_B1 = 0.9
ADAM_B2 = 0.999
ADAM_EPS = 1e-08
ADAM_WD = 0.01
ADAM_STEP = 10
VMEM_LIMIT = 48 * 1024 * 1024


def _cparams(sem):
    return pltpu.CompilerParams(dimension_semantics=sem, vmem_limit_bytes=VMEM_LIMIT)


def _tile(dim, cands):
    for c in cands:
        if c <= dim and dim % c == 0:
            return c
    return dim


def _my_pos():
    return lax.axis_index("x"), lax.axis_index("y"), lax.axis_index("c")


def _peer(pos, r):
    x, y, c = pos
    return (1 - x if r & 4 else x, 1 - y if r & 2 else y, 1 - c if r & 1 else c)


def _slot(pos):
    return 4 * pos[0] + 2 * pos[1] + pos[2]


class _Comm:
    def __init__(self, kind, arrs):
        self.kind, self.arrs, self.n = kind, list(arrs), len(arrs)
        if kind == "gather":
            self.out_shape = [jax.ShapeDtypeStruct((N_DEV,) + a.shape, a.dtype) for a in arrs]
        else:
            self.out_shape = [jax.ShapeDtypeStruct((N_DEV - 1,) + a.shape[1:], a.dtype) for a in arrs]
        self.scratch = [pltpu.SemaphoreType.DMA((7 * self.n,)), pltpu.SemaphoreType.DMA((7 * self.n,))]
        if kind == "gather":
            self.scratch.append(pltpu.SemaphoreType.DMA((self.n,)))

    def _exchange_copies(self, in_refs, out_refs, sems):
        me = _my_pos()
        cps = []
        for ai in range(self.n):
            for r in range(1, N_DEV):
                p = _peer(me, r)
                cps.append(pltpu.make_async_remote_copy(
                    src_ref=in_refs[ai].at[_slot(p)], dst_ref=out_refs[ai].at[r - 1],
                    send_sem=sems[0].at[ai * 7 + r - 1], recv_sem=sems[1].at[ai * 7 + r - 1],
                    device_id=p, device_id_type=MESH))
        return cps

    def _gather_parts(self, in_refs, out_refs, sems):
        x, y, c = _my_pos()
        me, sibling = (x, y, c), (x, y, 1 - c)
        chips = [(1 - x, y), (x, 1 - y), (1 - x, 1 - y)]

        def copy(ai, k, block, to, src=None):
            dst = out_refs[ai].at[_slot(block)]
            return pltpu.make_async_remote_copy(
                src_ref=dst if src is None else src, dst_ref=dst, send_sem=sems[0].at[ai * 7 + k],
                recv_sem=sems[1].at[ai * 7 + k], device_id=to, device_id_type=MESH)

        mine = [pltpu.make_async_copy(in_refs[ai], out_refs[ai].at[_slot(me)], sems[2].at[ai])
                for ai in range(self.n)]
        first = []
        for ai in range(self.n):
            first.append(copy(ai, 0, me, sibling, src=in_refs[ai]))
            first += [copy(ai, 1 + j, me, (*chip, c), src=in_refs[ai]) for j, chip in enumerate(chips)]
        return me, sibling, chips, c, copy, mine, first

    def start(self, in_refs, out_refs, sems):
        if self.kind == "exchange":
            for cp in self._exchange_copies(in_refs, out_refs, sems):
                cp.start()
            return
        _, _, _, _, _, mine, first = self._gather_parts(in_refs, out_refs, sems)
        for cp in mine + first:
            cp.start()

    def wait(self, in_refs, out_refs, sems):
        if self.kind == "exchange":
            for cp in self._exchange_copies(in_refs, out_refs, sems):
                cp.wait()
            return
        me, sibling, chips, c, copy, mine, first = self._gather_parts(in_refs, out_refs, sems)
        passed = []
        for ai in range(self.n):
            for j, chip in enumerate(chips):
                copy(ai, 1 + j, (*chip, c), me).wait_recv()
                fwd = copy(ai, 4 + j, (*chip, c), sibling)
                fwd.start()
                passed.append(fwd)
        for ai in range(self.n):
            copy(ai, 0, sibling, me).wait_recv()
            for j, chip in enumerate(chips):
                copy(ai, 4 + j, (*chip, 1 - c), me).wait_recv()
        for cp in first + passed:
            cp.wait_send()
        for cp in mine:
            cp.wait()


def _hosted_call(body, name, comm, first, last, *, args, in_specs, out_shape, out_specs, scratch, grid, sem):
    if comm is None:
        return pl.pallas_call(body, name=name, out_shape=out_shape, grid=grid, in_specs=in_specs, out_specs=out_specs,
                              scratch_shapes=scratch, compiler_params=_cparams(sem))(*args)
    ni, no, ns, nc = len(args), len(out_shape), len(scratch), comm.n
    hbm = pl.BlockSpec(memory_space=pl.ANY)

    def hosted(*refs):
        ins, cin = refs[:ni], refs[ni:ni + nc]
        outs, cout = refs[ni + nc:ni + nc + no], refs[ni + nc + no:ni + 2 * nc + no]
        scr, sems = refs[ni + 2 * nc + no:ni + 2 * nc + no + ns], refs[ni + 2 * nc + no + ns:]

        @pl.when(first())
        def _():
            comm.start(cin, cout, sems)

        body(*ins, *outs, *scr)

        @pl.when(last())
        def _():
            comm.wait(cin, cout, sems)

    res = pl.pallas_call(
        hosted, name=name, out_shape=list(out_shape) + comm.out_shape, grid=grid,
        in_specs=list(in_specs) + [hbm] * nc, out_specs=list(out_specs) + [hbm] * nc,
        scratch_shapes=list(scratch) + comm.scratch,
        compiler_params=_cparams(("arbitrary",) * len(grid)))(*args, *comm.arrs)
    return res[:no], res[no:]


_DIMS = {"nn": ((1,), (0,)), "nt": ((1,), (1,)), "tn": ((0,), (0,))}


def _matmul(name, a, b, mode, out_dtypes, epi=None, extras=(), comm=None):
    if mode == "nn":
        (m, k), n = a.shape, b.shape[1]
    elif mode == "nt":
        (m, k), n = a.shape, b.shape[0]
    else:
        (k, m), n = a.shape, b.shape[1]
    tn = _tile(n, (1024, 512, 256, 128))
    tm = _tile(m, (512, 256, 128, 64, 32, 16, 8) if tn >= 1024 else (1024, 512, 256, 128, 64, 32, 16, 8))
    tk = _tile(k, (2048, 1024, 512, 256, 128))
    nk = k // tk
    gm, gn = m // tm, n // tn
    a_spec = pl.BlockSpec((tk, tm), lambda i, j, q: (q, i)) if mode == "tn" else pl.BlockSpec((tm, tk), lambda i, j, q: (i, q))
    b_spec = pl.BlockSpec((tn, tk), lambda i, j, q: (j, q)) if mode == "nt" else pl.BlockSpec((tk, tn), lambda i, j, q: (q, j))
    mn_spec = pl.BlockSpec((tm, tn), lambda i, j, q: (i, j))
    ne, no = len(extras), len(out_dtypes)
    dims = (_DIMS[mode], ((), ()))

    def finish(r, extra_refs, out_refs):
        outs = (r,) if epi is None else epi(r, *[e[...] for e in extra_refs])
        for o_ref, o in zip(out_refs, outs):
            o_ref[...] = o.astype(o_ref.dtype)

    def body(a_ref, b_ref, *rest):
        extra_refs, out_refs = rest[:ne], rest[ne:ne + no]
        part = lax.dot_general(a_ref[...], b_ref[...], dims, preferred_element_type=F32)
        if nk == 1:
            finish(part, extra_refs, out_refs)
            return
        acc = rest[ne + no]
        q = pl.program_id(2)

        @pl.when(q == 0)
        def _():
            acc[...] = part

        @pl.when(jnp.logical_and(q > 0, q < nk - 1))
        def _():
            acc[...] += part

        @pl.when(q == nk - 1)
        def _():
            finish(acc[...] + part, extra_refs, out_refs)

    def first():
        return jnp.logical_and(jnp.logical_and(pl.program_id(0) == 0, pl.program_id(1) == 0), pl.program_id(2) == 0)

    def last():
        return jnp.logical_and(jnp.logical_and(pl.program_id(0) == gm - 1, pl.program_id(1) == gn - 1),
                               pl.program_id(2) == nk - 1)

    return _hosted_call(
        body, name, comm, first, last,
        args=[a, b, *extras], in_specs=[a_spec, b_spec] + [mn_spec] * ne,
        out_shape=[jax.ShapeDtypeStruct((m, n), dt) for dt in out_dtypes], out_specs=[mn_spec] * no,
        scratch=[pltpu.VMEM((tm, tn), F32)] if nk > 1 else [], grid=(gm, gn, nk),
        sem=("parallel", "parallel", "arbitrary"))


@jax.custom_vjp
def _mm(a, w):
    return jnp.dot(a.astype(BF16), w.astype(BF16), preferred_element_type=F32)


def _mm_fwd(a, w):
    return _mm(a, w), (a, w)


def _mm_bwd(res, ct):
    a, w = res
    ctb = ct.astype(BF16)
    da = lax.dot_general(ctb, w.astype(BF16), (((1,), (1,)), ((), ())), preferred_element_type=F32)
    dw = lax.dot_general(a.astype(BF16), ctb, (((0,), (0,)), ((), ())), preferred_element_type=F32)
    return da, dw


_mm.defvjp(_mm_fwd, _mm_bwd)


def _split3(x):
    hi = x.astype(BF16)
    r1 = x - hi.astype(F32)
    mid = r1.astype(BF16)
    lo = (r1 - mid.astype(F32)).astype(BF16)
    return hi, mid, lo


def _head_ones(width):
    r = lax.broadcasted_iota(jnp.int32, (width, width), 0) // HEAD
    c = lax.broadcasted_iota(jnp.int32, (width, width), 1) // HEAD
    return (r == c).astype(BF16)


@jax.custom_vjp
def _segsum(x):
    ones = _head_ones(x.shape[-1])
    out = None
    for piece in _split3(x):
        t = jnp.dot(piece, ones, preferred_element_type=F32)
        out = t if out is None else out + t
    return out


_segsum.defvjp(lambda x: (_segsum(x), None), lambda _, ct: (_segsum(ct),))


def _softplus(z):
    return jnp.maximum(z, 0.0) + jnp.log(1.0 + jnp.exp(-jnp.abs(z)))


def _sigmoid(z):
    return 1.0 / (1.0 + jnp.exp(-z))


def _rms(x, w):
    ms = jnp.mean(x * x, axis=-1, keepdims=True)
    return x * lax.rsqrt(ms + RMS_EPS) * w


def _row(ref, i):
    return ref[pl.ds(i, 1), :]


def _shift_down(x, prev_ref, n, first):
    rolled = pltpu.roll(x, n, 0)
    rows = lax.broadcasted_iota(jnp.int32, x.shape, 0)
    for q in range(n):
        halo = jnp.where(first, 0.0, _row(prev_ref, SUBLANES - n + q))
        rolled = jnp.where(rows == q, halo, rolled)
    return rolled


def _shift_up(x, next_ref, n, last):
    t = x.shape[0]
    rolled = pltpu.roll(x, t - n, 0)
    rows = lax.broadcasted_iota(jnp.int32, x.shape, 0)
    for q in range(n):
        halo = jnp.where(last, 0.0, _row(next_ref, q))
        rolled = jnp.where(rows == t - n + q, halo, rolled)
    return rolled


def _acc_out(ref, val, first):
    @pl.when(first)
    def _():
        ref[...] = val

    @pl.when(jnp.logical_not(first))
    def _():
        ref[...] += val


def _prep_fn(k, plm, w0, a0, kkw, kaw, wd, wi, wg):
    w_log = -_softplus(-(w0 + _mm(jnp.tanh(plm), wd))) - 0.5
    lw = -jnp.exp(w_log)
    a_g = _sigmoid(a0 + _mm(plm, wi))
    g = _mm(_sigmoid(plm), wg)
    kk = k * kkw
    kk = kk / jnp.maximum(jnp.sqrt(_segsum(kk * kk)), L2_EPS)
    k2 = k * (1.0 + (a_g - 1.0) * kaw)
    return lw, k2, -kk, kk * a_g, g


def _post_fn(y, r, k2, v, g, lnw, lnb, rk):
    mu = _segsum(y) * (1.0 / HEAD)
    yc = y - mu
    var = _segsum(yc * yc) * (1.0 / HEAD)
    yn = yc * lax.rsqrt(var + LNX_EPS) * lnw + lnb
    bonus = _segsum(r * k2 * rk) * v
    return (yn + bonus) * g


def _merge_fn(pga, pgb, ba, bb, ya, yb):
    return _sigmoid(pga + ba) * ya + _sigmoid(pgb + bb) * yb


_NN, _NT, _TN = ((2,), (1,)), ((2,), (2,)), ((1,), (1,))


def _dot3(a, b, dims):
    ah = a.astype(BF16)
    al = (a - ah.astype(F32)).astype(BF16)
    bh = b.astype(BF16)
    bl = (b - bh.astype(F32)).astype(BF16)
    dg = lambda p, q: lax.dot_general(p, q, (dims, ((0,), (0,))), preferred_element_type=F32)
    return dg(ah, bh) + (dg(ah, bl) + dg(al, bh))


@jax.custom_vjp
def _dnn(a, b):
    return _dot3(a, b, _NN)


@jax.custom_vjp
def _dnt(a, b):
    return _dot3(a, b, _NT)


@jax.custom_vjp
def _dtn(a, b):
    return _dot3(a, b, _TN)


_dnn.defvjp(lambda a, b: (_dnn(a, b), (a, b)), lambda res, ct: (_dnt(ct, res[1]), _dtn(res[0], ct)))
_dnt.defvjp(lambda a, b: (_dnt(a, b), (a, b)), lambda res, ct: (_dnn(ct, res[1]), _dtn(ct, res[0])))
_dtn.defvjp(lambda a, b: (_dtn(a, b), (a, b)), lambda res, ct: (_dnt(res[1], ct), _dnn(res[0], ct)))


def _chunk_fn(s, r, lw, k, v, a, b):
    np_, c = r.shape[0], r.shape[1]
    c2 = 2 * c
    ri = lax.broadcasted_iota(jnp.int32, (np_, c, c), 1)
    ci = lax.broadcasted_iota(jnp.int32, (np_, c, c), 2)
    tri = (ri >= ci).astype(F32)
    cum = _dnn(tri, lw)
    tot = jnp.sum(lw, axis=1, keepdims=True)
    g_in, g_inv, g_out = jnp.exp(cum), jnp.exp(-cum), jnp.exp(tot - cum)
    lane_head = lax.broadcasted_iota(jnp.int32, (1, 2, 1, LANES), 3) // HEAD
    which = lax.broadcasted_iota(jnp.int32, (1, 2, 1, LANES), 1)
    hmask = (lane_head == which).astype(F32)

    def st(x):
        return (x[:, None] * hmask).reshape(np_, c2, LANES)

    r2, a2 = st(r * g_in), st(a * jnp.exp(cum - lw))
    b2, k2, v2 = st(b * g_inv), st(k * g_inv), st(v)
    bo2, ko2 = st(b * g_out), st(k * g_out)
    r2i = lax.broadcasted_iota(jnp.int32, (np_, c2, c2), 1)
    c2i = lax.broadcasted_iota(jnp.int32, (np_, c2, c2), 2)
    same = (r2i >= c) == (c2i >= c)
    strict = jnp.logical_and(same, r2i > c2i)
    incl = jnp.logical_and(same, r2i >= c2i)
    lab = jnp.where(strict, _dnt(a2, b2), 0.0)
    lak = jnp.where(strict, _dnt(a2, k2), 0.0)
    mrb = jnp.where(incl, _dnt(r2, b2), 0.0)
    mrk = jnp.where(incl, _dnt(r2, k2), 0.0)
    x2 = _dnt(a2, s) + _dnn(lak, v2)
    eye = (r2i == c2i).astype(F32)
    tinv = eye + lab
    pw = lab
    for _ in range(int(math.log2(c)) - 1):
        pw = _dnn(pw, pw)
        tinv = tinv + _dnn(tinv, pw)
    u2 = _dnn(tinv, x2)
    y2 = _dnt(r2, s) + _dnn(mrb, u2) + _dnn(mrk, v2)
    y = jnp.sum(y2.reshape(np_, 2, c, LANES), axis=1)
    s_new = s * jnp.exp(tot) + _dtn(u2, bo2) + _dtn(v2, ko2)
    return y, s_new


def _norm_fwd(name, x, add, w, want_sum):
    t, d = x.shape
    tt = _tile(t, (128, 64, 32, 16, 8))
    row = pl.BlockSpec((tt, d), lambda i: (i, 0))
    par = pl.BlockSpec((1, d), lambda i: (0, 0))
    has_add = add is not None

    def body(*refs):
        x_ref = refs[0]
        add_ref = refs[1] if has_add else None
        w_ref = refs[1 + has_add]
        outs = refs[2 + has_add:]
        h = x_ref[...] + add_ref[...] if has_add else x_ref[...]
        if want_sum:
            outs[0][...] = h
        outs[-1][...] = _rms(h, w_ref[...]).astype(BF16)

    out_shape = ([jax.ShapeDtypeStruct((t, d), F32)] if want_sum else []) + [jax.ShapeDtypeStruct((t, d), BF16)]
    return pl.pallas_call(
        body, name=name, out_shape=out_shape, grid=(t // tt,),
        in_specs=[row] + ([row] if has_add else []) + [par],
        out_specs=[row] * len(out_shape),
        compiler_params=_cparams(("arbitrary",)),
    )(*([x] + ([add] if has_add else []) + [w]))


def _norm_bwd(name, xin, dy, dres, w):
    t, d = xin.shape
    tt = _tile(t, (128, 64, 32, 16, 8))
    row = pl.BlockSpec((tt, d), lambda i: (i, 0))
    par = pl.BlockSpec((1, d), lambda i: (0, 0))

    def body(x_ref, dy_ref, dres_ref, w_ref, dx_ref, dxb_ref, dw_ref):
        _, vjp = jax.vjp(_rms, x_ref[...], w_ref[...])
        dx, dw = vjp(dy_ref[...])
        dx = dx + dres_ref[...]
        dx_ref[...] = dx
        dxb_ref[...] = dx.astype(BF16)
        _acc_out(dw_ref, dw, pl.program_id(0) == 0)

    return pl.pallas_call(
        body, name=name,
        out_shape=[jax.ShapeDtypeStruct((t, d), F32), jax.ShapeDtypeStruct((t, d), BF16),
                   jax.ShapeDtypeStruct((1, d), F32)],
        grid=(t // tt,), in_specs=[row, row, row, par], out_specs=[row, row, par],
        compiler_params=_cparams(("arbitrary",)),
    )(xin, dy, dres, w)


def _final(name, h1, md, target, w):
    t, d = h1.shape
    tt = _tile(t, (128, 64, 32, 16, 8))
    row = pl.BlockSpec((tt, d), lambda i: (i, 0))
    par = pl.BlockSpec((1, d), lambda i: (0, 0))
    one = pl.BlockSpec((1, LANES), lambda i: (0, 0))

    def body(h1_ref, md_ref, tg_ref, w_ref, loss_ref, dh_ref, dhb_ref, dw_ref):
        tg = tg_ref[...]

        def f(h, wv):
            err = _rms(h, wv) - tg
            return 0.5 * jnp.sum(jnp.mean(err * err, axis=-1, keepdims=True), axis=0, keepdims=True)

        loss, vjp = jax.vjp(f, h1_ref[...] + md_ref[...], w_ref[...])
        dh, dw = vjp(jnp.ones((1, 1), F32))
        dh_ref[...] = dh
        dhb_ref[...] = dh.astype(BF16)
        first = pl.program_id(0) == 0
        _acc_out(dw_ref, dw, first)
        _acc_out(loss_ref, jnp.broadcast_to(loss, (1, LANES)), first)

    return pl.pallas_call(
        body, name=name,
        out_shape=[jax.ShapeDtypeStruct((1, LANES), F32), jax.ShapeDtypeStruct((t, d), F32),
                   jax.ShapeDtypeStruct((t, d), BF16), jax.ShapeDtypeStruct((1, d), F32)],
        grid=(t // tt,), in_specs=[row, row, row, par], out_specs=[one, row, row, par],
        compiler_params=_cparams(("arbitrary",)),
    )(h1, md, target, w)


def _halo_specs(tt, cb, nrow8, col_of):
    prev = pl.BlockSpec((SUBLANES, cb), lambda i, j: (jnp.maximum(i * (tt // SUBLANES) - 1, 0), col_of(j)))
    nxt = pl.BlockSpec((SUBLANES, cb), lambda i, j: (jnp.minimum((i + 1) * (tt // SUBLANES), nrow8 - 1), col_of(j)))
    return prev, nxt


def _mix_fwd(name, p_all, mu, width, cb):
    t = p_all.shape[0]
    tt = _tile(t, (256, 128, 64, 32, 16, 8))
    main = pl.BlockSpec((tt, cb), lambda i, j: (i, j))
    prev, _ = _halo_specs(tt, cb, t // SUBLANES, lambda j: j)
    par = pl.BlockSpec((1, cb), lambda i, j: (0, j))

    def body(p_ref, prev_ref, mu_ref, o_ref):
        p = p_ref[...]
        o_ref[...] = p + (_shift_down(p, prev_ref, 1, pl.program_id(0) == 0) - p) * mu_ref[...]

    return pl.pallas_call(
        body, name=name, out_shape=jax.ShapeDtypeStruct((t, width), F32),
        grid=(t // tt, width // cb), in_specs=[main, prev, par], out_specs=main,
        compiler_params=_cparams(("arbitrary", "arbitrary")),
    )(p_all, p_all, mu)


def _mix_bwd(name, dpm_list, p_all, col0, mu, cb):
    t, width = dpm_list[0].shape
    tt = _tile(t, (256, 128, 64, 32, 16, 8))
    n8 = t // SUBLANES
    nl = len(dpm_list)
    main = pl.BlockSpec((tt, cb), lambda j, i: (i, j))
    nxt = pl.BlockSpec((SUBLANES, cb), lambda j, i: (jnp.minimum((i + 1) * (tt // SUBLANES), n8 - 1), j))
    p_main = pl.BlockSpec((tt, cb), lambda j, i: (i, col0 + j))
    p_prev = pl.BlockSpec((SUBLANES, cb), lambda j, i: (jnp.maximum(i * (tt // SUBLANES) - 1, 0), col0 + j))
    par = pl.BlockSpec((1, cb), lambda j, i: (0, j))
    nt_ = t // tt

    def body(*refs):
        d_refs, dn_refs = refs[:nl], refs[nl:2 * nl]
        p_ref, pp_ref, mu_ref, dp_ref, dmu_ref, nx_scr = refs[2 * nl:]
        i = pl.program_id(1)
        dpm = d_refs[0][...]
        nx = dn_refs[0][...]
        for q in range(1, nl):
            dpm = dpm + d_refs[q][...]
            nx = nx + dn_refs[q][...]
        nx_scr[...] = nx
        mu_v = mu_ref[...]
        up = _shift_up(dpm, nx_scr, 1, i == nt_ - 1)
        dp_ref[...] = (dpm * (1.0 - mu_v) + up * mu_v).astype(BF16)
        p = p_ref[...]
        diff = _shift_down(p, pp_ref, 1, i == 0) - p
        _acc_out(dmu_ref, jnp.sum(dpm * diff, axis=0, keepdims=True), i == 0)

    return pl.pallas_call(
        body, name=name,
        out_shape=[jax.ShapeDtypeStruct((t, width), BF16), jax.ShapeDtypeStruct((1, width), F32)],
        grid=(width // cb, nt_),
        in_specs=[main] * nl + [nxt] * nl + [p_main, p_prev, par],
        out_specs=[main, par],
        scratch_shapes=[pltpu.VMEM((SUBLANES, cb), F32)],
        compiler_params=_cparams(("arbitrary", "arbitrary")),
    )(*dpm_list, *dpm_list, p_all, p_all, mu)


def _prep_fwd(name, pm, cfg, w0, a0, kkw, kaw, wd, wi, wg):
    t = pm.shape[0]
    dr, lp, cb = cfg["dr"], cfg["lp"], cfg["cb"]
    tt = _tile(t, (256, 128, 64, 32, 16, 8))
    nj = dr // cb
    kspec = pl.BlockSpec((tt, cb), lambda j, i: (i, nj + j))
    lspec = pl.BlockSpec((tt, lp), lambda j, i: (i, 3 * dr // lp))
    par = pl.BlockSpec((1, cb), lambda j, i: (0, j))
    wspec = pl.BlockSpec((lp, cb), lambda j, i: (0, j))
    out = pl.BlockSpec((tt, cb), lambda j, i: (i, j))

    def body(k_ref, l_ref, w0_ref, a0_ref, kk_ref, ka_ref, wd_ref, wi_ref, wg_ref, *outs):
        vals = _prep_fn(k_ref[...], l_ref[...], w0_ref[...], a0_ref[...], kk_ref[...], ka_ref[...],
                        wd_ref[...], wi_ref[...], wg_ref[...])
        for o_ref, val in zip(outs, vals):
            o_ref[...] = val

    return pl.pallas_call(
        body, name=name, out_shape=[jax.ShapeDtypeStruct((t, dr), F32)] * 5,
        grid=(nj, t // tt), in_specs=[kspec, lspec, par, par, par, par, wspec, wspec, wspec],
        out_specs=[out] * 5, compiler_params=_cparams(("arbitrary", "arbitrary")),
    )(pm, pm, w0, a0, kkw, kaw, wd, wi, wg)


def _prep_bwd(name, pm, cfg, w0, a0, kkw, kaw, wd, wi, wg, cts, dr_parts, dv_parts):
    t = pm.shape[0]
    dr, lp, cb = cfg["dr"], cfg["lp"], cfg["cb"]
    tt = _tile(t, (256, 128, 64, 32, 16, 8))
    nj = dr // cb
    kspec = pl.BlockSpec((tt, cb), lambda j, i: (i, nj + j))
    lspec = pl.BlockSpec((tt, lp), lambda j, i: (i, 3 * dr // lp))
    par = pl.BlockSpec((1, cb), lambda j, i: (0, j))
    wspec = pl.BlockSpec((lp, cb), lambda j, i: (0, j))
    blk = pl.BlockSpec((tt, cb), lambda j, i: (i, j))
    dpl_spec = pl.BlockSpec((None, tt, lp), lambda j, i: (j, i, 0))

    def body(k_ref, l_ref, w0_ref, a0_ref, kk_ref, ka_ref, wd_ref, wi_ref, wg_ref,
             dlw_ref, dk2a_ref, dk2b_ref, da_ref, db_ref, dg_ref, dr0_ref, dr1_ref, dv0_ref, dv1_ref,
             dpr_ref, dpk_ref, dpv_ref, dpl_ref, dw0_ref, da0_ref, dkk_ref, dka_ref, dwd_ref, dwi_ref, dwg_ref):
        _, vjp = jax.vjp(_prep_fn, k_ref[...], l_ref[...], w0_ref[...], a0_ref[...], kk_ref[...], ka_ref[...],
                         wd_ref[...], wi_ref[...], wg_ref[...])
        dk, dpl, dw0, da0, dkk, dka, dwd, dwi, dwg = vjp(
            (dlw_ref[...], dk2a_ref[...] + dk2b_ref[...], da_ref[...], db_ref[...], dg_ref[...]))
        dpr_ref[...] = dr0_ref[...] + dr1_ref[...]
        dpv_ref[...] = dv0_ref[...] + dv1_ref[...]
        dpk_ref[...] = dk
        dpl_ref[...] = dpl
        first = pl.program_id(1) == 0
        for ref, val in ((dw0_ref, dw0), (da0_ref, da0), (dkk_ref, dkk), (dka_ref, dka),
                         (dwd_ref, dwd), (dwi_ref, dwi), (dwg_ref, dwg)):
            _acc_out(ref, val, first)

    out_shape = ([jax.ShapeDtypeStruct((t, dr), F32)] * 3 + [jax.ShapeDtypeStruct((nj, t, lp), F32)]
                 + [jax.ShapeDtypeStruct((1, dr), F32)] * 4 + [jax.ShapeDtypeStruct((lp, dr), F32)] * 3)
    return pl.pallas_call(
        body, name=name, out_shape=out_shape, grid=(nj, t // tt),
        in_specs=[kspec, lspec, par, par, par, par, wspec, wspec, wspec] + [blk] * 10,
        out_specs=[blk] * 3 + [dpl_spec] + [par] * 4 + [wspec] * 3,
        compiler_params=_cparams(("arbitrary", "arbitrary")),
    )(pm, pm, w0, a0, kkw, kaw, wd, wi, wg, *cts, *dr_parts, *dv_parts)


def _post_specs(t, cfg):
    dr, cb = cfg["dr"], cfg["cb"]
    tt = _tile(t, (256, 128, 64, 32, 16, 8))
    nj = dr // cb
    blk = pl.BlockSpec((tt, cb), lambda j, i: (i, j))
    rspec = pl.BlockSpec((tt, cb), lambda j, i: (i, j))
    vspec = pl.BlockSpec((tt, cb), lambda j, i: (i, 2 * nj + j))
    par = pl.BlockSpec((1, cb), lambda j, i: (0, j))
    return tt, nj, blk, rspec, vspec, par


def _post_fwd(name, y, pm, k2, g, lnw, lnb, rk, cfg):
    t = y.shape[0]
    tt, nj, blk, rspec, vspec, par = _post_specs(t, cfg)

    def body(y_ref, r_ref, k_ref, v_ref, g_ref, lw_ref, lb_ref, rk_ref, o_ref):
        o_ref[...] = _post_fn(y_ref[...], r_ref[...], k_ref[...], v_ref[...], g_ref[...],
                              lw_ref[...], lb_ref[...], rk_ref[...]).astype(BF16)

    return pl.pallas_call(
        body, name=name, out_shape=jax.ShapeDtypeStruct((t, cfg["dr"]), BF16), grid=(nj, t // tt),
        in_specs=[blk, rspec, blk, vspec, blk, par, par, par], out_specs=blk,
        compiler_params=_cparams(("arbitrary", "arbitrary")),
    )(y, pm, k2, pm, g, lnw, lnb, rk)


def _post_bwd(name, y, pm, k2, g, lnw, lnb, rk, dout, cfg):
    t = y.shape[0]
    tt, nj, blk, rspec, vspec, par = _post_specs(t, cfg)

    def body(y_ref, r_ref, k_ref, v_ref, g_ref, lw_ref, lb_ref, rk_ref, do_ref,
             dy_ref, dr_ref, dk_ref, dv_ref, dg_ref, dlw_ref, dlb_ref, drk_ref):
        _, vjp = jax.vjp(_post_fn, y_ref[...], r_ref[...], k_ref[...], v_ref[...], g_ref[...],
                         lw_ref[...], lb_ref[...], rk_ref[...])
        dy, dr, dk, dv, dg, dlw, dlb, drk = vjp(do_ref[...])
        for ref, val in ((dy_ref, dy), (dr_ref, dr), (dk_ref, dk), (dv_ref, dv), (dg_ref, dg)):
            ref[...] = val
        first = pl.program_id(1) == 0
        for ref, val in ((dlw_ref, dlw), (dlb_ref, dlb), (drk_ref, drk)):
            _acc_out(ref, val, first)

    dr = cfg["dr"]
    return pl.pallas_call(
        body, name=name,
        out_shape=[jax.ShapeDtypeStruct((t, dr), F32)] * 5 + [jax.ShapeDtypeStruct((1, dr), F32)] * 3,
        grid=(nj, t // tt),
        in_specs=[blk, rspec, blk, vspec, blk, par, par, par, blk],
        out_specs=[blk] * 5 + [par] * 3,
        compiler_params=_cparams(("arbitrary", "arbitrary")),
    )(y, pm, k2, pm, g, lnw, lnb, rk, dout)


def _conv_specs(t, cfg):
    dc, cb = cfg["dc"], cfg["cb"]
    tt = _tile(t, (256, 128, 64, 32, 16, 8))
    nj = dc // cb
    c0 = cfg["off_conv"] // cb
    n8 = t // SUBLANES

    def sect(s):
        col = lambda j: c0 + s * nj + j
        main = pl.BlockSpec((tt, cb), lambda j, i: (i, col(j)))
        prev = pl.BlockSpec((SUBLANES, cb), lambda j, i: (jnp.maximum(i * (tt // SUBLANES) - 1, 0), col(j)))
        nxt = pl.BlockSpec((SUBLANES, cb), lambda j, i: (jnp.minimum((i + 1) * (tt // SUBLANES), n8 - 1), col(j)))
        return main, prev, nxt

    blk = pl.BlockSpec((tt, cb), lambda j, i: (i, j))
    wspec = pl.BlockSpec((SUBLANES, cb), lambda j, i: (0, j))
    return tt, nj, n8, sect, blk, wspec


def _conv_fwd(name, p_all, cw8, cfg):
    t = p_all.shape[0]
    tt, nj, n8, sect, blk, wspec = _conv_specs(t, cfg)
    (bm, _, _), (cm, cp, _), (um, up, _) = sect(0), sect(1), sect(2)

    def body(b_ref, c_ref, cp_ref, u_ref, up_ref, w_ref, o_ref, zp_scr):
        first = pl.program_id(1) == 0
        z = c_ref[...] * u_ref[...]
        zp_scr[...] = cp_ref[...] * up_ref[...]
        o = _row(w_ref, 2) * z + _row(w_ref, 1) * _shift_down(z, zp_scr, 1, first) \
            + _row(w_ref, 0) * _shift_down(z, zp_scr, 2, first)
        o_ref[...] = (b_ref[...] * o).astype(BF16)

    return pl.pallas_call(
        body, name=name, out_shape=jax.ShapeDtypeStruct((t, cfg["dc"]), BF16), grid=(nj, t // tt),
        in_specs=[bm, cm, cp, um, up, wspec], out_specs=blk,
        scratch_shapes=[pltpu.VMEM((SUBLANES, blk.block_shape[1]), F32)],
        compiler_params=_cparams(("arbitrary", "arbitrary")),
    )(p_all, p_all, p_all, p_all, p_all, cw8)


def _conv_bwd(name, p_all, cw8, dyb, cfg):
    t = p_all.shape[0]
    tt, nj, n8, sect, blk, wspec = _conv_specs(t, cfg)
    (bm, _, bn), (cm, cp, _), (um, up, _) = sect(0), sect(1), sect(2)
    cb = blk.block_shape[1]
    dnxt = pl.BlockSpec((SUBLANES, cb), lambda j, i: (jnp.minimum((i + 1) * (tt // SUBLANES), n8 - 1), j))
    nt_ = t // tt

    def body(b_ref, bn_ref, c_ref, cp_ref, u_ref, up_ref, w_ref, d_ref, dn_ref,
             db_ref, dc_ref, du_ref, dw_ref, zp_scr, don_scr):
        i = pl.program_id(1)
        first, last = i == 0, i == nt_ - 1
        c, u, b, dy = c_ref[...], u_ref[...], b_ref[...], d_ref[...]
        z = c * u
        zp_scr[...] = cp_ref[...] * up_ref[...]
        z1 = _shift_down(z, zp_scr, 1, first)
        z2 = _shift_down(z, zp_scr, 2, first)
        w0, w1, w2 = _row(w_ref, 0), _row(w_ref, 1), _row(w_ref, 2)
        o = w2 * z + w1 * z1 + w0 * z2
        do = dy * b
        don_scr[...] = dn_ref[...] * bn_ref[...]
        dz = w2 * do + w1 * _shift_up(do, don_scr, 1, last) + w0 * _shift_up(do, don_scr, 2, last)
        db_ref[...] = (dy * o).astype(BF16)
        dc_ref[...] = (dz * u).astype(BF16)
        du_ref[...] = (dz * c).astype(BF16)
        rows = lax.broadcasted_iota(jnp.int32, (SUBLANES, cb), 0)
        s0 = jnp.sum(do * z2, axis=0, keepdims=True)
        s1 = jnp.sum(do * z1, axis=0, keepdims=True)
        s2 = jnp.sum(do * z, axis=0, keepdims=True)
        dw = jnp.where(rows == 0, s0, jnp.where(rows == 1, s1, jnp.where(rows == 2, s2, 0.0)))
        _acc_out(dw_ref, dw, first)

    dc = cfg["dc"]
    return pl.pallas_call(
        body, name=name,
        out_shape=[jax.ShapeDtypeStruct((t, dc), BF16)] * 3 + [jax.ShapeDtypeStruct((SUBLANES, dc), F32)],
        grid=(nj, nt_),
        in_specs=[bm, bn, cm, cp, um, up, wspec, blk, dnxt],
        out_specs=[blk] * 3 + [wspec],
        scratch_shapes=[pltpu.VMEM((SUBLANES, cb), F32), pltpu.VMEM((SUBLANES, cb), F32)],
        compiler_params=_cparams(("arbitrary", "arbitrary")),
    )(p_all, p_all, p_all, p_all, p_all, p_all, cw8, dyb, dyb)


def _merge_specs(t, cfg):
    d, cb = cfg["d"], cfg["cb"]
    tt = _tile(t, (256, 128, 64, 32, 16, 8))
    nj = d // cb
    g0 = cfg["off_gate"] // cb
    ga = pl.BlockSpec((tt, cb), lambda j, i: (i, g0 + j))
    gb = pl.BlockSpec((tt, cb), lambda j, i: (i, g0 + nj + j))
    ba = pl.BlockSpec((1, cb), lambda j, i: (0, j))
    bb = pl.BlockSpec((1, cb), lambda j, i: (0, nj + j))
    blk = pl.BlockSpec((tt, cb), lambda j, i: (i, j))
    return tt, nj, ga, gb, ba, bb, blk


def _merge_fwd(name, p_all, bias, ya, yb, cfg):
    t = p_all.shape[0]
    tt, nj, ga, gb, ba, bb, blk = _merge_specs(t, cfg)

    def body(ga_ref, gb_ref, ba_ref, bb_ref, ya_ref, yb_ref, o_ref):
        o_ref[...] = _merge_fn(ga_ref[...], gb_ref[...], ba_ref[...], bb_ref[...],
                               ya_ref[...], yb_ref[...]).astype(BF16)

    return pl.pallas_call(
        body, name=name, out_shape=jax.ShapeDtypeStruct((t, cfg["d"]), BF16), grid=(nj, t // tt),
        in_specs=[ga, gb, ba, bb, blk, blk], out_specs=blk,
        compiler_params=_cparams(("arbitrary", "arbitrary")),
    )(p_all, p_all, bias, bias, ya, yb)


def _merge_bwd(name, p_all, bias, ya, yb, dm, cfg):
    t = p_all.shape[0]
    tt, nj, ga, gb, ba, bb, blk = _merge_specs(t, cfg)

    def body(ga_ref, gb_ref, ba_ref, bb_ref, ya_ref, yb_ref, dm_ref,
             dga_ref, dgb_ref, dya_ref, dyb_ref, dba_ref, dbb_ref):
        _, vjp = jax.vjp(_merge_fn, ga_ref[...], gb_ref[...], ba_ref[...], bb_ref[...], ya_ref[...], yb_ref[...])
        dga, dgb, dba, dbb, dya, dyb = vjp(dm_ref[...])
        for ref, val in ((dga_ref, dga), (dgb_ref, dgb), (dya_ref, dya), (dyb_ref, dyb)):
            ref[...] = val.astype(BF16)
        first = pl.program_id(1) == 0
        _acc_out(dba_ref, dba, first)
        _acc_out(dbb_ref, dbb, first)

    d = cfg["d"]
    par = pl.BlockSpec((1, blk.block_shape[1]), lambda j, i: (0, j))
    return pl.pallas_call(
        body, name=name,
        out_shape=[jax.ShapeDtypeStruct((t, d), BF16)] * 4 + [jax.ShapeDtypeStruct((1, d), F32)] * 2,
        grid=(nj, t // tt),
        in_specs=[ga, gb, ba, bb, blk, blk, blk], out_specs=[blk] * 4 + [par] * 2,
        compiler_params=_cparams(("arbitrary", "arbitrary")),
    )(p_all, p_all, bias, bias, ya, yb, dm)


PAIRS = 8


def _pair_stack(ref, pairs):
    return jnp.stack([ref[:, p * LANES:(p + 1) * LANES] for p in range(pairs)])


def _pair_store(ref, val):
    for p in range(val.shape[0]):
        ref[:, p * LANES:(p + 1) * LANES] = val[p]


def _rec_specs(t, cfg, rev):
    dr = cfg["dr"]
    nc = t // CHUNK
    hp = dr // LANES
    pairs = _tile(hp, (PAIRS, 2, 1))
    ng = hp // pairs
    w = LANES * pairs
    ch = (lambda c: nc - 1 - c) if rev else (lambda c: c)
    slab = pl.BlockSpec((CHUNK, w), lambda h, c: (ch(c), h))
    vspec = pl.BlockSpec((CHUNK, w), lambda h, c: (ch(c), 2 * ng + h))
    sspec = pl.BlockSpec((None, pairs, LANES, LANES), lambda h, c: (ch(c), h, 0, 0))
    first = lambda: jnp.logical_and(pl.program_id(0) == 0, pl.program_id(1) == 0)
    last = lambda: jnp.logical_and(pl.program_id(0) == ng - 1, pl.program_id(1) == nc - 1)
    return nc, hp, pairs, ng, slab, vspec, sspec, first, last


def _rec_fwd(name, pm, lw, k2, a, b, cfg, comm=None):
    t = pm.shape[0]
    nc, hp, pairs, ng, slab, vspec, sspec, first, last = _rec_specs(t, cfg, False)

    def body(r_ref, lw_ref, k_ref, v_ref, a_ref, b_ref, y_ref, s_ref, s_scr):
        @pl.when(pl.program_id(1) == 0)
        def _():
            s_scr[...] = jnp.zeros_like(s_scr)

        s = s_scr[...]
        s_ref[...] = s
        y, s_new = _chunk_fn(s, *[_pair_stack(ref, pairs) for ref in (r_ref, lw_ref, k_ref, v_ref, a_ref, b_ref)])
        _pair_store(y_ref, y)
        s_scr[...] = s_new

    return _hosted_call(
        body, name, comm, first, last, args=[pm, lw, k2, pm, a, b],
        in_specs=[slab, slab, slab, vspec, slab, slab],
        out_shape=[jax.ShapeDtypeStruct((t, cfg["dr"]), F32), jax.ShapeDtypeStruct((nc, hp, LANES, LANES), F32)],
        out_specs=[slab, sspec], scratch=[pltpu.VMEM((pairs, LANES, LANES), F32)], grid=(ng, nc),
        sem=("arbitrary", "arbitrary"))


def _rec_bwd(name, pm, lw, k2, a, b, s_chk, dy, cfg, comm=None):
    t = pm.shape[0]
    nc, hp, pairs, ng, slab, vspec, sspec, first, last = _rec_specs(t, cfg, True)

    def body(r_ref, lw_ref, k_ref, v_ref, a_ref, b_ref, s_ref, dy_ref,
             dr_ref, dlw_ref, dk_ref, dv_ref, da_ref, db_ref, ds_scr):
        @pl.when(pl.program_id(1) == 0)
        def _():
            ds_scr[...] = jnp.zeros_like(ds_scr)

        _, vjp = jax.vjp(_chunk_fn, s_ref[...],
                         *[_pair_stack(ref, pairs) for ref in (r_ref, lw_ref, k_ref, v_ref, a_ref, b_ref)])
        ds, dr, dlw, dk, dv, da, db = vjp((_pair_stack(dy_ref, pairs), ds_scr[...]))
        ds_scr[...] = ds
        for ref, val in ((dr_ref, dr), (dlw_ref, dlw), (dk_ref, dk), (dv_ref, dv), (da_ref, da), (db_ref, db)):
            _pair_store(ref, val)

    return _hosted_call(
        body, name, comm, first, last, args=[pm, lw, k2, pm, a, b, s_chk, dy],
        in_specs=[slab, slab, slab, vspec, slab, slab, sspec, slab],
        out_shape=[jax.ShapeDtypeStruct((t, cfg["dr"]), F32)] * 6, out_specs=[slab] * 6,
        scratch=[pltpu.VMEM((pairs, LANES, LANES), F32)], grid=(ng, nc), sem=("arbitrary", "arbitrary"))


def _comm_call(name, comm):
    n = comm.n
    hbm = pl.BlockSpec(memory_space=pl.ANY)

    def body(*refs):
        comm.start(refs[:n], refs[n:2 * n], refs[2 * n:])
        comm.wait(refs[:n], refs[n:2 * n], refs[2 * n:])

    return pl.pallas_call(body, name=name, out_shape=comm.out_shape, in_specs=[hbm] * n, out_specs=[hbm] * n,
                          scratch_shapes=comm.scratch)(*comm.arrs)


def _all_reduce_small(name, v):
    rows = v.shape[0]
    vm = pl.BlockSpec(memory_space=pltpu.VMEM)

    def body(x_ref, out_ref, buf, send_sems, recv_sems):
        x, y, c = _my_pos()
        me, sibling = (x, y, c), (x, y, 1 - c)
        chips = [(1 - x, y), (x, 1 - y), (1 - x, 1 - y)]

        def copy(k, block, to, src=None):
            px, py, pc = block
            dst = buf.at[4 * px + 2 * py + pc]
            return pltpu.make_async_remote_copy(
                src_ref=dst if src is None else src, dst_ref=dst,
                send_sem=send_sems.at[k], recv_sem=recv_sems.at[k], device_id=to, device_id_type=MESH)

        buf[4 * x + 2 * y + c] = x_ref[...]
        first = [copy(0, me, sibling, src=x_ref)]
        first += [copy(1 + j, me, (*chip, c), src=x_ref) for j, chip in enumerate(chips)]
        for cp in first:
            cp.start()
        passed = [copy(4 + j, (*chip, c), sibling) for j, chip in enumerate(chips)]
        for j, chip in enumerate(chips):
            copy(1 + j, (*chip, c), me).wait_recv()
            passed[j].start()
        copy(0, sibling, me).wait_recv()
        for j, chip in enumerate(chips):
            copy(4 + j, (*chip, 1 - c), me).wait_recv()
        for cp in first + passed:
            cp.wait_send()
        acc = buf[0]
        for d in range(1, N_DEV):
            acc = acc + buf[d]
        out_ref[...] = acc

    return pl.pallas_call(
        body, name=name, out_shape=jax.ShapeDtypeStruct(v.shape, F32),
        in_specs=[vm], out_specs=vm,
        scratch_shapes=[pltpu.VMEM((N_DEV, rows, LANES), F32), pltpu.SemaphoreType.DMA((7,)),
                        pltpu.SemaphoreType.DMA((7,))],
    )(v)


def _adamw(name, w, m, v, g_own, g_recv=None):
    rows, cols = w.shape
    per_el = 4 * 3 + g_own.dtype.itemsize + (7 * g_recv.dtype.itemsize if g_recv is not None else 0) + 16
    rb = SUBLANES * 2
    while rb * 2 <= rows and rows % (rb * 2) == 0 and rb * 2 * cols * per_el * 2 <= VMEM_LIMIT // 2:
        rb *= 2
    if rows % rb:
        rb = rows
    blk = pl.BlockSpec((rb, cols), lambda i: (i, 0))
    rblk = pl.BlockSpec((N_DEV - 1, rb, cols), lambda i: (0, i, 0))
    has_r = g_recv is not None
    bc1 = 1.0 - ADAM_B1 ** ADAM_STEP
    bc2 = 1.0 - ADAM_B2 ** ADAM_STEP

    def body(*refs):
        w_ref, m_ref, v_ref, go_ref = refs[:4]
        gr_ref = refs[4] if has_r else None
        g_out, d_out, m_out, v_out = refs[4 + has_r:]
        g = go_ref[...].astype(F32)
        if has_r:
            for r in range(N_DEV - 1):
                g = g + gr_ref[r].astype(F32)
        mn = ADAM_B1 * m_ref[...] + (1.0 - ADAM_B1) * g
        vn = ADAM_B2 * v_ref[...] + (1.0 - ADAM_B2) * (g * g)
        m_hat = mn / bc1
        v_hat = vn / bc2
        g_out[...] = g
        d_out[...] = -ADAM_LR * (m_hat / (jnp.sqrt(v_hat) + ADAM_EPS) + ADAM_WD * w_ref[...])
        m_out[...] = mn
        v_out[...] = vn

    return pl.pallas_call(
        body, name=name, out_shape=[jax.ShapeDtypeStruct((rows, cols), F32)] * 4, grid=(rows // rb,),
        in_specs=[blk] * 4 + ([rblk] if has_r else []), out_specs=[blk] * 4,
        compiler_params=_cparams(("arbitrary",)),
    )(*([w, m, v, g_own] + ([g_recv] if has_r else [])))


def _round_up(n, q):
    return (n + q - 1) // q * q


def _cols(a8):
    return jnp.transpose(a8, (1, 0, 2)).reshape(a8.shape[1], -1)


def _col_slabs(a):
    r_, c_ = a.shape
    return jnp.transpose(a.reshape(r_, N_DEV, c_ // N_DEV), (1, 0, 2))


_MID = ("w_out_a", "w_out_b", "w_out", "w_mlp_up", "w_mlp_down")


def _local_step(x, target, wts, shards, cfg):
    dr, dc, d, lp, cb = cfg["dr"], cfg["dc"], cfg["d"], cfg["lp"], cfg["cb"]
    dff = shards["w_mlp_down"].shape[0] * N_DEV
    wmix = 3 * dr + lp
    (xn,) = _norm_fwd("norm_mix_fwd", x, None, wts["norm_mix_w"], False)
    (p_all,), (g_oa, g_ob, g_o, g_d) = _matmul(
        "mm_in", xn, wts["w_all"], "nn", [F32],
        comm=_Comm("gather", [shards["w_out_a"], shards["w_out_b"], shards["w_out"], shards["w_mlp_down"]]))
    w_out_a, w_out_b, w_out, w_down = _cols(g_oa), _cols(g_ob), g_o.reshape(d, d), g_d.reshape(dff, d)
    pm = _mix_fwd("mix_fwd", p_all, wts["mu_pad"], wmix, cb)
    prep_w = (wts["w0"], wts["a0"], wts["k_k"], wts["k_a"], wts["wd"], wts["wi"], wts["wg"])
    lw, k2, a_in, b_in, g = _prep_fwd("prep_fwd", pm, cfg, *prep_w)
    (y_raw, s_chk), (g_u,) = _rec_fwd(
        "rec_fwd", pm, lw, k2, a_in, b_in, cfg, comm=_Comm("gather", [shards["w_mlp_up"]]))
    w_up = _cols(g_u)
    post_w = (wts["lnx_w"], wts["lnx_b"], wts["r_k"])
    ya_in = _post_fwd("post_fwd", y_raw, pm, k2, g, *post_w, cfg)
    (ya,) = _matmul("mm_out_a", ya_in, w_out_a, "nn", [F32])
    yb_in = _conv_fwd("conv_fwd", p_all, wts["conv_w8"], cfg)
    (yb,) = _matmul("mm_out_b", yb_in, w_out_b, "nn", [F32])
    mg = _merge_fwd("merge_fwd", p_all, wts["gate_bias"], ya, yb, cfg)
    (mo,) = _matmul("mm_out", mg, w_out, "nn", [F32])
    h1, hn = _norm_fwd("norm_mlp_fwd", x, mo, wts["norm_mlp_w"], True)
    u, act = _matmul("mm_up", hn, w_up, "nn", [F32, BF16],
                     epi=lambda r: (r, jnp.square(jnp.maximum(r, 0.0))))
    (md,) = _matmul("mm_down", act, w_down, "nn", [F32])
    loss, dh2, dh2b, g_norm_final = _final("final", h1, md, target, wts["norm_final_w"])
    (du,) = _matmul("mm_down_dx", dh2b, w_down, "nt", [BF16],
                    epi=lambda r, uu: (r * (2.0 * jnp.maximum(uu, 0.0)),), extras=(u,))
    (g_down,) = _matmul("mm_down_dw", act, dh2b, "tn", [BF16])
    (dhn,) = _matmul("mm_up_dx", du, w_up, "nt", [F32])
    (g_up,) = _matmul("mm_up_dw", hn, du, "tn", [BF16])
    dh1, dh1b, g_norm_mlp = _norm_bwd("norm_mlp_bwd", h1, dhn, dh2, wts["norm_mlp_w"])
    (dmg,) = _matmul("mm_out_dx", dh1b, w_out, "nt", [F32])
    (g_out,) = _matmul("mm_out_dw", mg, dh1b, "tn", [BF16])
    dpga, dpgb, dya, dyb, dba, dbb = _merge_bwd("merge_bwd", p_all, wts["gate_bias"], ya, yb, dmg, cfg)
    (dya_in,) = _matmul("mm_out_a_dx", dya, w_out_a, "nt", [F32])
    (g_out_a,) = _matmul("mm_out_a_dw", ya_in, dya, "tn", [BF16])
    (dyb_in,) = _matmul("mm_out_b_dx", dyb, w_out_b, "nt", [F32])
    (g_out_b,) = _matmul("mm_out_b_dw", yb_in, dyb, "tn", [BF16])
    dpb, dpc, dpu, g_conv8 = _conv_bwd("conv_bwd", p_all, wts["conv_w8"], dyb_in, cfg)
    dy_raw, dr_post, dk_post, dv_post, dg, g_lnw, g_lnb, g_rk = _post_bwd(
        "post_bwd", y_raw, pm, k2, g, *post_w, dya_in, cfg)
    mid = dict(w_out_a=_col_slabs(g_out_a), w_out_b=_col_slabs(g_out_b), w_out=g_out.reshape(N_DEV, d // N_DEV, d),
               w_mlp_up=_col_slabs(g_up), w_mlp_down=g_down.reshape(N_DEV, dff // N_DEV, d))
    (dr_rec, dlw, dk_rec, dv_rec, da_in, db_in), (r_up, r_down) = _rec_bwd(
        "rec_bwd", pm, lw, k2, a_in, b_in, s_chk, dy_raw, cfg,
        comm=_Comm("exchange", [mid["w_mlp_up"], mid["w_mlp_down"]]))
    (dpm_r, dpm_k, dpm_v, dpl, g_w0, g_a0, g_kk, g_ka, g_wd, g_wi, g_wg) = _prep_bwd(
        "prep_bwd", pm, cfg, *prep_w, (dlw, dk_rec, dk_post, da_in, db_in, dg),
        (dr_rec, dr_post), (dv_rec, dv_post))
    mu = wts["mu_pad"]
    nb = dr // cb
    dps, dmus = [], []
    for s, dpm_s in enumerate((dpm_r, dpm_k, dpm_v)):
        dp_s, dmu_s = _mix_bwd("mix_bwd_%d" % s, [dpm_s], p_all, s * nb, mu[:, s * dr:(s + 1) * dr], cb)
        dps.append(dp_s)
        dmus.append(dmu_s)
    dp_l, dmu_l = _mix_bwd("mix_bwd_l", [dpl[j] for j in range(nb)], p_all, 3 * nb, mu[:, 3 * dr:], min(cb, lp))
    dp_all = jnp.concatenate(dps + [dp_l, dpb, dpc, dpu, dpga, dpgb], axis=1)
    (dxn,), (r_oa, r_ob, r_o) = _matmul(
        "mm_in_dx", dp_all, wts["w_all"], "nt", [F32],
        comm=_Comm("exchange", [mid["w_out_a"], mid["w_out_b"], mid["w_out"]]))
    mid_recv = dict(w_out_a=r_oa, w_out_b=r_ob, w_out=r_o, w_mlp_up=r_up, w_mlp_down=r_down)
    (g_all,) = _matmul("mm_in_dw", xn, dp_all, "tn", [BF16])
    grad_x, _, g_norm_mix = _norm_bwd("norm_mix_bwd", x, dxn, dh1, wts["norm_mix_w"])
    grads = dict(
        norm_mix_w=g_norm_mix, w_all=g_all, gate_bias=jnp.concatenate([dba, dbb], axis=1),
        mu_pad=jnp.concatenate(dmus + [dmu_l], axis=1), w0=g_w0, a0=g_a0, k_k=g_kk, k_a=g_ka,
        wd=g_wd, wi=g_wi, wg=g_wg, r_k=g_rk, lnx_w=g_lnw, lnx_b=g_lnb, conv_w8=g_conv8,
        norm_mlp_w=g_norm_mlp, norm_final_w=g_norm_final)
    return loss, grad_x, grads, mid, mid_recv


_SMALL = ("norm_mix_w", "gate_bias", "shift_mu", "w0", "a0", "k_k", "k_a", "r_k", "lnx_w", "lnx_b",
          "norm_mlp_w", "norm_final_w")
_ORDER = ("norm_mix_w", "w_in", "gate_bias", "shift_mu", "w0", "w_decay_up", "a0", "w_iclr_up", "w_gate_up",
          "k_k", "k_a", "r_k", "lnx_w", "lnx_b", "w_out_a", "conv_w", "w_out_b", "w_out", "norm_mlp_w",
          "w_mlp_up", "w_mlp_down", "norm_final_w")


def _step(x, target, w, m, v):
    t, d = x.shape[1], x.shape[2]
    dr = w["w0"].shape[-1]
    ld, li, lg = w["w_decay_up"].shape[1], w["w_iclr_up"].shape[1], w["w_gate_up"].shape[1]
    lora = ld + li + lg
    lp = _round_up(lora, LANES)
    dc = w["conv_w"].shape[-1] * N_DEV
    cb = math.gcd(math.gcd(lp, dr), 512)
    cfg = dict(d=d, dr=dr, dc=dc, lp=lp, cb=cb, off_conv=3 * dr + lp, off_gate=3 * dr + lp + 3 * dc)
    x2, tg2 = x[0], target[0]

    small_sh = jnp.concatenate([w["w_decay_up"][0], w["w_iclr_up"][0], w["w_gate_up"][0], w["conv_w"][0]], axis=0)
    small_rows = small_sh.shape[0]
    small_sh = jnp.pad(small_sh, ((0, _round_up(small_rows, SUBLANES) - small_rows), (0, 0)))
    big = ("w_in",) + _MID
    g_in8, gsm = _comm_call("gather_weights", _Comm("gather", [w["w_in"][0].astype(BF16), small_sh]))
    shards = {n: w[n][0].astype(BF16) for n in _MID}
    w_in = _cols(g_in8)
    zpad = jnp.zeros((d, lp - lora), BF16)
    w_all = jnp.concatenate([w_in[:, :3 * dr + lora], zpad, w_in[:, 3 * dr + lora:]], axis=1)
    sm = _cols(gsm)
    lora_full = sm[:lora]

    def lora_pad(lo, hi):
        rows = lax.broadcasted_iota(jnp.int32, (lp, 1), 0)
        full = jnp.pad(lora_full, ((0, lp - lora), (0, 0)))
        return jnp.where(jnp.logical_and(rows >= lo, rows < hi), full, 0.0)

    conv_w8 = jnp.pad(sm[lora:lora + 3], ((0, SUBLANES - 3), (0, 0)))
    mu_pad = jnp.pad(w["shift_mu"], ((0, 0), (0, lp - lora)))
    wts = dict(
        w_all=w_all, wd=lora_pad(0, ld), wi=lora_pad(ld, ld + li), wg=lora_pad(ld + li, lora), conv_w8=conv_w8,
        mu_pad=mu_pad, norm_mix_w=w["norm_mix_w"], gate_bias=w["gate_bias"], w0=w["w0"], a0=w["a0"],
        k_k=w["k_k"], k_a=w["k_a"], r_k=w["r_k"].reshape(1, dr), lnx_w=w["lnx_w"], lnx_b=w["lnx_b"],
        norm_mlp_w=w["norm_mlp_w"], norm_final_w=w["norm_final_w"].reshape(1, d))

    loss, grad_x, gr, slabs, received = _local_step(x2, tg2, wts, shards, cfg)

    g_all = gr["w_all"]
    g_in = jnp.concatenate([g_all[:, :3 * dr + lora], g_all[:, 3 * dr + lp:]], axis=1)
    g_small = jnp.concatenate([gr["wd"][:ld], gr["wi"][ld:ld + li], gr["wg"][ld + li:lora], gr["conv_w8"][:3]], axis=0)
    g_small = jnp.pad(g_small, ((0, _round_up(small_rows, SUBLANES) - small_rows), (0, 0)))
    slabs = dict(slabs, w_in=_col_slabs(g_in), small=_col_slabs(g_small))
    names = big + ("small",)
    r_in, r_small = _comm_call("exchange_grads", _Comm("exchange", [slabs["w_in"], slabs["small"]]))
    received = dict(received, w_in=r_in, small=r_small)
    x_i, y_i, c_i = _my_pos()
    me = 4 * x_i + 2 * y_i + c_i
    own = {n: lax.dynamic_index_in_dim(slabs[n], me, axis=0, keepdims=False) for n in names}

    small_g = dict(norm_mix_w=gr["norm_mix_w"], gate_bias=gr["gate_bias"], shift_mu=gr["mu_pad"][:, :3 * dr + lora],
                   w0=gr["w0"], a0=gr["a0"], k_k=gr["k_k"], k_a=gr["k_a"], r_k=gr["r_k"], lnx_w=gr["lnx_w"],
                   lnx_b=gr["lnx_b"], norm_mlp_w=gr["norm_mlp_w"], norm_final_w=gr["norm_final_w"])
    sizes = [small_g[n].size for n in _SMALL]
    total = sum(sizes) + 1
    prow = _round_up(total, LANES * SUBLANES) // LANES

    def pack(parts):
        flat = jnp.concatenate([p.reshape(-1) for p in parts])
        return jnp.pad(flat, (0, prow * LANES - flat.size)).reshape(prow, LANES)

    g_packed = _all_reduce_small("reduce_small", pack([small_g[n] for n in _SMALL] + [loss[0, :1]]))
    one = jnp.zeros((1,), F32)
    packed = [pack([d_[n] for n in _SMALL] + [one]) for d_ in (w, m, v)]
    sm_out = _adamw("adamw_small", *packed, g_packed)
    loss_out = g_packed.reshape(-1)[total - 1]

    def unpack(flat2d):
        flat = flat2d.reshape(-1)
        out, o = {}, 0
        for n, s in zip(_SMALL, sizes):
            out[n] = flat[o:o + s].reshape(w[n].shape)
            o += s
        return out

    res = [unpack(a) for a in sm_out]

    def shard2d(a):
        return a.reshape(-1, a.shape[-1])

    for n in big:
        outs = _adamw("adamw_" + n, shard2d(w[n]), shard2d(m[n]), shard2d(v[n]), shard2d(own[n]),
                      received[n].reshape((N_DEV - 1,) + shard2d(own[n]).shape))
        for r_, o in zip(res, outs):
            r_[n] = o.reshape(w[n].shape)
    sm_names = ("w_decay_up", "w_iclr_up", "w_gate_up", "conv_w")
    stack = lambda d_: jnp.pad(jnp.concatenate([d_[n][0] for n in sm_names], axis=0),
                               ((0, _round_up(small_rows, SUBLANES) - small_rows), (0, 0)))
    outs = _adamw("adamw_stack", stack(w), stack(m), stack(v), own["small"], received["small"])
    bounds = (0, ld, ld + li, lora, lora + 3)
    for r_, o in zip(res, outs):
        for q, n in enumerate(sm_names):
            r_[n] = o[bounds[q]:bounds[q + 1]].reshape(w[n].shape)

    grad, delta, new_m, new_v = res
    return (loss_out, grad_x[None], *[grad[n] for n in _ORDER], *[delta[n] for n in _ORDER],
            *[new_m[n] for n in _ORDER], *[new_v[n] for n in _ORDER])


def kernel(x, norm_mix_w, w_in, gate_bias, shift_mu, w0, w_decay_up, a0, w_iclr_up, w_gate_up, k_k, k_a, r_k, lnx_w, lnx_b, w_out_a, conv_w, w_out_b, w_out, norm_mlp_w, w_mlp_up, w_mlp_down, norm_final_w, loss_target, m_norm_mix_w, m_w_in, m_gate_bias, m_shift_mu, m_w0, m_w_decay_up, m_a0, m_w_iclr_up, m_w_gate_up, m_k_k, m_k_a, m_r_k, m_lnx_w, m_lnx_b, m_w_out_a, m_conv_w, m_w_out_b, m_w_out, m_norm_mlp_w, m_w_mlp_up, m_w_mlp_down, m_norm_final_w, v_norm_mix_w, v_w_in, v_gate_bias, v_shift_mu, v_w0, v_w_decay_up, v_a0, v_w_iclr_up, v_w_gate_up, v_k_k, v_k_a, v_r_k, v_lnx_w, v_lnx_b, v_w_out_a, v_conv_w, v_w_out_b, v_w_out, v_norm_mlp_w, v_w_mlp_up, v_w_mlp_down, v_norm_final_w):
    w = dict(zip(_ORDER, (norm_mix_w, w_in, gate_bias, shift_mu, w0, w_decay_up, a0, w_iclr_up, w_gate_up, k_k, k_a,
                          r_k, lnx_w, lnx_b, w_out_a, conv_w, w_out_b, w_out, norm_mlp_w, w_mlp_up, w_mlp_down,
                          norm_final_w)))
    m = dict(zip(_ORDER, (m_norm_mix_w, m_w_in, m_gate_bias, m_shift_mu, m_w0, m_w_decay_up, m_a0, m_w_iclr_up,
                          m_w_gate_up, m_k_k, m_k_a, m_r_k, m_lnx_w, m_lnx_b, m_w_out_a, m_conv_w, m_w_out_b,
                          m_w_out, m_norm_mlp_w, m_w_mlp_up, m_w_mlp_down, m_norm_final_w)))
    v = dict(zip(_ORDER, (v_norm_mix_w, v_w_in, v_gate_bias, v_shift_mu, v_w0, v_w_decay_up, v_a0, v_w_iclr_up,
                          v_w_gate_up, v_k_k, v_k_a, v_r_k, v_lnx_w, v_lnx_b, v_w_out_a, v_conv_w, v_w_out_b,
                          v_w_out, v_norm_mlp_w, v_w_mlp_up, v_w_mlp_down, v_norm_final_w)))
    return _step(x, loss_target, w, m, v)
```

```python
import math

import jax
import jax.numpy as jnp
from jax import lax
from jax.experimental import pallas as pl
from jax.experimental.pallas import tpu as pltpu

F32 = jnp.float32
BF16 = jnp.bfloat16
MESH = pl.DeviceIdType.MESH

N_DEV = 8
HEAD = 64
LANES = 128
SUBLANES = 8
CHUNK = 64
RMS_EPS = 1e-5
LNX_EPS = 64e-5
L2_EPS = 1e-12
ADAM_LR = 0.001
ADAM_B1 = 0.9
ADAM_B2 = 0.999
ADAM_EPS = 1e-08
ADAM_WD = 0.01
ADAM_STEP = 10
VMEM_LIMIT = 48 * 1024 * 1024
MAX_FULL_K = 4096


def _cparams(sem):
    return pltpu.CompilerParams(dimension_semantics=sem, vmem_limit_bytes=VMEM_LIMIT)


def _tile(dim, cands):
    for c in cands:
        if c <= dim and dim % c == 0:
            return c
    return dim


def _my_pos():
    return lax.axis_index("x"), lax.axis_index("y"), lax.axis_index("c")


def _peer(pos, r):
    x, y, c = pos
    return (1 - x if r & 4 else x, 1 - y if r & 2 else y, 1 - c if r & 1 else c)


def _slot(pos):
    return 4 * pos[0] + 2 * pos[1] + pos[2]


class _Comm:
    def __init__(self, kind, arrs):
        self.kind, self.arrs, self.n = kind, list(arrs), len(arrs)
        if kind == "gather":
            self.out_shape = [jax.ShapeDtypeStruct((N_DEV,) + a.shape, a.dtype) for a in arrs]
        else:
            self.out_shape = [jax.ShapeDtypeStruct((N_DEV - 1,) + a.shape[1:], a.dtype) for a in arrs]
        self.scratch = [pltpu.SemaphoreType.DMA((7 * self.n,)), pltpu.SemaphoreType.DMA((7 * self.n,))]
        if kind == "gather":
            self.scratch.append(pltpu.SemaphoreType.DMA((self.n,)))

    def _exchange_copies(self, in_refs, out_refs, sems):
        me = _my_pos()
        cps = []
        for ai in range(self.n):
            for r in range(1, N_DEV):
                p = _peer(me, r)
                cps.append(pltpu.make_async_remote_copy(
                    src_ref=in_refs[ai].at[_slot(p)], dst_ref=out_refs[ai].at[r - 1],
                    send_sem=sems[0].at[ai * 7 + r - 1], recv_sem=sems[1].at[ai * 7 + r - 1],
                    device_id=p, device_id_type=MESH))
        return cps

    def _gather_parts(self, in_refs, out_refs, sems):
        x, y, c = _my_pos()
        me, sibling = (x, y, c), (x, y, 1 - c)
        chips = [(1 - x, y), (x, 1 - y), (1 - x, 1 - y)]

        def copy(ai, k, block, to, src=None):
            dst = out_refs[ai].at[_slot(block)]
            return pltpu.make_async_remote_copy(
                src_ref=dst if src is None else src, dst_ref=dst, send_sem=sems[0].at[ai * 7 + k],
                recv_sem=sems[1].at[ai * 7 + k], device_id=to, device_id_type=MESH)

        mine = [pltpu.make_async_copy(in_refs[ai], out_refs[ai].at[_slot(me)], sems[2].at[ai])
                for ai in range(self.n)]
        first = []
        for ai in range(self.n):
            first.append(copy(ai, 0, me, sibling, src=in_refs[ai]))
            first += [copy(ai, 1 + j, me, (*chip, c), src=in_refs[ai]) for j, chip in enumerate(chips)]
        return me, sibling, chips, c, copy, mine, first

    def start(self, in_refs, out_refs, sems):
        if self.kind == "exchange":
            for cp in self._exchange_copies(in_refs, out_refs, sems):
                cp.start()
            return
        _, _, _, _, _, mine, first = self._gather_parts(in_refs, out_refs, sems)
        for cp in mine + first:
            cp.start()

    def wait(self, in_refs, out_refs, sems):
        if self.kind == "exchange":
            for cp in self._exchange_copies(in_refs, out_refs, sems):
                cp.wait()
            return
        me, sibling, chips, c, copy, mine, first = self._gather_parts(in_refs, out_refs, sems)
        passed = []
        for ai in range(self.n):
            for j, chip in enumerate(chips):
                copy(ai, 1 + j, (*chip, c), me).wait_recv()
                fwd = copy(ai, 4 + j, (*chip, c), sibling)
                fwd.start()
                passed.append(fwd)
        for ai in range(self.n):
            copy(ai, 0, sibling, me).wait_recv()
            for j, chip in enumerate(chips):
                copy(ai, 4 + j, (*chip, 1 - c), me).wait_recv()
        for cp in first + passed:
            cp.wait_send()
        for cp in mine:
            cp.wait()


def _hosted_call(body, name, comm, first, last, *, args, in_specs, out_shape, out_specs, scratch, grid, sem):
    if comm is None:
        return pl.pallas_call(body, name=name, out_shape=out_shape, grid=grid, in_specs=in_specs, out_specs=out_specs,
                              scratch_shapes=scratch, compiler_params=_cparams(sem))(*args)
    ni, no, ns, nc = len(args), len(out_shape), len(scratch), comm.n
    hbm = pl.BlockSpec(memory_space=pl.ANY)

    def hosted(*refs):
        ins, cin = refs[:ni], refs[ni:ni + nc]
        outs, cout = refs[ni + nc:ni + nc + no], refs[ni + nc + no:ni + 2 * nc + no]
        scr, sems = refs[ni + 2 * nc + no:ni + 2 * nc + no + ns], refs[ni + 2 * nc + no + ns:]

        @pl.when(first())
        def _():
            comm.start(cin, cout, sems)

        body(*ins, *outs, *scr)

        @pl.when(last())
        def _():
            comm.wait(cin, cout, sems)

    res = pl.pallas_call(
        hosted, name=name, out_shape=list(out_shape) + comm.out_shape, grid=grid,
        in_specs=list(in_specs) + [hbm] * nc, out_specs=list(out_specs) + [hbm] * nc,
        scratch_shapes=list(scratch) + comm.scratch,
        compiler_params=_cparams(("arbitrary",) * len(grid)))(*args, *comm.arrs)
    return res[:no], res[no:]


_DIMS = {"nn": ((1,), (0,)), "nt": ((1,), (1,)), "tn": ((0,), (0,))}


def _matmul(name, a, b, mode, out_dtypes, epi=None, extras=(), comm=None):
    if mode == "nn":
        (m, k), n = a.shape, b.shape[1]
    elif mode == "nt":
        (m, k), n = a.shape, b.shape[0]
    else:
        (k, m), n = a.shape, b.shape[1]
    tm = _tile(m, (1024, 512, 256, 128, 64, 32, 16, 8))
    if k <= MAX_FULL_K:
        tk, tn = k, _tile(n, (512, 256, 128))
    else:
        tk, tn = _tile(k, (1024, 512, 256, 128)), _tile(n, (1024, 512, 256, 128))
    nk = k // tk
    gm, gn = m // tm, n // tn
    a_spec = pl.BlockSpec((tk, tm), lambda i, j, q: (q, i)) if mode == "tn" else pl.BlockSpec((tm, tk), lambda i, j, q: (i, q))
    b_spec = pl.BlockSpec((tn, tk), lambda i, j, q: (j, q)) if mode == "nt" else pl.BlockSpec((tk, tn), lambda i, j, q: (q, j))
    mn_spec = pl.BlockSpec((tm, tn), lambda i, j, q: (i, j))
    ne, no = len(extras), len(out_dtypes)
    dims = (_DIMS[mode], ((), ()))
    keep_t = mode == "tn" and nk == 1 and gn > 1

    def finish(r, extra_refs, out_refs):
        outs = (r,) if epi is None else epi(r, *[e[...] for e in extra_refs])
        for o_ref, o in zip(out_refs, outs):
            o_ref[...] = o.astype(o_ref.dtype)

    def body(a_ref, b_ref, *rest):
        extra_refs, out_refs = rest[:ne], rest[ne:ne + no]
        if keep_t:
            at = rest[ne + no]

            @pl.when(pl.program_id(1) == 0)
            def _():
                at[...] = a_ref[...].T

            part = jnp.dot(at[...], b_ref[...], preferred_element_type=F32)
        else:
            part = lax.dot_general(a_ref[...], b_ref[...], dims, preferred_element_type=F32)
        if nk == 1:
            finish(part, extra_refs, out_refs)
            return
        acc = rest[ne + no]
        q = pl.program_id(2)

        @pl.when(q == 0)
        def _():
            acc[...] = part

        @pl.when(jnp.logical_and(q > 0, q < nk - 1))
        def _():
            acc[...] += part

        @pl.when(q == nk - 1)
        def _():
            finish(acc[...] + part, extra_refs, out_refs)

    def first():
        return jnp.logical_and(jnp.logical_and(pl.program_id(0) == 0, pl.program_id(1) == 0), pl.program_id(2) == 0)

    def last():
        return jnp.logical_and(jnp.logical_and(pl.program_id(0) == gm - 1, pl.program_id(1) == gn - 1),
                               pl.program_id(2) == nk - 1)

    return _hosted_call(
        body, name, comm, first, last,
        args=[a, b, *extras], in_specs=[a_spec, b_spec] + [mn_spec] * ne,
        out_shape=[jax.ShapeDtypeStruct((m, n), dt) for dt in out_dtypes], out_specs=[mn_spec] * no,
        scratch=[pltpu.VMEM((tm, tn), F32)] if nk > 1 else ([pltpu.VMEM((tm, tk), a.dtype)] if keep_t else []),
        grid=(gm, gn, nk), sem=("parallel", "arbitrary" if keep_t else "parallel", "arbitrary"))


@jax.custom_vjp
def _mm(a, w):
    return jnp.dot(a.astype(BF16), w.astype(BF16), preferred_element_type=F32)


def _mm_fwd(a, w):
    return _mm(a, w), (a, w)


def _mm_bwd(res, ct):
    a, w = res
    ctb = ct.astype(BF16)
    da = lax.dot_general(ctb, w.astype(BF16), (((1,), (1,)), ((), ())), preferred_element_type=F32)
    dw = lax.dot_general(a.astype(BF16), ctb, (((0,), (0,)), ((), ())), preferred_element_type=F32)
    return da, dw


_mm.defvjp(_mm_fwd, _mm_bwd)


def _split3(x):
    hi = x.astype(BF16)
    r1 = x - hi.astype(F32)
    mid = r1.astype(BF16)
    lo = (r1 - mid.astype(F32)).astype(BF16)
    return hi, mid, lo


def _head_ones(width):
    r = lax.broadcasted_iota(jnp.int32, (width, width), 0) // HEAD
    c = lax.broadcasted_iota(jnp.int32, (width, width), 1) // HEAD
    return (r == c).astype(BF16)


@jax.custom_vjp
def _segsum(x):
    ones = _head_ones(x.shape[-1])
    out = None
    for piece in _split3(x):
        t = jnp.dot(piece, ones, preferred_element_type=F32)
        out = t if out is None else out + t
    return out


_segsum.defvjp(lambda x: (_segsum(x), None), lambda _, ct: (_segsum(ct),))


def _softplus(z):
    return jnp.maximum(z, 0.0) + jnp.log(1.0 + jnp.exp(-jnp.abs(z)))


def _sigmoid(z):
    return 1.0 / (1.0 + jnp.exp(-z))


def _rms(x, w):
    ms = jnp.mean(x * x, axis=-1, keepdims=True)
    return x * lax.rsqrt(ms + RMS_EPS) * w


def _row(ref, i):
    return ref[pl.ds(i, 1), :]


def _shift_down(x, prev_ref, n, first):
    rolled = pltpu.roll(x, n, 0)
    rows = lax.broadcasted_iota(jnp.int32, x.shape, 0)
    for q in range(n):
        halo = jnp.where(first, 0.0, _row(prev_ref, SUBLANES - n + q))
        rolled = jnp.where(rows == q, halo, rolled)
    return rolled


def _shift_up(x, next_ref, n, last):
    t = x.shape[0]
    rolled = pltpu.roll(x, t - n, 0)
    rows = lax.broadcasted_iota(jnp.int32, x.shape, 0)
    for q in range(n):
        halo = jnp.where(last, 0.0, _row(next_ref, q))
        rolled = jnp.where(rows == t - n + q, halo, rolled)
    return rolled


def _acc_out(ref, val, first):
    @pl.when(first)
    def _():
        ref[...] = val

    @pl.when(jnp.logical_not(first))
    def _():
        ref[...] += val


def _prep_fn(k, plm, w0, a0, kkw, kaw, wd, wi, wg):
    w_log = -_softplus(-(w0 + _mm(jnp.tanh(plm), wd))) - 0.5
    lw = -jnp.exp(w_log)
    a_g = _sigmoid(a0 + _mm(plm, wi))
    g = _mm(_sigmoid(plm), wg)
    kk = k * kkw
    kk = kk / jnp.maximum(jnp.sqrt(_segsum(kk * kk)), L2_EPS)
    k2 = k * (1.0 + (a_g - 1.0) * kaw)
    return lw, k2, -kk, kk * a_g, g


def _post_fn(y, r, k2, v, g, lnw, lnb, rk):
    mu = _segsum(y) * (1.0 / HEAD)
    yc = y - mu
    var = _segsum(yc * yc) * (1.0 / HEAD)
    yn = yc * lax.rsqrt(var + LNX_EPS) * lnw + lnb
    bonus = _segsum(r * k2 * rk) * v
    return (yn + bonus) * g


def _merge_fn(pga, pgb, ba, bb, ya, yb):
    return _sigmoid(pga + ba) * ya + _sigmoid(pgb + bb) * yb


_NN, _NT, _TN = ((2,), (1,)), ((2,), (2,)), ((1,), (1,))


def _dot3(a, b, dims):
    ah = a.astype(BF16)
    al = (a - ah.astype(F32)).astype(BF16)
    bh = b.astype(BF16)
    bl = (b - bh.astype(F32)).astype(BF16)
    dg = lambda p, q: lax.dot_general(p, q, (dims, ((0,), (0,))), preferred_element_type=F32)
    return dg(ah, bh) + (dg(ah, bl) + dg(al, bh))


@jax.custom_vjp
def _dnn(a, b):
    return _dot3(a, b, _NN)


@jax.custom_vjp
def _dnt(a, b):
    return _dot3(a, b, _NT)


@jax.custom_vjp
def _dtn(a, b):
    return _dot3(a, b, _TN)


_dnn.defvjp(lambda a, b: (_dnn(a, b), (a, b)), lambda res, ct: (_dnt(ct, res[1]), _dtn(res[0], ct)))
_dnt.defvjp(lambda a, b: (_dnt(a, b), (a, b)), lambda res, ct: (_dnn(ct, res[1]), _dtn(ct, res[0])))
_dtn.defvjp(lambda a, b: (_dtn(a, b), (a, b)), lambda res, ct: (_dnt(res[1], ct), _dnn(res[0], ct)))


def _chunk_fn(s, r, lw, k, v, a, b):
    np_, c = r.shape[0], r.shape[1]
    c2 = 2 * c
    ri = lax.broadcasted_iota(jnp.int32, (np_, c, c), 1)
    ci = lax.broadcasted_iota(jnp.int32, (np_, c, c), 2)
    tri = (ri >= ci).astype(F32)
    cum = _dnn(tri, lw)
    tot = jnp.sum(lw, axis=1, keepdims=True)
    g_in, g_inv, g_out = jnp.exp(cum), jnp.exp(-cum), jnp.exp(tot - cum)
    lane_head = lax.broadcasted_iota(jnp.int32, (1, 2, 1, LANES), 3) // HEAD
    which = lax.broadcasted_iota(jnp.int32, (1, 2, 1, LANES), 1)
    hmask = (lane_head == which).astype(F32)

    def st(x):
        return (x[:, None] * hmask).reshape(np_, c2, LANES)

    r2, a2 = st(r * g_in), st(a * jnp.exp(cum - lw))
    b2, k2, v2 = st(b * g_inv), st(k * g_inv), st(v)
    bo2, ko2 = st(b * g_out), st(k * g_out)
    r2i = lax.broadcasted_iota(jnp.int32, (np_, c2, c2), 1)
    c2i = lax.broadcasted_iota(jnp.int32, (np_, c2, c2), 2)
    same = (r2i >= c) == (c2i >= c)
    strict = jnp.logical_and(same, r2i > c2i)
    incl = jnp.logical_and(same, r2i >= c2i)
    lab = jnp.where(strict, _dnt(a2, b2), 0.0)
    lak = jnp.where(strict, _dnt(a2, k2), 0.0)
    mrb = jnp.where(incl, _dnt(r2, b2), 0.0)
    mrk = jnp.where(incl, _dnt(r2, k2), 0.0)
    x2 = _dnt(a2, s) + _dnn(lak, v2)
    eye = (r2i == c2i).astype(F32)
    tinv = eye + lab
    pw = lab
    for _ in range(int(math.log2(c)) - 1):
        pw = _dnn(pw, pw)
        tinv = tinv + _dnn(tinv, pw)
    u2 = _dnn(tinv, x2)
    y2 = _dnt(r2, s) + _dnn(mrb, u2) + _dnn(mrk, v2)
    y = jnp.sum(y2.reshape(np_, 2, c, LANES), axis=1)
    s_new = s * jnp.exp(tot) + _dtn(u2, bo2) + _dtn(v2, ko2)
    return y, s_new


def _norm_fwd(name, x, add, w, want_sum):
    t, d = x.shape
    tt = _tile(t, (128, 64, 32, 16, 8))
    row = pl.BlockSpec((tt, d), lambda i: (i, 0))
    par = pl.BlockSpec((1, d), lambda i: (0, 0))
    has_add = add is not None

    def body(*refs):
        x_ref = refs[0]
        add_ref = refs[1] if has_add else None
        w_ref = refs[1 + has_add]
        outs = refs[2 + has_add:]
        h = x_ref[...] + add_ref[...] if has_add else x_ref[...]
        if want_sum:
            outs[0][...] = h
        outs[-1][...] = _rms(h, w_ref[...]).astype(BF16)

    out_shape = ([jax.ShapeDtypeStruct((t, d), F32)] if want_sum else []) + [jax.ShapeDtypeStruct((t, d), BF16)]
    return pl.pallas_call(
        body, name=name, out_shape=out_shape, grid=(t // tt,),
        in_specs=[row] + ([row] if has_add else []) + [par],
        out_specs=[row] * len(out_shape),
        compiler_params=_cparams(("arbitrary",)),
    )(*([x] + ([add] if has_add else []) + [w]))


def _norm_bwd(name, xin, dy, dres, w):
    t, d = xin.shape
    tt = _tile(t, (128, 64, 32, 16, 8))
    row = pl.BlockSpec((tt, d), lambda i: (i, 0))
    par = pl.BlockSpec((1, d), lambda i: (0, 0))

    def body(x_ref, dy_ref, dres_ref, w_ref, dx_ref, dxb_ref, dw_ref):
        _, vjp = jax.vjp(_rms, x_ref[...], w_ref[...])
        dx, dw = vjp(dy_ref[...])
        dx = dx + dres_ref[...]
        dx_ref[...] = dx
        dxb_ref[...] = dx.astype(BF16)
        _acc_out(dw_ref, dw, pl.program_id(0) == 0)

    return pl.pallas_call(
        body, name=name,
        out_shape=[jax.ShapeDtypeStruct((t, d), F32), jax.ShapeDtypeStruct((t, d), BF16),
                   jax.ShapeDtypeStruct((1, d), F32)],
        grid=(t // tt,), in_specs=[row, row, row, par], out_specs=[row, row, par],
        compiler_params=_cparams(("arbitrary",)),
    )(xin, dy, dres, w)


def _final(name, h1, md, target, w):
    t, d = h1.shape
    tt = _tile(t, (128, 64, 32, 16, 8))
    row = pl.BlockSpec((tt, d), lambda i: (i, 0))
    par = pl.BlockSpec((1, d), lambda i: (0, 0))
    one = pl.BlockSpec((1, LANES), lambda i: (0, 0))

    def body(h1_ref, md_ref, tg_ref, w_ref, loss_ref, dh_ref, dhb_ref, dw_ref):
        tg = tg_ref[...]

        def f(h, wv):
            err = _rms(h, wv) - tg
            return 0.5 * jnp.sum(jnp.mean(err * err, axis=-1, keepdims=True), axis=0, keepdims=True)

        loss, vjp = jax.vjp(f, h1_ref[...] + md_ref[...], w_ref[...])
        dh, dw = vjp(jnp.ones((1, 1), F32))
        dh_ref[...] = dh
        dhb_ref[...] = dh.astype(BF16)
        first = pl.program_id(0) == 0
        _acc_out(dw_ref, dw, first)
        _acc_out(loss_ref, jnp.broadcast_to(loss, (1, LANES)), first)

    return pl.pallas_call(
        body, name=name,
        out_shape=[jax.ShapeDtypeStruct((1, LANES), F32), jax.ShapeDtypeStruct((t, d), F32),
                   jax.ShapeDtypeStruct((t, d), BF16), jax.ShapeDtypeStruct((1, d), F32)],
        grid=(t // tt,), in_specs=[row, row, row, par], out_specs=[one, row, row, par],
        compiler_params=_cparams(("arbitrary",)),
    )(h1, md, target, w)


def _halo_specs(tt, cb, nrow8, col_of):
    prev = pl.BlockSpec((SUBLANES, cb), lambda i, j: (jnp.maximum(i * (tt // SUBLANES) - 1, 0), col_of(j)))
    nxt = pl.BlockSpec((SUBLANES, cb), lambda i, j: (jnp.minimum((i + 1) * (tt // SUBLANES), nrow8 - 1), col_of(j)))
    return prev, nxt


def _mix_fwd(name, p_all, mu, width, cb):
    t = p_all.shape[0]
    tt = _tile(t, (256, 128, 64, 32, 16, 8))
    main = pl.BlockSpec((tt, cb), lambda i, j: (i, j))
    prev, _ = _halo_specs(tt, cb, t // SUBLANES, lambda j: j)
    par = pl.BlockSpec((1, cb), lambda i, j: (0, j))

    def body(p_ref, prev_ref, mu_ref, o_ref):
        p = p_ref[...]
        o_ref[...] = p + (_shift_down(p, prev_ref, 1, pl.program_id(0) == 0) - p) * mu_ref[...]

    return pl.pallas_call(
        body, name=name, out_shape=jax.ShapeDtypeStruct((t, width), F32),
        grid=(t // tt, width // cb), in_specs=[main, prev, par], out_specs=main,
        compiler_params=_cparams(("arbitrary", "arbitrary")),
    )(p_all, p_all, mu)


def _mix_bwd(name, dpm_list, p_all, col0, mu, cb):
    t, width = dpm_list[0].shape
    tt = _tile(t, (256, 128, 64, 32, 16, 8))
    n8 = t // SUBLANES
    nl = len(dpm_list)
    main = pl.BlockSpec((tt, cb), lambda j, i: (i, j))
    nxt = pl.BlockSpec((SUBLANES, cb), lambda j, i: (jnp.minimum((i + 1) * (tt // SUBLANES), n8 - 1), j))
    p_main = pl.BlockSpec((tt, cb), lambda j, i: (i, col0 + j))
    p_prev = pl.BlockSpec((SUBLANES, cb), lambda j, i: (jnp.maximum(i * (tt // SUBLANES) - 1, 0), col0 + j))
    par = pl.BlockSpec((1, cb), lambda j, i: (0, j))
    nt_ = t // tt

    def body(*refs):
        d_refs, dn_refs = refs[:nl], refs[nl:2 * nl]
        p_ref, pp_ref, mu_ref, dp_ref, dmu_ref, nx_scr = refs[2 * nl:]
        i = pl.program_id(1)
        dpm = d_refs[0][...]
        nx = dn_refs[0][...]
        for q in range(1, nl):
            dpm = dpm + d_refs[q][...]
            nx = nx + dn_refs[q][...]
        nx_scr[...] = nx
        mu_v = mu_ref[...]
        up = _shift_up(dpm, nx_scr, 1, i == nt_ - 1)
        dp_ref[...] = (dpm * (1.0 - mu_v) + up * mu_v).astype(BF16)
        p = p_ref[...]
        diff = _shift_down(p, pp_ref, 1, i == 0) - p
        _acc_out(dmu_ref, jnp.sum(dpm * diff, axis=0, keepdims=True), i == 0)

    return pl.pallas_call(
        body, name=name,
        out_shape=[jax.ShapeDtypeStruct((t, width), BF16), jax.ShapeDtypeStruct((1, width), F32)],
        grid=(width // cb, nt_),
        in_specs=[main] * nl + [nxt] * nl + [p_main, p_prev, par],
        out_specs=[main, par],
        scratch_shapes=[pltpu.VMEM((SUBLANES, cb), F32)],
        compiler_params=_cparams(("arbitrary", "arbitrary")),
    )(*dpm_list, *dpm_list, p_all, p_all, mu)


def _prep_fwd(name, pm, cfg, w0, a0, kkw, kaw, wd, wi, wg):
    t = pm.shape[0]
    dr, lp, cb = cfg["dr"], cfg["lp"], cfg["cb"]
    tt = _tile(t, (256, 128, 64, 32, 16, 8))
    nj = dr // cb
    kspec = pl.BlockSpec((tt, cb), lambda j, i: (i, nj + j))
    lspec = pl.BlockSpec((tt, lp), lambda j, i: (i, 3 * dr // lp))
    par = pl.BlockSpec((1, cb), lambda j, i: (0, j))
    wspec = pl.BlockSpec((lp, cb), lambda j, i: (0, j))
    out = pl.BlockSpec((tt, cb), lambda j, i: (i, j))

    def body(k_ref, l_ref, w0_ref, a0_ref, kk_ref, ka_ref, wd_ref, wi_ref, wg_ref, *outs):
        vals = _prep_fn(k_ref[...], l_ref[...], w0_ref[...], a0_ref[...], kk_ref[...], ka_ref[...],
                        wd_ref[...], wi_ref[...], wg_ref[...])
        for o_ref, val in zip(outs, vals):
            o_ref[...] = val

    return pl.pallas_call(
        body, name=name, out_shape=[jax.ShapeDtypeStruct((t, dr), F32)] * 5,
        grid=(nj, t // tt), in_specs=[kspec, lspec, par, par, par, par, wspec, wspec, wspec],
        out_specs=[out] * 5, compiler_params=_cparams(("arbitrary", "arbitrary")),
    )(pm, pm, w0, a0, kkw, kaw, wd, wi, wg)


def _prep_bwd(name, pm, cfg, w0, a0, kkw, kaw, wd, wi, wg, cts, dr_parts, dv_parts):
    t = pm.shape[0]
    dr, lp, cb = cfg["dr"], cfg["lp"], cfg["cb"]
    tt = _tile(t, (256, 128, 64, 32, 16, 8))
    nj = dr // cb
    kspec = pl.BlockSpec((tt, cb), lambda j, i: (i, nj + j))
    lspec = pl.BlockSpec((tt, lp), lambda j, i: (i, 3 * dr // lp))
    par = pl.BlockSpec((1, cb), lambda j, i: (0, j))
    wspec = pl.BlockSpec((lp, cb), lambda j, i: (0, j))
    blk = pl.BlockSpec((tt, cb), lambda j, i: (i, j))
    dpl_spec = pl.BlockSpec((None, tt, lp), lambda j, i: (j, i, 0))

    def body(k_ref, l_ref, w0_ref, a0_ref, kk_ref, ka_ref, wd_ref, wi_ref, wg_ref,
             dlw_ref, dk2a_ref, dk2b_ref, da_ref, db_ref, dg_ref, dr0_ref, dr1_ref, dv0_ref, dv1_ref,
             dpr_ref, dpk_ref, dpv_ref, dpl_ref, dw0_ref, da0_ref, dkk_ref, dka_ref, dwd_ref, dwi_ref, dwg_ref):
        _, vjp = jax.vjp(_prep_fn, k_ref[...], l_ref[...], w0_ref[...], a0_ref[...], kk_ref[...], ka_ref[...],
                         wd_ref[...], wi_ref[...], wg_ref[...])
        dk, dpl, dw0, da0, dkk, dka, dwd, dwi, dwg = vjp(
            (dlw_ref[...], dk2a_ref[...] + dk2b_ref[...], da_ref[...], db_ref[...], dg_ref[...]))
        dpr_ref[...] = dr0_ref[...] + dr1_ref[...]
        dpv_ref[...] = dv0_ref[...] + dv1_ref[...]
        dpk_ref[...] = dk
        dpl_ref[...] = dpl
        first = pl.program_id(1) == 0
        for ref, val in ((dw0_ref, dw0), (da0_ref, da0), (dkk_ref, dkk), (dka_ref, dka),
                         (dwd_ref, dwd), (dwi_ref, dwi), (dwg_ref, dwg)):
            _acc_out(ref, val, first)

    out_shape = ([jax.ShapeDtypeStruct((t, dr), F32)] * 3 + [jax.ShapeDtypeStruct((nj, t, lp), F32)]
                 + [jax.ShapeDtypeStruct((1, dr), F32)] * 4 + [jax.ShapeDtypeStruct((lp, dr), F32)] * 3)
    return pl.pallas_call(
        body, name=name, out_shape=out_shape, grid=(nj, t // tt),
        in_specs=[kspec, lspec, par, par, par, par, wspec, wspec, wspec] + [blk] * 10,
        out_specs=[blk] * 3 + [dpl_spec] + [par] * 4 + [wspec] * 3,
        compiler_params=_cparams(("arbitrary", "arbitrary")),
    )(pm, pm, w0, a0, kkw, kaw, wd, wi, wg, *cts, *dr_parts, *dv_parts)


def _post_specs(t, cfg):
    dr, cb = cfg["dr"], cfg["cb"]
    tt = _tile(t, (256, 128, 64, 32, 16, 8))
    nj = dr // cb
    blk = pl.BlockSpec((tt, cb), lambda j, i: (i, j))
    rspec = pl.BlockSpec((tt, cb), lambda j, i: (i, j))
    vspec = pl.BlockSpec((tt, cb), lambda j, i: (i, 2 * nj + j))
    par = pl.BlockSpec((1, cb), lambda j, i: (0, j))
    return tt, nj, blk, rspec, vspec, par


def _post_fwd(name, y, pm, k2, g, lnw, lnb, rk, cfg):
    t = y.shape[0]
    tt, nj, blk, rspec, vspec, par = _post_specs(t, cfg)

    def body(y_ref, r_ref, k_ref, v_ref, g_ref, lw_ref, lb_ref, rk_ref, o_ref):
        o_ref[...] = _post_fn(y_ref[...], r_ref[...], k_ref[...], v_ref[...], g_ref[...],
                              lw_ref[...], lb_ref[...], rk_ref[...]).astype(BF16)

    return pl.pallas_call(
        body, name=name, out_shape=jax.ShapeDtypeStruct((t, cfg["dr"]), BF16), grid=(nj, t // tt),
        in_specs=[blk, rspec, blk, vspec, blk, par, par, par], out_specs=blk,
        compiler_params=_cparams(("arbitrary", "arbitrary")),
    )(y, pm, k2, pm, g, lnw, lnb, rk)


def _post_bwd(name, y, pm, k2, g, lnw, lnb, rk, dout, cfg):
    t = y.shape[0]
    tt, nj, blk, rspec, vspec, par = _post_specs(t, cfg)

    def body(y_ref, r_ref, k_ref, v_ref, g_ref, lw_ref, lb_ref, rk_ref, do_ref,
             dy_ref, dr_ref, dk_ref, dv_ref, dg_ref, dlw_ref, dlb_ref, drk_ref):
        _, vjp = jax.vjp(_post_fn, y_ref[...], r_ref[...], k_ref[...], v_ref[...], g_ref[...],
                         lw_ref[...], lb_ref[...], rk_ref[...])
        dy, dr, dk, dv, dg, dlw, dlb, drk = vjp(do_ref[...])
        for ref, val in ((dy_ref, dy), (dr_ref, dr), (dk_ref, dk), (dv_ref, dv), (dg_ref, dg)):
            ref[...] = val
        first = pl.program_id(1) == 0
        for ref, val in ((dlw_ref, dlw), (dlb_ref, dlb), (drk_ref, drk)):
            _acc_out(ref, val, first)

    dr = cfg["dr"]
    return pl.pallas_call(
        body, name=name,
        out_shape=[jax.ShapeDtypeStruct((t, dr), F32)] * 5 + [jax.ShapeDtypeStruct((1, dr), F32)] * 3,
        grid=(nj, t // tt),
        in_specs=[blk, rspec, blk, vspec, blk, par, par, par, blk],
        out_specs=[blk] * 5 + [par] * 3,
        compiler_params=_cparams(("arbitrary", "arbitrary")),
    )(y, pm, k2, pm, g, lnw, lnb, rk, dout)


def _conv_specs(t, cfg):
    dc, cb = cfg["dc"], cfg["cb"]
    tt = _tile(t, (256, 128, 64, 32, 16, 8))
    nj = dc // cb
    c0 = cfg["off_conv"] // cb
    n8 = t // SUBLANES

    def sect(s):
        col = lambda j: c0 + s * nj + j
        main = pl.BlockSpec((tt, cb), lambda j, i: (i, col(j)))
        prev = pl.BlockSpec((SUBLANES, cb), lambda j, i: (jnp.maximum(i * (tt // SUBLANES) - 1, 0), col(j)))
        nxt = pl.BlockSpec((SUBLANES, cb), lambda j, i: (jnp.minimum((i + 1) * (tt // SUBLANES), n8 - 1), col(j)))
        return main, prev, nxt

    blk = pl.BlockSpec((tt, cb), lambda j, i: (i, j))
    wspec = pl.BlockSpec((SUBLANES, cb), lambda j, i: (0, j))
    return tt, nj, n8, sect, blk, wspec


def _conv_fwd(name, p_all, cw8, cfg):
    t = p_all.shape[0]
    tt, nj, n8, sect, blk, wspec = _conv_specs(t, cfg)
    (bm, _, _), (cm, cp, _), (um, up, _) = sect(0), sect(1), sect(2)

    def body(b_ref, c_ref, cp_ref, u_ref, up_ref, w_ref, o_ref, zp_scr):
        first = pl.program_id(1) == 0
        z = c_ref[...] * u_ref[...]
        zp_scr[...] = cp_ref[...] * up_ref[...]
        o = _row(w_ref, 2) * z + _row(w_ref, 1) * _shift_down(z, zp_scr, 1, first) \
            + _row(w_ref, 0) * _shift_down(z, zp_scr, 2, first)
        o_ref[...] = (b_ref[...] * o).astype(BF16)

    return pl.pallas_call(
        body, name=name, out_shape=jax.ShapeDtypeStruct((t, cfg["dc"]), BF16), grid=(nj, t // tt),
        in_specs=[bm, cm, cp, um, up, wspec], out_specs=blk,
        scratch_shapes=[pltpu.VMEM((SUBLANES, blk.block_shape[1]), F32)],
        compiler_params=_cparams(("arbitrary", "arbitrary")),
    )(p_all, p_all, p_all, p_all, p_all, cw8)


def _conv_bwd(name, p_all, cw8, dyb, cfg):
    t = p_all.shape[0]
    tt, nj, n8, sect, blk, wspec = _conv_specs(t, cfg)
    (bm, _, bn), (cm, cp, _), (um, up, _) = sect(0), sect(1), sect(2)
    cb = blk.block_shape[1]
    dnxt = pl.BlockSpec((SUBLANES, cb), lambda j, i: (jnp.minimum((i + 1) * (tt // SUBLANES), n8 - 1), j))
    nt_ = t // tt

    def body(b_ref, bn_ref, c_ref, cp_ref, u_ref, up_ref, w_ref, d_ref, dn_ref,
             db_ref, dc_ref, du_ref, dw_ref, zp_scr, don_scr):
        i = pl.program_id(1)
        first, last = i == 0, i == nt_ - 1
        c, u, b, dy = c_ref[...], u_ref[...], b_ref[...], d_ref[...]
        z = c * u
        zp_scr[...] = cp_ref[...] * up_ref[...]
        z1 = _shift_down(z, zp_scr, 1, first)
        z2 = _shift_down(z, zp_scr, 2, first)
        w0, w1, w2 = _row(w_ref, 0), _row(w_ref, 1), _row(w_ref, 2)
        o = w2 * z + w1 * z1 + w0 * z2
        do = dy * b
        don_scr[...] = dn_ref[...] * bn_ref[...]
        dz = w2 * do + w1 * _shift_up(do, don_scr, 1, last) + w0 * _shift_up(do, don_scr, 2, last)
        db_ref[...] = (dy * o).astype(BF16)
        dc_ref[...] = (dz * u).astype(BF16)
        du_ref[...] = (dz * c).astype(BF16)
        rows = lax.broadcasted_iota(jnp.int32, (SUBLANES, cb), 0)
        s0 = jnp.sum(do * z2, axis=0, keepdims=True)
        s1 = jnp.sum(do * z1, axis=0, keepdims=True)
        s2 = jnp.sum(do * z, axis=0, keepdims=True)
        dw = jnp.where(rows == 0, s0, jnp.where(rows == 1, s1, jnp.where(rows == 2, s2, 0.0)))
        _acc_out(dw_ref, dw, first)

    dc = cfg["dc"]
    return pl.pallas_call(
        body, name=name,
        out_shape=[jax.ShapeDtypeStruct((t, dc), BF16)] * 3 + [jax.ShapeDtypeStruct((SUBLANES, dc), F32)],
        grid=(nj, nt_),
        in_specs=[bm, bn, cm, cp, um, up, wspec, blk, dnxt],
        out_specs=[blk] * 3 + [wspec],
        scratch_shapes=[pltpu.VMEM((SUBLANES, cb), F32), pltpu.VMEM((SUBLANES, cb), F32)],
        compiler_params=_cparams(("arbitrary", "arbitrary")),
    )(p_all, p_all, p_all, p_all, p_all, p_all, cw8, dyb, dyb)


def _merge_specs(t, cfg):
    d, cb = cfg["d"], cfg["cb"]
    tt = _tile(t, (256, 128, 64, 32, 16, 8))
    nj = d // cb
    g0 = cfg["off_gate"] // cb
    ga = pl.BlockSpec((tt, cb), lambda j, i: (i, g0 + j))
    gb = pl.BlockSpec((tt, cb), lambda j, i: (i, g0 + nj + j))
    ba = pl.BlockSpec((1, cb), lambda j, i: (0, j))
    bb = pl.BlockSpec((1, cb), lambda j, i: (0, nj + j))
    blk = pl.BlockSpec((tt, cb), lambda j, i: (i, j))
    return tt, nj, ga, gb, ba, bb, blk


def _merge_fwd(name, p_all, bias, ya, yb, cfg):
    t = p_all.shape[0]
    tt, nj, ga, gb, ba, bb, blk = _merge_specs(t, cfg)

    def body(ga_ref, gb_ref, ba_ref, bb_ref, ya_ref, yb_ref, o_ref):
        o_ref[...] = _merge_fn(ga_ref[...], gb_ref[...], ba_ref[...], bb_ref[...],
                               ya_ref[...], yb_ref[...]).astype(BF16)

    return pl.pallas_call(
        body, name=name, out_shape=jax.ShapeDtypeStruct((t, cfg["d"]), BF16), grid=(nj, t // tt),
        in_specs=[ga, gb, ba, bb, blk, blk], out_specs=blk,
        compiler_params=_cparams(("arbitrary", "arbitrary")),
    )(p_all, p_all, bias, bias, ya, yb)


def _merge_bwd(name, p_all, bias, ya, yb, dm, cfg):
    t = p_all.shape[0]
    tt, nj, ga, gb, ba, bb, blk = _merge_specs(t, cfg)

    def body(ga_ref, gb_ref, ba_ref, bb_ref, ya_ref, yb_ref, dm_ref,
             dga_ref, dgb_ref, dya_ref, dyb_ref, dba_ref, dbb_ref):
        _, vjp = jax.vjp(_merge_fn, ga_ref[...], gb_ref[...], ba_ref[...], bb_ref[...], ya_ref[...], yb_ref[...])
        dga, dgb, dba, dbb, dya, dyb = vjp(dm_ref[...])
        for ref, val in ((dga_ref, dga), (dgb_ref, dgb), (dya_ref, dya), (dyb_ref, dyb)):
            ref[...] = val.astype(BF16)
        first = pl.program_id(1) == 0
        _acc_out(dba_ref, dba, first)
        _acc_out(dbb_ref, dbb, first)

    d = cfg["d"]
    par = pl.BlockSpec((1, blk.block_shape[1]), lambda j, i: (0, j))
    return pl.pallas_call(
        body, name=name,
        out_shape=[jax.ShapeDtypeStruct((t, d), BF16)] * 4 + [jax.ShapeDtypeStruct((1, d), F32)] * 2,
        grid=(nj, t // tt),
        in_specs=[ga, gb, ba, bb, blk, blk, blk], out_specs=[blk] * 4 + [par] * 2,
        compiler_params=_cparams(("arbitrary", "arbitrary")),
    )(p_all, p_all, bias, bias, ya, yb, dm)


PAIRS = 8


def _pair_stack(ref, pairs):
    return jnp.stack([ref[:, p * LANES:(p + 1) * LANES] for p in range(pairs)])


def _pair_store(ref, val):
    for p in range(val.shape[0]):
        ref[:, p * LANES:(p + 1) * LANES] = val[p]


def _rec_specs(t, cfg, rev):
    dr = cfg["dr"]
    nc = t // CHUNK
    hp = dr // LANES
    pairs = _tile(hp, (PAIRS, 2, 1))
    ng = hp // pairs
    w = LANES * pairs
    ch = (lambda c: nc - 1 - c) if rev else (lambda c: c)
    slab = pl.BlockSpec((CHUNK, w), lambda h, c: (ch(c), h))
    vspec = pl.BlockSpec((CHUNK, w), lambda h, c: (ch(c), 2 * ng + h))
    sspec = pl.BlockSpec((None, pairs, LANES, LANES), lambda h, c: (ch(c), h, 0, 0))
    first = lambda: jnp.logical_and(pl.program_id(0) == 0, pl.program_id(1) == 0)
    last = lambda: jnp.logical_and(pl.program_id(0) == ng - 1, pl.program_id(1) == nc - 1)
    return nc, hp, pairs, ng, slab, vspec, sspec, first, last


def _rec_fwd(name, pm, lw, k2, a, b, cfg, comm=None):
    t = pm.shape[0]
    nc, hp, pairs, ng, slab, vspec, sspec, first, last = _rec_specs(t, cfg, False)

    def body(r_ref, lw_ref, k_ref, v_ref, a_ref, b_ref, y_ref, s_ref, s_scr):
        @pl.when(pl.program_id(1) == 0)
        def _():
            s_scr[...] = jnp.zeros_like(s_scr)

        s = s_scr[...]
        s_ref[...] = s
        y, s_new = _chunk_fn(s, *[_pair_stack(ref, pairs) for ref in (r_ref, lw_ref, k_ref, v_ref, a_ref, b_ref)])
        _pair_store(y_ref, y)
        s_scr[...] = s_new

    return _hosted_call(
        body, name, comm, first, last, args=[pm, lw, k2, pm, a, b],
        in_specs=[slab, slab, slab, vspec, slab, slab],
        out_shape=[jax.ShapeDtypeStruct((t, cfg["dr"]), F32), jax.ShapeDtypeStruct((nc, hp, LANES, LANES), F32)],
        out_specs=[slab, sspec], scratch=[pltpu.VMEM((pairs, LANES, LANES), F32)], grid=(ng, nc),
        sem=("arbitrary", "arbitrary"))


def _rec_bwd(name, pm, lw, k2, a, b, s_chk, dy, cfg, comm=None):
    t = pm.shape[0]
    nc, hp, pairs, ng, slab, vspec, sspec, first, last = _rec_specs(t, cfg, True)

    def body(r_ref, lw_ref, k_ref, v_ref, a_ref, b_ref, s_ref, dy_ref,
             dr_ref, dlw_ref, dk_ref, dv_ref, da_ref, db_ref, ds_scr):
        @pl.when(pl.program_id(1) == 0)
        def _():
            ds_scr[...] = jnp.zeros_like(ds_scr)

        _, vjp = jax.vjp(_chunk_fn, s_ref[...],
                         *[_pair_stack(ref, pairs) for ref in (r_ref, lw_ref, k_ref, v_ref, a_ref, b_ref)])
        ds, dr, dlw, dk, dv, da, db = vjp((_pair_stack(dy_ref, pairs), ds_scr[...]))
        ds_scr[...] = ds
        for ref, val in ((dr_ref, dr), (dlw_ref, dlw), (dk_ref, dk), (dv_ref, dv), (da_ref, da), (db_ref, db)):
            _pair_store(ref, val)

    return _hosted_call(
        body, name, comm, first, last, args=[pm, lw, k2, pm, a, b, s_chk, dy],
        in_specs=[slab, slab, slab, vspec, slab, slab, sspec, slab],
        out_shape=[jax.ShapeDtypeStruct((t, cfg["dr"]), F32)] * 6, out_specs=[slab] * 6,
        scratch=[pltpu.VMEM((pairs, LANES, LANES), F32)], grid=(ng, nc), sem=("arbitrary", "arbitrary"))


def _comm_call(name, comm):
    n = comm.n
    hbm = pl.BlockSpec(memory_space=pl.ANY)

    def body(*refs):
        comm.start(refs[:n], refs[n:2 * n], refs[2 * n:])
        comm.wait(refs[:n], refs[n:2 * n], refs[2 * n:])

    return pl.pallas_call(body, name=name, out_shape=comm.out_shape, in_specs=[hbm] * n, out_specs=[hbm] * n,
                          scratch_shapes=comm.scratch)(*comm.arrs)


def _all_reduce_small(name, v):
    rows = v.shape[0]
    vm = pl.BlockSpec(memory_space=pltpu.VMEM)

    def body(x_ref, out_ref, buf, send_sems, recv_sems):
        x, y, c = _my_pos()
        me, sibling = (x, y, c), (x, y, 1 - c)
        chips = [(1 - x, y), (x, 1 - y), (1 - x, 1 - y)]

        def copy(k, block, to, src=None):
            px, py, pc = block
            dst = buf.at[4 * px + 2 * py + pc]
            return pltpu.make_async_remote_copy(
                src_ref=dst if src is None else src, dst_ref=dst,
                send_sem=send_sems.at[k], recv_sem=recv_sems.at[k], device_id=to, device_id_type=MESH)

        buf[4 * x + 2 * y + c] = x_ref[...]
        first = [copy(0, me, sibling, src=x_ref)]
        first += [copy(1 + j, me, (*chip, c), src=x_ref) for j, chip in enumerate(chips)]
        for cp in first:
            cp.start()
        passed = [copy(4 + j, (*chip, c), sibling) for j, chip in enumerate(chips)]
        for j, chip in enumerate(chips):
            copy(1 + j, (*chip, c), me).wait_recv()
            passed[j].start()
        copy(0, sibling, me).wait_recv()
        for j, chip in enumerate(chips):
            copy(4 + j, (*chip, 1 - c), me).wait_recv()
        for cp in first + passed:
            cp.wait_send()
        acc = buf[0]
        for d in range(1, N_DEV):
            acc = acc + buf[d]
        out_ref[...] = acc

    return pl.pallas_call(
        body, name=name, out_shape=jax.ShapeDtypeStruct(v.shape, F32),
        in_specs=[vm], out_specs=vm,
        scratch_shapes=[pltpu.VMEM((N_DEV, rows, LANES), F32), pltpu.SemaphoreType.DMA((7,)),
                        pltpu.SemaphoreType.DMA((7,))],
    )(v)


def _adamw(name, w, m, v, g_own, g_recv=None):
    rows, cols = w.shape
    per_el = 4 * 3 + g_own.dtype.itemsize + (7 * g_recv.dtype.itemsize if g_recv is not None else 0) + 16
    rb = SUBLANES * 2
    while rb * 2 <= rows and rows % (rb * 2) == 0 and rb * 2 * cols * per_el * 2 <= VMEM_LIMIT // 2:
        rb *= 2
    if rows % rb:
        rb = rows
    blk = pl.BlockSpec((rb, cols), lambda i: (i, 0))
    rblk = pl.BlockSpec((N_DEV - 1, rb, cols), lambda i: (0, i, 0))
    has_r = g_recv is not None
    bc1 = 1.0 - ADAM_B1 ** ADAM_STEP
    bc2 = 1.0 - ADAM_B2 ** ADAM_STEP

    def body(*refs):
        w_ref, m_ref, v_ref, go_ref = refs[:4]
        gr_ref = refs[4] if has_r else None
        g_out, d_out, m_out, v_out = refs[4 + has_r:]
        g = go_ref[...].astype(F32)
        if has_r:
            for r in range(N_DEV - 1):
                g = g + gr_ref[r].astype(F32)
        mn = ADAM_B1 * m_ref[...] + (1.0 - ADAM_B1) * g
        vn = ADAM_B2 * v_ref[...] + (1.0 - ADAM_B2) * (g * g)
        m_hat = mn / bc1
        v_hat = vn / bc2
        g_out[...] = g
        d_out[...] = -ADAM_LR * (m_hat / (jnp.sqrt(v_hat) + ADAM_EPS) + ADAM_WD * w_ref[...])
        m_out[...] = mn
        v_out[...] = vn

    return pl.pallas_call(
        body, name=name, out_shape=[jax.ShapeDtypeStruct((rows, cols), F32)] * 4, grid=(rows // rb,),
        in_specs=[blk] * 4 + ([rblk] if has_r else []), out_specs=[blk] * 4,
        compiler_params=_cparams(("arbitrary",)),
    )(*([w, m, v, g_own] + ([g_recv] if has_r else [])))


def _round_up(n, q):
    return (n + q - 1) // q * q


def _cols(a8):
    return jnp.transpose(a8, (1, 0, 2)).reshape(a8.shape[1], -1)


def _col_slabs(a):
    r_, c_ = a.shape
    return jnp.transpose(a.reshape(r_, N_DEV, c_ // N_DEV), (1, 0, 2))


_MID = ("w_out_a", "w_out_b", "w_out", "w_mlp_up", "w_mlp_down")


def _local_step(x, target, wts, shards, cfg):
    dr, dc, d, lp, cb = cfg["dr"], cfg["dc"], cfg["d"], cfg["lp"], cfg["cb"]
    dff = shards["w_mlp_down"].shape[0] * N_DEV
    wmix = 3 * dr + lp
    (xn,) = _norm_fwd("norm_mix_fwd", x, None, wts["norm_mix_w"], False)
    (p_all,), (g_oa, g_ob, g_o, g_d) = _matmul(
        "mm_in", xn, wts["w_all"], "nn", [F32],
        comm=_Comm("gather", [shards["w_out_a"], shards["w_out_b"], shards["w_out"], shards["w_mlp_down"]]))
    w_out_a, w_out_b, w_out, w_down = _cols(g_oa), _cols(g_ob), g_o.reshape(d, d), g_d.reshape(dff, d)
    pm = _mix_fwd("mix_fwd", p_all, wts["mu_pad"], wmix, cb)
    prep_w = (wts["w0"], wts["a0"], wts["k_k"], wts["k_a"], wts["wd"], wts["wi"], wts["wg"])
    lw, k2, a_in, b_in, g = _prep_fwd("prep_fwd", pm, cfg, *prep_w)
    (y_raw, s_chk), (g_u,) = _rec_fwd(
        "rec_fwd", pm, lw, k2, a_in, b_in, cfg, comm=_Comm("gather", [shards["w_mlp_up"]]))
    w_up = _cols(g_u)
    post_w = (wts["lnx_w"], wts["lnx_b"], wts["r_k"])
    ya_in = _post_fwd("post_fwd", y_raw, pm, k2, g, *post_w, cfg)
    (ya,) = _matmul("mm_out_a", ya_in, w_out_a, "nn", [F32])
    yb_in = _conv_fwd("conv_fwd", p_all, wts["conv_w8"], cfg)
    (yb,) = _matmul("mm_out_b", yb_in, w_out_b, "nn", [F32])
    mg = _merge_fwd("merge_fwd", p_all, wts["gate_bias"], ya, yb, cfg)
    (mo,) = _matmul("mm_out", mg, w_out, "nn", [F32])
    h1, hn = _norm_fwd("norm_mlp_fwd", x, mo, wts["norm_mlp_w"], True)
    u, act = _matmul("mm_up", hn, w_up, "nn", [F32, BF16],
                     epi=lambda r: (r, jnp.square(jnp.maximum(r, 0.0))))
    (md,) = _matmul("mm_down", act, w_down, "nn", [F32])
    loss, dh2, dh2b, g_norm_final = _final("final", h1, md, target, wts["norm_final_w"])
    (du,) = _matmul("mm_down_dx", dh2b, w_down, "nt", [BF16],
                    epi=lambda r, uu: (r * (2.0 * jnp.maximum(uu, 0.0)),), extras=(u,))
    (g_down,) = _matmul("mm_down_dw", act, dh2b, "tn", [BF16])
    (dhn,) = _matmul("mm_up_dx", du, w_up, "nt", [F32])
    (g_up,) = _matmul("mm_up_dw", hn, du, "tn", [BF16])
    dh1, dh1b, g_norm_mlp = _norm_bwd("norm_mlp_bwd", h1, dhn, dh2, wts["norm_mlp_w"])
    (dmg,) = _matmul("mm_out_dx", dh1b, w_out, "nt", [F32])
    (g_out,) = _matmul("mm_out_dw", mg, dh1b, "tn", [BF16])
    dpga, dpgb, dya, dyb, dba, dbb = _merge_bwd("merge_bwd", p_all, wts["gate_bias"], ya, yb, dmg, cfg)
    (dya_in,) = _matmul("mm_out_a_dx", dya, w_out_a, "nt", [F32])
    (g_out_a,) = _matmul("mm_out_a_dw", ya_in, dya, "tn", [BF16])
    (dyb_in,) = _matmul("mm_out_b_dx", dyb, w_out_b, "nt", [F32])
    (g_out_b,) = _matmul("mm_out_b_dw", yb_in, dyb, "tn", [BF16])
    dpb, dpc, dpu, g_conv8 = _conv_bwd("conv_bwd", p_all, wts["conv_w8"], dyb_in, cfg)
    dy_raw, dr_post, dk_post, dv_post, dg, g_lnw, g_lnb, g_rk = _post_bwd(
        "post_bwd", y_raw, pm, k2, g, *post_w, dya_in, cfg)
    mid = dict(w_out_a=_col_slabs(g_out_a), w_out_b=_col_slabs(g_out_b), w_out=g_out.reshape(N_DEV, d // N_DEV, d),
               w_mlp_up=_col_slabs(g_up), w_mlp_down=g_down.reshape(N_DEV, dff // N_DEV, d))
    (dr_rec, dlw, dk_rec, dv_rec, da_in, db_in), (r_up, r_down) = _rec_bwd(
        "rec_bwd", pm, lw, k2, a_in, b_in, s_chk, dy_raw, cfg,
        comm=_Comm("exchange", [mid["w_mlp_up"], mid["w_mlp_down"]]))
    (dpm_r, dpm_k, dpm_v, dpl, g_w0, g_a0, g_kk, g_ka, g_wd, g_wi, g_wg) = _prep_bwd(
        "prep_bwd", pm, cfg, *prep_w, (dlw, dk_rec, dk_post, da_in, db_in, dg),
        (dr_rec, dr_post), (dv_rec, dv_post))
    mu = wts["mu_pad"]
    nb = dr // cb
    dps, dmus = [], []
    for s, dpm_s in enumerate((dpm_r, dpm_k, dpm_v)):
        dp_s, dmu_s = _mix_bwd("mix_bwd_%d" % s, [dpm_s], p_all, s * nb, mu[:, s * dr:(s + 1) * dr], cb)
        dps.append(dp_s)
        dmus.append(dmu_s)
    dp_l, dmu_l = _mix_bwd("mix_bwd_l", [dpl[j] for j in range(nb)], p_all, 3 * nb, mu[:, 3 * dr:], min(cb, lp))
    tail = [jnp.zeros((x.shape[0], cfg["wall"] - cfg["used"]), BF16)] if cfg["wall"] > cfg["used"] else []
    dp_all = jnp.concatenate(dps + [dp_l, dpb, dpc, dpu, dpga, dpgb] + tail, axis=1)
    (g_all,), (r_oa, r_ob, r_o) = _matmul(
        "mm_in_dw", xn, dp_all, "tn", [BF16],
        comm=_Comm("exchange", [mid["w_out_a"], mid["w_out_b"], mid["w_out"]]))
    ld, li, lora = cfg["ld"], cfg["li"], cfg["lora"]
    g_in = jnp.concatenate([g_all[:, :3 * dr + lora], g_all[:, 3 * dr + lp:cfg["used"]]], axis=1)
    g_small = jnp.concatenate([g_wd[:ld], g_wi[ld:ld + li], g_wg[ld + li:lora], g_conv8[:3]], axis=0)
    g_small = jnp.pad(g_small, ((0, cfg["small_rows"] - g_small.shape[0]), (0, 0)))
    mid = dict(mid, w_in=_col_slabs(g_in), small=_col_slabs(g_small))
    (dxn,), (r_in, r_small) = _matmul(
        "mm_in_dx", dp_all, wts["w_all"], "nt", [F32], comm=_Comm("exchange", [mid["w_in"], mid["small"]]))
    mid_recv = dict(w_out_a=r_oa, w_out_b=r_ob, w_out=r_o, w_mlp_up=r_up, w_mlp_down=r_down, w_in=r_in,
                    small=r_small)
    grad_x, _, g_norm_mix = _norm_bwd("norm_mix_bwd", x, dxn, dh1, wts["norm_mix_w"])
    grads = dict(
        norm_mix_w=g_norm_mix, gate_bias=jnp.concatenate([dba, dbb], axis=1),
        mu_pad=jnp.concatenate(dmus + [dmu_l], axis=1), w0=g_w0, a0=g_a0, k_k=g_kk, k_a=g_ka,
        r_k=g_rk, lnx_w=g_lnw, lnx_b=g_lnb, norm_mlp_w=g_norm_mlp, norm_final_w=g_norm_final)
    return loss, grad_x, grads, mid, mid_recv


_SMALL = ("norm_mix_w", "gate_bias", "shift_mu", "w0", "a0", "k_k", "k_a", "r_k", "lnx_w", "lnx_b",
          "norm_mlp_w", "norm_final_w")
_ORDER = ("norm_mix_w", "w_in", "gate_bias", "shift_mu", "w0", "w_decay_up", "a0", "w_iclr_up", "w_gate_up",
          "k_k", "k_a", "r_k", "lnx_w", "lnx_b", "w_out_a", "conv_w", "w_out_b", "w_out", "norm_mlp_w",
          "w_mlp_up", "w_mlp_down", "norm_final_w")


def _step(x, target, w, m, v):
    t, d = x.shape[1], x.shape[2]
    dr = w["w0"].shape[-1]
    ld, li, lg = w["w_decay_up"].shape[1], w["w_iclr_up"].shape[1], w["w_gate_up"].shape[1]
    lora = ld + li + lg
    lp = _round_up(lora, LANES)
    dc = w["conv_w"].shape[-1] * N_DEV
    cb = math.gcd(math.gcd(lp, dr), 512)
    used = 3 * dr + lp + 3 * dc + 2 * d
    wall = _round_up(used, 1024 if used > MAX_FULL_K else LANES)
    small_rows = ld + li + lg + 3
    cfg = dict(d=d, dr=dr, dc=dc, lp=lp, cb=cb, off_conv=3 * dr + lp, off_gate=3 * dr + lp + 3 * dc, used=used,
               wall=wall, ld=ld, li=li, lora=lora, small_rows=_round_up(small_rows, SUBLANES))
    x2, tg2 = x[0], target[0]

    small_sh = jnp.concatenate([w["w_decay_up"][0], w["w_iclr_up"][0], w["w_gate_up"][0], w["conv_w"][0]], axis=0)
    small_sh = jnp.pad(small_sh, ((0, _round_up(small_rows, SUBLANES) - small_rows), (0, 0)))
    big = ("w_in",) + _MID
    g_in8, gsm = _comm_call("gather_weights", _Comm("gather", [w["w_in"][0].astype(BF16), small_sh]))
    shards = {n: w[n][0].astype(BF16) for n in _MID}
    w_in = _cols(g_in8)
    zpad = jnp.zeros((d, lp - lora), BF16)
    w_all = jnp.concatenate([w_in[:, :3 * dr + lora], zpad, w_in[:, 3 * dr + lora:],
                             jnp.zeros((d, wall - used), BF16)], axis=1)
    sm = _cols(gsm)
    lora_full = sm[:lora]

    def lora_pad(lo, hi):
        rows = lax.broadcasted_iota(jnp.int32, (lp, 1), 0)
        full = jnp.pad(lora_full, ((0, lp - lora), (0, 0)))
        return jnp.where(jnp.logical_and(rows >= lo, rows < hi), full, 0.0)

    conv_w8 = jnp.pad(sm[lora:lora + 3], ((0, SUBLANES - 3), (0, 0)))
    mu_pad = jnp.pad(w["shift_mu"], ((0, 0), (0, lp - lora)))
    wts = dict(
        w_all=w_all, wd=lora_pad(0, ld), wi=lora_pad(ld, ld + li), wg=lora_pad(ld + li, lora), conv_w8=conv_w8,
        mu_pad=mu_pad, norm_mix_w=w["norm_mix_w"], gate_bias=w["gate_bias"], w0=w["w0"], a0=w["a0"],
        k_k=w["k_k"], k_a=w["k_a"], r_k=w["r_k"].reshape(1, dr), lnx_w=w["lnx_w"], lnx_b=w["lnx_b"],
        norm_mlp_w=w["norm_mlp_w"], norm_final_w=w["norm_final_w"].reshape(1, d))

    loss, grad_x, gr, slabs, received = _local_step(x2, tg2, wts, shards, cfg)

    names = big + ("small",)
    x_i, y_i, c_i = _my_pos()
    me = 4 * x_i + 2 * y_i + c_i
    own = {n: lax.dynamic_index_in_dim(slabs[n], me, axis=0, keepdims=False) for n in names}

    small_g = dict(norm_mix_w=gr["norm_mix_w"], gate_bias=gr["gate_bias"], shift_mu=gr["mu_pad"][:, :3 * dr + lora],
                   w0=gr["w0"], a0=gr["a0"], k_k=gr["k_k"], k_a=gr["k_a"], r_k=gr["r_k"], lnx_w=gr["lnx_w"],
                   lnx_b=gr["lnx_b"], norm_mlp_w=gr["norm_mlp_w"], norm_final_w=gr["norm_final_w"])
    sizes = [small_g[n].size for n in _SMALL]
    total = sum(sizes) + 1
    prow = _round_up(total, LANES * SUBLANES) // LANES

    def pack(parts):
        flat = jnp.concatenate([p.reshape(-1) for p in parts])
        return jnp.pad(flat, (0, prow * LANES - flat.size)).reshape(prow, LANES)

    g_packed = _all_reduce_small("reduce_small", pack([small_g[n] for n in _SMALL] + [loss[0, :1]]))
    one = jnp.zeros((1,), F32)
    packed = [pack([d_[n] for n in _SMALL] + [one]) for d_ in (w, m, v)]
    sm_out = _adamw("adamw_small", *packed, g_packed)
    loss_out = g_packed.reshape(-1)[total - 1]

    def unpack(flat2d):
        flat = flat2d.reshape(-1)
        out, o = {}, 0
        for n, s in zip(_SMALL, sizes):
            out[n] = flat[o:o + s].reshape(w[n].shape)
            o += s
        return out

    res = [unpack(a) for a in sm_out]

    def shard2d(a):
        return a.reshape(-1, a.shape[-1])

    for n in big:
        outs = _adamw("adamw_" + n, shard2d(w[n]), shard2d(m[n]), shard2d(v[n]), shard2d(own[n]),
                      received[n].reshape((N_DEV - 1,) + shard2d(own[n]).shape))
        for r_, o in zip(res, outs):
            r_[n] = o.reshape(w[n].shape)
    sm_names = ("w_decay_up", "w_iclr_up", "w_gate_up", "conv_w")
    stack = lambda d_: jnp.pad(jnp.concatenate([d_[n][0] for n in sm_names], axis=0),
                               ((0, _round_up(small_rows, SUBLANES) - small_rows), (0, 0)))
    outs = _adamw("adamw_stack", stack(w), stack(m), stack(v), own["small"], received["small"])
    bounds = (0, ld, ld + li, lora, lora + 3)
    for r_, o in zip(res, outs):
        for q, n in enumerate(sm_names):
            r_[n] = o[bounds[q]:bounds[q + 1]].reshape(w[n].shape)

    grad, delta, new_m, new_v = res
    return (loss_out, grad_x[None], *[grad[n] for n in _ORDER], *[delta[n] for n in _ORDER],
            *[new_m[n] for n in _ORDER], *[new_v[n] for n in _ORDER])


def kernel(x, norm_mix_w, w_in, gate_bias, shift_mu, w0, w_decay_up, a0, w_iclr_up, w_gate_up, k_k, k_a, r_k, lnx_w, lnx_b, w_out_a, conv_w, w_out_b, w_out, norm_mlp_w, w_mlp_up, w_mlp_down, norm_final_w, loss_target, m_norm_mix_w, m_w_in, m_gate_bias, m_shift_mu, m_w0, m_w_decay_up, m_a0, m_w_iclr_up, m_w_gate_up, m_k_k, m_k_a, m_r_k, m_lnx_w, m_lnx_b, m_w_out_a, m_conv_w, m_w_out_b, m_w_out, m_norm_mlp_w, m_w_mlp_up, m_w_mlp_down, m_norm_final_w, v_norm_mix_w, v_w_in, v_gate_bias, v_shift_mu, v_w0, v_w_decay_up, v_a0, v_w_iclr_up, v_w_gate_up, v_k_k, v_k_a, v_r_k, v_lnx_w, v_lnx_b, v_w_out_a, v_conv_w, v_w_out_b, v_w_out, v_norm_mlp_w, v_w_mlp_up, v_w_mlp_down, v_norm_final_w):
    w = dict(zip(_ORDER, (norm_mix_w, w_in, gate_bias, shift_mu, w0, w_decay_up, a0, w_iclr_up, w_gate_up, k_k, k_a,
                          r_k, lnx_w, lnx_b, w_out_a, conv_w, w_out_b, w_out, norm_mlp_w, w_mlp_up, w_mlp_down,
                          norm_final_w)))
    m = dict(zip(_ORDER, (m_norm_mix_w, m_w_in, m_gate_bias, m_shift_mu, m_w0, m_w_decay_up, m_a0, m_w_iclr_up,
                          m_w_gate_up, m_k_k, m_k_a, m_r_k, m_lnx_w, m_lnx_b, m_w_out_a, m_conv_w, m_w_out_b,
                          m_w_out, m_norm_mlp_w, m_w_mlp_up, m_w_mlp_down, m_norm_final_w)))
    v = dict(zip(_ORDER, (v_norm_mix_w, v_w_in, v_gate_bias, v_shift_mu, v_w0, v_w_decay_up, v_a0, v_w_iclr_up,
                          v_w_gate_up, v_k_k, v_k_a, v_r_k, v_lnx_w, v_lnx_b, v_w_out_a, v_conv_w, v_w_out_b,
                          v_w_out, v_norm_mlp_w, v_w_mlp_up, v_w_mlp_down, v_norm_final_w)))
    return _step(x, loss_target, w, m, v)
```

```python
import math

import jax
import jax.numpy as jnp
from jax import lax
from jax.experimental import pallas as pl
from jax.experimental.pallas import tpu as pltpu

F32 = jnp.float32
BF16 = jnp.bfloat16
MESH = pl.DeviceIdType.MESH

N_DEV = 8
HEAD = 64
LANES = 128
SUBLANES = 8
CHUNK = 64
RMS_EPS = 1e-5
LNX_EPS = 64e-5
L2_EPS = 1e-12
ADAM_LR = 0.001
ADAM_B1 = 0.9
ADAM_B2 = 0.999
ADAM_EPS = 1e-08
ADAM_WD = 0.01
ADAM_STEP = 10
VMEM_LIMIT = 48 * 1024 * 1024
MAX_FULL_K = 4096


def _cparams(sem):
    return pltpu.CompilerParams(dimension_semantics=sem, vmem_limit_bytes=VMEM_LIMIT)


def _tile(dim, cands):
    for c in cands:
        if c <= dim and dim % c == 0:
            return c
    return dim


def _my_pos():
    return lax.axis_index("x"), lax.axis_index("y"), lax.axis_index("c")


def _peer(pos, r):
    x, y, c = pos
    return (1 - x if r & 4 else x, 1 - y if r & 2 else y, 1 - c if r & 1 else c)


def _slot(pos):
    return 4 * pos[0] + 2 * pos[1] + pos[2]


class _Comm:
    def __init__(self, kind, arrs):
        self.kind, self.arrs, self.n = kind, list(arrs), len(arrs)
        if kind == "gather":
            self.out_shape = [jax.ShapeDtypeStruct((N_DEV,) + a.shape, a.dtype) for a in arrs]
        elif kind == "exchange":
            self.out_shape = [jax.ShapeDtypeStruct((N_DEV - 1,) + a.shape[1:], a.dtype) for a in arrs]
        elif kind == "pair":
            self.out_shape = [jax.ShapeDtypeStruct(a.shape, a.dtype) for a in arrs]
        else:
            self.out_shape = [jax.ShapeDtypeStruct((3,) + a.shape[1:], a.dtype) for a in arrs]
        self.scratch = [pltpu.SemaphoreType.DMA((7 * self.n,)), pltpu.SemaphoreType.DMA((7 * self.n,))]
        if kind == "gather":
            self.scratch.append(pltpu.SemaphoreType.DMA((self.n,)))

    def _exchange_copies(self, in_refs, out_refs, sems):
        me = _my_pos()
        x, y, c = me
        cps = []
        for ai in range(self.n):
            if self.kind == "exchange":
                todo = [(in_refs[ai].at[_slot(_peer(me, r))], out_refs[ai].at[r - 1], _peer(me, r), r - 1)
                        for r in range(1, N_DEV)]
            elif self.kind == "pair":
                todo = [(in_refs[ai], out_refs[ai], (x, y, 1 - c), 0)]
            else:
                chips = [(1 - x, y), (x, 1 - y), (1 - x, 1 - y)]
                todo = [(in_refs[ai].at[2 * cx + cy], out_refs[ai].at[j], (cx, cy, c), j)
                        for j, (cx, cy) in enumerate(chips)]
            for src, dst, to, k in todo:
                cps.append(pltpu.make_async_remote_copy(
                    src_ref=src, dst_ref=dst, send_sem=sems[0].at[ai * 7 + k], recv_sem=sems[1].at[ai * 7 + k],
                    device_id=to, device_id_type=MESH))
        return cps

    def _gather_parts(self, in_refs, out_refs, sems):
        x, y, c = _my_pos()
        me, sibling = (x, y, c), (x, y, 1 - c)
        chips = [(1 - x, y), (x, 1 - y), (1 - x, 1 - y)]

        def copy(ai, k, block, to, src=None):
            dst = out_refs[ai].at[_slot(block)]
            return pltpu.make_async_remote_copy(
                src_ref=dst if src is None else src, dst_ref=dst, send_sem=sems[0].at[ai * 7 + k],
                recv_sem=sems[1].at[ai * 7 + k], device_id=to, device_id_type=MESH)

        mine = [pltpu.make_async_copy(in_refs[ai], out_refs[ai].at[_slot(me)], sems[2].at[ai])
                for ai in range(self.n)]
        first = []
        for ai in range(self.n):
            first.append(copy(ai, 0, me, sibling, src=in_refs[ai]))
            first += [copy(ai, 1 + j, me, (*chip, c), src=in_refs[ai]) for j, chip in enumerate(chips)]
        return me, sibling, chips, c, copy, mine, first

    def start(self, in_refs, out_refs, sems):
        if self.kind != "gather":
            for cp in self._exchange_copies(in_refs, out_refs, sems):
                cp.start()
            return
        _, _, _, _, _, mine, first = self._gather_parts(in_refs, out_refs, sems)
        for cp in mine + first:
            cp.start()

    def wait(self, in_refs, out_refs, sems):
        if self.kind != "gather":
            for cp in self._exchange_copies(in_refs, out_refs, sems):
                cp.wait()
            return
        me, sibling, chips, c, copy, mine, first = self._gather_parts(in_refs, out_refs, sems)
        passed = []
        for ai in range(self.n):
            for j, chip in enumerate(chips):
                copy(ai, 1 + j, (*chip, c), me).wait_recv()
                fwd = copy(ai, 4 + j, (*chip, c), sibling)
                fwd.start()
                passed.append(fwd)
        for ai in range(self.n):
            copy(ai, 0, sibling, me).wait_recv()
            for j, chip in enumerate(chips):
                copy(ai, 4 + j, (*chip, 1 - c), me).wait_recv()
        for cp in first + passed:
            cp.wait_send()
        for cp in mine:
            cp.wait()


def _hosted_call(body, name, comm, first, last, *, args, in_specs, out_shape, out_specs, scratch, grid, sem):
    if comm is None:
        return pl.pallas_call(body, name=name, out_shape=out_shape, grid=grid, in_specs=in_specs, out_specs=out_specs,
                              scratch_shapes=scratch, compiler_params=_cparams(sem))(*args)
    ni, no, ns, nc = len(args), len(out_shape), len(scratch), comm.n
    hbm = pl.BlockSpec(memory_space=pl.ANY)

    def hosted(*refs):
        ins, cin = refs[:ni], refs[ni:ni + nc]
        outs, cout = refs[ni + nc:ni + nc + no], refs[ni + nc + no:ni + 2 * nc + no]
        scr, sems = refs[ni + 2 * nc + no:ni + 2 * nc + no + ns], refs[ni + 2 * nc + no + ns:]

        @pl.when(first())
        def _():
            comm.start(cin, cout, sems)

        body(*ins, *outs, *scr)

        @pl.when(last())
        def _():
            comm.wait(cin, cout, sems)

    res = pl.pallas_call(
        hosted, name=name, out_shape=list(out_shape) + comm.out_shape, grid=grid,
        in_specs=list(in_specs) + [hbm] * nc, out_specs=list(out_specs) + [hbm] * nc,
        scratch_shapes=list(scratch) + comm.scratch,
        compiler_params=_cparams(("arbitrary",) * len(grid)))(*args, *comm.arrs)
    return res[:no], res[no:]


_DIMS = {"nn": ((1,), (0,)), "nt": ((1,), (1,)), "tn": ((0,), (0,))}


def _matmul(name, a, b, mode, out_dtypes, epi=None, extras=(), comm=None):
    if mode == "nn":
        (m, k), n = a.shape, b.shape[1]
    elif mode == "nt":
        (m, k), n = a.shape, b.shape[0]
    else:
        (k, m), n = a.shape, b.shape[1]
    tm = _tile(m, (1024, 512, 256, 128, 64, 32, 16, 8))
    if k <= MAX_FULL_K:
        tk, tn = k, _tile(n, (512, 256, 128))
    else:
        tk, tn = _tile(k, (2048, 1024, 512, 256, 128)), _tile(n, (1024, 512, 256, 128))
    nk = k // tk
    gm, gn = m // tm, n // tn
    a_spec = pl.BlockSpec((tk, tm), lambda i, j, q: (q, i)) if mode == "tn" else pl.BlockSpec((tm, tk), lambda i, j, q: (i, q))
    b_spec = pl.BlockSpec((tn, tk), lambda i, j, q: (j, q)) if mode == "nt" else pl.BlockSpec((tk, tn), lambda i, j, q: (q, j))
    mn_spec = pl.BlockSpec((tm, tn), lambda i, j, q: (i, j))
    ne, no = len(extras), len(out_dtypes)
    dims = (_DIMS[mode], ((), ()))
    keep_t = mode == "tn" and nk == 1 and gn > 1

    def finish(r, extra_refs, out_refs):
        outs = (r,) if epi is None else epi(r, *[e[...] for e in extra_refs])
        for o_ref, o in zip(out_refs, outs):
            o_ref[...] = o.astype(o_ref.dtype)

    def body(a_ref, b_ref, *rest):
        extra_refs, out_refs = rest[:ne], rest[ne:ne + no]
        if keep_t:
            at = rest[ne + no]

            @pl.when(pl.program_id(1) == 0)
            def _():
                at[...] = a_ref[...].T

            part = jnp.dot(at[...], b_ref[...], preferred_element_type=F32)
        else:
            part = lax.dot_general(a_ref[...], b_ref[...], dims, preferred_element_type=F32)
        if nk == 1:
            finish(part, extra_refs, out_refs)
            return
        acc = rest[ne + no]
        q = pl.program_id(2)

        @pl.when(q == 0)
        def _():
            acc[...] = part

        @pl.when(jnp.logical_and(q > 0, q < nk - 1))
        def _():
            acc[...] += part

        @pl.when(q == nk - 1)
        def _():
            finish(acc[...] + part, extra_refs, out_refs)

    def first():
        return jnp.logical_and(jnp.logical_and(pl.program_id(0) == 0, pl.program_id(1) == 0), pl.program_id(2) == 0)

    def last():
        return jnp.logical_and(jnp.logical_and(pl.program_id(0) == gm - 1, pl.program_id(1) == gn - 1),
                               pl.program_id(2) == nk - 1)

    return _hosted_call(
        body, name, comm, first, last,
        args=[a, b, *extras], in_specs=[a_spec, b_spec] + [mn_spec] * ne,
        out_shape=[jax.ShapeDtypeStruct((m, n), dt) for dt in out_dtypes], out_specs=[mn_spec] * no,
        scratch=[pltpu.VMEM((tm, tn), F32)] if nk > 1 else ([pltpu.VMEM((tm, tk), a.dtype)] if keep_t else []),
        grid=(gm, gn, nk), sem=("parallel", "arbitrary" if keep_t else "parallel", "arbitrary"))


@jax.custom_vjp
def _mm(a, w):
    return jnp.dot(a.astype(BF16), w.astype(BF16), preferred_element_type=F32)


def _mm_fwd(a, w):
    return _mm(a, w), (a, w)


def _mm_bwd(res, ct):
    a, w = res
    ctb = ct.astype(BF16)
    da = lax.dot_general(ctb, w.astype(BF16), (((1,), (1,)), ((), ())), preferred_element_type=F32)
    dw = lax.dot_general(a.astype(BF16), ctb, (((0,), (0,)), ((), ())), preferred_element_type=F32)
    return da, dw


_mm.defvjp(_mm_fwd, _mm_bwd)


def _split3(x):
    hi = x.astype(BF16)
    r1 = x - hi.astype(F32)
    mid = r1.astype(BF16)
    lo = (r1 - mid.astype(F32)).astype(BF16)
    return hi, mid, lo


def _head_ones(width):
    r = lax.broadcasted_iota(jnp.int32, (width, width), 0) // HEAD
    c = lax.broadcasted_iota(jnp.int32, (width, width), 1) // HEAD
    return (r == c).astype(BF16)


@jax.custom_vjp
def _segsum(x):
    ones = _head_ones(x.shape[-1])
    out = None
    for piece in _split3(x):
        t = jnp.dot(piece, ones, preferred_element_type=F32)
        out = t if out is None else out + t
    return out


_segsum.defvjp(lambda x: (_segsum(x), None), lambda _, ct: (_segsum(ct),))


def _softplus(z):
    return jnp.maximum(z, 0.0) + jnp.log(1.0 + jnp.exp(-jnp.abs(z)))


def _sigmoid(z):
    return 1.0 / (1.0 + jnp.exp(-z))


def _rms(x, w):
    ms = jnp.mean(x * x, axis=-1, keepdims=True)
    return x * lax.rsqrt(ms + RMS_EPS) * w


def _row(ref, i):
    return ref[pl.ds(i, 1), :]


def _shift_down(x, prev_ref, n, first):
    rolled = pltpu.roll(x, n, 0)
    rows = lax.broadcasted_iota(jnp.int32, x.shape, 0)
    for q in range(n):
        halo = jnp.where(first, 0.0, _row(prev_ref, SUBLANES - n + q))
        rolled = jnp.where(rows == q, halo, rolled)
    return rolled


def _shift_up(x, next_ref, n, last):
    t = x.shape[0]
    rolled = pltpu.roll(x, t - n, 0)
    rows = lax.broadcasted_iota(jnp.int32, x.shape, 0)
    for q in range(n):
        halo = jnp.where(last, 0.0, _row(next_ref, q))
        rolled = jnp.where(rows == t - n + q, halo, rolled)
    return rolled


def _acc_out(ref, val, first):
    @pl.when(first)
    def _():
        ref[...] = val

    @pl.when(jnp.logical_not(first))
    def _():
        ref[...] += val


def _prep_fn(k, plm, w0, a0, kkw, kaw, wd, wi, wg):
    w_log = -_softplus(-(w0 + _mm(jnp.tanh(plm), wd))) - 0.5
    lw = -jnp.exp(w_log)
    a_g = _sigmoid(a0 + _mm(plm, wi))
    g = _mm(_sigmoid(plm), wg)
    kk = k * kkw
    kk = kk / jnp.maximum(jnp.sqrt(_segsum(kk * kk)), L2_EPS)
    k2 = k * (1.0 + (a_g - 1.0) * kaw)
    return lw, k2, -kk, kk * a_g, g


def _post_fn(y, r, k2, v, g, lnw, lnb, rk):
    mu = _segsum(y) * (1.0 / HEAD)
    yc = y - mu
    var = _segsum(yc * yc) * (1.0 / HEAD)
    yn = yc * lax.rsqrt(var + LNX_EPS) * lnw + lnb
    bonus = _segsum(r * k2 * rk) * v
    return (yn + bonus) * g


def _merge_fn(pga, pgb, ba, bb, ya, yb):
    return _sigmoid(pga + ba) * ya + _sigmoid(pgb + bb) * yb


_NN, _NT, _TN = ((2,), (1,)), ((2,), (2,)), ((1,), (1,))


def _dot3(a, b, dims):
    ah = a.astype(BF16)
    al = (a - ah.astype(F32)).astype(BF16)
    bh = b.astype(BF16)
    bl = (b - bh.astype(F32)).astype(BF16)
    dg = lambda p, q: lax.dot_general(p, q, (dims, ((0,), (0,))), preferred_element_type=F32)
    return dg(ah, bh) + (dg(ah, bl) + dg(al, bh))


@jax.custom_vjp
def _dnn(a, b):
    return _dot3(a, b, _NN)


@jax.custom_vjp
def _dnt(a, b):
    return _dot3(a, b, _NT)


@jax.custom_vjp
def _dtn(a, b):
    return _dot3(a, b, _TN)


_dnn.defvjp(lambda a, b: (_dnn(a, b), (a, b)), lambda res, ct: (_dnt(ct, res[1]), _dtn(res[0], ct)))
_dnt.defvjp(lambda a, b: (_dnt(a, b), (a, b)), lambda res, ct: (_dnn(ct, res[1]), _dtn(ct, res[0])))
_dtn.defvjp(lambda a, b: (_dtn(a, b), (a, b)), lambda res, ct: (_dnt(res[1], ct), _dnn(res[0], ct)))


def _chunk_fn(s, r, lw, k, v, a, b):
    np_, c = r.shape[0], r.shape[1]
    c2 = 2 * c
    ri = lax.broadcasted_iota(jnp.int32, (np_, c, c), 1)
    ci = lax.broadcasted_iota(jnp.int32, (np_, c, c), 2)
    tri = (ri >= ci).astype(F32)
    cum = _dnn(tri, lw)
    tot = jnp.sum(lw, axis=1, keepdims=True)
    g_in, g_inv, g_out = jnp.exp(cum), jnp.exp(-cum), jnp.exp(tot - cum)
    lane_head = lax.broadcasted_iota(jnp.int32, (1, 2, 1, LANES), 3) // HEAD
    which = lax.broadcasted_iota(jnp.int32, (1, 2, 1, LANES), 1)
    hmask = (lane_head == which).astype(F32)

    def st(x):
        return (x[:, None] * hmask).reshape(np_, c2, LANES)

    r2, a2 = st(r * g_in), st(a * jnp.exp(cum - lw))
    b2, k2, v2 = st(b * g_inv), st(k * g_inv), st(v)
    bo2, ko2 = st(b * g_out), st(k * g_out)
    r2i = lax.broadcasted_iota(jnp.int32, (np_, c2, c2), 1)
    c2i = lax.broadcasted_iota(jnp.int32, (np_, c2, c2), 2)
    same = (r2i >= c) == (c2i >= c)
    strict = jnp.logical_and(same, r2i > c2i)
    incl = jnp.logical_and(same, r2i >= c2i)
    lab = jnp.where(strict, _dnt(a2, b2), 0.0)
    lak = jnp.where(strict, _dnt(a2, k2), 0.0)
    mrb = jnp.where(incl, _dnt(r2, b2), 0.0)
    mrk = jnp.where(incl, _dnt(r2, k2), 0.0)
    x2 = _dnt(a2, s) + _dnn(lak, v2)
    eye = (r2i == c2i).astype(F32)
    tinv = eye + lab
    pw = lab
    for _ in range(int(math.log2(c)) - 1):
        pw = _dnn(pw, pw)
        tinv = tinv + _dnn(tinv, pw)
    u2 = _dnn(tinv, x2)
    y2 = _dnt(r2, s) + _dnn(mrb, u2) + _dnn(mrk, v2)
    y = jnp.sum(y2.reshape(np_, 2, c, LANES), axis=1)
    s_new = s * jnp.exp(tot) + _dtn(u2, bo2) + _dtn(v2, ko2)
    return y, s_new


def _norm_fwd(name, x, add, w, want_sum):
    t, d = x.shape
    tt = _tile(t, (128, 64, 32, 16, 8))
    row = pl.BlockSpec((tt, d), lambda i: (i, 0))
    par = pl.BlockSpec((1, d), lambda i: (0, 0))
    has_add = add is not None

    def body(*refs):
        x_ref = refs[0]
        add_ref = refs[1] if has_add else None
        w_ref = refs[1 + has_add]
        outs = refs[2 + has_add:]
        h = x_ref[...] + add_ref[...] if has_add else x_ref[...]
        if want_sum:
            outs[0][...] = h
        outs[-1][...] = _rms(h, w_ref[...]).astype(BF16)

    out_shape = ([jax.ShapeDtypeStruct((t, d), F32)] if want_sum else []) + [jax.ShapeDtypeStruct((t, d), BF16)]
    return pl.pallas_call(
        body, name=name, out_shape=out_shape, grid=(t // tt,),
        in_specs=[row] + ([row] if has_add else []) + [par],
        out_specs=[row] * len(out_shape),
        compiler_params=_cparams(("arbitrary",)),
    )(*([x] + ([add] if has_add else []) + [w]))


def _norm_bwd(name, xin, dy, dres, w):
    t, d = xin.shape
    tt = _tile(t, (128, 64, 32, 16, 8))
    row = pl.BlockSpec((tt, d), lambda i: (i, 0))
    par = pl.BlockSpec((1, d), lambda i: (0, 0))

    def body(x_ref, dy_ref, dres_ref, w_ref, dx_ref, dxb_ref, dw_ref):
        _, vjp = jax.vjp(_rms, x_ref[...], w_ref[...])
        dx, dw = vjp(dy_ref[...])
        dx = dx + dres_ref[...]
        dx_ref[...] = dx
        dxb_ref[...] = dx.astype(BF16)
        _acc_out(dw_ref, dw, pl.program_id(0) == 0)

    return pl.pallas_call(
        body, name=name,
        out_shape=[jax.ShapeDtypeStruct((t, d), F32), jax.ShapeDtypeStruct((t, d), BF16),
                   jax.ShapeDtypeStruct((1, d), F32)],
        grid=(t // tt,), in_specs=[row, row, row, par], out_specs=[row, row, par],
        compiler_params=_cparams(("arbitrary",)),
    )(xin, dy, dres, w)


def _final(name, h1, md, target, w):
    t, d = h1.shape
    tt = _tile(t, (128, 64, 32, 16, 8))
    row = pl.BlockSpec((tt, d), lambda i: (i, 0))
    par = pl.BlockSpec((1, d), lambda i: (0, 0))
    one = pl.BlockSpec((1, LANES), lambda i: (0, 0))

    def body(h1_ref, md_ref, tg_ref, w_ref, loss_ref, dh_ref, dhb_ref, dw_ref):
        tg = tg_ref[...]

        def f(h, wv):
            err = _rms(h, wv) - tg
            return 0.5 * jnp.sum(jnp.mean(err * err, axis=-1, keepdims=True), axis=0, keepdims=True)

        loss, vjp = jax.vjp(f, h1_ref[...] + md_ref[...], w_ref[...])
        dh, dw = vjp(jnp.ones((1, 1), F32))
        dh_ref[...] = dh
        dhb_ref[...] = dh.astype(BF16)
        first = pl.program_id(0) == 0
        _acc_out(dw_ref, dw, first)
        _acc_out(loss_ref, jnp.broadcast_to(loss, (1, LANES)), first)

    return pl.pallas_call(
        body, name=name,
        out_shape=[jax.ShapeDtypeStruct((1, LANES), F32), jax.ShapeDtypeStruct((t, d), F32),
                   jax.ShapeDtypeStruct((t, d), BF16), jax.ShapeDtypeStruct((1, d), F32)],
        grid=(t // tt,), in_specs=[row, row, row, par], out_specs=[one, row, row, par],
        compiler_params=_cparams(("arbitrary",)),
    )(h1, md, target, w)


def _halo_specs(tt, cb, nrow8, col_of):
    prev = pl.BlockSpec((SUBLANES, cb), lambda i, j: (jnp.maximum(i * (tt // SUBLANES) - 1, 0), col_of(j)))
    nxt = pl.BlockSpec((SUBLANES, cb), lambda i, j: (jnp.minimum((i + 1) * (tt // SUBLANES), nrow8 - 1), col_of(j)))
    return prev, nxt


def _mix_fwd(name, p_all, mu, width, cb):
    t = p_all.shape[0]
    tt = _tile(t, (256, 128, 64, 32, 16, 8))
    main = pl.BlockSpec((tt, cb), lambda i, j: (i, j))
    prev, _ = _halo_specs(tt, cb, t // SUBLANES, lambda j: j)
    par = pl.BlockSpec((1, cb), lambda i, j: (0, j))

    def body(p_ref, prev_ref, mu_ref, o_ref):
        p = p_ref[...]
        o_ref[...] = p + (_shift_down(p, prev_ref, 1, pl.program_id(0) == 0) - p) * mu_ref[...]

    return pl.pallas_call(
        body, name=name, out_shape=jax.ShapeDtypeStruct((t, width), F32),
        grid=(t // tt, width // cb), in_specs=[main, prev, par], out_specs=main,
        compiler_params=_cparams(("arbitrary", "arbitrary")),
    )(p_all, p_all, mu)


def _mix_bwd(name, dpm_list, p_all, col0, mu, cb):
    t, width = dpm_list[0].shape
    tt = _tile(t, (256, 128, 64, 32, 16, 8))
    n8 = t // SUBLANES
    nl = len(dpm_list)
    main = pl.BlockSpec((tt, cb), lambda j, i: (i, j))
    nxt = pl.BlockSpec((SUBLANES, cb), lambda j, i: (jnp.minimum((i + 1) * (tt // SUBLANES), n8 - 1), j))
    p_main = pl.BlockSpec((tt, cb), lambda j, i: (i, col0 + j))
    p_prev = pl.BlockSpec((SUBLANES, cb), lambda j, i: (jnp.maximum(i * (tt // SUBLANES) - 1, 0), col0 + j))
    par = pl.BlockSpec((1, cb), lambda j, i: (0, j))
    nt_ = t // tt

    def body(*refs):
        d_refs, dn_refs = refs[:nl], refs[nl:2 * nl]
        p_ref, pp_ref, mu_ref, dp_ref, dmu_ref, nx_scr = refs[2 * nl:]
        i = pl.program_id(1)
        dpm = d_refs[0][...]
        nx = dn_refs[0][...]
        for q in range(1, nl):
            dpm = dpm + d_refs[q][...]
            nx = nx + dn_refs[q][...]
        nx_scr[...] = nx
        mu_v = mu_ref[...]
        up = _shift_up(dpm, nx_scr, 1, i == nt_ - 1)
        dp_ref[...] = (dpm * (1.0 - mu_v) + up * mu_v).astype(BF16)
        p = p_ref[...]
        diff = _shift_down(p, pp_ref, 1, i == 0) - p
        _acc_out(dmu_ref, jnp.sum(dpm * diff, axis=0, keepdims=True), i == 0)

    return pl.pallas_call(
        body, name=name,
        out_shape=[jax.ShapeDtypeStruct((t, width), BF16), jax.ShapeDtypeStruct((1, width), F32)],
        grid=(width // cb, nt_),
        in_specs=[main] * nl + [nxt] * nl + [p_main, p_prev, par],
        out_specs=[main, par],
        scratch_shapes=[pltpu.VMEM((SUBLANES, cb), F32)],
        compiler_params=_cparams(("arbitrary", "arbitrary")),
    )(*dpm_list, *dpm_list, p_all, p_all, mu)


def _prep_fwd(name, pm, cfg, w0, a0, kkw, kaw, wd, wi, wg):
    t = pm.shape[0]
    dr, lp, cb = cfg["dr"], cfg["lp"], cfg["cb"]
    tt = _tile(t, (256, 128, 64, 32, 16, 8))
    nj = dr // cb
    kspec = pl.BlockSpec((tt, cb), lambda j, i: (i, nj + j))
    lspec = pl.BlockSpec((tt, lp), lambda j, i: (i, 3 * dr // lp))
    par = pl.BlockSpec((1, cb), lambda j, i: (0, j))
    wspec = pl.BlockSpec((lp, cb), lambda j, i: (0, j))
    out = pl.BlockSpec((tt, cb), lambda j, i: (i, j))

    def body(k_ref, l_ref, w0_ref, a0_ref, kk_ref, ka_ref, wd_ref, wi_ref, wg_ref, *outs):
        vals = _prep_fn(k_ref[...], l_ref[...], w0_ref[...], a0_ref[...], kk_ref[...], ka_ref[...],
                        wd_ref[...], wi_ref[...], wg_ref[...])
        for o_ref, val in zip(outs, vals):
            o_ref[...] = val

    return pl.pallas_call(
        body, name=name, out_shape=[jax.ShapeDtypeStruct((t, dr), F32)] * 5,
        grid=(nj, t // tt), in_specs=[kspec, lspec, par, par, par, par, wspec, wspec, wspec],
        out_specs=[out] * 5, compiler_params=_cparams(("arbitrary", "arbitrary")),
    )(pm, pm, w0, a0, kkw, kaw, wd, wi, wg)


def _prep_bwd(name, pm, cfg, w0, a0, kkw, kaw, wd, wi, wg, cts, dr_parts, dv_parts):
    t = pm.shape[0]
    dr, lp, cb = cfg["dr"], cfg["lp"], cfg["cb"]
    tt = _tile(t, (256, 128, 64, 32, 16, 8))
    nj = dr // cb
    kspec = pl.BlockSpec((tt, cb), lambda j, i: (i, nj + j))
    lspec = pl.BlockSpec((tt, lp), lambda j, i: (i, 3 * dr // lp))
    par = pl.BlockSpec((1, cb), lambda j, i: (0, j))
    wspec = pl.BlockSpec((lp, cb), lambda j, i: (0, j))
    blk = pl.BlockSpec((tt, cb), lambda j, i: (i, j))
    dpl_spec = pl.BlockSpec((None, tt, lp), lambda j, i: (j, i, 0))

    def body(k_ref, l_ref, w0_ref, a0_ref, kk_ref, ka_ref, wd_ref, wi_ref, wg_ref,
             dlw_ref, dk2a_ref, dk2b_ref, da_ref, db_ref, dg_ref, dr0_ref, dr1_ref, dv0_ref, dv1_ref,
             dpr_ref, dpk_ref, dpv_ref, dpl_ref, dw0_ref, da0_ref, dkk_ref, dka_ref, dwd_ref, dwi_ref, dwg_ref):
        _, vjp = jax.vjp(_prep_fn, k_ref[...], l_ref[...], w0_ref[...], a0_ref[...], kk_ref[...], ka_ref[...],
                         wd_ref[...], wi_ref[...], wg_ref[...])
        dk, dpl, dw0, da0, dkk, dka, dwd, dwi, dwg = vjp(
            (dlw_ref[...], dk2a_ref[...] + dk2b_ref[...], da_ref[...], db_ref[...], dg_ref[...]))
        dpr_ref[...] = dr0_ref[...] + dr1_ref[...]
        dpv_ref[...] = dv0_ref[...] + dv1_ref[...]
        dpk_ref[...] = dk
        dpl_ref[...] = dpl
        first = pl.program_id(1) == 0
        for ref, val in ((dw0_ref, dw0), (da0_ref, da0), (dkk_ref, dkk), (dka_ref, dka),
                         (dwd_ref, dwd), (dwi_ref, dwi), (dwg_ref, dwg)):
            _acc_out(ref, val, first)

    out_shape = ([jax.ShapeDtypeStruct((t, dr), F32)] * 3 + [jax.ShapeDtypeStruct((nj, t, lp), F32)]
                 + [jax.ShapeDtypeStruct((1, dr), F32)] * 4 + [jax.ShapeDtypeStruct((lp, dr), F32)] * 3)
    return pl.pallas_call(
        body, name=name, out_shape=out_shape, grid=(nj, t // tt),
        in_specs=[kspec, lspec, par, par, par, par, wspec, wspec, wspec] + [blk] * 10,
        out_specs=[blk] * 3 + [dpl_spec] + [par] * 4 + [wspec] * 3,
        compiler_params=_cparams(("arbitrary", "arbitrary")),
    )(pm, pm, w0, a0, kkw, kaw, wd, wi, wg, *cts, *dr_parts, *dv_parts)


def _post_specs(t, cfg):
    dr, cb = cfg["dr"], cfg["cb"]
    tt = _tile(t, (256, 128, 64, 32, 16, 8))
    nj = dr // cb
    blk = pl.BlockSpec((tt, cb), lambda j, i: (i, j))
    rspec = pl.BlockSpec((tt, cb), lambda j, i: (i, j))
    vspec = pl.BlockSpec((tt, cb), lambda j, i: (i, 2 * nj + j))
    par = pl.BlockSpec((1, cb), lambda j, i: (0, j))
    return tt, nj, blk, rspec, vspec, par


def _post_fwd(name, y, pm, k2, g, lnw, lnb, rk, cfg):
    t = y.shape[0]
    tt, nj, blk, rspec, vspec, par = _post_specs(t, cfg)

    def body(y_ref, r_ref, k_ref, v_ref, g_ref, lw_ref, lb_ref, rk_ref, o_ref):
        o_ref[...] = _post_fn(y_ref[...], r_ref[...], k_ref[...], v_ref[...], g_ref[...],
                              lw_ref[...], lb_ref[...], rk_ref[...]).astype(BF16)

    return pl.pallas_call(
        body, name=name, out_shape=jax.ShapeDtypeStruct((t, cfg["dr"]), BF16), grid=(nj, t // tt),
        in_specs=[blk, rspec, blk, vspec, blk, par, par, par], out_specs=blk,
        compiler_params=_cparams(("arbitrary", "arbitrary")),
    )(y, pm, k2, pm, g, lnw, lnb, rk)


def _post_bwd(name, y, pm, k2, g, lnw, lnb, rk, dout, cfg):
    t = y.shape[0]
    tt, nj, blk, rspec, vspec, par = _post_specs(t, cfg)

    def body(y_ref, r_ref, k_ref, v_ref, g_ref, lw_ref, lb_ref, rk_ref, do_ref,
             dy_ref, dr_ref, dk_ref, dv_ref, dg_ref, dlw_ref, dlb_ref, drk_ref):
        _, vjp = jax.vjp(_post_fn, y_ref[...], r_ref[...], k_ref[...], v_ref[...], g_ref[...],
                         lw_ref[...], lb_ref[...], rk_ref[...])
        dy, dr, dk, dv, dg, dlw, dlb, drk = vjp(do_ref[...])
        for ref, val in ((dy_ref, dy), (dr_ref, dr), (dk_ref, dk), (dv_ref, dv), (dg_ref, dg)):
            ref[...] = val
        first = pl.program_id(1) == 0
        for ref, val in ((dlw_ref, dlw), (dlb_ref, dlb), (drk_ref, drk)):
            _acc_out(ref, val, first)

    dr = cfg["dr"]
    return pl.pallas_call(
        body, name=name,
        out_shape=[jax.ShapeDtypeStruct((t, dr), F32)] * 5 + [jax.ShapeDtypeStruct((1, dr), F32)] * 3,
        grid=(nj, t // tt),
        in_specs=[blk, rspec, blk, vspec, blk, par, par, par, blk],
        out_specs=[blk] * 5 + [par] * 3,
        compiler_params=_cparams(("arbitrary", "arbitrary")),
    )(y, pm, k2, pm, g, lnw, lnb, rk, dout)


def _conv_specs(t, cfg):
    dc, cb = cfg["dc"], cfg["cb"]
    tt = _tile(t, (256, 128, 64, 32, 16, 8))
    nj = dc // cb
    c0 = cfg["off_conv"] // cb
    n8 = t // SUBLANES

    def sect(s):
        col = lambda j: c0 + s * nj + j
        main = pl.BlockSpec((tt, cb), lambda j, i: (i, col(j)))
        prev = pl.BlockSpec((SUBLANES, cb), lambda j, i: (jnp.maximum(i * (tt // SUBLANES) - 1, 0), col(j)))
        nxt = pl.BlockSpec((SUBLANES, cb), lambda j, i: (jnp.minimum((i + 1) * (tt // SUBLANES), n8 - 1), col(j)))
        return main, prev, nxt

    blk = pl.BlockSpec((tt, cb), lambda j, i: (i, j))
    wspec = pl.BlockSpec((SUBLANES, cb), lambda j, i: (0, j))
    return tt, nj, n8, sect, blk, wspec


def _conv_fwd(name, p_all, cw8, cfg):
    t = p_all.shape[0]
    tt, nj, n8, sect, blk, wspec = _conv_specs(t, cfg)
    (bm, _, _), (cm, cp, _), (um, up, _) = sect(0), sect(1), sect(2)

    def body(b_ref, c_ref, cp_ref, u_ref, up_ref, w_ref, o_ref, zp_scr):
        first = pl.program_id(1) == 0
        z = c_ref[...] * u_ref[...]
        zp_scr[...] = cp_ref[...] * up_ref[...]
        o = _row(w_ref, 2) * z + _row(w_ref, 1) * _shift_down(z, zp_scr, 1, first) \
            + _row(w_ref, 0) * _shift_down(z, zp_scr, 2, first)
        o_ref[...] = (b_ref[...] * o).astype(BF16)

    return pl.pallas_call(
        body, name=name, out_shape=jax.ShapeDtypeStruct((t, cfg["dc"]), BF16), grid=(nj, t // tt),
        in_specs=[bm, cm, cp, um, up, wspec], out_specs=blk,
        scratch_shapes=[pltpu.VMEM((SUBLANES, blk.block_shape[1]), F32)],
        compiler_params=_cparams(("arbitrary", "arbitrary")),
    )(p_all, p_all, p_all, p_all, p_all, cw8)


def _conv_bwd(name, p_all, cw8, dyb, cfg):
    t = p_all.shape[0]
    tt, nj, n8, sect, blk, wspec = _conv_specs(t, cfg)
    (bm, _, bn), (cm, cp, _), (um, up, _) = sect(0), sect(1), sect(2)
    cb = blk.block_shape[1]
    dnxt = pl.BlockSpec((SUBLANES, cb), lambda j, i: (jnp.minimum((i + 1) * (tt // SUBLANES), n8 - 1), j))
    nt_ = t // tt

    def body(b_ref, bn_ref, c_ref, cp_ref, u_ref, up_ref, w_ref, d_ref, dn_ref,
             db_ref, dc_ref, du_ref, dw_ref, zp_scr, don_scr):
        i = pl.program_id(1)
        first, last = i == 0, i == nt_ - 1
        c, u, b, dy = c_ref[...], u_ref[...], b_ref[...], d_ref[...]
        z = c * u
        zp_scr[...] = cp_ref[...] * up_ref[...]
        z1 = _shift_down(z, zp_scr, 1, first)
        z2 = _shift_down(z, zp_scr, 2, first)
        w0, w1, w2 = _row(w_ref, 0), _row(w_ref, 1), _row(w_ref, 2)
        o = w2 * z + w1 * z1 + w0 * z2
        do = dy * b
        don_scr[...] = dn_ref[...] * bn_ref[...]
        dz = w2 * do + w1 * _shift_up(do, don_scr, 1, last) + w0 * _shift_up(do, don_scr, 2, last)
        db_ref[...] = (dy * o).astype(BF16)
        dc_ref[...] = (dz * u).astype(BF16)
        du_ref[...] = (dz * c).astype(BF16)
        rows = lax.broadcasted_iota(jnp.int32, (SUBLANES, cb), 0)
        s0 = jnp.sum(do * z2, axis=0, keepdims=True)
        s1 = jnp.sum(do * z1, axis=0, keepdims=True)
        s2 = jnp.sum(do * z, axis=0, keepdims=True)
        dw = jnp.where(rows == 0, s0, jnp.where(rows == 1, s1, jnp.where(rows == 2, s2, 0.0)))
        _acc_out(dw_ref, dw, first)

    dc = cfg["dc"]
    return pl.pallas_call(
        body, name=name,
        out_shape=[jax.ShapeDtypeStruct((t, dc), BF16)] * 3 + [jax.ShapeDtypeStruct((SUBLANES, dc), F32)],
        grid=(nj, nt_),
        in_specs=[bm, bn, cm, cp, um, up, wspec, blk, dnxt],
        out_specs=[blk] * 3 + [wspec],
        scratch_shapes=[pltpu.VMEM((SUBLANES, cb), F32), pltpu.VMEM((SUBLANES, cb), F32)],
        compiler_params=_cparams(("arbitrary", "arbitrary")),
    )(p_all, p_all, p_all, p_all, p_all, p_all, cw8, dyb, dyb)


def _merge_specs(t, cfg):
    d, cb = cfg["d"], cfg["cb"]
    tt = _tile(t, (256, 128, 64, 32, 16, 8))
    nj = d // cb
    g0 = cfg["off_gate"] // cb
    ga = pl.BlockSpec((tt, cb), lambda j, i: (i, g0 + j))
    gb = pl.BlockSpec((tt, cb), lambda j, i: (i, g0 + nj + j))
    ba = pl.BlockSpec((1, cb), lambda j, i: (0, j))
    bb = pl.BlockSpec((1, cb), lambda j, i: (0, nj + j))
    blk = pl.BlockSpec((tt, cb), lambda j, i: (i, j))
    return tt, nj, ga, gb, ba, bb, blk


def _merge_fwd(name, p_all, bias, ya, yb, cfg):
    t = p_all.shape[0]
    tt, nj, ga, gb, ba, bb, blk = _merge_specs(t, cfg)

    def body(ga_ref, gb_ref, ba_ref, bb_ref, ya_ref, yb_ref, o_ref):
        o_ref[...] = _merge_fn(ga_ref[...], gb_ref[...], ba_ref[...], bb_ref[...],
                               ya_ref[...], yb_ref[...]).astype(BF16)

    return pl.pallas_call(
        body, name=name, out_shape=jax.ShapeDtypeStruct((t, cfg["d"]), BF16), grid=(nj, t // tt),
        in_specs=[ga, gb, ba, bb, blk, blk], out_specs=blk,
        compiler_params=_cparams(("arbitrary", "arbitrary")),
    )(p_all, p_all, bias, bias, ya, yb)


def _merge_bwd(name, p_all, bias, ya, yb, dm, cfg):
    t = p_all.shape[0]
    tt, nj, ga, gb, ba, bb, blk = _merge_specs(t, cfg)

    def body(ga_ref, gb_ref, ba_ref, bb_ref, ya_ref, yb_ref, dm_ref,
             dga_ref, dgb_ref, dya_ref, dyb_ref, dba_ref, dbb_ref):
        _, vjp = jax.vjp(_merge_fn, ga_ref[...], gb_ref[...], ba_ref[...], bb_ref[...], ya_ref[...], yb_ref[...])
        dga, dgb, dba, dbb, dya, dyb = vjp(dm_ref[...])
        for ref, val in ((dga_ref, dga), (dgb_ref, dgb), (dya_ref, dya), (dyb_ref, dyb)):
            ref[...] = val.astype(BF16)
        first = pl.program_id(1) == 0
        _acc_out(dba_ref, dba, first)
        _acc_out(dbb_ref, dbb, first)

    d = cfg["d"]
    par = pl.BlockSpec((1, blk.block_shape[1]), lambda j, i: (0, j))
    return pl.pallas_call(
        body, name=name,
        out_shape=[jax.ShapeDtypeStruct((t, d), BF16)] * 4 + [jax.ShapeDtypeStruct((1, d), F32)] * 2,
        grid=(nj, t // tt),
        in_specs=[ga, gb, ba, bb, blk, blk, blk], out_specs=[blk] * 4 + [par] * 2,
        compiler_params=_cparams(("arbitrary", "arbitrary")),
    )(p_all, p_all, bias, bias, ya, yb, dm)


PAIRS = 8


def _pair_stack(ref, pairs):
    return jnp.stack([ref[:, p * LANES:(p + 1) * LANES] for p in range(pairs)])


def _pair_store(ref, val):
    for p in range(val.shape[0]):
        ref[:, p * LANES:(p + 1) * LANES] = val[p]


def _rec_specs(t, cfg, rev):
    dr = cfg["dr"]
    nc = t // CHUNK
    hp = dr // LANES
    pairs = _tile(hp, (PAIRS, 2, 1))
    ng = hp // pairs
    w = LANES * pairs
    ch = (lambda c: nc - 1 - c) if rev else (lambda c: c)
    slab = pl.BlockSpec((CHUNK, w), lambda h, c: (ch(c), h))
    vspec = pl.BlockSpec((CHUNK, w), lambda h, c: (ch(c), 2 * ng + h))
    sspec = pl.BlockSpec((None, pairs, LANES, LANES), lambda h, c: (ch(c), h, 0, 0))
    first = lambda: jnp.logical_and(pl.program_id(0) == 0, pl.program_id(1) == 0)
    last = lambda: jnp.logical_and(pl.program_id(0) == ng - 1, pl.program_id(1) == nc - 1)
    return nc, hp, pairs, ng, slab, vspec, sspec, first, last


def _rec_fwd(name, pm, lw, k2, a, b, cfg, comm=None):
    t = pm.shape[0]
    nc, hp, pairs, ng, slab, vspec, sspec, first, last = _rec_specs(t, cfg, False)

    def body(r_ref, lw_ref, k_ref, v_ref, a_ref, b_ref, y_ref, s_ref, s_scr):
        @pl.when(pl.program_id(1) == 0)
        def _():
            s_scr[...] = jnp.zeros_like(s_scr)

        s = s_scr[...]
        s_ref[...] = s
        y, s_new = _chunk_fn(s, *[_pair_stack(ref, pairs) for ref in (r_ref, lw_ref, k_ref, v_ref, a_ref, b_ref)])
        _pair_store(y_ref, y)
        s_scr[...] = s_new

    return _hosted_call(
        body, name, comm, first, last, args=[pm, lw, k2, pm, a, b],
        in_specs=[slab, slab, slab, vspec, slab, slab],
        out_shape=[jax.ShapeDtypeStruct((t, cfg["dr"]), F32), jax.ShapeDtypeStruct((nc, hp, LANES, LANES), F32)],
        out_specs=[slab, sspec], scratch=[pltpu.VMEM((pairs, LANES, LANES), F32)], grid=(ng, nc),
        sem=("arbitrary", "arbitrary"))


def _rec_bwd(name, pm, lw, k2, a, b, s_chk, dy, cfg, comm=None):
    t = pm.shape[0]
    nc, hp, pairs, ng, slab, vspec, sspec, first, last = _rec_specs(t, cfg, True)

    def body(r_ref, lw_ref, k_ref, v_ref, a_ref, b_ref, s_ref, dy_ref,
             dr_ref, dlw_ref, dk_ref, dv_ref, da_ref, db_ref, ds_scr):
        @pl.when(pl.program_id(1) == 0)
        def _():
            ds_scr[...] = jnp.zeros_like(ds_scr)

        _, vjp = jax.vjp(_chunk_fn, s_ref[...],
                         *[_pair_stack(ref, pairs) for ref in (r_ref, lw_ref, k_ref, v_ref, a_ref, b_ref)])
        ds, dr, dlw, dk, dv, da, db = vjp((_pair_stack(dy_ref, pairs), ds_scr[...]))
        ds_scr[...] = ds
        for ref, val in ((dr_ref, dr), (dlw_ref, dlw), (dk_ref, dk), (dv_ref, dv), (da_ref, da), (db_ref, db)):
            _pair_store(ref, val)

    return _hosted_call(
        body, name, comm, first, last, args=[pm, lw, k2, pm, a, b, s_chk, dy],
        in_specs=[slab, slab, slab, vspec, slab, slab, sspec, slab],
        out_shape=[jax.ShapeDtypeStruct((t, cfg["dr"]), F32)] * 6, out_specs=[slab] * 6,
        scratch=[pltpu.VMEM((pairs, LANES, LANES), F32)], grid=(ng, nc), sem=("arbitrary", "arbitrary"))


def _comm_call(name, comm):
    n = comm.n
    hbm = pl.BlockSpec(memory_space=pl.ANY)

    def body(*refs):
        comm.start(refs[:n], refs[n:2 * n], refs[2 * n:])
        comm.wait(refs[:n], refs[n:2 * n], refs[2 * n:])

    return pl.pallas_call(body, name=name, out_shape=comm.out_shape, in_specs=[hbm] * n, out_specs=[hbm] * n,
                          scratch_shapes=comm.scratch)(*comm.arrs)


def _all_reduce_small(name, v):
    rows = v.shape[0]
    vm = pl.BlockSpec(memory_space=pltpu.VMEM)

    def body(x_ref, out_ref, buf, send_sems, recv_sems):
        x, y, c = _my_pos()
        me, sibling = (x, y, c), (x, y, 1 - c)
        chips = [(1 - x, y), (x, 1 - y), (1 - x, 1 - y)]

        def copy(k, block, to, src=None):
            px, py, pc = block
            dst = buf.at[4 * px + 2 * py + pc]
            return pltpu.make_async_remote_copy(
                src_ref=dst if src is None else src, dst_ref=dst,
                send_sem=send_sems.at[k], recv_sem=recv_sems.at[k], device_id=to, device_id_type=MESH)

        buf[4 * x + 2 * y + c] = x_ref[...]
        first = [copy(0, me, sibling, src=x_ref)]
        first += [copy(1 + j, me, (*chip, c), src=x_ref) for j, chip in enumerate(chips)]
        for cp in first:
            cp.start()
        passed = [copy(4 + j, (*chip, c), sibling) for j, chip in enumerate(chips)]
        for j, chip in enumerate(chips):
            copy(1 + j, (*chip, c), me).wait_recv()
            passed[j].start()
        copy(0, sibling, me).wait_recv()
        for j, chip in enumerate(chips):
            copy(4 + j, (*chip, 1 - c), me).wait_recv()
        for cp in first + passed:
            cp.wait_send()
        acc = buf[0]
        for d in range(1, N_DEV):
            acc = acc + buf[d]
        out_ref[...] = acc

    return pl.pallas_call(
        body, name=name, out_shape=jax.ShapeDtypeStruct(v.shape, F32),
        in_specs=[vm], out_specs=vm,
        scratch_shapes=[pltpu.VMEM((N_DEV, rows, LANES), F32), pltpu.SemaphoreType.DMA((7,)),
                        pltpu.SemaphoreType.DMA((7,))],
    )(v)


def _pair_sum(name, a, b):
    s, rows, cols = a.shape
    rb = _tile(rows, (256, 128, 64, 32, 16, 8))
    blk = pl.BlockSpec((None, rb, cols), lambda i, j: (i, j, 0))

    def body(a_ref, b_ref, o_ref):
        o_ref[...] = (a_ref[...].astype(F32) + b_ref[...].astype(F32)).astype(o_ref.dtype)

    return pl.pallas_call(
        body, name=name, out_shape=jax.ShapeDtypeStruct(a.shape, a.dtype), grid=(s, rows // rb),
        in_specs=[blk, blk], out_specs=blk, compiler_params=_cparams(("arbitrary", "arbitrary")))(a, b)


def _adamw(name, w, m, v, g_own, g_recv=None):
    rows, cols = w.shape
    nr = g_recv.shape[0] if g_recv is not None else 0
    per_el = 4 * 3 + g_own.dtype.itemsize + (nr * g_recv.dtype.itemsize if nr else 0) + 16
    rb = SUBLANES * 2
    while rb * 2 <= rows and rows % (rb * 2) == 0 and rb * 2 * cols * per_el * 2 <= VMEM_LIMIT // 2:
        rb *= 2
    if rows % rb:
        rb = rows
    blk = pl.BlockSpec((rb, cols), lambda i: (i, 0))
    rblk = pl.BlockSpec((max(nr, 1), rb, cols), lambda i: (0, i, 0))
    has_r = g_recv is not None
    bc1 = 1.0 - ADAM_B1 ** ADAM_STEP
    bc2 = 1.0 - ADAM_B2 ** ADAM_STEP

    def body(*refs):
        w_ref, m_ref, v_ref, go_ref = refs[:4]
        gr_ref = refs[4] if has_r else None
        g_out, d_out, m_out, v_out = refs[4 + has_r:]
        g = go_ref[...].astype(F32)
        if has_r:
            for r in range(nr):
                g = g + gr_ref[r].astype(F32)
        mn = ADAM_B1 * m_ref[...] + (1.0 - ADAM_B1) * g
        vn = ADAM_B2 * v_ref[...] + (1.0 - ADAM_B2) * (g * g)
        m_hat = mn / bc1
        v_hat = vn / bc2
        g_out[...] = g
        d_out[...] = -ADAM_LR * (m_hat / (jnp.sqrt(v_hat) + ADAM_EPS) + ADAM_WD * w_ref[...])
        m_out[...] = mn
        v_out[...] = vn

    return pl.pallas_call(
        body, name=name, out_shape=[jax.ShapeDtypeStruct((rows, cols), F32)] * 4, grid=(rows // rb,),
        in_specs=[blk] * 4 + ([rblk] if has_r else []), out_specs=[blk] * 4,
        compiler_params=_cparams(("arbitrary",)),
    )(*([w, m, v, g_own] + ([g_recv] if has_r else [])))


def _round_up(n, q):
    return (n + q - 1) // q * q


def _cols(a8):
    return jnp.transpose(a8, (1, 0, 2)).reshape(a8.shape[1], -1)


def _col_slabs(a):
    r_, c_ = a.shape
    return jnp.transpose(a.reshape(r_, N_DEV, c_ // N_DEV), (1, 0, 2))


_MID = ("w_out_a", "w_out_b", "w_out", "w_mlp_up", "w_mlp_down")


def _local_step(x, target, wts, shards, cfg):
    dr, dc, d, lp, cb = cfg["dr"], cfg["dc"], cfg["d"], cfg["lp"], cfg["cb"]
    dff = shards["w_mlp_down"].shape[0] * N_DEV
    wmix = 3 * dr + lp
    (xn,) = _norm_fwd("norm_mix_fwd", x, None, wts["norm_mix_w"], False)
    (p_all,), (g_oa, g_ob, g_o) = _matmul(
        "mm_in", xn, wts["w_all"], "nn", [F32],
        comm=_Comm("gather", [shards["w_out_a"], shards["w_out_b"], shards["w_out"]]))
    w_out_a, w_out_b, w_out = _cols(g_oa), _cols(g_ob), g_o.reshape(d, d)
    pm = _mix_fwd("mix_fwd", p_all, wts["mu_pad"], wmix, cb)
    prep_w = (wts["w0"], wts["a0"], wts["k_k"], wts["k_a"], wts["wd"], wts["wi"], wts["wg"])
    lw, k2, a_in, b_in, g = _prep_fwd("prep_fwd", pm, cfg, *prep_w)
    (y_raw, s_chk), (g_u,) = _rec_fwd(
        "rec_fwd", pm, lw, k2, a_in, b_in, cfg, comm=_Comm("gather", [shards["w_mlp_up"]]))
    w_up = _cols(g_u)
    post_w = (wts["lnx_w"], wts["lnx_b"], wts["r_k"])
    ya_in = _post_fwd("post_fwd", y_raw, pm, k2, g, *post_w, cfg)
    (ya,) = _matmul("mm_out_a", ya_in, w_out_a, "nn", [F32])
    yb_in = _conv_fwd("conv_fwd", p_all, wts["conv_w8"], cfg)
    (yb,) = _matmul("mm_out_b", yb_in, w_out_b, "nn", [F32])
    mg = _merge_fwd("merge_fwd", p_all, wts["gate_bias"], ya, yb, cfg)
    (mo,) = _matmul("mm_out", mg, w_out, "nn", [F32])
    h1, hn = _norm_fwd("norm_mlp_fwd", x, mo, wts["norm_mlp_w"], True)
    (u, act), (g_d,) = _matmul("mm_up", hn, w_up, "nn", [F32, BF16],
                               epi=lambda r: (r, jnp.square(jnp.maximum(r, 0.0))),
                               comm=_Comm("gather", [shards["w_mlp_down"]]))
    w_down = g_d.reshape(dff, d)
    (md,) = _matmul("mm_down", act, w_down, "nn", [F32])
    loss, dh2, dh2b, g_norm_final = _final("final", h1, md, target, wts["norm_final_w"])
    (du,) = _matmul("mm_down_dx", dh2b, w_down, "nt", [BF16],
                    epi=lambda r, uu: (r * (2.0 * jnp.maximum(uu, 0.0)),), extras=(u,))
    (g_down,) = _matmul("mm_down_dw", act, dh2b, "tn", [BF16])
    (dhn,) = _matmul("mm_up_dx", du, w_up, "nt", [F32])
    (g_up,) = _matmul("mm_up_dw", hn, du, "tn", [BF16])
    dh1, dh1b, g_norm_mlp = _norm_bwd("norm_mlp_bwd", h1, dhn, dh2, wts["norm_mlp_w"])
    (dmg,) = _matmul("mm_out_dx", dh1b, w_out, "nt", [F32])
    (g_out,) = _matmul("mm_out_dw", mg, dh1b, "tn", [BF16])
    dpga, dpgb, dya, dyb, dba, dbb = _merge_bwd("merge_bwd", p_all, wts["gate_bias"], ya, yb, dmg, cfg)
    (dya_in,) = _matmul("mm_out_a_dx", dya, w_out_a, "nt", [F32])
    (g_out_a,) = _matmul("mm_out_a_dw", ya_in, dya, "tn", [BF16])
    (dyb_in,) = _matmul("mm_out_b_dx", dyb, w_out_b, "nt", [F32])
    (g_out_b,) = _matmul("mm_out_b_dw", yb_in, dyb, "tn", [BF16])
    dpb, dpc, dpu, g_conv8 = _conv_bwd("conv_bwd", p_all, wts["conv_w8"], dyb_in, cfg)
    dy_raw, dr_post, dk_post, dv_post, dg, g_lnw, g_lnb, g_rk = _post_bwd(
        "post_bwd", y_raw, pm, k2, g, *post_w, dya_in, cfg)
    mid = dict(w_out_a=_col_slabs(g_out_a), w_out_b=_col_slabs(g_out_b), w_out=g_out.reshape(N_DEV, d // N_DEV, d),
               w_mlp_up=_col_slabs(g_up), w_mlp_down=g_down.reshape(N_DEV, dff // N_DEV, d))
    (dr_rec, dlw, dk_rec, dv_rec, da_in, db_in), (r_up, r_down) = _rec_bwd(
        "rec_bwd", pm, lw, k2, a_in, b_in, s_chk, dy_raw, cfg,
        comm=_Comm("exchange", [mid["w_mlp_up"], mid["w_mlp_down"]]))
    (dpm_r, dpm_k, dpm_v, dpl, g_w0, g_a0, g_kk, g_ka, g_wd, g_wi, g_wg) = _prep_bwd(
        "prep_bwd", pm, cfg, *prep_w, (dlw, dk_rec, dk_post, da_in, db_in, dg),
        (dr_rec, dr_post), (dv_rec, dv_post))
    mu = wts["mu_pad"]
    nb = dr // cb
    dps, dmus = [], []
    for s, dpm_s in enumerate((dpm_r, dpm_k, dpm_v)):
        dp_s, dmu_s = _mix_bwd("mix_bwd_%d" % s, [dpm_s], p_all, s * nb, mu[:, s * dr:(s + 1) * dr], cb)
        dps.append(dp_s)
        dmus.append(dmu_s)
    dp_l, dmu_l = _mix_bwd("mix_bwd_l", [dpl[j] for j in range(nb)], p_all, 3 * nb, mu[:, 3 * dr:], min(cb, lp))
    tail = [jnp.zeros((x.shape[0], cfg["wall"] - cfg["used"]), BF16)] if cfg["wall"] > cfg["used"] else []
    dp_all = jnp.concatenate(dps + [dp_l, dpb, dpc, dpu, dpga, dpgb] + tail, axis=1)
    ld, li, lora = cfg["ld"], cfg["li"], cfg["lora"]
    g_small = jnp.concatenate([g_wd[:ld], g_wi[ld:ld + li], g_wg[ld + li:lora], g_conv8[:3]], axis=0)
    g_small = jnp.pad(g_small, ((0, cfg["small_rows"] - g_small.shape[0]), (0, 0)))
    mid["small"] = _col_slabs(g_small)
    (g_all,), (r_oa, r_ob, r_o, r_small) = _matmul(
        "mm_in_dw", xn, dp_all, "tn", [BF16],
        comm=_Comm("exchange", [mid["w_out_a"], mid["w_out_b"], mid["w_out"], mid["small"]]))
    g_in = jnp.concatenate([g_all[:, :3 * dr + lora], g_all[:, 3 * dr + lp:cfg["used"]]], axis=1)
    by_core = _col_slabs(g_in).reshape(N_DEV // 2, 2, d, g_in.shape[1] // N_DEV)
    my_c = lax.axis_index("c")
    keep = lax.dynamic_index_in_dim(by_core, my_c, axis=1, keepdims=False)
    give = lax.dynamic_index_in_dim(by_core, 1 - my_c, axis=1, keepdims=False)
    (got,) = _comm_call("pair_exchange", _Comm("pair", [give]))
    chip_sum = _pair_sum("pair_sum", keep, got)
    (dxn,), (r_in,) = _matmul(
        "mm_in_dx", dp_all, wts["w_all"], "nt", [F32], comm=_Comm("chips", [chip_sum]))
    my_chip = 2 * lax.axis_index("x") + lax.axis_index("y")
    mid["w_in_own"] = lax.dynamic_index_in_dim(chip_sum, my_chip, axis=0, keepdims=False)
    mid_recv = dict(w_out_a=r_oa, w_out_b=r_ob, w_out=r_o, w_mlp_up=r_up, w_mlp_down=r_down, w_in=r_in,
                    small=r_small)
    grad_x, _, g_norm_mix = _norm_bwd("norm_mix_bwd", x, dxn, dh1, wts["norm_mix_w"])
    grads = dict(
        norm_mix_w=g_norm_mix, gate_bias=jnp.concatenate([dba, dbb], axis=1),
        mu_pad=jnp.concatenate(dmus + [dmu_l], axis=1), w0=g_w0, a0=g_a0, k_k=g_kk, k_a=g_ka,
        r_k=g_rk, lnx_w=g_lnw, lnx_b=g_lnb, norm_mlp_w=g_norm_mlp, norm_final_w=g_norm_final)
    return loss, grad_x, grads, mid, mid_recv


_SMALL = ("norm_mix_w", "gate_bias", "shift_mu", "w0", "a0", "k_k", "k_a", "r_k", "lnx_w", "lnx_b",
          "norm_mlp_w", "norm_final_w")
_ORDER = ("norm_mix_w", "w_in", "gate_bias", "shift_mu", "w0", "w_decay_up", "a0", "w_iclr_up", "w_gate_up",
          "k_k", "k_a", "r_k", "lnx_w", "lnx_b", "w_out_a", "conv_w", "w_out_b", "w_out", "norm_mlp_w",
          "w_mlp_up", "w_mlp_down", "norm_final_w")


def _step(x, target, w, m, v):
    t, d = x.shape[1], x.shape[2]
    dr = w["w0"].shape[-1]
    ld, li, lg = w["w_decay_up"].shape[1], w["w_iclr_up"].shape[1], w["w_gate_up"].shape[1]
    lora = ld + li + lg
    lp = _round_up(lora, LANES)
    dc = w["conv_w"].shape[-1] * N_DEV
    cb = math.gcd(math.gcd(lp, dr), 512)
    used = 3 * dr + lp + 3 * dc + 2 * d
    wall = _round_up(used, 1024 if used > MAX_FULL_K else LANES)
    small_rows = ld + li + lg + 3
    cfg = dict(d=d, dr=dr, dc=dc, lp=lp, cb=cb, off_conv=3 * dr + lp, off_gate=3 * dr + lp + 3 * dc, used=used,
               wall=wall, ld=ld, li=li, lora=lora, small_rows=_round_up(small_rows, SUBLANES))
    x2, tg2 = x[0], target[0]

    small_sh = jnp.concatenate([w["w_decay_up"][0], w["w_iclr_up"][0], w["w_gate_up"][0], w["conv_w"][0]], axis=0)
    small_sh = jnp.pad(small_sh, ((0, _round_up(small_rows, SUBLANES) - small_rows), (0, 0)))
    big = ("w_in",) + _MID
    g_in8, gsm = _comm_call("gather_weights", _Comm("gather", [w["w_in"][0].astype(BF16), small_sh]))
    shards = {n: w[n][0].astype(BF16) for n in _MID}
    w_in = _cols(g_in8)
    zpad = jnp.zeros((d, lp - lora), BF16)
    w_all = jnp.concatenate([w_in[:, :3 * dr + lora], zpad, w_in[:, 3 * dr + lora:],
                             jnp.zeros((d, wall - used), BF16)], axis=1)
    sm = _cols(gsm)
    lora_full = sm[:lora]

    def lora_pad(lo, hi):
        rows = lax.broadcasted_iota(jnp.int32, (lp, 1), 0)
        full = jnp.pad(lora_full, ((0, lp - lora), (0, 0)))
        return jnp.where(jnp.logical_and(rows >= lo, rows < hi), full, 0.0)

    conv_w8 = jnp.pad(sm[lora:lora + 3], ((0, SUBLANES - 3), (0, 0)))
    mu_pad = jnp.pad(w["shift_mu"], ((0, 0), (0, lp - lora)))
    wts = dict(
        w_all=w_all, wd=lora_pad(0, ld), wi=lora_pad(ld, ld + li), wg=lora_pad(ld + li, lora), conv_w8=conv_w8,
        mu_pad=mu_pad, norm_mix_w=w["norm_mix_w"], gate_bias=w["gate_bias"], w0=w["w0"], a0=w["a0"],
        k_k=w["k_k"], k_a=w["k_a"], r_k=w["r_k"].reshape(1, dr), lnx_w=w["lnx_w"], lnx_b=w["lnx_b"],
        norm_mlp_w=w["norm_mlp_w"], norm_final_w=w["norm_final_w"].reshape(1, d))

    loss, grad_x, gr, slabs, received = _local_step(x2, tg2, wts, shards, cfg)

    x_i, y_i, c_i = _my_pos()
    me = 4 * x_i + 2 * y_i + c_i
    own = {n: lax.dynamic_index_in_dim(slabs[n], me, axis=0, keepdims=False) for n in _MID + ("small",)}
    own["w_in"] = slabs["w_in_own"]

    small_g = dict(norm_mix_w=gr["norm_mix_w"], gate_bias=gr["gate_bias"], shift_mu=gr["mu_pad"][:, :3 * dr + lora],
                   w0=gr["w0"], a0=gr["a0"], k_k=gr["k_k"], k_a=gr["k_a"], r_k=gr["r_k"], lnx_w=gr["lnx_w"],
                   lnx_b=gr["lnx_b"], norm_mlp_w=gr["norm_mlp_w"], norm_final_w=gr["norm_final_w"])
    sizes = [small_g[n].size for n in _SMALL]
    total = sum(sizes) + 1
    prow = _round_up(total, LANES * SUBLANES) // LANES

    def pack(parts):
        flat = jnp.concatenate([p.reshape(-1) for p in parts])
        return jnp.pad(flat, (0, prow * LANES - flat.size)).reshape(prow, LANES)

    g_packed = _all_reduce_small("reduce_small", pack([small_g[n] for n in _SMALL] + [loss[0, :1]]))
    one = jnp.zeros((1,), F32)
    packed = [pack([d_[n] for n in _SMALL] + [one]) for d_ in (w, m, v)]
    sm_out = _adamw("adamw_small", *packed, g_packed)
    loss_out = g_packed.reshape(-1)[total - 1]

    def unpack(flat2d):
        flat = flat2d.reshape(-1)
        out, o = {}, 0
        for n, s in zip(_SMALL, sizes):
            out[n] = flat[o:o + s].reshape(w[n].shape)
            o += s
        return out

    res = [unpack(a) for a in sm_out]

    def shard2d(a):
        return a.reshape(-1, a.shape[-1])

    for n in big:
        outs = _adamw("adamw_" + n, shard2d(w[n]), shard2d(m[n]), shard2d(v[n]), shard2d(own[n]),
                      received[n].reshape(received[n].shape[:1] + shard2d(own[n]).shape))
        for r_, o in zip(res, outs):
            r_[n] = o.reshape(w[n].shape)
    sm_names = ("w_decay_up", "w_iclr_up", "w_gate_up", "conv_w")
    stack = lambda d_: jnp.pad(jnp.concatenate([d_[n][0] for n in sm_names], axis=0),
                               ((0, _round_up(small_rows, SUBLANES) - small_rows), (0, 0)))
    outs = _adamw("adamw_stack", stack(w), stack(m), stack(v), own["small"], received["small"])
    bounds = (0, ld, ld + li, lora, lora + 3)
    for r_, o in zip(res, outs):
        for q, n in enumerate(sm_names):
            r_[n] = o[bounds[q]:bounds[q + 1]].reshape(w[n].shape)

    grad, delta, new_m, new_v = res
    return (loss_out, grad_x[None], *[grad[n] for n in _ORDER], *[delta[n] for n in _ORDER],
            *[new_m[n] for n in _ORDER], *[new_v[n] for n in _ORDER])


def kernel(x, norm_mix_w, w_in, gate_bias, shift_mu, w0, w_decay_up, a0, w_iclr_up, w_gate_up, k_k, k_a, r_k, lnx_w, lnx_b, w_out_a, conv_w, w_out_b, w_out, norm_mlp_w, w_mlp_up, w_mlp_down, norm_final_w, loss_target, m_norm_mix_w, m_w_in, m_gate_bias, m_shift_mu, m_w0, m_w_decay_up, m_a0, m_w_iclr_up, m_w_gate_up, m_k_k, m_k_a, m_r_k, m_lnx_w, m_lnx_b, m_w_out_a, m_conv_w, m_w_out_b, m_w_out, m_norm_mlp_w, m_w_mlp_up, m_w_mlp_down, m_norm_final_w, v_norm_mix_w, v_w_in, v_gate_bias, v_shift_mu, v_w0, v_w_decay_up, v_a0, v_w_iclr_up, v_w_gate_up, v_k_k, v_k_a, v_r_k, v_lnx_w, v_lnx_b, v_w_out_a, v_conv_w, v_w_out_b, v_w_out, v_norm_mlp_w, v_w_mlp_up, v_w_mlp_down, v_norm_final_w):
    w = dict(zip(_ORDER, (norm_mix_w, w_in, gate_bias, shift_mu, w0, w_decay_up, a0, w_iclr_up, w_gate_up, k_k, k_a,
                          r_k, lnx_w, lnx_b, w_out_a, conv_w, w_out_b, w_out, norm_mlp_w, w_mlp_up, w_mlp_down,
                          norm_final_w)))
    m = dict(zip(_ORDER, (m_norm_mix_w, m_w_in, m_gate_bias, m_shift_mu, m_w0, m_w_decay_up, m_a0, m_w_iclr_up,
                          m_w_gate_up, m_k_k, m_k_a, m_r_k, m_lnx_w, m_lnx_b, m_w_out_a, m_conv_w, m_w_out_b,
                          m_w_out, m_norm_mlp_w, m_w_mlp_up, m_w_mlp_down, m_norm_final_w)))
    v = dict(zip(_ORDER, (v_norm_mix_w, v_w_in, v_gate_bias, v_shift_mu, v_w0, v_w_decay_up, v_a0, v_w_iclr_up,
                          v_w_gate_up, v_k_k, v_k_a, v_r_k, v_lnx_w, v_lnx_b, v_w_out_a, v_conv_w, v_w_out_b,
                          v_w_out, v_norm_mlp_w, v_w_mlp_up, v_w_mlp_down, v_norm_final_w)))
    return _step(x, loss_target, w, m, v)
```

```python
import math

import jax
import jax.numpy as jnp
from jax import lax
from jax.experimental import pallas as pl
from jax.experimental.pallas import tpu as pltpu

F32 = jnp.float32
BF16 = jnp.bfloat16
MESH = pl.DeviceIdType.MESH

N_DEV = 8
HEAD = 64
LANES = 128
SUBLANES = 8
CHUNK = 64
RMS_EPS = 1e-5
LNX_EPS = 64e-5
L2_EPS = 1e-12
ADAM_LR = 0.001
ADAM_B1 = 0.9
ADAM_B2 = 0.999
ADAM_EPS = 1e-08
ADAM_WD = 0.01
ADAM_STEP = 10
VMEM_LIMIT = 48 * 1024 * 1024
MAX_FULL_K = 4096


def _cparams(sem):
    return pltpu.CompilerParams(dimension_semantics=sem, vmem_limit_bytes=VMEM_LIMIT)


def _tile(dim, cands):
    for c in cands:
        if c <= dim and dim % c == 0:
            return c
    return dim


def _my_pos():
    return lax.axis_index("x"), lax.axis_index("y"), lax.axis_index("c")


def _peer(pos, r):
    x, y, c = pos
    return (1 - x if r & 4 else x, 1 - y if r & 2 else y, 1 - c if r & 1 else c)


def _slot(pos):
    return 4 * pos[0] + 2 * pos[1] + pos[2]


class _Comm:
    def __init__(self, kind, arrs):
        self.kind, self.arrs, self.n = kind, list(arrs), len(arrs)
        if kind == "gather":
            self.out_shape = [jax.ShapeDtypeStruct((N_DEV,) + a.shape, a.dtype) for a in arrs]
        elif kind == "exchange":
            self.out_shape = [jax.ShapeDtypeStruct((N_DEV - 1,) + a.shape[1:], a.dtype) for a in arrs]
        elif kind == "pair":
            self.out_shape = [jax.ShapeDtypeStruct((N_DEV // 2,) + a.shape[1:], a.dtype) for a in arrs]
        else:
            self.out_shape = [jax.ShapeDtypeStruct((3,) + a.shape[1:], a.dtype) for a in arrs]
        self.scratch = [pltpu.SemaphoreType.DMA((7 * self.n,)), pltpu.SemaphoreType.DMA((7 * self.n,))]
        if kind == "gather":
            self.scratch.append(pltpu.SemaphoreType.DMA((self.n,)))

    def _exchange_copies(self, in_refs, out_refs, sems):
        me = _my_pos()
        x, y, c = me
        cps = []
        for ai in range(self.n):
            if self.kind == "exchange":
                todo = [(in_refs[ai].at[_slot(_peer(me, r))], out_refs[ai].at[r - 1], _peer(me, r), r - 1)
                        for r in range(1, N_DEV)]
            elif self.kind == "pair":
                todo = [(in_refs[ai].at[2 * q + 1 - c], out_refs[ai].at[q], (x, y, 1 - c), q)
                        for q in range(N_DEV // 2)]
            else:
                chips = [(1 - x, y), (x, 1 - y), (1 - x, 1 - y)]
                todo = [(in_refs[ai].at[2 * cx + cy], out_refs[ai].at[j], (cx, cy, c), j)
                        for j, (cx, cy) in enumerate(chips)]
            for src, dst, to, k in todo:
                cps.append(pltpu.make_async_remote_copy(
                    src_ref=src, dst_ref=dst, send_sem=sems[0].at[ai * 7 + k], recv_sem=sems[1].at[ai * 7 + k],
                    device_id=to, device_id_type=MESH))
        return cps

    def _gather_parts(self, in_refs, out_refs, sems):
        x, y, c = _my_pos()
        me, sibling = (x, y, c), (x, y, 1 - c)
        chips = [(1 - x, y), (x, 1 - y), (1 - x, 1 - y)]

        def copy(ai, k, block, to, src=None):
            dst = out_refs[ai].at[_slot(block)]
            return pltpu.make_async_remote_copy(
                src_ref=dst if src is None else src, dst_ref=dst, send_sem=sems[0].at[ai * 7 + k],
                recv_sem=sems[1].at[ai * 7 + k], device_id=to, device_id_type=MESH)

        mine = [pltpu.make_async_copy(in_refs[ai], out_refs[ai].at[_slot(me)], sems[2].at[ai])
                for ai in range(self.n)]
        first = []
        for ai in range(self.n):
            first.append(copy(ai, 0, me, sibling, src=in_refs[ai]))
            first += [copy(ai, 1 + j, me, (*chip, c), src=in_refs[ai]) for j, chip in enumerate(chips)]
        return me, sibling, chips, c, copy, mine, first

    def start(self, in_refs, out_refs, sems):
        if self.kind != "gather":
            for cp in self._exchange_copies(in_refs, out_refs, sems):
                cp.start()
            return
        _, _, _, _, _, mine, first = self._gather_parts(in_refs, out_refs, sems)
        for cp in mine + first:
            cp.start()

    def wait(self, in_refs, out_refs, sems):
        if self.kind != "gather":
            for cp in self._exchange_copies(in_refs, out_refs, sems):
                cp.wait()
            return
        me, sibling, chips, c, copy, mine, first = self._gather_parts(in_refs, out_refs, sems)
        passed = []
        for ai in range(self.n):
            for j, chip in enumerate(chips):
                copy(ai, 1 + j, (*chip, c), me).wait_recv()
                fwd = copy(ai, 4 + j, (*chip, c), sibling)
                fwd.start()
                passed.append(fwd)
        for ai in range(self.n):
            copy(ai, 0, sibling, me).wait_recv()
            for j, chip in enumerate(chips):
                copy(ai, 4 + j, (*chip, 1 - c), me).wait_recv()
        for cp in first + passed:
            cp.wait_send()
        for cp in mine:
            cp.wait()


def _hosted_call(body, name, comm, first, last, *, args, in_specs, out_shape, out_specs, scratch, grid, sem):
    if comm is None:
        return pl.pallas_call(body, name=name, out_shape=out_shape, grid=grid, in_specs=in_specs, out_specs=out_specs,
                              scratch_shapes=scratch, compiler_params=_cparams(sem))(*args)
    ni, no, ns, nc = len(args), len(out_shape), len(scratch), comm.n
    hbm = pl.BlockSpec(memory_space=pl.ANY)

    def hosted(*refs):
        ins, cin = refs[:ni], refs[ni:ni + nc]
        outs, cout = refs[ni + nc:ni + nc + no], refs[ni + nc + no:ni + 2 * nc + no]
        scr, sems = refs[ni + 2 * nc + no:ni + 2 * nc + no + ns], refs[ni + 2 * nc + no + ns:]

        @pl.when(first())
        def _():
            comm.start(cin, cout, sems)

        body(*ins, *outs, *scr)

        @pl.when(last())
        def _():
            comm.wait(cin, cout, sems)

    res = pl.pallas_call(
        hosted, name=name, out_shape=list(out_shape) + comm.out_shape, grid=grid,
        in_specs=list(in_specs) + [hbm] * nc, out_specs=list(out_specs) + [hbm] * nc,
        scratch_shapes=list(scratch) + comm.scratch,
        compiler_params=_cparams(("arbitrary",) * len(grid)))(*args, *comm.arrs)
    return res[:no], res[no:]


_DIMS = {"nn": ((1,), (0,)), "nt": ((1,), (1,)), "tn": ((0,), (0,))}


def _matmul(name, a, b, mode, out_dtypes, epi=None, extras=(), comm=None, out_slabs=False):
    if mode == "nn":
        (m, k), n = a.shape, b.shape[1]
    elif mode == "nt":
        (m, k), n = a.shape, b.shape[0]
    else:
        (k, m), n = a.shape, b.shape[1]
    tm = _tile(m, (1024, 512, 256, 128, 64, 32, 16, 8))
    if k <= MAX_FULL_K:
        tk, tn = k, _tile(n // N_DEV if out_slabs else n, (512, 256, 128))
    else:
        tk, tn = _tile(k, (2048, 1024, 512, 256, 128)), _tile(n, (1024, 512, 256, 128))
    nk = k // tk
    gm, gn = m // tm, n // tn
    a_spec = pl.BlockSpec((tk, tm), lambda i, j, q: (q, i)) if mode == "tn" else pl.BlockSpec((tm, tk), lambda i, j, q: (i, q))
    b_spec = pl.BlockSpec((tn, tk), lambda i, j, q: (j, q)) if mode == "nt" else pl.BlockSpec((tk, tn), lambda i, j, q: (q, j))
    mn_spec = pl.BlockSpec((tm, tn), lambda i, j, q: (i, j))
    per = n // N_DEV // tn if out_slabs else 0
    out_spec = pl.BlockSpec((None, tm, tn), lambda i, j, q: (j // per, i, j % per)) if out_slabs else mn_spec
    ne, no = len(extras), len(out_dtypes)
    dims = (_DIMS[mode], ((), ()))
    keep_t = mode == "tn" and nk == 1 and gn > 1

    def finish(r, extra_refs, out_refs):
        outs = (r,) if epi is None else epi(r, *[e[...] for e in extra_refs])
        for o_ref, o in zip(out_refs, outs):
            o_ref[...] = o.astype(o_ref.dtype)

    def body(a_ref, b_ref, *rest):
        extra_refs, out_refs = rest[:ne], rest[ne:ne + no]
        if keep_t:
            at = rest[ne + no]

            @pl.when(pl.program_id(1) == 0)
            def _():
                at[...] = a_ref[...].T

            part = jnp.dot(at[...], b_ref[...], preferred_element_type=F32)
        else:
            part = lax.dot_general(a_ref[...], b_ref[...], dims, preferred_element_type=F32)
        if nk == 1:
            finish(part, extra_refs, out_refs)
            return
        acc = rest[ne + no]
        q = pl.program_id(2)

        @pl.when(q == 0)
        def _():
            acc[...] = part

        @pl.when(jnp.logical_and(q > 0, q < nk - 1))
        def _():
            acc[...] += part

        @pl.when(q == nk - 1)
        def _():
            finish(acc[...] + part, extra_refs, out_refs)

    def first():
        return jnp.logical_and(jnp.logical_and(pl.program_id(0) == 0, pl.program_id(1) == 0), pl.program_id(2) == 0)

    def last():
        return jnp.logical_and(jnp.logical_and(pl.program_id(0) == gm - 1, pl.program_id(1) == gn - 1),
                               pl.program_id(2) == nk - 1)

    return _hosted_call(
        body, name, comm, first, last,
        args=[a, b, *extras], in_specs=[a_spec, b_spec] + [mn_spec] * ne,
        out_shape=[jax.ShapeDtypeStruct((N_DEV, m, n // N_DEV) if out_slabs else (m, n), dt) for dt in out_dtypes],
        out_specs=[out_spec] * no,
        scratch=[pltpu.VMEM((tm, tn), F32)] if nk > 1 else ([pltpu.VMEM((tm, tk), a.dtype)] if keep_t else []),
        grid=(gm, gn, nk), sem=("parallel", "arbitrary" if keep_t else "parallel", "arbitrary"))


@jax.custom_vjp
def _mm(a, w):
    return jnp.dot(a.astype(BF16), w.astype(BF16), preferred_element_type=F32)


def _mm_fwd(a, w):
    return _mm(a, w), (a, w)


def _mm_bwd(res, ct):
    a, w = res
    ctb = ct.astype(BF16)
    da = lax.dot_general(ctb, w.astype(BF16), (((1,), (1,)), ((), ())), preferred_element_type=F32)
    dw = lax.dot_general(a.astype(BF16), ctb, (((0,), (0,)), ((), ())), preferred_element_type=F32)
    return da, dw


_mm.defvjp(_mm_fwd, _mm_bwd)


def _split3(x):
    hi = x.astype(BF16)
    r1 = x - hi.astype(F32)
    mid = r1.astype(BF16)
    lo = (r1 - mid.astype(F32)).astype(BF16)
    return hi, mid, lo


def _head_ones(width):
    r = lax.broadcasted_iota(jnp.int32, (width, width), 0) // HEAD
    c = lax.broadcasted_iota(jnp.int32, (width, width), 1) // HEAD
    return (r == c).astype(BF16)


@jax.custom_vjp
def _segsum(x):
    ones = _head_ones(x.shape[-1])
    out = None
    for piece in _split3(x):
        t = jnp.dot(piece, ones, preferred_element_type=F32)
        out = t if out is None else out + t
    return out


_segsum.defvjp(lambda x: (_segsum(x), None), lambda _, ct: (_segsum(ct),))


def _softplus(z):
    return jnp.maximum(z, 0.0) + jnp.log(1.0 + jnp.exp(-jnp.abs(z)))


def _sigmoid(z):
    return 1.0 / (1.0 + jnp.exp(-z))


def _rms(x, w):
    ms = jnp.mean(x * x, axis=-1, keepdims=True)
    return x * lax.rsqrt(ms + RMS_EPS) * w


def _row(ref, i):
    return ref[pl.ds(i, 1), :]


def _shift_down(x, prev_ref, n, first):
    rolled = pltpu.roll(x, n, 0)
    rows = lax.broadcasted_iota(jnp.int32, x.shape, 0)
    for q in range(n):
        halo = jnp.where(first, 0.0, _row(prev_ref, SUBLANES - n + q))
        rolled = jnp.where(rows == q, halo, rolled)
    return rolled


def _shift_up(x, next_ref, n, last):
    t = x.shape[0]
    rolled = pltpu.roll(x, t - n, 0)
    rows = lax.broadcasted_iota(jnp.int32, x.shape, 0)
    for q in range(n):
        halo = jnp.where(last, 0.0, _row(next_ref, q))
        rolled = jnp.where(rows == t - n + q, halo, rolled)
    return rolled


def _acc_out(ref, val, first):
    @pl.when(first)
    def _():
        ref[...] = val

    @pl.when(jnp.logical_not(first))
    def _():
        ref[...] += val


def _prep_fn(k, plm, w0, a0, kkw, kaw, wd, wi, wg):
    w_log = -_softplus(-(w0 + _mm(jnp.tanh(plm), wd))) - 0.5
    lw = -jnp.exp(w_log)
    a_g = _sigmoid(a0 + _mm(plm, wi))
    g = _mm(_sigmoid(plm), wg)
    kk = k * kkw
    kk = kk / jnp.maximum(jnp.sqrt(_segsum(kk * kk)), L2_EPS)
    k2 = k * (1.0 + (a_g - 1.0) * kaw)
    return lw, k2, -kk, kk * a_g, g


def _post_fn(y, r, k2, v, g, lnw, lnb, rk):
    mu = _segsum(y) * (1.0 / HEAD)
    yc = y - mu
    var = _segsum(yc * yc) * (1.0 / HEAD)
    yn = yc * lax.rsqrt(var + LNX_EPS) * lnw + lnb
    bonus = _segsum(r * k2 * rk) * v
    return (yn + bonus) * g


def _merge_fn(pga, pgb, ba, bb, ya, yb):
    return _sigmoid(pga + ba) * ya + _sigmoid(pgb + bb) * yb


_NN, _NT, _TN = ((2,), (1,)), ((2,), (2,)), ((1,), (1,))


def _dot3(a, b, dims):
    ah = a.astype(BF16)
    al = (a - ah.astype(F32)).astype(BF16)
    bh = b.astype(BF16)
    bl = (b - bh.astype(F32)).astype(BF16)
    dg = lambda p, q: lax.dot_general(p, q, (dims, ((0,), (0,))), preferred_element_type=F32)
    return dg(ah, bh) + (dg(ah, bl) + dg(al, bh))


@jax.custom_vjp
def _dnn(a, b):
    return _dot3(a, b, _NN)


@jax.custom_vjp
def _dnt(a, b):
    return _dot3(a, b, _NT)


@jax.custom_vjp
def _dtn(a, b):
    return _dot3(a, b, _TN)


_dnn.defvjp(lambda a, b: (_dnn(a, b), (a, b)), lambda res, ct: (_dnt(ct, res[1]), _dtn(res[0], ct)))
_dnt.defvjp(lambda a, b: (_dnt(a, b), (a, b)), lambda res, ct: (_dnn(ct, res[1]), _dtn(ct, res[0])))
_dtn.defvjp(lambda a, b: (_dtn(a, b), (a, b)), lambda res, ct: (_dnt(res[1], ct), _dnn(res[0], ct)))


@jax.custom_vjp
def _unit_lower_inverse(low):
    n = low.shape[-1]
    ri = lax.broadcasted_iota(jnp.int32, low.shape, 1)
    ci = lax.broadcasted_iota(jnp.int32, low.shape, 2)
    inv = (ri == ci).astype(F32) + low
    pw = low
    for _ in range(int(math.log2(n // 2)) - 1):
        pw = _dnn(pw, pw)
        inv = inv + _dnn(inv, pw)
    return inv


def _unit_lower_inverse_bwd(inv, ct):
    return (_dnt(_dtn(inv, ct), inv),)


_unit_lower_inverse.defvjp(lambda low: (_unit_lower_inverse(low),) * 2, _unit_lower_inverse_bwd)


def _chunk_fn(s, r, lw, k, v, a, b):
    np_, c = r.shape[0], r.shape[1]
    c2 = 2 * c
    ri = lax.broadcasted_iota(jnp.int32, (np_, c, c), 1)
    ci = lax.broadcasted_iota(jnp.int32, (np_, c, c), 2)
    tri = (ri >= ci).astype(F32)
    cum = _dnn(tri, lw)
    tot = jnp.sum(lw, axis=1, keepdims=True)
    g_in, g_inv, g_out = jnp.exp(cum), jnp.exp(-cum), jnp.exp(tot - cum)
    lane_head = lax.broadcasted_iota(jnp.int32, (1, 2, 1, LANES), 3) // HEAD
    which = lax.broadcasted_iota(jnp.int32, (1, 2, 1, LANES), 1)
    hmask = (lane_head == which).astype(F32)

    def st(x):
        return (x[:, None] * hmask).reshape(np_, c2, LANES)

    r2, a2 = st(r * g_in), st(a * jnp.exp(cum - lw))
    b2, k2, v2 = st(b * g_inv), st(k * g_inv), st(v)
    bo2, ko2 = st(b * g_out), st(k * g_out)
    r2i = lax.broadcasted_iota(jnp.int32, (np_, c2, c2), 1)
    c2i = lax.broadcasted_iota(jnp.int32, (np_, c2, c2), 2)
    same = (r2i >= c) == (c2i >= c)
    strict = jnp.logical_and(same, r2i > c2i)
    incl = jnp.logical_and(same, r2i >= c2i)
    lab = jnp.where(strict, _dnt(a2, b2), 0.0)
    lak = jnp.where(strict, _dnt(a2, k2), 0.0)
    mrb = jnp.where(incl, _dnt(r2, b2), 0.0)
    mrk = jnp.where(incl, _dnt(r2, k2), 0.0)
    x2 = _dnt(a2, s) + _dnn(lak, v2)
    u2 = _dnn(_unit_lower_inverse(lab), x2)
    y2 = _dnt(r2, s) + _dnn(mrb, u2) + _dnn(mrk, v2)
    y = jnp.sum(y2.reshape(np_, 2, c, LANES), axis=1)
    s_new = s * jnp.exp(tot) + _dtn(u2, bo2) + _dtn(v2, ko2)
    return y, s_new


def _norm_fwd(name, x, add, w, want_sum):
    t, d = x.shape
    tt = _tile(t, (128, 64, 32, 16, 8))
    row = pl.BlockSpec((tt, d), lambda i: (i, 0))
    par = pl.BlockSpec((1, d), lambda i: (0, 0))
    has_add = add is not None

    def body(*refs):
        x_ref = refs[0]
        add_ref = refs[1] if has_add else None
        w_ref = refs[1 + has_add]
        outs = refs[2 + has_add:]
        h = x_ref[...] + add_ref[...] if has_add else x_ref[...]
        if want_sum:
            outs[0][...] = h
        outs[-1][...] = _rms(h, w_ref[...]).astype(BF16)

    out_shape = ([jax.ShapeDtypeStruct((t, d), F32)] if want_sum else []) + [jax.ShapeDtypeStruct((t, d), BF16)]
    return pl.pallas_call(
        body, name=name, out_shape=out_shape, grid=(t // tt,),
        in_specs=[row] + ([row] if has_add else []) + [par],
        out_specs=[row] * len(out_shape),
        compiler_params=_cparams(("arbitrary",)),
    )(*([x] + ([add] if has_add else []) + [w]))


def _norm_bwd(name, xin, dy, dres, w):
    t, d = xin.shape
    tt = _tile(t, (128, 64, 32, 16, 8))
    row = pl.BlockSpec((tt, d), lambda i: (i, 0))
    par = pl.BlockSpec((1, d), lambda i: (0, 0))

    def body(x_ref, dy_ref, dres_ref, w_ref, dx_ref, dxb_ref, dw_ref):
        _, vjp = jax.vjp(_rms, x_ref[...], w_ref[...])
        dx, dw = vjp(dy_ref[...])
        dx = dx + dres_ref[...]
        dx_ref[...] = dx
        dxb_ref[...] = dx.astype(BF16)
        _acc_out(dw_ref, dw, pl.program_id(0) == 0)

    return pl.pallas_call(
        body, name=name,
        out_shape=[jax.ShapeDtypeStruct((t, d), F32), jax.ShapeDtypeStruct((t, d), BF16),
                   jax.ShapeDtypeStruct((1, d), F32)],
        grid=(t // tt,), in_specs=[row, row, row, par], out_specs=[row, row, par],
        compiler_params=_cparams(("arbitrary",)),
    )(xin, dy, dres, w)


def _final(name, h1, md, target, w):
    t, d = h1.shape
    tt = _tile(t, (128, 64, 32, 16, 8))
    row = pl.BlockSpec((tt, d), lambda i: (i, 0))
    par = pl.BlockSpec((1, d), lambda i: (0, 0))
    one = pl.BlockSpec((1, LANES), lambda i: (0, 0))

    def body(h1_ref, md_ref, tg_ref, w_ref, loss_ref, dh_ref, dhb_ref, dw_ref):
        tg = tg_ref[...]

        def f(h, wv):
            err = _rms(h, wv) - tg
            return 0.5 * jnp.sum(jnp.mean(err * err, axis=-1, keepdims=True), axis=0, keepdims=True)

        loss, vjp = jax.vjp(f, h1_ref[...] + md_ref[...], w_ref[...])
        dh, dw = vjp(jnp.ones((1, 1), F32))
        dh_ref[...] = dh
        dhb_ref[...] = dh.astype(BF16)
        first = pl.program_id(0) == 0
        _acc_out(dw_ref, dw, first)
        _acc_out(loss_ref, jnp.broadcast_to(loss, (1, LANES)), first)

    return pl.pallas_call(
        body, name=name,
        out_shape=[jax.ShapeDtypeStruct((1, LANES), F32), jax.ShapeDtypeStruct((t, d), F32),
                   jax.ShapeDtypeStruct((t, d), BF16), jax.ShapeDtypeStruct((1, d), F32)],
        grid=(t // tt,), in_specs=[row, row, row, par], out_specs=[one, row, row, par],
        compiler_params=_cparams(("arbitrary",)),
    )(h1, md, target, w)


def _halo_specs(tt, cb, nrow8, col_of):
    prev = pl.BlockSpec((SUBLANES, cb), lambda i, j: (jnp.maximum(i * (tt // SUBLANES) - 1, 0), col_of(j)))
    nxt = pl.BlockSpec((SUBLANES, cb), lambda i, j: (jnp.minimum((i + 1) * (tt // SUBLANES), nrow8 - 1), col_of(j)))
    return prev, nxt


def _mix_fwd(name, p_all, mu, width, cb):
    t = p_all.shape[0]
    tt = _tile(t, (256, 128, 64, 32, 16, 8))
    main = pl.BlockSpec((tt, cb), lambda i, j: (i, j))
    prev, _ = _halo_specs(tt, cb, t // SUBLANES, lambda j: j)
    par = pl.BlockSpec((1, cb), lambda i, j: (0, j))

    def body(p_ref, prev_ref, mu_ref, o_ref):
        p = p_ref[...]
        o_ref[...] = p + (_shift_down(p, prev_ref, 1, pl.program_id(0) == 0) - p) * mu_ref[...]

    return pl.pallas_call(
        body, name=name, out_shape=jax.ShapeDtypeStruct((t, width), F32),
        grid=(t // tt, width // cb), in_specs=[main, prev, par], out_specs=main,
        compiler_params=_cparams(("arbitrary", "arbitrary")),
    )(p_all, p_all, mu)


def _mix_bwd(name, dpm_list, p_all, col0, mu, cb):
    t, width = dpm_list[0].shape
    tt = _tile(t, (256, 128, 64, 32, 16, 8))
    n8 = t // SUBLANES
    nl = len(dpm_list)
    main = pl.BlockSpec((tt, cb), lambda j, i: (i, j))
    nxt = pl.BlockSpec((SUBLANES, cb), lambda j, i: (jnp.minimum((i + 1) * (tt // SUBLANES), n8 - 1), j))
    p_main = pl.BlockSpec((tt, cb), lambda j, i: (i, col0 + j))
    p_prev = pl.BlockSpec((SUBLANES, cb), lambda j, i: (jnp.maximum(i * (tt // SUBLANES) - 1, 0), col0 + j))
    par = pl.BlockSpec((1, cb), lambda j, i: (0, j))
    nt_ = t // tt

    def body(*refs):
        d_refs, dn_refs = refs[:nl], refs[nl:2 * nl]
        p_ref, pp_ref, mu_ref, dp_ref, dmu_ref, nx_scr = refs[2 * nl:]
        i = pl.program_id(1)
        dpm = d_refs[0][...]
        nx = dn_refs[0][...]
        for q in range(1, nl):
            dpm = dpm + d_refs[q][...]
            nx = nx + dn_refs[q][...]
        nx_scr[...] = nx
        mu_v = mu_ref[...]
        up = _shift_up(dpm, nx_scr, 1, i == nt_ - 1)
        dp_ref[...] = (dpm * (1.0 - mu_v) + up * mu_v).astype(BF16)
        p = p_ref[...]
        diff = _shift_down(p, pp_ref, 1, i == 0) - p
        _acc_out(dmu_ref, jnp.sum(dpm * diff, axis=0, keepdims=True), i == 0)

    return pl.pallas_call(
        body, name=name,
        out_shape=[jax.ShapeDtypeStruct((t, width), BF16), jax.ShapeDtypeStruct((1, width), F32)],
        grid=(width // cb, nt_),
        in_specs=[main] * nl + [nxt] * nl + [p_main, p_prev, par],
        out_specs=[main, par],
        scratch_shapes=[pltpu.VMEM((SUBLANES, cb), F32)],
        compiler_params=_cparams(("arbitrary", "arbitrary")),
    )(*dpm_list, *dpm_list, p_all, p_all, mu)


def _prep_fwd(name, pm, cfg, w0, a0, kkw, kaw, wd, wi, wg):
    t = pm.shape[0]
    dr, lp, cb = cfg["dr"], cfg["lp"], cfg["cb"]
    tt = _tile(t, (256, 128, 64, 32, 16, 8))
    nj = dr // cb
    kspec = pl.BlockSpec((tt, cb), lambda j, i: (i, nj + j))
    lspec = pl.BlockSpec((tt, lp), lambda j, i: (i, 3 * dr // lp))
    par = pl.BlockSpec((1, cb), lambda j, i: (0, j))
    wspec = pl.BlockSpec((lp, cb), lambda j, i: (0, j))
    out = pl.BlockSpec((tt, cb), lambda j, i: (i, j))

    def body(k_ref, l_ref, w0_ref, a0_ref, kk_ref, ka_ref, wd_ref, wi_ref, wg_ref, *outs):
        vals = _prep_fn(k_ref[...], l_ref[...], w0_ref[...], a0_ref[...], kk_ref[...], ka_ref[...],
                        wd_ref[...], wi_ref[...], wg_ref[...])
        for o_ref, val in zip(outs, vals):
            o_ref[...] = val

    return pl.pallas_call(
        body, name=name, out_shape=[jax.ShapeDtypeStruct((t, dr), F32)] * 5,
        grid=(nj, t // tt), in_specs=[kspec, lspec, par, par, par, par, wspec, wspec, wspec],
        out_specs=[out] * 5, compiler_params=_cparams(("arbitrary", "arbitrary")),
    )(pm, pm, w0, a0, kkw, kaw, wd, wi, wg)


def _prep_bwd(name, pm, cfg, w0, a0, kkw, kaw, wd, wi, wg, cts, dr_parts, dv_parts):
    t = pm.shape[0]
    dr, lp, cb = cfg["dr"], cfg["lp"], cfg["cb"]
    tt = _tile(t, (256, 128, 64, 32, 16, 8))
    nj = dr // cb
    kspec = pl.BlockSpec((tt, cb), lambda j, i: (i, nj + j))
    lspec = pl.BlockSpec((tt, lp), lambda j, i: (i, 3 * dr // lp))
    par = pl.BlockSpec((1, cb), lambda j, i: (0, j))
    wspec = pl.BlockSpec((lp, cb), lambda j, i: (0, j))
    blk = pl.BlockSpec((tt, cb), lambda j, i: (i, j))
    dpl_spec = pl.BlockSpec((None, tt, lp), lambda j, i: (j, i, 0))

    def body(k_ref, l_ref, w0_ref, a0_ref, kk_ref, ka_ref, wd_ref, wi_ref, wg_ref,
             dlw_ref, dk2a_ref, dk2b_ref, da_ref, db_ref, dg_ref, dr0_ref, dr1_ref, dv0_ref, dv1_ref,
             dpr_ref, dpk_ref, dpv_ref, dpl_ref, dw0_ref, da0_ref, dkk_ref, dka_ref, dwd_ref, dwi_ref, dwg_ref):
        _, vjp = jax.vjp(_prep_fn, k_ref[...], l_ref[...], w0_ref[...], a0_ref[...], kk_ref[...], ka_ref[...],
                         wd_ref[...], wi_ref[...], wg_ref[...])
        dk, dpl, dw0, da0, dkk, dka, dwd, dwi, dwg = vjp(
            (dlw_ref[...], dk2a_ref[...] + dk2b_ref[...], da_ref[...], db_ref[...], dg_ref[...]))
        dpr_ref[...] = dr0_ref[...] + dr1_ref[...]
        dpv_ref[...] = dv0_ref[...] + dv1_ref[...]
        dpk_ref[...] = dk
        dpl_ref[...] = dpl
        first = pl.program_id(1) == 0
        for ref, val in ((dw0_ref, dw0), (da0_ref, da0), (dkk_ref, dkk), (dka_ref, dka),
                         (dwd_ref, dwd), (dwi_ref, dwi), (dwg_ref, dwg)):
            _acc_out(ref, val, first)

    out_shape = ([jax.ShapeDtypeStruct((t, dr), F32)] * 3 + [jax.ShapeDtypeStruct((nj, t, lp), F32)]
                 + [jax.ShapeDtypeStruct((1, dr), F32)] * 4 + [jax.ShapeDtypeStruct((lp, dr), F32)] * 3)
    return pl.pallas_call(
        body, name=name, out_shape=out_shape, grid=(nj, t // tt),
        in_specs=[kspec, lspec, par, par, par, par, wspec, wspec, wspec] + [blk] * 10,
        out_specs=[blk] * 3 + [dpl_spec] + [par] * 4 + [wspec] * 3,
        compiler_params=_cparams(("arbitrary", "arbitrary")),
    )(pm, pm, w0, a0, kkw, kaw, wd, wi, wg, *cts, *dr_parts, *dv_parts)


def _post_specs(t, cfg):
    dr, cb = cfg["dr"], cfg["cb"]
    tt = _tile(t, (256, 128, 64, 32, 16, 8))
    nj = dr // cb
    blk = pl.BlockSpec((tt, cb), lambda j, i: (i, j))
    rspec = pl.BlockSpec((tt, cb), lambda j, i: (i, j))
    vspec = pl.BlockSpec((tt, cb), lambda j, i: (i, 2 * nj + j))
    par = pl.BlockSpec((1, cb), lambda j, i: (0, j))
    return tt, nj, blk, rspec, vspec, par


def _post_fwd(name, y, pm, k2, g, lnw, lnb, rk, cfg):
    t = y.shape[0]
    tt, nj, blk, rspec, vspec, par = _post_specs(t, cfg)

    def body(y_ref, r_ref, k_ref, v_ref, g_ref, lw_ref, lb_ref, rk_ref, o_ref):
        o_ref[...] = _post_fn(y_ref[...], r_ref[...], k_ref[...], v_ref[...], g_ref[...],
                              lw_ref[...], lb_ref[...], rk_ref[...]).astype(BF16)

    return pl.pallas_call(
        body, name=name, out_shape=jax.ShapeDtypeStruct((t, cfg["dr"]), BF16), grid=(nj, t // tt),
        in_specs=[blk, rspec, blk, vspec, blk, par, par, par], out_specs=blk,
        compiler_params=_cparams(("arbitrary", "arbitrary")),
    )(y, pm, k2, pm, g, lnw, lnb, rk)


def _post_bwd(name, y, pm, k2, g, lnw, lnb, rk, dout, cfg):
    t = y.shape[0]
    tt, nj, blk, rspec, vspec, par = _post_specs(t, cfg)

    def body(y_ref, r_ref, k_ref, v_ref, g_ref, lw_ref, lb_ref, rk_ref, do_ref,
             dy_ref, dr_ref, dk_ref, dv_ref, dg_ref, dlw_ref, dlb_ref, drk_ref):
        _, vjp = jax.vjp(_post_fn, y_ref[...], r_ref[...], k_ref[...], v_ref[...], g_ref[...],
                         lw_ref[...], lb_ref[...], rk_ref[...])
        dy, dr, dk, dv, dg, dlw, dlb, drk = vjp(do_ref[...])
        for ref, val in ((dy_ref, dy), (dr_ref, dr), (dk_ref, dk), (dv_ref, dv), (dg_ref, dg)):
            ref[...] = val
        first = pl.program_id(1) == 0
        for ref, val in ((dlw_ref, dlw), (dlb_ref, dlb), (drk_ref, drk)):
            _acc_out(ref, val, first)

    dr = cfg["dr"]
    return pl.pallas_call(
        body, name=name,
        out_shape=[jax.ShapeDtypeStruct((t, dr), F32)] * 5 + [jax.ShapeDtypeStruct((1, dr), F32)] * 3,
        grid=(nj, t // tt),
        in_specs=[blk, rspec, blk, vspec, blk, par, par, par, blk],
        out_specs=[blk] * 5 + [par] * 3,
        compiler_params=_cparams(("arbitrary", "arbitrary")),
    )(y, pm, k2, pm, g, lnw, lnb, rk, dout)


def _conv_specs(t, cfg):
    dc, cb = cfg["dc"], cfg["cb"]
    tt = _tile(t, (256, 128, 64, 32, 16, 8))
    nj = dc // cb
    c0 = cfg["off_conv"] // cb
    n8 = t // SUBLANES

    def sect(s):
        col = lambda j: c0 + s * nj + j
        main = pl.BlockSpec((tt, cb), lambda j, i: (i, col(j)))
        prev = pl.BlockSpec((SUBLANES, cb), lambda j, i: (jnp.maximum(i * (tt // SUBLANES) - 1, 0), col(j)))
        nxt = pl.BlockSpec((SUBLANES, cb), lambda j, i: (jnp.minimum((i + 1) * (tt // SUBLANES), n8 - 1), col(j)))
        return main, prev, nxt

    blk = pl.BlockSpec((tt, cb), lambda j, i: (i, j))
    wspec = pl.BlockSpec((SUBLANES, cb), lambda j, i: (0, j))
    return tt, nj, n8, sect, blk, wspec


def _conv_fwd(name, p_all, cw8, cfg):
    t = p_all.shape[0]
    tt, nj, n8, sect, blk, wspec = _conv_specs(t, cfg)
    (bm, _, _), (cm, cp, _), (um, up, _) = sect(0), sect(1), sect(2)

    def body(b_ref, c_ref, cp_ref, u_ref, up_ref, w_ref, o_ref, zp_scr):
        first = pl.program_id(1) == 0
        z = c_ref[...] * u_ref[...]
        zp_scr[...] = cp_ref[...] * up_ref[...]
        o = _row(w_ref, 2) * z + _row(w_ref, 1) * _shift_down(z, zp_scr, 1, first) \
            + _row(w_ref, 0) * _shift_down(z, zp_scr, 2, first)
        o_ref[...] = (b_ref[...] * o).astype(BF16)

    return pl.pallas_call(
        body, name=name, out_shape=jax.ShapeDtypeStruct((t, cfg["dc"]), BF16), grid=(nj, t // tt),
        in_specs=[bm, cm, cp, um, up, wspec], out_specs=blk,
        scratch_shapes=[pltpu.VMEM((SUBLANES, blk.block_shape[1]), F32)],
        compiler_params=_cparams(("arbitrary", "arbitrary")),
    )(p_all, p_all, p_all, p_all, p_all, cw8)


def _conv_bwd(name, p_all, cw8, dyb, cfg):
    t = p_all.shape[0]
    tt, nj, n8, sect, blk, wspec = _conv_specs(t, cfg)
    (bm, _, bn), (cm, cp, _), (um, up, _) = sect(0), sect(1), sect(2)
    cb = blk.block_shape[1]
    dnxt = pl.BlockSpec((SUBLANES, cb), lambda j, i: (jnp.minimum((i + 1) * (tt // SUBLANES), n8 - 1), j))
    nt_ = t // tt

    def body(b_ref, bn_ref, c_ref, cp_ref, u_ref, up_ref, w_ref, d_ref, dn_ref,
             db_ref, dc_ref, du_ref, dw_ref, zp_scr, don_scr):
        i = pl.program_id(1)
        first, last = i == 0, i == nt_ - 1
        c, u, b, dy = c_ref[...], u_ref[...], b_ref[...], d_ref[...]
        z = c * u
        zp_scr[...] = cp_ref[...] * up_ref[...]
        z1 = _shift_down(z, zp_scr, 1, first)
        z2 = _shift_down(z, zp_scr, 2, first)
        w0, w1, w2 = _row(w_ref, 0), _row(w_ref, 1), _row(w_ref, 2)
        o = w2 * z + w1 * z1 + w0 * z2
        do = dy * b
        don_scr[...] = dn_ref[...] * bn_ref[...]
        dz = w2 * do + w1 * _shift_up(do, don_scr, 1, last) + w0 * _shift_up(do, don_scr, 2, last)
        db_ref[...] = (dy * o).astype(BF16)
        dc_ref[...] = (dz * u).astype(BF16)
        du_ref[...] = (dz * c).astype(BF16)
        rows = lax.broadcasted_iota(jnp.int32, (SUBLANES, cb), 0)
        s0 = jnp.sum(do * z2, axis=0, keepdims=True)
        s1 = jnp.sum(do * z1, axis=0, keepdims=True)
        s2 = jnp.sum(do * z, axis=0, keepdims=True)
        dw = jnp.where(rows == 0, s0, jnp.where(rows == 1, s1, jnp.where(rows == 2, s2, 0.0)))
        _acc_out(dw_ref, dw, first)

    dc = cfg["dc"]
    return pl.pallas_call(
        body, name=name,
        out_shape=[jax.ShapeDtypeStruct((t, dc), BF16)] * 3 + [jax.ShapeDtypeStruct((SUBLANES, dc), F32)],
        grid=(nj, nt_),
        in_specs=[bm, bn, cm, cp, um, up, wspec, blk, dnxt],
        out_specs=[blk] * 3 + [wspec],
        scratch_shapes=[pltpu.VMEM((SUBLANES, cb), F32), pltpu.VMEM((SUBLANES, cb), F32)],
        compiler_params=_cparams(("arbitrary", "arbitrary")),
    )(p_all, p_all, p_all, p_all, p_all, p_all, cw8, dyb, dyb)


def _merge_specs(t, cfg):
    d, cb = cfg["d"], cfg["cb"]
    tt = _tile(t, (256, 128, 64, 32, 16, 8))
    nj = d // cb
    g0 = cfg["off_gate"] // cb
    ga = pl.BlockSpec((tt, cb), lambda j, i: (i, g0 + j))
    gb = pl.BlockSpec((tt, cb), lambda j, i: (i, g0 + nj + j))
    ba = pl.BlockSpec((1, cb), lambda j, i: (0, j))
    bb = pl.BlockSpec((1, cb), lambda j, i: (0, nj + j))
    blk = pl.BlockSpec((tt, cb), lambda j, i: (i, j))
    return tt, nj, ga, gb, ba, bb, blk


def _merge_fwd(name, p_all, bias, ya, yb, cfg):
    t = p_all.shape[0]
    tt, nj, ga, gb, ba, bb, blk = _merge_specs(t, cfg)

    def body(ga_ref, gb_ref, ba_ref, bb_ref, ya_ref, yb_ref, o_ref):
        o_ref[...] = _merge_fn(ga_ref[...], gb_ref[...], ba_ref[...], bb_ref[...],
                               ya_ref[...], yb_ref[...]).astype(BF16)

    return pl.pallas_call(
        body, name=name, out_shape=jax.ShapeDtypeStruct((t, cfg["d"]), BF16), grid=(nj, t // tt),
        in_specs=[ga, gb, ba, bb, blk, blk], out_specs=blk,
        compiler_params=_cparams(("arbitrary", "arbitrary")),
    )(p_all, p_all, bias, bias, ya, yb)


def _merge_bwd(name, p_all, bias, ya, yb, dm, cfg):
    t = p_all.shape[0]
    tt, nj, ga, gb, ba, bb, blk = _merge_specs(t, cfg)

    def body(ga_ref, gb_ref, ba_ref, bb_ref, ya_ref, yb_ref, dm_ref,
             dga_ref, dgb_ref, dya_ref, dyb_ref, dba_ref, dbb_ref):
        _, vjp = jax.vjp(_merge_fn, ga_ref[...], gb_ref[...], ba_ref[...], bb_ref[...], ya_ref[...], yb_ref[...])
        dga, dgb, dba, dbb, dya, dyb = vjp(dm_ref[...])
        for ref, val in ((dga_ref, dga), (dgb_ref, dgb), (dya_ref, dya), (dyb_ref, dyb)):
            ref[...] = val.astype(BF16)
        first = pl.program_id(1) == 0
        _acc_out(dba_ref, dba, first)
        _acc_out(dbb_ref, dbb, first)

    d = cfg["d"]
    par = pl.BlockSpec((1, blk.block_shape[1]), lambda j, i: (0, j))
    return pl.pallas_call(
        body, name=name,
        out_shape=[jax.ShapeDtypeStruct((t, d), BF16)] * 4 + [jax.ShapeDtypeStruct((1, d), F32)] * 2,
        grid=(nj, t // tt),
        in_specs=[ga, gb, ba, bb, blk, blk, blk], out_specs=[blk] * 4 + [par] * 2,
        compiler_params=_cparams(("arbitrary", "arbitrary")),
    )(p_all, p_all, bias, bias, ya, yb, dm)


PAIRS = 8


def _pair_stack(ref, pairs):
    return jnp.stack([ref[:, p * LANES:(p + 1) * LANES] for p in range(pairs)])


def _pair_store(ref, val):
    for p in range(val.shape[0]):
        ref[:, p * LANES:(p + 1) * LANES] = val[p]


def _rec_specs(t, cfg, rev):
    dr = cfg["dr"]
    nc = t // CHUNK
    hp = dr // LANES
    pairs = _tile(hp, (PAIRS, 2, 1))
    ng = hp // pairs
    w = LANES * pairs
    ch = (lambda c: nc - 1 - c) if rev else (lambda c: c)
    slab = pl.BlockSpec((CHUNK, w), lambda h, c: (ch(c), h))
    vspec = pl.BlockSpec((CHUNK, w), lambda h, c: (ch(c), 2 * ng + h))
    sspec = pl.BlockSpec((None, pairs, LANES, LANES), lambda h, c: (ch(c), h, 0, 0))
    first = lambda: jnp.logical_and(pl.program_id(0) == 0, pl.program_id(1) == 0)
    last = lambda: jnp.logical_and(pl.program_id(0) == ng - 1, pl.program_id(1) == nc - 1)
    return nc, hp, pairs, ng, slab, vspec, sspec, first, last


def _rec_fwd(name, pm, lw, k2, a, b, cfg, comm=None):
    t = pm.shape[0]
    nc, hp, pairs, ng, slab, vspec, sspec, first, last = _rec_specs(t, cfg, False)

    def body(r_ref, lw_ref, k_ref, v_ref, a_ref, b_ref, y_ref, s_ref, s_scr):
        @pl.when(pl.program_id(1) == 0)
        def _():
            s_scr[...] = jnp.zeros_like(s_scr)

        s = s_scr[...]
        s_ref[...] = s
        y, s_new = _chunk_fn(s, *[_pair_stack(ref, pairs) for ref in (r_ref, lw_ref, k_ref, v_ref, a_ref, b_ref)])
        _pair_store(y_ref, y)
        s_scr[...] = s_new

    return _hosted_call(
        body, name, comm, first, last, args=[pm, lw, k2, pm, a, b],
        in_specs=[slab, slab, slab, vspec, slab, slab],
        out_shape=[jax.ShapeDtypeStruct((t, cfg["dr"]), F32), jax.ShapeDtypeStruct((nc, hp, LANES, LANES), F32)],
        out_specs=[slab, sspec], scratch=[pltpu.VMEM((pairs, LANES, LANES), F32)], grid=(ng, nc),
        sem=("arbitrary", "arbitrary"))


def _rec_bwd(name, pm, lw, k2, a, b, s_chk, dy, cfg, comm=None):
    t = pm.shape[0]
    nc, hp, pairs, ng, slab, vspec, sspec, first, last = _rec_specs(t, cfg, True)

    def body(r_ref, lw_ref, k_ref, v_ref, a_ref, b_ref, s_ref, dy_ref,
             dr_ref, dlw_ref, dk_ref, dv_ref, da_ref, db_ref, ds_scr):
        @pl.when(pl.program_id(1) == 0)
        def _():
            ds_scr[...] = jnp.zeros_like(ds_scr)

        _, vjp = jax.vjp(_chunk_fn, s_ref[...],
                         *[_pair_stack(ref, pairs) for ref in (r_ref, lw_ref, k_ref, v_ref, a_ref, b_ref)])
        ds, dr, dlw, dk, dv, da, db = vjp((_pair_stack(dy_ref, pairs), ds_scr[...]))
        ds_scr[...] = ds
        for ref, val in ((dr_ref, dr), (dlw_ref, dlw), (dk_ref, dk), (dv_ref, dv), (da_ref, da), (db_ref, db)):
            _pair_store(ref, val)

    return _hosted_call(
        body, name, comm, first, last, args=[pm, lw, k2, pm, a, b, s_chk, dy],
        in_specs=[slab, slab, slab, vspec, slab, slab, sspec, slab],
        out_shape=[jax.ShapeDtypeStruct((t, cfg["dr"]), F32)] * 6, out_specs=[slab] * 6,
        scratch=[pltpu.VMEM((pairs, LANES, LANES), F32)], grid=(ng, nc), sem=("arbitrary", "arbitrary"))


def _comm_call(name, comm):
    n = comm.n
    hbm = pl.BlockSpec(memory_space=pl.ANY)

    def body(*refs):
        comm.start(refs[:n], refs[n:2 * n], refs[2 * n:])
        comm.wait(refs[:n], refs[n:2 * n], refs[2 * n:])

    return pl.pallas_call(body, name=name, out_shape=comm.out_shape, in_specs=[hbm] * n, out_specs=[hbm] * n,
                          scratch_shapes=comm.scratch)(*comm.arrs)


def _all_reduce_small(name, v):
    rows = v.shape[0]
    vm = pl.BlockSpec(memory_space=pltpu.VMEM)

    def body(x_ref, out_ref, buf, send_sems, recv_sems):
        x, y, c = _my_pos()
        me, sibling = (x, y, c), (x, y, 1 - c)
        chips = [(1 - x, y), (x, 1 - y), (1 - x, 1 - y)]

        def copy(k, block, to, src=None):
            px, py, pc = block
            dst = buf.at[4 * px + 2 * py + pc]
            return pltpu.make_async_remote_copy(
                src_ref=dst if src is None else src, dst_ref=dst,
                send_sem=send_sems.at[k], recv_sem=recv_sems.at[k], device_id=to, device_id_type=MESH)

        buf[4 * x + 2 * y + c] = x_ref[...]
        first = [copy(0, me, sibling, src=x_ref)]
        first += [copy(1 + j, me, (*chip, c), src=x_ref) for j, chip in enumerate(chips)]
        for cp in first:
            cp.start()
        passed = [copy(4 + j, (*chip, c), sibling) for j, chip in enumerate(chips)]
        for j, chip in enumerate(chips):
            copy(1 + j, (*chip, c), me).wait_recv()
            passed[j].start()
        copy(0, sibling, me).wait_recv()
        for j, chip in enumerate(chips):
            copy(4 + j, (*chip, 1 - c), me).wait_recv()
        for cp in first + passed:
            cp.wait_send()
        acc = buf[0]
        for d in range(1, N_DEV):
            acc = acc + buf[d]
        out_ref[...] = acc

    return pl.pallas_call(
        body, name=name, out_shape=jax.ShapeDtypeStruct(v.shape, F32),
        in_specs=[vm], out_specs=vm,
        scratch_shapes=[pltpu.VMEM((N_DEV, rows, LANES), F32), pltpu.SemaphoreType.DMA((7,)),
                        pltpu.SemaphoreType.DMA((7,))],
    )(v)


def _pair_sum(name, slabs, got, core):
    _, rows, cols = slabs.shape
    nq = got.shape[0]
    rb = _tile(rows, (256, 128, 64, 32, 16, 8))
    mine = pl.BlockSpec((None, rb, cols), lambda q, j, c_ref: (2 * q + c_ref[0], j, 0))
    blk = pl.BlockSpec((None, rb, cols), lambda q, j, c_ref: (q, j, 0))

    def body(c_ref, a_ref, b_ref, o_ref):
        o_ref[...] = (a_ref[...].astype(F32) + b_ref[...].astype(F32)).astype(o_ref.dtype)

    return pl.pallas_call(
        body, name=name, out_shape=jax.ShapeDtypeStruct(got.shape, got.dtype),
        grid_spec=pltpu.PrefetchScalarGridSpec(num_scalar_prefetch=1, grid=(nq, rows // rb),
                                               in_specs=[mine, blk], out_specs=blk),
        compiler_params=_cparams(("arbitrary", "arbitrary")))(core, slabs, got)


def _adamw(name, w, m, v, g_own, g_recv=None):
    rows, cols = w.shape
    nr = g_recv.shape[0] if g_recv is not None else 0
    per_el = 4 * 3 + g_own.dtype.itemsize + (nr * g_recv.dtype.itemsize if nr else 0) + 16
    rb = SUBLANES * 2
    while rb * 2 <= rows and rows % (rb * 2) == 0 and rb * 2 * cols * per_el * 2 <= VMEM_LIMIT // 2:
        rb *= 2
    if rows % rb:
        rb = rows
    blk = pl.BlockSpec((rb, cols), lambda i: (i, 0))
    rblk = pl.BlockSpec((max(nr, 1), rb, cols), lambda i: (0, i, 0))
    has_r = g_recv is not None
    bc1 = 1.0 - ADAM_B1 ** ADAM_STEP
    bc2 = 1.0 - ADAM_B2 ** ADAM_STEP

    def body(*refs):
        w_ref, m_ref, v_ref, go_ref = refs[:4]
        gr_ref = refs[4] if has_r else None
        g_out, d_out, m_out, v_out = refs[4 + has_r:]
        g = go_ref[...].astype(F32)
        if has_r:
            for r in range(nr):
                g = g + gr_ref[r].astype(F32)
        mn = ADAM_B1 * m_ref[...] + (1.0 - ADAM_B1) * g
        vn = ADAM_B2 * v_ref[...] + (1.0 - ADAM_B2) * (g * g)
        m_hat = mn / bc1
        v_hat = vn / bc2
        g_out[...] = g
        d_out[...] = -ADAM_LR * (m_hat / (jnp.sqrt(v_hat) + ADAM_EPS) + ADAM_WD * w_ref[...])
        m_out[...] = mn
        v_out[...] = vn

    return pl.pallas_call(
        body, name=name, out_shape=[jax.ShapeDtypeStruct((rows, cols), F32)] * 4, grid=(rows // rb,),
        in_specs=[blk] * 4 + ([rblk] if has_r else []), out_specs=[blk] * 4,
        compiler_params=_cparams(("arbitrary",)),
    )(*([w, m, v, g_own] + ([g_recv] if has_r else [])))


def _round_up(n, q):
    return (n + q - 1) // q * q


def _cols(a8):
    return jnp.transpose(a8, (1, 0, 2)).reshape(a8.shape[1], -1)


def _col_slabs(a):
    r_, c_ = a.shape
    return jnp.transpose(a.reshape(r_, N_DEV, c_ // N_DEV), (1, 0, 2))


_MID = ("w_out_a", "w_out_b", "w_out", "w_mlp_up", "w_mlp_down")


def _local_step(x, target, wts, shards, cfg):
    dr, dc, d, lp, cb = cfg["dr"], cfg["dc"], cfg["d"], cfg["lp"], cfg["cb"]
    dff = shards["w_mlp_down"].shape[0] * N_DEV
    wmix = 3 * dr + lp
    (xn,) = _norm_fwd("norm_mix_fwd", x, None, wts["norm_mix_w"], False)
    (p_all,), (g_oa, g_ob, g_o) = _matmul(
        "mm_in", xn, wts["w_all"], "nn", [F32],
        comm=_Comm("gather", [shards["w_out_a"], shards["w_out_b"], shards["w_out"]]))
    w_out_a, w_out_b, w_out = _cols(g_oa), _cols(g_ob), g_o.reshape(d, d)
    pm = _mix_fwd("mix_fwd", p_all, wts["mu_pad"], wmix, cb)
    prep_w = (wts["w0"], wts["a0"], wts["k_k"], wts["k_a"], wts["wd"], wts["wi"], wts["wg"])
    lw, k2, a_in, b_in, g = _prep_fwd("prep_fwd", pm, cfg, *prep_w)
    (y_raw, s_chk), (g_u,) = _rec_fwd(
        "rec_fwd", pm, lw, k2, a_in, b_in, cfg, comm=_Comm("gather", [shards["w_mlp_up"]]))
    w_up = _cols(g_u)
    post_w = (wts["lnx_w"], wts["lnx_b"], wts["r_k"])
    ya_in = _post_fwd("post_fwd", y_raw, pm, k2, g, *post_w, cfg)
    (ya,) = _matmul("mm_out_a", ya_in, w_out_a, "nn", [F32])
    yb_in = _conv_fwd("conv_fwd", p_all, wts["conv_w8"], cfg)
    (yb,) = _matmul("mm_out_b", yb_in, w_out_b, "nn", [F32])
    mg = _merge_fwd("merge_fwd", p_all, wts["gate_bias"], ya, yb, cfg)
    (mo,) = _matmul("mm_out", mg, w_out, "nn", [F32])
    h1, hn = _norm_fwd("norm_mlp_fwd", x, mo, wts["norm_mlp_w"], True)
    (u, act), (g_d,) = _matmul("mm_up", hn, w_up, "nn", [F32, BF16],
                               epi=lambda r: (r, jnp.square(jnp.maximum(r, 0.0))),
                               comm=_Comm("gather", [shards["w_mlp_down"]]))
    w_down = g_d.reshape(dff, d)
    (md,) = _matmul("mm_down", act, w_down, "nn", [F32])
    loss, dh2, dh2b, g_norm_final = _final("final", h1, md, target, wts["norm_final_w"])
    (du,) = _matmul("mm_down_dx", dh2b, w_down, "nt", [BF16],
                    epi=lambda r, uu: (r * (2.0 * jnp.maximum(uu, 0.0)),), extras=(u,))
    (g_down,) = _matmul("mm_down_dw", act, dh2b, "tn", [BF16])
    (dhn,) = _matmul("mm_up_dx", du, w_up, "nt", [F32])
    (g_up,) = _matmul("mm_up_dw", hn, du, "tn", [BF16], out_slabs=True)
    dh1, dh1b, g_norm_mlp = _norm_bwd("norm_mlp_bwd", h1, dhn, dh2, wts["norm_mlp_w"])
    (dmg,) = _matmul("mm_out_dx", dh1b, w_out, "nt", [F32])
    (g_out,) = _matmul("mm_out_dw", mg, dh1b, "tn", [BF16])
    dpga, dpgb, dya, dyb, dba, dbb = _merge_bwd("merge_bwd", p_all, wts["gate_bias"], ya, yb, dmg, cfg)
    (dya_in,) = _matmul("mm_out_a_dx", dya, w_out_a, "nt", [F32])
    (g_out_a,) = _matmul("mm_out_a_dw", ya_in, dya, "tn", [BF16], out_slabs=True)
    (dyb_in,) = _matmul("mm_out_b_dx", dyb, w_out_b, "nt", [F32])
    (g_out_b,) = _matmul("mm_out_b_dw", yb_in, dyb, "tn", [BF16], out_slabs=True)
    dpb, dpc, dpu, g_conv8 = _conv_bwd("conv_bwd", p_all, wts["conv_w8"], dyb_in, cfg)
    dy_raw, dr_post, dk_post, dv_post, dg, g_lnw, g_lnb, g_rk = _post_bwd(
        "post_bwd", y_raw, pm, k2, g, *post_w, dya_in, cfg)
    mid = dict(w_out_a=g_out_a, w_out_b=g_out_b, w_out=g_out.reshape(N_DEV, d // N_DEV, d),
               w_mlp_up=g_up, w_mlp_down=g_down.reshape(N_DEV, dff // N_DEV, d))
    (dr_rec, dlw, dk_rec, dv_rec, da_in, db_in), (r_up, r_down) = _rec_bwd(
        "rec_bwd", pm, lw, k2, a_in, b_in, s_chk, dy_raw, cfg,
        comm=_Comm("exchange", [mid["w_mlp_up"], mid["w_mlp_down"]]))
    (dpm_r, dpm_k, dpm_v, dpl, g_w0, g_a0, g_kk, g_ka, g_wd, g_wi, g_wg) = _prep_bwd(
        "prep_bwd", pm, cfg, *prep_w, (dlw, dk_rec, dk_post, da_in, db_in, dg),
        (dr_rec, dr_post), (dv_rec, dv_post))
    mu = wts["mu_pad"]
    nb = dr // cb
    dps, dmus = [], []
    for s, dpm_s in enumerate((dpm_r, dpm_k, dpm_v)):
        dp_s, dmu_s = _mix_bwd("mix_bwd_%d" % s, [dpm_s], p_all, s * nb, mu[:, s * dr:(s + 1) * dr], cb)
        dps.append(dp_s)
        dmus.append(dmu_s)
    dp_l, dmu_l = _mix_bwd("mix_bwd_l", [dpl[j] for j in range(nb)], p_all, 3 * nb, mu[:, 3 * dr:], min(cb, lp))
    tail = [jnp.zeros((x.shape[0], cfg["wall"] - cfg["used"]), BF16)] if cfg["wall"] > cfg["used"] else []
    dp_all = jnp.concatenate(dps + [dp_l, dpb, dpc, dpu, dpga, dpgb] + tail, axis=1)
    ld, li, lora = cfg["ld"], cfg["li"], cfg["lora"]
    g_small = jnp.concatenate([g_wd[:ld], g_wi[ld:ld + li], g_wg[ld + li:lora], g_conv8[:3]], axis=0)
    g_small = jnp.pad(g_small, ((0, cfg["small_rows"] - g_small.shape[0]), (0, 0)))
    mid["small"] = _col_slabs(g_small)
    (g_all,), (r_oa, r_ob, r_o, r_small) = _matmul(
        "mm_in_dw", xn, dp_all, "tn", [BF16],
        comm=_Comm("exchange", [mid["w_out_a"], mid["w_out_b"], mid["w_out"], mid["small"]]))
    g_in = jnp.concatenate([g_all[:, :3 * dr + lora], g_all[:, 3 * dr + lp:cfg["used"]]], axis=1)
    in_slabs = _col_slabs(g_in)
    (got,) = _comm_call("pair_exchange", _Comm("pair", [in_slabs]))
    chip_sum = _pair_sum("pair_sum", in_slabs, got, lax.axis_index("c").astype(jnp.int32).reshape(1))
    (dxn,), (r_in,) = _matmul(
        "mm_in_dx", dp_all, wts["w_all"], "nt", [F32], comm=_Comm("chips", [chip_sum]))
    my_chip = 2 * lax.axis_index("x") + lax.axis_index("y")
    mid["w_in_own"] = lax.dynamic_index_in_dim(chip_sum, my_chip, axis=0, keepdims=False)
    mid_recv = dict(w_out_a=r_oa, w_out_b=r_ob, w_out=r_o, w_mlp_up=r_up, w_mlp_down=r_down, w_in=r_in,
                    small=r_small)
    grad_x, _, g_norm_mix = _norm_bwd("norm_mix_bwd", x, dxn, dh1, wts["norm_mix_w"])
    grads = dict(
        norm_mix_w=g_norm_mix, gate_bias=jnp.concatenate([dba, dbb], axis=1),
        mu_pad=jnp.concatenate(dmus + [dmu_l], axis=1), w0=g_w0, a0=g_a0, k_k=g_kk, k_a=g_ka,
        r_k=g_rk, lnx_w=g_lnw, lnx_b=g_lnb, norm_mlp_w=g_norm_mlp, norm_final_w=g_norm_final)
    return loss, grad_x, grads, mid, mid_recv


_SMALL = ("norm_mix_w", "gate_bias", "shift_mu", "w0", "a0", "k_k", "k_a", "r_k", "lnx_w", "lnx_b",
          "norm_mlp_w", "norm_final_w")
_ORDER = ("norm_mix_w", "w_in", "gate_bias", "shift_mu", "w0", "w_decay_up", "a0", "w_iclr_up", "w_gate_up",
          "k_k", "k_a", "r_k", "lnx_w", "lnx_b", "w_out_a", "conv_w", "w_out_b", "w_out", "norm_mlp_w",
          "w_mlp_up", "w_mlp_down", "norm_final_w")


def _step(x, target, w, m, v):
    t, d = x.shape[1], x.shape[2]
    dr = w["w0"].shape[-1]
    ld, li, lg = w["w_decay_up"].shape[1], w["w_iclr_up"].shape[1], w["w_gate_up"].shape[1]
    lora = ld + li + lg
    lp = _round_up(lora, LANES)
    dc = w["conv_w"].shape[-1] * N_DEV
    cb = math.gcd(math.gcd(lp, dr), 512)
    used = 3 * dr + lp + 3 * dc + 2 * d
    wall = _round_up(used, 1024 if used > MAX_FULL_K else LANES)
    small_rows = ld + li + lg + 3
    cfg = dict(d=d, dr=dr, dc=dc, lp=lp, cb=cb, off_conv=3 * dr + lp, off_gate=3 * dr + lp + 3 * dc, used=used,
               wall=wall, ld=ld, li=li, lora=lora, small_rows=_round_up(small_rows, SUBLANES))
    x2, tg2 = x[0], target[0]

    small_sh = jnp.concatenate([w["w_decay_up"][0], w["w_iclr_up"][0], w["w_gate_up"][0], w["conv_w"][0]], axis=0)
    small_sh = jnp.pad(small_sh, ((0, _round_up(small_rows, SUBLANES) - small_rows), (0, 0)))
    big = ("w_in",) + _MID
    g_in8, gsm = _comm_call("gather_weights", _Comm("gather", [w["w_in"][0].astype(BF16), small_sh]))
    shards = {n: w[n][0].astype(BF16) for n in _MID}
    w_in = _cols(g_in8)
    zpad = jnp.zeros((d, lp - lora), BF16)
    w_all = jnp.concatenate([w_in[:, :3 * dr + lora], zpad, w_in[:, 3 * dr + lora:],
                             jnp.zeros((d, wall - used), BF16)], axis=1)
    sm = _cols(gsm)
    lora_full = sm[:lora]

    def lora_pad(lo, hi):
        rows = lax.broadcasted_iota(jnp.int32, (lp, 1), 0)
        full = jnp.pad(lora_full, ((0, lp - lora), (0, 0)))
        return jnp.where(jnp.logical_and(rows >= lo, rows < hi), full, 0.0)

    conv_w8 = jnp.pad(sm[lora:lora + 3], ((0, SUBLANES - 3), (0, 0)))
    mu_pad = jnp.pad(w["shift_mu"], ((0, 0), (0, lp - lora)))
    wts = dict(
        w_all=w_all, wd=lora_pad(0, ld), wi=lora_pad(ld, ld + li), wg=lora_pad(ld + li, lora), conv_w8=conv_w8,
        mu_pad=mu_pad, norm_mix_w=w["norm_mix_w"], gate_bias=w["gate_bias"], w0=w["w0"], a0=w["a0"],
        k_k=w["k_k"], k_a=w["k_a"], r_k=w["r_k"].reshape(1, dr), lnx_w=w["lnx_w"], lnx_b=w["lnx_b"],
        norm_mlp_w=w["norm_mlp_w"], norm_final_w=w["norm_final_w"].reshape(1, d))

    loss, grad_x, gr, slabs, received = _local_step(x2, tg2, wts, shards, cfg)

    x_i, y_i, c_i = _my_pos()
    me = 4 * x_i + 2 * y_i + c_i
    own = {n: lax.dynamic_index_in_dim(slabs[n], me, axis=0, keepdims=False) for n in _MID + ("small",)}
    own["w_in"] = slabs["w_in_own"]

    small_g = dict(norm_mix_w=gr["norm_mix_w"], gate_bias=gr["gate_bias"], shift_mu=gr["mu_pad"][:, :3 * dr + lora],
                   w0=gr["w0"], a0=gr["a0"], k_k=gr["k_k"], k_a=gr["k_a"], r_k=gr["r_k"], lnx_w=gr["lnx_w"],
                   lnx_b=gr["lnx_b"], norm_mlp_w=gr["norm_mlp_w"], norm_final_w=gr["norm_final_w"])
    sizes = [small_g[n].size for n in _SMALL]
    total = sum(sizes) + 1
    prow = _round_up(total, LANES * SUBLANES) // LANES

    def pack(parts):
        flat = jnp.concatenate([p.reshape(-1) for p in parts])
        return jnp.pad(flat, (0, prow * LANES - flat.size)).reshape(prow, LANES)

    g_packed = _all_reduce_small("reduce_small", pack([small_g[n] for n in _SMALL] + [loss[0, :1]]))
    one = jnp.zeros((1,), F32)
    packed = [pack([d_[n] for n in _SMALL] + [one]) for d_ in (w, m, v)]
    sm_out = _adamw("adamw_small", *packed, g_packed)
    loss_out = g_packed.reshape(-1)[total - 1]

    def unpack(flat2d):
        flat = flat2d.reshape(-1)
        out, o = {}, 0
        for n, s in zip(_SMALL, sizes):
            out[n] = flat[o:o + s].reshape(w[n].shape)
            o += s
        return out

    res = [unpack(a) for a in sm_out]

    def shard2d(a):
        return a.reshape(-1, a.shape[-1])

    for n in big:
        outs = _adamw("adamw_" + n, shard2d(w[n]), shard2d(m[n]), shard2d(v[n]), shard2d(own[n]),
                      received[n].reshape(received[n].shape[:1] + shard2d(own[n]).shape))
        for r_, o in zip(res, outs):
            r_[n] = o.reshape(w[n].shape)
    sm_names = ("w_decay_up", "w_iclr_up", "w_gate_up", "conv_w")
    stack = lambda d_: jnp.pad(jnp.concatenate([d_[n][0] for n in sm_names], axis=0),
                               ((0, _round_up(small_rows, SUBLANES) - small_rows), (0, 0)))
    outs = _adamw("adamw_stack", stack(w), stack(m), stack(v), own["small"], received["small"])
    bounds = (0, ld, ld + li, lora, lora + 3)
    for r_, o in zip(res, outs):
        for q, n in enumerate(sm_names):
            r_[n] = o[bounds[q]:bounds[q + 1]].reshape(w[n].shape)

    grad, delta, new_m, new_v = res
    return (loss_out, grad_x[None], *[grad[n] for n in _ORDER], *[delta[n] for n in _ORDER],
            *[new_m[n] for n in _ORDER], *[new_v[n] for n in _ORDER])


def kernel(x, norm_mix_w, w_in, gate_bias, shift_mu, w0, w_decay_up, a0, w_iclr_up, w_gate_up, k_k, k_a, r_k, lnx_w, lnx_b, w_out_a, conv_w, w_out_b, w_out, norm_mlp_w, w_mlp_up, w_mlp_down, norm_final_w, loss_target, m_norm_mix_w, m_w_in, m_gate_bias, m_shift_mu, m_w0, m_w_decay_up, m_a0, m_w_iclr_up, m_w_gate_up, m_k_k, m_k_a, m_r_k, m_lnx_w, m_lnx_b, m_w_out_a, m_conv_w, m_w_out_b, m_w_out, m_norm_mlp_w, m_w_mlp_up, m_w_mlp_down, m_norm_final_w, v_norm_mix_w, v_w_in, v_gate_bias, v_shift_mu, v_w0, v_w_decay_up, v_a0, v_w_iclr_up, v_w_gate_up, v_k_k, v_k_a, v_r_k, v_lnx_w, v_lnx_b, v_w_out_a, v_conv_w, v_w_out_b, v_w_out, v_norm_mlp_w, v_w_mlp_up, v_w_mlp_down, v_norm_final_w):
    w = dict(zip(_ORDER, (norm_mix_w, w_in, gate_bias, shift_mu, w0, w_decay_up, a0, w_iclr_up, w_gate_up, k_k, k_a,
                          r_k, lnx_w, lnx_b, w_out_a, conv_w, w_out_b, w_out, norm_mlp_w, w_mlp_up, w_mlp_down,
                          norm_final_w)))
    m = dict(zip(_ORDER, (m_norm_mix_w, m_w_in, m_gate_bias, m_shift_mu, m_w0, m_w_decay_up, m_a0, m_w_iclr_up,
                          m_w_gate_up, m_k_k, m_k_a, m_r_k, m_lnx_w, m_lnx_b, m_w_out_a, m_conv_w, m_w_out_b,
                          m_w_out, m_norm_mlp_w, m_w_mlp_up, m_w_mlp_down, m_norm_final_w)))
    v = dict(zip(_ORDER, (v_norm_mix_w, v_w_in, v_gate_bias, v_shift_mu, v_w0, v_w_decay_up, v_a0, v_w_iclr_up,
                          v_w_gate_up, v_k_k, v_k_a, v_r_k, v_lnx_w, v_lnx_b, v_w_out_a, v_conv_w, v_w_out_b,
                          v_w_out, v_norm_mlp_w, v_w_mlp_up, v_w_mlp_down, v_norm_final_w)))
    return _step(x, loss_target, w, m, v)
```

```python
import math

import jax
import jax.numpy as jnp
from jax import lax
from jax.experimental import pallas as pl
from jax.experimental.pallas import tpu as pltpu

F32 = jnp.float32
BF16 = jnp.bfloat16
MESH = pl.DeviceIdType.MESH

N_DEV = 8
HEAD = 64
LANES = 128
SUBLANES = 8
CHUNK = 64
RMS_EPS = 1e-5
LNX_EPS = 64e-5
L2_EPS = 1e-12
ADAM_LR = 0.001
ADAM_B1 = 0.9
ADAM_B2 = 0.999
ADAM_EPS = 1e-08
ADAM_WD = 0.01
ADAM_STEP = 10
VMEM_LIMIT = 48 * 1024 * 1024
MAX_FULL_K = 4096


def _cparams(sem):
    return pltpu.CompilerParams(dimension_semantics=sem, vmem_limit_bytes=VMEM_LIMIT)


def _tile(dim, cands):
    for c in cands:
        if c <= dim and dim % c == 0:
            return c
    return dim


def _my_pos():
    return lax.axis_index("x"), lax.axis_index("y"), lax.axis_index("c")


def _peer(pos, r):
    x, y, c = pos
    return (1 - x if r & 4 else x, 1 - y if r & 2 else y, 1 - c if r & 1 else c)


def _slot(pos):
    return 4 * pos[0] + 2 * pos[1] + pos[2]


class _Comm:
    def __init__(self, kind, arrs):
        self.kind, self.arrs, self.n = kind, list(arrs), len(arrs)
        if kind == "gather":
            self.out_shape = [jax.ShapeDtypeStruct((N_DEV,) + a.shape, a.dtype) for a in arrs]
        elif kind == "exchange":
            self.out_shape = [jax.ShapeDtypeStruct((N_DEV - 1,) + a.shape[1:], a.dtype) for a in arrs]
        elif kind == "pair":
            self.out_shape = [jax.ShapeDtypeStruct((N_DEV // 2,) + a.shape[1:], a.dtype) for a in arrs]
        else:
            self.out_shape = [jax.ShapeDtypeStruct((3,) + a.shape[1:], a.dtype) for a in arrs]
        self.scratch = [pltpu.SemaphoreType.DMA((7 * self.n,)), pltpu.SemaphoreType.DMA((7 * self.n,))]
        if kind == "gather":
            self.scratch.append(pltpu.SemaphoreType.DMA((self.n,)))

    def _exchange_copies(self, in_refs, out_refs, sems):
        me = _my_pos()
        x, y, c = me
        cps = []
        for ai in range(self.n):
            if self.kind == "exchange":
                todo = [(in_refs[ai].at[_slot(_peer(me, r))], out_refs[ai].at[r - 1], _peer(me, r), r - 1)
                        for r in range(1, N_DEV)]
            elif self.kind == "pair":
                todo = [(in_refs[ai].at[2 * q + 1 - c], out_refs[ai].at[q], (x, y, 1 - c), q)
                        for q in range(N_DEV // 2)]
            else:
                chips = [(1 - x, y), (x, 1 - y), (1 - x, 1 - y)]
                todo = [(in_refs[ai].at[2 * cx + cy], out_refs[ai].at[j], (cx, cy, c), j)
                        for j, (cx, cy) in enumerate(chips)]
            for src, dst, to, k in todo:
                cps.append(pltpu.make_async_remote_copy(
                    src_ref=src, dst_ref=dst, send_sem=sems[0].at[ai * 7 + k], recv_sem=sems[1].at[ai * 7 + k],
                    device_id=to, device_id_type=MESH))
        return cps

    def _gather_parts(self, in_refs, out_refs, sems):
        x, y, c = _my_pos()
        me, sibling = (x, y, c), (x, y, 1 - c)
        chips = [(1 - x, y), (x, 1 - y), (1 - x, 1 - y)]

        def copy(ai, k, block, to, src=None):
            dst = out_refs[ai].at[_slot(block)]
            return pltpu.make_async_remote_copy(
                src_ref=dst if src is None else src, dst_ref=dst, send_sem=sems[0].at[ai * 7 + k],
                recv_sem=sems[1].at[ai * 7 + k], device_id=to, device_id_type=MESH)

        mine = [pltpu.make_async_copy(in_refs[ai], out_refs[ai].at[_slot(me)], sems[2].at[ai])
                for ai in range(self.n)]
        first = []
        for ai in range(self.n):
            first.append(copy(ai, 0, me, sibling, src=in_refs[ai]))
            first += [copy(ai, 1 + j, me, (*chip, c), src=in_refs[ai]) for j, chip in enumerate(chips)]
        return me, sibling, chips, c, copy, mine, first

    def start(self, in_refs, out_refs, sems):
        if self.kind != "gather":
            for cp in self._exchange_copies(in_refs, out_refs, sems):
                cp.start()
            return
        _, _, _, _, _, mine, first = self._gather_parts(in_refs, out_refs, sems)
        for cp in mine + first:
            cp.start()

    def wait(self, in_refs, out_refs, sems):
        if self.kind != "gather":
            for cp in self._exchange_copies(in_refs, out_refs, sems):
                cp.wait()
            return
        me, sibling, chips, c, copy, mine, first = self._gather_parts(in_refs, out_refs, sems)
        passed = []
        for ai in range(self.n):
            for j, chip in enumerate(chips):
                copy(ai, 1 + j, (*chip, c), me).wait_recv()
                fwd = copy(ai, 4 + j, (*chip, c), sibling)
                fwd.start()
                passed.append(fwd)
        for ai in range(self.n):
            copy(ai, 0, sibling, me).wait_recv()
            for j, chip in enumerate(chips):
                copy(ai, 4 + j, (*chip, 1 - c), me).wait_recv()
        for cp in first + passed:
            cp.wait_send()
        for cp in mine:
            cp.wait()


def _hosted_call(body, name, comm, first, last, *, args, in_specs, out_shape, out_specs, scratch, grid, sem):
    if comm is None:
        return pl.pallas_call(body, name=name, out_shape=out_shape, grid=grid, in_specs=in_specs, out_specs=out_specs,
                              scratch_shapes=scratch, compiler_params=_cparams(sem))(*args)
    ni, no, ns, nc = len(args), len(out_shape), len(scratch), comm.n
    hbm = pl.BlockSpec(memory_space=pl.ANY)

    def hosted(*refs):
        ins, cin = refs[:ni], refs[ni:ni + nc]
        outs, cout = refs[ni + nc:ni + nc + no], refs[ni + nc + no:ni + 2 * nc + no]
        scr, sems = refs[ni + 2 * nc + no:ni + 2 * nc + no + ns], refs[ni + 2 * nc + no + ns:]

        @pl.when(first())
        def _():
            comm.start(cin, cout, sems)

        body(*ins, *outs, *scr)

        @pl.when(last())
        def _():
            comm.wait(cin, cout, sems)

    res = pl.pallas_call(
        hosted, name=name, out_shape=list(out_shape) + comm.out_shape, grid=grid,
        in_specs=list(in_specs) + [hbm] * nc, out_specs=list(out_specs) + [hbm] * nc,
        scratch_shapes=list(scratch) + comm.scratch,
        compiler_params=_cparams(("arbitrary",) * len(grid)))(*args, *comm.arrs)
    return res[:no], res[no:]


_DIMS = {"nn": ((1,), (0,)), "nt": ((1,), (1,)), "tn": ((0,), (0,))}


def _matmul(name, a, b, mode, out_dtypes, epi=None, extras=(), comm=None, out_slabs=False):
    if mode == "nn":
        (m, k), n = a.shape, b.shape[1]
    elif mode == "nt":
        (m, k), n = a.shape, b.shape[0]
    else:
        (k, m), n = a.shape, b.shape[1]
    tm = _tile(m, (1024, 512, 256, 128, 64, 32, 16, 8))
    if k <= MAX_FULL_K:
        tk, tn = k, _tile(n // N_DEV if out_slabs else n, (512, 256, 128))
    else:
        tk, tn = _tile(k, (2048, 1024, 512, 256, 128)), _tile(n, (1024, 512, 256, 128))
    nk = k // tk
    gm, gn = m // tm, n // tn
    a_spec = pl.BlockSpec((tk, tm), lambda i, j, q: (q, i)) if mode == "tn" else pl.BlockSpec((tm, tk), lambda i, j, q: (i, q))
    b_spec = pl.BlockSpec((tn, tk), lambda i, j, q: (j, q)) if mode == "nt" else pl.BlockSpec((tk, tn), lambda i, j, q: (q, j))
    mn_spec = pl.BlockSpec((tm, tn), lambda i, j, q: (i, j))
    per = n // N_DEV // tn if out_slabs else 0
    out_spec = pl.BlockSpec((None, tm, tn), lambda i, j, q: (j // per, i, j % per)) if out_slabs else mn_spec
    ne, no = len(extras), len(out_dtypes)
    dims = (_DIMS[mode], ((), ()))
    keep_t = mode == "tn" and nk == 1 and gn > 1

    def finish(r, extra_refs, out_refs):
        outs = (r,) if epi is None else epi(r, *[e[...] for e in extra_refs])
        for o_ref, o in zip(out_refs, outs):
            o_ref[...] = o.astype(o_ref.dtype)

    def body(a_ref, b_ref, *rest):
        extra_refs, out_refs = rest[:ne], rest[ne:ne + no]
        if keep_t:
            at = rest[ne + no]

            @pl.when(pl.program_id(1) == 0)
            def _():
                at[...] = a_ref[...].T

            part = jnp.dot(at[...], b_ref[...], preferred_element_type=F32)
        else:
            part = lax.dot_general(a_ref[...], b_ref[...], dims, preferred_element_type=F32)
        if nk == 1:
            finish(part, extra_refs, out_refs)
            return
        acc = rest[ne + no]
        q = pl.program_id(2)

        @pl.when(q == 0)
        def _():
            acc[...] = part

        @pl.when(jnp.logical_and(q > 0, q < nk - 1))
        def _():
            acc[...] += part

        @pl.when(q == nk - 1)
        def _():
            finish(acc[...] + part, extra_refs, out_refs)

    def first():
        return jnp.logical_and(jnp.logical_and(pl.program_id(0) == 0, pl.program_id(1) == 0), pl.program_id(2) == 0)

    def last():
        return jnp.logical_and(jnp.logical_and(pl.program_id(0) == gm - 1, pl.program_id(1) == gn - 1),
                               pl.program_id(2) == nk - 1)

    return _hosted_call(
        body, name, comm, first, last,
        args=[a, b, *extras], in_specs=[a_spec, b_spec] + [mn_spec] * ne,
        out_shape=[jax.ShapeDtypeStruct((N_DEV, m, n // N_DEV) if out_slabs else (m, n), dt) for dt in out_dtypes],
        out_specs=[out_spec] * no,
        scratch=[pltpu.VMEM((tm, tn), F32)] if nk > 1 else ([pltpu.VMEM((tm, tk), a.dtype)] if keep_t else []),
        grid=(gm, gn, nk), sem=("parallel", "arbitrary" if keep_t else "parallel", "arbitrary"))


@jax.custom_vjp
def _mm(a, w):
    return jnp.dot(a.astype(BF16), w.astype(BF16), preferred_element_type=F32)


def _mm_fwd(a, w):
    return _mm(a, w), (a, w)


def _mm_bwd(res, ct):
    a, w = res
    ctb = ct.astype(BF16)
    da = lax.dot_general(ctb, w.astype(BF16), (((1,), (1,)), ((), ())), preferred_element_type=F32)
    dw = lax.dot_general(a.astype(BF16), ctb, (((0,), (0,)), ((), ())), preferred_element_type=F32)
    return da, dw


_mm.defvjp(_mm_fwd, _mm_bwd)


def _split3(x):
    hi = x.astype(BF16)
    r1 = x - hi.astype(F32)
    mid = r1.astype(BF16)
    lo = (r1 - mid.astype(F32)).astype(BF16)
    return hi, mid, lo


def _head_ones(width):
    r = lax.broadcasted_iota(jnp.int32, (width, width), 0) // HEAD
    c = lax.broadcasted_iota(jnp.int32, (width, width), 1) // HEAD
    return (r == c).astype(BF16)


@jax.custom_vjp
def _segsum(x):
    ones = _head_ones(x.shape[-1])
    out = None
    for piece in _split3(x):
        t = jnp.dot(piece, ones, preferred_element_type=F32)
        out = t if out is None else out + t
    return out


_segsum.defvjp(lambda x: (_segsum(x), None), lambda _, ct: (_segsum(ct),))


def _softplus(z):
    return jnp.maximum(z, 0.0) + jnp.log(1.0 + jnp.exp(-jnp.abs(z)))


def _sigmoid(z):
    return 1.0 / (1.0 + jnp.exp(-z))


def _rms(x, w):
    ms = jnp.mean(x * x, axis=-1, keepdims=True)
    return x * lax.rsqrt(ms + RMS_EPS) * w


def _row(ref, i):
    return ref[pl.ds(i, 1), :]


def _shift_down(x, prev_ref, n, first):
    rolled = pltpu.roll(x, n, 0)
    rows = lax.broadcasted_iota(jnp.int32, x.shape, 0)
    for q in range(n):
        halo = jnp.where(first, 0.0, _row(prev_ref, SUBLANES - n + q))
        rolled = jnp.where(rows == q, halo, rolled)
    return rolled


def _shift_up(x, next_ref, n, last):
    t = x.shape[0]
    rolled = pltpu.roll(x, t - n, 0)
    rows = lax.broadcasted_iota(jnp.int32, x.shape, 0)
    for q in range(n):
        halo = jnp.where(last, 0.0, _row(next_ref, q))
        rolled = jnp.where(rows == t - n + q, halo, rolled)
    return rolled


def _acc_out(ref, val, first):
    @pl.when(first)
    def _():
        ref[...] = val

    @pl.when(jnp.logical_not(first))
    def _():
        ref[...] += val


def _prep_fn(k, plm, w0, a0, kkw, kaw, wd, wi, wg):
    w_log = -_softplus(-(w0 + _mm(jnp.tanh(plm), wd))) - 0.5
    lw = -jnp.exp(w_log)
    a_g = _sigmoid(a0 + _mm(plm, wi))
    g = _mm(_sigmoid(plm), wg)
    kk = k * kkw
    kk = kk / jnp.maximum(jnp.sqrt(_segsum(kk * kk)), L2_EPS)
    k2 = k * (1.0 + (a_g - 1.0) * kaw)
    return lw, k2, -kk, kk * a_g, g


def _post_fn(y, r, k2, v, g, lnw, lnb, rk):
    mu = _segsum(y) * (1.0 / HEAD)
    yc = y - mu
    var = _segsum(yc * yc) * (1.0 / HEAD)
    yn = yc * lax.rsqrt(var + LNX_EPS) * lnw + lnb
    bonus = _segsum(r * k2 * rk) * v
    return (yn + bonus) * g


def _merge_fn(pga, pgb, ba, bb, ya, yb):
    return _sigmoid(pga + ba) * ya + _sigmoid(pgb + bb) * yb


_NN, _NT, _TN = ((2,), (1,)), ((2,), (2,)), ((1,), (1,))


def _dot3(a, b, dims):
    ah = a.astype(BF16)
    al = (a - ah.astype(F32)).astype(BF16)
    bh = b.astype(BF16)
    bl = (b - bh.astype(F32)).astype(BF16)
    dg = lambda p, q: lax.dot_general(p, q, (dims, ((0,), (0,))), preferred_element_type=F32)
    return dg(ah, bh) + (dg(ah, bl) + dg(al, bh))


@jax.custom_vjp
def _dnn(a, b):
    return _dot3(a, b, _NN)


@jax.custom_vjp
def _dnt(a, b):
    return _dot3(a, b, _NT)


@jax.custom_vjp
def _dtn(a, b):
    return _dot3(a, b, _TN)


_dnn.defvjp(lambda a, b: (_dnn(a, b), (a, b)), lambda res, ct: (_dnt(ct, res[1]), _dtn(res[0], ct)))
_dnt.defvjp(lambda a, b: (_dnt(a, b), (a, b)), lambda res, ct: (_dnn(ct, res[1]), _dtn(ct, res[0])))
_dtn.defvjp(lambda a, b: (_dtn(a, b), (a, b)), lambda res, ct: (_dnt(res[1], ct), _dnn(res[0], ct)))


@jax.custom_vjp
def _unit_lower_inverse(low):
    n = low.shape[-1]
    ri = lax.broadcasted_iota(jnp.int32, low.shape, 1)
    ci = lax.broadcasted_iota(jnp.int32, low.shape, 2)
    inv = (ri == ci).astype(F32) + low
    pw = low
    for _ in range(int(math.log2(n // 2)) - 1):
        pw = _dnn(pw, pw)
        inv = inv + _dnn(inv, pw)
    return inv


def _unit_lower_inverse_bwd(inv, ct):
    return (_dnt(_dtn(inv, ct), inv),)


_unit_lower_inverse.defvjp(lambda low: (_unit_lower_inverse(low),) * 2, _unit_lower_inverse_bwd)


def _chunk_fn(s, r, lw, k, v, a, b):
    np_, c = r.shape[0], r.shape[1]
    c2 = 2 * c
    ri = lax.broadcasted_iota(jnp.int32, (np_, c, c), 1)
    ci = lax.broadcasted_iota(jnp.int32, (np_, c, c), 2)
    tri = (ri >= ci).astype(F32)
    cum = _dnn(tri, lw)
    tot = jnp.sum(lw, axis=1, keepdims=True)
    g_in, g_inv, g_out = jnp.exp(cum), jnp.exp(-cum), jnp.exp(tot - cum)
    lane_head = lax.broadcasted_iota(jnp.int32, (1, 2, 1, LANES), 3) // HEAD
    which = lax.broadcasted_iota(jnp.int32, (1, 2, 1, LANES), 1)
    hmask = (lane_head == which).astype(F32)

    def st(x):
        return (x[:, None] * hmask).reshape(np_, c2, LANES)

    r2, a2 = st(r * g_in), st(a * jnp.exp(cum - lw))
    b2, k2, v2 = st(b * g_inv), st(k * g_inv), st(v)
    bo2, ko2 = st(b * g_out), st(k * g_out)
    r2i = lax.broadcasted_iota(jnp.int32, (np_, c2, c2), 1)
    c2i = lax.broadcasted_iota(jnp.int32, (np_, c2, c2), 2)
    same = (r2i >= c) == (c2i >= c)
    strict = jnp.logical_and(same, r2i > c2i)
    incl = jnp.logical_and(same, r2i >= c2i)
    lab = jnp.where(strict, _dnt(a2, b2), 0.0)
    lak = jnp.where(strict, _dnt(a2, k2), 0.0)
    mrb = jnp.where(incl, _dnt(r2, b2), 0.0)
    mrk = jnp.where(incl, _dnt(r2, k2), 0.0)
    x2 = _dnt(a2, s) + _dnn(lak, v2)
    u2 = _dnn(_unit_lower_inverse(lab), x2)
    y2 = _dnt(r2, s) + _dnn(mrb, u2) + _dnn(mrk, v2)
    y = jnp.sum(y2.reshape(np_, 2, c, LANES), axis=1)
    s_new = s * jnp.exp(tot) + _dtn(u2, bo2) + _dtn(v2, ko2)
    return y, s_new


def _norm_fwd(name, x, add, w, want_sum):
    t, d = x.shape
    tt = _tile(t, (128, 64, 32, 16, 8))
    row = pl.BlockSpec((tt, d), lambda i: (i, 0))
    par = pl.BlockSpec((1, d), lambda i: (0, 0))
    has_add = add is not None

    def body(*refs):
        x_ref = refs[0]
        add_ref = refs[1] if has_add else None
        w_ref = refs[1 + has_add]
        outs = refs[2 + has_add:]
        h = x_ref[...] + add_ref[...] if has_add else x_ref[...]
        if want_sum:
            outs[0][...] = h
        outs[-1][...] = _rms(h, w_ref[...]).astype(BF16)

    out_shape = ([jax.ShapeDtypeStruct((t, d), F32)] if want_sum else []) + [jax.ShapeDtypeStruct((t, d), BF16)]
    return pl.pallas_call(
        body, name=name, out_shape=out_shape, grid=(t // tt,),
        in_specs=[row] + ([row] if has_add else []) + [par],
        out_specs=[row] * len(out_shape),
        compiler_params=_cparams(("arbitrary",)),
    )(*([x] + ([add] if has_add else []) + [w]))


def _norm_bwd(name, xin, dy, dres, w):
    t, d = xin.shape
    tt = _tile(t, (128, 64, 32, 16, 8))
    row = pl.BlockSpec((tt, d), lambda i: (i, 0))
    par = pl.BlockSpec((1, d), lambda i: (0, 0))

    def body(x_ref, dy_ref, dres_ref, w_ref, dx_ref, dxb_ref, dw_ref):
        _, vjp = jax.vjp(_rms, x_ref[...], w_ref[...])
        dx, dw = vjp(dy_ref[...])
        dx = dx + dres_ref[...]
        dx_ref[...] = dx
        dxb_ref[...] = dx.astype(BF16)
        _acc_out(dw_ref, dw, pl.program_id(0) == 0)

    return pl.pallas_call(
        body, name=name,
        out_shape=[jax.ShapeDtypeStruct((t, d), F32), jax.ShapeDtypeStruct((t, d), BF16),
                   jax.ShapeDtypeStruct((1, d), F32)],
        grid=(t // tt,), in_specs=[row, row, row, par], out_specs=[row, row, par],
        compiler_params=_cparams(("arbitrary",)),
    )(xin, dy, dres, w)


def _final(name, h1, md, target, w):
    t, d = h1.shape
    tt = _tile(t, (128, 64, 32, 16, 8))
    row = pl.BlockSpec((tt, d), lambda i: (i, 0))
    par = pl.BlockSpec((1, d), lambda i: (0, 0))
    one = pl.BlockSpec((1, LANES), lambda i: (0, 0))

    def body(h1_ref, md_ref, tg_ref, w_ref, loss_ref, dh_ref, dhb_ref, dw_ref):
        tg = tg_ref[...]

        def f(h, wv):
            err = _rms(h, wv) - tg
            return 0.5 * jnp.sum(jnp.mean(err * err, axis=-1, keepdims=True), axis=0, keepdims=True)

        loss, vjp = jax.vjp(f, h1_ref[...] + md_ref[...], w_ref[...])
        dh, dw = vjp(jnp.ones((1, 1), F32))
        dh_ref[...] = dh
        dhb_ref[...] = dh.astype(BF16)
        first = pl.program_id(0) == 0
        _acc_out(dw_ref, dw, first)
        _acc_out(loss_ref, jnp.broadcast_to(loss, (1, LANES)), first)

    return pl.pallas_call(
        body, name=name,
        out_shape=[jax.ShapeDtypeStruct((1, LANES), F32), jax.ShapeDtypeStruct((t, d), F32),
                   jax.ShapeDtypeStruct((t, d), BF16), jax.ShapeDtypeStruct((1, d), F32)],
        grid=(t // tt,), in_specs=[row, row, row, par], out_specs=[one, row, row, par],
        compiler_params=_cparams(("arbitrary",)),
    )(h1, md, target, w)


def _halo_specs(tt, cb, nrow8, col_of):
    prev = pl.BlockSpec((SUBLANES, cb), lambda i, j: (jnp.maximum(i * (tt // SUBLANES) - 1, 0), col_of(j)))
    nxt = pl.BlockSpec((SUBLANES, cb), lambda i, j: (jnp.minimum((i + 1) * (tt // SUBLANES), nrow8 - 1), col_of(j)))
    return prev, nxt


def _mix_fwd(name, p_all, mu, width, cb):
    t = p_all.shape[0]
    tt = _tile(t, (256, 128, 64, 32, 16, 8))
    main = pl.BlockSpec((tt, cb), lambda i, j: (i, j))
    prev, _ = _halo_specs(tt, cb, t // SUBLANES, lambda j: j)
    par = pl.BlockSpec((1, cb), lambda i, j: (0, j))

    def body(p_ref, prev_ref, mu_ref, o_ref):
        p = p_ref[...]
        o_ref[...] = p + (_shift_down(p, prev_ref, 1, pl.program_id(0) == 0) - p) * mu_ref[...]

    return pl.pallas_call(
        body, name=name, out_shape=jax.ShapeDtypeStruct((t, width), F32),
        grid=(t // tt, width // cb), in_specs=[main, prev, par], out_specs=main,
        compiler_params=_cparams(("arbitrary", "arbitrary")),
    )(p_all, p_all, mu)


def _mix_bwd(name, dpm_list, p_all, col0, mu, cb):
    t, width = dpm_list[0].shape
    tt = _tile(t, (256, 128, 64, 32, 16, 8))
    n8 = t // SUBLANES
    nl = len(dpm_list)
    main = pl.BlockSpec((tt, cb), lambda j, i: (i, j))
    nxt = pl.BlockSpec((SUBLANES, cb), lambda j, i: (jnp.minimum((i + 1) * (tt // SUBLANES), n8 - 1), j))
    p_main = pl.BlockSpec((tt, cb), lambda j, i: (i, col0 + j))
    p_prev = pl.BlockSpec((SUBLANES, cb), lambda j, i: (jnp.maximum(i * (tt // SUBLANES) - 1, 0), col0 + j))
    par = pl.BlockSpec((1, cb), lambda j, i: (0, j))
    nt_ = t // tt

    def body(*refs):
        d_refs, dn_refs = refs[:nl], refs[nl:2 * nl]
        p_ref, pp_ref, mu_ref, dp_ref, dmu_ref, nx_scr = refs[2 * nl:]
        i = pl.program_id(1)
        dpm = d_refs[0][...]
        nx = dn_refs[0][...]
        for q in range(1, nl):
            dpm = dpm + d_refs[q][...]
            nx = nx + dn_refs[q][...]
        nx_scr[...] = nx
        mu_v = mu_ref[...]
        up = _shift_up(dpm, nx_scr, 1, i == nt_ - 1)
        dp_ref[...] = (dpm * (1.0 - mu_v) + up * mu_v).astype(BF16)
        p = p_ref[...]
        diff = _shift_down(p, pp_ref, 1, i == 0) - p
        _acc_out(dmu_ref, jnp.sum(dpm * diff, axis=0, keepdims=True), i == 0)

    return pl.pallas_call(
        body, name=name,
        out_shape=[jax.ShapeDtypeStruct((t, width), BF16), jax.ShapeDtypeStruct((1, width), F32)],
        grid=(width // cb, nt_),
        in_specs=[main] * nl + [nxt] * nl + [p_main, p_prev, par],
        out_specs=[main, par],
        scratch_shapes=[pltpu.VMEM((SUBLANES, cb), F32)],
        compiler_params=_cparams(("arbitrary", "arbitrary")),
    )(*dpm_list, *dpm_list, p_all, p_all, mu)


def _prep_fwd(name, pm, cfg, w0, a0, kkw, kaw, wd, wi, wg):
    t = pm.shape[0]
    dr, lp, cb = cfg["dr"], cfg["lp"], cfg["cb"]
    tt = _tile(t, (256, 128, 64, 32, 16, 8))
    nj = dr // cb
    kspec = pl.BlockSpec((tt, cb), lambda j, i: (i, nj + j))
    lspec = pl.BlockSpec((tt, lp), lambda j, i: (i, 3 * dr // lp))
    par = pl.BlockSpec((1, cb), lambda j, i: (0, j))
    wspec = pl.BlockSpec((lp, cb), lambda j, i: (0, j))
    out = pl.BlockSpec((tt, cb), lambda j, i: (i, j))

    def body(k_ref, l_ref, w0_ref, a0_ref, kk_ref, ka_ref, wd_ref, wi_ref, wg_ref, *outs):
        vals = _prep_fn(k_ref[...], l_ref[...], w0_ref[...], a0_ref[...], kk_ref[...], ka_ref[...],
                        wd_ref[...], wi_ref[...], wg_ref[...])
        for o_ref, val in zip(outs, vals):
            o_ref[...] = val

    return pl.pallas_call(
        body, name=name, out_shape=[jax.ShapeDtypeStruct((t, dr), F32)] * 5,
        grid=(nj, t // tt), in_specs=[kspec, lspec, par, par, par, par, wspec, wspec, wspec],
        out_specs=[out] * 5, compiler_params=_cparams(("arbitrary", "arbitrary")),
    )(pm, pm, w0, a0, kkw, kaw, wd, wi, wg)


def _prep_bwd(name, pm, cfg, w0, a0, kkw, kaw, wd, wi, wg, cts, dr_parts, dv_parts):
    t = pm.shape[0]
    dr, lp, cb = cfg["dr"], cfg["lp"], cfg["cb"]
    tt = _tile(t, (256, 128, 64, 32, 16, 8))
    nj = dr // cb
    kspec = pl.BlockSpec((tt, cb), lambda j, i: (i, nj + j))
    lspec = pl.BlockSpec((tt, lp), lambda j, i: (i, 3 * dr // lp))
    par = pl.BlockSpec((1, cb), lambda j, i: (0, j))
    wspec = pl.BlockSpec((lp, cb), lambda j, i: (0, j))
    blk = pl.BlockSpec((tt, cb), lambda j, i: (i, j))
    dpl_spec = pl.BlockSpec((None, tt, lp), lambda j, i: (j, i, 0))

    def body(k_ref, l_ref, w0_ref, a0_ref, kk_ref, ka_ref, wd_ref, wi_ref, wg_ref,
             dlw_ref, dk2a_ref, dk2b_ref, da_ref, db_ref, dg_ref, dr0_ref, dr1_ref, dv0_ref, dv1_ref,
             dpr_ref, dpk_ref, dpv_ref, dpl_ref, dw0_ref, da0_ref, dkk_ref, dka_ref, dwd_ref, dwi_ref, dwg_ref):
        _, vjp = jax.vjp(_prep_fn, k_ref[...], l_ref[...], w0_ref[...], a0_ref[...], kk_ref[...], ka_ref[...],
                         wd_ref[...], wi_ref[...], wg_ref[...])
        dk, dpl, dw0, da0, dkk, dka, dwd, dwi, dwg = vjp(
            (dlw_ref[...], dk2a_ref[...] + dk2b_ref[...], da_ref[...], db_ref[...], dg_ref[...]))
        dpr_ref[...] = dr0_ref[...] + dr1_ref[...]
        dpv_ref[...] = dv0_ref[...] + dv1_ref[...]
        dpk_ref[...] = dk
        dpl_ref[...] = dpl
        first = pl.program_id(1) == 0
        for ref, val in ((dw0_ref, dw0), (da0_ref, da0), (dkk_ref, dkk), (dka_ref, dka),
                         (dwd_ref, dwd), (dwi_ref, dwi), (dwg_ref, dwg)):
            _acc_out(ref, val, first)

    out_shape = ([jax.ShapeDtypeStruct((t, dr), F32)] * 3 + [jax.ShapeDtypeStruct((nj, t, lp), F32)]
                 + [jax.ShapeDtypeStruct((1, dr), F32)] * 4 + [jax.ShapeDtypeStruct((lp, dr), F32)] * 3)
    return pl.pallas_call(
        body, name=name, out_shape=out_shape, grid=(nj, t // tt),
        in_specs=[kspec, lspec, par, par, par, par, wspec, wspec, wspec] + [blk] * 10,
        out_specs=[blk] * 3 + [dpl_spec] + [par] * 4 + [wspec] * 3,
        compiler_params=_cparams(("arbitrary", "arbitrary")),
    )(pm, pm, w0, a0, kkw, kaw, wd, wi, wg, *cts, *dr_parts, *dv_parts)


def _post_specs(t, cfg):
    dr, cb = cfg["dr"], cfg["cb"]
    tt = _tile(t, (256, 128, 64, 32, 16, 8))
    nj = dr // cb
    blk = pl.BlockSpec((tt, cb), lambda j, i: (i, j))
    rspec = pl.BlockSpec((tt, cb), lambda j, i: (i, j))
    vspec = pl.BlockSpec((tt, cb), lambda j, i: (i, 2 * nj + j))
    par = pl.BlockSpec((1, cb), lambda j, i: (0, j))
    return tt, nj, blk, rspec, vspec, par


def _post_fwd(name, y, pm, k2, g, lnw, lnb, rk, cfg):
    t = y.shape[0]
    tt, nj, blk, rspec, vspec, par = _post_specs(t, cfg)

    def body(y_ref, r_ref, k_ref, v_ref, g_ref, lw_ref, lb_ref, rk_ref, o_ref):
        o_ref[...] = _post_fn(y_ref[...], r_ref[...], k_ref[...], v_ref[...], g_ref[...],
                              lw_ref[...], lb_ref[...], rk_ref[...]).astype(BF16)

    return pl.pallas_call(
        body, name=name, out_shape=jax.ShapeDtypeStruct((t, cfg["dr"]), BF16), grid=(nj, t // tt),
        in_specs=[blk, rspec, blk, vspec, blk, par, par, par], out_specs=blk,
        compiler_params=_cparams(("arbitrary", "arbitrary")),
    )(y, pm, k2, pm, g, lnw, lnb, rk)


def _post_bwd(name, y, pm, k2, g, lnw, lnb, rk, dout, cfg):
    t = y.shape[0]
    tt, nj, blk, rspec, vspec, par = _post_specs(t, cfg)

    def body(y_ref, r_ref, k_ref, v_ref, g_ref, lw_ref, lb_ref, rk_ref, do_ref,
             dy_ref, dr_ref, dk_ref, dv_ref, dg_ref, dlw_ref, dlb_ref, drk_ref):
        _, vjp = jax.vjp(_post_fn, y_ref[...], r_ref[...], k_ref[...], v_ref[...], g_ref[...],
                         lw_ref[...], lb_ref[...], rk_ref[...])
        dy, dr, dk, dv, dg, dlw, dlb, drk = vjp(do_ref[...])
        for ref, val in ((dy_ref, dy), (dr_ref, dr), (dk_ref, dk), (dv_ref, dv), (dg_ref, dg)):
            ref[...] = val
        first = pl.program_id(1) == 0
        for ref, val in ((dlw_ref, dlw), (dlb_ref, dlb), (drk_ref, drk)):
            _acc_out(ref, val, first)

    dr = cfg["dr"]
    return pl.pallas_call(
        body, name=name,
        out_shape=[jax.ShapeDtypeStruct((t, dr), F32)] * 5 + [jax.ShapeDtypeStruct((1, dr), F32)] * 3,
        grid=(nj, t // tt),
        in_specs=[blk, rspec, blk, vspec, blk, par, par, par, blk],
        out_specs=[blk] * 5 + [par] * 3,
        compiler_params=_cparams(("arbitrary", "arbitrary")),
    )(y, pm, k2, pm, g, lnw, lnb, rk, dout)


def _conv_specs(t, cfg):
    dc, cb = cfg["dc"], cfg["cb"]
    tt = _tile(t, (256, 128, 64, 32, 16, 8))
    nj = dc // cb
    c0 = cfg["off_conv"] // cb
    n8 = t // SUBLANES

    def sect(s):
        col = lambda j: c0 + s * nj + j
        main = pl.BlockSpec((tt, cb), lambda j, i: (i, col(j)))
        prev = pl.BlockSpec((SUBLANES, cb), lambda j, i: (jnp.maximum(i * (tt // SUBLANES) - 1, 0), col(j)))
        nxt = pl.BlockSpec((SUBLANES, cb), lambda j, i: (jnp.minimum((i + 1) * (tt // SUBLANES), n8 - 1), col(j)))
        return main, prev, nxt

    blk = pl.BlockSpec((tt, cb), lambda j, i: (i, j))
    wspec = pl.BlockSpec((SUBLANES, cb), lambda j, i: (0, j))
    return tt, nj, n8, sect, blk, wspec


def _conv_fwd(name, p_all, cw8, cfg):
    t = p_all.shape[0]
    tt, nj, n8, sect, blk, wspec = _conv_specs(t, cfg)
    (bm, _, _), (cm, cp, _), (um, up, _) = sect(0), sect(1), sect(2)

    def body(b_ref, c_ref, cp_ref, u_ref, up_ref, w_ref, o_ref, zp_scr):
        first = pl.program_id(1) == 0
        z = c_ref[...] * u_ref[...]
        zp_scr[...] = cp_ref[...] * up_ref[...]
        o = _row(w_ref, 2) * z + _row(w_ref, 1) * _shift_down(z, zp_scr, 1, first) \
            + _row(w_ref, 0) * _shift_down(z, zp_scr, 2, first)
        o_ref[...] = (b_ref[...] * o).astype(BF16)

    return pl.pallas_call(
        body, name=name, out_shape=jax.ShapeDtypeStruct((t, cfg["dc"]), BF16), grid=(nj, t // tt),
        in_specs=[bm, cm, cp, um, up, wspec], out_specs=blk,
        scratch_shapes=[pltpu.VMEM((SUBLANES, blk.block_shape[1]), F32)],
        compiler_params=_cparams(("arbitrary", "arbitrary")),
    )(p_all, p_all, p_all, p_all, p_all, cw8)


def _conv_bwd(name, p_all, cw8, dyb, cfg):
    t = p_all.shape[0]
    tt, nj, n8, sect, blk, wspec = _conv_specs(t, cfg)
    (bm, _, bn), (cm, cp, _), (um, up, _) = sect(0), sect(1), sect(2)
    cb = blk.block_shape[1]
    dnxt = pl.BlockSpec((SUBLANES, cb), lambda j, i: (jnp.minimum((i + 1) * (tt // SUBLANES), n8 - 1), j))
    nt_ = t // tt

    def body(b_ref, bn_ref, c_ref, cp_ref, u_ref, up_ref, w_ref, d_ref, dn_ref,
             db_ref, dc_ref, du_ref, dw_ref, zp_scr, don_scr):
        i = pl.program_id(1)
        first, last = i == 0, i == nt_ - 1
        c, u, b, dy = c_ref[...], u_ref[...], b_ref[...], d_ref[...]
        z = c * u
        zp_scr[...] = cp_ref[...] * up_ref[...]
        z1 = _shift_down(z, zp_scr, 1, first)
        z2 = _shift_down(z, zp_scr, 2, first)
        w0, w1, w2 = _row(w_ref, 0), _row(w_ref, 1), _row(w_ref, 2)
        o = w2 * z + w1 * z1 + w0 * z2
        do = dy * b
        don_scr[...] = dn_ref[...] * bn_ref[...]
        dz = w2 * do + w1 * _shift_up(do, don_scr, 1, last) + w0 * _shift_up(do, don_scr, 2, last)
        db_ref[...] = (dy * o).astype(BF16)
        dc_ref[...] = (dz * u).astype(BF16)
        du_ref[...] = (dz * c).astype(BF16)
        rows = lax.broadcasted_iota(jnp.int32, (SUBLANES, cb), 0)
        s0 = jnp.sum(do * z2, axis=0, keepdims=True)
        s1 = jnp.sum(do * z1, axis=0, keepdims=True)
        s2 = jnp.sum(do * z, axis=0, keepdims=True)
        dw = jnp.where(rows == 0, s0, jnp.where(rows == 1, s1, jnp.where(rows == 2, s2, 0.0)))
        _acc_out(dw_ref, dw, first)

    dc = cfg["dc"]
    return pl.pallas_call(
        body, name=name,
        out_shape=[jax.ShapeDtypeStruct((t, dc), BF16)] * 3 + [jax.ShapeDtypeStruct((SUBLANES, dc), F32)],
        grid=(nj, nt_),
        in_specs=[bm, bn, cm, cp, um, up, wspec, blk, dnxt],
        out_specs=[blk] * 3 + [wspec],
        scratch_shapes=[pltpu.VMEM((SUBLANES, cb), F32), pltpu.VMEM((SUBLANES, cb), F32)],
        compiler_params=_cparams(("arbitrary", "arbitrary")),
    )(p_all, p_all, p_all, p_all, p_all, p_all, cw8, dyb, dyb)


def _merge_specs(t, cfg):
    d, cb = cfg["d"], cfg["cb"]
    tt = _tile(t, (256, 128, 64, 32, 16, 8))
    nj = d // cb
    g0 = cfg["off_gate"] // cb
    ga = pl.BlockSpec((tt, cb), lambda j, i: (i, g0 + j))
    gb = pl.BlockSpec((tt, cb), lambda j, i: (i, g0 + nj + j))
    ba = pl.BlockSpec((1, cb), lambda j, i: (0, j))
    bb = pl.BlockSpec((1, cb), lambda j, i: (0, nj + j))
    blk = pl.BlockSpec((tt, cb), lambda j, i: (i, j))
    return tt, nj, ga, gb, ba, bb, blk


def _merge_fwd(name, p_all, bias, ya, yb, cfg):
    t = p_all.shape[0]
    tt, nj, ga, gb, ba, bb, blk = _merge_specs(t, cfg)

    def body(ga_ref, gb_ref, ba_ref, bb_ref, ya_ref, yb_ref, o_ref):
        o_ref[...] = _merge_fn(ga_ref[...], gb_ref[...], ba_ref[...], bb_ref[...],
                               ya_ref[...], yb_ref[...]).astype(BF16)

    return pl.pallas_call(
        body, name=name, out_shape=jax.ShapeDtypeStruct((t, cfg["d"]), BF16), grid=(nj, t // tt),
        in_specs=[ga, gb, ba, bb, blk, blk], out_specs=blk,
        compiler_params=_cparams(("arbitrary", "arbitrary")),
    )(p_all, p_all, bias, bias, ya, yb)


def _merge_bwd(name, p_all, bias, ya, yb, dm, cfg):
    t = p_all.shape[0]
    tt, nj, ga, gb, ba, bb, blk = _merge_specs(t, cfg)

    def body(ga_ref, gb_ref, ba_ref, bb_ref, ya_ref, yb_ref, dm_ref,
             dga_ref, dgb_ref, dya_ref, dyb_ref, dba_ref, dbb_ref):
        _, vjp = jax.vjp(_merge_fn, ga_ref[...], gb_ref[...], ba_ref[...], bb_ref[...], ya_ref[...], yb_ref[...])
        dga, dgb, dba, dbb, dya, dyb = vjp(dm_ref[...])
        for ref, val in ((dga_ref, dga), (dgb_ref, dgb), (dya_ref, dya), (dyb_ref, dyb)):
            ref[...] = val.astype(BF16)
        first = pl.program_id(1) == 0
        _acc_out(dba_ref, dba, first)
        _acc_out(dbb_ref, dbb, first)

    d = cfg["d"]
    par = pl.BlockSpec((1, blk.block_shape[1]), lambda j, i: (0, j))
    return pl.pallas_call(
        body, name=name,
        out_shape=[jax.ShapeDtypeStruct((t, d), BF16)] * 4 + [jax.ShapeDtypeStruct((1, d), F32)] * 2,
        grid=(nj, t // tt),
        in_specs=[ga, gb, ba, bb, blk, blk, blk], out_specs=[blk] * 4 + [par] * 2,
        compiler_params=_cparams(("arbitrary", "arbitrary")),
    )(p_all, p_all, bias, bias, ya, yb, dm)


PAIRS = 8


def _pair_stack(ref, pairs):
    return jnp.stack([ref[:, p * LANES:(p + 1) * LANES] for p in range(pairs)])


def _pair_store(ref, val):
    for p in range(val.shape[0]):
        ref[:, p * LANES:(p + 1) * LANES] = val[p]


def _rec_specs(t, cfg, rev):
    dr = cfg["dr"]
    nc = t // CHUNK
    hp = dr // LANES
    pairs = _tile(hp, (PAIRS, 2, 1))
    ng = hp // pairs
    w = LANES * pairs
    ch = (lambda c: nc - 1 - c) if rev else (lambda c: c)
    slab = pl.BlockSpec((CHUNK, w), lambda h, c: (ch(c), h))
    vspec = pl.BlockSpec((CHUNK, w), lambda h, c: (ch(c), 2 * ng + h))
    sspec = pl.BlockSpec((None, pairs, LANES, LANES), lambda h, c: (ch(c), h, 0, 0))
    first = lambda: jnp.logical_and(pl.program_id(0) == 0, pl.program_id(1) == 0)
    last = lambda: jnp.logical_and(pl.program_id(0) == ng - 1, pl.program_id(1) == nc - 1)
    return nc, hp, pairs, ng, slab, vspec, sspec, first, last


def _rec_fwd(name, pm, lw, k2, a, b, cfg, comm=None):
    t = pm.shape[0]
    nc, hp, pairs, ng, slab, vspec, sspec, first, last = _rec_specs(t, cfg, False)

    def body(r_ref, lw_ref, k_ref, v_ref, a_ref, b_ref, y_ref, s_ref, s_scr):
        @pl.when(pl.program_id(1) == 0)
        def _():
            s_scr[...] = jnp.zeros_like(s_scr)

        s = s_scr[...]
        s_ref[...] = s
        y, s_new = _chunk_fn(s, *[_pair_stack(ref, pairs) for ref in (r_ref, lw_ref, k_ref, v_ref, a_ref, b_ref)])
        _pair_store(y_ref, y)
        s_scr[...] = s_new

    return _hosted_call(
        body, name, comm, first, last, args=[pm, lw, k2, pm, a, b],
        in_specs=[slab, slab, slab, vspec, slab, slab],
        out_shape=[jax.ShapeDtypeStruct((t, cfg["dr"]), F32), jax.ShapeDtypeStruct((nc, hp, LANES, LANES), F32)],
        out_specs=[slab, sspec], scratch=[pltpu.VMEM((pairs, LANES, LANES), F32)], grid=(ng, nc),
        sem=("arbitrary", "arbitrary"))


def _rec_bwd(name, pm, lw, k2, a, b, s_chk, dy, cfg, comm=None):
    t = pm.shape[0]
    nc, hp, pairs, ng, slab, vspec, sspec, first, last = _rec_specs(t, cfg, True)

    def body(r_ref, lw_ref, k_ref, v_ref, a_ref, b_ref, s_ref, dy_ref,
             dr_ref, dlw_ref, dk_ref, dv_ref, da_ref, db_ref, ds_scr):
        @pl.when(pl.program_id(1) == 0)
        def _():
            ds_scr[...] = jnp.zeros_like(ds_scr)

        _, vjp = jax.vjp(_chunk_fn, s_ref[...],
                         *[_pair_stack(ref, pairs) for ref in (r_ref, lw_ref, k_ref, v_ref, a_ref, b_ref)])
        ds, dr, dlw, dk, dv, da, db = vjp((_pair_stack(dy_ref, pairs), ds_scr[...]))
        ds_scr[...] = ds
        for ref, val in ((dr_ref, dr), (dlw_ref, dlw), (dk_ref, dk), (dv_ref, dv), (da_ref, da), (db_ref, db)):
            _pair_store(ref, val)

    return _hosted_call(
        body, name, comm, first, last, args=[pm, lw, k2, pm, a, b, s_chk, dy],
        in_specs=[slab, slab, slab, vspec, slab, slab, sspec, slab],
        out_shape=[jax.ShapeDtypeStruct((t, cfg["dr"]), F32)] * 6, out_specs=[slab] * 6,
        scratch=[pltpu.VMEM((pairs, LANES, LANES), F32)], grid=(ng, nc), sem=("arbitrary", "arbitrary"))


def _comm_call(name, comm):
    n = comm.n
    hbm = pl.BlockSpec(memory_space=pl.ANY)

    def body(*refs):
        comm.start(refs[:n], refs[n:2 * n], refs[2 * n:])
        comm.wait(refs[:n], refs[n:2 * n], refs[2 * n:])

    return pl.pallas_call(body, name=name, out_shape=comm.out_shape, in_specs=[hbm] * n, out_specs=[hbm] * n,
                          scratch_shapes=comm.scratch)(*comm.arrs)


def _all_reduce_small(name, v):
    rows = v.shape[0]
    vm = pl.BlockSpec(memory_space=pltpu.VMEM)

    def body(x_ref, out_ref, buf, send_sems, recv_sems):
        x, y, c = _my_pos()
        me, sibling = (x, y, c), (x, y, 1 - c)
        chips = [(1 - x, y), (x, 1 - y), (1 - x, 1 - y)]

        def copy(k, block, to, src=None):
            px, py, pc = block
            dst = buf.at[4 * px + 2 * py + pc]
            return pltpu.make_async_remote_copy(
                src_ref=dst if src is None else src, dst_ref=dst,
                send_sem=send_sems.at[k], recv_sem=recv_sems.at[k], device_id=to, device_id_type=MESH)

        buf[4 * x + 2 * y + c] = x_ref[...]
        first = [copy(0, me, sibling, src=x_ref)]
        first += [copy(1 + j, me, (*chip, c), src=x_ref) for j, chip in enumerate(chips)]
        for cp in first:
            cp.start()
        passed = [copy(4 + j, (*chip, c), sibling) for j, chip in enumerate(chips)]
        for j, chip in enumerate(chips):
            copy(1 + j, (*chip, c), me).wait_recv()
            passed[j].start()
        copy(0, sibling, me).wait_recv()
        for j, chip in enumerate(chips):
            copy(4 + j, (*chip, 1 - c), me).wait_recv()
        for cp in first + passed:
            cp.wait_send()
        acc = buf[0]
        for d in range(1, N_DEV):
            acc = acc + buf[d]
        out_ref[...] = acc

    return pl.pallas_call(
        body, name=name, out_shape=jax.ShapeDtypeStruct(v.shape, F32),
        in_specs=[vm], out_specs=vm,
        scratch_shapes=[pltpu.VMEM((N_DEV, rows, LANES), F32), pltpu.SemaphoreType.DMA((7,)),
                        pltpu.SemaphoreType.DMA((7,))],
    )(v)


def _pair_sum(name, slabs, got, core):
    _, rows, cols = slabs.shape
    nq = got.shape[0]
    rb = _tile(rows, (256, 128, 64, 32, 16, 8))
    mine = pl.BlockSpec((None, rb, cols), lambda q, j, c_ref: (2 * q + c_ref[0], j, 0))
    blk = pl.BlockSpec((None, rb, cols), lambda q, j, c_ref: (q, j, 0))

    def body(c_ref, a_ref, b_ref, o_ref):
        o_ref[...] = (a_ref[...].astype(F32) + b_ref[...].astype(F32)).astype(o_ref.dtype)

    return pl.pallas_call(
        body, name=name, out_shape=jax.ShapeDtypeStruct(got.shape, got.dtype),
        grid_spec=pltpu.PrefetchScalarGridSpec(num_scalar_prefetch=1, grid=(nq, rows // rb),
                                               in_specs=[mine, blk], out_specs=blk),
        compiler_params=_cparams(("arbitrary", "arbitrary")))(core, slabs, got)


def _adamw(name, w, m, v, g_own, g_recv=None):
    rows, cols = w.shape
    nr = g_recv.shape[0] if g_recv is not None else 0
    per_el = 4 * 3 + g_own.dtype.itemsize + (nr * g_recv.dtype.itemsize if nr else 0) + 16
    rb = SUBLANES * 2
    while rb * 2 <= rows and rows % (rb * 2) == 0 and rb * 2 * cols * per_el * 2 <= VMEM_LIMIT // 2:
        rb *= 2
    if rows % rb:
        rb = rows
    blk = pl.BlockSpec((rb, cols), lambda i: (i, 0))
    rblk = pl.BlockSpec((max(nr, 1), rb, cols), lambda i: (0, i, 0))
    has_r = g_recv is not None
    bc1 = 1.0 - ADAM_B1 ** ADAM_STEP
    bc2 = 1.0 - ADAM_B2 ** ADAM_STEP

    def body(*refs):
        w_ref, m_ref, v_ref, go_ref = refs[:4]
        gr_ref = refs[4] if has_r else None
        g_out, d_out, m_out, v_out = refs[4 + has_r:]
        g = go_ref[...].astype(F32)
        if has_r:
            for r in range(nr):
                g = g + gr_ref[r].astype(F32)
        mn = ADAM_B1 * m_ref[...] + (1.0 - ADAM_B1) * g
        vn = ADAM_B2 * v_ref[...] + (1.0 - ADAM_B2) * (g * g)
        m_hat = mn / bc1
        v_hat = vn / bc2
        g_out[...] = g
        d_out[...] = -ADAM_LR * (m_hat / (jnp.sqrt(v_hat) + ADAM_EPS) + ADAM_WD * w_ref[...])
        m_out[...] = mn
        v_out[...] = vn

    return pl.pallas_call(
        body, name=name, out_shape=[jax.ShapeDtypeStruct((rows, cols), F32)] * 4, grid=(rows // rb,),
        in_specs=[blk] * 4 + ([rblk] if has_r else []), out_specs=[blk] * 4,
        compiler_params=_cparams(("arbitrary",)),
    )(*([w, m, v, g_own] + ([g_recv] if has_r else [])))


def _round_up(n, q):
    return (n + q - 1) // q * q


def _cols(a8):
    return jnp.transpose(a8, (1, 0, 2)).reshape(a8.shape[1], -1)


def _col_slabs(a):
    r_, c_ = a.shape
    return jnp.transpose(a.reshape(r_, N_DEV, c_ // N_DEV), (1, 0, 2))


_MID = ("w_out_a", "w_out_b", "w_out", "w_mlp_up", "w_mlp_down")


def _local_step(x, target, wts, shards, cfg):
    dr, dc, d, lp, cb = cfg["dr"], cfg["dc"], cfg["d"], cfg["lp"], cfg["cb"]
    dff = shards["w_mlp_down"].shape[0] * N_DEV
    wmix = 3 * dr + lp
    (xn,) = _norm_fwd("norm_mix_fwd", x, None, wts["norm_mix_w"], False)
    (p_all,), (g_oa, g_ob, g_o) = _matmul(
        "mm_in", xn, wts["w_all"], "nn", [F32],
        comm=_Comm("gather", [shards["w_out_a"], shards["w_out_b"], shards["w_out"]]))
    w_out_a, w_out_b, w_out = _cols(g_oa), _cols(g_ob), g_o.reshape(d, d)
    pm = _mix_fwd("mix_fwd", p_all, wts["mu_pad"], wmix, cb)
    prep_w = (wts["w0"], wts["a0"], wts["k_k"], wts["k_a"], wts["wd"], wts["wi"], wts["wg"])
    lw, k2, a_in, b_in, g = _prep_fwd("prep_fwd", pm, cfg, *prep_w)
    (y_raw, s_chk), (g_u,) = _rec_fwd(
        "rec_fwd", pm, lw, k2, a_in, b_in, cfg, comm=_Comm("gather", [shards["w_mlp_up"]]))
    w_up = _cols(g_u)
    post_w = (wts["lnx_w"], wts["lnx_b"], wts["r_k"])
    ya_in = _post_fwd("post_fwd", y_raw, pm, k2, g, *post_w, cfg)
    (ya,) = _matmul("mm_out_a", ya_in, w_out_a, "nn", [F32])
    yb_in = _conv_fwd("conv_fwd", p_all, wts["conv_w8"], cfg)
    (yb,) = _matmul("mm_out_b", yb_in, w_out_b, "nn", [F32])
    mg = _merge_fwd("merge_fwd", p_all, wts["gate_bias"], ya, yb, cfg)
    (mo,) = _matmul("mm_out", mg, w_out, "nn", [F32])
    h1, hn = _norm_fwd("norm_mlp_fwd", x, mo, wts["norm_mlp_w"], True)
    (u, act), (g_d,) = _matmul("mm_up", hn, w_up, "nn", [F32, BF16],
                               epi=lambda r: (r, jnp.square(jnp.maximum(r, 0.0))),
                               comm=_Comm("gather", [shards["w_mlp_down"]]))
    w_down = g_d.reshape(dff, d)
    (md,) = _matmul("mm_down", act, w_down, "nn", [F32])
    loss, dh2, dh2b, g_norm_final = _final("final", h1, md, target, wts["norm_final_w"])
    (du,) = _matmul("mm_down_dx", dh2b, w_down, "nt", [BF16],
                    epi=lambda r, uu: (r * (2.0 * jnp.maximum(uu, 0.0)),), extras=(u,))
    (g_down,) = _matmul("mm_down_dw", act, dh2b, "tn", [BF16])
    core = lax.axis_index("c").astype(jnp.int32).reshape(1)
    my_chip = 2 * lax.axis_index("x") + lax.axis_index("y")
    me = 2 * my_chip + lax.axis_index("c")
    own, recv = {}, {}

    def chip_own(chip_sum):
        return lax.dynamic_index_in_dim(chip_sum, my_chip, axis=0, keepdims=False)

    down_slabs = g_down.reshape(N_DEV, dff // N_DEV, d)
    (dhn,), (got,) = _matmul("mm_up_dx", du, w_up, "nt", [F32], comm=_Comm("pair", [down_slabs]))
    down_sum = _pair_sum("pair_sum_down", down_slabs, got, core)
    (g_up,), (recv["w_mlp_down"],) = _matmul("mm_up_dw", hn, du, "tn", [BF16], out_slabs=True,
                                            comm=_Comm("chips", [down_sum]))
    own["w_mlp_down"] = chip_own(down_sum)
    dh1, dh1b, g_norm_mlp = _norm_bwd("norm_mlp_bwd", h1, dhn, dh2, wts["norm_mlp_w"])
    (dmg,), (got,) = _matmul("mm_out_dx", dh1b, w_out, "nt", [F32], comm=_Comm("pair", [g_up]))
    up_sum = _pair_sum("pair_sum_up", g_up, got, core)
    own["w_mlp_up"] = chip_own(up_sum)
    (g_out,) = _matmul("mm_out_dw", mg, dh1b, "tn", [BF16])
    dpga, dpgb, dya, dyb, dba, dbb = _merge_bwd("merge_bwd", p_all, wts["gate_bias"], ya, yb, dmg, cfg)
    (dya_in,) = _matmul("mm_out_a_dx", dya, w_out_a, "nt", [F32])
    (g_out_a,) = _matmul("mm_out_a_dw", ya_in, dya, "tn", [BF16], out_slabs=True)
    (dyb_in,) = _matmul("mm_out_b_dx", dyb, w_out_b, "nt", [F32])
    (g_out_b,) = _matmul("mm_out_b_dw", yb_in, dyb, "tn", [BF16], out_slabs=True)
    dpb, dpc, dpu, g_conv8 = _conv_bwd("conv_bwd", p_all, wts["conv_w8"], dyb_in, cfg)
    dy_raw, dr_post, dk_post, dv_post, dg, g_lnw, g_lnb, g_rk = _post_bwd(
        "post_bwd", y_raw, pm, k2, g, *post_w, dya_in, cfg)
    (dr_rec, dlw, dk_rec, dv_rec, da_in, db_in), (recv["w_mlp_up"],) = _rec_bwd(
        "rec_bwd", pm, lw, k2, a_in, b_in, s_chk, dy_raw, cfg, comm=_Comm("chips", [up_sum]))
    (dpm_r, dpm_k, dpm_v, dpl, g_w0, g_a0, g_kk, g_ka, g_wd, g_wi, g_wg) = _prep_bwd(
        "prep_bwd", pm, cfg, *prep_w, (dlw, dk_rec, dk_post, da_in, db_in, dg),
        (dr_rec, dr_post), (dv_rec, dv_post))
    mu = wts["mu_pad"]
    nb = dr // cb
    dps, dmus = [], []
    for s, dpm_s in enumerate((dpm_r, dpm_k, dpm_v)):
        dp_s, dmu_s = _mix_bwd("mix_bwd_%d" % s, [dpm_s], p_all, s * nb, mu[:, s * dr:(s + 1) * dr], cb)
        dps.append(dp_s)
        dmus.append(dmu_s)
    dp_l, dmu_l = _mix_bwd("mix_bwd_l", [dpl[j] for j in range(nb)], p_all, 3 * nb, mu[:, 3 * dr:], min(cb, lp))
    tail = [jnp.zeros((x.shape[0], cfg["wall"] - cfg["used"]), BF16)] if cfg["wall"] > cfg["used"] else []
    dp_all = jnp.concatenate(dps + [dp_l, dpb, dpc, dpu, dpga, dpgb] + tail, axis=1)
    ld, li, lora = cfg["ld"], cfg["li"], cfg["lora"]
    g_small = jnp.concatenate([g_wd[:ld], g_wi[ld:ld + li], g_wg[ld + li:lora], g_conv8[:3]], axis=0)
    g_small = jnp.pad(g_small, ((0, cfg["small_rows"] - g_small.shape[0]), (0, 0)))
    direct = dict(w_out_a=g_out_a, w_out_b=g_out_b, w_out=g_out.reshape(N_DEV, d // N_DEV, d),
                  small=_col_slabs(g_small))
    (g_all,), got4 = _matmul("mm_in_dw", xn, dp_all, "tn", [BF16],
                             comm=_Comm("exchange", list(direct.values())))
    for n, slabs, r in zip(direct, direct.values(), got4):
        own[n] = lax.dynamic_index_in_dim(slabs, me, axis=0, keepdims=False)
        recv[n] = r
    g_in = jnp.concatenate([g_all[:, :3 * dr + lora], g_all[:, 3 * dr + lp:cfg["used"]]], axis=1)
    in_slabs = _col_slabs(g_in)
    (got,) = _comm_call("pair_exchange", _Comm("pair", [in_slabs]))
    in_sum = _pair_sum("pair_sum_in", in_slabs, got, core)
    (dxn,), (recv["w_in"],) = _matmul("mm_in_dx", dp_all, wts["w_all"], "nt", [F32], comm=_Comm("chips", [in_sum]))
    own["w_in"] = chip_own(in_sum)
    grad_x, _, g_norm_mix = _norm_bwd("norm_mix_bwd", x, dxn, dh1, wts["norm_mix_w"])
    grads = dict(
        norm_mix_w=g_norm_mix, gate_bias=jnp.concatenate([dba, dbb], axis=1),
        mu_pad=jnp.concatenate(dmus + [dmu_l], axis=1), w0=g_w0, a0=g_a0, k_k=g_kk, k_a=g_ka,
        r_k=g_rk, lnx_w=g_lnw, lnx_b=g_lnb, norm_mlp_w=g_norm_mlp, norm_final_w=g_norm_final)
    return loss, grad_x, grads, own, recv


_SMALL = ("norm_mix_w", "gate_bias", "shift_mu", "w0", "a0", "k_k", "k_a", "r_k", "lnx_w", "lnx_b",
          "norm_mlp_w", "norm_final_w")
_ORDER = ("norm_mix_w", "w_in", "gate_bias", "shift_mu", "w0", "w_decay_up", "a0", "w_iclr_up", "w_gate_up",
          "k_k", "k_a", "r_k", "lnx_w", "lnx_b", "w_out_a", "conv_w", "w_out_b", "w_out", "norm_mlp_w",
          "w_mlp_up", "w_mlp_down", "norm_final_w")


def _step(x, target, w, m, v):
    t, d = x.shape[1], x.shape[2]
    dr = w["w0"].shape[-1]
    ld, li, lg = w["w_decay_up"].shape[1], w["w_iclr_up"].shape[1], w["w_gate_up"].shape[1]
    lora = ld + li + lg
    lp = _round_up(lora, LANES)
    dc = w["conv_w"].shape[-1] * N_DEV
    cb = math.gcd(math.gcd(lp, dr), 512)
    used = 3 * dr + lp + 3 * dc + 2 * d
    wall = _round_up(used, 1024 if used > MAX_FULL_K else LANES)
    small_rows = ld + li + lg + 3
    cfg = dict(d=d, dr=dr, dc=dc, lp=lp, cb=cb, off_conv=3 * dr + lp, off_gate=3 * dr + lp + 3 * dc, used=used,
               wall=wall, ld=ld, li=li, lora=lora, small_rows=_round_up(small_rows, SUBLANES))
    x2, tg2 = x[0], target[0]

    small_sh = jnp.concatenate([w["w_decay_up"][0], w["w_iclr_up"][0], w["w_gate_up"][0], w["conv_w"][0]], axis=0)
    small_sh = jnp.pad(small_sh, ((0, _round_up(small_rows, SUBLANES) - small_rows), (0, 0)))
    big = ("w_in",) + _MID
    g_in8, gsm = _comm_call("gather_weights", _Comm("gather", [w["w_in"][0].astype(BF16), small_sh]))
    shards = {n: w[n][0].astype(BF16) for n in _MID}
    w_in = _cols(g_in8)
    zpad = jnp.zeros((d, lp - lora), BF16)
    w_all = jnp.concatenate([w_in[:, :3 * dr + lora], zpad, w_in[:, 3 * dr + lora:],
                             jnp.zeros((d, wall - used), BF16)], axis=1)
    sm = _cols(gsm)
    lora_full = sm[:lora]

    def lora_pad(lo, hi):
        rows = lax.broadcasted_iota(jnp.int32, (lp, 1), 0)
        full = jnp.pad(lora_full, ((0, lp - lora), (0, 0)))
        return jnp.where(jnp.logical_and(rows >= lo, rows < hi), full, 0.0)

    conv_w8 = jnp.pad(sm[lora:lora + 3], ((0, SUBLANES - 3), (0, 0)))
    mu_pad = jnp.pad(w["shift_mu"], ((0, 0), (0, lp - lora)))
    wts = dict(
        w_all=w_all, wd=lora_pad(0, ld), wi=lora_pad(ld, ld + li), wg=lora_pad(ld + li, lora), conv_w8=conv_w8,
        mu_pad=mu_pad, norm_mix_w=w["norm_mix_w"], gate_bias=w["gate_bias"], w0=w["w0"], a0=w["a0"],
        k_k=w["k_k"], k_a=w["k_a"], r_k=w["r_k"].reshape(1, dr), lnx_w=w["lnx_w"], lnx_b=w["lnx_b"],
        norm_mlp_w=w["norm_mlp_w"], norm_final_w=w["norm_final_w"].reshape(1, d))

    loss, grad_x, gr, own, received = _local_step(x2, tg2, wts, shards, cfg)

    small_g = dict(norm_mix_w=gr["norm_mix_w"], gate_bias=gr["gate_bias"], shift_mu=gr["mu_pad"][:, :3 * dr + lora],
                   w0=gr["w0"], a0=gr["a0"], k_k=gr["k_k"], k_a=gr["k_a"], r_k=gr["r_k"], lnx_w=gr["lnx_w"],
                   lnx_b=gr["lnx_b"], norm_mlp_w=gr["norm_mlp_w"], norm_final_w=gr["norm_final_w"])
    sizes = [small_g[n].size for n in _SMALL]
    total = sum(sizes) + 1
    prow = _round_up(total, LANES * SUBLANES) // LANES

    def pack(parts):
        flat = jnp.concatenate([p.reshape(-1) for p in parts])
        return jnp.pad(flat, (0, prow * LANES - flat.size)).reshape(prow, LANES)

    g_packed = _all_reduce_small("reduce_small", pack([small_g[n] for n in _SMALL] + [loss[0, :1]]))
    one = jnp.zeros((1,), F32)
    packed = [pack([d_[n] for n in _SMALL] + [one]) for d_ in (w, m, v)]
    sm_out = _adamw("adamw_small", *packed, g_packed)
    loss_out = g_packed.reshape(-1)[total - 1]

    def unpack(flat2d):
        flat = flat2d.reshape(-1)
        out, o = {}, 0
        for n, s in zip(_SMALL, sizes):
            out[n] = flat[o:o + s].reshape(w[n].shape)
            o += s
        return out

    res = [unpack(a) for a in sm_out]

    def shard2d(a):
        return a.reshape(-1, a.shape[-1])

    for n in big:
        outs = _adamw("adamw_" + n, shard2d(w[n]), shard2d(m[n]), shard2d(v[n]), shard2d(own[n]),
                      received[n].reshape(received[n].shape[:1] + shard2d(own[n]).shape))
        for r_, o in zip(res, outs):
            r_[n] = o.reshape(w[n].shape)
    sm_names = ("w_decay_up", "w_iclr_up", "w_gate_up", "conv_w")
    stack = lambda d_: jnp.pad(jnp.concatenate([d_[n][0] for n in sm_names], axis=0),
                               ((0, _round_up(small_rows, SUBLANES) - small_rows), (0, 0)))
    outs = _adamw("adamw_stack", stack(w), stack(m), stack(v), own["small"], received["small"])
    bounds = (0, ld, ld + li, lora, lora + 3)
    for r_, o in zip(res, outs):
        for q, n in enumerate(sm_names):
            r_[n] = o[bounds[q]:bounds[q + 1]].reshape(w[n].shape)

    grad, delta, new_m, new_v = res
    return (loss_out, grad_x[None], *[grad[n] for n in _ORDER], *[delta[n] for n in _ORDER],
            *[new_m[n] for n in _ORDER], *[new_v[n] for n in _ORDER])


def kernel(x, norm_mix_w, w_in, gate_bias, shift_mu, w0, w_decay_up, a0, w_iclr_up, w_gate_up, k_k, k_a, r_k, lnx_w, lnx_b, w_out_a, conv_w, w_out_b, w_out, norm_mlp_w, w_mlp_up, w_mlp_down, norm_final_w, loss_target, m_norm_mix_w, m_w_in, m_gate_bias, m_shift_mu, m_w0, m_w_decay_up, m_a0, m_w_iclr_up, m_w_gate_up, m_k_k, m_k_a, m_r_k, m_lnx_w, m_lnx_b, m_w_out_a, m_conv_w, m_w_out_b, m_w_out, m_norm_mlp_w, m_w_mlp_up, m_w_mlp_down, m_norm_final_w, v_norm_mix_w, v_w_in, v_gate_bias, v_shift_mu, v_w0, v_w_decay_up, v_a0, v_w_iclr_up, v_w_gate_up, v_k_k, v_k_a, v_r_k, v_lnx_w, v_lnx_b, v_w_out_a, v_conv_w, v_w_out_b, v_w_out, v_norm_mlp_w, v_w_mlp_up, v_w_mlp_down, v_norm_final_w):
    w = dict(zip(_ORDER, (norm_mix_w, w_in, gate_bias, shift_mu, w0, w_decay_up, a0, w_iclr_up, w_gate_up, k_k, k_a,
                          r_k, lnx_w, lnx_b, w_out_a, conv_w, w_out_b, w_out, norm_mlp_w, w_mlp_up, w_mlp_down,
                          norm_final_w)))
    m = dict(zip(_ORDER, (m_norm_mix_w, m_w_in, m_gate_bias, m_shift_mu, m_w0, m_w_decay_up, m_a0, m_w_iclr_up,
                          m_w_gate_up, m_k_k, m_k_a, m_r_k, m_lnx_w, m_lnx_b, m_w_out_a, m_conv_w, m_w_out_b,
                          m_w_out, m_norm_mlp_w, m_w_mlp_up, m_w_mlp_down, m_norm_final_w)))
    v = dict(zip(_ORDER, (v_norm_mix_w, v_w_in, v_gate_bias, v_shift_mu, v_w0, v_w_decay_up, v_a0, v_w_iclr_up,
                          v_w_gate_up, v_k_k, v_k_a, v_r_k, v_lnx_w, v_lnx_b, v_w_out_a, v_conv_w, v_w_out_b,
                          v_w_out, v_norm_mlp_w, v_w_mlp_up, v_w_mlp_down, v_norm_final_w)))
    return _step(x, loss_target, w, m, v)
```

```python
import math

import jax
import jax.numpy as jnp
from jax import lax
from jax.experimental import pallas as pl
from jax.experimental.pallas import tpu as pltpu

F32 = jnp.float32
BF16 = jnp.bfloat16
MESH = pl.DeviceIdType.MESH

N_DEV = 8
HEAD = 64
LANES = 128
SUBLANES = 8
CHUNK = 64
RMS_EPS = 1e-5
LNX_EPS = 64e-5
L2_EPS = 1e-12
ADAM_LR = 0.001
ADAM_B1 = 0.9
ADAM_B2 = 0.999
ADAM_EPS = 1e-08
ADAM_WD = 0.01
ADAM_STEP = 10
VMEM_LIMIT = 48 * 1024 * 1024
MAX_FULL_K = 4096
LIGHT_ROWS = (512, 256, 128, 64, 32, 16, 8)


def _cparams(sem):
    return pltpu.CompilerParams(dimension_semantics=sem, vmem_limit_bytes=VMEM_LIMIT)


def _tile(dim, cands):
    for c in cands:
        if c <= dim and dim % c == 0:
            return c
    return dim


def _my_pos():
    return lax.axis_index("x"), lax.axis_index("y"), lax.axis_index("c")


def _peer(pos, r):
    x, y, c = pos
    return (1 - x if r & 4 else x, 1 - y if r & 2 else y, 1 - c if r & 1 else c)


def _slot(pos):
    return 4 * pos[0] + 2 * pos[1] + pos[2]


class _Comm:
    def __init__(self, kind, arrs):
        self.kind, self.arrs, self.n = kind, list(arrs), len(arrs)
        if kind == "gather":
            self.out_shape = [jax.ShapeDtypeStruct((N_DEV,) + a.shape, a.dtype) for a in arrs]
        elif kind == "exchange":
            self.out_shape = [jax.ShapeDtypeStruct((N_DEV - 1,) + a.shape[1:], a.dtype) for a in arrs]
        elif kind == "pair":
            self.out_shape = [jax.ShapeDtypeStruct((N_DEV // 2,) + a.shape[1:], a.dtype) for a in arrs]
        else:
            self.out_shape = [jax.ShapeDtypeStruct((3,) + a.shape[1:], a.dtype) for a in arrs]
        self.scratch = [pltpu.SemaphoreType.DMA((7 * self.n,)), pltpu.SemaphoreType.DMA((7 * self.n,))]
        if kind == "gather":
            self.scratch.append(pltpu.SemaphoreType.DMA((self.n,)))

    def _exchange_copies(self, in_refs, out_refs, sems):
        me = _my_pos()
        x, y, c = me
        cps = []
        for ai in range(self.n):
            if self.kind == "exchange":
                todo = [(in_refs[ai].at[_slot(_peer(me, r))], out_refs[ai].at[r - 1], _peer(me, r), r - 1)
                        for r in range(1, N_DEV)]
            elif self.kind == "pair":
                todo = [(in_refs[ai].at[2 * q + 1 - c], out_refs[ai].at[q], (x, y, 1 - c), q)
                        for q in range(N_DEV // 2)]
            else:
                chips = [(1 - x, y), (x, 1 - y), (1 - x, 1 - y)]
                todo = [(in_refs[ai].at[2 * cx + cy], out_refs[ai].at[j], (cx, cy, c), j)
                        for j, (cx, cy) in enumerate(chips)]
            for src, dst, to, k in todo:
                cps.append(pltpu.make_async_remote_copy(
                    src_ref=src, dst_ref=dst, send_sem=sems[0].at[ai * 7 + k], recv_sem=sems[1].at[ai * 7 + k],
                    device_id=to, device_id_type=MESH))
        return cps

    def _gather_parts(self, in_refs, out_refs, sems):
        x, y, c = _my_pos()
        me, sibling = (x, y, c), (x, y, 1 - c)
        chips = [(1 - x, y), (x, 1 - y), (1 - x, 1 - y)]

        def copy(ai, k, block, to, src=None):
            dst = out_refs[ai].at[_slot(block)]
            return pltpu.make_async_remote_copy(
                src_ref=dst if src is None else src, dst_ref=dst, send_sem=sems[0].at[ai * 7 + k],
                recv_sem=sems[1].at[ai * 7 + k], device_id=to, device_id_type=MESH)

        mine = [pltpu.make_async_copy(in_refs[ai], out_refs[ai].at[_slot(me)], sems[2].at[ai])
                for ai in range(self.n)]
        first = []
        for ai in range(self.n):
            first.append(copy(ai, 0, me, sibling, src=in_refs[ai]))
            first += [copy(ai, 1 + j, me, (*chip, c), src=in_refs[ai]) for j, chip in enumerate(chips)]
        return me, sibling, chips, c, copy, mine, first

    def start(self, in_refs, out_refs, sems):
        if self.kind != "gather":
            for cp in self._exchange_copies(in_refs, out_refs, sems):
                cp.start()
            return
        _, _, _, _, _, mine, first = self._gather_parts(in_refs, out_refs, sems)
        for cp in mine + first:
            cp.start()

    def wait(self, in_refs, out_refs, sems):
        if self.kind != "gather":
            for cp in self._exchange_copies(in_refs, out_refs, sems):
                cp.wait()
            return
        me, sibling, chips, c, copy, mine, first = self._gather_parts(in_refs, out_refs, sems)
        passed = []
        for ai in range(self.n):
            for j, chip in enumerate(chips):
                copy(ai, 1 + j, (*chip, c), me).wait_recv()
                fwd = copy(ai, 4 + j, (*chip, c), sibling)
                fwd.start()
                passed.append(fwd)
        for ai in range(self.n):
            copy(ai, 0, sibling, me).wait_recv()
            for j, chip in enumerate(chips):
                copy(ai, 4 + j, (*chip, 1 - c), me).wait_recv()
        for cp in first + passed:
            cp.wait_send()
        for cp in mine:
            cp.wait()


def _hosted_call(body, name, comm, first, last, *, args, in_specs, out_shape, out_specs, scratch, grid, sem):
    if comm is None:
        return pl.pallas_call(body, name=name, out_shape=out_shape, grid=grid, in_specs=in_specs, out_specs=out_specs,
                              scratch_shapes=scratch, compiler_params=_cparams(sem))(*args)
    ni, no, ns, nc = len(args), len(out_shape), len(scratch), comm.n
    hbm = pl.BlockSpec(memory_space=pl.ANY)

    def hosted(*refs):
        ins, cin = refs[:ni], refs[ni:ni + nc]
        outs, cout = refs[ni + nc:ni + nc + no], refs[ni + nc + no:ni + 2 * nc + no]
        scr, sems = refs[ni + 2 * nc + no:ni + 2 * nc + no + ns], refs[ni + 2 * nc + no + ns:]

        @pl.when(first())
        def _():
            comm.start(cin, cout, sems)

        body(*ins, *outs, *scr)

        @pl.when(last())
        def _():
            comm.wait(cin, cout, sems)

    res = pl.pallas_call(
        hosted, name=name, out_shape=list(out_shape) + comm.out_shape, grid=grid,
        in_specs=list(in_specs) + [hbm] * nc, out_specs=list(out_specs) + [hbm] * nc,
        scratch_shapes=list(scratch) + comm.scratch,
        compiler_params=_cparams(("arbitrary",) * len(grid)))(*args, *comm.arrs)
    return res[:no], res[no:]


_DIMS = {"nn": ((1,), (0,)), "nt": ((1,), (1,)), "tn": ((0,), (0,))}


def _matmul(name, a, b, mode, out_dtypes, epi=None, extras=(), comm=None, out_slabs=False):
    if mode == "nn":
        (m, k), n = a.shape, b.shape[1]
    elif mode == "nt":
        (m, k), n = a.shape, b.shape[0]
    else:
        (k, m), n = a.shape, b.shape[1]
    tm = _tile(m, (1024, 512, 256, 128, 64, 32, 16, 8))
    if k <= MAX_FULL_K:
        tk, tn = k, _tile(n // N_DEV if out_slabs else n, (512, 256, 128))
    else:
        tk, tn = _tile(k, (2048, 1024, 512, 256, 128)), _tile(n, (1024, 512, 256, 128))
    nk = k // tk
    gm, gn = m // tm, n // tn
    a_spec = pl.BlockSpec((tk, tm), lambda i, j, q: (q, i)) if mode == "tn" else pl.BlockSpec((tm, tk), lambda i, j, q: (i, q))
    b_spec = pl.BlockSpec((tn, tk), lambda i, j, q: (j, q)) if mode == "nt" else pl.BlockSpec((tk, tn), lambda i, j, q: (q, j))
    mn_spec = pl.BlockSpec((tm, tn), lambda i, j, q: (i, j))
    per = n // N_DEV // tn if out_slabs else 0
    out_spec = pl.BlockSpec((None, tm, tn), lambda i, j, q: (j // per, i, j % per)) if out_slabs else mn_spec
    ne, no = len(extras), len(out_dtypes)
    dims = (_DIMS[mode], ((), ()))
    keep_t = mode == "tn" and nk == 1 and gn > 1

    def finish(r, extra_refs, out_refs):
        outs = (r,) if epi is None else epi(r, *[e[...] for e in extra_refs])
        for o_ref, o in zip(out_refs, outs):
            o_ref[...] = o.astype(o_ref.dtype)

    def body(a_ref, b_ref, *rest):
        extra_refs, out_refs = rest[:ne], rest[ne:ne + no]
        if keep_t:
            at = rest[ne + no]

            @pl.when(pl.program_id(1) == 0)
            def _():
                at[...] = a_ref[...].T

            part = jnp.dot(at[...], b_ref[...], preferred_element_type=F32)
        else:
            part = lax.dot_general(a_ref[...], b_ref[...], dims, preferred_element_type=F32)
        if nk == 1:
            finish(part, extra_refs, out_refs)
            return
        acc = rest[ne + no]
        q = pl.program_id(2)

        @pl.when(q == 0)
        def _():
            acc[...] = part

        @pl.when(jnp.logical_and(q > 0, q < nk - 1))
        def _():
            acc[...] += part

        @pl.when(q == nk - 1)
        def _():
            finish(acc[...] + part, extra_refs, out_refs)

    def first():
        return jnp.logical_and(jnp.logical_and(pl.program_id(0) == 0, pl.program_id(1) == 0), pl.program_id(2) == 0)

    def last():
        return jnp.logical_and(jnp.logical_and(pl.program_id(0) == gm - 1, pl.program_id(1) == gn - 1),
                               pl.program_id(2) == nk - 1)

    return _hosted_call(
        body, name, comm, first, last,
        args=[a, b, *extras], in_specs=[a_spec, b_spec] + [mn_spec] * ne,
        out_shape=[jax.ShapeDtypeStruct((N_DEV, m, n // N_DEV) if out_slabs else (m, n), dt) for dt in out_dtypes],
        out_specs=[out_spec] * no,
        scratch=[pltpu.VMEM((tm, tn), F32)] if nk > 1 else ([pltpu.VMEM((tm, tk), a.dtype)] if keep_t else []),
        grid=(gm, gn, nk), sem=("parallel", "arbitrary" if keep_t else "parallel", "arbitrary"))


@jax.custom_vjp
def _mm(a, w):
    return jnp.dot(a.astype(BF16), w.astype(BF16), preferred_element_type=F32)


def _mm_fwd(a, w):
    return _mm(a, w), (a, w)


def _mm_bwd(res, ct):
    a, w = res
    ctb = ct.astype(BF16)
    da = lax.dot_general(ctb, w.astype(BF16), (((1,), (1,)), ((), ())), preferred_element_type=F32)
    dw = lax.dot_general(a.astype(BF16), ctb, (((0,), (0,)), ((), ())), preferred_element_type=F32)
    return da, dw


_mm.defvjp(_mm_fwd, _mm_bwd)


def _split3(x):
    hi = x.astype(BF16)
    r1 = x - hi.astype(F32)
    mid = r1.astype(BF16)
    lo = (r1 - mid.astype(F32)).astype(BF16)
    return hi, mid, lo


def _head_ones(width):
    r = lax.broadcasted_iota(jnp.int32, (width, width), 0) // HEAD
    c = lax.broadcasted_iota(jnp.int32, (width, width), 1) // HEAD
    return (r == c).astype(BF16)


@jax.custom_vjp
def _segsum(x):
    ones = _head_ones(x.shape[-1])
    out = None
    for piece in _split3(x):
        t = jnp.dot(piece, ones, preferred_element_type=F32)
        out = t if out is None else out + t
    return out


_segsum.defvjp(lambda x: (_segsum(x), None), lambda _, ct: (_segsum(ct),))


def _softplus(z):
    return jnp.maximum(z, 0.0) + jnp.log(1.0 + jnp.exp(-jnp.abs(z)))


def _sigmoid(z):
    return 1.0 / (1.0 + jnp.exp(-z))


def _rms(x, w):
    ms = jnp.mean(x * x, axis=-1, keepdims=True)
    return x * lax.rsqrt(ms + RMS_EPS) * w


def _row(ref, i):
    return ref[pl.ds(i, 1), :]


def _shift_down(x, prev_ref, n, first):
    rolled = pltpu.roll(x, n, 0)
    rows = lax.broadcasted_iota(jnp.int32, x.shape, 0)
    for q in range(n):
        halo = jnp.where(first, 0.0, _row(prev_ref, SUBLANES - n + q))
        rolled = jnp.where(rows == q, halo, rolled)
    return rolled


def _shift_up(x, next_ref, n, last):
    t = x.shape[0]
    rolled = pltpu.roll(x, t - n, 0)
    rows = lax.broadcasted_iota(jnp.int32, x.shape, 0)
    for q in range(n):
        halo = jnp.where(last, 0.0, _row(next_ref, q))
        rolled = jnp.where(rows == t - n + q, halo, rolled)
    return rolled


def _acc_out(ref, val, first):
    @pl.when(first)
    def _():
        ref[...] = val

    @pl.when(jnp.logical_not(first))
    def _():
        ref[...] += val


def _prep_fn(k, plm, w0, a0, kkw, kaw, wd, wi, wg):
    w_log = -_softplus(-(w0 + _mm(jnp.tanh(plm), wd))) - 0.5
    lw = -jnp.exp(w_log)
    a_g = _sigmoid(a0 + _mm(plm, wi))
    g = _mm(_sigmoid(plm), wg)
    kk = k * kkw
    kk = kk / jnp.maximum(jnp.sqrt(_segsum(kk * kk)), L2_EPS)
    k2 = k * (1.0 + (a_g - 1.0) * kaw)
    return lw, k2, -kk, kk * a_g, g


def _post_fn(y, r, k2, v, g, lnw, lnb, rk):
    mu = _segsum(y) * (1.0 / HEAD)
    yc = y - mu
    var = _segsum(yc * yc) * (1.0 / HEAD)
    yn = yc * lax.rsqrt(var + LNX_EPS) * lnw + lnb
    bonus = _segsum(r * k2 * rk) * v
    return (yn + bonus) * g


def _merge_fn(pga, pgb, ba, bb, ya, yb):
    return _sigmoid(pga + ba) * ya + _sigmoid(pgb + bb) * yb


_NN, _NT, _TN = ((2,), (1,)), ((2,), (2,)), ((1,), (1,))


def _dot3(a, b, dims):
    ah = a.astype(BF16)
    al = (a - ah.astype(F32)).astype(BF16)
    bh = b.astype(BF16)
    bl = (b - bh.astype(F32)).astype(BF16)
    dg = lambda p, q: lax.dot_general(p, q, (dims, ((0,), (0,))), preferred_element_type=F32)
    (ca,), (cb_,) = dims
    if a.shape[ca] % LANES:
        return dg(ah, bh) + (dg(ah, bl) + dg(al, bh))
    cross = dg(jnp.concatenate([ah, al], axis=ca), jnp.concatenate([bl, bh], axis=cb_))
    return dg(ah, bh) + cross


@jax.custom_vjp
def _dnn(a, b):
    return _dot3(a, b, _NN)


@jax.custom_vjp
def _dnt(a, b):
    return _dot3(a, b, _NT)


@jax.custom_vjp
def _dtn(a, b):
    return _dot3(a, b, _TN)


_dnn.defvjp(lambda a, b: (_dnn(a, b), (a, b)), lambda res, ct: (_dnt(ct, res[1]), _dtn(res[0], ct)))
_dnt.defvjp(lambda a, b: (_dnt(a, b), (a, b)), lambda res, ct: (_dnn(ct, res[1]), _dtn(ct, res[0])))
_dtn.defvjp(lambda a, b: (_dtn(a, b), (a, b)), lambda res, ct: (_dnt(res[1], ct), _dnn(res[0], ct)))


@jax.custom_vjp
def _unit_lower_inverse(low):
    n = low.shape[-1]
    ri = lax.broadcasted_iota(jnp.int32, low.shape, 1)
    ci = lax.broadcasted_iota(jnp.int32, low.shape, 2)
    inv = (ri == ci).astype(F32) + low
    pw = low
    for _ in range(int(math.log2(n // 2)) - 1):
        pw = _dnn(pw, pw)
        inv = inv + _dnn(inv, pw)
    return inv


def _unit_lower_inverse_bwd(inv, ct):
    return (_dnt(_dtn(inv, ct), inv),)


_unit_lower_inverse.defvjp(lambda low: (_unit_lower_inverse(low),) * 2, _unit_lower_inverse_bwd)


def _chunk_fn(s, r, lw, k, v, a, b):
    np_, c = r.shape[0], r.shape[1]
    c2 = 2 * c
    ri = lax.broadcasted_iota(jnp.int32, (np_, c, c), 1)
    ci = lax.broadcasted_iota(jnp.int32, (np_, c, c), 2)
    tri = (ri >= ci).astype(F32)
    cum = _dnn(tri, lw)
    tot = jnp.sum(lw, axis=1, keepdims=True)
    g_in, g_inv, g_out = jnp.exp(cum), jnp.exp(-cum), jnp.exp(tot - cum)
    lane_head = lax.broadcasted_iota(jnp.int32, (1, 2, 1, LANES), 3) // HEAD
    which = lax.broadcasted_iota(jnp.int32, (1, 2, 1, LANES), 1)
    hmask = (lane_head == which).astype(F32)

    def st(x):
        return (x[:, None] * hmask).reshape(np_, c2, LANES)

    r2, a2 = st(r * g_in), st(a * jnp.exp(cum - lw))
    b2, k2, v2 = st(b * g_inv), st(k * g_inv), st(v)
    bo2, ko2 = st(b * g_out), st(k * g_out)
    r2i = lax.broadcasted_iota(jnp.int32, (np_, c2, c2), 1)
    c2i = lax.broadcasted_iota(jnp.int32, (np_, c2, c2), 2)
    same = (r2i >= c) == (c2i >= c)
    strict = jnp.logical_and(same, r2i > c2i)
    incl = jnp.logical_and(same, r2i >= c2i)
    lab = jnp.where(strict, _dnt(a2, b2), 0.0)
    lak = jnp.where(strict, _dnt(a2, k2), 0.0)
    mrb = jnp.where(incl, _dnt(r2, b2), 0.0)
    mrk = jnp.where(incl, _dnt(r2, k2), 0.0)
    x2 = _dnt(a2, s) + _dnn(lak, v2)
    u2 = _dnn(_unit_lower_inverse(lab), x2)
    y2 = _dnt(r2, s) + _dnn(mrb, u2) + _dnn(mrk, v2)
    y = jnp.sum(y2.reshape(np_, 2, c, LANES), axis=1)
    s_new = s * jnp.exp(tot) + _dtn(u2, bo2) + _dtn(v2, ko2)
    return y, s_new


def _norm_fwd(name, x, add, w, want_sum):
    t, d = x.shape
    tt = _tile(t, (128, 64, 32, 16, 8))
    row = pl.BlockSpec((tt, d), lambda i: (i, 0))
    par = pl.BlockSpec((1, d), lambda i: (0, 0))
    has_add = add is not None

    def body(*refs):
        x_ref = refs[0]
        add_ref = refs[1] if has_add else None
        w_ref = refs[1 + has_add]
        outs = refs[2 + has_add:]
        h = x_ref[...] + add_ref[...] if has_add else x_ref[...]
        if want_sum:
            outs[0][...] = h
        outs[-1][...] = _rms(h, w_ref[...]).astype(BF16)

    out_shape = ([jax.ShapeDtypeStruct((t, d), F32)] if want_sum else []) + [jax.ShapeDtypeStruct((t, d), BF16)]
    return pl.pallas_call(
        body, name=name, out_shape=out_shape, grid=(t // tt,),
        in_specs=[row] + ([row] if has_add else []) + [par],
        out_specs=[row] * len(out_shape),
        compiler_params=_cparams(("arbitrary",)),
    )(*([x] + ([add] if has_add else []) + [w]))


def _norm_bwd(name, xin, dy, dres, w):
    t, d = xin.shape
    tt = _tile(t, (128, 64, 32, 16, 8))
    row = pl.BlockSpec((tt, d), lambda i: (i, 0))
    par = pl.BlockSpec((1, d), lambda i: (0, 0))

    def body(x_ref, dy_ref, dres_ref, w_ref, dx_ref, dxb_ref, dw_ref):
        _, vjp = jax.vjp(_rms, x_ref[...], w_ref[...])
        dx, dw = vjp(dy_ref[...])
        dx = dx + dres_ref[...]
        dx_ref[...] = dx
        dxb_ref[...] = dx.astype(BF16)
        _acc_out(dw_ref, dw, pl.program_id(0) == 0)

    return pl.pallas_call(
        body, name=name,
        out_shape=[jax.ShapeDtypeStruct((t, d), F32), jax.ShapeDtypeStruct((t, d), BF16),
                   jax.ShapeDtypeStruct((1, d), F32)],
        grid=(t // tt,), in_specs=[row, row, row, par], out_specs=[row, row, par],
        compiler_params=_cparams(("arbitrary",)),
    )(xin, dy, dres, w)


def _final(name, h1, md, target, w):
    t, d = h1.shape
    tt = _tile(t, (128, 64, 32, 16, 8))
    row = pl.BlockSpec((tt, d), lambda i: (i, 0))
    par = pl.BlockSpec((1, d), lambda i: (0, 0))
    one = pl.BlockSpec((1, LANES), lambda i: (0, 0))

    def body(h1_ref, md_ref, tg_ref, w_ref, loss_ref, dh_ref, dhb_ref, dw_ref):
        tg = tg_ref[...]

        def f(h, wv):
            err = _rms(h, wv) - tg
            return 0.5 * jnp.sum(jnp.mean(err * err, axis=-1, keepdims=True), axis=0, keepdims=True)

        loss, vjp = jax.vjp(f, h1_ref[...] + md_ref[...], w_ref[...])
        dh, dw = vjp(jnp.ones((1, 1), F32))
        dh_ref[...] = dh
        dhb_ref[...] = dh.astype(BF16)
        first = pl.program_id(0) == 0
        _acc_out(dw_ref, dw, first)
        _acc_out(loss_ref, jnp.broadcast_to(loss, (1, LANES)), first)

    return pl.pallas_call(
        body, name=name,
        out_shape=[jax.ShapeDtypeStruct((1, LANES), F32), jax.ShapeDtypeStruct((t, d), F32),
                   jax.ShapeDtypeStruct((t, d), BF16), jax.ShapeDtypeStruct((1, d), F32)],
        grid=(t // tt,), in_specs=[row, row, row, par], out_specs=[one, row, row, par],
        compiler_params=_cparams(("arbitrary",)),
    )(h1, md, target, w)


def _halo_specs(tt, cb, nrow8, col_of):
    prev = pl.BlockSpec((SUBLANES, cb), lambda i, j: (jnp.maximum(i * (tt // SUBLANES) - 1, 0), col_of(j)))
    nxt = pl.BlockSpec((SUBLANES, cb), lambda i, j: (jnp.minimum((i + 1) * (tt // SUBLANES), nrow8 - 1), col_of(j)))
    return prev, nxt


def _mix_fwd(name, p_all, mu, width, cb):
    t = p_all.shape[0]
    tt = _tile(t, LIGHT_ROWS)
    main = pl.BlockSpec((tt, cb), lambda i, j: (i, j))
    prev, _ = _halo_specs(tt, cb, t // SUBLANES, lambda j: j)
    par = pl.BlockSpec((1, cb), lambda i, j: (0, j))

    def body(p_ref, prev_ref, mu_ref, o_ref):
        p = p_ref[...]
        o_ref[...] = p + (_shift_down(p, prev_ref, 1, pl.program_id(0) == 0) - p) * mu_ref[...]

    return pl.pallas_call(
        body, name=name, out_shape=jax.ShapeDtypeStruct((t, width), F32),
        grid=(t // tt, width // cb), in_specs=[main, prev, par], out_specs=main,
        compiler_params=_cparams(("arbitrary", "arbitrary")),
    )(p_all, p_all, mu)


def _mix_bwd(name, dpm_list, p_all, col0, mu, cb):
    t, width = dpm_list[0].shape
    tt = _tile(t, LIGHT_ROWS)
    n8 = t // SUBLANES
    nl = len(dpm_list)
    main = pl.BlockSpec((tt, cb), lambda j, i: (i, j))
    nxt = pl.BlockSpec((SUBLANES, cb), lambda j, i: (jnp.minimum((i + 1) * (tt // SUBLANES), n8 - 1), j))
    p_main = pl.BlockSpec((tt, cb), lambda j, i: (i, col0 + j))
    p_prev = pl.BlockSpec((SUBLANES, cb), lambda j, i: (jnp.maximum(i * (tt // SUBLANES) - 1, 0), col0 + j))
    par = pl.BlockSpec((1, cb), lambda j, i: (0, j))
    nt_ = t // tt

    def body(*refs):
        d_refs, dn_refs = refs[:nl], refs[nl:2 * nl]
        p_ref, pp_ref, mu_ref, dp_ref, dmu_ref, nx_scr = refs[2 * nl:]
        i = pl.program_id(1)
        dpm = d_refs[0][...]
        nx = dn_refs[0][...]
        for q in range(1, nl):
            dpm = dpm + d_refs[q][...]
            nx = nx + dn_refs[q][...]
        nx_scr[...] = nx
        mu_v = mu_ref[...]
        up = _shift_up(dpm, nx_scr, 1, i == nt_ - 1)
        dp_ref[...] = (dpm * (1.0 - mu_v) + up * mu_v).astype(BF16)
        p = p_ref[...]
        diff = _shift_down(p, pp_ref, 1, i == 0) - p
        _acc_out(dmu_ref, jnp.sum(dpm * diff, axis=0, keepdims=True), i == 0)

    return pl.pallas_call(
        body, name=name,
        out_shape=[jax.ShapeDtypeStruct((t, width), BF16), jax.ShapeDtypeStruct((1, width), F32)],
        grid=(width // cb, nt_),
        in_specs=[main] * nl + [nxt] * nl + [p_main, p_prev, par],
        out_specs=[main, par],
        scratch_shapes=[pltpu.VMEM((SUBLANES, cb), F32)],
        compiler_params=_cparams(("arbitrary", "arbitrary")),
    )(*dpm_list, *dpm_list, p_all, p_all, mu)


def _prep_fwd(name, pm, cfg, w0, a0, kkw, kaw, wd, wi, wg):
    t = pm.shape[0]
    dr, lp, cb = cfg["dr"], cfg["lp"], cfg["cb"]
    tt = _tile(t, (256, 128, 64, 32, 16, 8))
    nj = dr // cb
    kspec = pl.BlockSpec((tt, cb), lambda j, i: (i, nj + j))
    lspec = pl.BlockSpec((tt, lp), lambda j, i: (i, 3 * dr // lp))
    par = pl.BlockSpec((1, cb), lambda j, i: (0, j))
    wspec = pl.BlockSpec((lp, cb), lambda j, i: (0, j))
    out = pl.BlockSpec((tt, cb), lambda j, i: (i, j))

    def body(k_ref, l_ref, w0_ref, a0_ref, kk_ref, ka_ref, wd_ref, wi_ref, wg_ref, *outs):
        vals = _prep_fn(k_ref[...], l_ref[...], w0_ref[...], a0_ref[...], kk_ref[...], ka_ref[...],
                        wd_ref[...], wi_ref[...], wg_ref[...])
        for o_ref, val in zip(outs, vals):
            o_ref[...] = val

    return pl.pallas_call(
        body, name=name, out_shape=[jax.ShapeDtypeStruct((t, dr), F32)] * 5,
        grid=(nj, t // tt), in_specs=[kspec, lspec, par, par, par, par, wspec, wspec, wspec],
        out_specs=[out] * 5, compiler_params=_cparams(("arbitrary", "arbitrary")),
    )(pm, pm, w0, a0, kkw, kaw, wd, wi, wg)


def _prep_bwd(name, pm, cfg, w0, a0, kkw, kaw, wd, wi, wg, cts, dr_parts, dv_parts):
    t = pm.shape[0]
    dr, lp, cb = cfg["dr"], cfg["lp"], cfg["cb"]
    tt = _tile(t, (256, 128, 64, 32, 16, 8))
    nj = dr // cb
    kspec = pl.BlockSpec((tt, cb), lambda j, i: (i, nj + j))
    lspec = pl.BlockSpec((tt, lp), lambda j, i: (i, 3 * dr // lp))
    par = pl.BlockSpec((1, cb), lambda j, i: (0, j))
    wspec = pl.BlockSpec((lp, cb), lambda j, i: (0, j))
    blk = pl.BlockSpec((tt, cb), lambda j, i: (i, j))
    dpl_spec = pl.BlockSpec((None, tt, lp), lambda j, i: (j, i, 0))

    def body(k_ref, l_ref, w0_ref, a0_ref, kk_ref, ka_ref, wd_ref, wi_ref, wg_ref,
             dlw_ref, dk2a_ref, dk2b_ref, da_ref, db_ref, dg_ref, dr0_ref, dr1_ref, dv0_ref, dv1_ref,
             dpr_ref, dpk_ref, dpv_ref, dpl_ref, dw0_ref, da0_ref, dkk_ref, dka_ref, dwd_ref, dwi_ref, dwg_ref):
        _, vjp = jax.vjp(_prep_fn, k_ref[...], l_ref[...], w0_ref[...], a0_ref[...], kk_ref[...], ka_ref[...],
                         wd_ref[...], wi_ref[...], wg_ref[...])
        dk, dpl, dw0, da0, dkk, dka, dwd, dwi, dwg = vjp(
            (dlw_ref[...], dk2a_ref[...] + dk2b_ref[...], da_ref[...], db_ref[...], dg_ref[...]))
        dpr_ref[...] = dr0_ref[...] + dr1_ref[...]
        dpv_ref[...] = dv0_ref[...] + dv1_ref[...]
        dpk_ref[...] = dk
        dpl_ref[...] = dpl
        first = pl.program_id(1) == 0
        for ref, val in ((dw0_ref, dw0), (da0_ref, da0), (dkk_ref, dkk), (dka_ref, dka),
                         (dwd_ref, dwd), (dwi_ref, dwi), (dwg_ref, dwg)):
            _acc_out(ref, val, first)

    out_shape = ([jax.ShapeDtypeStruct((t, dr), F32)] * 3 + [jax.ShapeDtypeStruct((nj, t, lp), F32)]
                 + [jax.ShapeDtypeStruct((1, dr), F32)] * 4 + [jax.ShapeDtypeStruct((lp, dr), F32)] * 3)
    return pl.pallas_call(
        body, name=name, out_shape=out_shape, grid=(nj, t // tt),
        in_specs=[kspec, lspec, par, par, par, par, wspec, wspec, wspec] + [blk] * 10,
        out_specs=[blk] * 3 + [dpl_spec] + [par] * 4 + [wspec] * 3,
        compiler_params=_cparams(("arbitrary", "arbitrary")),
    )(pm, pm, w0, a0, kkw, kaw, wd, wi, wg, *cts, *dr_parts, *dv_parts)


def _post_specs(t, cfg):
    dr, cb = cfg["dr"], cfg["cb"]
    tt = _tile(t, (256, 128, 64, 32, 16, 8))
    nj = dr // cb
    blk = pl.BlockSpec((tt, cb), lambda j, i: (i, j))
    rspec = pl.BlockSpec((tt, cb), lambda j, i: (i, j))
    vspec = pl.BlockSpec((tt, cb), lambda j, i: (i, 2 * nj + j))
    par = pl.BlockSpec((1, cb), lambda j, i: (0, j))
    return tt, nj, blk, rspec, vspec, par


def _post_fwd(name, y, pm, k2, g, lnw, lnb, rk, cfg):
    t = y.shape[0]
    tt, nj, blk, rspec, vspec, par = _post_specs(t, cfg)

    def body(y_ref, r_ref, k_ref, v_ref, g_ref, lw_ref, lb_ref, rk_ref, o_ref):
        o_ref[...] = _post_fn(y_ref[...], r_ref[...], k_ref[...], v_ref[...], g_ref[...],
                              lw_ref[...], lb_ref[...], rk_ref[...]).astype(BF16)

    return pl.pallas_call(
        body, name=name, out_shape=jax.ShapeDtypeStruct((t, cfg["dr"]), BF16), grid=(nj, t // tt),
        in_specs=[blk, rspec, blk, vspec, blk, par, par, par], out_specs=blk,
        compiler_params=_cparams(("arbitrary", "arbitrary")),
    )(y, pm, k2, pm, g, lnw, lnb, rk)


def _post_bwd(name, y, pm, k2, g, lnw, lnb, rk, dout, cfg):
    t = y.shape[0]
    tt, nj, blk, rspec, vspec, par = _post_specs(t, cfg)

    def body(y_ref, r_ref, k_ref, v_ref, g_ref, lw_ref, lb_ref, rk_ref, do_ref,
             dy_ref, dr_ref, dk_ref, dv_ref, dg_ref, dlw_ref, dlb_ref, drk_ref):
        _, vjp = jax.vjp(_post_fn, y_ref[...], r_ref[...], k_ref[...], v_ref[...], g_ref[...],
                         lw_ref[...], lb_ref[...], rk_ref[...])
        dy, dr, dk, dv, dg, dlw, dlb, drk = vjp(do_ref[...])
        for ref, val in ((dy_ref, dy), (dr_ref, dr), (dk_ref, dk), (dv_ref, dv), (dg_ref, dg)):
            ref[...] = val
        first = pl.program_id(1) == 0
        for ref, val in ((dlw_ref, dlw), (dlb_ref, dlb), (drk_ref, drk)):
            _acc_out(ref, val, first)

    dr = cfg["dr"]
    return pl.pallas_call(
        body, name=name,
        out_shape=[jax.ShapeDtypeStruct((t, dr), F32)] * 5 + [jax.ShapeDtypeStruct((1, dr), F32)] * 3,
        grid=(nj, t // tt),
        in_specs=[blk, rspec, blk, vspec, blk, par, par, par, blk],
        out_specs=[blk] * 5 + [par] * 3,
        compiler_params=_cparams(("arbitrary", "arbitrary")),
    )(y, pm, k2, pm, g, lnw, lnb, rk, dout)


def _conv_specs(t, cfg):
    dc, cb = cfg["dc"], cfg["cb"]
    tt = _tile(t, LIGHT_ROWS)
    nj = dc // cb
    c0 = cfg["off_conv"] // cb
    n8 = t // SUBLANES

    def sect(s):
        col = lambda j: c0 + s * nj + j
        main = pl.BlockSpec((tt, cb), lambda j, i: (i, col(j)))
        prev = pl.BlockSpec((SUBLANES, cb), lambda j, i: (jnp.maximum(i * (tt // SUBLANES) - 1, 0), col(j)))
        nxt = pl.BlockSpec((SUBLANES, cb), lambda j, i: (jnp.minimum((i + 1) * (tt // SUBLANES), n8 - 1), col(j)))
        return main, prev, nxt

    blk = pl.BlockSpec((tt, cb), lambda j, i: (i, j))
    wspec = pl.BlockSpec((SUBLANES, cb), lambda j, i: (0, j))
    return tt, nj, n8, sect, blk, wspec


def _conv_fwd(name, p_all, cw8, cfg):
    t = p_all.shape[0]
    tt, nj, n8, sect, blk, wspec = _conv_specs(t, cfg)
    (bm, _, _), (cm, cp, _), (um, up, _) = sect(0), sect(1), sect(2)

    def body(b_ref, c_ref, cp_ref, u_ref, up_ref, w_ref, o_ref, zp_scr):
        first = pl.program_id(1) == 0
        z = c_ref[...] * u_ref[...]
        zp_scr[...] = cp_ref[...] * up_ref[...]
        o = _row(w_ref, 2) * z + _row(w_ref, 1) * _shift_down(z, zp_scr, 1, first) \
            + _row(w_ref, 0) * _shift_down(z, zp_scr, 2, first)
        o_ref[...] = (b_ref[...] * o).astype(BF16)

    return pl.pallas_call(
        body, name=name, out_shape=jax.ShapeDtypeStruct((t, cfg["dc"]), BF16), grid=(nj, t // tt),
        in_specs=[bm, cm, cp, um, up, wspec], out_specs=blk,
        scratch_shapes=[pltpu.VMEM((SUBLANES, blk.block_shape[1]), F32)],
        compiler_params=_cparams(("arbitrary", "arbitrary")),
    )(p_all, p_all, p_all, p_all, p_all, cw8)


def _conv_bwd(name, p_all, cw8, dyb, cfg):
    t = p_all.shape[0]
    tt, nj, n8, sect, blk, wspec = _conv_specs(t, cfg)
    (bm, _, bn), (cm, cp, _), (um, up, _) = sect(0), sect(1), sect(2)
    cb = blk.block_shape[1]
    dnxt = pl.BlockSpec((SUBLANES, cb), lambda j, i: (jnp.minimum((i + 1) * (tt // SUBLANES), n8 - 1), j))
    nt_ = t // tt

    def body(b_ref, bn_ref, c_ref, cp_ref, u_ref, up_ref, w_ref, d_ref, dn_ref,
             db_ref, dc_ref, du_ref, dw_ref, zp_scr, don_scr):
        i = pl.program_id(1)
        first, last = i == 0, i == nt_ - 1
        c, u, b, dy = c_ref[...], u_ref[...], b_ref[...], d_ref[...]
        z = c * u
        zp_scr[...] = cp_ref[...] * up_ref[...]
        z1 = _shift_down(z, zp_scr, 1, first)
        z2 = _shift_down(z, zp_scr, 2, first)
        w0, w1, w2 = _row(w_ref, 0), _row(w_ref, 1), _row(w_ref, 2)
        o = w2 * z + w1 * z1 + w0 * z2
        do = dy * b
        don_scr[...] = dn_ref[...] * bn_ref[...]
        dz = w2 * do + w1 * _shift_up(do, don_scr, 1, last) + w0 * _shift_up(do, don_scr, 2, last)
        db_ref[...] = (dy * o).astype(BF16)
        dc_ref[...] = (dz * u).astype(BF16)
        du_ref[...] = (dz * c).astype(BF16)
        rows = lax.broadcasted_iota(jnp.int32, (SUBLANES, cb), 0)
        s0 = jnp.sum(do * z2, axis=0, keepdims=True)
        s1 = jnp.sum(do * z1, axis=0, keepdims=True)
        s2 = jnp.sum(do * z, axis=0, keepdims=True)
        dw = jnp.where(rows == 0, s0, jnp.where(rows == 1, s1, jnp.where(rows == 2, s2, 0.0)))
        _acc_out(dw_ref, dw, first)

    dc = cfg["dc"]
    return pl.pallas_call(
        body, name=name,
        out_shape=[jax.ShapeDtypeStruct((t, dc), BF16)] * 3 + [jax.ShapeDtypeStruct((SUBLANES, dc), F32)],
        grid=(nj, nt_),
        in_specs=[bm, bn, cm, cp, um, up, wspec, blk, dnxt],
        out_specs=[blk] * 3 + [wspec],
        scratch_shapes=[pltpu.VMEM((SUBLANES, cb), F32), pltpu.VMEM((SUBLANES, cb), F32)],
        compiler_params=_cparams(("arbitrary", "arbitrary")),
    )(p_all, p_all, p_all, p_all, p_all, p_all, cw8, dyb, dyb)


def _merge_specs(t, cfg):
    d, cb = cfg["d"], cfg["cb"]
    tt = _tile(t, LIGHT_ROWS)
    nj = d // cb
    g0 = cfg["off_gate"] // cb
    ga = pl.BlockSpec((tt, cb), lambda j, i: (i, g0 + j))
    gb = pl.BlockSpec((tt, cb), lambda j, i: (i, g0 + nj + j))
    ba = pl.BlockSpec((1, cb), lambda j, i: (0, j))
    bb = pl.BlockSpec((1, cb), lambda j, i: (0, nj + j))
    blk = pl.BlockSpec((tt, cb), lambda j, i: (i, j))
    return tt, nj, ga, gb, ba, bb, blk


def _merge_fwd(name, p_all, bias, ya, yb, cfg):
    t = p_all.shape[0]
    tt, nj, ga, gb, ba, bb, blk = _merge_specs(t, cfg)

    def body(ga_ref, gb_ref, ba_ref, bb_ref, ya_ref, yb_ref, o_ref):
        o_ref[...] = _merge_fn(ga_ref[...], gb_ref[...], ba_ref[...], bb_ref[...],
                               ya_ref[...], yb_ref[...]).astype(BF16)

    return pl.pallas_call(
        body, name=name, out_shape=jax.ShapeDtypeStruct((t, cfg["d"]), BF16), grid=(nj, t // tt),
        in_specs=[ga, gb, ba, bb, blk, blk], out_specs=blk,
        compiler_params=_cparams(("arbitrary", "arbitrary")),
    )(p_all, p_all, bias, bias, ya, yb)


def _merge_bwd(name, p_all, bias, ya, yb, dm, cfg):
    t = p_all.shape[0]
    tt, nj, ga, gb, ba, bb, blk = _merge_specs(t, cfg)

    def body(ga_ref, gb_ref, ba_ref, bb_ref, ya_ref, yb_ref, dm_ref,
             dga_ref, dgb_ref, dya_ref, dyb_ref, dba_ref, dbb_ref):
        _, vjp = jax.vjp(_merge_fn, ga_ref[...], gb_ref[...], ba_ref[...], bb_ref[...], ya_ref[...], yb_ref[...])
        dga, dgb, dba, dbb, dya, dyb = vjp(dm_ref[...])
        for ref, val in ((dga_ref, dga), (dgb_ref, dgb), (dya_ref, dya), (dyb_ref, dyb)):
            ref[...] = val.astype(BF16)
        first = pl.program_id(1) == 0
        _acc_out(dba_ref, dba, first)
        _acc_out(dbb_ref, dbb, first)

    d = cfg["d"]
    par = pl.BlockSpec((1, blk.block_shape[1]), lambda j, i: (0, j))
    return pl.pallas_call(
        body, name=name,
        out_shape=[jax.ShapeDtypeStruct((t, d), BF16)] * 4 + [jax.ShapeDtypeStruct((1, d), F32)] * 2,
        grid=(nj, t // tt),
        in_specs=[ga, gb, ba, bb, blk, blk, blk], out_specs=[blk] * 4 + [par] * 2,
        compiler_params=_cparams(("arbitrary", "arbitrary")),
    )(p_all, p_all, bias, bias, ya, yb, dm)


PAIRS = 8


def _pair_stack(ref, pairs):
    return jnp.stack([ref[:, p * LANES:(p + 1) * LANES] for p in range(pairs)])


def _pair_store(ref, val):
    for p in range(val.shape[0]):
        ref[:, p * LANES:(p + 1) * LANES] = val[p]


def _rec_specs(t, cfg, rev):
    dr = cfg["dr"]
    nc = t // CHUNK
    hp = dr // LANES
    pairs = _tile(hp, (PAIRS, 2, 1))
    ng = hp // pairs
    w = LANES * pairs
    ch = (lambda c: nc - 1 - c) if rev else (lambda c: c)
    slab = pl.BlockSpec((CHUNK, w), lambda h, c: (ch(c), h))
    vspec = pl.BlockSpec((CHUNK, w), lambda h, c: (ch(c), 2 * ng + h))
    sspec = pl.BlockSpec((None, pairs, LANES, LANES), lambda h, c: (ch(c), h, 0, 0))
    first = lambda: jnp.logical_and(pl.program_id(0) == 0, pl.program_id(1) == 0)
    last = lambda: jnp.logical_and(pl.program_id(0) == ng - 1, pl.program_id(1) == nc - 1)
    return nc, hp, pairs, ng, slab, vspec, sspec, first, last


def _rec_fwd(name, pm, lw, k2, a, b, cfg, comm=None):
    t = pm.shape[0]
    nc, hp, pairs, ng, slab, vspec, sspec, first, last = _rec_specs(t, cfg, False)

    def body(r_ref, lw_ref, k_ref, v_ref, a_ref, b_ref, y_ref, s_ref, s_scr):
        @pl.when(pl.program_id(1) == 0)
        def _():
            s_scr[...] = jnp.zeros_like(s_scr)

        s = s_scr[...]
        s_ref[...] = s
        y, s_new = _chunk_fn(s, *[_pair_stack(ref, pairs) for ref in (r_ref, lw_ref, k_ref, v_ref, a_ref, b_ref)])
        _pair_store(y_ref, y)
        s_scr[...] = s_new

    return _hosted_call(
        body, name, comm, first, last, args=[pm, lw, k2, pm, a, b],
        in_specs=[slab, slab, slab, vspec, slab, slab],
        out_shape=[jax.ShapeDtypeStruct((t, cfg["dr"]), F32), jax.ShapeDtypeStruct((nc, hp, LANES, LANES), F32)],
        out_specs=[slab, sspec], scratch=[pltpu.VMEM((pairs, LANES, LANES), F32)], grid=(ng, nc),
        sem=("arbitrary", "arbitrary"))


def _rec_bwd(name, pm, lw, k2, a, b, s_chk, dy, cfg, comm=None):
    t = pm.shape[0]
    nc, hp, pairs, ng, slab, vspec, sspec, first, last = _rec_specs(t, cfg, True)

    def body(r_ref, lw_ref, k_ref, v_ref, a_ref, b_ref, s_ref, dy_ref,
             dr_ref, dlw_ref, dk_ref, dv_ref, da_ref, db_ref, ds_scr):
        @pl.when(pl.program_id(1) == 0)
        def _():
            ds_scr[...] = jnp.zeros_like(ds_scr)

        _, vjp = jax.vjp(_chunk_fn, s_ref[...],
                         *[_pair_stack(ref, pairs) for ref in (r_ref, lw_ref, k_ref, v_ref, a_ref, b_ref)])
        ds, dr, dlw, dk, dv, da, db = vjp((_pair_stack(dy_ref, pairs), ds_scr[...]))
        ds_scr[...] = ds
        for ref, val in ((dr_ref, dr), (dlw_ref, dlw), (dk_ref, dk), (dv_ref, dv), (da_ref, da), (db_ref, db)):
            _pair_store(ref, val)

    return _hosted_call(
        body, name, comm, first, last, args=[pm, lw, k2, pm, a, b, s_chk, dy],
        in_specs=[slab, slab, slab, vspec, slab, slab, sspec, slab],
        out_shape=[jax.ShapeDtypeStruct((t, cfg["dr"]), F32)] * 6, out_specs=[slab] * 6,
        scratch=[pltpu.VMEM((pairs, LANES, LANES), F32)], grid=(ng, nc), sem=("arbitrary", "arbitrary"))


def _comm_call(name, comm):
    n = comm.n
    hbm = pl.BlockSpec(memory_space=pl.ANY)

    def body(*refs):
        comm.start(refs[:n], refs[n:2 * n], refs[2 * n:])
        comm.wait(refs[:n], refs[n:2 * n], refs[2 * n:])

    return pl.pallas_call(body, name=name, out_shape=comm.out_shape, in_specs=[hbm] * n, out_specs=[hbm] * n,
                          scratch_shapes=comm.scratch)(*comm.arrs)


def _all_reduce_small(name, v):
    rows = v.shape[0]
    vm = pl.BlockSpec(memory_space=pltpu.VMEM)

    def body(x_ref, out_ref, buf, send_sems, recv_sems):
        x, y, c = _my_pos()
        me, sibling = (x, y, c), (x, y, 1 - c)
        chips = [(1 - x, y), (x, 1 - y), (1 - x, 1 - y)]

        def copy(k, block, to, src=None):
            px, py, pc = block
            dst = buf.at[4 * px + 2 * py + pc]
            return pltpu.make_async_remote_copy(
                src_ref=dst if src is None else src, dst_ref=dst,
                send_sem=send_sems.at[k], recv_sem=recv_sems.at[k], device_id=to, device_id_type=MESH)

        buf[4 * x + 2 * y + c] = x_ref[...]
        first = [copy(0, me, sibling, src=x_ref)]
        first += [copy(1 + j, me, (*chip, c), src=x_ref) for j, chip in enumerate(chips)]
        for cp in first:
            cp.start()
        passed = [copy(4 + j, (*chip, c), sibling) for j, chip in enumerate(chips)]
        for j, chip in enumerate(chips):
            copy(1 + j, (*chip, c), me).wait_recv()
            passed[j].start()
        copy(0, sibling, me).wait_recv()
        for j, chip in enumerate(chips):
            copy(4 + j, (*chip, 1 - c), me).wait_recv()
        for cp in first + passed:
            cp.wait_send()
        acc = buf[0]
        for d in range(1, N_DEV):
            acc = acc + buf[d]
        out_ref[...] = acc

    return pl.pallas_call(
        body, name=name, out_shape=jax.ShapeDtypeStruct(v.shape, F32),
        in_specs=[vm], out_specs=vm,
        scratch_shapes=[pltpu.VMEM((N_DEV, rows, LANES), F32), pltpu.SemaphoreType.DMA((7,)),
                        pltpu.SemaphoreType.DMA((7,))],
    )(v)


def _pair_sum(name, slabs, got, core):
    _, rows, cols = slabs.shape
    nq = got.shape[0]
    rb = _tile(rows, (256, 128, 64, 32, 16, 8))
    mine = pl.BlockSpec((None, rb, cols), lambda q, j, c_ref: (2 * q + c_ref[0], j, 0))
    blk = pl.BlockSpec((None, rb, cols), lambda q, j, c_ref: (q, j, 0))

    def body(c_ref, a_ref, b_ref, o_ref):
        o_ref[...] = (a_ref[...].astype(F32) + b_ref[...].astype(F32)).astype(o_ref.dtype)

    return pl.pallas_call(
        body, name=name, out_shape=jax.ShapeDtypeStruct(got.shape, got.dtype),
        grid_spec=pltpu.PrefetchScalarGridSpec(num_scalar_prefetch=1, grid=(nq, rows // rb),
                                               in_specs=[mine, blk], out_specs=blk),
        compiler_params=_cparams(("arbitrary", "arbitrary")))(core, slabs, got)


def _adamw(name, w, m, v, g_own, g_recv=None):
    rows, cols = w.shape
    nr = g_recv.shape[0] if g_recv is not None else 0
    per_el = 4 * 3 + g_own.dtype.itemsize + (nr * g_recv.dtype.itemsize if nr else 0) + 16
    rb = SUBLANES * 2
    while rb * 2 <= rows and rows % (rb * 2) == 0 and rb * 2 * cols * per_el * 2 <= VMEM_LIMIT // 2:
        rb *= 2
    if rows % rb:
        rb = rows
    blk = pl.BlockSpec((rb, cols), lambda i: (i, 0))
    rblk = pl.BlockSpec((max(nr, 1), rb, cols), lambda i: (0, i, 0))
    has_r = g_recv is not None
    bc1 = 1.0 - ADAM_B1 ** ADAM_STEP
    bc2 = 1.0 - ADAM_B2 ** ADAM_STEP

    def body(*refs):
        w_ref, m_ref, v_ref, go_ref = refs[:4]
        gr_ref = refs[4] if has_r else None
        g_out, d_out, m_out, v_out = refs[4 + has_r:]
        g = go_ref[...].astype(F32)
        if has_r:
            for r in range(nr):
                g = g + gr_ref[r].astype(F32)
        mn = ADAM_B1 * m_ref[...] + (1.0 - ADAM_B1) * g
        vn = ADAM_B2 * v_ref[...] + (1.0 - ADAM_B2) * (g * g)
        m_hat = mn / bc1
        v_hat = vn / bc2
        g_out[...] = g
        d_out[...] = -ADAM_LR * (m_hat / (jnp.sqrt(v_hat) + ADAM_EPS) + ADAM_WD * w_ref[...])
        m_out[...] = mn
        v_out[...] = vn

    return pl.pallas_call(
        body, name=name, out_shape=[jax.ShapeDtypeStruct((rows, cols), F32)] * 4, grid=(rows // rb,),
        in_specs=[blk] * 4 + ([rblk] if has_r else []), out_specs=[blk] * 4,
        compiler_params=_cparams(("arbitrary",)),
    )(*([w, m, v, g_own] + ([g_recv] if has_r else [])))


def _round_up(n, q):
    return (n + q - 1) // q * q


def _cols(a8):
    return jnp.transpose(a8, (1, 0, 2)).reshape(a8.shape[1], -1)


def _col_slabs(a):
    r_, c_ = a.shape
    return jnp.transpose(a.reshape(r_, N_DEV, c_ // N_DEV), (1, 0, 2))


def _padded_from_slabs(slabs, gap_at, gap, total):
    _, rows, c8 = slabs.shape
    zeros = lambda n: jnp.zeros((rows, n), slabs.dtype)
    pieces = []
    for dd in range(N_DEV):
        lo, hi = dd * c8, (dd + 1) * c8
        if gap and lo <= gap_at < hi:
            pieces += [slabs[dd][:, :gap_at - lo], zeros(gap), slabs[dd][:, gap_at - lo:]]
        else:
            pieces.append(slabs[dd])
    if total > N_DEV * c8 + gap:
        pieces.append(zeros(total - N_DEV * c8 - gap))
    return jnp.concatenate([p for p in pieces if p.shape[1]], axis=1)


def _slabs_from_padded(mat, gap_at, gap, c8):
    out = []
    for dd in range(N_DEV):
        lo, hi = dd * c8, (dd + 1) * c8
        if gap and lo < gap_at < hi:
            out.append(jnp.concatenate([mat[:, lo:gap_at], mat[:, gap_at + gap:hi + gap]], axis=1))
        else:
            start = lo + (gap if lo >= gap_at else 0)
            out.append(mat[:, start:start + c8])
    return jnp.stack(out)


_MID = ("w_out_a", "w_out_b", "w_out", "w_mlp_up", "w_mlp_down")


def _local_step(x, target, wts, shards, cfg):
    dr, dc, d, lp, cb = cfg["dr"], cfg["dc"], cfg["d"], cfg["lp"], cfg["cb"]
    dff = shards["w_mlp_down"].shape[0] * N_DEV
    wmix = 3 * dr + lp
    (xn,) = _norm_fwd("norm_mix_fwd", x, None, wts["norm_mix_w"], False)
    (p_all,), (g_oa, g_ob, g_o) = _matmul(
        "mm_in", xn, wts["w_all"], "nn", [F32],
        comm=_Comm("gather", [shards["w_out_a"], shards["w_out_b"], shards["w_out"]]))
    w_out_a, w_out_b, w_out = _cols(g_oa), _cols(g_ob), g_o.reshape(d, d)
    pm = _mix_fwd("mix_fwd", p_all, wts["mu_pad"], wmix, cb)
    prep_w = (wts["w0"], wts["a0"], wts["k_k"], wts["k_a"], wts["wd"], wts["wi"], wts["wg"])
    lw, k2, a_in, b_in, g = _prep_fwd("prep_fwd", pm, cfg, *prep_w)
    (y_raw, s_chk), (g_u,) = _rec_fwd(
        "rec_fwd", pm, lw, k2, a_in, b_in, cfg, comm=_Comm("gather", [shards["w_mlp_up"]]))
    w_up = _cols(g_u)
    post_w = (wts["lnx_w"], wts["lnx_b"], wts["r_k"])
    ya_in = _post_fwd("post_fwd", y_raw, pm, k2, g, *post_w, cfg)
    (ya,) = _matmul("mm_out_a", ya_in, w_out_a, "nn", [F32])
    yb_in = _conv_fwd("conv_fwd", p_all, wts["conv_w8"], cfg)
    (yb,) = _matmul("mm_out_b", yb_in, w_out_b, "nn", [F32])
    mg = _merge_fwd("merge_fwd", p_all, wts["gate_bias"], ya, yb, cfg)
    (mo,) = _matmul("mm_out", mg, w_out, "nn", [F32])
    h1, hn = _norm_fwd("norm_mlp_fwd", x, mo, wts["norm_mlp_w"], True)
    (u, act), (g_d,) = _matmul("mm_up", hn, w_up, "nn", [F32, BF16],
                               epi=lambda r: (r, jnp.square(jnp.maximum(r, 0.0))),
                               comm=_Comm("gather", [shards["w_mlp_down"]]))
    w_down = g_d.reshape(dff, d)
    (md,) = _matmul("mm_down", act, w_down, "nn", [F32])
    loss, dh2, dh2b, g_norm_final = _final("final", h1, md, target, wts["norm_final_w"])
    (du,) = _matmul("mm_down_dx", dh2b, w_down, "nt", [BF16],
                    epi=lambda r, uu: (r * (2.0 * jnp.maximum(uu, 0.0)),), extras=(u,))
    (g_down,) = _matmul("mm_down_dw", act, dh2b, "tn", [BF16])
    core = lax.axis_index("c").astype(jnp.int32).reshape(1)
    my_chip = 2 * lax.axis_index("x") + lax.axis_index("y")
    me = 2 * my_chip + lax.axis_index("c")
    own, recv = {}, {}

    def chip_own(chip_sum):
        return lax.dynamic_index_in_dim(chip_sum, my_chip, axis=0, keepdims=False)

    down_slabs = g_down.reshape(N_DEV, dff // N_DEV, d)
    (dhn,), (got,) = _matmul("mm_up_dx", du, w_up, "nt", [F32], comm=_Comm("pair", [down_slabs]))
    down_sum = _pair_sum("pair_sum_down", down_slabs, got, core)
    (g_up,), (recv["w_mlp_down"],) = _matmul("mm_up_dw", hn, du, "tn", [BF16], out_slabs=True,
                                            comm=_Comm("chips", [down_sum]))
    own["w_mlp_down"] = chip_own(down_sum)
    dh1, dh1b, g_norm_mlp = _norm_bwd("norm_mlp_bwd", h1, dhn, dh2, wts["norm_mlp_w"])
    (dmg,), (got,) = _matmul("mm_out_dx", dh1b, w_out, "nt", [F32], comm=_Comm("pair", [g_up]))
    up_sum = _pair_sum("pair_sum_up", g_up, got, core)
    own["w_mlp_up"] = chip_own(up_sum)
    (g_out,) = _matmul("mm_out_dw", mg, dh1b, "tn", [BF16])
    dpga, dpgb, dya, dyb, dba, dbb = _merge_bwd("merge_bwd", p_all, wts["gate_bias"], ya, yb, dmg, cfg)
    (dya_in,) = _matmul("mm_out_a_dx", dya, w_out_a, "nt", [F32])
    (g_out_a,) = _matmul("mm_out_a_dw", ya_in, dya, "tn", [BF16], out_slabs=True)
    (dyb_in,) = _matmul("mm_out_b_dx", dyb, w_out_b, "nt", [F32])
    (g_out_b,) = _matmul("mm_out_b_dw", yb_in, dyb, "tn", [BF16], out_slabs=True)
    dpb, dpc, dpu, g_conv8 = _conv_bwd("conv_bwd", p_all, wts["conv_w8"], dyb_in, cfg)
    dy_raw, dr_post, dk_post, dv_post, dg, g_lnw, g_lnb, g_rk = _post_bwd(
        "post_bwd", y_raw, pm, k2, g, *post_w, dya_in, cfg)
    (dr_rec, dlw, dk_rec, dv_rec, da_in, db_in), (recv["w_mlp_up"],) = _rec_bwd(
        "rec_bwd", pm, lw, k2, a_in, b_in, s_chk, dy_raw, cfg, comm=_Comm("chips", [up_sum]))
    (dpm_r, dpm_k, dpm_v, dpl, g_w0, g_a0, g_kk, g_ka, g_wd, g_wi, g_wg) = _prep_bwd(
        "prep_bwd", pm, cfg, *prep_w, (dlw, dk_rec, dk_post, da_in, db_in, dg),
        (dr_rec, dr_post), (dv_rec, dv_post))
    mu = wts["mu_pad"]
    nb = dr // cb
    dps, dmus = [], []
    for s, dpm_s in enumerate((dpm_r, dpm_k, dpm_v)):
        dp_s, dmu_s = _mix_bwd("mix_bwd_%d" % s, [dpm_s], p_all, s * nb, mu[:, s * dr:(s + 1) * dr], cb)
        dps.append(dp_s)
        dmus.append(dmu_s)
    dp_l, dmu_l = _mix_bwd("mix_bwd_l", [dpl[j] for j in range(nb)], p_all, 3 * nb, mu[:, 3 * dr:], min(cb, lp))
    tail = [jnp.zeros((x.shape[0], cfg["wall"] - cfg["used"]), BF16)] if cfg["wall"] > cfg["used"] else []
    dp_all = jnp.concatenate(dps + [dp_l, dpb, dpc, dpu, dpga, dpgb] + tail, axis=1)
    ld, li, lora = cfg["ld"], cfg["li"], cfg["lora"]
    g_small = jnp.concatenate([g_wd[:ld], g_wi[ld:ld + li], g_wg[ld + li:lora], g_conv8[:3]], axis=0)
    g_small = jnp.pad(g_small, ((0, cfg["small_rows"] - g_small.shape[0]), (0, 0)))
    direct = dict(w_out_a=g_out_a, w_out_b=g_out_b, w_out=g_out.reshape(N_DEV, d // N_DEV, d),
                  small=_col_slabs(g_small))
    (g_all,), got4 = _matmul("mm_in_dw", xn, dp_all, "tn", [BF16],
                             comm=_Comm("exchange", list(direct.values())))
    for n, slabs, r in zip(direct, direct.values(), got4):
        own[n] = lax.dynamic_index_in_dim(slabs, me, axis=0, keepdims=False)
        recv[n] = r
    in_slabs = _slabs_from_padded(g_all, 3 * dr + lora, lp - lora,
                                  (cfg["used"] - lp + lora) // N_DEV)
    (got,) = _comm_call("pair_exchange", _Comm("pair", [in_slabs]))
    in_sum = _pair_sum("pair_sum_in", in_slabs, got, core)
    (dxn,), (recv["w_in"],) = _matmul("mm_in_dx", dp_all, wts["w_all"], "nt", [F32], comm=_Comm("chips", [in_sum]))
    own["w_in"] = chip_own(in_sum)
    grad_x, _, g_norm_mix = _norm_bwd("norm_mix_bwd", x, dxn, dh1, wts["norm_mix_w"])
    grads = dict(
        norm_mix_w=g_norm_mix, gate_bias=jnp.concatenate([dba, dbb], axis=1),
        mu_pad=jnp.concatenate(dmus + [dmu_l], axis=1), w0=g_w0, a0=g_a0, k_k=g_kk, k_a=g_ka,
        r_k=g_rk, lnx_w=g_lnw, lnx_b=g_lnb, norm_mlp_w=g_norm_mlp, norm_final_w=g_norm_final)
    return loss, grad_x, grads, own, recv


_SMALL = ("norm_mix_w", "gate_bias", "shift_mu", "w0", "a0", "k_k", "k_a", "r_k", "lnx_w", "lnx_b",
          "norm_mlp_w", "norm_final_w")
_ORDER = ("norm_mix_w", "w_in", "gate_bias", "shift_mu", "w0", "w_decay_up", "a0", "w_iclr_up", "w_gate_up",
          "k_k", "k_a", "r_k", "lnx_w", "lnx_b", "w_out_a", "conv_w", "w_out_b", "w_out", "norm_mlp_w",
          "w_mlp_up", "w_mlp_down", "norm_final_w")


def _step(x, target, w, m, v):
    t, d = x.shape[1], x.shape[2]
    dr = w["w0"].shape[-1]
    ld, li, lg = w["w_decay_up"].shape[1], w["w_iclr_up"].shape[1], w["w_gate_up"].shape[1]
    lora = ld + li + lg
    lp = _round_up(lora, LANES)
    dc = w["conv_w"].shape[-1] * N_DEV
    cb = math.gcd(math.gcd(lp, dr), 512)
    used = 3 * dr + lp + 3 * dc + 2 * d
    wall = _round_up(used, 1024 if used > MAX_FULL_K else LANES)
    small_rows = ld + li + lg + 3
    cfg = dict(d=d, dr=dr, dc=dc, lp=lp, cb=cb, off_conv=3 * dr + lp, off_gate=3 * dr + lp + 3 * dc, used=used,
               wall=wall, ld=ld, li=li, lora=lora, small_rows=_round_up(small_rows, SUBLANES))
    x2, tg2 = x[0], target[0]

    small_sh = jnp.concatenate([w["w_decay_up"][0], w["w_iclr_up"][0], w["w_gate_up"][0], w["conv_w"][0]], axis=0)
    small_sh = jnp.pad(small_sh, ((0, _round_up(small_rows, SUBLANES) - small_rows), (0, 0)))
    big = ("w_in",) + _MID
    g_in8, gsm = _comm_call("gather_weights", _Comm("gather", [w["w_in"][0].astype(BF16), small_sh]))
    shards = {n: w[n][0].astype(BF16) for n in _MID}
    w_all = _padded_from_slabs(g_in8, 3 * dr + lora, lp - lora, wall)
    sm = _cols(gsm)
    lora_full = sm[:lora]

    def lora_pad(lo, hi):
        rows = lax.broadcasted_iota(jnp.int32, (lp, 1), 0)
        full = jnp.pad(lora_full, ((0, lp - lora), (0, 0)))
        return jnp.where(jnp.logical_and(rows >= lo, rows < hi), full, 0.0)

    conv_w8 = jnp.pad(sm[lora:lora + 3], ((0, SUBLANES - 3), (0, 0)))
    mu_pad = jnp.pad(w["shift_mu"], ((0, 0), (0, lp - lora)))
    wts = dict(
        w_all=w_all, wd=lora_pad(0, ld), wi=lora_pad(ld, ld + li), wg=lora_pad(ld + li, lora), conv_w8=conv_w8,
        mu_pad=mu_pad, norm_mix_w=w["norm_mix_w"], gate_bias=w["gate_bias"], w0=w["w0"], a0=w["a0"],
        k_k=w["k_k"], k_a=w["k_a"], r_k=w["r_k"].reshape(1, dr), lnx_w=w["lnx_w"], lnx_b=w["lnx_b"],
        norm_mlp_w=w["norm_mlp_w"], norm_final_w=w["norm_final_w"].reshape(1, d))

    loss, grad_x, gr, own, received = _local_step(x2, tg2, wts, shards, cfg)

    small_g = dict(norm_mix_w=gr["norm_mix_w"], gate_bias=gr["gate_bias"], shift_mu=gr["mu_pad"][:, :3 * dr + lora],
                   w0=gr["w0"], a0=gr["a0"], k_k=gr["k_k"], k_a=gr["k_a"], r_k=gr["r_k"], lnx_w=gr["lnx_w"],
                   lnx_b=gr["lnx_b"], norm_mlp_w=gr["norm_mlp_w"], norm_final_w=gr["norm_final_w"])
    sizes = [small_g[n].size for n in _SMALL]
    total = sum(sizes) + 1
    prow = _round_up(total, LANES * SUBLANES) // LANES

    def pack(parts):
        flat = jnp.concatenate([p.reshape(-1) for p in parts])
        return jnp.pad(flat, (0, prow * LANES - flat.size)).reshape(prow, LANES)

    g_packed = _all_reduce_small("reduce_small", pack([small_g[n] for n in _SMALL] + [loss[0, :1]]))
    one = jnp.zeros((1,), F32)
    packed = [pack([d_[n] for n in _SMALL] + [one]) for d_ in (w, m, v)]
    sm_out = _adamw("adamw_small", *packed, g_packed)
    loss_out = g_packed.reshape(-1)[total - 1]

    def unpack(flat2d):
        flat = flat2d.reshape(-1)
        out, o = {}, 0
        for n, s in zip(_SMALL, sizes):
            out[n] = flat[o:o + s].reshape(w[n].shape)
            o += s
        return out

    res = [unpack(a) for a in sm_out]

    def shard2d(a):
        return a.reshape(-1, a.shape[-1])

    for n in big:
        outs = _adamw("adamw_" + n, shard2d(w[n]), shard2d(m[n]), shard2d(v[n]), shard2d(own[n]),
                      received[n].reshape(received[n].shape[:1] + shard2d(own[n]).shape))
        for r_, o in zip(res, outs):
            r_[n] = o.reshape(w[n].shape)
    sm_names = ("w_decay_up", "w_iclr_up", "w_gate_up", "conv_w")
    stack = lambda d_: jnp.pad(jnp.concatenate([d_[n][0] for n in sm_names], axis=0),
                               ((0, _round_up(small_rows, SUBLANES) - small_rows), (0, 0)))
    outs = _adamw("adamw_stack", stack(w), stack(m), stack(v), own["small"], received["small"])
    bounds = (0, ld, ld + li, lora, lora + 3)
    for r_, o in zip(res, outs):
        for q, n in enumerate(sm_names):
            r_[n] = o[bounds[q]:bounds[q + 1]].reshape(w[n].shape)

    grad, delta, new_m, new_v = res
    return (loss_out, grad_x[None], *[grad[n] for n in _ORDER], *[delta[n] for n in _ORDER],
            *[new_m[n] for n in _ORDER], *[new_v[n] for n in _ORDER])


def kernel(x, norm_mix_w, w_in, gate_bias, shift_mu, w0, w_decay_up, a0, w_iclr_up, w_gate_up, k_k, k_a, r_k, lnx_w, lnx_b, w_out_a, conv_w, w_out_b, w_out, norm_mlp_w, w_mlp_up, w_mlp_down, norm_final_w, loss_target, m_norm_mix_w, m_w_in, m_gate_bias, m_shift_mu, m_w0, m_w_decay_up, m_a0, m_w_iclr_up, m_w_gate_up, m_k_k, m_k_a, m_r_k, m_lnx_w, m_lnx_b, m_w_out_a, m_conv_w, m_w_out_b, m_w_out, m_norm_mlp_w, m_w_mlp_up, m_w_mlp_down, m_norm_final_w, v_norm_mix_w, v_w_in, v_gate_bias, v_shift_mu, v_w0, v_w_decay_up, v_a0, v_w_iclr_up, v_w_gate_up, v_k_k, v_k_a, v_r_k, v_lnx_w, v_lnx_b, v_w_out_a, v_conv_w, v_w_out_b, v_w_out, v_norm_mlp_w, v_w_mlp_up, v_w_mlp_down, v_norm_final_w):
    w = dict(zip(_ORDER, (norm_mix_w, w_in, gate_bias, shift_mu, w0, w_decay_up, a0, w_iclr_up, w_gate_up, k_k, k_a,
                          r_k, lnx_w, lnx_b, w_out_a, conv_w, w_out_b, w_out, norm_mlp_w, w_mlp_up, w_mlp_down,
                          norm_final_w)))
    m = dict(zip(_ORDER, (m_norm_mix_w, m_w_in, m_gate_bias, m_shift_mu, m_w0, m_w_decay_up, m_a0, m_w_iclr_up,
                          m_w_gate_up, m_k_k, m_k_a, m_r_k, m_lnx_w, m_lnx_b, m_w_out_a, m_conv_w, m_w_out_b,
                          m_w_out, m_norm_mlp_w, m_w_mlp_up, m_w_mlp_down, m_norm_final_w)))
    v = dict(zip(_ORDER, (v_norm_mix_w, v_w_in, v_gate_bias, v_shift_mu, v_w0, v_w_decay_up, v_a0, v_w_iclr_up,
                          v_w_gate_up, v_k_k, v_k_a, v_r_k, v_lnx_w, v_lnx_b, v_w_out_a, v_conv_w, v_w_out_b,
                          v_w_out, v_norm_mlp_w, v_w_mlp_up, v_w_mlp_down, v_norm_final_w)))
    return _step(x, loss_target, w, m, v)
```

```python
import math

import jax
import jax.numpy as jnp
from jax import lax
from jax.experimental import pallas as pl
from jax.experimental.pallas import tpu as pltpu

F32 = jnp.float32
BF16 = jnp.bfloat16
MESH = pl.DeviceIdType.MESH

N_DEV = 8
HEAD = 64
LANES = 128
SUBLANES = 8
CHUNK = 64
RMS_EPS = 1e-5
LNX_EPS = 64e-5
L2_EPS = 1e-12
ADAM_LR = 0.001
ADAM_B1 = 0.9
ADAM_B2 = 0.999
ADAM_EPS = 1e-08
ADAM_WD = 0.01
ADAM_STEP = 10
VMEM_LIMIT = 48 * 1024 * 1024
MAX_FULL_K = 4096
LIGHT_ROWS = (512, 256, 128, 64, 32, 16, 8)


def _cparams(sem):
    return pltpu.CompilerParams(dimension_semantics=sem, vmem_limit_bytes=VMEM_LIMIT)


def _tile(dim, cands):
    for c in cands:
        if c <= dim and dim % c == 0:
            return c
    return dim


def _my_pos():
    return lax.axis_index("x"), lax.axis_index("y"), lax.axis_index("c")


def _peer(pos, r):
    x, y, c = pos
    return (1 - x if r & 4 else x, 1 - y if r & 2 else y, 1 - c if r & 1 else c)


def _slot(pos):
    return 4 * pos[0] + 2 * pos[1] + pos[2]


class _Comm:
    def __init__(self, kind, arrs):
        self.kind, self.arrs, self.n = kind, list(arrs), len(arrs)
        if kind == "gather":
            self.out_shape = [jax.ShapeDtypeStruct((N_DEV,) + a.shape, a.dtype) for a in arrs]
        elif kind == "exchange":
            self.out_shape = [jax.ShapeDtypeStruct((N_DEV - 1,) + a.shape[1:], a.dtype) for a in arrs]
        elif kind == "pair":
            self.out_shape = [jax.ShapeDtypeStruct((N_DEV // 2,) + a.shape[1:], a.dtype) for a in arrs]
        else:
            self.out_shape = [jax.ShapeDtypeStruct((3,) + a.shape[1:], a.dtype) for a in arrs]
        self.scratch = [pltpu.SemaphoreType.DMA((7 * self.n,)), pltpu.SemaphoreType.DMA((7 * self.n,))]
        if kind == "gather":
            self.scratch.append(pltpu.SemaphoreType.DMA((self.n,)))

    def _exchange_copies(self, in_refs, out_refs, sems):
        me = _my_pos()
        x, y, c = me
        cps = []
        for ai in range(self.n):
            if self.kind == "exchange":
                todo = [(in_refs[ai].at[_slot(_peer(me, r))], out_refs[ai].at[r - 1], _peer(me, r), r - 1)
                        for r in range(1, N_DEV)]
            elif self.kind == "pair":
                todo = [(in_refs[ai].at[2 * q + 1 - c], out_refs[ai].at[q], (x, y, 1 - c), q)
                        for q in range(N_DEV // 2)]
            else:
                chips = [(1 - x, y), (x, 1 - y), (1 - x, 1 - y)]
                todo = [(in_refs[ai].at[2 * cx + cy], out_refs[ai].at[j], (cx, cy, c), j)
                        for j, (cx, cy) in enumerate(chips)]
            for src, dst, to, k in todo:
                cps.append(pltpu.make_async_remote_copy(
                    src_ref=src, dst_ref=dst, send_sem=sems[0].at[ai * 7 + k], recv_sem=sems[1].at[ai * 7 + k],
                    device_id=to, device_id_type=MESH))
        return cps

    def _gather_parts(self, in_refs, out_refs, sems):
        x, y, c = _my_pos()
        me, sibling = (x, y, c), (x, y, 1 - c)
        chips = [(1 - x, y), (x, 1 - y), (1 - x, 1 - y)]

        def copy(ai, k, block, to, src=None):
            dst = out_refs[ai].at[_slot(block)]
            return pltpu.make_async_remote_copy(
                src_ref=dst if src is None else src, dst_ref=dst, send_sem=sems[0].at[ai * 7 + k],
                recv_sem=sems[1].at[ai * 7 + k], device_id=to, device_id_type=MESH)

        mine = [pltpu.make_async_copy(in_refs[ai], out_refs[ai].at[_slot(me)], sems[2].at[ai])
                for ai in range(self.n)]
        first = []
        for ai in range(self.n):
            first.append(copy(ai, 0, me, sibling, src=in_refs[ai]))
            first += [copy(ai, 1 + j, me, (*chip, c), src=in_refs[ai]) for j, chip in enumerate(chips)]
        return me, sibling, chips, c, copy, mine, first

    def start(self, in_refs, out_refs, sems):
        if self.kind != "gather":
            for cp in self._exchange_copies(in_refs, out_refs, sems):
                cp.start()
            return
        _, _, _, _, _, mine, first = self._gather_parts(in_refs, out_refs, sems)
        for cp in mine + first:
            cp.start()

    def wait(self, in_refs, out_refs, sems):
        if self.kind != "gather":
            for cp in self._exchange_copies(in_refs, out_refs, sems):
                cp.wait()
            return
        me, sibling, chips, c, copy, mine, first = self._gather_parts(in_refs, out_refs, sems)
        passed = []
        for ai in range(self.n):
            for j, chip in enumerate(chips):
                copy(ai, 1 + j, (*chip, c), me).wait_recv()
                fwd = copy(ai, 4 + j, (*chip, c), sibling)
                fwd.start()
                passed.append(fwd)
        for ai in range(self.n):
            copy(ai, 0, sibling, me).wait_recv()
            for j, chip in enumerate(chips):
                copy(ai, 4 + j, (*chip, 1 - c), me).wait_recv()
        for cp in first + passed:
            cp.wait_send()
        for cp in mine:
            cp.wait()


def _hosted_call(body, name, comm, first, last, *, args, in_specs, out_shape, out_specs, scratch, grid, sem):
    if comm is None:
        return pl.pallas_call(body, name=name, out_shape=out_shape, grid=grid, in_specs=in_specs, out_specs=out_specs,
                              scratch_shapes=scratch, compiler_params=_cparams(sem))(*args)
    ni, no, ns, nc = len(args), len(out_shape), len(scratch), comm.n
    hbm = pl.BlockSpec(memory_space=pl.ANY)

    def hosted(*refs):
        ins, cin = refs[:ni], refs[ni:ni + nc]
        outs, cout = refs[ni + nc:ni + nc + no], refs[ni + nc + no:ni + 2 * nc + no]
        scr, sems = refs[ni + 2 * nc + no:ni + 2 * nc + no + ns], refs[ni + 2 * nc + no + ns:]

        @pl.when(first())
        def _():
            comm.start(cin, cout, sems)

        body(*ins, *outs, *scr)

        @pl.when(last())
        def _():
            comm.wait(cin, cout, sems)

    res = pl.pallas_call(
        hosted, name=name, out_shape=list(out_shape) + comm.out_shape, grid=grid,
        in_specs=list(in_specs) + [hbm] * nc, out_specs=list(out_specs) + [hbm] * nc,
        scratch_shapes=list(scratch) + comm.scratch,
        compiler_params=_cparams(("arbitrary",) * len(grid)))(*args, *comm.arrs)
    return res[:no], res[no:]


_DIMS = {"nn": ((1,), (0,)), "nt": ((1,), (1,)), "tn": ((0,), (0,))}


def _matmul(name, a, b, mode, out_dtypes, epi=None, extras=(), comm=None, out_slabs=False):
    if mode == "nn":
        (m, k), n = a.shape, b.shape[1]
    elif mode == "nt":
        (m, k), n = a.shape, b.shape[0]
    else:
        (k, m), n = a.shape, b.shape[1]
    tm = _tile(m, (1024, 512, 256, 128, 64, 32, 16, 8))
    if k <= MAX_FULL_K:
        tk, tn = k, _tile(n // N_DEV if out_slabs else n, (512, 256, 128))
    else:
        tk, tn = _tile(k, (2048, 1024, 512, 256, 128)), _tile(n, (1024, 512, 256, 128))
    nk = k // tk
    gm, gn = m // tm, n // tn
    a_spec = pl.BlockSpec((tk, tm), lambda i, j, q: (q, i)) if mode == "tn" else pl.BlockSpec((tm, tk), lambda i, j, q: (i, q))
    b_spec = pl.BlockSpec((tn, tk), lambda i, j, q: (j, q)) if mode == "nt" else pl.BlockSpec((tk, tn), lambda i, j, q: (q, j))
    mn_spec = pl.BlockSpec((tm, tn), lambda i, j, q: (i, j))
    per = n // N_DEV // tn if out_slabs else 0
    out_spec = pl.BlockSpec((None, tm, tn), lambda i, j, q: (j // per, i, j % per)) if out_slabs else mn_spec
    ne, no = len(extras), len(out_dtypes)
    dims = (_DIMS[mode], ((), ()))
    keep_t = mode == "tn" and nk == 1 and gn > 1

    def finish(r, extra_refs, out_refs):
        outs = (r,) if epi is None else epi(r, *[e[...] for e in extra_refs])
        for o_ref, o in zip(out_refs, outs):
            o_ref[...] = o.astype(o_ref.dtype)

    def body(a_ref, b_ref, *rest):
        extra_refs, out_refs = rest[:ne], rest[ne:ne + no]
        if keep_t:
            at = rest[ne + no]

            @pl.when(pl.program_id(1) == 0)
            def _():
                at[...] = a_ref[...].T

            part = jnp.dot(at[...], b_ref[...], preferred_element_type=F32)
        else:
            part = lax.dot_general(a_ref[...], b_ref[...], dims, preferred_element_type=F32)
        if nk == 1:
            finish(part, extra_refs, out_refs)
            return
        acc = rest[ne + no]
        q = pl.program_id(2)

        @pl.when(q == 0)
        def _():
            acc[...] = part

        @pl.when(jnp.logical_and(q > 0, q < nk - 1))
        def _():
            acc[...] += part

        @pl.when(q == nk - 1)
        def _():
            finish(acc[...] + part, extra_refs, out_refs)

    def first():
        return jnp.logical_and(jnp.logical_and(pl.program_id(0) == 0, pl.program_id(1) == 0), pl.program_id(2) == 0)

    def last():
        return jnp.logical_and(jnp.logical_and(pl.program_id(0) == gm - 1, pl.program_id(1) == gn - 1),
                               pl.program_id(2) == nk - 1)

    return _hosted_call(
        body, name, comm, first, last,
        args=[a, b, *extras], in_specs=[a_spec, b_spec] + [mn_spec] * ne,
        out_shape=[jax.ShapeDtypeStruct((N_DEV, m, n // N_DEV) if out_slabs else (m, n), dt) for dt in out_dtypes],
        out_specs=[out_spec] * no,
        scratch=[pltpu.VMEM((tm, tn), F32)] if nk > 1 else ([pltpu.VMEM((tm, tk), a.dtype)] if keep_t else []),
        grid=(gm, gn, nk), sem=("parallel", "arbitrary" if keep_t else "parallel", "arbitrary"))


@jax.custom_vjp
def _mm(a, w):
    return jnp.dot(a.astype(BF16), w.astype(BF16), preferred_element_type=F32)


def _mm_fwd(a, w):
    return _mm(a, w), (a, w)


def _mm_bwd(res, ct):
    a, w = res
    ctb = ct.astype(BF16)
    da = lax.dot_general(ctb, w.astype(BF16), (((1,), (1,)), ((), ())), preferred_element_type=F32)
    dw = lax.dot_general(a.astype(BF16), ctb, (((0,), (0,)), ((), ())), preferred_element_type=F32)
    return da, dw


_mm.defvjp(_mm_fwd, _mm_bwd)


def _split3(x):
    hi = x.astype(BF16)
    r1 = x - hi.astype(F32)
    mid = r1.astype(BF16)
    lo = (r1 - mid.astype(F32)).astype(BF16)
    return hi, mid, lo


def _head_ones(width):
    r = lax.broadcasted_iota(jnp.int32, (width, width), 0) // HEAD
    c = lax.broadcasted_iota(jnp.int32, (width, width), 1) // HEAD
    return (r == c).astype(BF16)


@jax.custom_vjp
def _segsum(x):
    ones = _head_ones(x.shape[-1])
    out = None
    for piece in _split3(x):
        t = jnp.dot(piece, ones, preferred_element_type=F32)
        out = t if out is None else out + t
    return out


_segsum.defvjp(lambda x: (_segsum(x), None), lambda _, ct: (_segsum(ct),))


def _softplus(z):
    return jnp.maximum(z, 0.0) + jnp.log(1.0 + jnp.exp(-jnp.abs(z)))


def _sigmoid(z):
    return 1.0 / (1.0 + jnp.exp(-z))


def _rms(x, w):
    ms = jnp.mean(x * x, axis=-1, keepdims=True)
    return x * lax.rsqrt(ms + RMS_EPS) * w


def _row(ref, i):
    return ref[pl.ds(i, 1), :]


def _shift_down(x, prev_ref, n, first):
    rolled = pltpu.roll(x, n, 0)
    rows = lax.broadcasted_iota(jnp.int32, x.shape, 0)
    for q in range(n):
        halo = jnp.where(first, 0.0, _row(prev_ref, SUBLANES - n + q))
        rolled = jnp.where(rows == q, halo, rolled)
    return rolled


def _shift_up(x, next_ref, n, last):
    t = x.shape[0]
    rolled = pltpu.roll(x, t - n, 0)
    rows = lax.broadcasted_iota(jnp.int32, x.shape, 0)
    for q in range(n):
        halo = jnp.where(last, 0.0, _row(next_ref, q))
        rolled = jnp.where(rows == t - n + q, halo, rolled)
    return rolled


def _acc_out(ref, val, first):
    @pl.when(first)
    def _():
        ref[...] = val

    @pl.when(jnp.logical_not(first))
    def _():
        ref[...] += val


def _prep_fn(k, plm, w0, a0, kkw, kaw, wd, wi, wg):
    w_log = -_softplus(-(w0 + _mm(jnp.tanh(plm), wd))) - 0.5
    lw = -jnp.exp(w_log)
    a_g = _sigmoid(a0 + _mm(plm, wi))
    g = _mm(_sigmoid(plm), wg)
    kk = k * kkw
    kk = kk / jnp.maximum(jnp.sqrt(_segsum(kk * kk)), L2_EPS)
    k2 = k * (1.0 + (a_g - 1.0) * kaw)
    return lw, k2, -kk, kk * a_g, g


def _post_fn(y, r, k2, v, g, lnw, lnb, rk):
    mu = _segsum(y) * (1.0 / HEAD)
    yc = y - mu
    var = _segsum(yc * yc) * (1.0 / HEAD)
    yn = yc * lax.rsqrt(var + LNX_EPS) * lnw + lnb
    bonus = _segsum(r * k2 * rk) * v
    return (yn + bonus) * g


def _merge_fn(pga, pgb, ba, bb, ya, yb):
    return _sigmoid(pga + ba) * ya + _sigmoid(pgb + bb) * yb


_NN, _NT, _TN = ((2,), (1,)), ((2,), (2,)), ((1,), (1,))


def _dot3(a, b, dims):
    ah = a.astype(BF16)
    al = (a - ah.astype(F32)).astype(BF16)
    bh = b.astype(BF16)
    bl = (b - bh.astype(F32)).astype(BF16)
    dg = lambda p, q: lax.dot_general(p, q, (dims, ((0,), (0,))), preferred_element_type=F32)
    (ca,), (cb_,) = dims
    if a.shape[ca] % LANES:
        return dg(ah, bh) + (dg(ah, bl) + dg(al, bh))
    cross = dg(jnp.concatenate([ah, al], axis=ca), jnp.concatenate([bl, bh], axis=cb_))
    return dg(ah, bh) + cross


@jax.custom_vjp
def _dnn(a, b):
    return _dot3(a, b, _NN)


@jax.custom_vjp
def _dnt(a, b):
    return _dot3(a, b, _NT)


@jax.custom_vjp
def _dtn(a, b):
    return _dot3(a, b, _TN)


_dnn.defvjp(lambda a, b: (_dnn(a, b), (a, b)), lambda res, ct: (_dnt(ct, res[1]), _dtn(res[0], ct)))
_dnt.defvjp(lambda a, b: (_dnt(a, b), (a, b)), lambda res, ct: (_dnn(ct, res[1]), _dtn(ct, res[0])))
_dtn.defvjp(lambda a, b: (_dtn(a, b), (a, b)), lambda res, ct: (_dnt(res[1], ct), _dnn(res[0], ct)))


@jax.custom_vjp
def _unit_lower_inverse(low):
    n = low.shape[-1]
    ri = lax.broadcasted_iota(jnp.int32, low.shape, 1)
    ci = lax.broadcasted_iota(jnp.int32, low.shape, 2)
    inv = (ri == ci).astype(F32) + low
    pw = low
    for _ in range(int(math.log2(n // 2)) - 1):
        pw = _dnn(pw, pw)
        inv = inv + _dnn(inv, pw)
    return inv


def _unit_lower_inverse_bwd(inv, ct):
    return (_dnt(_dtn(inv, ct), inv),)


_unit_lower_inverse.defvjp(lambda low: (_unit_lower_inverse(low),) * 2, _unit_lower_inverse_bwd)


def _chunk_fn(s, r, lw, k, v, a, b):
    np_, c = r.shape[0], r.shape[1]
    c2 = 2 * c
    ri = lax.broadcasted_iota(jnp.int32, (np_, c, c), 1)
    ci = lax.broadcasted_iota(jnp.int32, (np_, c, c), 2)
    tri = (ri >= ci).astype(F32)
    cum = _dnn(tri, lw)
    tot = jnp.sum(lw, axis=1, keepdims=True)
    g_in, g_inv, g_out = jnp.exp(cum), jnp.exp(-cum), jnp.exp(tot - cum)
    lane_head = lax.broadcasted_iota(jnp.int32, (1, 2, 1, LANES), 3) // HEAD
    which = lax.broadcasted_iota(jnp.int32, (1, 2, 1, LANES), 1)
    hmask = (lane_head == which).astype(F32)

    def st(x):
        return (x[:, None] * hmask).reshape(np_, c2, LANES)

    r2, a2 = st(r * g_in), st(a * jnp.exp(cum - lw))
    b2, k2, v2 = st(b * g_inv), st(k * g_inv), st(v)
    bo2, ko2 = st(b * g_out), st(k * g_out)
    r2i = lax.broadcasted_iota(jnp.int32, (np_, c2, c2), 1)
    c2i = lax.broadcasted_iota(jnp.int32, (np_, c2, c2), 2)
    same = (r2i >= c) == (c2i >= c)
    strict = jnp.logical_and(same, r2i > c2i)
    incl = jnp.logical_and(same, r2i >= c2i)
    lab = jnp.where(strict, _dnt(a2, b2), 0.0)
    lak = jnp.where(strict, _dnt(a2, k2), 0.0)
    mrb = jnp.where(incl, _dnt(r2, b2), 0.0)
    mrk = jnp.where(incl, _dnt(r2, k2), 0.0)
    x2 = _dnt(a2, s) + _dnn(lak, v2)
    u2 = _dnn(_unit_lower_inverse(lab), x2)
    y2 = _dnt(r2, s) + _dnn(mrb, u2) + _dnn(mrk, v2)
    y = jnp.sum(y2.reshape(np_, 2, c, LANES), axis=1)
    s_new = s * jnp.exp(tot) + _dtn(u2, bo2) + _dtn(v2, ko2)
    return y, s_new


def _norm_fwd(name, x, add, w, want_sum):
    t, d = x.shape
    tt = _tile(t, (128, 64, 32, 16, 8))
    row = pl.BlockSpec((tt, d), lambda i: (i, 0))
    par = pl.BlockSpec((1, d), lambda i: (0, 0))
    has_add = add is not None

    def body(*refs):
        x_ref = refs[0]
        add_ref = refs[1] if has_add else None
        w_ref = refs[1 + has_add]
        outs = refs[2 + has_add:]
        h = x_ref[...] + add_ref[...] if has_add else x_ref[...]
        if want_sum:
            outs[0][...] = h
        outs[-1][...] = _rms(h, w_ref[...]).astype(BF16)

    out_shape = ([jax.ShapeDtypeStruct((t, d), F32)] if want_sum else []) + [jax.ShapeDtypeStruct((t, d), BF16)]
    return pl.pallas_call(
        body, name=name, out_shape=out_shape, grid=(t // tt,),
        in_specs=[row] + ([row] if has_add else []) + [par],
        out_specs=[row] * len(out_shape),
        compiler_params=_cparams(("arbitrary",)),
    )(*([x] + ([add] if has_add else []) + [w]))


def _norm_bwd(name, xin, dy, dres, w):
    t, d = xin.shape
    tt = _tile(t, (128, 64, 32, 16, 8))
    row = pl.BlockSpec((tt, d), lambda i: (i, 0))
    par = pl.BlockSpec((1, d), lambda i: (0, 0))

    def body(x_ref, dy_ref, dres_ref, w_ref, dx_ref, dxb_ref, dw_ref):
        _, vjp = jax.vjp(_rms, x_ref[...], w_ref[...])
        dx, dw = vjp(dy_ref[...])
        dx = dx + dres_ref[...]
        dx_ref[...] = dx
        dxb_ref[...] = dx.astype(BF16)
        _acc_out(dw_ref, dw, pl.program_id(0) == 0)

    return pl.pallas_call(
        body, name=name,
        out_shape=[jax.ShapeDtypeStruct((t, d), F32), jax.ShapeDtypeStruct((t, d), BF16),
                   jax.ShapeDtypeStruct((1, d), F32)],
        grid=(t // tt,), in_specs=[row, row, row, par], out_specs=[row, row, par],
        compiler_params=_cparams(("arbitrary",)),
    )(xin, dy, dres, w)


def _final(name, h1, md, target, w):
    t, d = h1.shape
    tt = _tile(t, (128, 64, 32, 16, 8))
    row = pl.BlockSpec((tt, d), lambda i: (i, 0))
    par = pl.BlockSpec((1, d), lambda i: (0, 0))
    one = pl.BlockSpec((1, LANES), lambda i: (0, 0))

    def body(h1_ref, md_ref, tg_ref, w_ref, loss_ref, dh_ref, dhb_ref, dw_ref):
        tg = tg_ref[...]

        def f(h, wv):
            err = _rms(h, wv) - tg
            return 0.5 * jnp.sum(jnp.mean(err * err, axis=-1, keepdims=True), axis=0, keepdims=True)

        loss, vjp = jax.vjp(f, h1_ref[...] + md_ref[...], w_ref[...])
        dh, dw = vjp(jnp.ones((1, 1), F32))
        dh_ref[...] = dh
        dhb_ref[...] = dh.astype(BF16)
        first = pl.program_id(0) == 0
        _acc_out(dw_ref, dw, first)
        _acc_out(loss_ref, jnp.broadcast_to(loss, (1, LANES)), first)

    return pl.pallas_call(
        body, name=name,
        out_shape=[jax.ShapeDtypeStruct((1, LANES), F32), jax.ShapeDtypeStruct((t, d), F32),
                   jax.ShapeDtypeStruct((t, d), BF16), jax.ShapeDtypeStruct((1, d), F32)],
        grid=(t // tt,), in_specs=[row, row, row, par], out_specs=[one, row, row, par],
        compiler_params=_cparams(("arbitrary",)),
    )(h1, md, target, w)


def _halo_specs(tt, cb, nrow8, col_of):
    prev = pl.BlockSpec((SUBLANES, cb), lambda i, j: (jnp.maximum(i * (tt // SUBLANES) - 1, 0), col_of(j)))
    nxt = pl.BlockSpec((SUBLANES, cb), lambda i, j: (jnp.minimum((i + 1) * (tt // SUBLANES), nrow8 - 1), col_of(j)))
    return prev, nxt


def _mix_fwd(name, p_all, mu, width, cb):
    t = p_all.shape[0]
    tt = _tile(t, LIGHT_ROWS)
    main = pl.BlockSpec((tt, cb), lambda i, j: (i, j))
    prev, _ = _halo_specs(tt, cb, t // SUBLANES, lambda j: j)
    par = pl.BlockSpec((1, cb), lambda i, j: (0, j))

    def body(p_ref, prev_ref, mu_ref, o_ref):
        p = p_ref[...]
        o_ref[...] = p + (_shift_down(p, prev_ref, 1, pl.program_id(0) == 0) - p) * mu_ref[...]

    return pl.pallas_call(
        body, name=name, out_shape=jax.ShapeDtypeStruct((t, width), F32),
        grid=(t // tt, width // cb), in_specs=[main, prev, par], out_specs=main,
        compiler_params=_cparams(("arbitrary", "arbitrary")),
    )(p_all, p_all, mu)


def _mix_bwd(name, dpm_list, p_all, col0, mu, cb):
    t, width = dpm_list[0].shape
    tt = _tile(t, LIGHT_ROWS)
    n8 = t // SUBLANES
    nl = len(dpm_list)
    main = pl.BlockSpec((tt, cb), lambda j, i: (i, j))
    nxt = pl.BlockSpec((SUBLANES, cb), lambda j, i: (jnp.minimum((i + 1) * (tt // SUBLANES), n8 - 1), j))
    p_main = pl.BlockSpec((tt, cb), lambda j, i: (i, col0 + j))
    p_prev = pl.BlockSpec((SUBLANES, cb), lambda j, i: (jnp.maximum(i * (tt // SUBLANES) - 1, 0), col0 + j))
    par = pl.BlockSpec((1, cb), lambda j, i: (0, j))
    nt_ = t // tt

    def body(*refs):
        d_refs, dn_refs = refs[:nl], refs[nl:2 * nl]
        p_ref, pp_ref, mu_ref, dp_ref, dmu_ref, nx_scr = refs[2 * nl:]
        i = pl.program_id(1)
        dpm = d_refs[0][...]
        nx = dn_refs[0][...]
        for q in range(1, nl):
            dpm = dpm + d_refs[q][...]
            nx = nx + dn_refs[q][...]
        nx_scr[...] = nx
        mu_v = mu_ref[...]
        up = _shift_up(dpm, nx_scr, 1, i == nt_ - 1)
        dp_ref[...] = (dpm * (1.0 - mu_v) + up * mu_v).astype(BF16)
        p = p_ref[...]
        diff = _shift_down(p, pp_ref, 1, i == 0) - p
        _acc_out(dmu_ref, jnp.sum(dpm * diff, axis=0, keepdims=True), i == 0)

    return pl.pallas_call(
        body, name=name,
        out_shape=[jax.ShapeDtypeStruct((t, width), BF16), jax.ShapeDtypeStruct((1, width), F32)],
        grid=(width // cb, nt_),
        in_specs=[main] * nl + [nxt] * nl + [p_main, p_prev, par],
        out_specs=[main, par],
        scratch_shapes=[pltpu.VMEM((SUBLANES, cb), F32)],
        compiler_params=_cparams(("arbitrary", "arbitrary")),
    )(*dpm_list, *dpm_list, p_all, p_all, mu)


def _prep_fwd(name, pm, cfg, w0, a0, kkw, kaw, wd, wi, wg):
    t = pm.shape[0]
    dr, lp, cb = cfg["dr"], cfg["lp"], cfg["cb"]
    tt = _tile(t, (256, 128, 64, 32, 16, 8))
    nj = dr // cb
    kspec = pl.BlockSpec((tt, cb), lambda j, i: (i, nj + j))
    lspec = pl.BlockSpec((tt, lp), lambda j, i: (i, 3 * dr // lp))
    par = pl.BlockSpec((1, cb), lambda j, i: (0, j))
    wspec = pl.BlockSpec((lp, cb), lambda j, i: (0, j))
    out = pl.BlockSpec((tt, cb), lambda j, i: (i, j))

    def body(k_ref, l_ref, w0_ref, a0_ref, kk_ref, ka_ref, wd_ref, wi_ref, wg_ref, *outs):
        vals = _prep_fn(k_ref[...], l_ref[...], w0_ref[...], a0_ref[...], kk_ref[...], ka_ref[...],
                        wd_ref[...], wi_ref[...], wg_ref[...])
        for o_ref, val in zip(outs, vals):
            o_ref[...] = val

    return pl.pallas_call(
        body, name=name, out_shape=[jax.ShapeDtypeStruct((t, dr), F32)] * 5,
        grid=(nj, t // tt), in_specs=[kspec, lspec, par, par, par, par, wspec, wspec, wspec],
        out_specs=[out] * 5, compiler_params=_cparams(("arbitrary", "arbitrary")),
    )(pm, pm, w0, a0, kkw, kaw, wd, wi, wg)


def _prep_bwd(name, pm, cfg, w0, a0, kkw, kaw, wd, wi, wg, cts, dr_parts, dv_parts):
    t = pm.shape[0]
    dr, lp, cb = cfg["dr"], cfg["lp"], cfg["cb"]
    tt = _tile(t, (256, 128, 64, 32, 16, 8))
    nj = dr // cb
    kspec = pl.BlockSpec((tt, cb), lambda j, i: (i, nj + j))
    lspec = pl.BlockSpec((tt, lp), lambda j, i: (i, 3 * dr // lp))
    par = pl.BlockSpec((1, cb), lambda j, i: (0, j))
    wspec = pl.BlockSpec((lp, cb), lambda j, i: (0, j))
    blk = pl.BlockSpec((tt, cb), lambda j, i: (i, j))
    dpl_spec = pl.BlockSpec((None, tt, lp), lambda j, i: (j, i, 0))

    def body(k_ref, l_ref, w0_ref, a0_ref, kk_ref, ka_ref, wd_ref, wi_ref, wg_ref,
             dlw_ref, dk2a_ref, dk2b_ref, da_ref, db_ref, dg_ref, dr0_ref, dr1_ref, dv0_ref, dv1_ref,
             dpr_ref, dpk_ref, dpv_ref, dpl_ref, dw0_ref, da0_ref, dkk_ref, dka_ref, dwd_ref, dwi_ref, dwg_ref):
        _, vjp = jax.vjp(_prep_fn, k_ref[...], l_ref[...], w0_ref[...], a0_ref[...], kk_ref[...], ka_ref[...],
                         wd_ref[...], wi_ref[...], wg_ref[...])
        dk, dpl, dw0, da0, dkk, dka, dwd, dwi, dwg = vjp(
            (dlw_ref[...], dk2a_ref[...] + dk2b_ref[...], da_ref[...], db_ref[...], dg_ref[...]))
        dpr_ref[...] = dr0_ref[...] + dr1_ref[...]
        dpv_ref[...] = dv0_ref[...] + dv1_ref[...]
        dpk_ref[...] = dk
        dpl_ref[...] = dpl
        first = pl.program_id(1) == 0
        for ref, val in ((dw0_ref, dw0), (da0_ref, da0), (dkk_ref, dkk), (dka_ref, dka),
                         (dwd_ref, dwd), (dwi_ref, dwi), (dwg_ref, dwg)):
            _acc_out(ref, val, first)

    out_shape = ([jax.ShapeDtypeStruct((t, dr), F32)] * 3 + [jax.ShapeDtypeStruct((nj, t, lp), F32)]
                 + [jax.ShapeDtypeStruct((1, dr), F32)] * 4 + [jax.ShapeDtypeStruct((lp, dr), F32)] * 3)
    return pl.pallas_call(
        body, name=name, out_shape=out_shape, grid=(nj, t // tt),
        in_specs=[kspec, lspec, par, par, par, par, wspec, wspec, wspec] + [blk] * 10,
        out_specs=[blk] * 3 + [dpl_spec] + [par] * 4 + [wspec] * 3,
        compiler_params=_cparams(("arbitrary", "arbitrary")),
    )(pm, pm, w0, a0, kkw, kaw, wd, wi, wg, *cts, *dr_parts, *dv_parts)


def _post_specs(t, cfg):
    dr, cb = cfg["dr"], cfg["cb"]
    tt = _tile(t, (256, 128, 64, 32, 16, 8))
    nj = dr // cb
    blk = pl.BlockSpec((tt, cb), lambda j, i: (i, j))
    rspec = pl.BlockSpec((tt, cb), lambda j, i: (i, j))
    vspec = pl.BlockSpec((tt, cb), lambda j, i: (i, 2 * nj + j))
    par = pl.BlockSpec((1, cb), lambda j, i: (0, j))
    return tt, nj, blk, rspec, vspec, par


def _post_fwd(name, y, pm, k2, g, lnw, lnb, rk, cfg):
    t = y.shape[0]
    tt, nj, blk, rspec, vspec, par = _post_specs(t, cfg)

    def body(y_ref, r_ref, k_ref, v_ref, g_ref, lw_ref, lb_ref, rk_ref, o_ref):
        o_ref[...] = _post_fn(y_ref[...], r_ref[...], k_ref[...], v_ref[...], g_ref[...],
                              lw_ref[...], lb_ref[...], rk_ref[...]).astype(BF16)

    return pl.pallas_call(
        body, name=name, out_shape=jax.ShapeDtypeStruct((t, cfg["dr"]), BF16), grid=(nj, t // tt),
        in_specs=[blk, rspec, blk, vspec, blk, par, par, par], out_specs=blk,
        compiler_params=_cparams(("arbitrary", "arbitrary")),
    )(y, pm, k2, pm, g, lnw, lnb, rk)


def _post_bwd(name, y, pm, k2, g, lnw, lnb, rk, dout, cfg):
    t = y.shape[0]
    tt, nj, blk, rspec, vspec, par = _post_specs(t, cfg)

    def body(y_ref, r_ref, k_ref, v_ref, g_ref, lw_ref, lb_ref, rk_ref, do_ref,
             dy_ref, dr_ref, dk_ref, dv_ref, dg_ref, dlw_ref, dlb_ref, drk_ref):
        _, vjp = jax.vjp(_post_fn, y_ref[...], r_ref[...], k_ref[...], v_ref[...], g_ref[...],
                         lw_ref[...], lb_ref[...], rk_ref[...])
        dy, dr, dk, dv, dg, dlw, dlb, drk = vjp(do_ref[...])
        for ref, val in ((dy_ref, dy), (dr_ref, dr), (dk_ref, dk), (dv_ref, dv), (dg_ref, dg)):
            ref[...] = val
        first = pl.program_id(1) == 0
        for ref, val in ((dlw_ref, dlw), (dlb_ref, dlb), (drk_ref, drk)):
            _acc_out(ref, val, first)

    dr = cfg["dr"]
    return pl.pallas_call(
        body, name=name,
        out_shape=[jax.ShapeDtypeStruct((t, dr), F32)] * 5 + [jax.ShapeDtypeStruct((1, dr), F32)] * 3,
        grid=(nj, t // tt),
        in_specs=[blk, rspec, blk, vspec, blk, par, par, par, blk],
        out_specs=[blk] * 5 + [par] * 3,
        compiler_params=_cparams(("arbitrary", "arbitrary")),
    )(y, pm, k2, pm, g, lnw, lnb, rk, dout)


def _conv_specs(t, cfg):
    dc, cb = cfg["dc"], cfg["cb"]
    tt = _tile(t, LIGHT_ROWS)
    nj = dc // cb
    c0 = cfg["off_conv"] // cb
    n8 = t // SUBLANES

    def sect(s):
        col = lambda j: c0 + s * nj + j
        main = pl.BlockSpec((tt, cb), lambda j, i: (i, col(j)))
        prev = pl.BlockSpec((SUBLANES, cb), lambda j, i: (jnp.maximum(i * (tt // SUBLANES) - 1, 0), col(j)))
        nxt = pl.BlockSpec((SUBLANES, cb), lambda j, i: (jnp.minimum((i + 1) * (tt // SUBLANES), n8 - 1), col(j)))
        return main, prev, nxt

    blk = pl.BlockSpec((tt, cb), lambda j, i: (i, j))
    wspec = pl.BlockSpec((SUBLANES, cb), lambda j, i: (0, j))
    return tt, nj, n8, sect, blk, wspec


def _conv_fwd(name, p_all, cw8, cfg):
    t = p_all.shape[0]
    tt, nj, n8, sect, blk, wspec = _conv_specs(t, cfg)
    (bm, _, _), (cm, cp, _), (um, up, _) = sect(0), sect(1), sect(2)

    def body(b_ref, c_ref, cp_ref, u_ref, up_ref, w_ref, o_ref, zp_scr):
        first = pl.program_id(1) == 0
        z = c_ref[...] * u_ref[...]
        zp_scr[...] = cp_ref[...] * up_ref[...]
        o = _row(w_ref, 2) * z + _row(w_ref, 1) * _shift_down(z, zp_scr, 1, first) \
            + _row(w_ref, 0) * _shift_down(z, zp_scr, 2, first)
        o_ref[...] = (b_ref[...] * o).astype(BF16)

    return pl.pallas_call(
        body, name=name, out_shape=jax.ShapeDtypeStruct((t, cfg["dc"]), BF16), grid=(nj, t // tt),
        in_specs=[bm, cm, cp, um, up, wspec], out_specs=blk,
        scratch_shapes=[pltpu.VMEM((SUBLANES, blk.block_shape[1]), F32)],
        compiler_params=_cparams(("arbitrary", "arbitrary")),
    )(p_all, p_all, p_all, p_all, p_all, cw8)


def _conv_bwd(name, p_all, cw8, dyb, cfg):
    t = p_all.shape[0]
    tt, nj, n8, sect, blk, wspec = _conv_specs(t, cfg)
    (bm, _, bn), (cm, cp, _), (um, up, _) = sect(0), sect(1), sect(2)
    cb = blk.block_shape[1]
    dnxt = pl.BlockSpec((SUBLANES, cb), lambda j, i: (jnp.minimum((i + 1) * (tt // SUBLANES), n8 - 1), j))
    nt_ = t // tt

    def body(b_ref, bn_ref, c_ref, cp_ref, u_ref, up_ref, w_ref, d_ref, dn_ref,
             db_ref, dc_ref, du_ref, dw_ref, zp_scr, don_scr):
        i = pl.program_id(1)
        first, last = i == 0, i == nt_ - 1
        c, u, b, dy = c_ref[...], u_ref[...], b_ref[...], d_ref[...]
        z = c * u
        zp_scr[...] = cp_ref[...] * up_ref[...]
        z1 = _shift_down(z, zp_scr, 1, first)
        z2 = _shift_down(z, zp_scr, 2, first)
        w0, w1, w2 = _row(w_ref, 0), _row(w_ref, 1), _row(w_ref, 2)
        o = w2 * z + w1 * z1 + w0 * z2
        do = dy * b
        don_scr[...] = dn_ref[...] * bn_ref[...]
        dz = w2 * do + w1 * _shift_up(do, don_scr, 1, last) + w0 * _shift_up(do, don_scr, 2, last)
        db_ref[...] = (dy * o).astype(BF16)
        dc_ref[...] = (dz * u).astype(BF16)
        du_ref[...] = (dz * c).astype(BF16)
        rows = lax.broadcasted_iota(jnp.int32, (SUBLANES, cb), 0)
        s0 = jnp.sum(do * z2, axis=0, keepdims=True)
        s1 = jnp.sum(do * z1, axis=0, keepdims=True)
        s2 = jnp.sum(do * z, axis=0, keepdims=True)
        dw = jnp.where(rows == 0, s0, jnp.where(rows == 1, s1, jnp.where(rows == 2, s2, 0.0)))
        _acc_out(dw_ref, dw, first)

    dc = cfg["dc"]
    return pl.pallas_call(
        body, name=name,
        out_shape=[jax.ShapeDtypeStruct((t, dc), BF16)] * 3 + [jax.ShapeDtypeStruct((SUBLANES, dc), F32)],
        grid=(nj, nt_),
        in_specs=[bm, bn, cm, cp, um, up, wspec, blk, dnxt],
        out_specs=[blk] * 3 + [wspec],
        scratch_shapes=[pltpu.VMEM((SUBLANES, cb), F32), pltpu.VMEM((SUBLANES, cb), F32)],
        compiler_params=_cparams(("arbitrary", "arbitrary")),
    )(p_all, p_all, p_all, p_all, p_all, p_all, cw8, dyb, dyb)


def _merge_specs(t, cfg):
    d, cb = cfg["d"], cfg["cb"]
    tt = _tile(t, LIGHT_ROWS)
    nj = d // cb
    g0 = cfg["off_gate"] // cb
    ga = pl.BlockSpec((tt, cb), lambda j, i: (i, g0 + j))
    gb = pl.BlockSpec((tt, cb), lambda j, i: (i, g0 + nj + j))
    ba = pl.BlockSpec((1, cb), lambda j, i: (0, j))
    bb = pl.BlockSpec((1, cb), lambda j, i: (0, nj + j))
    blk = pl.BlockSpec((tt, cb), lambda j, i: (i, j))
    return tt, nj, ga, gb, ba, bb, blk


def _merge_fwd(name, p_all, bias, ya, yb, cfg):
    t = p_all.shape[0]
    tt, nj, ga, gb, ba, bb, blk = _merge_specs(t, cfg)

    def body(ga_ref, gb_ref, ba_ref, bb_ref, ya_ref, yb_ref, o_ref):
        o_ref[...] = _merge_fn(ga_ref[...], gb_ref[...], ba_ref[...], bb_ref[...],
                               ya_ref[...], yb_ref[...]).astype(BF16)

    return pl.pallas_call(
        body, name=name, out_shape=jax.ShapeDtypeStruct((t, cfg["d"]), BF16), grid=(nj, t // tt),
        in_specs=[ga, gb, ba, bb, blk, blk], out_specs=blk,
        compiler_params=_cparams(("arbitrary", "arbitrary")),
    )(p_all, p_all, bias, bias, ya, yb)


def _merge_bwd(name, p_all, bias, ya, yb, dm, cfg):
    t = p_all.shape[0]
    tt, nj, ga, gb, ba, bb, blk = _merge_specs(t, cfg)

    def body(ga_ref, gb_ref, ba_ref, bb_ref, ya_ref, yb_ref, dm_ref,
             dga_ref, dgb_ref, dya_ref, dyb_ref, dba_ref, dbb_ref):
        _, vjp = jax.vjp(_merge_fn, ga_ref[...], gb_ref[...], ba_ref[...], bb_ref[...], ya_ref[...], yb_ref[...])
        dga, dgb, dba, dbb, dya, dyb = vjp(dm_ref[...])
        for ref, val in ((dga_ref, dga), (dgb_ref, dgb), (dya_ref, dya), (dyb_ref, dyb)):
            ref[...] = val.astype(BF16)
        first = pl.program_id(1) == 0
        _acc_out(dba_ref, dba, first)
        _acc_out(dbb_ref, dbb, first)

    d = cfg["d"]
    par = pl.BlockSpec((1, blk.block_shape[1]), lambda j, i: (0, j))
    return pl.pallas_call(
        body, name=name,
        out_shape=[jax.ShapeDtypeStruct((t, d), BF16)] * 4 + [jax.ShapeDtypeStruct((1, d), F32)] * 2,
        grid=(nj, t // tt),
        in_specs=[ga, gb, ba, bb, blk, blk, blk], out_specs=[blk] * 4 + [par] * 2,
        compiler_params=_cparams(("arbitrary", "arbitrary")),
    )(p_all, p_all, bias, bias, ya, yb, dm)


PAIRS = 8


def _pair_stack(ref, pairs):
    return jnp.stack([ref[:, p * LANES:(p + 1) * LANES] for p in range(pairs)])


def _pair_store(ref, val):
    for p in range(val.shape[0]):
        ref[:, p * LANES:(p + 1) * LANES] = val[p]


def _rec_specs(t, cfg, rev):
    dr = cfg["dr"]
    nc = t // CHUNK
    hp = dr // LANES
    pairs = _tile(hp, (PAIRS, 2, 1))
    ng = hp // pairs
    w = LANES * pairs
    ch = (lambda c: nc - 1 - c) if rev else (lambda c: c)
    slab = pl.BlockSpec((CHUNK, w), lambda h, c: (ch(c), h))
    vspec = pl.BlockSpec((CHUNK, w), lambda h, c: (ch(c), 2 * ng + h))
    sspec = pl.BlockSpec((None, pairs, LANES, LANES), lambda h, c: (ch(c), h, 0, 0))
    first = lambda: jnp.logical_and(pl.program_id(0) == 0, pl.program_id(1) == 0)
    last = lambda: jnp.logical_and(pl.program_id(0) == ng - 1, pl.program_id(1) == nc - 1)
    return nc, hp, pairs, ng, slab, vspec, sspec, first, last


def _rec_fwd(name, pm, lw, k2, a, b, cfg, comm=None):
    t = pm.shape[0]
    nc, hp, pairs, ng, slab, vspec, sspec, first, last = _rec_specs(t, cfg, False)

    def body(r_ref, lw_ref, k_ref, v_ref, a_ref, b_ref, y_ref, s_ref, s_scr):
        @pl.when(pl.program_id(1) == 0)
        def _():
            s_scr[...] = jnp.zeros_like(s_scr)

        s = s_scr[...]
        s_ref[...] = s
        y, s_new = _chunk_fn(s, *[_pair_stack(ref, pairs) for ref in (r_ref, lw_ref, k_ref, v_ref, a_ref, b_ref)])
        _pair_store(y_ref, y)
        s_scr[...] = s_new

    return _hosted_call(
        body, name, comm, first, last, args=[pm, lw, k2, pm, a, b],
        in_specs=[slab, slab, slab, vspec, slab, slab],
        out_shape=[jax.ShapeDtypeStruct((t, cfg["dr"]), F32), jax.ShapeDtypeStruct((nc, hp, LANES, LANES), F32)],
        out_specs=[slab, sspec], scratch=[pltpu.VMEM((pairs, LANES, LANES), F32)], grid=(ng, nc),
        sem=("arbitrary", "arbitrary"))


def _rec_bwd(name, pm, lw, k2, a, b, s_chk, dy, cfg, comm=None):
    t = pm.shape[0]
    nc, hp, pairs, ng, slab, vspec, sspec, first, last = _rec_specs(t, cfg, True)

    def body(r_ref, lw_ref, k_ref, v_ref, a_ref, b_ref, s_ref, dy_ref,
             dr_ref, dlw_ref, dk_ref, dv_ref, da_ref, db_ref, ds_scr):
        @pl.when(pl.program_id(1) == 0)
        def _():
            ds_scr[...] = jnp.zeros_like(ds_scr)

        _, vjp = jax.vjp(_chunk_fn, s_ref[...],
                         *[_pair_stack(ref, pairs) for ref in (r_ref, lw_ref, k_ref, v_ref, a_ref, b_ref)])
        ds, dr, dlw, dk, dv, da, db = vjp((_pair_stack(dy_ref, pairs), ds_scr[...]))
        ds_scr[...] = ds
        for ref, val in ((dr_ref, dr), (dlw_ref, dlw), (dk_ref, dk), (dv_ref, dv), (da_ref, da), (db_ref, db)):
            _pair_store(ref, val)

    return _hosted_call(
        body, name, comm, first, last, args=[pm, lw, k2, pm, a, b, s_chk, dy],
        in_specs=[slab, slab, slab, vspec, slab, slab, sspec, slab],
        out_shape=[jax.ShapeDtypeStruct((t, cfg["dr"]), F32)] * 6, out_specs=[slab] * 6,
        scratch=[pltpu.VMEM((pairs, LANES, LANES), F32)], grid=(ng, nc), sem=("arbitrary", "arbitrary"))


def _comm_call(name, comm):
    n = comm.n
    hbm = pl.BlockSpec(memory_space=pl.ANY)

    def body(*refs):
        comm.start(refs[:n], refs[n:2 * n], refs[2 * n:])
        comm.wait(refs[:n], refs[n:2 * n], refs[2 * n:])

    return pl.pallas_call(body, name=name, out_shape=comm.out_shape, in_specs=[hbm] * n, out_specs=[hbm] * n,
                          scratch_shapes=comm.scratch)(*comm.arrs)


def _all_reduce_small(name, v):
    rows = v.shape[0]
    vm = pl.BlockSpec(memory_space=pltpu.VMEM)

    def body(x_ref, out_ref, buf, send_sems, recv_sems):
        x, y, c = _my_pos()
        me, sibling = (x, y, c), (x, y, 1 - c)
        chips = [(1 - x, y), (x, 1 - y), (1 - x, 1 - y)]

        def copy(k, block, to, src=None):
            px, py, pc = block
            dst = buf.at[4 * px + 2 * py + pc]
            return pltpu.make_async_remote_copy(
                src_ref=dst if src is None else src, dst_ref=dst,
                send_sem=send_sems.at[k], recv_sem=recv_sems.at[k], device_id=to, device_id_type=MESH)

        buf[4 * x + 2 * y + c] = x_ref[...]
        first = [copy(0, me, sibling, src=x_ref)]
        first += [copy(1 + j, me, (*chip, c), src=x_ref) for j, chip in enumerate(chips)]
        for cp in first:
            cp.start()
        passed = [copy(4 + j, (*chip, c), sibling) for j, chip in enumerate(chips)]
        for j, chip in enumerate(chips):
            copy(1 + j, (*chip, c), me).wait_recv()
            passed[j].start()
        copy(0, sibling, me).wait_recv()
        for j, chip in enumerate(chips):
            copy(4 + j, (*chip, 1 - c), me).wait_recv()
        for cp in first + passed:
            cp.wait_send()
        acc = buf[0]
        for d in range(1, N_DEV):
            acc = acc + buf[d]
        out_ref[...] = acc

    return pl.pallas_call(
        body, name=name, out_shape=jax.ShapeDtypeStruct(v.shape, F32),
        in_specs=[vm], out_specs=vm,
        scratch_shapes=[pltpu.VMEM((N_DEV, rows, LANES), F32), pltpu.SemaphoreType.DMA((7,)),
                        pltpu.SemaphoreType.DMA((7,))],
    )(v)


def _pair_sum(name, slabs, got, core):
    _, rows, cols = slabs.shape
    nq = got.shape[0]
    rb = _tile(rows, (256, 128, 64, 32, 16, 8))
    mine = pl.BlockSpec((None, rb, cols), lambda q, j, c_ref: (2 * q + c_ref[0], j, 0))
    blk = pl.BlockSpec((None, rb, cols), lambda q, j, c_ref: (q, j, 0))

    def body(c_ref, a_ref, b_ref, o_ref):
        o_ref[...] = (a_ref[...].astype(F32) + b_ref[...].astype(F32)).astype(o_ref.dtype)

    return pl.pallas_call(
        body, name=name, out_shape=jax.ShapeDtypeStruct(got.shape, got.dtype),
        grid_spec=pltpu.PrefetchScalarGridSpec(num_scalar_prefetch=1, grid=(nq, rows // rb),
                                               in_specs=[mine, blk], out_specs=blk),
        compiler_params=_cparams(("arbitrary", "arbitrary")))(core, slabs, got)


def _adamw(name, w, m, v, g_own, g_recv=None):
    rows, cols = w.shape
    nr = g_recv.shape[0] if g_recv is not None else 0
    per_el = 4 * 3 + g_own.dtype.itemsize + (nr * g_recv.dtype.itemsize if nr else 0) + 16
    rb = SUBLANES * 2
    while rb * 2 <= rows and rows % (rb * 2) == 0 and rb * 2 * cols * per_el * 2 <= VMEM_LIMIT // 2:
        rb *= 2
    if rows % rb:
        rb = rows
    blk = pl.BlockSpec((rb, cols), lambda i: (i, 0))
    rblk = pl.BlockSpec((max(nr, 1), rb, cols), lambda i: (0, i, 0))
    has_r = g_recv is not None
    bc1 = 1.0 - ADAM_B1 ** ADAM_STEP
    bc2 = 1.0 - ADAM_B2 ** ADAM_STEP

    def body(*refs):
        w_ref, m_ref, v_ref, go_ref = refs[:4]
        gr_ref = refs[4] if has_r else None
        g_out, d_out, m_out, v_out = refs[4 + has_r:]
        g = go_ref[...].astype(F32)
        if has_r:
            for r in range(nr):
                g = g + gr_ref[r].astype(F32)
        mn = ADAM_B1 * m_ref[...] + (1.0 - ADAM_B1) * g
        vn = ADAM_B2 * v_ref[...] + (1.0 - ADAM_B2) * (g * g)
        m_hat = mn / bc1
        v_hat = vn / bc2
        g_out[...] = g
        d_out[...] = -ADAM_LR * (m_hat / (jnp.sqrt(v_hat) + ADAM_EPS) + ADAM_WD * w_ref[...])
        m_out[...] = mn
        v_out[...] = vn

    return pl.pallas_call(
        body, name=name, out_shape=[jax.ShapeDtypeStruct((rows, cols), F32)] * 4, grid=(rows // rb,),
        in_specs=[blk] * 4 + ([rblk] if has_r else []), out_specs=[blk] * 4,
        compiler_params=_cparams(("arbitrary",)),
    )(*([w, m, v, g_own] + ([g_recv] if has_r else [])))


def _round_up(n, q):
    return (n + q - 1) // q * q


def _cols(a8):
    return jnp.transpose(a8, (1, 0, 2)).reshape(a8.shape[1], -1)


def _col_slabs(a):
    r_, c_ = a.shape
    return jnp.transpose(a.reshape(r_, N_DEV, c_ // N_DEV), (1, 0, 2))


def _padded_from_slabs(slabs, gap_at, gap, total):
    _, rows, c8 = slabs.shape
    zeros = lambda n: jnp.zeros((rows, n), slabs.dtype)
    pieces = []
    for dd in range(N_DEV):
        lo, hi = dd * c8, (dd + 1) * c8
        if gap and lo <= gap_at < hi:
            pieces += [slabs[dd][:, :gap_at - lo], zeros(gap), slabs[dd][:, gap_at - lo:]]
        else:
            pieces.append(slabs[dd])
    if total > N_DEV * c8 + gap:
        pieces.append(zeros(total - N_DEV * c8 - gap))
    return jnp.concatenate([p for p in pieces if p.shape[1]], axis=1)


def _slabs_from_padded(mat, gap_at, gap, c8):
    out = []
    for dd in range(N_DEV):
        lo, hi = dd * c8, (dd + 1) * c8
        if gap and lo < gap_at < hi:
            out.append(jnp.concatenate([mat[:, lo:gap_at], mat[:, gap_at + gap:hi + gap]], axis=1))
        else:
            start = lo + (gap if lo >= gap_at else 0)
            out.append(mat[:, start:start + c8])
    return jnp.stack(out)


_MID = ("w_out_a", "w_out_b", "w_out", "w_mlp_up", "w_mlp_down")


def _local_step(x, target, wts, shards, cfg):
    dr, dc, d, lp, cb = cfg["dr"], cfg["dc"], cfg["d"], cfg["lp"], cfg["cb"]
    dff = shards["w_mlp_down"].shape[0] * N_DEV
    wmix = 3 * dr + lp
    (xn,) = _norm_fwd("norm_mix_fwd", x, None, wts["norm_mix_w"], False)
    half = d // 2
    (p_top,), (bot8,) = _matmul("mm_in_top", xn[:, :half], wts["w_top"], "nn", [F32],
                                comm=_Comm("gather", [shards["w_in_bot"]]))
    w_bot = _padded_from_slabs(bot8, 3 * dr + cfg["lora"], lp - cfg["lora"], cfg["wall"])
    (p_all,), (g_oa, g_ob, g_o) = _matmul(
        "mm_in_bot", xn[:, half:], w_bot, "nn", [F32], epi=lambda r, top: (r + top,), extras=(p_top,),
        comm=_Comm("gather", [shards["w_out_a"], shards["w_out_b"], shards["w_out"]]))
    w_all = jnp.concatenate([wts["w_top"], w_bot], axis=0)
    w_out_a, w_out_b, w_out = _cols(g_oa), _cols(g_ob), g_o.reshape(d, d)
    pm = _mix_fwd("mix_fwd", p_all, wts["mu_pad"], wmix, cb)
    prep_w = (wts["w0"], wts["a0"], wts["k_k"], wts["k_a"], wts["wd"], wts["wi"], wts["wg"])
    lw, k2, a_in, b_in, g = _prep_fwd("prep_fwd", pm, cfg, *prep_w)
    (y_raw, s_chk), (g_u,) = _rec_fwd(
        "rec_fwd", pm, lw, k2, a_in, b_in, cfg, comm=_Comm("gather", [shards["w_mlp_up"]]))
    w_up = _cols(g_u)
    post_w = (wts["lnx_w"], wts["lnx_b"], wts["r_k"])
    ya_in = _post_fwd("post_fwd", y_raw, pm, k2, g, *post_w, cfg)
    (ya,) = _matmul("mm_out_a", ya_in, w_out_a, "nn", [F32])
    yb_in = _conv_fwd("conv_fwd", p_all, wts["conv_w8"], cfg)
    (yb,) = _matmul("mm_out_b", yb_in, w_out_b, "nn", [F32])
    mg = _merge_fwd("merge_fwd", p_all, wts["gate_bias"], ya, yb, cfg)
    (mo,) = _matmul("mm_out", mg, w_out, "nn", [F32])
    h1, hn = _norm_fwd("norm_mlp_fwd", x, mo, wts["norm_mlp_w"], True)
    (u, act), (g_d,) = _matmul("mm_up", hn, w_up, "nn", [F32, BF16],
                               epi=lambda r: (r, jnp.square(jnp.maximum(r, 0.0))),
                               comm=_Comm("gather", [shards["w_mlp_down"]]))
    w_down = g_d.reshape(dff, d)
    (md,) = _matmul("mm_down", act, w_down, "nn", [F32])
    loss, dh2, dh2b, g_norm_final = _final("final", h1, md, target, wts["norm_final_w"])
    (du,) = _matmul("mm_down_dx", dh2b, w_down, "nt", [BF16],
                    epi=lambda r, uu: (r * (2.0 * jnp.maximum(uu, 0.0)),), extras=(u,))
    (g_down,) = _matmul("mm_down_dw", act, dh2b, "tn", [BF16])
    core = lax.axis_index("c").astype(jnp.int32).reshape(1)
    my_chip = 2 * lax.axis_index("x") + lax.axis_index("y")
    me = 2 * my_chip + lax.axis_index("c")
    own, recv = {}, {}

    def chip_own(chip_sum):
        return lax.dynamic_index_in_dim(chip_sum, my_chip, axis=0, keepdims=False)

    down_slabs = g_down.reshape(N_DEV, dff // N_DEV, d)
    (dhn,), (got,) = _matmul("mm_up_dx", du, w_up, "nt", [F32], comm=_Comm("pair", [down_slabs]))
    down_sum = _pair_sum("pair_sum_down", down_slabs, got, core)
    (g_up,) = _matmul("mm_up_dw", hn, du, "tn", [BF16], out_slabs=True)
    own["w_mlp_down"] = chip_own(down_sum)
    dh1, dh1b, g_norm_mlp = _norm_bwd("norm_mlp_bwd", h1, dhn, dh2, wts["norm_mlp_w"])
    (dmg,), (got,) = _matmul("mm_out_dx", dh1b, w_out, "nt", [F32], comm=_Comm("pair", [g_up]))
    up_sum = _pair_sum("pair_sum_up", g_up, got, core)
    own["w_mlp_up"] = chip_own(up_sum)
    (g_out,) = _matmul("mm_out_dw", mg, dh1b, "tn", [BF16])
    dpga, dpgb, dya, dyb, dba, dbb = _merge_bwd("merge_bwd", p_all, wts["gate_bias"], ya, yb, dmg, cfg)
    (dya_in,) = _matmul("mm_out_a_dx", dya, w_out_a, "nt", [F32])
    (g_out_a,) = _matmul("mm_out_a_dw", ya_in, dya, "tn", [BF16], out_slabs=True)
    (dyb_in,) = _matmul("mm_out_b_dx", dyb, w_out_b, "nt", [F32])
    (g_out_b,) = _matmul("mm_out_b_dw", yb_in, dyb, "tn", [BF16], out_slabs=True)
    dpb, dpc, dpu, g_conv8 = _conv_bwd("conv_bwd", p_all, wts["conv_w8"], dyb_in, cfg)
    dy_raw, dr_post, dk_post, dv_post, dg, g_lnw, g_lnb, g_rk = _post_bwd(
        "post_bwd", y_raw, pm, k2, g, *post_w, dya_in, cfg)
    (dr_rec, dlw, dk_rec, dv_rec, da_in, db_in), (recv["w_mlp_up"], recv["w_mlp_down"]) = _rec_bwd(
        "rec_bwd", pm, lw, k2, a_in, b_in, s_chk, dy_raw, cfg, comm=_Comm("chips", [up_sum, down_sum]))
    (dpm_r, dpm_k, dpm_v, dpl, g_w0, g_a0, g_kk, g_ka, g_wd, g_wi, g_wg) = _prep_bwd(
        "prep_bwd", pm, cfg, *prep_w, (dlw, dk_rec, dk_post, da_in, db_in, dg),
        (dr_rec, dr_post), (dv_rec, dv_post))
    mu = wts["mu_pad"]
    nb = dr // cb
    dps, dmus = [], []
    for s, dpm_s in enumerate((dpm_r, dpm_k, dpm_v)):
        dp_s, dmu_s = _mix_bwd("mix_bwd_%d" % s, [dpm_s], p_all, s * nb, mu[:, s * dr:(s + 1) * dr], cb)
        dps.append(dp_s)
        dmus.append(dmu_s)
    dp_l, dmu_l = _mix_bwd("mix_bwd_l", [dpl[j] for j in range(nb)], p_all, 3 * nb, mu[:, 3 * dr:], min(cb, lp))
    tail = [jnp.zeros((x.shape[0], cfg["wall"] - cfg["used"]), BF16)] if cfg["wall"] > cfg["used"] else []
    dp_all = jnp.concatenate(dps + [dp_l, dpb, dpc, dpu, dpga, dpgb] + tail, axis=1)
    ld, li, lora = cfg["ld"], cfg["li"], cfg["lora"]
    g_small = jnp.concatenate([g_wd[:ld], g_wi[ld:ld + li], g_wg[ld + li:lora], g_conv8[:3]], axis=0)
    g_small = jnp.pad(g_small, ((0, cfg["small_rows"] - g_small.shape[0]), (0, 0)))
    direct = dict(w_out_a=g_out_a, w_out_b=g_out_b, w_out=g_out.reshape(N_DEV, d // N_DEV, d),
                  small=_col_slabs(g_small))
    (g_all,), got4 = _matmul("mm_in_dw", xn, dp_all, "tn", [BF16],
                             comm=_Comm("exchange", list(direct.values())))
    for n, slabs, r in zip(direct, direct.values(), got4):
        own[n] = lax.dynamic_index_in_dim(slabs, me, axis=0, keepdims=False)
        recv[n] = r
    in_slabs = _slabs_from_padded(g_all, 3 * dr + lora, lp - lora,
                                  (cfg["used"] - lp + lora) // N_DEV)
    (got,) = _comm_call("pair_exchange", _Comm("pair", [in_slabs]))
    in_sum = _pair_sum("pair_sum_in", in_slabs, got, core)
    (dxn,), (recv["w_in"],) = _matmul("mm_in_dx", dp_all, w_all, "nt", [F32], comm=_Comm("chips", [in_sum]))
    own["w_in"] = chip_own(in_sum)
    grad_x, _, g_norm_mix = _norm_bwd("norm_mix_bwd", x, dxn, dh1, wts["norm_mix_w"])
    grads = dict(
        norm_mix_w=g_norm_mix, gate_bias=jnp.concatenate([dba, dbb], axis=1),
        mu_pad=jnp.concatenate(dmus + [dmu_l], axis=1), w0=g_w0, a0=g_a0, k_k=g_kk, k_a=g_ka,
        r_k=g_rk, lnx_w=g_lnw, lnx_b=g_lnb, norm_mlp_w=g_norm_mlp, norm_final_w=g_norm_final)
    return loss, grad_x, grads, own, recv


_SMALL = ("norm_mix_w", "gate_bias", "shift_mu", "w0", "a0", "k_k", "k_a", "r_k", "lnx_w", "lnx_b",
          "norm_mlp_w", "norm_final_w")
_ORDER = ("norm_mix_w", "w_in", "gate_bias", "shift_mu", "w0", "w_decay_up", "a0", "w_iclr_up", "w_gate_up",
          "k_k", "k_a", "r_k", "lnx_w", "lnx_b", "w_out_a", "conv_w", "w_out_b", "w_out", "norm_mlp_w",
          "w_mlp_up", "w_mlp_down", "norm_final_w")


def _step(x, target, w, m, v):
    t, d = x.shape[1], x.shape[2]
    dr = w["w0"].shape[-1]
    ld, li, lg = w["w_decay_up"].shape[1], w["w_iclr_up"].shape[1], w["w_gate_up"].shape[1]
    lora = ld + li + lg
    lp = _round_up(lora, LANES)
    dc = w["conv_w"].shape[-1] * N_DEV
    cb = math.gcd(math.gcd(lp, dr), 512)
    used = 3 * dr + lp + 3 * dc + 2 * d
    wall = _round_up(used, 1024 if used > MAX_FULL_K else LANES)
    small_rows = ld + li + lg + 3
    cfg = dict(d=d, dr=dr, dc=dc, lp=lp, cb=cb, off_conv=3 * dr + lp, off_gate=3 * dr + lp + 3 * dc, used=used,
               wall=wall, ld=ld, li=li, lora=lora, small_rows=_round_up(small_rows, SUBLANES))
    x2, tg2 = x[0], target[0]

    small_sh = jnp.concatenate([w["w_decay_up"][0], w["w_iclr_up"][0], w["w_gate_up"][0], w["conv_w"][0]], axis=0)
    small_sh = jnp.pad(small_sh, ((0, _round_up(small_rows, SUBLANES) - small_rows), (0, 0)))
    big = ("w_in",) + _MID
    w_in_b = w["w_in"][0].astype(BF16)
    top8, gsm = _comm_call("gather_weights", _Comm("gather", [w_in_b[:d // 2], small_sh]))
    shards = {n: w[n][0].astype(BF16) for n in _MID}
    shards["w_in_bot"] = w_in_b[d // 2:]
    w_top = _padded_from_slabs(top8, 3 * dr + lora, lp - lora, wall)
    sm = _cols(gsm)
    lora_full = sm[:lora]

    def lora_pad(lo, hi):
        rows = lax.broadcasted_iota(jnp.int32, (lp, 1), 0)
        full = jnp.pad(lora_full, ((0, lp - lora), (0, 0)))
        return jnp.where(jnp.logical_and(rows >= lo, rows < hi), full, 0.0)

    conv_w8 = jnp.pad(sm[lora:lora + 3], ((0, SUBLANES - 3), (0, 0)))
    mu_pad = jnp.pad(w["shift_mu"], ((0, 0), (0, lp - lora)))
    wts = dict(
        w_top=w_top, wd=lora_pad(0, ld), wi=lora_pad(ld, ld + li), wg=lora_pad(ld + li, lora), conv_w8=conv_w8,
        mu_pad=mu_pad, norm_mix_w=w["norm_mix_w"], gate_bias=w["gate_bias"], w0=w["w0"], a0=w["a0"],
        k_k=w["k_k"], k_a=w["k_a"], r_k=w["r_k"].reshape(1, dr), lnx_w=w["lnx_w"], lnx_b=w["lnx_b"],
        norm_mlp_w=w["norm_mlp_w"], norm_final_w=w["norm_final_w"].reshape(1, d))

    loss, grad_x, gr, own, received = _local_step(x2, tg2, wts, shards, cfg)

    small_g = dict(norm_mix_w=gr["norm_mix_w"], gate_bias=gr["gate_bias"], shift_mu=gr["mu_pad"][:, :3 * dr + lora],
                   w0=gr["w0"], a0=gr["a0"], k_k=gr["k_k"], k_a=gr["k_a"], r_k=gr["r_k"], lnx_w=gr["lnx_w"],
                   lnx_b=gr["lnx_b"], norm_mlp_w=gr["norm_mlp_w"], norm_final_w=gr["norm_final_w"])
    sizes = [small_g[n].size for n in _SMALL]
    total = sum(sizes) + 1
    prow = _round_up(total, LANES * SUBLANES) // LANES

    def pack(parts):
        flat = jnp.concatenate([p.reshape(-1) for p in parts])
        return jnp.pad(flat, (0, prow * LANES - flat.size)).reshape(prow, LANES)

    g_packed = _all_reduce_small("reduce_small", pack([small_g[n] for n in _SMALL] + [loss[0, :1]]))
    one = jnp.zeros((1,), F32)
    packed = [pack([d_[n] for n in _SMALL] + [one]) for d_ in (w, m, v)]
    sm_out = _adamw("adamw_small", *packed, g_packed)
    loss_out = g_packed.reshape(-1)[total - 1]

    def unpack(flat2d):
        flat = flat2d.reshape(-1)
        out, o = {}, 0
        for n, s in zip(_SMALL, sizes):
            out[n] = flat[o:o + s].reshape(w[n].shape)
            o += s
        return out

    res = [unpack(a) for a in sm_out]

    def shard2d(a):
        return a.reshape(-1, a.shape[-1])

    for n in big:
        outs = _adamw("adamw_" + n, shard2d(w[n]), shard2d(m[n]), shard2d(v[n]), shard2d(own[n]),
                      received[n].reshape(received[n].shape[:1] + shard2d(own[n]).shape))
        for r_, o in zip(res, outs):
            r_[n] = o.reshape(w[n].shape)
    sm_names = ("w_decay_up", "w_iclr_up", "w_gate_up", "conv_w")
    stack = lambda d_: jnp.pad(jnp.concatenate([d_[n][0] for n in sm_names], axis=0),
                               ((0, _round_up(small_rows, SUBLANES) - small_rows), (0, 0)))
    outs = _adamw("adamw_stack", stack(w), stack(m), stack(v), own["small"], received["small"])
    bounds = (0, ld, ld + li, lora, lora + 3)
    for r_, o in zip(res, outs):
        for q, n in enumerate(sm_names):
            r_[n] = o[bounds[q]:bounds[q + 1]].reshape(w[n].shape)

    grad, delta, new_m, new_v = res
    return (loss_out, grad_x[None], *[grad[n] for n in _ORDER], *[delta[n] for n in _ORDER],
            *[new_m[n] for n in _ORDER], *[new_v[n] for n in _ORDER])


def kernel(x, norm_mix_w, w_in, gate_bias, shift_mu, w0, w_decay_up, a0, w_iclr_up, w_gate_up, k_k, k_a, r_k, lnx_w, lnx_b, w_out_a, conv_w, w_out_b, w_out, norm_mlp_w, w_mlp_up, w_mlp_down, norm_final_w, loss_target, m_norm_mix_w, m_w_in, m_gate_bias, m_shift_mu, m_w0, m_w_decay_up, m_a0, m_w_iclr_up, m_w_gate_up, m_k_k, m_k_a, m_r_k, m_lnx_w, m_lnx_b, m_w_out_a, m_conv_w, m_w_out_b, m_w_out, m_norm_mlp_w, m_w_mlp_up, m_w_mlp_down, m_norm_final_w, v_norm_mix_w, v_w_in, v_gate_bias, v_shift_mu, v_w0, v_w_decay_up, v_a0, v_w_iclr_up, v_w_gate_up, v_k_k, v_k_a, v_r_k, v_lnx_w, v_lnx_b, v_w_out_a, v_conv_w, v_w_out_b, v_w_out, v_norm_mlp_w, v_w_mlp_up, v_w_mlp_down, v_norm_final_w):
    w = dict(zip(_ORDER, (norm_mix_w, w_in, gate_bias, shift_mu, w0, w_decay_up, a0, w_iclr_up, w_gate_up, k_k, k_a,
                          r_k, lnx_w, lnx_b, w_out_a, conv_w, w_out_b, w_out, norm_mlp_w, w_mlp_up, w_mlp_down,
                          norm_final_w)))
    m = dict(zip(_ORDER, (m_norm_mix_w, m_w_in, m_gate_bias, m_shift_mu, m_w0, m_w_decay_up, m_a0, m_w_iclr_up,
                          m_w_gate_up, m_k_k, m_k_a, m_r_k, m_lnx_w, m_lnx_b, m_w_out_a, m_conv_w, m_w_out_b,
                          m_w_out, m_norm_mlp_w, m_w_mlp_up, m_w_mlp_down, m_norm_final_w)))
    v = dict(zip(_ORDER, (v_norm_mix_w, v_w_in, v_gate_bias, v_shift_mu, v_w0, v_w_decay_up, v_a0, v_w_iclr_up,
                          v_w_gate_up, v_k_k, v_k_a, v_r_k, v_lnx_w, v_lnx_b, v_w_out_a, v_conv_w, v_w_out_b,
                          v_w_out, v_norm_mlp_w, v_w_mlp_up, v_w_mlp_down, v_norm_final_w)))
    return _step(x, loss_target, w, m, v)
```

```python
import math

import jax
import jax.numpy as jnp
from jax import lax
from jax.experimental import pallas as pl
from jax.experimental.pallas import tpu as pltpu

F32 = jnp.float32
BF16 = jnp.bfloat16
MESH = pl.DeviceIdType.MESH

N_DEV = 8
HEAD = 64
LANES = 128
SUBLANES = 8
CHUNK = 64
RMS_EPS = 1e-5
LNX_EPS = 64e-5
L2_EPS = 1e-12
ADAM_LR = 0.001
ADAM_B1 = 0.9
ADAM_B2 = 0.999
ADAM_EPS = 1e-08
ADAM_WD = 0.01
ADAM_STEP = 10
VMEM_LIMIT = 48 * 1024 * 1024
MAX_FULL_K = 4096
LIGHT_ROWS = (512, 256, 128, 64, 32, 16, 8)


def _cparams(sem):
    return pltpu.CompilerParams(dimension_semantics=sem, vmem_limit_bytes=VMEM_LIMIT)


def _tile(dim, cands):
    for c in cands:
        if c <= dim and dim % c == 0:
            return c
    return dim


def _my_pos():
    return lax.axis_index("x"), lax.axis_index("y"), lax.axis_index("c")


def _peer(pos, r):
    x, y, c = pos
    return (1 - x if r & 4 else x, 1 - y if r & 2 else y, 1 - c if r & 1 else c)


def _slot(pos):
    return 4 * pos[0] + 2 * pos[1] + pos[2]


class _Comm:
    def __init__(self, kind, arrs):
        self.kind, self.arrs, self.n = kind, list(arrs), len(arrs)
        if kind == "gather":
            self.out_shape = [jax.ShapeDtypeStruct((N_DEV,) + a.shape, a.dtype) for a in arrs]
        elif kind == "exchange":
            self.out_shape = [jax.ShapeDtypeStruct((N_DEV - 1,) + a.shape[1:], a.dtype) for a in arrs]
        elif kind == "pair":
            self.out_shape = [jax.ShapeDtypeStruct((N_DEV // 2,) + a.shape[1:], a.dtype) for a in arrs]
        else:
            self.out_shape = [jax.ShapeDtypeStruct((3,) + a.shape[1:], a.dtype) for a in arrs]
        self.scratch = [pltpu.SemaphoreType.DMA((7 * self.n,)), pltpu.SemaphoreType.DMA((7 * self.n,))]
        if kind == "gather":
            self.scratch.append(pltpu.SemaphoreType.DMA((self.n,)))

    def _exchange_copies(self, in_refs, out_refs, sems):
        me = _my_pos()
        x, y, c = me
        cps = []
        for ai in range(self.n):
            if self.kind == "exchange":
                todo = [(in_refs[ai].at[_slot(_peer(me, r))], out_refs[ai].at[r - 1], _peer(me, r), r - 1)
                        for r in range(1, N_DEV)]
            elif self.kind == "pair":
                todo = [(in_refs[ai].at[2 * q + 1 - c], out_refs[ai].at[q], (x, y, 1 - c), q)
                        for q in range(N_DEV // 2)]
            else:
                chips = [(1 - x, y), (x, 1 - y), (1 - x, 1 - y)]
                todo = [(in_refs[ai].at[2 * cx + cy], out_refs[ai].at[j], (cx, cy, c), j)
                        for j, (cx, cy) in enumerate(chips)]
            for src, dst, to, k in todo:
                cps.append(pltpu.make_async_remote_copy(
                    src_ref=src, dst_ref=dst, send_sem=sems[0].at[ai * 7 + k], recv_sem=sems[1].at[ai * 7 + k],
                    device_id=to, device_id_type=MESH))
        return cps

    def _gather_parts(self, in_refs, out_refs, sems):
        x, y, c = _my_pos()
        me, sibling = (x, y, c), (x, y, 1 - c)
        chips = [(1 - x, y), (x, 1 - y), (1 - x, 1 - y)]

        def copy(ai, k, block, to, src=None):
            dst = out_refs[ai].at[_slot(block)]
            return pltpu.make_async_remote_copy(
                src_ref=dst if src is None else src, dst_ref=dst, send_sem=sems[0].at[ai * 7 + k],
                recv_sem=sems[1].at[ai * 7 + k], device_id=to, device_id_type=MESH)

        mine = [pltpu.make_async_copy(in_refs[ai], out_refs[ai].at[_slot(me)], sems[2].at[ai])
                for ai in range(self.n)]
        first = []
        for ai in range(self.n):
            first.append(copy(ai, 0, me, sibling, src=in_refs[ai]))
            first += [copy(ai, 1 + j, me, (*chip, c), src=in_refs[ai]) for j, chip in enumerate(chips)]
        return me, sibling, chips, c, copy, mine, first

    def start(self, in_refs, out_refs, sems):
        if self.kind != "gather":
            for cp in self._exchange_copies(in_refs, out_refs, sems):
                cp.start()
            return
        _, _, _, _, _, mine, first = self._gather_parts(in_refs, out_refs, sems)
        for cp in mine + first:
            cp.start()

    def wait(self, in_refs, out_refs, sems):
        if self.kind != "gather":
            for cp in self._exchange_copies(in_refs, out_refs, sems):
                cp.wait()
            return
        me, sibling, chips, c, copy, mine, first = self._gather_parts(in_refs, out_refs, sems)
        passed = []
        for ai in range(self.n):
            for j, chip in enumerate(chips):
                copy(ai, 1 + j, (*chip, c), me).wait_recv()
                fwd = copy(ai, 4 + j, (*chip, c), sibling)
                fwd.start()
                passed.append(fwd)
        for ai in range(self.n):
            copy(ai, 0, sibling, me).wait_recv()
            for j, chip in enumerate(chips):
                copy(ai, 4 + j, (*chip, 1 - c), me).wait_recv()
        for cp in first + passed:
            cp.wait_send()
        for cp in mine:
            cp.wait()


def _hosted_call(body, name, comm, first, last, *, args, in_specs, out_shape, out_specs, scratch, grid, sem):
    if comm is None:
        return pl.pallas_call(body, name=name, out_shape=out_shape, grid=grid, in_specs=in_specs, out_specs=out_specs,
                              scratch_shapes=scratch, compiler_params=_cparams(sem))(*args)
    ni, no, ns, nc = len(args), len(out_shape), len(scratch), comm.n
    hbm = pl.BlockSpec(memory_space=pl.ANY)

    def hosted(*refs):
        ins, cin = refs[:ni], refs[ni:ni + nc]
        outs, cout = refs[ni + nc:ni + nc + no], refs[ni + nc + no:ni + 2 * nc + no]
        scr, sems = refs[ni + 2 * nc + no:ni + 2 * nc + no + ns], refs[ni + 2 * nc + no + ns:]

        @pl.when(first())
        def _():
            comm.start(cin, cout, sems)

        body(*ins, *outs, *scr)

        @pl.when(last())
        def _():
            comm.wait(cin, cout, sems)

    res = pl.pallas_call(
        hosted, name=name, out_shape=list(out_shape) + comm.out_shape, grid=grid,
        in_specs=list(in_specs) + [hbm] * nc, out_specs=list(out_specs) + [hbm] * nc,
        scratch_shapes=list(scratch) + comm.scratch,
        compiler_params=_cparams(("arbitrary",) * len(grid)))(*args, *comm.arrs)
    return res[:no], res[no:]


_DIMS = {"nn": ((1,), (0,)), "nt": ((1,), (1,)), "tn": ((0,), (0,))}


def _matmul(name, a, b, mode, out_dtypes, epi=None, extras=(), comm=None, out_slabs=False):
    bs = list(b) if isinstance(b, (tuple, list)) else [b]
    if mode == "nn":
        (m, k), n = a.shape, b.shape[1]
    elif mode == "nt":
        (m, k), n = a.shape, sum(x.shape[0] for x in bs)
    else:
        (k, m), n = a.shape, b.shape[1]
    tm = _tile(m, (1024, 512, 256, 128, 64, 32, 16, 8))
    if k <= MAX_FULL_K:
        tk, tn = k, _tile(n // N_DEV if out_slabs else n // len(bs), (512, 256, 128))
    else:
        tk, tn = _tile(k, (2048, 1024, 512, 256, 128)), _tile(n // len(bs), (1024, 512, 256, 128))
    nk = k // tk
    gm, gn = m // tm, n // tn
    a_spec = pl.BlockSpec((tk, tm), lambda i, j, q: (q, i)) if mode == "tn" else pl.BlockSpec((tm, tk), lambda i, j, q: (i, q))
    b_spec = pl.BlockSpec((tn, tk), lambda i, j, q: (j, q)) if mode == "nt" else pl.BlockSpec((tk, tn), lambda i, j, q: (q, j))
    gh = gn // 2
    b_specs = [b_spec] if len(bs) == 1 else [
        pl.BlockSpec((tn, tk), lambda i, j, q: (jnp.minimum(j, gh - 1), q)),
        pl.BlockSpec((tn, tk), lambda i, j, q: (jnp.maximum(j - gh, 0), q))]
    mn_spec = pl.BlockSpec((tm, tn), lambda i, j, q: (i, j))
    per = n // N_DEV // tn if out_slabs else 0
    out_spec = pl.BlockSpec((None, tm, tn), lambda i, j, q: (j // per, i, j % per)) if out_slabs else mn_spec
    ne, no = len(extras), len(out_dtypes)
    dims = (_DIMS[mode], ((), ()))
    keep_t = mode == "tn" and nk == 1 and gn > 1

    def finish(r, extra_refs, out_refs):
        outs = (r,) if epi is None else epi(r, *[e[...] for e in extra_refs])
        for o_ref, o in zip(out_refs, outs):
            o_ref[...] = o.astype(o_ref.dtype)

    def body(a_ref, *rest):
        if len(bs) == 1:
            return step(a_ref, *rest)

        @pl.when(pl.program_id(1) < gh)
        def _():
            step(a_ref, rest[0], *rest[2:])

        @pl.when(pl.program_id(1) >= gh)
        def _():
            step(a_ref, rest[1], *rest[2:])

    def step(a_ref, b_ref, *rest):
        extra_refs, out_refs = rest[:ne], rest[ne:ne + no]
        if keep_t:
            at = rest[ne + no]

            @pl.when(pl.program_id(1) == 0)
            def _():
                at[...] = a_ref[...].T

            part = jnp.dot(at[...], b_ref[...], preferred_element_type=F32)
        else:
            part = lax.dot_general(a_ref[...], b_ref[...], dims, preferred_element_type=F32)
        if nk == 1:
            finish(part, extra_refs, out_refs)
            return
        acc = rest[ne + no]
        q = pl.program_id(2)

        @pl.when(q == 0)
        def _():
            acc[...] = part

        @pl.when(jnp.logical_and(q > 0, q < nk - 1))
        def _():
            acc[...] += part

        @pl.when(q == nk - 1)
        def _():
            finish(acc[...] + part, extra_refs, out_refs)

    def first():
        return jnp.logical_and(jnp.logical_and(pl.program_id(0) == 0, pl.program_id(1) == 0), pl.program_id(2) == 0)

    def last():
        return jnp.logical_and(jnp.logical_and(pl.program_id(0) == gm - 1, pl.program_id(1) == gn - 1),
                               pl.program_id(2) == nk - 1)

    return _hosted_call(
        body, name, comm, first, last,
        args=[a, *bs, *extras], in_specs=[a_spec] + b_specs + [mn_spec] * ne,
        out_shape=[jax.ShapeDtypeStruct((N_DEV, m, n // N_DEV) if out_slabs else (m, n), dt) for dt in out_dtypes],
        out_specs=[out_spec] * no,
        scratch=[pltpu.VMEM((tm, tn), F32)] if nk > 1 else ([pltpu.VMEM((tm, tk), a.dtype)] if keep_t else []),
        grid=(gm, gn, nk), sem=("parallel", "arbitrary" if keep_t else "parallel", "arbitrary"))


@jax.custom_vjp
def _mm(a, w):
    return jnp.dot(a.astype(BF16), w.astype(BF16), preferred_element_type=F32)


def _mm_fwd(a, w):
    return _mm(a, w), (a, w)


def _mm_bwd(res, ct):
    a, w = res
    ctb = ct.astype(BF16)
    da = lax.dot_general(ctb, w.astype(BF16), (((1,), (1,)), ((), ())), preferred_element_type=F32)
    dw = lax.dot_general(a.astype(BF16), ctb, (((0,), (0,)), ((), ())), preferred_element_type=F32)
    return da, dw


_mm.defvjp(_mm_fwd, _mm_bwd)


def _split3(x):
    hi = x.astype(BF16)
    r1 = x - hi.astype(F32)
    mid = r1.astype(BF16)
    lo = (r1 - mid.astype(F32)).astype(BF16)
    return hi, mid, lo


def _head_ones(width):
    r = lax.broadcasted_iota(jnp.int32, (width, width), 0) // HEAD
    c = lax.broadcasted_iota(jnp.int32, (width, width), 1) // HEAD
    return (r == c).astype(BF16)


@jax.custom_vjp
def _segsum(x):
    ones = _head_ones(x.shape[-1])
    out = None
    for piece in _split3(x):
        t = jnp.dot(piece, ones, preferred_element_type=F32)
        out = t if out is None else out + t
    return out


_segsum.defvjp(lambda x: (_segsum(x), None), lambda _, ct: (_segsum(ct),))


def _softplus(z):
    return jnp.maximum(z, 0.0) + jnp.log(1.0 + jnp.exp(-jnp.abs(z)))


def _sigmoid(z):
    return 1.0 / (1.0 + jnp.exp(-z))


def _rms(x, w):
    ms = jnp.mean(x * x, axis=-1, keepdims=True)
    return x * lax.rsqrt(ms + RMS_EPS) * w


def _row(ref, i):
    return ref[pl.ds(i, 1), :]


def _shift_down(x, prev_ref, n, first):
    rolled = pltpu.roll(x, n, 0)
    rows = lax.broadcasted_iota(jnp.int32, x.shape, 0)
    for q in range(n):
        halo = jnp.where(first, 0.0, _row(prev_ref, SUBLANES - n + q))
        rolled = jnp.where(rows == q, halo, rolled)
    return rolled


def _shift_up(x, next_ref, n, last):
    t = x.shape[0]
    rolled = pltpu.roll(x, t - n, 0)
    rows = lax.broadcasted_iota(jnp.int32, x.shape, 0)
    for q in range(n):
        halo = jnp.where(last, 0.0, _row(next_ref, q))
        rolled = jnp.where(rows == t - n + q, halo, rolled)
    return rolled


def _acc_out(ref, val, first):
    @pl.when(first)
    def _():
        ref[...] = val

    @pl.when(jnp.logical_not(first))
    def _():
        ref[...] += val


def _prep_fn(k, plm, w0, a0, kkw, kaw, wd, wi, wg):
    w_log = -_softplus(-(w0 + _mm(jnp.tanh(plm), wd))) - 0.5
    lw = -jnp.exp(w_log)
    a_g = _sigmoid(a0 + _mm(plm, wi))
    g = _mm(_sigmoid(plm), wg)
    kk = k * kkw
    kk = kk / jnp.maximum(jnp.sqrt(_segsum(kk * kk)), L2_EPS)
    k2 = k * (1.0 + (a_g - 1.0) * kaw)
    return lw, k2, -kk, kk * a_g, g


def _post_fn(y, r, k2, v, g, lnw, lnb, rk):
    mu = _segsum(y) * (1.0 / HEAD)
    yc = y - mu
    var = _segsum(yc * yc) * (1.0 / HEAD)
    yn = yc * lax.rsqrt(var + LNX_EPS) * lnw + lnb
    bonus = _segsum(r * k2 * rk) * v
    return (yn + bonus) * g


def _merge_fn(pga, pgb, ba, bb, ya, yb):
    return _sigmoid(pga + ba) * ya + _sigmoid(pgb + bb) * yb


_NN, _NT, _TN = ((2,), (1,)), ((2,), (2,)), ((1,), (1,))


def _dot3(a, b, dims):
    ah = a.astype(BF16)
    al = (a - ah.astype(F32)).astype(BF16)
    bh = b.astype(BF16)
    bl = (b - bh.astype(F32)).astype(BF16)
    dg = lambda p, q: lax.dot_general(p, q, (dims, ((0,), (0,))), preferred_element_type=F32)
    (ca,), (cb_,) = dims
    if a.shape[ca] % LANES:
        return dg(ah, bh) + (dg(ah, bl) + dg(al, bh))
    cross = dg(jnp.concatenate([ah, al], axis=ca), jnp.concatenate([bl, bh], axis=cb_))
    return dg(ah, bh) + cross


@jax.custom_vjp
def _dnn(a, b):
    return _dot3(a, b, _NN)


@jax.custom_vjp
def _dnt(a, b):
    return _dot3(a, b, _NT)


@jax.custom_vjp
def _dtn(a, b):
    return _dot3(a, b, _TN)


_dnn.defvjp(lambda a, b: (_dnn(a, b), (a, b)), lambda res, ct: (_dnt(ct, res[1]), _dtn(res[0], ct)))
_dnt.defvjp(lambda a, b: (_dnt(a, b), (a, b)), lambda res, ct: (_dnn(ct, res[1]), _dtn(ct, res[0])))
_dtn.defvjp(lambda a, b: (_dtn(a, b), (a, b)), lambda res, ct: (_dnt(res[1], ct), _dnn(res[0], ct)))


@jax.custom_vjp
def _unit_lower_inverse(low):
    n = low.shape[-1]
    ri = lax.broadcasted_iota(jnp.int32, low.shape, 1)
    ci = lax.broadcasted_iota(jnp.int32, low.shape, 2)
    inv = (ri == ci).astype(F32) + low
    pw = low
    for _ in range(int(math.log2(n // 2)) - 1):
        pw = _dnn(pw, pw)
        inv = inv + _dnn(inv, pw)
    return inv


def _unit_lower_inverse_bwd(inv, ct):
    return (_dnt(_dtn(inv, ct), inv),)


_unit_lower_inverse.defvjp(lambda low: (_unit_lower_inverse(low),) * 2, _unit_lower_inverse_bwd)


def _chunk_fn(s, r, lw, k, v, a, b):
    np_, c = r.shape[0], r.shape[1]
    c2 = 2 * c
    ri = lax.broadcasted_iota(jnp.int32, (np_, c, c), 1)
    ci = lax.broadcasted_iota(jnp.int32, (np_, c, c), 2)
    tri = (ri >= ci).astype(F32)
    cum = _dnn(tri, lw)
    tot = jnp.sum(lw, axis=1, keepdims=True)
    g_in, g_inv, g_out = jnp.exp(cum), jnp.exp(-cum), jnp.exp(tot - cum)
    lane_head = lax.broadcasted_iota(jnp.int32, (1, 2, 1, LANES), 3) // HEAD
    which = lax.broadcasted_iota(jnp.int32, (1, 2, 1, LANES), 1)
    hmask = (lane_head == which).astype(F32)

    def st(x):
        return (x[:, None] * hmask).reshape(np_, c2, LANES)

    r2, a2 = st(r * g_in), st(a * jnp.exp(cum - lw))
    b2, k2, v2 = st(b * g_inv), st(k * g_inv), st(v)
    bo2, ko2 = st(b * g_out), st(k * g_out)
    r2i = lax.broadcasted_iota(jnp.int32, (np_, c2, c2), 1)
    c2i = lax.broadcasted_iota(jnp.int32, (np_, c2, c2), 2)
    same = (r2i >= c) == (c2i >= c)
    strict = jnp.logical_and(same, r2i > c2i)
    incl = jnp.logical_and(same, r2i >= c2i)
    lab = jnp.where(strict, _dnt(a2, b2), 0.0)
    lak = jnp.where(strict, _dnt(a2, k2), 0.0)
    mrb = jnp.where(incl, _dnt(r2, b2), 0.0)
    mrk = jnp.where(incl, _dnt(r2, k2), 0.0)
    x2 = _dnt(a2, s) + _dnn(lak, v2)
    u2 = _dnn(_unit_lower_inverse(lab), x2)
    y2 = _dnt(r2, s) + _dnn(mrb, u2) + _dnn(mrk, v2)
    y = jnp.sum(y2.reshape(np_, 2, c, LANES), axis=1)
    s_new = s * jnp.exp(tot) + _dtn(u2, bo2) + _dtn(v2, ko2)
    return y, s_new


def _norm_fwd(name, x, add, w, want_sum):
    t, d = x.shape
    tt = _tile(t, (128, 64, 32, 16, 8))
    row = pl.BlockSpec((tt, d), lambda i: (i, 0))
    par = pl.BlockSpec((1, d), lambda i: (0, 0))
    has_add = add is not None

    def body(*refs):
        x_ref = refs[0]
        add_ref = refs[1] if has_add else None
        w_ref = refs[1 + has_add]
        outs = refs[2 + has_add:]
        h = x_ref[...] + add_ref[...] if has_add else x_ref[...]
        if want_sum:
            outs[0][...] = h
        outs[-1][...] = _rms(h, w_ref[...]).astype(BF16)

    out_shape = ([jax.ShapeDtypeStruct((t, d), F32)] if want_sum else []) + [jax.ShapeDtypeStruct((t, d), BF16)]
    return pl.pallas_call(
        body, name=name, out_shape=out_shape, grid=(t // tt,),
        in_specs=[row] + ([row] if has_add else []) + [par],
        out_specs=[row] * len(out_shape),
        compiler_params=_cparams(("arbitrary",)),
    )(*([x] + ([add] if has_add else []) + [w]))


def _norm_bwd(name, xin, dy, dres, w):
    t, d = xin.shape
    tt = _tile(t, (128, 64, 32, 16, 8))
    row = pl.BlockSpec((tt, d), lambda i: (i, 0))
    par = pl.BlockSpec((1, d), lambda i: (0, 0))

    def body(x_ref, dy_ref, dres_ref, w_ref, dx_ref, dxb_ref, dw_ref):
        _, vjp = jax.vjp(_rms, x_ref[...], w_ref[...])
        dx, dw = vjp(dy_ref[...])
        dx = dx + dres_ref[...]
        dx_ref[...] = dx
        dxb_ref[...] = dx.astype(BF16)
        _acc_out(dw_ref, dw, pl.program_id(0) == 0)

    return pl.pallas_call(
        body, name=name,
        out_shape=[jax.ShapeDtypeStruct((t, d), F32), jax.ShapeDtypeStruct((t, d), BF16),
                   jax.ShapeDtypeStruct((1, d), F32)],
        grid=(t // tt,), in_specs=[row, row, row, par], out_specs=[row, row, par],
        compiler_params=_cparams(("arbitrary",)),
    )(xin, dy, dres, w)


def _final(name, h1, md, target, w):
    t, d = h1.shape
    tt = _tile(t, (128, 64, 32, 16, 8))
    row = pl.BlockSpec((tt, d), lambda i: (i, 0))
    par = pl.BlockSpec((1, d), lambda i: (0, 0))
    one = pl.BlockSpec((1, LANES), lambda i: (0, 0))

    def body(h1_ref, md_ref, tg_ref, w_ref, loss_ref, dh_ref, dhb_ref, dw_ref):
        tg = tg_ref[...]

        def f(h, wv):
            err = _rms(h, wv) - tg
            return 0.5 * jnp.sum(jnp.mean(err * err, axis=-1, keepdims=True), axis=0, keepdims=True)

        loss, vjp = jax.vjp(f, h1_ref[...] + md_ref[...], w_ref[...])
        dh, dw = vjp(jnp.ones((1, 1), F32))
        dh_ref[...] = dh
        dhb_ref[...] = dh.astype(BF16)
        first = pl.program_id(0) == 0
        _acc_out(dw_ref, dw, first)
        _acc_out(loss_ref, jnp.broadcast_to(loss, (1, LANES)), first)

    return pl.pallas_call(
        body, name=name,
        out_shape=[jax.ShapeDtypeStruct((1, LANES), F32), jax.ShapeDtypeStruct((t, d), F32),
                   jax.ShapeDtypeStruct((t, d), BF16), jax.ShapeDtypeStruct((1, d), F32)],
        grid=(t // tt,), in_specs=[row, row, row, par], out_specs=[one, row, row, par],
        compiler_params=_cparams(("arbitrary",)),
    )(h1, md, target, w)


def _halo_specs(tt, cb, nrow8, col_of):
    prev = pl.BlockSpec((SUBLANES, cb), lambda i, j: (jnp.maximum(i * (tt // SUBLANES) - 1, 0), col_of(j)))
    nxt = pl.BlockSpec((SUBLANES, cb), lambda i, j: (jnp.minimum((i + 1) * (tt // SUBLANES), nrow8 - 1), col_of(j)))
    return prev, nxt


def _mix_fwd(name, p_all, mu, width, cb):
    t = p_all.shape[0]
    tt = _tile(t, LIGHT_ROWS)
    main = pl.BlockSpec((tt, cb), lambda i, j: (i, j))
    prev, _ = _halo_specs(tt, cb, t // SUBLANES, lambda j: j)
    par = pl.BlockSpec((1, cb), lambda i, j: (0, j))

    def body(p_ref, prev_ref, mu_ref, o_ref):
        p = p_ref[...]
        o_ref[...] = p + (_shift_down(p, prev_ref, 1, pl.program_id(0) == 0) - p) * mu_ref[...]

    return pl.pallas_call(
        body, name=name, out_shape=jax.ShapeDtypeStruct((t, width), F32),
        grid=(t // tt, width // cb), in_specs=[main, prev, par], out_specs=main,
        compiler_params=_cparams(("arbitrary", "arbitrary")),
    )(p_all, p_all, mu)


def _mix_bwd(name, dpm_list, p_all, col0, mu, cb):
    t, width = dpm_list[0].shape
    tt = _tile(t, LIGHT_ROWS)
    n8 = t // SUBLANES
    nl = len(dpm_list)
    main = pl.BlockSpec((tt, cb), lambda j, i: (i, j))
    nxt = pl.BlockSpec((SUBLANES, cb), lambda j, i: (jnp.minimum((i + 1) * (tt // SUBLANES), n8 - 1), j))
    p_main = pl.BlockSpec((tt, cb), lambda j, i: (i, col0 + j))
    p_prev = pl.BlockSpec((SUBLANES, cb), lambda j, i: (jnp.maximum(i * (tt // SUBLANES) - 1, 0), col0 + j))
    par = pl.BlockSpec((1, cb), lambda j, i: (0, j))
    nt_ = t // tt

    def body(*refs):
        d_refs, dn_refs = refs[:nl], refs[nl:2 * nl]
        p_ref, pp_ref, mu_ref, dp_ref, dmu_ref, nx_scr = refs[2 * nl:]
        i = pl.program_id(1)
        dpm = d_refs[0][...]
        nx = dn_refs[0][...]
        for q in range(1, nl):
            dpm = dpm + d_refs[q][...]
            nx = nx + dn_refs[q][...]
        nx_scr[...] = nx
        mu_v = mu_ref[...]
        up = _shift_up(dpm, nx_scr, 1, i == nt_ - 1)
        dp_ref[...] = (dpm * (1.0 - mu_v) + up * mu_v).astype(BF16)
        p = p_ref[...]
        diff = _shift_down(p, pp_ref, 1, i == 0) - p
        _acc_out(dmu_ref, jnp.sum(dpm * diff, axis=0, keepdims=True), i == 0)

    return pl.pallas_call(
        body, name=name,
        out_shape=[jax.ShapeDtypeStruct((t, width), BF16), jax.ShapeDtypeStruct((1, width), F32)],
        grid=(width // cb, nt_),
        in_specs=[main] * nl + [nxt] * nl + [p_main, p_prev, par],
        out_specs=[main, par],
        scratch_shapes=[pltpu.VMEM((SUBLANES, cb), F32)],
        compiler_params=_cparams(("arbitrary", "arbitrary")),
    )(*dpm_list, *dpm_list, p_all, p_all, mu)


def _prep_fwd(name, pm, cfg, w0, a0, kkw, kaw, wd, wi, wg):
    t = pm.shape[0]
    dr, lp, cb = cfg["dr"], cfg["lp"], cfg["cb"]
    tt = _tile(t, (256, 128, 64, 32, 16, 8))
    nj = dr // cb
    kspec = pl.BlockSpec((tt, cb), lambda j, i: (i, nj + j))
    lspec = pl.BlockSpec((tt, lp), lambda j, i: (i, 3 * dr // lp))
    par = pl.BlockSpec((1, cb), lambda j, i: (0, j))
    wspec = pl.BlockSpec((lp, cb), lambda j, i: (0, j))
    out = pl.BlockSpec((tt, cb), lambda j, i: (i, j))

    def body(k_ref, l_ref, w0_ref, a0_ref, kk_ref, ka_ref, wd_ref, wi_ref, wg_ref, *outs):
        vals = _prep_fn(k_ref[...], l_ref[...], w0_ref[...], a0_ref[...], kk_ref[...], ka_ref[...],
                        wd_ref[...], wi_ref[...], wg_ref[...])
        for o_ref, val in zip(outs, vals):
            o_ref[...] = val

    return pl.pallas_call(
        body, name=name, out_shape=[jax.ShapeDtypeStruct((t, dr), F32)] * 5,
        grid=(nj, t // tt), in_specs=[kspec, lspec, par, par, par, par, wspec, wspec, wspec],
        out_specs=[out] * 5, compiler_params=_cparams(("arbitrary", "arbitrary")),
    )(pm, pm, w0, a0, kkw, kaw, wd, wi, wg)


def _prep_bwd(name, pm, cfg, w0, a0, kkw, kaw, wd, wi, wg, cts, dr_parts, dv_parts):
    t = pm.shape[0]
    dr, lp, cb = cfg["dr"], cfg["lp"], cfg["cb"]
    tt = _tile(t, (256, 128, 64, 32, 16, 8))
    nj = dr // cb
    kspec = pl.BlockSpec((tt, cb), lambda j, i: (i, nj + j))
    lspec = pl.BlockSpec((tt, lp), lambda j, i: (i, 3 * dr // lp))
    par = pl.BlockSpec((1, cb), lambda j, i: (0, j))
    wspec = pl.BlockSpec((lp, cb), lambda j, i: (0, j))
    blk = pl.BlockSpec((tt, cb), lambda j, i: (i, j))
    dpl_spec = pl.BlockSpec((None, tt, lp), lambda j, i: (j, i, 0))

    def body(k_ref, l_ref, w0_ref, a0_ref, kk_ref, ka_ref, wd_ref, wi_ref, wg_ref,
             dlw_ref, dk2a_ref, dk2b_ref, da_ref, db_ref, dg_ref, dr0_ref, dr1_ref, dv0_ref, dv1_ref,
             dpr_ref, dpk_ref, dpv_ref, dpl_ref, dw0_ref, da0_ref, dkk_ref, dka_ref, dwd_ref, dwi_ref, dwg_ref):
        _, vjp = jax.vjp(_prep_fn, k_ref[...], l_ref[...], w0_ref[...], a0_ref[...], kk_ref[...], ka_ref[...],
                         wd_ref[...], wi_ref[...], wg_ref[...])
        dk, dpl, dw0, da0, dkk, dka, dwd, dwi, dwg = vjp(
            (dlw_ref[...], dk2a_ref[...] + dk2b_ref[...], da_ref[...], db_ref[...], dg_ref[...]))
        dpr_ref[...] = dr0_ref[...] + dr1_ref[...]
        dpv_ref[...] = dv0_ref[...] + dv1_ref[...]
        dpk_ref[...] = dk
        dpl_ref[...] = dpl
        first = pl.program_id(1) == 0
        for ref, val in ((dw0_ref, dw0), (da0_ref, da0), (dkk_ref, dkk), (dka_ref, dka),
                         (dwd_ref, dwd), (dwi_ref, dwi), (dwg_ref, dwg)):
            _acc_out(ref, val, first)

    out_shape = ([jax.ShapeDtypeStruct((t, dr), F32)] * 3 + [jax.ShapeDtypeStruct((nj, t, lp), F32)]
                 + [jax.ShapeDtypeStruct((1, dr), F32)] * 4 + [jax.ShapeDtypeStruct((lp, dr), F32)] * 3)
    return pl.pallas_call(
        body, name=name, out_shape=out_shape, grid=(nj, t // tt),
        in_specs=[kspec, lspec, par, par, par, par, wspec, wspec, wspec] + [blk] * 10,
        out_specs=[blk] * 3 + [dpl_spec] + [par] * 4 + [wspec] * 3,
        compiler_params=_cparams(("arbitrary", "arbitrary")),
    )(pm, pm, w0, a0, kkw, kaw, wd, wi, wg, *cts, *dr_parts, *dv_parts)


def _post_specs(t, cfg):
    dr, cb = cfg["dr"], cfg["cb"]
    tt = _tile(t, (256, 128, 64, 32, 16, 8))
    nj = dr // cb
    blk = pl.BlockSpec((tt, cb), lambda j, i: (i, j))
    rspec = pl.BlockSpec((tt, cb), lambda j, i: (i, j))
    vspec = pl.BlockSpec((tt, cb), lambda j, i: (i, 2 * nj + j))
    par = pl.BlockSpec((1, cb), lambda j, i: (0, j))
    return tt, nj, blk, rspec, vspec, par


def _post_fwd(name, y, pm, k2, g, lnw, lnb, rk, cfg):
    t = y.shape[0]
    tt, nj, blk, rspec, vspec, par = _post_specs(t, cfg)

    def body(y_ref, r_ref, k_ref, v_ref, g_ref, lw_ref, lb_ref, rk_ref, o_ref):
        o_ref[...] = _post_fn(y_ref[...], r_ref[...], k_ref[...], v_ref[...], g_ref[...],
                              lw_ref[...], lb_ref[...], rk_ref[...]).astype(BF16)

    return pl.pallas_call(
        body, name=name, out_shape=jax.ShapeDtypeStruct((t, cfg["dr"]), BF16), grid=(nj, t // tt),
        in_specs=[blk, rspec, blk, vspec, blk, par, par, par], out_specs=blk,
        compiler_params=_cparams(("arbitrary", "arbitrary")),
    )(y, pm, k2, pm, g, lnw, lnb, rk)


def _post_bwd(name, y, pm, k2, g, lnw, lnb, rk, dout, cfg):
    t = y.shape[0]
    tt, nj, blk, rspec, vspec, par = _post_specs(t, cfg)

    def body(y_ref, r_ref, k_ref, v_ref, g_ref, lw_ref, lb_ref, rk_ref, do_ref,
             dy_ref, dr_ref, dk_ref, dv_ref, dg_ref, dlw_ref, dlb_ref, drk_ref):
        _, vjp = jax.vjp(_post_fn, y_ref[...], r_ref[...], k_ref[...], v_ref[...], g_ref[...],
                         lw_ref[...], lb_ref[...], rk_ref[...])
        dy, dr, dk, dv, dg, dlw, dlb, drk = vjp(do_ref[...])
        for ref, val in ((dy_ref, dy), (dr_ref, dr), (dk_ref, dk), (dv_ref, dv), (dg_ref, dg)):
            ref[...] = val
        first = pl.program_id(1) == 0
        for ref, val in ((dlw_ref, dlw), (dlb_ref, dlb), (drk_ref, drk)):
            _acc_out(ref, val, first)

    dr = cfg["dr"]
    return pl.pallas_call(
        body, name=name,
        out_shape=[jax.ShapeDtypeStruct((t, dr), F32)] * 5 + [jax.ShapeDtypeStruct((1, dr), F32)] * 3,
        grid=(nj, t // tt),
        in_specs=[blk, rspec, blk, vspec, blk, par, par, par, blk],
        out_specs=[blk] * 5 + [par] * 3,
        compiler_params=_cparams(("arbitrary", "arbitrary")),
    )(y, pm, k2, pm, g, lnw, lnb, rk, dout)


def _conv_specs(t, cfg):
    dc, cb = cfg["dc"], cfg["cb"]
    tt = _tile(t, LIGHT_ROWS)
    nj = dc // cb
    c0 = cfg["off_conv"] // cb
    n8 = t // SUBLANES

    def sect(s):
        col = lambda j: c0 + s * nj + j
        main = pl.BlockSpec((tt, cb), lambda j, i: (i, col(j)))
        prev = pl.BlockSpec((SUBLANES, cb), lambda j, i: (jnp.maximum(i * (tt // SUBLANES) - 1, 0), col(j)))
        nxt = pl.BlockSpec((SUBLANES, cb), lambda j, i: (jnp.minimum((i + 1) * (tt // SUBLANES), n8 - 1), col(j)))
        return main, prev, nxt

    blk = pl.BlockSpec((tt, cb), lambda j, i: (i, j))
    wspec = pl.BlockSpec((SUBLANES, cb), lambda j, i: (0, j))
    return tt, nj, n8, sect, blk, wspec


def _conv_fwd(name, p_all, cw8, cfg):
    t = p_all.shape[0]
    tt, nj, n8, sect, blk, wspec = _conv_specs(t, cfg)
    (bm, _, _), (cm, cp, _), (um, up, _) = sect(0), sect(1), sect(2)

    def body(b_ref, c_ref, cp_ref, u_ref, up_ref, w_ref, o_ref, zp_scr):
        first = pl.program_id(1) == 0
        z = c_ref[...] * u_ref[...]
        zp_scr[...] = cp_ref[...] * up_ref[...]
        o = _row(w_ref, 2) * z + _row(w_ref, 1) * _shift_down(z, zp_scr, 1, first) \
            + _row(w_ref, 0) * _shift_down(z, zp_scr, 2, first)
        o_ref[...] = (b_ref[...] * o).astype(BF16)

    return pl.pallas_call(
        body, name=name, out_shape=jax.ShapeDtypeStruct((t, cfg["dc"]), BF16), grid=(nj, t // tt),
        in_specs=[bm, cm, cp, um, up, wspec], out_specs=blk,
        scratch_shapes=[pltpu.VMEM((SUBLANES, blk.block_shape[1]), F32)],
        compiler_params=_cparams(("arbitrary", "arbitrary")),
    )(p_all, p_all, p_all, p_all, p_all, cw8)


def _conv_bwd(name, p_all, cw8, dyb, cfg):
    t = p_all.shape[0]
    tt, nj, n8, sect, blk, wspec = _conv_specs(t, cfg)
    (bm, _, bn), (cm, cp, _), (um, up, _) = sect(0), sect(1), sect(2)
    cb = blk.block_shape[1]
    dnxt = pl.BlockSpec((SUBLANES, cb), lambda j, i: (jnp.minimum((i + 1) * (tt // SUBLANES), n8 - 1), j))
    nt_ = t // tt

    def body(b_ref, bn_ref, c_ref, cp_ref, u_ref, up_ref, w_ref, d_ref, dn_ref,
             db_ref, dc_ref, du_ref, dw_ref, zp_scr, don_scr):
        i = pl.program_id(1)
        first, last = i == 0, i == nt_ - 1
        c, u, b, dy = c_ref[...], u_ref[...], b_ref[...], d_ref[...]
        z = c * u
        zp_scr[...] = cp_ref[...] * up_ref[...]
        z1 = _shift_down(z, zp_scr, 1, first)
        z2 = _shift_down(z, zp_scr, 2, first)
        w0, w1, w2 = _row(w_ref, 0), _row(w_ref, 1), _row(w_ref, 2)
        o = w2 * z + w1 * z1 + w0 * z2
        do = dy * b
        don_scr[...] = dn_ref[...] * bn_ref[...]
        dz = w2 * do + w1 * _shift_up(do, don_scr, 1, last) + w0 * _shift_up(do, don_scr, 2, last)
        db_ref[...] = (dy * o).astype(BF16)
        dc_ref[...] = (dz * u).astype(BF16)
        du_ref[...] = (dz * c).astype(BF16)
        rows = lax.broadcasted_iota(jnp.int32, (SUBLANES, cb), 0)
        s0 = jnp.sum(do * z2, axis=0, keepdims=True)
        s1 = jnp.sum(do * z1, axis=0, keepdims=True)
        s2 = jnp.sum(do * z, axis=0, keepdims=True)
        dw = jnp.where(rows == 0, s0, jnp.where(rows == 1, s1, jnp.where(rows == 2, s2, 0.0)))
        _acc_out(dw_ref, dw, first)

    dc = cfg["dc"]
    return pl.pallas_call(
        body, name=name,
        out_shape=[jax.ShapeDtypeStruct((t, dc), BF16)] * 3 + [jax.ShapeDtypeStruct((SUBLANES, dc), F32)],
        grid=(nj, nt_),
        in_specs=[bm, bn, cm, cp, um, up, wspec, blk, dnxt],
        out_specs=[blk] * 3 + [wspec],
        scratch_shapes=[pltpu.VMEM((SUBLANES, cb), F32), pltpu.VMEM((SUBLANES, cb), F32)],
        compiler_params=_cparams(("arbitrary", "arbitrary")),
    )(p_all, p_all, p_all, p_all, p_all, p_all, cw8, dyb, dyb)


def _merge_specs(t, cfg):
    d, cb = cfg["d"], cfg["cb"]
    tt = _tile(t, LIGHT_ROWS)
    nj = d // cb
    g0 = cfg["off_gate"] // cb
    ga = pl.BlockSpec((tt, cb), lambda j, i: (i, g0 + j))
    gb = pl.BlockSpec((tt, cb), lambda j, i: (i, g0 + nj + j))
    ba = pl.BlockSpec((1, cb), lambda j, i: (0, j))
    bb = pl.BlockSpec((1, cb), lambda j, i: (0, nj + j))
    blk = pl.BlockSpec((tt, cb), lambda j, i: (i, j))
    return tt, nj, ga, gb, ba, bb, blk


def _merge_fwd(name, p_all, bias, ya, yb, cfg):
    t = p_all.shape[0]
    tt, nj, ga, gb, ba, bb, blk = _merge_specs(t, cfg)

    def body(ga_ref, gb_ref, ba_ref, bb_ref, ya_ref, yb_ref, o_ref):
        o_ref[...] = _merge_fn(ga_ref[...], gb_ref[...], ba_ref[...], bb_ref[...],
                               ya_ref[...], yb_ref[...]).astype(BF16)

    return pl.pallas_call(
        body, name=name, out_shape=jax.ShapeDtypeStruct((t, cfg["d"]), BF16), grid=(nj, t // tt),
        in_specs=[ga, gb, ba, bb, blk, blk], out_specs=blk,
        compiler_params=_cparams(("arbitrary", "arbitrary")),
    )(p_all, p_all, bias, bias, ya, yb)


def _merge_bwd(name, p_all, bias, ya, yb, dm, cfg):
    t = p_all.shape[0]
    tt, nj, ga, gb, ba, bb, blk = _merge_specs(t, cfg)

    def body(ga_ref, gb_ref, ba_ref, bb_ref, ya_ref, yb_ref, dm_ref,
             dga_ref, dgb_ref, dya_ref, dyb_ref, dba_ref, dbb_ref):
        _, vjp = jax.vjp(_merge_fn, ga_ref[...], gb_ref[...], ba_ref[...], bb_ref[...], ya_ref[...], yb_ref[...])
        dga, dgb, dba, dbb, dya, dyb = vjp(dm_ref[...])
        for ref, val in ((dga_ref, dga), (dgb_ref, dgb), (dya_ref, dya), (dyb_ref, dyb)):
            ref[...] = val.astype(BF16)
        first = pl.program_id(1) == 0
        _acc_out(dba_ref, dba, first)
        _acc_out(dbb_ref, dbb, first)

    d = cfg["d"]
    par = pl.BlockSpec((1, blk.block_shape[1]), lambda j, i: (0, j))
    return pl.pallas_call(
        body, name=name,
        out_shape=[jax.ShapeDtypeStruct((t, d), BF16)] * 4 + [jax.ShapeDtypeStruct((1, d), F32)] * 2,
        grid=(nj, t // tt),
        in_specs=[ga, gb, ba, bb, blk, blk, blk], out_specs=[blk] * 4 + [par] * 2,
        compiler_params=_cparams(("arbitrary", "arbitrary")),
    )(p_all, p_all, bias, bias, ya, yb, dm)


PAIRS = 8


def _pair_stack(ref, pairs):
    return jnp.stack([ref[:, p * LANES:(p + 1) * LANES] for p in range(pairs)])


def _pair_store(ref, val):
    for p in range(val.shape[0]):
        ref[:, p * LANES:(p + 1) * LANES] = val[p]


def _rec_specs(t, cfg, rev):
    dr = cfg["dr"]
    nc = t // CHUNK
    hp = dr // LANES
    pairs = _tile(hp, (PAIRS, 2, 1))
    ng = hp // pairs
    w = LANES * pairs
    ch = (lambda c: nc - 1 - c) if rev else (lambda c: c)
    slab = pl.BlockSpec((CHUNK, w), lambda h, c: (ch(c), h))
    vspec = pl.BlockSpec((CHUNK, w), lambda h, c: (ch(c), 2 * ng + h))
    sspec = pl.BlockSpec((None, pairs, LANES, LANES), lambda h, c: (ch(c), h, 0, 0))
    first = lambda: jnp.logical_and(pl.program_id(0) == 0, pl.program_id(1) == 0)
    last = lambda: jnp.logical_and(pl.program_id(0) == ng - 1, pl.program_id(1) == nc - 1)
    return nc, hp, pairs, ng, slab, vspec, sspec, first, last


def _rec_fwd(name, pm, lw, k2, a, b, cfg, comm=None):
    t = pm.shape[0]
    nc, hp, pairs, ng, slab, vspec, sspec, first, last = _rec_specs(t, cfg, False)

    def body(r_ref, lw_ref, k_ref, v_ref, a_ref, b_ref, y_ref, s_ref, s_scr):
        @pl.when(pl.program_id(1) == 0)
        def _():
            s_scr[...] = jnp.zeros_like(s_scr)

        s = s_scr[...]
        s_ref[...] = s
        y, s_new = _chunk_fn(s, *[_pair_stack(ref, pairs) for ref in (r_ref, lw_ref, k_ref, v_ref, a_ref, b_ref)])
        _pair_store(y_ref, y)
        s_scr[...] = s_new

    return _hosted_call(
        body, name, comm, first, last, args=[pm, lw, k2, pm, a, b],
        in_specs=[slab, slab, slab, vspec, slab, slab],
        out_shape=[jax.ShapeDtypeStruct((t, cfg["dr"]), F32), jax.ShapeDtypeStruct((nc, hp, LANES, LANES), F32)],
        out_specs=[slab, sspec], scratch=[pltpu.VMEM((pairs, LANES, LANES), F32)], grid=(ng, nc),
        sem=("arbitrary", "arbitrary"))


def _rec_bwd(name, pm, lw, k2, a, b, s_chk, dy, cfg, comm=None):
    t = pm.shape[0]
    nc, hp, pairs, ng, slab, vspec, sspec, first, last = _rec_specs(t, cfg, True)

    def body(r_ref, lw_ref, k_ref, v_ref, a_ref, b_ref, s_ref, dy_ref,
             dr_ref, dlw_ref, dk_ref, dv_ref, da_ref, db_ref, ds_scr):
        @pl.when(pl.program_id(1) == 0)
        def _():
            ds_scr[...] = jnp.zeros_like(ds_scr)

        _, vjp = jax.vjp(_chunk_fn, s_ref[...],
                         *[_pair_stack(ref, pairs) for ref in (r_ref, lw_ref, k_ref, v_ref, a_ref, b_ref)])
        ds, dr, dlw, dk, dv, da, db = vjp((_pair_stack(dy_ref, pairs), ds_scr[...]))
        ds_scr[...] = ds
        for ref, val in ((dr_ref, dr), (dlw_ref, dlw), (dk_ref, dk), (dv_ref, dv), (da_ref, da), (db_ref, db)):
            _pair_store(ref, val)

    return _hosted_call(
        body, name, comm, first, last, args=[pm, lw, k2, pm, a, b, s_chk, dy],
        in_specs=[slab, slab, slab, vspec, slab, slab, sspec, slab],
        out_shape=[jax.ShapeDtypeStruct((t, cfg["dr"]), F32)] * 6, out_specs=[slab] * 6,
        scratch=[pltpu.VMEM((pairs, LANES, LANES), F32)], grid=(ng, nc), sem=("arbitrary", "arbitrary"))


def _comm_call(name, comm):
    n = comm.n
    hbm = pl.BlockSpec(memory_space=pl.ANY)

    def body(*refs):
        comm.start(refs[:n], refs[n:2 * n], refs[2 * n:])
        comm.wait(refs[:n], refs[n:2 * n], refs[2 * n:])

    return pl.pallas_call(body, name=name, out_shape=comm.out_shape, in_specs=[hbm] * n, out_specs=[hbm] * n,
                          scratch_shapes=comm.scratch)(*comm.arrs)


def _all_reduce_small(name, v):
    rows = v.shape[0]
    vm = pl.BlockSpec(memory_space=pltpu.VMEM)

    def body(x_ref, out_ref, buf, send_sems, recv_sems):
        x, y, c = _my_pos()
        me, sibling = (x, y, c), (x, y, 1 - c)
        chips = [(1 - x, y), (x, 1 - y), (1 - x, 1 - y)]

        def copy(k, block, to, src=None):
            px, py, pc = block
            dst = buf.at[4 * px + 2 * py + pc]
            return pltpu.make_async_remote_copy(
                src_ref=dst if src is None else src, dst_ref=dst,
                send_sem=send_sems.at[k], recv_sem=recv_sems.at[k], device_id=to, device_id_type=MESH)

        buf[4 * x + 2 * y + c] = x_ref[...]
        first = [copy(0, me, sibling, src=x_ref)]
        first += [copy(1 + j, me, (*chip, c), src=x_ref) for j, chip in enumerate(chips)]
        for cp in first:
            cp.start()
        passed = [copy(4 + j, (*chip, c), sibling) for j, chip in enumerate(chips)]
        for j, chip in enumerate(chips):
            copy(1 + j, (*chip, c), me).wait_recv()
            passed[j].start()
        copy(0, sibling, me).wait_recv()
        for j, chip in enumerate(chips):
            copy(4 + j, (*chip, 1 - c), me).wait_recv()
        for cp in first + passed:
            cp.wait_send()
        acc = buf[0]
        for d in range(1, N_DEV):
            acc = acc + buf[d]
        out_ref[...] = acc

    return pl.pallas_call(
        body, name=name, out_shape=jax.ShapeDtypeStruct(v.shape, F32),
        in_specs=[vm], out_specs=vm,
        scratch_shapes=[pltpu.VMEM((N_DEV, rows, LANES), F32), pltpu.SemaphoreType.DMA((7,)),
                        pltpu.SemaphoreType.DMA((7,))],
    )(v)


def _pair_sum(name, slabs, got, core):
    _, rows, cols = slabs.shape
    nq = got.shape[0]
    rb = _tile(rows, (256, 128, 64, 32, 16, 8))
    mine = pl.BlockSpec((None, rb, cols), lambda q, j, c_ref: (2 * q + c_ref[0], j, 0))
    blk = pl.BlockSpec((None, rb, cols), lambda q, j, c_ref: (q, j, 0))

    def body(c_ref, a_ref, b_ref, o_ref):
        o_ref[...] = (a_ref[...].astype(F32) + b_ref[...].astype(F32)).astype(o_ref.dtype)

    return pl.pallas_call(
        body, name=name, out_shape=jax.ShapeDtypeStruct(got.shape, got.dtype),
        grid_spec=pltpu.PrefetchScalarGridSpec(num_scalar_prefetch=1, grid=(nq, rows // rb),
                                               in_specs=[mine, blk], out_specs=blk),
        compiler_params=_cparams(("arbitrary", "arbitrary")))(core, slabs, got)


def _adamw(name, w, m, v, g_own, g_recv=None):
    rows, cols = w.shape
    nr = g_recv.shape[0] if g_recv is not None else 0
    per_el = 4 * 3 + g_own.dtype.itemsize + (nr * g_recv.dtype.itemsize if nr else 0) + 16
    rb = SUBLANES * 2
    while rb * 2 <= rows and rows % (rb * 2) == 0 and rb * 2 * cols * per_el * 2 <= VMEM_LIMIT // 2:
        rb *= 2
    if rows % rb:
        rb = rows
    blk = pl.BlockSpec((rb, cols), lambda i: (i, 0))
    rblk = pl.BlockSpec((max(nr, 1), rb, cols), lambda i: (0, i, 0))
    has_r = g_recv is not None
    bc1 = 1.0 - ADAM_B1 ** ADAM_STEP
    bc2 = 1.0 - ADAM_B2 ** ADAM_STEP

    def body(*refs):
        w_ref, m_ref, v_ref, go_ref = refs[:4]
        gr_ref = refs[4] if has_r else None
        g_out, d_out, m_out, v_out = refs[4 + has_r:]
        g = go_ref[...].astype(F32)
        if has_r:
            for r in range(nr):
                g = g + gr_ref[r].astype(F32)
        mn = ADAM_B1 * m_ref[...] + (1.0 - ADAM_B1) * g
        vn = ADAM_B2 * v_ref[...] + (1.0 - ADAM_B2) * (g * g)
        m_hat = mn / bc1
        v_hat = vn / bc2
        g_out[...] = g
        d_out[...] = -ADAM_LR * (m_hat / (jnp.sqrt(v_hat) + ADAM_EPS) + ADAM_WD * w_ref[...])
        m_out[...] = mn
        v_out[...] = vn

    return pl.pallas_call(
        body, name=name, out_shape=[jax.ShapeDtypeStruct((rows, cols), F32)] * 4, grid=(rows // rb,),
        in_specs=[blk] * 4 + ([rblk] if has_r else []), out_specs=[blk] * 4,
        compiler_params=_cparams(("arbitrary",)),
    )(*([w, m, v, g_own] + ([g_recv] if has_r else [])))


def _round_up(n, q):
    return (n + q - 1) // q * q


def _cols(a8):
    return jnp.transpose(a8, (1, 0, 2)).reshape(a8.shape[1], -1)


def _col_slabs(a):
    r_, c_ = a.shape
    return jnp.transpose(a.reshape(r_, N_DEV, c_ // N_DEV), (1, 0, 2))


def _padded_from_slabs(slabs, gap_at, gap, total):
    _, rows, c8 = slabs.shape
    zeros = lambda n: jnp.zeros((rows, n), slabs.dtype)
    pieces = []
    for dd in range(N_DEV):
        lo, hi = dd * c8, (dd + 1) * c8
        if gap and lo <= gap_at < hi:
            pieces += [slabs[dd][:, :gap_at - lo], zeros(gap), slabs[dd][:, gap_at - lo:]]
        else:
            pieces.append(slabs[dd])
    if total > N_DEV * c8 + gap:
        pieces.append(zeros(total - N_DEV * c8 - gap))
    return jnp.concatenate([p for p in pieces if p.shape[1]], axis=1)


def _slabs_from_padded(mat, gap_at, gap, c8):
    out = []
    for dd in range(N_DEV):
        lo, hi = dd * c8, (dd + 1) * c8
        if gap and lo < gap_at < hi:
            out.append(jnp.concatenate([mat[:, lo:gap_at], mat[:, gap_at + gap:hi + gap]], axis=1))
        else:
            start = lo + (gap if lo >= gap_at else 0)
            out.append(mat[:, start:start + c8])
    return jnp.stack(out)


_MID = ("w_out_a", "w_out_b", "w_out", "w_mlp_up", "w_mlp_down")


def _local_step(x, target, wts, shards, cfg):
    dr, dc, d, lp, cb = cfg["dr"], cfg["dc"], cfg["d"], cfg["lp"], cfg["cb"]
    dff = shards["w_mlp_down"].shape[0] * N_DEV
    wmix = 3 * dr + lp
    (xn,) = _norm_fwd("norm_mix_fwd", x, None, wts["norm_mix_w"], False)
    half = d // 2
    (p_top,), (bot8,) = _matmul("mm_in_top", xn[:, :half], wts["w_top"], "nn", [F32],
                                comm=_Comm("gather", [shards["w_in_bot"]]))
    w_bot = _padded_from_slabs(bot8, 3 * dr + cfg["lora"], lp - cfg["lora"], cfg["wall"])
    (p_all,), (g_oa, g_ob, g_o) = _matmul(
        "mm_in_bot", xn[:, half:], w_bot, "nn", [F32], epi=lambda r, top: (r + top,), extras=(p_top,),
        comm=_Comm("gather", [shards["w_out_a"], shards["w_out_b"], shards["w_out"]]))
    w_out_a, w_out_b, w_out = _cols(g_oa), _cols(g_ob), g_o.reshape(d, d)
    pm = _mix_fwd("mix_fwd", p_all, wts["mu_pad"], wmix, cb)
    prep_w = (wts["w0"], wts["a0"], wts["k_k"], wts["k_a"], wts["wd"], wts["wi"], wts["wg"])
    lw, k2, a_in, b_in, g = _prep_fwd("prep_fwd", pm, cfg, *prep_w)
    (y_raw, s_chk), (g_u,) = _rec_fwd(
        "rec_fwd", pm, lw, k2, a_in, b_in, cfg, comm=_Comm("gather", [shards["w_mlp_up"]]))
    w_up = _cols(g_u)
    post_w = (wts["lnx_w"], wts["lnx_b"], wts["r_k"])
    ya_in = _post_fwd("post_fwd", y_raw, pm, k2, g, *post_w, cfg)
    (ya,) = _matmul("mm_out_a", ya_in, w_out_a, "nn", [F32])
    yb_in = _conv_fwd("conv_fwd", p_all, wts["conv_w8"], cfg)
    (yb,) = _matmul("mm_out_b", yb_in, w_out_b, "nn", [F32])
    mg = _merge_fwd("merge_fwd", p_all, wts["gate_bias"], ya, yb, cfg)
    (mo,) = _matmul("mm_out", mg, w_out, "nn", [F32])
    h1, hn = _norm_fwd("norm_mlp_fwd", x, mo, wts["norm_mlp_w"], True)
    (u, act), (g_d,) = _matmul("mm_up", hn, w_up, "nn", [F32, BF16],
                               epi=lambda r: (r, jnp.square(jnp.maximum(r, 0.0))),
                               comm=_Comm("gather", [shards["w_mlp_down"]]))
    w_down = g_d.reshape(dff, d)
    (md,) = _matmul("mm_down", act, w_down, "nn", [F32])
    loss, dh2, dh2b, g_norm_final = _final("final", h1, md, target, wts["norm_final_w"])
    (du,) = _matmul("mm_down_dx", dh2b, w_down, "nt", [BF16],
                    epi=lambda r, uu: (r * (2.0 * jnp.maximum(uu, 0.0)),), extras=(u,))
    (g_down,) = _matmul("mm_down_dw", act, dh2b, "tn", [BF16])
    core = lax.axis_index("c").astype(jnp.int32).reshape(1)
    my_chip = 2 * lax.axis_index("x") + lax.axis_index("y")
    me = 2 * my_chip + lax.axis_index("c")
    own, recv = {}, {}

    def chip_own(chip_sum):
        return lax.dynamic_index_in_dim(chip_sum, my_chip, axis=0, keepdims=False)

    down_slabs = g_down.reshape(N_DEV, dff // N_DEV, d)
    (dhn,), (got,) = _matmul("mm_up_dx", du, w_up, "nt", [F32], comm=_Comm("pair", [down_slabs]))
    down_sum = _pair_sum("pair_sum_down", down_slabs, got, core)
    (g_up,) = _matmul("mm_up_dw", hn, du, "tn", [BF16], out_slabs=True)
    own["w_mlp_down"] = chip_own(down_sum)
    dh1, dh1b, g_norm_mlp = _norm_bwd("norm_mlp_bwd", h1, dhn, dh2, wts["norm_mlp_w"])
    (dmg,), (got,) = _matmul("mm_out_dx", dh1b, w_out, "nt", [F32], comm=_Comm("pair", [g_up]))
    up_sum = _pair_sum("pair_sum_up", g_up, got, core)
    own["w_mlp_up"] = chip_own(up_sum)
    (g_out,) = _matmul("mm_out_dw", mg, dh1b, "tn", [BF16])
    dpga, dpgb, dya, dyb, dba, dbb = _merge_bwd("merge_bwd", p_all, wts["gate_bias"], ya, yb, dmg, cfg)
    (dya_in,) = _matmul("mm_out_a_dx", dya, w_out_a, "nt", [F32])
    (g_out_a,) = _matmul("mm_out_a_dw", ya_in, dya, "tn", [BF16], out_slabs=True)
    (dyb_in,) = _matmul("mm_out_b_dx", dyb, w_out_b, "nt", [F32])
    (g_out_b,) = _matmul("mm_out_b_dw", yb_in, dyb, "tn", [BF16], out_slabs=True)
    dpb, dpc, dpu, g_conv8 = _conv_bwd("conv_bwd", p_all, wts["conv_w8"], dyb_in, cfg)
    dy_raw, dr_post, dk_post, dv_post, dg, g_lnw, g_lnb, g_rk = _post_bwd(
        "post_bwd", y_raw, pm, k2, g, *post_w, dya_in, cfg)
    (dr_rec, dlw, dk_rec, dv_rec, da_in, db_in), (recv["w_mlp_up"], recv["w_mlp_down"]) = _rec_bwd(
        "rec_bwd", pm, lw, k2, a_in, b_in, s_chk, dy_raw, cfg, comm=_Comm("chips", [up_sum, down_sum]))
    (dpm_r, dpm_k, dpm_v, dpl, g_w0, g_a0, g_kk, g_ka, g_wd, g_wi, g_wg) = _prep_bwd(
        "prep_bwd", pm, cfg, *prep_w, (dlw, dk_rec, dk_post, da_in, db_in, dg),
        (dr_rec, dr_post), (dv_rec, dv_post))
    mu = wts["mu_pad"]
    nb = dr // cb
    dps, dmus = [], []
    for s, dpm_s in enumerate((dpm_r, dpm_k, dpm_v)):
        dp_s, dmu_s = _mix_bwd("mix_bwd_%d" % s, [dpm_s], p_all, s * nb, mu[:, s * dr:(s + 1) * dr], cb)
        dps.append(dp_s)
        dmus.append(dmu_s)
    dp_l, dmu_l = _mix_bwd("mix_bwd_l", [dpl[j] for j in range(nb)], p_all, 3 * nb, mu[:, 3 * dr:], min(cb, lp))
    tail = [jnp.zeros((x.shape[0], cfg["wall"] - cfg["used"]), BF16)] if cfg["wall"] > cfg["used"] else []
    dp_all = jnp.concatenate(dps + [dp_l, dpb, dpc, dpu, dpga, dpgb] + tail, axis=1)
    ld, li, lora = cfg["ld"], cfg["li"], cfg["lora"]
    g_small = jnp.concatenate([g_wd[:ld], g_wi[ld:ld + li], g_wg[ld + li:lora], g_conv8[:3]], axis=0)
    g_small = jnp.pad(g_small, ((0, cfg["small_rows"] - g_small.shape[0]), (0, 0)))
    direct = dict(w_out_a=g_out_a, w_out_b=g_out_b, w_out=g_out.reshape(N_DEV, d // N_DEV, d),
                  small=_col_slabs(g_small))
    (g_all,), got4 = _matmul("mm_in_dw", xn, dp_all, "tn", [BF16],
                             comm=_Comm("exchange", list(direct.values())))
    for n, slabs, r in zip(direct, direct.values(), got4):
        own[n] = lax.dynamic_index_in_dim(slabs, me, axis=0, keepdims=False)
        recv[n] = r
    in_slabs = _slabs_from_padded(g_all, 3 * dr + lora, lp - lora,
                                  (cfg["used"] - lp + lora) // N_DEV)
    (got,) = _comm_call("pair_exchange", _Comm("pair", [in_slabs]))
    in_sum = _pair_sum("pair_sum_in", in_slabs, got, core)
    (dxn,), (recv["w_in"],) = _matmul("mm_in_dx", dp_all, (wts["w_top"], w_bot), "nt", [F32],
                                      comm=_Comm("chips", [in_sum]))
    own["w_in"] = chip_own(in_sum)
    grad_x, _, g_norm_mix = _norm_bwd("norm_mix_bwd", x, dxn, dh1, wts["norm_mix_w"])
    grads = dict(
        norm_mix_w=g_norm_mix, gate_bias=jnp.concatenate([dba, dbb], axis=1),
        mu_pad=jnp.concatenate(dmus + [dmu_l], axis=1), w0=g_w0, a0=g_a0, k_k=g_kk, k_a=g_ka,
        r_k=g_rk, lnx_w=g_lnw, lnx_b=g_lnb, norm_mlp_w=g_norm_mlp, norm_final_w=g_norm_final)
    return loss, grad_x, grads, own, recv


_SMALL = ("norm_mix_w", "gate_bias", "shift_mu", "w0", "a0", "k_k", "k_a", "r_k", "lnx_w", "lnx_b",
          "norm_mlp_w", "norm_final_w")
_ORDER = ("norm_mix_w", "w_in", "gate_bias", "shift_mu", "w0", "w_decay_up", "a0", "w_iclr_up", "w_gate_up",
          "k_k", "k_a", "r_k", "lnx_w", "lnx_b", "w_out_a", "conv_w", "w_out_b", "w_out", "norm_mlp_w",
          "w_mlp_up", "w_mlp_down", "norm_final_w")


def _step(x, target, w, m, v):
    t, d = x.shape[1], x.shape[2]
    dr = w["w0"].shape[-1]
    ld, li, lg = w["w_decay_up"].shape[1], w["w_iclr_up"].shape[1], w["w_gate_up"].shape[1]
    lora = ld + li + lg
    lp = _round_up(lora, LANES)
    dc = w["conv_w"].shape[-1] * N_DEV
    cb = math.gcd(math.gcd(lp, dr), 512)
    used = 3 * dr + lp + 3 * dc + 2 * d
    wall = _round_up(used, 1024 if used > MAX_FULL_K else LANES)
    small_rows = ld + li + lg + 3
    cfg = dict(d=d, dr=dr, dc=dc, lp=lp, cb=cb, off_conv=3 * dr + lp, off_gate=3 * dr + lp + 3 * dc, used=used,
               wall=wall, ld=ld, li=li, lora=lora, small_rows=_round_up(small_rows, SUBLANES))
    x2, tg2 = x[0], target[0]

    small_sh = jnp.concatenate([w["w_decay_up"][0], w["w_iclr_up"][0], w["w_gate_up"][0], w["conv_w"][0]], axis=0)
    small_sh = jnp.pad(small_sh, ((0, _round_up(small_rows, SUBLANES) - small_rows), (0, 0)))
    big = ("w_in",) + _MID
    w_in_b = w["w_in"][0].astype(BF16)
    top8, gsm = _comm_call("gather_weights", _Comm("gather", [w_in_b[:d // 2], small_sh]))
    shards = {n: w[n][0].astype(BF16) for n in _MID}
    shards["w_in_bot"] = w_in_b[d // 2:]
    w_top = _padded_from_slabs(top8, 3 * dr + lora, lp - lora, wall)
    sm = _cols(gsm)
    lora_full = sm[:lora]

    def lora_pad(lo, hi):
        rows = lax.broadcasted_iota(jnp.int32, (lp, 1), 0)
        full = jnp.pad(lora_full, ((0, lp - lora), (0, 0)))
        return jnp.where(jnp.logical_and(rows >= lo, rows < hi), full, 0.0)

    conv_w8 = jnp.pad(sm[lora:lora + 3], ((0, SUBLANES - 3), (0, 0)))
    mu_pad = jnp.pad(w["shift_mu"], ((0, 0), (0, lp - lora)))
    wts = dict(
        w_top=w_top, wd=lora_pad(0, ld), wi=lora_pad(ld, ld + li), wg=lora_pad(ld + li, lora), conv_w8=conv_w8,
        mu_pad=mu_pad, norm_mix_w=w["norm_mix_w"], gate_bias=w["gate_bias"], w0=w["w0"], a0=w["a0"],
        k_k=w["k_k"], k_a=w["k_a"], r_k=w["r_k"].reshape(1, dr), lnx_w=w["lnx_w"], lnx_b=w["lnx_b"],
        norm_mlp_w=w["norm_mlp_w"], norm_final_w=w["norm_final_w"].reshape(1, d))

    loss, grad_x, gr, own, received = _local_step(x2, tg2, wts, shards, cfg)

    small_g = dict(norm_mix_w=gr["norm_mix_w"], gate_bias=gr["gate_bias"], shift_mu=gr["mu_pad"][:, :3 * dr + lora],
                   w0=gr["w0"], a0=gr["a0"], k_k=gr["k_k"], k_a=gr["k_a"], r_k=gr["r_k"], lnx_w=gr["lnx_w"],
                   lnx_b=gr["lnx_b"], norm_mlp_w=gr["norm_mlp_w"], norm_final_w=gr["norm_final_w"])
    sizes = [small_g[n].size for n in _SMALL]
    total = sum(sizes) + 1
    prow = _round_up(total, LANES * SUBLANES) // LANES

    def pack(parts):
        flat = jnp.concatenate([p.reshape(-1) for p in parts])
        return jnp.pad(flat, (0, prow * LANES - flat.size)).reshape(prow, LANES)

    g_packed = _all_reduce_small("reduce_small", pack([small_g[n] for n in _SMALL] + [loss[0, :1]]))
    one = jnp.zeros((1,), F32)
    packed = [pack([d_[n] for n in _SMALL] + [one]) for d_ in (w, m, v)]
    sm_out = _adamw("adamw_small", *packed, g_packed)
    loss_out = g_packed.reshape(-1)[total - 1]

    def unpack(flat2d):
        flat = flat2d.reshape(-1)
        out, o = {}, 0
        for n, s in zip(_SMALL, sizes):
            out[n] = flat[o:o + s].reshape(w[n].shape)
            o += s
        return out

    res = [unpack(a) for a in sm_out]

    def shard2d(a):
        return a.reshape(-1, a.shape[-1])

    for n in big:
        outs = _adamw("adamw_" + n, shard2d(w[n]), shard2d(m[n]), shard2d(v[n]), shard2d(own[n]),
                      received[n].reshape(received[n].shape[:1] + shard2d(own[n]).shape))
        for r_, o in zip(res, outs):
            r_[n] = o.reshape(w[n].shape)
    sm_names = ("w_decay_up", "w_iclr_up", "w_gate_up", "conv_w")
    stack = lambda d_: jnp.pad(jnp.concatenate([d_[n][0] for n in sm_names], axis=0),
                               ((0, _round_up(small_rows, SUBLANES) - small_rows), (0, 0)))
    outs = _adamw("adamw_stack", stack(w), stack(m), stack(v), own["small"], received["small"])
    bounds = (0, ld, ld + li, lora, lora + 3)
    for r_, o in zip(res, outs):
        for q, n in enumerate(sm_names):
            r_[n] = o[bounds[q]:bounds[q + 1]].reshape(w[n].shape)

    grad, delta, new_m, new_v = res
    return (loss_out, grad_x[None], *[grad[n] for n in _ORDER], *[delta[n] for n in _ORDER],
            *[new_m[n] for n in _ORDER], *[new_v[n] for n in _ORDER])


def kernel(x, norm_mix_w, w_in, gate_bias, shift_mu, w0, w_decay_up, a0, w_iclr_up, w_gate_up, k_k, k_a, r_k, lnx_w, lnx_b, w_out_a, conv_w, w_out_b, w_out, norm_mlp_w, w_mlp_up, w_mlp_down, norm_final_w, loss_target, m_norm_mix_w, m_w_in, m_gate_bias, m_shift_mu, m_w0, m_w_decay_up, m_a0, m_w_iclr_up, m_w_gate_up, m_k_k, m_k_a, m_r_k, m_lnx_w, m_lnx_b, m_w_out_a, m_conv_w, m_w_out_b, m_w_out, m_norm_mlp_w, m_w_mlp_up, m_w_mlp_down, m_norm_final_w, v_norm_mix_w, v_w_in, v_gate_bias, v_shift_mu, v_w0, v_w_decay_up, v_a0, v_w_iclr_up, v_w_gate_up, v_k_k, v_k_a, v_r_k, v_lnx_w, v_lnx_b, v_w_out_a, v_conv_w, v_w_out_b, v_w_out, v_norm_mlp_w, v_w_mlp_up, v_w_mlp_down, v_norm_final_w):
    w = dict(zip(_ORDER, (norm_mix_w, w_in, gate_bias, shift_mu, w0, w_decay_up, a0, w_iclr_up, w_gate_up, k_k, k_a,
                          r_k, lnx_w, lnx_b, w_out_a, conv_w, w_out_b, w_out, norm_mlp_w, w_mlp_up, w_mlp_down,
                          norm_final_w)))
    m = dict(zip(_ORDER, (m_norm_mix_w, m_w_in, m_gate_bias, m_shift_mu, m_w0, m_w_decay_up, m_a0, m_w_iclr_up,
                          m_w_gate_up, m_k_k, m_k_a, m_r_k, m_lnx_w, m_lnx_b, m_w_out_a, m_conv_w, m_w_out_b,
                          m_w_out, m_norm_mlp_w, m_w_mlp_up, m_w_mlp_down, m_norm_final_w)))
    v = dict(zip(_ORDER, (v_norm_mix_w, v_w_in, v_gate_bias, v_shift_mu, v_w0, v_w_decay_up, v_a0, v_w_iclr_up,
                          v_w_gate_up, v_k_k, v_k_a, v_r_k, v_lnx_w, v_lnx_b, v_w_out_a, v_conv_w, v_w_out_b,
                          v_w_out, v_norm_mlp_w, v_w_mlp_up, v_w_mlp_down, v_norm_final_w)))
    return _step(x, loss_target, w, m, v)
```

```python
import math

import jax
import jax.numpy as jnp
from jax import lax
from jax.experimental import pallas as pl
from jax.experimental.pallas import tpu as pltpu

F32 = jnp.float32
BF16 = jnp.bfloat16
MESH = pl.DeviceIdType.MESH

N_DEV = 8
HEAD = 64
LANES = 128
SUBLANES = 8
CHUNK = 64
RMS_EPS = 1e-5
LNX_EPS = 64e-5
L2_EPS = 1e-12
ADAM_LR = 0.001
ADAM_B1 = 0.9
ADAM_B2 = 0.999
ADAM_EPS = 1e-08
ADAM_WD = 0.01
ADAM_STEP = 10
VMEM_LIMIT = 48 * 1024 * 1024
MAX_FULL_K = 4096
LIGHT_ROWS = (512, 256, 128, 64, 32, 16, 8)


def _cparams(sem):
    return pltpu.CompilerParams(dimension_semantics=sem, vmem_limit_bytes=VMEM_LIMIT)


def _tile(dim, cands):
    for c in cands:
        if c <= dim and dim % c == 0:
            return c
    return dim


def _my_pos():
    return lax.axis_index("x"), lax.axis_index("y"), lax.axis_index("c")


def _peer(pos, r):
    x, y, c = pos
    return (1 - x if r & 4 else x, 1 - y if r & 2 else y, 1 - c if r & 1 else c)


def _slot(pos):
    return 4 * pos[0] + 2 * pos[1] + pos[2]


class _Comm:
    def __init__(self, kind, arrs):
        self.kind, self.arrs, self.n = kind, list(arrs), len(arrs)
        if kind == "gather":
            self.out_shape = [jax.ShapeDtypeStruct((N_DEV,) + a.shape, a.dtype) for a in arrs]
        elif kind == "exchange":
            self.out_shape = [jax.ShapeDtypeStruct((N_DEV - 1,) + a.shape[1:], a.dtype) for a in arrs]
        elif kind == "pair":
            self.out_shape = [jax.ShapeDtypeStruct((N_DEV // 2,) + a.shape[1:], a.dtype) for a in arrs]
        else:
            self.out_shape = [jax.ShapeDtypeStruct((3,) + a.shape[1:], a.dtype) for a in arrs]
        self.scratch = [pltpu.SemaphoreType.DMA((7 * self.n,)), pltpu.SemaphoreType.DMA((7 * self.n,))]
        if kind == "gather":
            self.scratch.append(pltpu.SemaphoreType.DMA((self.n,)))

    def _exchange_copies(self, in_refs, out_refs, sems):
        me = _my_pos()
        x, y, c = me
        cps = []
        for ai in range(self.n):
            if self.kind == "exchange":
                todo = [(in_refs[ai].at[_slot(_peer(me, r))], out_refs[ai].at[r - 1], _peer(me, r), r - 1)
                        for r in range(1, N_DEV)]
            elif self.kind == "pair":
                todo = [(in_refs[ai].at[2 * q + 1 - c], out_refs[ai].at[q], (x, y, 1 - c), q)
                        for q in range(N_DEV // 2)]
            else:
                chips = [(1 - x, y), (x, 1 - y), (1 - x, 1 - y)]
                todo = [(in_refs[ai].at[2 * cx + cy], out_refs[ai].at[j], (cx, cy, c), j)
                        for j, (cx, cy) in enumerate(chips)]
            for src, dst, to, k in todo:
                cps.append(pltpu.make_async_remote_copy(
                    src_ref=src, dst_ref=dst, send_sem=sems[0].at[ai * 7 + k], recv_sem=sems[1].at[ai * 7 + k],
                    device_id=to, device_id_type=MESH))
        return cps

    def _gather_parts(self, in_refs, out_refs, sems):
        x, y, c = _my_pos()
        me, sibling = (x, y, c), (x, y, 1 - c)
        chips = [(1 - x, y), (x, 1 - y), (1 - x, 1 - y)]

        def copy(ai, k, block, to, src=None):
            dst = out_refs[ai].at[_slot(block)]
            return pltpu.make_async_remote_copy(
                src_ref=dst if src is None else src, dst_ref=dst, send_sem=sems[0].at[ai * 7 + k],
                recv_sem=sems[1].at[ai * 7 + k], device_id=to, device_id_type=MESH)

        mine = [pltpu.make_async_copy(in_refs[ai], out_refs[ai].at[_slot(me)], sems[2].at[ai])
                for ai in range(self.n)]
        first = []
        for ai in range(self.n):
            first.append(copy(ai, 0, me, sibling, src=in_refs[ai]))
            first += [copy(ai, 1 + j, me, (*chip, c), src=in_refs[ai]) for j, chip in enumerate(chips)]
        return me, sibling, chips, c, copy, mine, first

    def start(self, in_refs, out_refs, sems):
        if self.kind != "gather":
            for cp in self._exchange_copies(in_refs, out_refs, sems):
                cp.start()
            return
        _, _, _, _, _, mine, first = self._gather_parts(in_refs, out_refs, sems)
        for cp in mine + first:
            cp.start()

    def wait(self, in_refs, out_refs, sems):
        if self.kind != "gather":
            for cp in self._exchange_copies(in_refs, out_refs, sems):
                cp.wait()
            return
        me, sibling, chips, c, copy, mine, first = self._gather_parts(in_refs, out_refs, sems)
        passed = []
        for ai in range(self.n):
            for j, chip in enumerate(chips):
                copy(ai, 1 + j, (*chip, c), me).wait_recv()
                fwd = copy(ai, 4 + j, (*chip, c), sibling)
                fwd.start()
                passed.append(fwd)
        for ai in range(self.n):
            copy(ai, 0, sibling, me).wait_recv()
            for j, chip in enumerate(chips):
                copy(ai, 4 + j, (*chip, 1 - c), me).wait_recv()
        for cp in first + passed:
            cp.wait_send()
        for cp in mine:
            cp.wait()


def _hosted_call(body, name, comm, first, last, *, args, in_specs, out_shape, out_specs, scratch, grid, sem):
    if comm is None:
        return pl.pallas_call(body, name=name, out_shape=out_shape, grid=grid, in_specs=in_specs, out_specs=out_specs,
                              scratch_shapes=scratch, compiler_params=_cparams(sem))(*args)
    ni, no, ns, nc = len(args), len(out_shape), len(scratch), comm.n
    hbm = pl.BlockSpec(memory_space=pl.ANY)

    def hosted(*refs):
        ins, cin = refs[:ni], refs[ni:ni + nc]
        outs, cout = refs[ni + nc:ni + nc + no], refs[ni + nc + no:ni + 2 * nc + no]
        scr, sems = refs[ni + 2 * nc + no:ni + 2 * nc + no + ns], refs[ni + 2 * nc + no + ns:]

        @pl.when(first())
        def _():
            comm.start(cin, cout, sems)

        body(*ins, *outs, *scr)

        @pl.when(last())
        def _():
            comm.wait(cin, cout, sems)

    res = pl.pallas_call(
        hosted, name=name, out_shape=list(out_shape) + comm.out_shape, grid=grid,
        in_specs=list(in_specs) + [hbm] * nc, out_specs=list(out_specs) + [hbm] * nc,
        scratch_shapes=list(scratch) + comm.scratch,
        compiler_params=_cparams(("arbitrary",) * len(grid)))(*args, *comm.arrs)
    return res[:no], res[no:]


_DIMS = {"nn": ((1,), (0,)), "nt": ((1,), (1,)), "tn": ((0,), (0,))}


def _matmul(name, a, b, mode, out_dtypes, epi=None, extras=(), comm=None, out_slabs=False):
    bs = list(b) if isinstance(b, (tuple, list)) else [b]
    if mode == "nn":
        (m, k), n = a.shape, b.shape[1]
    elif mode == "nt":
        (m, k), n = a.shape, sum(x.shape[0] for x in bs)
    else:
        (k, m), n = a.shape, b.shape[1]
    tm = _tile(m, (1024, 512, 256, 128, 64, 32, 16, 8))
    if k <= MAX_FULL_K:
        tk, tn = k, _tile(n // N_DEV if out_slabs else n // len(bs), (512, 256, 128))
    else:
        tk, tn = _tile(k, (2048, 1024, 512, 256, 128)), _tile(n // len(bs), (1024, 512, 256, 128))
    nk = k // tk
    gm, gn = m // tm, n // tn
    a_spec = pl.BlockSpec((tk, tm), lambda i, j, q: (q, i)) if mode == "tn" else pl.BlockSpec((tm, tk), lambda i, j, q: (i, q))
    b_spec = pl.BlockSpec((tn, tk), lambda i, j, q: (j, q)) if mode == "nt" else pl.BlockSpec((tk, tn), lambda i, j, q: (q, j))
    gh = gn // 2
    b_specs = [b_spec] if len(bs) == 1 else [
        pl.BlockSpec((tn, tk), lambda i, j, q: (jnp.minimum(j, gh - 1), q)),
        pl.BlockSpec((tn, tk), lambda i, j, q: (jnp.maximum(j - gh, 0), q))]
    mn_spec = pl.BlockSpec((tm, tn), lambda i, j, q: (i, j))
    per = n // N_DEV // tn if out_slabs else 0
    out_spec = pl.BlockSpec((None, tm, tn), lambda i, j, q: (j // per, i, j % per)) if out_slabs else mn_spec
    ne, no = len(extras), len(out_dtypes)
    dims = (_DIMS[mode], ((), ()))
    keep_t = mode == "tn" and nk == 1 and gn > 1

    def finish(r, extra_refs, out_refs):
        outs = (r,) if epi is None else epi(r, *[e[...] for e in extra_refs])
        for o_ref, o in zip(out_refs, outs):
            o_ref[...] = o.astype(o_ref.dtype)

    def body(a_ref, *rest):
        if len(bs) == 1:
            return step(a_ref, *rest)

        @pl.when(pl.program_id(1) < gh)
        def _():
            step(a_ref, rest[0], *rest[2:])

        @pl.when(pl.program_id(1) >= gh)
        def _():
            step(a_ref, rest[1], *rest[2:])

    def step(a_ref, b_ref, *rest):
        extra_refs, out_refs = rest[:ne], rest[ne:ne + no]
        if keep_t:
            at = rest[ne + no]

            @pl.when(pl.program_id(1) == 0)
            def _():
                at[...] = a_ref[...].T

            part = jnp.dot(at[...], b_ref[...], preferred_element_type=F32)
        else:
            part = lax.dot_general(a_ref[...], b_ref[...], dims, preferred_element_type=F32)
        if nk == 1:
            finish(part, extra_refs, out_refs)
            return
        acc = rest[ne + no]
        q = pl.program_id(2)

        @pl.when(q == 0)
        def _():
            acc[...] = part

        @pl.when(jnp.logical_and(q > 0, q < nk - 1))
        def _():
            acc[...] += part

        @pl.when(q == nk - 1)
        def _():
            finish(acc[...] + part, extra_refs, out_refs)

    def first():
        return jnp.logical_and(jnp.logical_and(pl.program_id(0) == 0, pl.program_id(1) == 0), pl.program_id(2) == 0)

    def last():
        return jnp.logical_and(jnp.logical_and(pl.program_id(0) == gm - 1, pl.program_id(1) == gn - 1),
                               pl.program_id(2) == nk - 1)

    return _hosted_call(
        body, name, comm, first, last,
        args=[a, *bs, *extras], in_specs=[a_spec] + b_specs + [mn_spec] * ne,
        out_shape=[jax.ShapeDtypeStruct((N_DEV, m, n // N_DEV) if out_slabs else (m, n), dt) for dt in out_dtypes],
        out_specs=[out_spec] * no,
        scratch=[pltpu.VMEM((tm, tn), F32)] if nk > 1 else ([pltpu.VMEM((tm, tk), a.dtype)] if keep_t else []),
        grid=(gm, gn, nk), sem=("parallel", "arbitrary" if keep_t else "parallel", "arbitrary"))


@jax.custom_vjp
def _mm(a, w):
    return jnp.dot(a.astype(BF16), w.astype(BF16), preferred_element_type=F32)


def _mm_fwd(a, w):
    return _mm(a, w), (a, w)


def _mm_bwd(res, ct):
    a, w = res
    ctb = ct.astype(BF16)
    da = lax.dot_general(ctb, w.astype(BF16), (((1,), (1,)), ((), ())), preferred_element_type=F32)
    dw = lax.dot_general(a.astype(BF16), ctb, (((0,), (0,)), ((), ())), preferred_element_type=F32)
    return da, dw


_mm.defvjp(_mm_fwd, _mm_bwd)


def _split3(x):
    hi = x.astype(BF16)
    r1 = x - hi.astype(F32)
    mid = r1.astype(BF16)
    lo = (r1 - mid.astype(F32)).astype(BF16)
    return hi, mid, lo


def _head_ones(width):
    r = lax.broadcasted_iota(jnp.int32, (width, width), 0) // HEAD
    c = lax.broadcasted_iota(jnp.int32, (width, width), 1) // HEAD
    return (r == c).astype(BF16)


@jax.custom_vjp
def _segsum(x):
    ones = _head_ones(x.shape[-1])
    out = None
    for piece in _split3(x):
        t = jnp.dot(piece, ones, preferred_element_type=F32)
        out = t if out is None else out + t
    return out


_segsum.defvjp(lambda x: (_segsum(x), None), lambda _, ct: (_segsum(ct),))


def _softplus(z):
    return jnp.maximum(z, 0.0) + jnp.log(1.0 + jnp.exp(-jnp.abs(z)))


def _sigmoid(z):
    return 1.0 / (1.0 + jnp.exp(-z))


def _rms(x, w):
    ms = jnp.mean(x * x, axis=-1, keepdims=True)
    return x * lax.rsqrt(ms + RMS_EPS) * w


def _row(ref, i):
    return ref[pl.ds(i, 1), :]


def _shift_down(x, prev_ref, n, first):
    rolled = pltpu.roll(x, n, 0)
    rows = lax.broadcasted_iota(jnp.int32, x.shape, 0)
    for q in range(n):
        halo = jnp.where(first, 0.0, _row(prev_ref, SUBLANES - n + q))
        rolled = jnp.where(rows == q, halo, rolled)
    return rolled


def _shift_up(x, next_ref, n, last):
    t = x.shape[0]
    rolled = pltpu.roll(x, t - n, 0)
    rows = lax.broadcasted_iota(jnp.int32, x.shape, 0)
    for q in range(n):
        halo = jnp.where(last, 0.0, _row(next_ref, q))
        rolled = jnp.where(rows == t - n + q, halo, rolled)
    return rolled


def _acc_out(ref, val, first):
    @pl.when(first)
    def _():
        ref[...] = val

    @pl.when(jnp.logical_not(first))
    def _():
        ref[...] += val


def _prep_fn(k, plm, w0, a0, kkw, kaw, wd, wi, wg):
    w_log = -_softplus(-(w0 + _mm(jnp.tanh(plm), wd))) - 0.5
    lw = -jnp.exp(w_log)
    a_g = _sigmoid(a0 + _mm(plm, wi))
    g = _mm(_sigmoid(plm), wg)
    kk = k * kkw
    kk = kk / jnp.maximum(jnp.sqrt(_segsum(kk * kk)), L2_EPS)
    k2 = k * (1.0 + (a_g - 1.0) * kaw)
    return lw, k2, -kk, kk * a_g, g


def _post_fn(y, r, k2, v, g, lnw, lnb, rk):
    mu = _segsum(y) * (1.0 / HEAD)
    yc = y - mu
    var = _segsum(yc * yc) * (1.0 / HEAD)
    yn = yc * lax.rsqrt(var + LNX_EPS) * lnw + lnb
    bonus = _segsum(r * k2 * rk) * v
    return (yn + bonus) * g


def _merge_fn(pga, pgb, ba, bb, ya, yb):
    return _sigmoid(pga + ba) * ya + _sigmoid(pgb + bb) * yb


_NN, _NT, _TN = ((2,), (1,)), ((2,), (2,)), ((1,), (1,))


def _dot3(a, b, dims):
    ah = a.astype(BF16)
    al = (a - ah.astype(F32)).astype(BF16)
    bh = b.astype(BF16)
    bl = (b - bh.astype(F32)).astype(BF16)
    dg = lambda p, q: lax.dot_general(p, q, (dims, ((0,), (0,))), preferred_element_type=F32)
    (ca,), (cb_,) = dims
    if a.shape[ca] % LANES:
        return dg(ah, bh) + (dg(ah, bl) + dg(al, bh))
    cross = dg(jnp.concatenate([ah, al], axis=ca), jnp.concatenate([bl, bh], axis=cb_))
    return dg(ah, bh) + cross


@jax.custom_vjp
def _dnn(a, b):
    return _dot3(a, b, _NN)


@jax.custom_vjp
def _dnt(a, b):
    return _dot3(a, b, _NT)


@jax.custom_vjp
def _dtn(a, b):
    return _dot3(a, b, _TN)


_dnn.defvjp(lambda a, b: (_dnn(a, b), (a, b)), lambda res, ct: (_dnt(ct, res[1]), _dtn(res[0], ct)))
_dnt.defvjp(lambda a, b: (_dnt(a, b), (a, b)), lambda res, ct: (_dnn(ct, res[1]), _dtn(ct, res[0])))
_dtn.defvjp(lambda a, b: (_dtn(a, b), (a, b)), lambda res, ct: (_dnt(res[1], ct), _dnn(res[0], ct)))


@jax.custom_vjp
def _unit_lower_inverse(low):
    n = low.shape[-1]
    ri = lax.broadcasted_iota(jnp.int32, low.shape, 1)
    ci = lax.broadcasted_iota(jnp.int32, low.shape, 2)
    inv = (ri == ci).astype(F32) + low
    pw = low
    for _ in range(int(math.log2(n // 2)) - 1):
        pw = _dnn(pw, pw)
        inv = inv + _dnn(inv, pw)
    return inv


def _unit_lower_inverse_bwd(inv, ct):
    return (_dnt(_dtn(inv, ct), inv),)


_unit_lower_inverse.defvjp(lambda low: (_unit_lower_inverse(low),) * 2, _unit_lower_inverse_bwd)


def _chunk_fn(s, r, lw, k, v, a, b):
    np_, c = r.shape[0], r.shape[1]
    c2 = 2 * c
    ri = lax.broadcasted_iota(jnp.int32, (np_, c, c), 1)
    ci = lax.broadcasted_iota(jnp.int32, (np_, c, c), 2)
    tri = (ri >= ci).astype(F32)
    cum = _dnn(tri, lw)
    tot = jnp.sum(lw, axis=1, keepdims=True)
    g_in, g_inv, g_out = jnp.exp(cum), jnp.exp(-cum), jnp.exp(tot - cum)
    lane_head = lax.broadcasted_iota(jnp.int32, (1, 2, 1, LANES), 3) // HEAD
    which = lax.broadcasted_iota(jnp.int32, (1, 2, 1, LANES), 1)
    hmask = (lane_head == which).astype(F32)

    def st(x):
        return (x[:, None] * hmask).reshape(np_, c2, LANES)

    r2, a2 = st(r * g_in), st(a * jnp.exp(cum - lw))
    b2, k2, v2 = st(b * g_inv), st(k * g_inv), st(v)
    bo2, ko2 = st(b * g_out), st(k * g_out)
    r2i = lax.broadcasted_iota(jnp.int32, (np_, c2, c2), 1)
    c2i = lax.broadcasted_iota(jnp.int32, (np_, c2, c2), 2)
    same = (r2i >= c) == (c2i >= c)
    strict = jnp.logical_and(same, r2i > c2i)
    incl = jnp.logical_and(same, r2i >= c2i)
    lab = jnp.where(strict, _dnt(a2, b2), 0.0)
    lak = jnp.where(strict, _dnt(a2, k2), 0.0)
    mrb = jnp.where(incl, _dnt(r2, b2), 0.0)
    mrk = jnp.where(incl, _dnt(r2, k2), 0.0)
    x2 = _dnt(a2, s) + _dnn(lak, v2)
    u2 = _dnn(_unit_lower_inverse(lab), x2)
    y2 = _dnt(r2, s) + _dnn(mrb, u2) + _dnn(mrk, v2)
    y = jnp.sum(y2.reshape(np_, 2, c, LANES), axis=1)
    s_new = s * jnp.exp(tot) + _dtn(u2, bo2) + _dtn(v2, ko2)
    return y, s_new


def _norm_fwd(name, x, add, w, want_sum):
    t, d = x.shape
    tt = _tile(t, (128, 64, 32, 16, 8))
    row = pl.BlockSpec((tt, d), lambda i: (i, 0))
    par = pl.BlockSpec((1, d), lambda i: (0, 0))
    has_add = add is not None

    def body(*refs):
        x_ref = refs[0]
        add_ref = refs[1] if has_add else None
        w_ref = refs[1 + has_add]
        outs = refs[2 + has_add:]
        h = x_ref[...] + add_ref[...] if has_add else x_ref[...]
        if want_sum:
            outs[0][...] = h
        outs[-1][...] = _rms(h, w_ref[...]).astype(BF16)

    out_shape = ([jax.ShapeDtypeStruct((t, d), F32)] if want_sum else []) + [jax.ShapeDtypeStruct((t, d), BF16)]
    return pl.pallas_call(
        body, name=name, out_shape=out_shape, grid=(t // tt,),
        in_specs=[row] + ([row] if has_add else []) + [par],
        out_specs=[row] * len(out_shape),
        compiler_params=_cparams(("arbitrary",)),
    )(*([x] + ([add] if has_add else []) + [w]))


def _norm_bwd(name, xin, dy, dres, w):
    t, d = xin.shape
    tt = _tile(t, (128, 64, 32, 16, 8))
    row = pl.BlockSpec((tt, d), lambda i: (i, 0))
    par = pl.BlockSpec((1, d), lambda i: (0, 0))

    def body(x_ref, dy_ref, dres_ref, w_ref, dx_ref, dxb_ref, dw_ref):
        _, vjp = jax.vjp(_rms, x_ref[...], w_ref[...])
        dx, dw = vjp(dy_ref[...])
        dx = dx + dres_ref[...]
        dx_ref[...] = dx
        dxb_ref[...] = dx.astype(BF16)
        _acc_out(dw_ref, dw, pl.program_id(0) == 0)

    return pl.pallas_call(
        body, name=name,
        out_shape=[jax.ShapeDtypeStruct((t, d), F32), jax.ShapeDtypeStruct((t, d), BF16),
                   jax.ShapeDtypeStruct((1, d), F32)],
        grid=(t // tt,), in_specs=[row, row, row, par], out_specs=[row, row, par],
        compiler_params=_cparams(("arbitrary",)),
    )(xin, dy, dres, w)


def _final(name, h1, md, target, w):
    t, d = h1.shape
    tt = _tile(t, (128, 64, 32, 16, 8))
    row = pl.BlockSpec((tt, d), lambda i: (i, 0))
    par = pl.BlockSpec((1, d), lambda i: (0, 0))
    one = pl.BlockSpec((1, LANES), lambda i: (0, 0))

    def body(h1_ref, md_ref, tg_ref, w_ref, loss_ref, dh_ref, dhb_ref, dw_ref):
        tg = tg_ref[...]

        def f(h, wv):
            err = _rms(h, wv) - tg
            return 0.5 * jnp.sum(jnp.mean(err * err, axis=-1, keepdims=True), axis=0, keepdims=True)

        loss, vjp = jax.vjp(f, h1_ref[...] + md_ref[...], w_ref[...])
        dh, dw = vjp(jnp.ones((1, 1), F32))
        dh_ref[...] = dh
        dhb_ref[...] = dh.astype(BF16)
        first = pl.program_id(0) == 0
        _acc_out(dw_ref, dw, first)
        _acc_out(loss_ref, jnp.broadcast_to(loss, (1, LANES)), first)

    return pl.pallas_call(
        body, name=name,
        out_shape=[jax.ShapeDtypeStruct((1, LANES), F32), jax.ShapeDtypeStruct((t, d), F32),
                   jax.ShapeDtypeStruct((t, d), BF16), jax.ShapeDtypeStruct((1, d), F32)],
        grid=(t // tt,), in_specs=[row, row, row, par], out_specs=[one, row, row, par],
        compiler_params=_cparams(("arbitrary",)),
    )(h1, md, target, w)


def _halo_specs(tt, cb, nrow8, col_of):
    prev = pl.BlockSpec((SUBLANES, cb), lambda i, j: (jnp.maximum(i * (tt // SUBLANES) - 1, 0), col_of(j)))
    nxt = pl.BlockSpec((SUBLANES, cb), lambda i, j: (jnp.minimum((i + 1) * (tt // SUBLANES), nrow8 - 1), col_of(j)))
    return prev, nxt


def _mix_fwd(name, p_all, mu, width, cb):
    t = p_all.shape[0]
    tt = _tile(t, LIGHT_ROWS)
    main = pl.BlockSpec((tt, cb), lambda i, j: (i, j))
    prev, _ = _halo_specs(tt, cb, t // SUBLANES, lambda j: j)
    par = pl.BlockSpec((1, cb), lambda i, j: (0, j))

    def body(p_ref, prev_ref, mu_ref, o_ref):
        p = p_ref[...]
        o_ref[...] = p + (_shift_down(p, prev_ref, 1, pl.program_id(0) == 0) - p) * mu_ref[...]

    return pl.pallas_call(
        body, name=name, out_shape=jax.ShapeDtypeStruct((t, width), F32),
        grid=(t // tt, width // cb), in_specs=[main, prev, par], out_specs=main,
        compiler_params=_cparams(("arbitrary", "arbitrary")),
    )(p_all, p_all, mu)


def _mix_bwd(name, dpm_list, p_all, col0, mu, cb):
    t, width = dpm_list[0].shape
    tt = _tile(t, LIGHT_ROWS)
    n8 = t // SUBLANES
    nl = len(dpm_list)
    main = pl.BlockSpec((tt, cb), lambda j, i: (i, j))
    nxt = pl.BlockSpec((SUBLANES, cb), lambda j, i: (jnp.minimum((i + 1) * (tt // SUBLANES), n8 - 1), j))
    p_main = pl.BlockSpec((tt, cb), lambda j, i: (i, col0 + j))
    p_prev = pl.BlockSpec((SUBLANES, cb), lambda j, i: (jnp.maximum(i * (tt // SUBLANES) - 1, 0), col0 + j))
    par = pl.BlockSpec((1, cb), lambda j, i: (0, j))
    nt_ = t // tt

    def body(*refs):
        d_refs, dn_refs = refs[:nl], refs[nl:2 * nl]
        p_ref, pp_ref, mu_ref, dp_ref, dmu_ref, nx_scr = refs[2 * nl:]
        i = pl.program_id(1)
        dpm = d_refs[0][...]
        nx = dn_refs[0][...]
        for q in range(1, nl):
            dpm = dpm + d_refs[q][...]
            nx = nx + dn_refs[q][...]
        nx_scr[...] = nx
        mu_v = mu_ref[...]
        up = _shift_up(dpm, nx_scr, 1, i == nt_ - 1)
        dp_ref[...] = (dpm * (1.0 - mu_v) + up * mu_v).astype(BF16)
        p = p_ref[...]
        diff = _shift_down(p, pp_ref, 1, i == 0) - p
        _acc_out(dmu_ref, jnp.sum(dpm * diff, axis=0, keepdims=True), i == 0)

    return pl.pallas_call(
        body, name=name,
        out_shape=[jax.ShapeDtypeStruct((t, width), BF16), jax.ShapeDtypeStruct((1, width), F32)],
        grid=(width // cb, nt_),
        in_specs=[main] * nl + [nxt] * nl + [p_main, p_prev, par],
        out_specs=[main, par],
        scratch_shapes=[pltpu.VMEM((SUBLANES, cb), F32)],
        compiler_params=_cparams(("arbitrary", "arbitrary")),
    )(*dpm_list, *dpm_list, p_all, p_all, mu)


def _prep_fwd(name, pm, cfg, w0, a0, kkw, kaw, wd, wi, wg):
    t = pm.shape[0]
    dr, lp, cb = cfg["dr"], cfg["lp"], cfg["cb"]
    tt = _tile(t, (256, 128, 64, 32, 16, 8))
    nj = dr // cb
    kspec = pl.BlockSpec((tt, cb), lambda j, i: (i, nj + j))
    lspec = pl.BlockSpec((tt, lp), lambda j, i: (i, 3 * dr // lp))
    par = pl.BlockSpec((1, cb), lambda j, i: (0, j))
    wspec = pl.BlockSpec((lp, cb), lambda j, i: (0, j))
    out = pl.BlockSpec((tt, cb), lambda j, i: (i, j))

    def body(k_ref, l_ref, w0_ref, a0_ref, kk_ref, ka_ref, wd_ref, wi_ref, wg_ref, *outs):
        vals = _prep_fn(k_ref[...], l_ref[...], w0_ref[...], a0_ref[...], kk_ref[...], ka_ref[...],
                        wd_ref[...], wi_ref[...], wg_ref[...])
        for o_ref, val in zip(outs, vals):
            o_ref[...] = val

    return pl.pallas_call(
        body, name=name, out_shape=[jax.ShapeDtypeStruct((t, dr), F32)] * 5,
        grid=(nj, t // tt), in_specs=[kspec, lspec, par, par, par, par, wspec, wspec, wspec],
        out_specs=[out] * 5, compiler_params=_cparams(("arbitrary", "arbitrary")),
    )(pm, pm, w0, a0, kkw, kaw, wd, wi, wg)


def _prep_bwd(name, pm, cfg, w0, a0, kkw, kaw, wd, wi, wg, cts, dr_parts, dv_parts):
    t = pm.shape[0]
    dr, lp, cb = cfg["dr"], cfg["lp"], cfg["cb"]
    tt = _tile(t, (256, 128, 64, 32, 16, 8))
    nj = dr // cb
    kspec = pl.BlockSpec((tt, cb), lambda j, i: (i, nj + j))
    lspec = pl.BlockSpec((tt, lp), lambda j, i: (i, 3 * dr // lp))
    par = pl.BlockSpec((1, cb), lambda j, i: (0, j))
    wspec = pl.BlockSpec((lp, cb), lambda j, i: (0, j))
    blk = pl.BlockSpec((tt, cb), lambda j, i: (i, j))
    dpl_spec = pl.BlockSpec((None, tt, lp), lambda j, i: (j, i, 0))

    def body(k_ref, l_ref, w0_ref, a0_ref, kk_ref, ka_ref, wd_ref, wi_ref, wg_ref,
             dlw_ref, dk2a_ref, dk2b_ref, da_ref, db_ref, dg_ref, dr0_ref, dr1_ref, dv0_ref, dv1_ref,
             dpr_ref, dpk_ref, dpv_ref, dpl_ref, dw0_ref, da0_ref, dkk_ref, dka_ref, dwd_ref, dwi_ref, dwg_ref):
        _, vjp = jax.vjp(_prep_fn, k_ref[...], l_ref[...], w0_ref[...], a0_ref[...], kk_ref[...], ka_ref[...],
                         wd_ref[...], wi_ref[...], wg_ref[...])
        dk, dpl, dw0, da0, dkk, dka, dwd, dwi, dwg = vjp(
            (dlw_ref[...], dk2a_ref[...] + dk2b_ref[...], da_ref[...], db_ref[...], dg_ref[...]))
        dpr_ref[...] = dr0_ref[...] + dr1_ref[...]
        dpv_ref[...] = dv0_ref[...] + dv1_ref[...]
        dpk_ref[...] = dk
        dpl_ref[...] = dpl
        first = pl.program_id(1) == 0
        for ref, val in ((dw0_ref, dw0), (da0_ref, da0), (dkk_ref, dkk), (dka_ref, dka),
                         (dwd_ref, dwd), (dwi_ref, dwi), (dwg_ref, dwg)):
            _acc_out(ref, val, first)

    out_shape = ([jax.ShapeDtypeStruct((t, dr), F32)] * 3 + [jax.ShapeDtypeStruct((nj, t, lp), F32)]
                 + [jax.ShapeDtypeStruct((1, dr), F32)] * 4 + [jax.ShapeDtypeStruct((lp, dr), F32)] * 3)
    return pl.pallas_call(
        body, name=name, out_shape=out_shape, grid=(nj, t // tt),
        in_specs=[kspec, lspec, par, par, par, par, wspec, wspec, wspec] + [blk] * 10,
        out_specs=[blk] * 3 + [dpl_spec] + [par] * 4 + [wspec] * 3,
        compiler_params=_cparams(("arbitrary", "arbitrary")),
    )(pm, pm, w0, a0, kkw, kaw, wd, wi, wg, *cts, *dr_parts, *dv_parts)


def _post_specs(t, cfg):
    dr, cb = cfg["dr"], cfg["cb"]
    tt = _tile(t, (256, 128, 64, 32, 16, 8))
    nj = dr // cb
    blk = pl.BlockSpec((tt, cb), lambda j, i: (i, j))
    rspec = pl.BlockSpec((tt, cb), lambda j, i: (i, j))
    vspec = pl.BlockSpec((tt, cb), lambda j, i: (i, 2 * nj + j))
    par = pl.BlockSpec((1, cb), lambda j, i: (0, j))
    return tt, nj, blk, rspec, vspec, par


def _post_fwd(name, y, pm, k2, g, lnw, lnb, rk, cfg):
    t = y.shape[0]
    tt, nj, blk, rspec, vspec, par = _post_specs(t, cfg)

    def body(y_ref, r_ref, k_ref, v_ref, g_ref, lw_ref, lb_ref, rk_ref, o_ref):
        o_ref[...] = _post_fn(y_ref[...], r_ref[...], k_ref[...], v_ref[...], g_ref[...],
                              lw_ref[...], lb_ref[...], rk_ref[...]).astype(BF16)

    return pl.pallas_call(
        body, name=name, out_shape=jax.ShapeDtypeStruct((t, cfg["dr"]), BF16), grid=(nj, t // tt),
        in_specs=[blk, rspec, blk, vspec, blk, par, par, par], out_specs=blk,
        compiler_params=_cparams(("arbitrary", "arbitrary")),
    )(y, pm, k2, pm, g, lnw, lnb, rk)


def _post_bwd(name, y, pm, k2, g, lnw, lnb, rk, dout, cfg):
    t = y.shape[0]
    tt, nj, blk, rspec, vspec, par = _post_specs(t, cfg)

    def body(y_ref, r_ref, k_ref, v_ref, g_ref, lw_ref, lb_ref, rk_ref, do_ref,
             dy_ref, dr_ref, dk_ref, dv_ref, dg_ref, dlw_ref, dlb_ref, drk_ref):
        _, vjp = jax.vjp(_post_fn, y_ref[...], r_ref[...], k_ref[...], v_ref[...], g_ref[...],
                         lw_ref[...], lb_ref[...], rk_ref[...])
        dy, dr, dk, dv, dg, dlw, dlb, drk = vjp(do_ref[...])
        for ref, val in ((dy_ref, dy), (dr_ref, dr), (dk_ref, dk), (dv_ref, dv), (dg_ref, dg)):
            ref[...] = val
        first = pl.program_id(1) == 0
        for ref, val in ((dlw_ref, dlw), (dlb_ref, dlb), (drk_ref, drk)):
            _acc_out(ref, val, first)

    dr = cfg["dr"]
    return pl.pallas_call(
        body, name=name,
        out_shape=[jax.ShapeDtypeStruct((t, dr), F32)] * 5 + [jax.ShapeDtypeStruct((1, dr), F32)] * 3,
        grid=(nj, t // tt),
        in_specs=[blk, rspec, blk, vspec, blk, par, par, par, blk],
        out_specs=[blk] * 5 + [par] * 3,
        compiler_params=_cparams(("arbitrary", "arbitrary")),
    )(y, pm, k2, pm, g, lnw, lnb, rk, dout)


def _conv_specs(t, cfg):
    dc, cb = cfg["dc"], cfg["cb"]
    tt = _tile(t, LIGHT_ROWS)
    nj = dc // cb
    c0 = cfg["off_conv"] // cb
    n8 = t // SUBLANES

    def sect(s):
        col = lambda j: c0 + s * nj + j
        main = pl.BlockSpec((tt, cb), lambda j, i: (i, col(j)))
        prev = pl.BlockSpec((SUBLANES, cb), lambda j, i: (jnp.maximum(i * (tt // SUBLANES) - 1, 0), col(j)))
        nxt = pl.BlockSpec((SUBLANES, cb), lambda j, i: (jnp.minimum((i + 1) * (tt // SUBLANES), n8 - 1), col(j)))
        return main, prev, nxt

    blk = pl.BlockSpec((tt, cb), lambda j, i: (i, j))
    wspec = pl.BlockSpec((SUBLANES, cb), lambda j, i: (0, j))
    return tt, nj, n8, sect, blk, wspec


def _conv_fwd(name, p_all, cw8, cfg):
    t = p_all.shape[0]
    tt, nj, n8, sect, blk, wspec = _conv_specs(t, cfg)
    (bm, _, _), (cm, cp, _), (um, up, _) = sect(0), sect(1), sect(2)

    def body(b_ref, c_ref, cp_ref, u_ref, up_ref, w_ref, o_ref, zp_scr):
        first = pl.program_id(1) == 0
        z = c_ref[...] * u_ref[...]
        zp_scr[...] = cp_ref[...] * up_ref[...]
        o = _row(w_ref, 2) * z + _row(w_ref, 1) * _shift_down(z, zp_scr, 1, first) \
            + _row(w_ref, 0) * _shift_down(z, zp_scr, 2, first)
        o_ref[...] = (b_ref[...] * o).astype(BF16)

    return pl.pallas_call(
        body, name=name, out_shape=jax.ShapeDtypeStruct((t, cfg["dc"]), BF16), grid=(nj, t // tt),
        in_specs=[bm, cm, cp, um, up, wspec], out_specs=blk,
        scratch_shapes=[pltpu.VMEM((SUBLANES, blk.block_shape[1]), F32)],
        compiler_params=_cparams(("arbitrary", "arbitrary")),
    )(p_all, p_all, p_all, p_all, p_all, cw8)


def _conv_bwd(name, p_all, cw8, dyb, cfg):
    t = p_all.shape[0]
    tt, nj, n8, sect, blk, wspec = _conv_specs(t, cfg)
    (bm, _, bn), (cm, cp, _), (um, up, _) = sect(0), sect(1), sect(2)
    cb = blk.block_shape[1]
    dnxt = pl.BlockSpec((SUBLANES, cb), lambda j, i: (jnp.minimum((i + 1) * (tt // SUBLANES), n8 - 1), j))
    nt_ = t // tt

    def body(b_ref, bn_ref, c_ref, cp_ref, u_ref, up_ref, w_ref, d_ref, dn_ref,
             db_ref, dc_ref, du_ref, dw_ref, zp_scr, don_scr):
        i = pl.program_id(1)
        first, last = i == 0, i == nt_ - 1
        c, u, b, dy = c_ref[...], u_ref[...], b_ref[...], d_ref[...]
        z = c * u
        zp_scr[...] = cp_ref[...] * up_ref[...]
        z1 = _shift_down(z, zp_scr, 1, first)
        z2 = _shift_down(z, zp_scr, 2, first)
        w0, w1, w2 = _row(w_ref, 0), _row(w_ref, 1), _row(w_ref, 2)
        o = w2 * z + w1 * z1 + w0 * z2
        do = dy * b
        don_scr[...] = dn_ref[...] * bn_ref[...]
        dz = w2 * do + w1 * _shift_up(do, don_scr, 1, last) + w0 * _shift_up(do, don_scr, 2, last)
        db_ref[...] = (dy * o).astype(BF16)
        dc_ref[...] = (dz * u).astype(BF16)
        du_ref[...] = (dz * c).astype(BF16)
        rows = lax.broadcasted_iota(jnp.int32, (SUBLANES, cb), 0)
        s0 = jnp.sum(do * z2, axis=0, keepdims=True)
        s1 = jnp.sum(do * z1, axis=0, keepdims=True)
        s2 = jnp.sum(do * z, axis=0, keepdims=True)
        dw = jnp.where(rows == 0, s0, jnp.where(rows == 1, s1, jnp.where(rows == 2, s2, 0.0)))
        _acc_out(dw_ref, dw, first)

    dc = cfg["dc"]
    return pl.pallas_call(
        body, name=name,
        out_shape=[jax.ShapeDtypeStruct((t, dc), BF16)] * 3 + [jax.ShapeDtypeStruct((SUBLANES, dc), F32)],
        grid=(nj, nt_),
        in_specs=[bm, bn, cm, cp, um, up, wspec, blk, dnxt],
        out_specs=[blk] * 3 + [wspec],
        scratch_shapes=[pltpu.VMEM((SUBLANES, cb), F32), pltpu.VMEM((SUBLANES, cb), F32)],
        compiler_params=_cparams(("arbitrary", "arbitrary")),
    )(p_all, p_all, p_all, p_all, p_all, p_all, cw8, dyb, dyb)


def _merge_specs(t, cfg):
    d, cb = cfg["d"], cfg["cb"]
    tt = _tile(t, LIGHT_ROWS)
    nj = d // cb
    g0 = cfg["off_gate"] // cb
    ga = pl.BlockSpec((tt, cb), lambda j, i: (i, g0 + j))
    gb = pl.BlockSpec((tt, cb), lambda j, i: (i, g0 + nj + j))
    ba = pl.BlockSpec((1, cb), lambda j, i: (0, j))
    bb = pl.BlockSpec((1, cb), lambda j, i: (0, nj + j))
    blk = pl.BlockSpec((tt, cb), lambda j, i: (i, j))
    return tt, nj, ga, gb, ba, bb, blk


def _merge_fwd(name, p_all, bias, ya, yb, cfg):
    t = p_all.shape[0]
    tt, nj, ga, gb, ba, bb, blk = _merge_specs(t, cfg)

    def body(ga_ref, gb_ref, ba_ref, bb_ref, ya_ref, yb_ref, o_ref):
        o_ref[...] = _merge_fn(ga_ref[...], gb_ref[...], ba_ref[...], bb_ref[...],
                               ya_ref[...], yb_ref[...]).astype(BF16)

    return pl.pallas_call(
        body, name=name, out_shape=jax.ShapeDtypeStruct((t, cfg["d"]), BF16), grid=(nj, t // tt),
        in_specs=[ga, gb, ba, bb, blk, blk], out_specs=blk,
        compiler_params=_cparams(("arbitrary", "arbitrary")),
    )(p_all, p_all, bias, bias, ya, yb)


def _merge_bwd(name, p_all, bias, ya, yb, dm, cfg):
    t = p_all.shape[0]
    tt, nj, ga, gb, ba, bb, blk = _merge_specs(t, cfg)

    def body(ga_ref, gb_ref, ba_ref, bb_ref, ya_ref, yb_ref, dm_ref,
             dga_ref, dgb_ref, dya_ref, dyb_ref, dba_ref, dbb_ref):
        _, vjp = jax.vjp(_merge_fn, ga_ref[...], gb_ref[...], ba_ref[...], bb_ref[...], ya_ref[...], yb_ref[...])
        dga, dgb, dba, dbb, dya, dyb = vjp(dm_ref[...])
        for ref, val in ((dga_ref, dga), (dgb_ref, dgb), (dya_ref, dya), (dyb_ref, dyb)):
            ref[...] = val.astype(BF16)
        first = pl.program_id(1) == 0
        _acc_out(dba_ref, dba, first)
        _acc_out(dbb_ref, dbb, first)

    d = cfg["d"]
    par = pl.BlockSpec((1, blk.block_shape[1]), lambda j, i: (0, j))
    return pl.pallas_call(
        body, name=name,
        out_shape=[jax.ShapeDtypeStruct((t, d), BF16)] * 4 + [jax.ShapeDtypeStruct((1, d), F32)] * 2,
        grid=(nj, t // tt),
        in_specs=[ga, gb, ba, bb, blk, blk, blk], out_specs=[blk] * 4 + [par] * 2,
        compiler_params=_cparams(("arbitrary", "arbitrary")),
    )(p_all, p_all, bias, bias, ya, yb, dm)


PAIRS = 8


def _pair_stack(ref, pairs):
    return jnp.stack([ref[:, p * LANES:(p + 1) * LANES] for p in range(pairs)])


def _pair_store(ref, val):
    for p in range(val.shape[0]):
        ref[:, p * LANES:(p + 1) * LANES] = val[p]


def _rec_specs(t, cfg, rev):
    dr = cfg["dr"]
    nc = t // CHUNK
    hp = dr // LANES
    pairs = _tile(hp, (PAIRS, 2, 1))
    ng = hp // pairs
    w = LANES * pairs
    ch = (lambda c: nc - 1 - c) if rev else (lambda c: c)
    slab = pl.BlockSpec((CHUNK, w), lambda h, c: (ch(c), h))
    vspec = pl.BlockSpec((CHUNK, w), lambda h, c: (ch(c), 2 * ng + h))
    sspec = pl.BlockSpec((None, pairs, LANES, LANES), lambda h, c: (ch(c), h, 0, 0))
    first = lambda: jnp.logical_and(pl.program_id(0) == 0, pl.program_id(1) == 0)
    last = lambda: jnp.logical_and(pl.program_id(0) == ng - 1, pl.program_id(1) == nc - 1)
    return nc, hp, pairs, ng, slab, vspec, sspec, first, last


def _rec_fwd(name, pm, lw, k2, a, b, cfg, comm=None):
    t = pm.shape[0]
    nc, hp, pairs, ng, slab, vspec, sspec, first, last = _rec_specs(t, cfg, False)

    def body(r_ref, lw_ref, k_ref, v_ref, a_ref, b_ref, y_ref, s_ref, s_scr):
        @pl.when(pl.program_id(1) == 0)
        def _():
            s_scr[...] = jnp.zeros_like(s_scr)

        s = s_scr[...]
        s_ref[...] = s
        y, s_new = _chunk_fn(s, *[_pair_stack(ref, pairs) for ref in (r_ref, lw_ref, k_ref, v_ref, a_ref, b_ref)])
        _pair_store(y_ref, y)
        s_scr[...] = s_new

    return _hosted_call(
        body, name, comm, first, last, args=[pm, lw, k2, pm, a, b],
        in_specs=[slab, slab, slab, vspec, slab, slab],
        out_shape=[jax.ShapeDtypeStruct((t, cfg["dr"]), F32), jax.ShapeDtypeStruct((nc, hp, LANES, LANES), F32)],
        out_specs=[slab, sspec], scratch=[pltpu.VMEM((pairs, LANES, LANES), F32)], grid=(ng, nc),
        sem=("arbitrary", "arbitrary"))


def _rec_bwd(name, pm, lw, k2, a, b, s_chk, dy, cfg, comm=None):
    t = pm.shape[0]
    nc, hp, pairs, ng, slab, vspec, sspec, first, last = _rec_specs(t, cfg, True)

    def body(r_ref, lw_ref, k_ref, v_ref, a_ref, b_ref, s_ref, dy_ref,
             dr_ref, dlw_ref, dk_ref, dv_ref, da_ref, db_ref, ds_scr):
        @pl.when(pl.program_id(1) == 0)
        def _():
            ds_scr[...] = jnp.zeros_like(ds_scr)

        _, vjp = jax.vjp(_chunk_fn, s_ref[...],
                         *[_pair_stack(ref, pairs) for ref in (r_ref, lw_ref, k_ref, v_ref, a_ref, b_ref)])
        ds, dr, dlw, dk, dv, da, db = vjp((_pair_stack(dy_ref, pairs), ds_scr[...]))
        ds_scr[...] = ds
        for ref, val in ((dr_ref, dr), (dlw_ref, dlw), (dk_ref, dk), (dv_ref, dv), (da_ref, da), (db_ref, db)):
            _pair_store(ref, val)

    return _hosted_call(
        body, name, comm, first, last, args=[pm, lw, k2, pm, a, b, s_chk, dy],
        in_specs=[slab, slab, slab, vspec, slab, slab, sspec, slab],
        out_shape=[jax.ShapeDtypeStruct((t, cfg["dr"]), F32)] * 6, out_specs=[slab] * 6,
        scratch=[pltpu.VMEM((pairs, LANES, LANES), F32)], grid=(ng, nc), sem=("arbitrary", "arbitrary"))


def _comm_call(name, comm):
    n = comm.n
    hbm = pl.BlockSpec(memory_space=pl.ANY)

    def body(*refs):
        comm.start(refs[:n], refs[n:2 * n], refs[2 * n:])
        comm.wait(refs[:n], refs[n:2 * n], refs[2 * n:])

    return pl.pallas_call(body, name=name, out_shape=comm.out_shape, in_specs=[hbm] * n, out_specs=[hbm] * n,
                          scratch_shapes=comm.scratch)(*comm.arrs)


def _all_reduce_small(name, v):
    rows = v.shape[0]
    vm = pl.BlockSpec(memory_space=pltpu.VMEM)

    def body(x_ref, out_ref, buf, send_sems, recv_sems):
        x, y, c = _my_pos()
        me, sibling = (x, y, c), (x, y, 1 - c)
        chips = [(1 - x, y), (x, 1 - y), (1 - x, 1 - y)]

        def copy(k, block, to, src=None):
            px, py, pc = block
            dst = buf.at[4 * px + 2 * py + pc]
            return pltpu.make_async_remote_copy(
                src_ref=dst if src is None else src, dst_ref=dst,
                send_sem=send_sems.at[k], recv_sem=recv_sems.at[k], device_id=to, device_id_type=MESH)

        buf[4 * x + 2 * y + c] = x_ref[...]
        first = [copy(0, me, sibling, src=x_ref)]
        first += [copy(1 + j, me, (*chip, c), src=x_ref) for j, chip in enumerate(chips)]
        for cp in first:
            cp.start()
        passed = [copy(4 + j, (*chip, c), sibling) for j, chip in enumerate(chips)]
        for j, chip in enumerate(chips):
            copy(1 + j, (*chip, c), me).wait_recv()
            passed[j].start()
        copy(0, sibling, me).wait_recv()
        for j, chip in enumerate(chips):
            copy(4 + j, (*chip, 1 - c), me).wait_recv()
        for cp in first + passed:
            cp.wait_send()
        acc = buf[0]
        for d in range(1, N_DEV):
            acc = acc + buf[d]
        out_ref[...] = acc

    return pl.pallas_call(
        body, name=name, out_shape=jax.ShapeDtypeStruct(v.shape, F32),
        in_specs=[vm], out_specs=vm,
        scratch_shapes=[pltpu.VMEM((N_DEV, rows, LANES), F32), pltpu.SemaphoreType.DMA((7,)),
                        pltpu.SemaphoreType.DMA((7,))],
    )(v)


def _pair_sum(name, slabs, got, core):
    _, rows, cols = slabs.shape
    nq = got.shape[0]
    rb = _tile(rows, (256, 128, 64, 32, 16, 8))
    mine = pl.BlockSpec((None, rb, cols), lambda q, j, c_ref: (2 * q + c_ref[0], j, 0))
    blk = pl.BlockSpec((None, rb, cols), lambda q, j, c_ref: (q, j, 0))

    def body(c_ref, a_ref, b_ref, o_ref):
        o_ref[...] = (a_ref[...].astype(F32) + b_ref[...].astype(F32)).astype(o_ref.dtype)

    return pl.pallas_call(
        body, name=name, out_shape=jax.ShapeDtypeStruct(got.shape, got.dtype),
        grid_spec=pltpu.PrefetchScalarGridSpec(num_scalar_prefetch=1, grid=(nq, rows // rb),
                                               in_specs=[mine, blk], out_specs=blk),
        compiler_params=_cparams(("arbitrary", "arbitrary")))(core, slabs, got)


def _adamw(name, w, m, v, g_own, g_recv=None):
    rows, cols = w.shape
    nr = g_recv.shape[0] if g_recv is not None else 0
    per_el = 4 * 3 + g_own.dtype.itemsize + (nr * g_recv.dtype.itemsize if nr else 0) + 16
    rb = SUBLANES * 2
    while rb * 2 <= rows and rows % (rb * 2) == 0 and rb * 2 * cols * per_el * 2 <= VMEM_LIMIT // 2:
        rb *= 2
    if rows % rb:
        rb = rows
    blk = pl.BlockSpec((rb, cols), lambda i: (i, 0))
    rblk = pl.BlockSpec((max(nr, 1), rb, cols), lambda i: (0, i, 0))
    has_r = g_recv is not None
    bc1 = 1.0 - ADAM_B1 ** ADAM_STEP
    bc2 = 1.0 - ADAM_B2 ** ADAM_STEP

    def body(*refs):
        w_ref, m_ref, v_ref, go_ref = refs[:4]
        gr_ref = refs[4] if has_r else None
        g_out, d_out, m_out, v_out = refs[4 + has_r:]
        g = go_ref[...].astype(F32)
        if has_r:
            for r in range(nr):
                g = g + gr_ref[r].astype(F32)
        mn = ADAM_B1 * m_ref[...] + (1.0 - ADAM_B1) * g
        vn = ADAM_B2 * v_ref[...] + (1.0 - ADAM_B2) * (g * g)
        m_hat = mn / bc1
        v_hat = vn / bc2
        g_out[...] = g
        d_out[...] = -ADAM_LR * (m_hat / (jnp.sqrt(v_hat) + ADAM_EPS) + ADAM_WD * w_ref[...])
        m_out[...] = mn
        v_out[...] = vn

    return pl.pallas_call(
        body, name=name, out_shape=[jax.ShapeDtypeStruct((rows, cols), F32)] * 4, grid=(rows // rb,),
        in_specs=[blk] * 4 + ([rblk] if has_r else []), out_specs=[blk] * 4,
        compiler_params=_cparams(("arbitrary",)),
    )(*([w, m, v, g_own] + ([g_recv] if has_r else [])))


def _round_up(n, q):
    return (n + q - 1) // q * q


def _cols(a8):
    return jnp.transpose(a8, (1, 0, 2)).reshape(a8.shape[1], -1)


def _col_slabs(a):
    r_, c_ = a.shape
    return jnp.transpose(a.reshape(r_, N_DEV, c_ // N_DEV), (1, 0, 2))


def _padded_from_slabs_call(name, slabs, gap_at, gap, total):
    _, rows, c8 = slabs.shape
    tr = _tile(rows, (128, 64, 32, 16))

    def body(s_ref, o_ref):
        o_ref[...] = jnp.zeros_like(o_ref)
        for dd in range(N_DEV):
            lo, hi = dd * c8, (dd + 1) * c8
            if gap and lo <= gap_at < hi:
                cut = gap_at - lo
                if cut:
                    o_ref[:, lo:gap_at] = s_ref[dd, :, :cut]
                o_ref[:, gap_at + gap:hi + gap] = s_ref[dd, :, cut:]
            else:
                off = lo + (gap if lo >= gap_at else 0)
                o_ref[:, off:off + c8] = s_ref[dd]

    return pl.pallas_call(
        body, name=name, out_shape=jax.ShapeDtypeStruct((rows, total), slabs.dtype), grid=(rows // tr,),
        in_specs=[pl.BlockSpec((N_DEV, tr, c8), lambda i: (0, i, 0))], out_specs=pl.BlockSpec((tr, total), lambda i: (i, 0)),
        compiler_params=_cparams(("arbitrary",)))(slabs)


def _slabs_from_padded_call(name, mat, gap_at, gap, c8):
    rows, total = mat.shape
    tr = _tile(rows, (128, 64, 32, 16))

    def body(m_ref, o_ref):
        for dd in range(N_DEV):
            lo, hi = dd * c8, (dd + 1) * c8
            if gap and lo < gap_at < hi:
                cut = gap_at - lo
                o_ref[dd, :, :cut] = m_ref[:, lo:gap_at]
                o_ref[dd, :, cut:] = m_ref[:, gap_at + gap:hi + gap]
            else:
                off = lo + (gap if lo >= gap_at else 0)
                o_ref[dd] = m_ref[:, off:off + c8]

    return pl.pallas_call(
        body, name=name, out_shape=jax.ShapeDtypeStruct((N_DEV, rows, c8), mat.dtype), grid=(rows // tr,),
        in_specs=[pl.BlockSpec((tr, total), lambda i: (i, 0))], out_specs=pl.BlockSpec((N_DEV, tr, c8), lambda i: (0, i, 0)),
        compiler_params=_cparams(("arbitrary",)))(mat)


_MID = ("w_out_a", "w_out_b", "w_out", "w_mlp_up", "w_mlp_down")


def _local_step(x, target, wts, shards, cfg):
    dr, dc, d, lp, cb = cfg["dr"], cfg["dc"], cfg["d"], cfg["lp"], cfg["cb"]
    dff = shards["w_mlp_down"].shape[0] * N_DEV
    wmix = 3 * dr + lp
    (xn,) = _norm_fwd("norm_mix_fwd", x, None, wts["norm_mix_w"], False)
    half = d // 2
    (p_top,), (bot8,) = _matmul("mm_in_top", xn[:, :half], wts["w_top"], "nn", [F32],
                                comm=_Comm("gather", [shards["w_in_bot"]]))
    w_bot = _padded_from_slabs_call("relayout_w_bot", bot8, 3 * dr + cfg["lora"], lp - cfg["lora"], cfg["wall"])
    (p_all,), (g_oa, g_ob, g_o) = _matmul(
        "mm_in_bot", xn[:, half:], w_bot, "nn", [F32], epi=lambda r, top: (r + top,), extras=(p_top,),
        comm=_Comm("gather", [shards["w_out_a"], shards["w_out_b"], shards["w_out"]]))
    w_out_a, w_out_b, w_out = _cols(g_oa), _cols(g_ob), g_o.reshape(d, d)
    pm = _mix_fwd("mix_fwd", p_all, wts["mu_pad"], wmix, cb)
    prep_w = (wts["w0"], wts["a0"], wts["k_k"], wts["k_a"], wts["wd"], wts["wi"], wts["wg"])
    lw, k2, a_in, b_in, g = _prep_fwd("prep_fwd", pm, cfg, *prep_w)
    (y_raw, s_chk), (g_u,) = _rec_fwd(
        "rec_fwd", pm, lw, k2, a_in, b_in, cfg, comm=_Comm("gather", [shards["w_mlp_up"]]))
    w_up = _cols(g_u)
    post_w = (wts["lnx_w"], wts["lnx_b"], wts["r_k"])
    ya_in = _post_fwd("post_fwd", y_raw, pm, k2, g, *post_w, cfg)
    (ya,) = _matmul("mm_out_a", ya_in, w_out_a, "nn", [F32])
    yb_in = _conv_fwd("conv_fwd", p_all, wts["conv_w8"], cfg)
    (yb,) = _matmul("mm_out_b", yb_in, w_out_b, "nn", [F32])
    mg = _merge_fwd("merge_fwd", p_all, wts["gate_bias"], ya, yb, cfg)
    (mo,) = _matmul("mm_out", mg, w_out, "nn", [F32])
    h1, hn = _norm_fwd("norm_mlp_fwd", x, mo, wts["norm_mlp_w"], True)
    (u, act), (g_d,) = _matmul("mm_up", hn, w_up, "nn", [F32, BF16],
                               epi=lambda r: (r, jnp.square(jnp.maximum(r, 0.0))),
                               comm=_Comm("gather", [shards["w_mlp_down"]]))
    w_down = g_d.reshape(dff, d)
    (md,) = _matmul("mm_down", act, w_down, "nn", [F32])
    loss, dh2, dh2b, g_norm_final = _final("final", h1, md, target, wts["norm_final_w"])
    (du,) = _matmul("mm_down_dx", dh2b, w_down, "nt", [BF16],
                    epi=lambda r, uu: (r * (2.0 * jnp.maximum(uu, 0.0)),), extras=(u,))
    (g_down,) = _matmul("mm_down_dw", act, dh2b, "tn", [BF16])
    core = lax.axis_index("c").astype(jnp.int32).reshape(1)
    my_chip = 2 * lax.axis_index("x") + lax.axis_index("y")
    me = 2 * my_chip + lax.axis_index("c")
    own, recv = {}, {}

    def chip_own(chip_sum):
        return lax.dynamic_index_in_dim(chip_sum, my_chip, axis=0, keepdims=False)

    down_slabs = g_down.reshape(N_DEV, dff // N_DEV, d)
    (dhn,), (got,) = _matmul("mm_up_dx", du, w_up, "nt", [F32], comm=_Comm("pair", [down_slabs]))
    down_sum = _pair_sum("pair_sum_down", down_slabs, got, core)
    (g_up,) = _matmul("mm_up_dw", hn, du, "tn", [BF16], out_slabs=True)
    own["w_mlp_down"] = chip_own(down_sum)
    dh1, dh1b, g_norm_mlp = _norm_bwd("norm_mlp_bwd", h1, dhn, dh2, wts["norm_mlp_w"])
    (dmg,), (got,) = _matmul("mm_out_dx", dh1b, w_out, "nt", [F32], comm=_Comm("pair", [g_up]))
    up_sum = _pair_sum("pair_sum_up", g_up, got, core)
    own["w_mlp_up"] = chip_own(up_sum)
    (g_out,) = _matmul("mm_out_dw", mg, dh1b, "tn", [BF16])
    dpga, dpgb, dya, dyb, dba, dbb = _merge_bwd("merge_bwd", p_all, wts["gate_bias"], ya, yb, dmg, cfg)
    (dya_in,) = _matmul("mm_out_a_dx", dya, w_out_a, "nt", [F32])
    (g_out_a,) = _matmul("mm_out_a_dw", ya_in, dya, "tn", [BF16], out_slabs=True)
    (dyb_in,) = _matmul("mm_out_b_dx", dyb, w_out_b, "nt", [F32])
    (g_out_b,) = _matmul("mm_out_b_dw", yb_in, dyb, "tn", [BF16], out_slabs=True)
    dpb, dpc, dpu, g_conv8 = _conv_bwd("conv_bwd", p_all, wts["conv_w8"], dyb_in, cfg)
    dy_raw, dr_post, dk_post, dv_post, dg, g_lnw, g_lnb, g_rk = _post_bwd(
        "post_bwd", y_raw, pm, k2, g, *post_w, dya_in, cfg)
    (dr_rec, dlw, dk_rec, dv_rec, da_in, db_in), (recv["w_mlp_up"], recv["w_mlp_down"]) = _rec_bwd(
        "rec_bwd", pm, lw, k2, a_in, b_in, s_chk, dy_raw, cfg, comm=_Comm("chips", [up_sum, down_sum]))
    (dpm_r, dpm_k, dpm_v, dpl, g_w0, g_a0, g_kk, g_ka, g_wd, g_wi, g_wg) = _prep_bwd(
        "prep_bwd", pm, cfg, *prep_w, (dlw, dk_rec, dk_post, da_in, db_in, dg),
        (dr_rec, dr_post), (dv_rec, dv_post))
    mu = wts["mu_pad"]
    nb = dr // cb
    dps, dmus = [], []
    for s, dpm_s in enumerate((dpm_r, dpm_k, dpm_v)):
        dp_s, dmu_s = _mix_bwd("mix_bwd_%d" % s, [dpm_s], p_all, s * nb, mu[:, s * dr:(s + 1) * dr], cb)
        dps.append(dp_s)
        dmus.append(dmu_s)
    dp_l, dmu_l = _mix_bwd("mix_bwd_l", [dpl[j] for j in range(nb)], p_all, 3 * nb, mu[:, 3 * dr:], min(cb, lp))
    tail = [jnp.zeros((x.shape[0], cfg["wall"] - cfg["used"]), BF16)] if cfg["wall"] > cfg["used"] else []
    dp_all = jnp.concatenate(dps + [dp_l, dpb, dpc, dpu, dpga, dpgb] + tail, axis=1)
    ld, li, lora = cfg["ld"], cfg["li"], cfg["lora"]
    g_small = jnp.concatenate([g_wd[:ld], g_wi[ld:ld + li], g_wg[ld + li:lora], g_conv8[:3]], axis=0)
    g_small = jnp.pad(g_small, ((0, cfg["small_rows"] - g_small.shape[0]), (0, 0)))
    direct = dict(w_out_a=g_out_a, w_out_b=g_out_b, w_out=g_out.reshape(N_DEV, d // N_DEV, d),
                  small=_col_slabs(g_small))
    (g_all,), got4 = _matmul("mm_in_dw", xn, dp_all, "tn", [BF16],
                             comm=_Comm("exchange", list(direct.values())))
    for n, slabs, r in zip(direct, direct.values(), got4):
        own[n] = lax.dynamic_index_in_dim(slabs, me, axis=0, keepdims=False)
        recv[n] = r
    in_slabs = _slabs_from_padded_call("relayout_g_in", g_all, 3 * dr + lora, lp - lora,
                                       (cfg["used"] - lp + lora) // N_DEV)
    (got,) = _comm_call("pair_exchange", _Comm("pair", [in_slabs]))
    in_sum = _pair_sum("pair_sum_in", in_slabs, got, core)
    (dxn,), (recv["w_in"],) = _matmul("mm_in_dx", dp_all, (wts["w_top"], w_bot), "nt", [F32],
                                      comm=_Comm("chips", [in_sum]))
    own["w_in"] = chip_own(in_sum)
    grad_x, _, g_norm_mix = _norm_bwd("norm_mix_bwd", x, dxn, dh1, wts["norm_mix_w"])
    grads = dict(
        norm_mix_w=g_norm_mix, gate_bias=jnp.concatenate([dba, dbb], axis=1),
        mu_pad=jnp.concatenate(dmus + [dmu_l], axis=1), w0=g_w0, a0=g_a0, k_k=g_kk, k_a=g_ka,
        r_k=g_rk, lnx_w=g_lnw, lnx_b=g_lnb, norm_mlp_w=g_norm_mlp, norm_final_w=g_norm_final)
    return loss, grad_x, grads, own, recv


_SMALL = ("norm_mix_w", "gate_bias", "shift_mu", "w0", "a0", "k_k", "k_a", "r_k", "lnx_w", "lnx_b",
          "norm_mlp_w", "norm_final_w")
_ORDER = ("norm_mix_w", "w_in", "gate_bias", "shift_mu", "w0", "w_decay_up", "a0", "w_iclr_up", "w_gate_up",
          "k_k", "k_a", "r_k", "lnx_w", "lnx_b", "w_out_a", "conv_w", "w_out_b", "w_out", "norm_mlp_w",
          "w_mlp_up", "w_mlp_down", "norm_final_w")


def _step(x, target, w, m, v):
    t, d = x.shape[1], x.shape[2]
    dr = w["w0"].shape[-1]
    ld, li, lg = w["w_decay_up"].shape[1], w["w_iclr_up"].shape[1], w["w_gate_up"].shape[1]
    lora = ld + li + lg
    lp = _round_up(lora, LANES)
    dc = w["conv_w"].shape[-1] * N_DEV
    cb = math.gcd(math.gcd(lp, dr), 512)
    used = 3 * dr + lp + 3 * dc + 2 * d
    wall = _round_up(used, 1024 if used > MAX_FULL_K else LANES)
    small_rows = ld + li + lg + 3
    cfg = dict(d=d, dr=dr, dc=dc, lp=lp, cb=cb, off_conv=3 * dr + lp, off_gate=3 * dr + lp + 3 * dc, used=used,
               wall=wall, ld=ld, li=li, lora=lora, small_rows=_round_up(small_rows, SUBLANES))
    x2, tg2 = x[0], target[0]

    small_sh = jnp.concatenate([w["w_decay_up"][0], w["w_iclr_up"][0], w["w_gate_up"][0], w["conv_w"][0]], axis=0)
    small_sh = jnp.pad(small_sh, ((0, _round_up(small_rows, SUBLANES) - small_rows), (0, 0)))
    big = ("w_in",) + _MID
    w_in_b = w["w_in"][0].astype(BF16)
    top8, gsm = _comm_call("gather_weights", _Comm("gather", [w_in_b[:d // 2], small_sh]))
    shards = {n: w[n][0].astype(BF16) for n in _MID}
    shards["w_in_bot"] = w_in_b[d // 2:]
    w_top = _padded_from_slabs_call("relayout_w_top", top8, 3 * dr + lora, lp - lora, wall)
    sm = _cols(gsm)
    lora_full = sm[:lora]

    def lora_pad(lo, hi):
        rows = lax.broadcasted_iota(jnp.int32, (lp, 1), 0)
        full = jnp.pad(lora_full, ((0, lp - lora), (0, 0)))
        return jnp.where(jnp.logical_and(rows >= lo, rows < hi), full, 0.0)

    conv_w8 = jnp.pad(sm[lora:lora + 3], ((0, SUBLANES - 3), (0, 0)))
    mu_pad = jnp.pad(w["shift_mu"], ((0, 0), (0, lp - lora)))
    wts = dict(
        w_top=w_top, wd=lora_pad(0, ld), wi=lora_pad(ld, ld + li), wg=lora_pad(ld + li, lora), conv_w8=conv_w8,
        mu_pad=mu_pad, norm_mix_w=w["norm_mix_w"], gate_bias=w["gate_bias"], w0=w["w0"], a0=w["a0"],
        k_k=w["k_k"], k_a=w["k_a"], r_k=w["r_k"].reshape(1, dr), lnx_w=w["lnx_w"], lnx_b=w["lnx_b"],
        norm_mlp_w=w["norm_mlp_w"], norm_final_w=w["norm_final_w"].reshape(1, d))

    loss, grad_x, gr, own, received = _local_step(x2, tg2, wts, shards, cfg)

    small_g = dict(norm_mix_w=gr["norm_mix_w"], gate_bias=gr["gate_bias"], shift_mu=gr["mu_pad"][:, :3 * dr + lora],
                   w0=gr["w0"], a0=gr["a0"], k_k=gr["k_k"], k_a=gr["k_a"], r_k=gr["r_k"], lnx_w=gr["lnx_w"],
                   lnx_b=gr["lnx_b"], norm_mlp_w=gr["norm_mlp_w"], norm_final_w=gr["norm_final_w"])
    sizes = [small_g[n].size for n in _SMALL]
    total = sum(sizes) + 1
    prow = _round_up(total, LANES * SUBLANES) // LANES

    def pack(parts):
        flat = jnp.concatenate([p.reshape(-1) for p in parts])
        return jnp.pad(flat, (0, prow * LANES - flat.size)).reshape(prow, LANES)

    g_packed = _all_reduce_small("reduce_small", pack([small_g[n] for n in _SMALL] + [loss[0, :1]]))
    one = jnp.zeros((1,), F32)
    packed = [pack([d_[n] for n in _SMALL] + [one]) for d_ in (w, m, v)]
    sm_out = _adamw("adamw_small", *packed, g_packed)
    loss_out = g_packed.reshape(-1)[total - 1]

    def unpack(flat2d):
        flat = flat2d.reshape(-1)
        out, o = {}, 0
        for n, s in zip(_SMALL, sizes):
            out[n] = flat[o:o + s].reshape(w[n].shape)
            o += s
        return out

    res = [unpack(a) for a in sm_out]

    def shard2d(a):
        return a.reshape(-1, a.shape[-1])

    for n in big:
        outs = _adamw("adamw_" + n, shard2d(w[n]), shard2d(m[n]), shard2d(v[n]), shard2d(own[n]),
                      received[n].reshape(received[n].shape[:1] + shard2d(own[n]).shape))
        for r_, o in zip(res, outs):
            r_[n] = o.reshape(w[n].shape)
    sm_names = ("w_decay_up", "w_iclr_up", "w_gate_up", "conv_w")
    stack = lambda d_: jnp.pad(jnp.concatenate([d_[n][0] for n in sm_names], axis=0),
                               ((0, _round_up(small_rows, SUBLANES) - small_rows), (0, 0)))
    outs = _adamw("adamw_stack", stack(w), stack(m), stack(v), own["small"], received["small"])
    bounds = (0, ld, ld + li, lora, lora + 3)
    for r_, o in zip(res, outs):
        for q, n in enumerate(sm_names):
            r_[n] = o[bounds[q]:bounds[q + 1]].reshape(w[n].shape)

    grad, delta, new_m, new_v = res
    return (loss_out, grad_x[None], *[grad[n] for n in _ORDER], *[delta[n] for n in _ORDER],
            *[new_m[n] for n in _ORDER], *[new_v[n] for n in _ORDER])


def kernel(x, norm_mix_w, w_in, gate_bias, shift_mu, w0, w_decay_up, a0, w_iclr_up, w_gate_up, k_k, k_a, r_k, lnx_w, lnx_b, w_out_a, conv_w, w_out_b, w_out, norm_mlp_w, w_mlp_up, w_mlp_down, norm_final_w, loss_target, m_norm_mix_w, m_w_in, m_gate_bias, m_shift_mu, m_w0, m_w_decay_up, m_a0, m_w_iclr_up, m_w_gate_up, m_k_k, m_k_a, m_r_k, m_lnx_w, m_lnx_b, m_w_out_a, m_conv_w, m_w_out_b, m_w_out, m_norm_mlp_w, m_w_mlp_up, m_w_mlp_down, m_norm_final_w, v_norm_mix_w, v_w_in, v_gate_bias, v_shift_mu, v_w0, v_w_decay_up, v_a0, v_w_iclr_up, v_w_gate_up, v_k_k, v_k_a, v_r_k, v_lnx_w, v_lnx_b, v_w_out_a, v_conv_w, v_w_out_b, v_w_out, v_norm_mlp_w, v_w_mlp_up, v_w_mlp_down, v_norm_final_w):
    w = dict(zip(_ORDER, (norm_mix_w, w_in, gate_bias, shift_mu, w0, w_decay_up, a0, w_iclr_up, w_gate_up, k_k, k_a,
                          r_k, lnx_w, lnx_b, w_out_a, conv_w, w_out_b, w_out, norm_mlp_w, w_mlp_up, w_mlp_down,
                          norm_final_w)))
    m = dict(zip(_ORDER, (m_norm_mix_w, m_w_in, m_gate_bias, m_shift_mu, m_w0, m_w_decay_up, m_a0, m_w_iclr_up,
                          m_w_gate_up, m_k_k, m_k_a, m_r_k, m_lnx_w, m_lnx_b, m_w_out_a, m_conv_w, m_w_out_b,
                          m_w_out, m_norm_mlp_w, m_w_mlp_up, m_w_mlp_down, m_norm_final_w)))
    v = dict(zip(_ORDER, (v_norm_mix_w, v_w_in, v_gate_bias, v_shift_mu, v_w0, v_w_decay_up, v_a0, v_w_iclr_up,
                          v_w_gate_up, v_k_k, v_k_a, v_r_k, v_lnx_w, v_lnx_b, v_w_out_a, v_conv_w, v_w_out_b,
                          v_w_out, v_norm_mlp_w, v_w_mlp_up, v_w_mlp_down, v_norm_final_w)))
    return _step(x, loss_target, w, m, v)
```

```python
import math

import jax
import jax.numpy as jnp
from jax import lax
from jax.experimental import pallas as pl
from jax.experimental.pallas import tpu as pltpu

F32 = jnp.float32
BF16 = jnp.bfloat16
MESH = pl.DeviceIdType.MESH

N_DEV = 8
HEAD = 64
LANES = 128
SUBLANES = 8
CHUNK = 64
RMS_EPS = 1e-5
LNX_EPS = 64e-5
L2_EPS = 1e-12
ADAM_LR = 0.001
ADAM_B1 = 0.9
ADAM_B2 = 0.999
ADAM_EPS = 1e-08
ADAM_WD = 0.01
ADAM_STEP = 10
VMEM_LIMIT = 48 * 1024 * 1024
MAX_FULL_K = 4096
LIGHT_ROWS = (512, 256, 128, 64, 32, 16, 8)


def _cparams(sem):
    return pltpu.CompilerParams(dimension_semantics=sem, vmem_limit_bytes=VMEM_LIMIT)


def _tile(dim, cands):
    for c in cands:
        if c <= dim and dim % c == 0:
            return c
    return dim


def _my_pos():
    return lax.axis_index("x"), lax.axis_index("y"), lax.axis_index("c")


def _peer(pos, r):
    x, y, c = pos
    return (1 - x if r & 4 else x, 1 - y if r & 2 else y, 1 - c if r & 1 else c)


def _slot(pos):
    return 4 * pos[0] + 2 * pos[1] + pos[2]


class _Comm:
    def __init__(self, kind, arrs):
        self.kind, self.arrs, self.n = kind, list(arrs), len(arrs)
        if kind == "gather":
            self.out_shape = [jax.ShapeDtypeStruct((N_DEV,) + a.shape, a.dtype) for a in arrs]
        elif kind == "exchange":
            self.out_shape = [jax.ShapeDtypeStruct((N_DEV - 1,) + a.shape[1:], a.dtype) for a in arrs]
        elif kind == "pair":
            self.out_shape = [jax.ShapeDtypeStruct((N_DEV // 2,) + a.shape[1:], a.dtype) for a in arrs]
        else:
            self.out_shape = [jax.ShapeDtypeStruct((3,) + a.shape[1:], a.dtype) for a in arrs]
        self.scratch = [pltpu.SemaphoreType.DMA((7 * self.n,)), pltpu.SemaphoreType.DMA((7 * self.n,))]
        if kind == "gather":
            self.scratch.append(pltpu.SemaphoreType.DMA((self.n,)))

    def _exchange_copies(self, in_refs, out_refs, sems):
        me = _my_pos()
        x, y, c = me
        cps = []
        for ai in range(self.n):
            if self.kind == "exchange":
                todo = [(in_refs[ai].at[_slot(_peer(me, r))], out_refs[ai].at[r - 1], _peer(me, r), r - 1)
                        for r in range(1, N_DEV)]
            elif self.kind == "pair":
                todo = [(in_refs[ai].at[2 * q + 1 - c], out_refs[ai].at[q], (x, y, 1 - c), q)
                        for q in range(N_DEV // 2)]
            else:
                chips = [(1 - x, y), (x, 1 - y), (1 - x, 1 - y)]
                todo = [(in_refs[ai].at[2 * cx + cy], out_refs[ai].at[j], (cx, cy, c), j)
                        for j, (cx, cy) in enumerate(chips)]
            for src, dst, to, k in todo:
                cps.append(pltpu.make_async_remote_copy(
                    src_ref=src, dst_ref=dst, send_sem=sems[0].at[ai * 7 + k], recv_sem=sems[1].at[ai * 7 + k],
                    device_id=to, device_id_type=MESH))
        return cps

    def _gather_parts(self, in_refs, out_refs, sems):
        x, y, c = _my_pos()
        me, sibling = (x, y, c), (x, y, 1 - c)
        chips = [(1 - x, y), (x, 1 - y), (1 - x, 1 - y)]

        def copy(ai, k, block, to, src=None):
            dst = out_refs[ai].at[_slot(block)]
            return pltpu.make_async_remote_copy(
                src_ref=dst if src is None else src, dst_ref=dst, send_sem=sems[0].at[ai * 7 + k],
                recv_sem=sems[1].at[ai * 7 + k], device_id=to, device_id_type=MESH)

        mine = [pltpu.make_async_copy(in_refs[ai], out_refs[ai].at[_slot(me)], sems[2].at[ai])
                for ai in range(self.n)]
        first = []
        for ai in range(self.n):
            first.append(copy(ai, 0, me, sibling, src=in_refs[ai]))
            first += [copy(ai, 1 + j, me, (*chip, c), src=in_refs[ai]) for j, chip in enumerate(chips)]
        return me, sibling, chips, c, copy, mine, first

    def start(self, in_refs, out_refs, sems):
        if self.kind != "gather":
            for cp in self._exchange_copies(in_refs, out_refs, sems):
                cp.start()
            return
        _, _, _, _, _, mine, first = self._gather_parts(in_refs, out_refs, sems)
        for cp in mine + first:
            cp.start()

    def wait(self, in_refs, out_refs, sems):
        if self.kind != "gather":
            for cp in self._exchange_copies(in_refs, out_refs, sems):
                cp.wait()
            return
        me, sibling, chips, c, copy, mine, first = self._gather_parts(in_refs, out_refs, sems)
        passed = []
        for ai in range(self.n):
            for j, chip in enumerate(chips):
                copy(ai, 1 + j, (*chip, c), me).wait_recv()
                fwd = copy(ai, 4 + j, (*chip, c), sibling)
                fwd.start()
                passed.append(fwd)
        for ai in range(self.n):
            copy(ai, 0, sibling, me).wait_recv()
            for j, chip in enumerate(chips):
                copy(ai, 4 + j, (*chip, 1 - c), me).wait_recv()
        for cp in first + passed:
            cp.wait_send()
        for cp in mine:
            cp.wait()


def _hosted_call(body, name, comm, first, last, *, args, in_specs, out_shape, out_specs, scratch, grid, sem):
    if comm is None:
        return pl.pallas_call(body, name=name, out_shape=out_shape, grid=grid, in_specs=in_specs, out_specs=out_specs,
                              scratch_shapes=scratch, compiler_params=_cparams(sem))(*args)
    ni, no, ns, nc = len(args), len(out_shape), len(scratch), comm.n
    hbm = pl.BlockSpec(memory_space=pl.ANY)

    def hosted(*refs):
        ins, cin = refs[:ni], refs[ni:ni + nc]
        outs, cout = refs[ni + nc:ni + nc + no], refs[ni + nc + no:ni + 2 * nc + no]
        scr, sems = refs[ni + 2 * nc + no:ni + 2 * nc + no + ns], refs[ni + 2 * nc + no + ns:]

        @pl.when(first())
        def _():
            comm.start(cin, cout, sems)

        body(*ins, *outs, *scr)

        @pl.when(last())
        def _():
            comm.wait(cin, cout, sems)

    res = pl.pallas_call(
        hosted, name=name, out_shape=list(out_shape) + comm.out_shape, grid=grid,
        in_specs=list(in_specs) + [hbm] * nc, out_specs=list(out_specs) + [hbm] * nc,
        scratch_shapes=list(scratch) + comm.scratch,
        compiler_params=_cparams(("arbitrary",) * len(grid)))(*args, *comm.arrs)
    return res[:no], res[no:]


_DIMS = {"nn": ((1,), (0,)), "nt": ((1,), (1,)), "tn": ((0,), (0,))}


def _matmul(name, a, b, mode, out_dtypes, epi=None, extras=(), comm=None, out_slabs=False):
    bs = list(b) if isinstance(b, (tuple, list)) else [b]
    if mode == "nn":
        (m, k), n = a.shape, b.shape[1]
    elif mode == "nt":
        (m, k), n = a.shape, sum(x.shape[0] for x in bs)
    else:
        (k, m), n = a.shape, b.shape[1]
    tm = _tile(m, (1024, 512, 256, 128, 64, 32, 16, 8))
    if k <= MAX_FULL_K:
        tk, tn = k, _tile(n // N_DEV if out_slabs else n // len(bs), (512, 256, 128))
    else:
        tk, tn = _tile(k, (2048, 1024, 512, 256, 128)), _tile(n // len(bs), (1024, 512, 256, 128))
    nk = k // tk
    gm, gn = m // tm, n // tn
    a_spec = pl.BlockSpec((tk, tm), lambda i, j, q: (q, i)) if mode == "tn" else pl.BlockSpec((tm, tk), lambda i, j, q: (i, q))
    b_spec = pl.BlockSpec((tn, tk), lambda i, j, q: (j, q)) if mode == "nt" else pl.BlockSpec((tk, tn), lambda i, j, q: (q, j))
    gh = gn // 2
    b_specs = [b_spec] if len(bs) == 1 else [
        pl.BlockSpec((tn, tk), lambda i, j, q: (jnp.minimum(j, gh - 1), q)),
        pl.BlockSpec((tn, tk), lambda i, j, q: (jnp.maximum(j - gh, 0), q))]
    mn_spec = pl.BlockSpec((tm, tn), lambda i, j, q: (i, j))
    per = n // N_DEV // tn if out_slabs else 0
    out_spec = pl.BlockSpec((None, tm, tn), lambda i, j, q: (j // per, i, j % per)) if out_slabs else mn_spec
    ne, no = len(extras), len(out_dtypes)
    dims = (_DIMS[mode], ((), ()))
    keep_t = mode == "tn" and nk == 1 and gn > 1

    def finish(r, extra_refs, out_refs):
        outs = (r,) if epi is None else epi(r, *[e[...] for e in extra_refs])
        for o_ref, o in zip(out_refs, outs):
            o_ref[...] = o.astype(o_ref.dtype)

    def body(a_ref, *rest):
        if len(bs) == 1:
            return step(a_ref, *rest)

        @pl.when(pl.program_id(1) < gh)
        def _():
            step(a_ref, rest[0], *rest[2:])

        @pl.when(pl.program_id(1) >= gh)
        def _():
            step(a_ref, rest[1], *rest[2:])

    def step(a_ref, b_ref, *rest):
        extra_refs, out_refs = rest[:ne], rest[ne:ne + no]
        if keep_t:
            at = rest[ne + no]

            @pl.when(pl.program_id(1) == 0)
            def _():
                at[...] = a_ref[...].T

            part = jnp.dot(at[...], b_ref[...], preferred_element_type=F32)
        else:
            part = lax.dot_general(a_ref[...], b_ref[...], dims, preferred_element_type=F32)
        if nk == 1:
            finish(part, extra_refs, out_refs)
            return
        acc = rest[ne + no]
        q = pl.program_id(2)

        @pl.when(q == 0)
        def _():
            acc[...] = part

        @pl.when(jnp.logical_and(q > 0, q < nk - 1))
        def _():
            acc[...] += part

        @pl.when(q == nk - 1)
        def _():
            finish(acc[...] + part, extra_refs, out_refs)

    def first():
        return jnp.logical_and(jnp.logical_and(pl.program_id(0) == 0, pl.program_id(1) == 0), pl.program_id(2) == 0)

    def last():
        return jnp.logical_and(jnp.logical_and(pl.program_id(0) == gm - 1, pl.program_id(1) == gn - 1),
                               pl.program_id(2) == nk - 1)

    return _hosted_call(
        body, name, comm, first, last,
        args=[a, *bs, *extras], in_specs=[a_spec] + b_specs + [mn_spec] * ne,
        out_shape=[jax.ShapeDtypeStruct((N_DEV, m, n // N_DEV) if out_slabs else (m, n), dt) for dt in out_dtypes],
        out_specs=[out_spec] * no,
        scratch=[pltpu.VMEM((tm, tn), F32)] if nk > 1 else ([pltpu.VMEM((tm, tk), a.dtype)] if keep_t else []),
        grid=(gm, gn, nk), sem=("parallel", "arbitrary" if keep_t else "parallel", "arbitrary"))


@jax.custom_vjp
def _mm(a, w):
    return jnp.dot(a.astype(BF16), w.astype(BF16), preferred_element_type=F32)


def _mm_fwd(a, w):
    return _mm(a, w), (a, w)


def _mm_bwd(res, ct):
    a, w = res
    ctb = ct.astype(BF16)
    da = lax.dot_general(ctb, w.astype(BF16), (((1,), (1,)), ((), ())), preferred_element_type=F32)
    dw = lax.dot_general(a.astype(BF16), ctb, (((0,), (0,)), ((), ())), preferred_element_type=F32)
    return da, dw


_mm.defvjp(_mm_fwd, _mm_bwd)


def _split3(x):
    hi = x.astype(BF16)
    r1 = x - hi.astype(F32)
    mid = r1.astype(BF16)
    lo = (r1 - mid.astype(F32)).astype(BF16)
    return hi, mid, lo


def _head_ones(width):
    r = lax.broadcasted_iota(jnp.int32, (width, width), 0) // HEAD
    c = lax.broadcasted_iota(jnp.int32, (width, width), 1) // HEAD
    return (r == c).astype(BF16)


@jax.custom_vjp
def _segsum(x):
    ones = _head_ones(x.shape[-1])
    out = None
    for piece in _split3(x):
        t = jnp.dot(piece, ones, preferred_element_type=F32)
        out = t if out is None else out + t
    return out


_segsum.defvjp(lambda x: (_segsum(x), None), lambda _, ct: (_segsum(ct),))


def _softplus(z):
    return jnp.maximum(z, 0.0) + jnp.log(1.0 + jnp.exp(-jnp.abs(z)))


def _sigmoid(z):
    return 1.0 / (1.0 + jnp.exp(-z))


def _rms(x, w):
    ms = jnp.mean(x * x, axis=-1, keepdims=True)
    return x * lax.rsqrt(ms + RMS_EPS) * w


def _row(ref, i):
    return ref[pl.ds(i, 1), :]


def _shift_down(x, prev_ref, n, first):
    rolled = pltpu.roll(x, n, 0)
    rows = lax.broadcasted_iota(jnp.int32, x.shape, 0)
    for q in range(n):
        halo = jnp.where(first, 0.0, _row(prev_ref, SUBLANES - n + q))
        rolled = jnp.where(rows == q, halo, rolled)
    return rolled


def _shift_up(x, next_ref, n, last):
    t = x.shape[0]
    rolled = pltpu.roll(x, t - n, 0)
    rows = lax.broadcasted_iota(jnp.int32, x.shape, 0)
    for q in range(n):
        halo = jnp.where(last, 0.0, _row(next_ref, q))
        rolled = jnp.where(rows == t - n + q, halo, rolled)
    return rolled


def _acc_out(ref, val, first):
    @pl.when(first)
    def _():
        ref[...] = val

    @pl.when(jnp.logical_not(first))
    def _():
        ref[...] += val


def _prep_fn(k, plm, w0, a0, kkw, kaw, wd, wi, wg):
    w_log = -_softplus(-(w0 + _mm(jnp.tanh(plm), wd))) - 0.5
    lw = -jnp.exp(w_log)
    a_g = _sigmoid(a0 + _mm(plm, wi))
    g = _mm(_sigmoid(plm), wg)
    kk = k * kkw
    kk = kk / jnp.maximum(jnp.sqrt(_segsum(kk * kk)), L2_EPS)
    k2 = k * (1.0 + (a_g - 1.0) * kaw)
    return lw, k2, -kk, kk * a_g, g


def _post_fn(y, r, k2, v, g, lnw, lnb, rk):
    mu = _segsum(y) * (1.0 / HEAD)
    yc = y - mu
    var = _segsum(yc * yc) * (1.0 / HEAD)
    yn = yc * lax.rsqrt(var + LNX_EPS) * lnw + lnb
    bonus = _segsum(r * k2 * rk) * v
    return (yn + bonus) * g


def _merge_fn(pga, pgb, ba, bb, ya, yb):
    return _sigmoid(pga + ba) * ya + _sigmoid(pgb + bb) * yb


_NN, _NT, _TN = ((2,), (1,)), ((2,), (2,)), ((1,), (1,))


def _dot3(a, b, dims):
    ah = a.astype(BF16)
    al = (a - ah.astype(F32)).astype(BF16)
    bh = b.astype(BF16)
    bl = (b - bh.astype(F32)).astype(BF16)
    dg = lambda p, q: lax.dot_general(p, q, (dims, ((0,), (0,))), preferred_element_type=F32)
    (ca,), (cb_,) = dims
    if a.shape[ca] % LANES:
        return dg(ah, bh) + (dg(ah, bl) + dg(al, bh))
    cross = dg(jnp.concatenate([ah, al], axis=ca), jnp.concatenate([bl, bh], axis=cb_))
    return dg(ah, bh) + cross


@jax.custom_vjp
def _dnn(a, b):
    return _dot3(a, b, _NN)


@jax.custom_vjp
def _dnt(a, b):
    return _dot3(a, b, _NT)


@jax.custom_vjp
def _dtn(a, b):
    return _dot3(a, b, _TN)


_dnn.defvjp(lambda a, b: (_dnn(a, b), (a, b)), lambda res, ct: (_dnt(ct, res[1]), _dtn(res[0], ct)))
_dnt.defvjp(lambda a, b: (_dnt(a, b), (a, b)), lambda res, ct: (_dnn(ct, res[1]), _dtn(ct, res[0])))
_dtn.defvjp(lambda a, b: (_dtn(a, b), (a, b)), lambda res, ct: (_dnt(res[1], ct), _dnn(res[0], ct)))


@jax.custom_vjp
def _unit_lower_inverse(low):
    n = low.shape[-1]
    ri = lax.broadcasted_iota(jnp.int32, low.shape, 1)
    ci = lax.broadcasted_iota(jnp.int32, low.shape, 2)
    inv = (ri == ci).astype(F32) + low
    pw = low
    for _ in range(int(math.log2(n // 2)) - 1):
        pw = _dnn(pw, pw)
        inv = inv + _dnn(inv, pw)
    return inv


def _unit_lower_inverse_bwd(inv, ct):
    return (_dnt(_dtn(inv, ct), inv),)


_unit_lower_inverse.defvjp(lambda low: (_unit_lower_inverse(low),) * 2, _unit_lower_inverse_bwd)


def _chunk_fn(s, r, lw, k, v, a, b):
    np_, c = r.shape[0], r.shape[1]
    c2 = 2 * c
    ri = lax.broadcasted_iota(jnp.int32, (np_, c, c), 1)
    ci = lax.broadcasted_iota(jnp.int32, (np_, c, c), 2)
    tri = (ri >= ci).astype(F32)
    cum = _dnn(tri, lw)
    tot = jnp.sum(lw, axis=1, keepdims=True)
    g_in, g_inv, g_out = jnp.exp(cum), jnp.exp(-cum), jnp.exp(tot - cum)
    lane_head = lax.broadcasted_iota(jnp.int32, (1, 2, 1, LANES), 3) // HEAD
    which = lax.broadcasted_iota(jnp.int32, (1, 2, 1, LANES), 1)
    hmask = (lane_head == which).astype(F32)

    def st(x):
        return (x[:, None] * hmask).reshape(np_, c2, LANES)

    r2, a2 = st(r * g_in), st(a * jnp.exp(cum - lw))
    b2, k2, v2 = st(b * g_inv), st(k * g_inv), st(v)
    bo2, ko2 = st(b * g_out), st(k * g_out)
    r2i = lax.broadcasted_iota(jnp.int32, (np_, c2, c2), 1)
    c2i = lax.broadcasted_iota(jnp.int32, (np_, c2, c2), 2)
    same = (r2i >= c) == (c2i >= c)
    strict = jnp.logical_and(same, r2i > c2i)
    incl = jnp.logical_and(same, r2i >= c2i)
    lab = jnp.where(strict, _dnt(a2, b2), 0.0)
    lak = jnp.where(strict, _dnt(a2, k2), 0.0)
    mrb = jnp.where(incl, _dnt(r2, b2), 0.0)
    mrk = jnp.where(incl, _dnt(r2, k2), 0.0)
    x2 = _dnt(a2, s) + _dnn(lak, v2)
    u2 = _dnn(_unit_lower_inverse(lab), x2)
    y2 = _dnt(r2, s) + _dnn(mrb, u2) + _dnn(mrk, v2)
    y = jnp.sum(y2.reshape(np_, 2, c, LANES), axis=1)
    s_new = s * jnp.exp(tot) + _dtn(u2, bo2) + _dtn(v2, ko2)
    return y, s_new


def _norm_fwd(name, x, add, w, want_sum):
    t, d = x.shape
    tt = _tile(t, (128, 64, 32, 16, 8))
    row = pl.BlockSpec((tt, d), lambda i: (i, 0))
    par = pl.BlockSpec((1, d), lambda i: (0, 0))
    has_add = add is not None

    def body(*refs):
        x_ref = refs[0]
        add_ref = refs[1] if has_add else None
        w_ref = refs[1 + has_add]
        outs = refs[2 + has_add:]
        h = x_ref[...] + add_ref[...] if has_add else x_ref[...]
        if want_sum:
            outs[0][...] = h
        outs[-1][...] = _rms(h, w_ref[...]).astype(BF16)

    out_shape = ([jax.ShapeDtypeStruct((t, d), F32)] if want_sum else []) + [jax.ShapeDtypeStruct((t, d), BF16)]
    return pl.pallas_call(
        body, name=name, out_shape=out_shape, grid=(t // tt,),
        in_specs=[row] + ([row] if has_add else []) + [par],
        out_specs=[row] * len(out_shape),
        compiler_params=_cparams(("arbitrary",)),
    )(*([x] + ([add] if has_add else []) + [w]))


def _norm_bwd(name, xin, dy, dres, w):
    t, d = xin.shape
    tt = _tile(t, (128, 64, 32, 16, 8))
    row = pl.BlockSpec((tt, d), lambda i: (i, 0))
    par = pl.BlockSpec((1, d), lambda i: (0, 0))

    def body(x_ref, dy_ref, dres_ref, w_ref, dx_ref, dxb_ref, dw_ref):
        _, vjp = jax.vjp(_rms, x_ref[...], w_ref[...])
        dx, dw = vjp(dy_ref[...])
        dx = dx + dres_ref[...]
        dx_ref[...] = dx
        dxb_ref[...] = dx.astype(BF16)
        _acc_out(dw_ref, dw, pl.program_id(0) == 0)

    return pl.pallas_call(
        body, name=name,
        out_shape=[jax.ShapeDtypeStruct((t, d), F32), jax.ShapeDtypeStruct((t, d), BF16),
                   jax.ShapeDtypeStruct((1, d), F32)],
        grid=(t // tt,), in_specs=[row, row, row, par], out_specs=[row, row, par],
        compiler_params=_cparams(("arbitrary",)),
    )(xin, dy, dres, w)


def _final(name, h1, md, target, w):
    t, d = h1.shape
    tt = _tile(t, (128, 64, 32, 16, 8))
    row = pl.BlockSpec((tt, d), lambda i: (i, 0))
    par = pl.BlockSpec((1, d), lambda i: (0, 0))
    one = pl.BlockSpec((1, LANES), lambda i: (0, 0))

    def body(h1_ref, md_ref, tg_ref, w_ref, loss_ref, dh_ref, dhb_ref, dw_ref):
        tg = tg_ref[...]

        def f(h, wv):
            err = _rms(h, wv) - tg
            return 0.5 * jnp.sum(jnp.mean(err * err, axis=-1, keepdims=True), axis=0, keepdims=True)

        loss, vjp = jax.vjp(f, h1_ref[...] + md_ref[...], w_ref[...])
        dh, dw = vjp(jnp.ones((1, 1), F32))
        dh_ref[...] = dh
        dhb_ref[...] = dh.astype(BF16)
        first = pl.program_id(0) == 0
        _acc_out(dw_ref, dw, first)
        _acc_out(loss_ref, jnp.broadcast_to(loss, (1, LANES)), first)

    return pl.pallas_call(
        body, name=name,
        out_shape=[jax.ShapeDtypeStruct((1, LANES), F32), jax.ShapeDtypeStruct((t, d), F32),
                   jax.ShapeDtypeStruct((t, d), BF16), jax.ShapeDtypeStruct((1, d), F32)],
        grid=(t // tt,), in_specs=[row, row, row, par], out_specs=[one, row, row, par],
        compiler_params=_cparams(("arbitrary",)),
    )(h1, md, target, w)


def _halo_specs(tt, cb, nrow8, col_of):
    prev = pl.BlockSpec((SUBLANES, cb), lambda i, j: (jnp.maximum(i * (tt // SUBLANES) - 1, 0), col_of(j)))
    nxt = pl.BlockSpec((SUBLANES, cb), lambda i, j: (jnp.minimum((i + 1) * (tt // SUBLANES), nrow8 - 1), col_of(j)))
    return prev, nxt


def _mix_fwd(name, p_all, mu, width, cb):
    t = p_all.shape[0]
    tt = _tile(t, LIGHT_ROWS)
    main = pl.BlockSpec((tt, cb), lambda i, j: (i, j))
    prev, _ = _halo_specs(tt, cb, t // SUBLANES, lambda j: j)
    par = pl.BlockSpec((1, cb), lambda i, j: (0, j))

    def body(p_ref, prev_ref, mu_ref, o_ref):
        p = p_ref[...]
        o_ref[...] = p + (_shift_down(p, prev_ref, 1, pl.program_id(0) == 0) - p) * mu_ref[...]

    return pl.pallas_call(
        body, name=name, out_shape=jax.ShapeDtypeStruct((t, width), F32),
        grid=(t // tt, width // cb), in_specs=[main, prev, par], out_specs=main,
        compiler_params=_cparams(("arbitrary", "arbitrary")),
    )(p_all, p_all, mu)


def _mix_bwd(name, dpm_list, p_all, col0, mu, cb):
    t, width = dpm_list[0].shape
    tt = _tile(t, LIGHT_ROWS)
    n8 = t // SUBLANES
    nl = len(dpm_list)
    main = pl.BlockSpec((tt, cb), lambda j, i: (i, j))
    nxt = pl.BlockSpec((SUBLANES, cb), lambda j, i: (jnp.minimum((i + 1) * (tt // SUBLANES), n8 - 1), j))
    p_main = pl.BlockSpec((tt, cb), lambda j, i: (i, col0 + j))
    p_prev = pl.BlockSpec((SUBLANES, cb), lambda j, i: (jnp.maximum(i * (tt // SUBLANES) - 1, 0), col0 + j))
    par = pl.BlockSpec((1, cb), lambda j, i: (0, j))
    nt_ = t // tt

    def body(*refs):
        d_refs, dn_refs = refs[:nl], refs[nl:2 * nl]
        p_ref, pp_ref, mu_ref, dp_ref, dmu_ref, nx_scr = refs[2 * nl:]
        i = pl.program_id(1)
        dpm = d_refs[0][...]
        nx = dn_refs[0][...]
        for q in range(1, nl):
            dpm = dpm + d_refs[q][...]
            nx = nx + dn_refs[q][...]
        nx_scr[...] = nx
        mu_v = mu_ref[...]
        up = _shift_up(dpm, nx_scr, 1, i == nt_ - 1)
        dp_ref[...] = (dpm * (1.0 - mu_v) + up * mu_v).astype(BF16)
        p = p_ref[...]
        diff = _shift_down(p, pp_ref, 1, i == 0) - p
        _acc_out(dmu_ref, jnp.sum(dpm * diff, axis=0, keepdims=True), i == 0)

    return pl.pallas_call(
        body, name=name,
        out_shape=[jax.ShapeDtypeStruct((t, width), BF16), jax.ShapeDtypeStruct((1, width), F32)],
        grid=(width // cb, nt_),
        in_specs=[main] * nl + [nxt] * nl + [p_main, p_prev, par],
        out_specs=[main, par],
        scratch_shapes=[pltpu.VMEM((SUBLANES, cb), F32)],
        compiler_params=_cparams(("arbitrary", "arbitrary")),
    )(*dpm_list, *dpm_list, p_all, p_all, mu)


def _prep_fwd(name, pm, cfg, w0, a0, kkw, kaw, wd, wi, wg):
    t = pm.shape[0]
    dr, lp, cb = cfg["dr"], cfg["lp"], cfg["cb"]
    tt = _tile(t, LIGHT_ROWS)
    nj = dr // cb
    kspec = pl.BlockSpec((tt, cb), lambda j, i: (i, nj + j))
    lspec = pl.BlockSpec((tt, lp), lambda j, i: (i, 3 * dr // lp))
    par = pl.BlockSpec((1, cb), lambda j, i: (0, j))
    wspec = pl.BlockSpec((lp, cb), lambda j, i: (0, j))
    out = pl.BlockSpec((tt, cb), lambda j, i: (i, j))

    def body(k_ref, l_ref, w0_ref, a0_ref, kk_ref, ka_ref, wd_ref, wi_ref, wg_ref, *outs):
        vals = _prep_fn(k_ref[...], l_ref[...], w0_ref[...], a0_ref[...], kk_ref[...], ka_ref[...],
                        wd_ref[...], wi_ref[...], wg_ref[...])
        for o_ref, val in zip(outs, vals):
            o_ref[...] = val

    return pl.pallas_call(
        body, name=name, out_shape=[jax.ShapeDtypeStruct((t, dr), F32)] * 5,
        grid=(nj, t // tt), in_specs=[kspec, lspec, par, par, par, par, wspec, wspec, wspec],
        out_specs=[out] * 5, compiler_params=_cparams(("arbitrary", "arbitrary")),
    )(pm, pm, w0, a0, kkw, kaw, wd, wi, wg)


def _prep_bwd(name, pm, cfg, w0, a0, kkw, kaw, wd, wi, wg, cts):
    t = pm.shape[0]
    dr, lp, cb = cfg["dr"], cfg["lp"], cfg["cb"]
    tt = _tile(t, (256, 128, 64, 32, 16, 8))
    nj = dr // cb
    kspec = pl.BlockSpec((tt, cb), lambda j, i: (i, nj + j))
    lspec = pl.BlockSpec((tt, lp), lambda j, i: (i, 3 * dr // lp))
    par = pl.BlockSpec((1, cb), lambda j, i: (0, j))
    wspec = pl.BlockSpec((lp, cb), lambda j, i: (0, j))
    blk = pl.BlockSpec((tt, cb), lambda j, i: (i, j))
    dpl_spec = pl.BlockSpec((None, tt, lp), lambda j, i: (j, i, 0))

    def body(k_ref, l_ref, w0_ref, a0_ref, kk_ref, ka_ref, wd_ref, wi_ref, wg_ref,
             dlw_ref, dk2a_ref, dk2b_ref, da_ref, db_ref, dg_ref,
             dpk_ref, dpl_ref, dw0_ref, da0_ref, dkk_ref, dka_ref, dwd_ref, dwi_ref, dwg_ref):
        _, vjp = jax.vjp(_prep_fn, k_ref[...], l_ref[...], w0_ref[...], a0_ref[...], kk_ref[...], ka_ref[...],
                         wd_ref[...], wi_ref[...], wg_ref[...])
        dk, dpl, dw0, da0, dkk, dka, dwd, dwi, dwg = vjp(
            (dlw_ref[...], dk2a_ref[...] + dk2b_ref[...], da_ref[...], db_ref[...], dg_ref[...]))
        dpk_ref[...] = dk
        dpl_ref[...] = dpl
        first = pl.program_id(1) == 0
        for ref, val in ((dw0_ref, dw0), (da0_ref, da0), (dkk_ref, dkk), (dka_ref, dka),
                         (dwd_ref, dwd), (dwi_ref, dwi), (dwg_ref, dwg)):
            _acc_out(ref, val, first)

    out_shape = ([jax.ShapeDtypeStruct((t, dr), F32), jax.ShapeDtypeStruct((nj, t, lp), F32)]
                 + [jax.ShapeDtypeStruct((1, dr), F32)] * 4 + [jax.ShapeDtypeStruct((lp, dr), F32)] * 3)
    return pl.pallas_call(
        body, name=name, out_shape=out_shape, grid=(nj, t // tt),
        in_specs=[kspec, lspec, par, par, par, par, wspec, wspec, wspec] + [blk] * 6,
        out_specs=[blk, dpl_spec] + [par] * 4 + [wspec] * 3,
        compiler_params=_cparams(("arbitrary", "arbitrary")),
    )(pm, pm, w0, a0, kkw, kaw, wd, wi, wg, *cts)


def _post_specs(t, cfg):
    dr, cb = cfg["dr"], cfg["cb"]
    tt = _tile(t, (256, 128, 64, 32, 16, 8))
    nj = dr // cb
    blk = pl.BlockSpec((tt, cb), lambda j, i: (i, j))
    rspec = pl.BlockSpec((tt, cb), lambda j, i: (i, j))
    vspec = pl.BlockSpec((tt, cb), lambda j, i: (i, 2 * nj + j))
    par = pl.BlockSpec((1, cb), lambda j, i: (0, j))
    return tt, nj, blk, rspec, vspec, par


def _post_fwd(name, y, pm, k2, g, lnw, lnb, rk, cfg):
    t = y.shape[0]
    tt, nj, blk, rspec, vspec, par = _post_specs(t, cfg)

    def body(y_ref, r_ref, k_ref, v_ref, g_ref, lw_ref, lb_ref, rk_ref, o_ref):
        o_ref[...] = _post_fn(y_ref[...], r_ref[...], k_ref[...], v_ref[...], g_ref[...],
                              lw_ref[...], lb_ref[...], rk_ref[...]).astype(BF16)

    return pl.pallas_call(
        body, name=name, out_shape=jax.ShapeDtypeStruct((t, cfg["dr"]), BF16), grid=(nj, t // tt),
        in_specs=[blk, rspec, blk, vspec, blk, par, par, par], out_specs=blk,
        compiler_params=_cparams(("arbitrary", "arbitrary")),
    )(y, pm, k2, pm, g, lnw, lnb, rk)


def _post_bwd(name, y, pm, k2, g, lnw, lnb, rk, dout, cfg):
    t = y.shape[0]
    tt, nj, blk, rspec, vspec, par = _post_specs(t, cfg)

    def body(y_ref, r_ref, k_ref, v_ref, g_ref, lw_ref, lb_ref, rk_ref, do_ref,
             dy_ref, dr_ref, dk_ref, dv_ref, dg_ref, dlw_ref, dlb_ref, drk_ref):
        _, vjp = jax.vjp(_post_fn, y_ref[...], r_ref[...], k_ref[...], v_ref[...], g_ref[...],
                         lw_ref[...], lb_ref[...], rk_ref[...])
        dy, dr, dk, dv, dg, dlw, dlb, drk = vjp(do_ref[...])
        for ref, val in ((dy_ref, dy), (dr_ref, dr), (dk_ref, dk), (dv_ref, dv), (dg_ref, dg)):
            ref[...] = val
        first = pl.program_id(1) == 0
        for ref, val in ((dlw_ref, dlw), (dlb_ref, dlb), (drk_ref, drk)):
            _acc_out(ref, val, first)

    dr = cfg["dr"]
    return pl.pallas_call(
        body, name=name,
        out_shape=[jax.ShapeDtypeStruct((t, dr), F32)] * 5 + [jax.ShapeDtypeStruct((1, dr), F32)] * 3,
        grid=(nj, t // tt),
        in_specs=[blk, rspec, blk, vspec, blk, par, par, par, blk],
        out_specs=[blk] * 5 + [par] * 3,
        compiler_params=_cparams(("arbitrary", "arbitrary")),
    )(y, pm, k2, pm, g, lnw, lnb, rk, dout)


def _conv_specs(t, cfg):
    dc, cb = cfg["dc"], cfg["cb"]
    tt = _tile(t, LIGHT_ROWS)
    nj = dc // cb
    c0 = cfg["off_conv"] // cb
    n8 = t // SUBLANES

    def sect(s):
        col = lambda j: c0 + s * nj + j
        main = pl.BlockSpec((tt, cb), lambda j, i: (i, col(j)))
        prev = pl.BlockSpec((SUBLANES, cb), lambda j, i: (jnp.maximum(i * (tt // SUBLANES) - 1, 0), col(j)))
        nxt = pl.BlockSpec((SUBLANES, cb), lambda j, i: (jnp.minimum((i + 1) * (tt // SUBLANES), n8 - 1), col(j)))
        return main, prev, nxt

    blk = pl.BlockSpec((tt, cb), lambda j, i: (i, j))
    wspec = pl.BlockSpec((SUBLANES, cb), lambda j, i: (0, j))
    return tt, nj, n8, sect, blk, wspec


def _conv_fwd(name, p_all, cw8, cfg):
    t = p_all.shape[0]
    tt, nj, n8, sect, blk, wspec = _conv_specs(t, cfg)
    (bm, _, _), (cm, cp, _), (um, up, _) = sect(0), sect(1), sect(2)

    def body(b_ref, c_ref, cp_ref, u_ref, up_ref, w_ref, o_ref, zp_scr):
        first = pl.program_id(1) == 0
        z = c_ref[...] * u_ref[...]
        zp_scr[...] = cp_ref[...] * up_ref[...]
        o = _row(w_ref, 2) * z + _row(w_ref, 1) * _shift_down(z, zp_scr, 1, first) \
            + _row(w_ref, 0) * _shift_down(z, zp_scr, 2, first)
        o_ref[...] = (b_ref[...] * o).astype(BF16)

    return pl.pallas_call(
        body, name=name, out_shape=jax.ShapeDtypeStruct((t, cfg["dc"]), BF16), grid=(nj, t // tt),
        in_specs=[bm, cm, cp, um, up, wspec], out_specs=blk,
        scratch_shapes=[pltpu.VMEM((SUBLANES, blk.block_shape[1]), F32)],
        compiler_params=_cparams(("arbitrary", "arbitrary")),
    )(p_all, p_all, p_all, p_all, p_all, cw8)


def _conv_bwd(name, p_all, cw8, dyb, cfg):
    t = p_all.shape[0]
    tt, nj, n8, sect, blk, wspec = _conv_specs(t, cfg)
    (bm, _, bn), (cm, cp, _), (um, up, _) = sect(0), sect(1), sect(2)
    cb = blk.block_shape[1]
    dnxt = pl.BlockSpec((SUBLANES, cb), lambda j, i: (jnp.minimum((i + 1) * (tt // SUBLANES), n8 - 1), j))
    nt_ = t // tt

    def body(b_ref, bn_ref, c_ref, cp_ref, u_ref, up_ref, w_ref, d_ref, dn_ref,
             db_ref, dc_ref, du_ref, dw_ref, zp_scr, don_scr):
        i = pl.program_id(1)
        first, last = i == 0, i == nt_ - 1
        c, u, b, dy = c_ref[...], u_ref[...], b_ref[...], d_ref[...]
        z = c * u
        zp_scr[...] = cp_ref[...] * up_ref[...]
        z1 = _shift_down(z, zp_scr, 1, first)
        z2 = _shift_down(z, zp_scr, 2, first)
        w0, w1, w2 = _row(w_ref, 0), _row(w_ref, 1), _row(w_ref, 2)
        o = w2 * z + w1 * z1 + w0 * z2
        do = dy * b
        don_scr[...] = dn_ref[...] * bn_ref[...]
        dz = w2 * do + w1 * _shift_up(do, don_scr, 1, last) + w0 * _shift_up(do, don_scr, 2, last)
        db_ref[...] = (dy * o).astype(BF16)
        dc_ref[...] = (dz * u).astype(BF16)
        du_ref[...] = (dz * c).astype(BF16)
        rows = lax.broadcasted_iota(jnp.int32, (SUBLANES, cb), 0)
        s0 = jnp.sum(do * z2, axis=0, keepdims=True)
        s1 = jnp.sum(do * z1, axis=0, keepdims=True)
        s2 = jnp.sum(do * z, axis=0, keepdims=True)
        dw = jnp.where(rows == 0, s0, jnp.where(rows == 1, s1, jnp.where(rows == 2, s2, 0.0)))
        _acc_out(dw_ref, dw, first)

    dc = cfg["dc"]
    return pl.pallas_call(
        body, name=name,
        out_shape=[jax.ShapeDtypeStruct((t, dc), BF16)] * 3 + [jax.ShapeDtypeStruct((SUBLANES, dc), F32)],
        grid=(nj, nt_),
        in_specs=[bm, bn, cm, cp, um, up, wspec, blk, dnxt],
        out_specs=[blk] * 3 + [wspec],
        scratch_shapes=[pltpu.VMEM((SUBLANES, cb), F32), pltpu.VMEM((SUBLANES, cb), F32)],
        compiler_params=_cparams(("arbitrary", "arbitrary")),
    )(p_all, p_all, p_all, p_all, p_all, p_all, cw8, dyb, dyb)


def _merge_specs(t, cfg):
    d, cb = cfg["d"], cfg["cb"]
    tt = _tile(t, LIGHT_ROWS)
    nj = d // cb
    g0 = cfg["off_gate"] // cb
    ga = pl.BlockSpec((tt, cb), lambda j, i: (i, g0 + j))
    gb = pl.BlockSpec((tt, cb), lambda j, i: (i, g0 + nj + j))
    ba = pl.BlockSpec((1, cb), lambda j, i: (0, j))
    bb = pl.BlockSpec((1, cb), lambda j, i: (0, nj + j))
    blk = pl.BlockSpec((tt, cb), lambda j, i: (i, j))
    return tt, nj, ga, gb, ba, bb, blk


def _merge_fwd(name, p_all, bias, ya, yb, cfg):
    t = p_all.shape[0]
    tt, nj, ga, gb, ba, bb, blk = _merge_specs(t, cfg)

    def body(ga_ref, gb_ref, ba_ref, bb_ref, ya_ref, yb_ref, o_ref):
        o_ref[...] = _merge_fn(ga_ref[...], gb_ref[...], ba_ref[...], bb_ref[...],
                               ya_ref[...], yb_ref[...]).astype(BF16)

    return pl.pallas_call(
        body, name=name, out_shape=jax.ShapeDtypeStruct((t, cfg["d"]), BF16), grid=(nj, t // tt),
        in_specs=[ga, gb, ba, bb, blk, blk], out_specs=blk,
        compiler_params=_cparams(("arbitrary", "arbitrary")),
    )(p_all, p_all, bias, bias, ya, yb)


def _merge_bwd(name, p_all, bias, ya, yb, dm, cfg):
    t = p_all.shape[0]
    tt, nj, ga, gb, ba, bb, blk = _merge_specs(t, cfg)

    def body(ga_ref, gb_ref, ba_ref, bb_ref, ya_ref, yb_ref, dm_ref,
             dga_ref, dgb_ref, dya_ref, dyb_ref, dba_ref, dbb_ref):
        _, vjp = jax.vjp(_merge_fn, ga_ref[...], gb_ref[...], ba_ref[...], bb_ref[...], ya_ref[...], yb_ref[...])
        dga, dgb, dba, dbb, dya, dyb = vjp(dm_ref[...])
        for ref, val in ((dga_ref, dga), (dgb_ref, dgb), (dya_ref, dya), (dyb_ref, dyb)):
            ref[...] = val.astype(BF16)
        first = pl.program_id(1) == 0
        _acc_out(dba_ref, dba, first)
        _acc_out(dbb_ref, dbb, first)

    d = cfg["d"]
    par = pl.BlockSpec((1, blk.block_shape[1]), lambda j, i: (0, j))
    return pl.pallas_call(
        body, name=name,
        out_shape=[jax.ShapeDtypeStruct((t, d), BF16)] * 4 + [jax.ShapeDtypeStruct((1, d), F32)] * 2,
        grid=(nj, t // tt),
        in_specs=[ga, gb, ba, bb, blk, blk, blk], out_specs=[blk] * 4 + [par] * 2,
        compiler_params=_cparams(("arbitrary", "arbitrary")),
    )(p_all, p_all, bias, bias, ya, yb, dm)


PAIRS = 8


def _pair_stack(ref, pairs):
    return jnp.stack([ref[:, p * LANES:(p + 1) * LANES] for p in range(pairs)])


def _pair_store(ref, val):
    for p in range(val.shape[0]):
        ref[:, p * LANES:(p + 1) * LANES] = val[p]


def _rec_specs(t, cfg, rev):
    dr = cfg["dr"]
    nc = t // CHUNK
    hp = dr // LANES
    pairs = _tile(hp, (PAIRS, 2, 1))
    ng = hp // pairs
    w = LANES * pairs
    ch = (lambda c: nc - 1 - c) if rev else (lambda c: c)
    slab = pl.BlockSpec((CHUNK, w), lambda h, c: (ch(c), h))
    vspec = pl.BlockSpec((CHUNK, w), lambda h, c: (ch(c), 2 * ng + h))
    sspec = pl.BlockSpec((None, pairs, LANES, LANES), lambda h, c: (ch(c), h, 0, 0))
    first = lambda: jnp.logical_and(pl.program_id(0) == 0, pl.program_id(1) == 0)
    last = lambda: jnp.logical_and(pl.program_id(0) == ng - 1, pl.program_id(1) == nc - 1)
    return nc, hp, pairs, ng, slab, vspec, sspec, first, last


def _rec_fwd(name, pm, lw, k2, a, b, cfg, comm=None):
    t = pm.shape[0]
    nc, hp, pairs, ng, slab, vspec, sspec, first, last = _rec_specs(t, cfg, False)

    def body(r_ref, lw_ref, k_ref, v_ref, a_ref, b_ref, y_ref, s_ref, s_scr):
        @pl.when(pl.program_id(1) == 0)
        def _():
            s_scr[...] = jnp.zeros_like(s_scr)

        s = s_scr[...]
        s_ref[...] = s
        y, s_new = _chunk_fn(s, *[_pair_stack(ref, pairs) for ref in (r_ref, lw_ref, k_ref, v_ref, a_ref, b_ref)])
        _pair_store(y_ref, y)
        s_scr[...] = s_new

    return _hosted_call(
        body, name, comm, first, last, args=[pm, lw, k2, pm, a, b],
        in_specs=[slab, slab, slab, vspec, slab, slab],
        out_shape=[jax.ShapeDtypeStruct((t, cfg["dr"]), F32), jax.ShapeDtypeStruct((nc, hp, LANES, LANES), F32)],
        out_specs=[slab, sspec], scratch=[pltpu.VMEM((pairs, LANES, LANES), F32)], grid=(ng, nc),
        sem=("arbitrary", "arbitrary"))


def _rec_bwd(name, pm, lw, k2, a, b, s_chk, dy, cfg, comm=None):
    t = pm.shape[0]
    nc, hp, pairs, ng, slab, vspec, sspec, first, last = _rec_specs(t, cfg, True)

    def body(r_ref, lw_ref, k_ref, v_ref, a_ref, b_ref, s_ref, dy_ref,
             dr_ref, dlw_ref, dk_ref, dv_ref, da_ref, db_ref, ds_scr):
        @pl.when(pl.program_id(1) == 0)
        def _():
            ds_scr[...] = jnp.zeros_like(ds_scr)

        _, vjp = jax.vjp(_chunk_fn, s_ref[...],
                         *[_pair_stack(ref, pairs) for ref in (r_ref, lw_ref, k_ref, v_ref, a_ref, b_ref)])
        ds, dr, dlw, dk, dv, da, db = vjp((_pair_stack(dy_ref, pairs), ds_scr[...]))
        ds_scr[...] = ds
        for ref, val in ((dr_ref, dr), (dlw_ref, dlw), (dk_ref, dk), (dv_ref, dv), (da_ref, da), (db_ref, db)):
            _pair_store(ref, val)

    return _hosted_call(
        body, name, comm, first, last, args=[pm, lw, k2, pm, a, b, s_chk, dy],
        in_specs=[slab, slab, slab, vspec, slab, slab, sspec, slab],
        out_shape=[jax.ShapeDtypeStruct((t, cfg["dr"]), F32)] * 6, out_specs=[slab] * 6,
        scratch=[pltpu.VMEM((pairs, LANES, LANES), F32)], grid=(ng, nc), sem=("arbitrary", "arbitrary"))


def _comm_call(name, comm):
    n = comm.n
    hbm = pl.BlockSpec(memory_space=pl.ANY)

    def body(*refs):
        comm.start(refs[:n], refs[n:2 * n], refs[2 * n:])
        comm.wait(refs[:n], refs[n:2 * n], refs[2 * n:])

    return pl.pallas_call(body, name=name, out_shape=comm.out_shape, in_specs=[hbm] * n, out_specs=[hbm] * n,
                          scratch_shapes=comm.scratch)(*comm.arrs)


def _all_reduce_small(name, v):
    rows = v.shape[0]
    vm = pl.BlockSpec(memory_space=pltpu.VMEM)

    def body(x_ref, out_ref, buf, send_sems, recv_sems):
        x, y, c = _my_pos()
        me, sibling = (x, y, c), (x, y, 1 - c)
        chips = [(1 - x, y), (x, 1 - y), (1 - x, 1 - y)]

        def copy(k, block, to, src=None):
            px, py, pc = block
            dst = buf.at[4 * px + 2 * py + pc]
            return pltpu.make_async_remote_copy(
                src_ref=dst if src is None else src, dst_ref=dst,
                send_sem=send_sems.at[k], recv_sem=recv_sems.at[k], device_id=to, device_id_type=MESH)

        buf[4 * x + 2 * y + c] = x_ref[...]
        first = [copy(0, me, sibling, src=x_ref)]
        first += [copy(1 + j, me, (*chip, c), src=x_ref) for j, chip in enumerate(chips)]
        for cp in first:
            cp.start()
        passed = [copy(4 + j, (*chip, c), sibling) for j, chip in enumerate(chips)]
        for j, chip in enumerate(chips):
            copy(1 + j, (*chip, c), me).wait_recv()
            passed[j].start()
        copy(0, sibling, me).wait_recv()
        for j, chip in enumerate(chips):
            copy(4 + j, (*chip, 1 - c), me).wait_recv()
        for cp in first + passed:
            cp.wait_send()
        acc = buf[0]
        for d in range(1, N_DEV):
            acc = acc + buf[d]
        out_ref[...] = acc

    return pl.pallas_call(
        body, name=name, out_shape=jax.ShapeDtypeStruct(v.shape, F32),
        in_specs=[vm], out_specs=vm,
        scratch_shapes=[pltpu.VMEM((N_DEV, rows, LANES), F32), pltpu.SemaphoreType.DMA((7,)),
                        pltpu.SemaphoreType.DMA((7,))],
    )(v)


def _pair_sum(name, slabs, got, core):
    _, rows, cols = slabs.shape
    nq = got.shape[0]
    rb = _tile(rows, (256, 128, 64, 32, 16, 8))
    mine = pl.BlockSpec((None, rb, cols), lambda q, j, c_ref: (2 * q + c_ref[0], j, 0))
    blk = pl.BlockSpec((None, rb, cols), lambda q, j, c_ref: (q, j, 0))

    def body(c_ref, a_ref, b_ref, o_ref):
        o_ref[...] = (a_ref[...].astype(F32) + b_ref[...].astype(F32)).astype(o_ref.dtype)

    return pl.pallas_call(
        body, name=name, out_shape=jax.ShapeDtypeStruct(got.shape, got.dtype),
        grid_spec=pltpu.PrefetchScalarGridSpec(num_scalar_prefetch=1, grid=(nq, rows // rb),
                                               in_specs=[mine, blk], out_specs=blk),
        compiler_params=_cparams(("arbitrary", "arbitrary")))(core, slabs, got)


def _adamw(name, w, m, v, g_own, g_recv=None):
    rows, cols = w.shape[-2:]
    nr = g_recv.shape[0] if g_recv is not None else 0
    per_el = 4 * 3 + g_own.dtype.itemsize + (nr * g_recv.dtype.itemsize if nr else 0) + 16
    rb = SUBLANES * 2
    while rb * 2 <= rows and rows % (rb * 2) == 0 and rb * 2 * cols * per_el * 2 <= VMEM_LIMIT // 2:
        rb *= 2
    if rows % rb:
        rb = rows
    blk = pl.BlockSpec((rb, cols), lambda i: (i, 0))
    wblk = pl.BlockSpec((None, rb, cols), lambda i: (0, i, 0)) if w.ndim == 3 else blk
    rblk = pl.BlockSpec((max(nr, 1), rb, cols), lambda i: (0, i, 0))
    has_r = g_recv is not None
    bc1 = 1.0 - ADAM_B1 ** ADAM_STEP
    bc2 = 1.0 - ADAM_B2 ** ADAM_STEP

    def body(*refs):
        w_ref, m_ref, v_ref, go_ref = refs[:4]
        gr_ref = refs[4] if has_r else None
        g_out, d_out, m_out, v_out = refs[4 + has_r:]
        g = go_ref[...].astype(F32)
        if has_r:
            for r in range(nr):
                g = g + gr_ref[r].astype(F32)
        mn = ADAM_B1 * m_ref[...] + (1.0 - ADAM_B1) * g
        vn = ADAM_B2 * v_ref[...] + (1.0 - ADAM_B2) * (g * g)
        m_hat = mn / bc1
        v_hat = vn / bc2
        g_out[...] = g
        d_out[...] = -ADAM_LR * (m_hat / (jnp.sqrt(v_hat) + ADAM_EPS) + ADAM_WD * w_ref[...])
        m_out[...] = mn
        v_out[...] = vn

    return pl.pallas_call(
        body, name=name, out_shape=[jax.ShapeDtypeStruct(w.shape, F32)] * 4, grid=(rows // rb,),
        in_specs=[wblk] * 3 + [blk] + ([rblk] if has_r else []), out_specs=[wblk] * 4,
        compiler_params=_cparams(("arbitrary",)),
    )(*([w, m, v, g_own] + ([g_recv] if has_r else [])))


def _round_up(n, q):
    return (n + q - 1) // q * q


def _cols(a8):
    return jnp.transpose(a8, (1, 0, 2)).reshape(a8.shape[1], -1)


def _col_slabs(a):
    r_, c_ = a.shape
    return jnp.transpose(a.reshape(r_, N_DEV, c_ // N_DEV), (1, 0, 2))


def _padded_from_slabs_call(name, slabs, gap_at, gap, total):
    _, rows, c8 = slabs.shape
    tr = _tile(rows, (128, 64, 32, 16))

    def body(s_ref, o_ref):
        o_ref[...] = jnp.zeros_like(o_ref)
        for dd in range(N_DEV):
            lo, hi = dd * c8, (dd + 1) * c8
            if gap and lo <= gap_at < hi:
                cut = gap_at - lo
                if cut:
                    o_ref[:, lo:gap_at] = s_ref[dd, :, :cut]
                o_ref[:, gap_at + gap:hi + gap] = s_ref[dd, :, cut:]
            else:
                off = lo + (gap if lo >= gap_at else 0)
                o_ref[:, off:off + c8] = s_ref[dd]

    return pl.pallas_call(
        body, name=name, out_shape=jax.ShapeDtypeStruct((rows, total), slabs.dtype), grid=(rows // tr,),
        in_specs=[pl.BlockSpec((N_DEV, tr, c8), lambda i: (0, i, 0))], out_specs=pl.BlockSpec((tr, total), lambda i: (i, 0)),
        compiler_params=_cparams(("arbitrary",)))(slabs)


def _slabs_from_padded_call(name, mat, gap_at, gap, c8):
    rows, total = mat.shape
    tr = _tile(rows, (128, 64, 32, 16))

    def body(m_ref, o_ref):
        for dd in range(N_DEV):
            lo, hi = dd * c8, (dd + 1) * c8
            if gap and lo < gap_at < hi:
                cut = gap_at - lo
                o_ref[dd, :, :cut] = m_ref[:, lo:gap_at]
                o_ref[dd, :, cut:] = m_ref[:, gap_at + gap:hi + gap]
            else:
                off = lo + (gap if lo >= gap_at else 0)
                o_ref[dd] = m_ref[:, off:off + c8]

    return pl.pallas_call(
        body, name=name, out_shape=jax.ShapeDtypeStruct((N_DEV, rows, c8), mat.dtype), grid=(rows // tr,),
        in_specs=[pl.BlockSpec((tr, total), lambda i: (i, 0))], out_specs=pl.BlockSpec((N_DEV, tr, c8), lambda i: (0, i, 0)),
        compiler_params=_cparams(("arbitrary",)))(mat)


_MID = ("w_out_a", "w_out_b", "w_out", "w_mlp_up", "w_mlp_down")


def _local_step(x, target, wts, shards, cfg):
    dr, dc, d, lp, cb = cfg["dr"], cfg["dc"], cfg["d"], cfg["lp"], cfg["cb"]
    dff = shards["w_mlp_down"].shape[0] * N_DEV
    wmix = 3 * dr + lp
    (xn,) = _norm_fwd("norm_mix_fwd", x, None, wts["norm_mix_w"], False)
    half = d // 2
    (p_top,), (bot8,) = _matmul("mm_in_top", xn[:, :half], wts["w_top"], "nn", [F32],
                                comm=_Comm("gather", [shards["w_in_bot"]]))
    w_bot = _padded_from_slabs_call("relayout_w_bot", bot8, 3 * dr + cfg["lora"], lp - cfg["lora"], cfg["wall"])
    (p_all,), (g_oa, g_ob, g_o) = _matmul(
        "mm_in_bot", xn[:, half:], w_bot, "nn", [F32], epi=lambda r, top: (r + top,), extras=(p_top,),
        comm=_Comm("gather", [shards["w_out_a"], shards["w_out_b"], shards["w_out"]]))
    w_out_a, w_out_b, w_out = _cols(g_oa), _cols(g_ob), g_o.reshape(d, d)
    pm = _mix_fwd("mix_fwd", p_all, wts["mu_pad"], wmix, cb)
    prep_w = (wts["w0"], wts["a0"], wts["k_k"], wts["k_a"], wts["wd"], wts["wi"], wts["wg"])
    lw, k2, a_in, b_in, g = _prep_fwd("prep_fwd", pm, cfg, *prep_w)
    (y_raw, s_chk), (g_u,) = _rec_fwd(
        "rec_fwd", pm, lw, k2, a_in, b_in, cfg, comm=_Comm("gather", [shards["w_mlp_up"]]))
    w_up = _cols(g_u)
    post_w = (wts["lnx_w"], wts["lnx_b"], wts["r_k"])
    ya_in = _post_fwd("post_fwd", y_raw, pm, k2, g, *post_w, cfg)
    (ya,) = _matmul("mm_out_a", ya_in, w_out_a, "nn", [F32])
    yb_in = _conv_fwd("conv_fwd", p_all, wts["conv_w8"], cfg)
    (yb,) = _matmul("mm_out_b", yb_in, w_out_b, "nn", [F32])
    mg = _merge_fwd("merge_fwd", p_all, wts["gate_bias"], ya, yb, cfg)
    (mo,) = _matmul("mm_out", mg, w_out, "nn", [F32])
    h1, hn = _norm_fwd("norm_mlp_fwd", x, mo, wts["norm_mlp_w"], True)
    (u, act), (g_d,) = _matmul("mm_up", hn, w_up, "nn", [F32, BF16],
                               epi=lambda r: (r, jnp.square(jnp.maximum(r, 0.0))),
                               comm=_Comm("gather", [shards["w_mlp_down"]]))
    w_down = g_d.reshape(dff, d)
    (md,) = _matmul("mm_down", act, w_down, "nn", [F32])
    loss, dh2, dh2b, g_norm_final = _final("final", h1, md, target, wts["norm_final_w"])
    (du,) = _matmul("mm_down_dx", dh2b, w_down, "nt", [BF16],
                    epi=lambda r, uu: (r * (2.0 * jnp.maximum(uu, 0.0)),), extras=(u,))
    (g_down,) = _matmul("mm_down_dw", act, dh2b, "tn", [BF16])
    core = lax.axis_index("c").astype(jnp.int32).reshape(1)
    my_chip = 2 * lax.axis_index("x") + lax.axis_index("y")
    me = 2 * my_chip + lax.axis_index("c")
    own, recv = {}, {}

    def chip_own(chip_sum):
        return lax.dynamic_index_in_dim(chip_sum, my_chip, axis=0, keepdims=False)

    down_slabs = g_down.reshape(N_DEV, dff // N_DEV, d)
    (dhn,), (got,) = _matmul("mm_up_dx", du, w_up, "nt", [F32], comm=_Comm("pair", [down_slabs]))
    down_sum = _pair_sum("pair_sum_down", down_slabs, got, core)
    (g_up,) = _matmul("mm_up_dw", hn, du, "tn", [BF16], out_slabs=True)
    own["w_mlp_down"] = chip_own(down_sum)
    dh1, dh1b, g_norm_mlp = _norm_bwd("norm_mlp_bwd", h1, dhn, dh2, wts["norm_mlp_w"])
    (dmg,), (got,) = _matmul("mm_out_dx", dh1b, w_out, "nt", [F32], comm=_Comm("pair", [g_up]))
    up_sum = _pair_sum("pair_sum_up", g_up, got, core)
    own["w_mlp_up"] = chip_own(up_sum)
    (g_out,) = _matmul("mm_out_dw", mg, dh1b, "tn", [BF16])
    dpga, dpgb, dya, dyb, dba, dbb = _merge_bwd("merge_bwd", p_all, wts["gate_bias"], ya, yb, dmg, cfg)
    (dya_in,) = _matmul("mm_out_a_dx", dya, w_out_a, "nt", [F32])
    (g_out_a,) = _matmul("mm_out_a_dw", ya_in, dya, "tn", [BF16], out_slabs=True)
    (dyb_in,) = _matmul("mm_out_b_dx", dyb, w_out_b, "nt", [F32])
    (g_out_b,) = _matmul("mm_out_b_dw", yb_in, dyb, "tn", [BF16], out_slabs=True)
    dpb, dpc, dpu, g_conv8 = _conv_bwd("conv_bwd", p_all, wts["conv_w8"], dyb_in, cfg)
    dy_raw, dr_post, dk_post, dv_post, dg, g_lnw, g_lnb, g_rk = _post_bwd(
        "post_bwd", y_raw, pm, k2, g, *post_w, dya_in, cfg)
    (dr_rec, dlw, dk_rec, dv_rec, da_in, db_in), (recv["w_mlp_up"], recv["w_mlp_down"]) = _rec_bwd(
        "rec_bwd", pm, lw, k2, a_in, b_in, s_chk, dy_raw, cfg, comm=_Comm("chips", [up_sum, down_sum]))
    (dpm_k, dpl, g_w0, g_a0, g_kk, g_ka, g_wd, g_wi, g_wg) = _prep_bwd(
        "prep_bwd", pm, cfg, *prep_w, (dlw, dk_rec, dk_post, da_in, db_in, dg))
    mu = wts["mu_pad"]
    nb = dr // cb
    dps, dmus = [], []
    for s, dpm_s in enumerate(([dr_rec, dr_post], [dpm_k], [dv_rec, dv_post])):
        dp_s, dmu_s = _mix_bwd("mix_bwd_%d" % s, dpm_s, p_all, s * nb, mu[:, s * dr:(s + 1) * dr], cb)
        dps.append(dp_s)
        dmus.append(dmu_s)
    dp_l, dmu_l = _mix_bwd("mix_bwd_l", [dpl[j] for j in range(nb)], p_all, 3 * nb, mu[:, 3 * dr:], min(cb, lp))
    tail = [jnp.zeros((x.shape[0], cfg["wall"] - cfg["used"]), BF16)] if cfg["wall"] > cfg["used"] else []
    dp_all = jnp.concatenate(dps + [dp_l, dpb, dpc, dpu, dpga, dpgb] + tail, axis=1)
    ld, li, lora = cfg["ld"], cfg["li"], cfg["lora"]
    g_small = jnp.concatenate([g_wd[:ld], g_wi[ld:ld + li], g_wg[ld + li:lora], g_conv8[:3]], axis=0)
    g_small = jnp.pad(g_small, ((0, cfg["small_rows"] - g_small.shape[0]), (0, 0)))
    direct = dict(w_out_a=g_out_a, w_out_b=g_out_b, w_out=g_out.reshape(N_DEV, d // N_DEV, d),
                  small=_col_slabs(g_small))
    (g_all,), got4 = _matmul("mm_in_dw", xn, dp_all, "tn", [BF16],
                             comm=_Comm("exchange", list(direct.values())))
    for n, slabs, r in zip(direct, direct.values(), got4):
        own[n] = lax.dynamic_index_in_dim(slabs, me, axis=0, keepdims=False)
        recv[n] = r
    in_slabs = _slabs_from_padded_call("relayout_g_in", g_all, 3 * dr + lora, lp - lora,
                                       (cfg["used"] - lp + lora) // N_DEV)
    (got,) = _comm_call("pair_exchange", _Comm("pair", [in_slabs]))
    in_sum = _pair_sum("pair_sum_in", in_slabs, got, core)
    (dxn,), (recv["w_in"],) = _matmul("mm_in_dx", dp_all, (wts["w_top"], w_bot), "nt", [F32],
                                      comm=_Comm("chips", [in_sum]))
    own["w_in"] = chip_own(in_sum)
    grad_x, _, g_norm_mix = _norm_bwd("norm_mix_bwd", x, dxn, dh1, wts["norm_mix_w"])
    grads = dict(
        norm_mix_w=g_norm_mix, gate_bias=jnp.concatenate([dba, dbb], axis=1),
        mu_pad=jnp.concatenate(dmus + [dmu_l], axis=1), w0=g_w0, a0=g_a0, k_k=g_kk, k_a=g_ka,
        r_k=g_rk, lnx_w=g_lnw, lnx_b=g_lnb, norm_mlp_w=g_norm_mlp, norm_final_w=g_norm_final)
    return loss, grad_x, grads, own, recv


_SMALL = ("norm_mix_w", "gate_bias", "shift_mu", "w0", "a0", "k_k", "k_a", "r_k", "lnx_w", "lnx_b",
          "norm_mlp_w", "norm_final_w")
_ORDER = ("norm_mix_w", "w_in", "gate_bias", "shift_mu", "w0", "w_decay_up", "a0", "w_iclr_up", "w_gate_up",
          "k_k", "k_a", "r_k", "lnx_w", "lnx_b", "w_out_a", "conv_w", "w_out_b", "w_out", "norm_mlp_w",
          "w_mlp_up", "w_mlp_down", "norm_final_w")


def _step(x, target, w, m, v):
    t, d = x.shape[1], x.shape[2]
    dr = w["w0"].shape[-1]
    ld, li, lg = w["w_decay_up"].shape[1], w["w_iclr_up"].shape[1], w["w_gate_up"].shape[1]
    lora = ld + li + lg
    lp = _round_up(lora, LANES)
    dc = w["conv_w"].shape[-1] * N_DEV
    cb = math.gcd(math.gcd(lp, dr), 512)
    used = 3 * dr + lp + 3 * dc + 2 * d
    wall = _round_up(used, 1024 if used > MAX_FULL_K else LANES)
    small_rows = ld + li + lg + 3
    cfg = dict(d=d, dr=dr, dc=dc, lp=lp, cb=cb, off_conv=3 * dr + lp, off_gate=3 * dr + lp + 3 * dc, used=used,
               wall=wall, ld=ld, li=li, lora=lora, small_rows=_round_up(small_rows, SUBLANES))
    x2, tg2 = x[0], target[0]

    small_sh = jnp.concatenate([w["w_decay_up"][0], w["w_iclr_up"][0], w["w_gate_up"][0], w["conv_w"][0]], axis=0)
    small_sh = jnp.pad(small_sh, ((0, _round_up(small_rows, SUBLANES) - small_rows), (0, 0)))
    big = ("w_in",) + _MID
    w_in_b = w["w_in"][0].astype(BF16)
    top8, gsm = _comm_call("gather_weights", _Comm("gather", [w_in_b[:d // 2], small_sh]))
    shards = {n: w[n][0].astype(BF16) for n in _MID}
    shards["w_in_bot"] = w_in_b[d // 2:]
    w_top = _padded_from_slabs_call("relayout_w_top", top8, 3 * dr + lora, lp - lora, wall)
    sm = _cols(gsm)
    lora_full = sm[:lora]

    def lora_pad(lo, hi):
        rows = lax.broadcasted_iota(jnp.int32, (lp, 1), 0)
        full = jnp.pad(lora_full, ((0, lp - lora), (0, 0)))
        return jnp.where(jnp.logical_and(rows >= lo, rows < hi), full, 0.0)

    conv_w8 = jnp.pad(sm[lora:lora + 3], ((0, SUBLANES - 3), (0, 0)))
    mu_pad = jnp.pad(w["shift_mu"], ((0, 0), (0, lp - lora)))
    wts = dict(
        w_top=w_top, wd=lora_pad(0, ld), wi=lora_pad(ld, ld + li), wg=lora_pad(ld + li, lora), conv_w8=conv_w8,
        mu_pad=mu_pad, norm_mix_w=w["norm_mix_w"], gate_bias=w["gate_bias"], w0=w["w0"], a0=w["a0"],
        k_k=w["k_k"], k_a=w["k_a"], r_k=w["r_k"].reshape(1, dr), lnx_w=w["lnx_w"], lnx_b=w["lnx_b"],
        norm_mlp_w=w["norm_mlp_w"], norm_final_w=w["norm_final_w"].reshape(1, d))

    loss, grad_x, gr, own, received = _local_step(x2, tg2, wts, shards, cfg)

    small_g = dict(norm_mix_w=gr["norm_mix_w"], gate_bias=gr["gate_bias"], shift_mu=gr["mu_pad"][:, :3 * dr + lora],
                   w0=gr["w0"], a0=gr["a0"], k_k=gr["k_k"], k_a=gr["k_a"], r_k=gr["r_k"], lnx_w=gr["lnx_w"],
                   lnx_b=gr["lnx_b"], norm_mlp_w=gr["norm_mlp_w"], norm_final_w=gr["norm_final_w"])
    sizes = [small_g[n].size for n in _SMALL]
    total = sum(sizes) + 1
    prow = _round_up(total, LANES * SUBLANES) // LANES

    def pack(parts):
        flat = jnp.concatenate([p.reshape(-1) for p in parts])
        return jnp.pad(flat, (0, prow * LANES - flat.size)).reshape(prow, LANES)

    g_packed = _all_reduce_small("reduce_small", pack([small_g[n] for n in _SMALL] + [loss[0, :1]]))
    one = jnp.zeros((1,), F32)
    packed = [pack([d_[n] for n in _SMALL] + [one]) for d_ in (w, m, v)]
    sm_out = _adamw("adamw_small", *packed, g_packed)
    loss_out = g_packed.reshape(-1)[total - 1]

    def unpack(flat2d):
        flat = flat2d.reshape(-1)
        out, o = {}, 0
        for n, s in zip(_SMALL, sizes):
            out[n] = flat[o:o + s].reshape(w[n].shape)
            o += s
        return out

    res = [unpack(a) for a in sm_out]

    def shard2d(a):
        return a.reshape(-1, a.shape[-1])

    for n in big:
        outs = _adamw("adamw_" + n, w[n], m[n], v[n], shard2d(own[n]),
                      received[n].reshape(received[n].shape[:1] + shard2d(own[n]).shape))
        for r_, o in zip(res, outs):
            r_[n] = o
    sm_names = ("w_decay_up", "w_iclr_up", "w_gate_up", "conv_w")
    stack = lambda d_: jnp.pad(jnp.concatenate([d_[n][0] for n in sm_names], axis=0),
                               ((0, _round_up(small_rows, SUBLANES) - small_rows), (0, 0)))
    outs = _adamw("adamw_stack", stack(w), stack(m), stack(v), own["small"], received["small"])
    bounds = (0, ld, ld + li, lora, lora + 3)
    for r_, o in zip(res, outs):
        for q, n in enumerate(sm_names):
            r_[n] = o[bounds[q]:bounds[q + 1]].reshape(w[n].shape)

    grad, delta, new_m, new_v = res
    return (loss_out, grad_x[None], *[grad[n] for n in _ORDER], *[delta[n] for n in _ORDER],
            *[new_m[n] for n in _ORDER], *[new_v[n] for n in _ORDER])


def kernel(x, norm_mix_w, w_in, gate_bias, shift_mu, w0, w_decay_up, a0, w_iclr_up, w_gate_up, k_k, k_a, r_k, lnx_w, lnx_b, w_out_a, conv_w, w_out_b, w_out, norm_mlp_w, w_mlp_up, w_mlp_down, norm_final_w, loss_target, m_norm_mix_w, m_w_in, m_gate_bias, m_shift_mu, m_w0, m_w_decay_up, m_a0, m_w_iclr_up, m_w_gate_up, m_k_k, m_k_a, m_r_k, m_lnx_w, m_lnx_b, m_w_out_a, m_conv_w, m_w_out_b, m_w_out, m_norm_mlp_w, m_w_mlp_up, m_w_mlp_down, m_norm_final_w, v_norm_mix_w, v_w_in, v_gate_bias, v_shift_mu, v_w0, v_w_decay_up, v_a0, v_w_iclr_up, v_w_gate_up, v_k_k, v_k_a, v_r_k, v_lnx_w, v_lnx_b, v_w_out_a, v_conv_w, v_w_out_b, v_w_out, v_norm_mlp_w, v_w_mlp_up, v_w_mlp_down, v_norm_final_w):
    w = dict(zip(_ORDER, (norm_mix_w, w_in, gate_bias, shift_mu, w0, w_decay_up, a0, w_iclr_up, w_gate_up, k_k, k_a,
                          r_k, lnx_w, lnx_b, w_out_a, conv_w, w_out_b, w_out, norm_mlp_w, w_mlp_up, w_mlp_down,
                          norm_final_w)))
    m = dict(zip(_ORDER, (m_norm_mix_w, m_w_in, m_gate_bias, m_shift_mu, m_w0, m_w_decay_up, m_a0, m_w_iclr_up,
                          m_w_gate_up, m_k_k, m_k_a, m_r_k, m_lnx_w, m_lnx_b, m_w_out_a, m_conv_w, m_w_out_b,
                          m_w_out, m_norm_mlp_w, m_w_mlp_up, m_w_mlp_down, m_norm_final_w)))
    v = dict(zip(_ORDER, (v_norm_mix_w, v_w_in, v_gate_bias, v_shift_mu, v_w0, v_w_decay_up, v_a0, v_w_iclr_up,
                          v_w_gate_up, v_k_k, v_k_a, v_r_k, v_lnx_w, v_lnx_b, v_w_out_a, v_conv_w, v_w_out_b,
                          v_w_out, v_norm_mlp_w, v_w_mlp_up, v_w_mlp_down, v_norm_final_w)))
    return _step(x, loss_target, w, m, v)
```

```python
import math

import jax
import jax.numpy as jnp
from jax import lax
from jax.experimental import pallas as pl
from jax.experimental.pallas import tpu as pltpu

F32 = jnp.float32
BF16 = jnp.bfloat16
MESH = pl.DeviceIdType.MESH

N_DEV = 8
HEAD = 64
LANES = 128
SUBLANES = 8
CHUNK = 64
RMS_EPS = 1e-5
LNX_EPS = 64e-5
L2_EPS = 1e-12
ADAM_LR = 0.001
ADAM_B1 = 0.9
ADAM_B2 = 0.999
ADAM_EPS = 1e-08
ADAM_WD = 0.01
ADAM_STEP = 10
VMEM_LIMIT = 48 * 1024 * 1024
MAX_FULL_K = 4096
LIGHT_ROWS = (512, 256, 128, 64, 32, 16, 8)


def _cparams(sem):
    return pltpu.CompilerParams(dimension_semantics=sem, vmem_limit_bytes=VMEM_LIMIT)


def _tile(dim, cands):
    for c in cands:
        if c <= dim and dim % c == 0:
            return c
    return dim


def _my_pos():
    return lax.axis_index("x"), lax.axis_index("y"), lax.axis_index("c")


def _peer(pos, r):
    x, y, c = pos
    return (1 - x if r & 4 else x, 1 - y if r & 2 else y, 1 - c if r & 1 else c)


def _slot(pos):
    return 4 * pos[0] + 2 * pos[1] + pos[2]


class _Comm:
    def __init__(self, kind, arrs):
        self.kind, self.arrs, self.n = kind, list(arrs), len(arrs)
        if kind == "gather":
            self.out_shape = [jax.ShapeDtypeStruct((N_DEV,) + a.shape, a.dtype) for a in arrs]
        elif kind == "exchange":
            self.out_shape = [jax.ShapeDtypeStruct((N_DEV - 1,) + a.shape[1:], a.dtype) for a in arrs]
        elif kind == "pair":
            self.out_shape = [jax.ShapeDtypeStruct((N_DEV // 2,) + a.shape[1:], a.dtype) for a in arrs]
        else:
            self.out_shape = [jax.ShapeDtypeStruct((3,) + a.shape[1:], a.dtype) for a in arrs]
        self.scratch = [pltpu.SemaphoreType.DMA((7 * self.n,)), pltpu.SemaphoreType.DMA((7 * self.n,))]
        if kind == "gather":
            self.scratch.append(pltpu.SemaphoreType.DMA((self.n,)))

    def _exchange_copies(self, in_refs, out_refs, sems):
        me = _my_pos()
        x, y, c = me
        cps = []
        for ai in range(self.n):
            if self.kind == "exchange":
                todo = [(in_refs[ai].at[_slot(_peer(me, r))], out_refs[ai].at[r - 1], _peer(me, r), r - 1)
                        for r in range(1, N_DEV)]
            elif self.kind == "pair":
                todo = [(in_refs[ai].at[2 * q + 1 - c], out_refs[ai].at[q], (x, y, 1 - c), q)
                        for q in range(N_DEV // 2)]
            else:
                chips = [(1 - x, y), (x, 1 - y), (1 - x, 1 - y)]
                todo = [(in_refs[ai].at[2 * cx + cy], out_refs[ai].at[j], (cx, cy, c), j)
                        for j, (cx, cy) in enumerate(chips)]
            for src, dst, to, k in todo:
                cps.append(pltpu.make_async_remote_copy(
                    src_ref=src, dst_ref=dst, send_sem=sems[0].at[ai * 7 + k], recv_sem=sems[1].at[ai * 7 + k],
                    device_id=to, device_id_type=MESH))
        return cps

    def _gather_parts(self, in_refs, out_refs, sems):
        x, y, c = _my_pos()
        me, sibling = (x, y, c), (x, y, 1 - c)
        chips = [(1 - x, y), (x, 1 - y), (1 - x, 1 - y)]

        def copy(ai, k, block, to, src=None):
            dst = out_refs[ai].at[_slot(block)]
            return pltpu.make_async_remote_copy(
                src_ref=dst if src is None else src, dst_ref=dst, send_sem=sems[0].at[ai * 7 + k],
                recv_sem=sems[1].at[ai * 7 + k], device_id=to, device_id_type=MESH)

        mine = [pltpu.make_async_copy(in_refs[ai], out_refs[ai].at[_slot(me)], sems[2].at[ai])
                for ai in range(self.n)]
        first = []
        for ai in range(self.n):
            first.append(copy(ai, 0, me, sibling, src=in_refs[ai]))
            first += [copy(ai, 1 + j, me, (*chip, c), src=in_refs[ai]) for j, chip in enumerate(chips)]
        return me, sibling, chips, c, copy, mine, first

    def start(self, in_refs, out_refs, sems):
        if self.kind != "gather":
            for cp in self._exchange_copies(in_refs, out_refs, sems):
                cp.start()
            return
        _, _, _, _, _, mine, first = self._gather_parts(in_refs, out_refs, sems)
        for cp in mine + first:
            cp.start()

    def wait(self, in_refs, out_refs, sems):
        if self.kind != "gather":
            for cp in self._exchange_copies(in_refs, out_refs, sems):
                cp.wait()
            return
        me, sibling, chips, c, copy, mine, first = self._gather_parts(in_refs, out_refs, sems)
        passed = []
        for ai in range(self.n):
            for j, chip in enumerate(chips):
                copy(ai, 1 + j, (*chip, c), me).wait_recv()
                fwd = copy(ai, 4 + j, (*chip, c), sibling)
                fwd.start()
                passed.append(fwd)
        for ai in range(self.n):
            copy(ai, 0, sibling, me).wait_recv()
            for j, chip in enumerate(chips):
                copy(ai, 4 + j, (*chip, 1 - c), me).wait_recv()
        for cp in first + passed:
            cp.wait_send()
        for cp in mine:
            cp.wait()


def _hosted_call(body, name, comm, first, last, *, args, in_specs, out_shape, out_specs, scratch, grid, sem):
    if comm is None:
        return pl.pallas_call(body, name=name, out_shape=out_shape, grid=grid, in_specs=in_specs, out_specs=out_specs,
                              scratch_shapes=scratch, compiler_params=_cparams(sem))(*args)
    ni, no, ns, nc = len(args), len(out_shape), len(scratch), comm.n
    hbm = pl.BlockSpec(memory_space=pl.ANY)

    def hosted(*refs):
        ins, cin = refs[:ni], refs[ni:ni + nc]
        outs, cout = refs[ni + nc:ni + nc + no], refs[ni + nc + no:ni + 2 * nc + no]
        scr, sems = refs[ni + 2 * nc + no:ni + 2 * nc + no + ns], refs[ni + 2 * nc + no + ns:]

        @pl.when(first())
        def _():
            comm.start(cin, cout, sems)

        body(*ins, *outs, *scr)

        @pl.when(last())
        def _():
            comm.wait(cin, cout, sems)

    res = pl.pallas_call(
        hosted, name=name, out_shape=list(out_shape) + comm.out_shape, grid=grid,
        in_specs=list(in_specs) + [hbm] * nc, out_specs=list(out_specs) + [hbm] * nc,
        scratch_shapes=list(scratch) + comm.scratch,
        compiler_params=_cparams(("arbitrary",) * len(grid)))(*args, *comm.arrs)
    return res[:no], res[no:]


_DIMS = {"nn": ((1,), (0,)), "nt": ((1,), (1,)), "tn": ((0,), (0,))}


def _matmul(name, a, b, mode, out_dtypes, epi=None, extras=(), comm=None, out_slabs=False):
    bs = list(b) if isinstance(b, (tuple, list)) else [b]
    if mode == "nn":
        (m, k), n = a.shape, b.shape[1]
    elif mode == "nt":
        (m, k), n = a.shape, sum(x.shape[0] for x in bs)
    else:
        (k, m), n = a.shape, b.shape[1]
    tm = _tile(m, (1024, 512, 256, 128, 64, 32, 16, 8))
    if k <= MAX_FULL_K:
        tk, tn = k, _tile(n // N_DEV if out_slabs else n // len(bs), (512, 256, 128))
    else:
        tk, tn = _tile(k, (2048, 1024, 512, 256, 128)), _tile(n // len(bs), (1024, 512, 256, 128))
    nk = k // tk
    gm, gn = m // tm, n // tn
    a_spec = pl.BlockSpec((tk, tm), lambda i, j, q: (q, i)) if mode == "tn" else pl.BlockSpec((tm, tk), lambda i, j, q: (i, q))
    b_spec = pl.BlockSpec((tn, tk), lambda i, j, q: (j, q)) if mode == "nt" else pl.BlockSpec((tk, tn), lambda i, j, q: (q, j))
    gh = gn // 2
    b_specs = [b_spec] if len(bs) == 1 else [
        pl.BlockSpec((tn, tk), lambda i, j, q: (jnp.minimum(j, gh - 1), q)),
        pl.BlockSpec((tn, tk), lambda i, j, q: (jnp.maximum(j - gh, 0), q))]
    mn_spec = pl.BlockSpec((tm, tn), lambda i, j, q: (i, j))
    per = n // N_DEV // tn if out_slabs else 0
    out_spec = pl.BlockSpec((None, tm, tn), lambda i, j, q: (j // per, i, j % per)) if out_slabs else mn_spec
    ne, no = len(extras), len(out_dtypes)
    dims = (_DIMS[mode], ((), ()))
    keep_t = mode == "tn" and nk == 1 and gn > 1

    def finish(r, extra_refs, out_refs):
        outs = (r,) if epi is None else epi(r, *[e[...] for e in extra_refs])
        for o_ref, o in zip(out_refs, outs):
            o_ref[...] = o.astype(o_ref.dtype)

    def body(a_ref, *rest):
        if len(bs) == 1:
            return step(a_ref, *rest)

        @pl.when(pl.program_id(1) < gh)
        def _():
            step(a_ref, rest[0], *rest[2:])

        @pl.when(pl.program_id(1) >= gh)
        def _():
            step(a_ref, rest[1], *rest[2:])

    def step(a_ref, b_ref, *rest):
        extra_refs, out_refs = rest[:ne], rest[ne:ne + no]
        if keep_t:
            at = rest[ne + no]

            @pl.when(pl.program_id(1) == 0)
            def _():
                at[...] = a_ref[...].T

            part = jnp.dot(at[...], b_ref[...], preferred_element_type=F32)
        else:
            part = lax.dot_general(a_ref[...], b_ref[...], dims, preferred_element_type=F32)
        if nk == 1:
            finish(part, extra_refs, out_refs)
            return
        acc = rest[ne + no]
        q = pl.program_id(2)

        @pl.when(q == 0)
        def _():
            acc[...] = part

        @pl.when(jnp.logical_and(q > 0, q < nk - 1))
        def _():
            acc[...] += part

        @pl.when(q == nk - 1)
        def _():
            finish(acc[...] + part, extra_refs, out_refs)

    def first():
        return jnp.logical_and(jnp.logical_and(pl.program_id(0) == 0, pl.program_id(1) == 0), pl.program_id(2) == 0)

    def last():
        return jnp.logical_and(jnp.logical_and(pl.program_id(0) == gm - 1, pl.program_id(1) == gn - 1),
                               pl.program_id(2) == nk - 1)

    return _hosted_call(
        body, name, comm, first, last,
        args=[a, *bs, *extras], in_specs=[a_spec] + b_specs + [mn_spec] * ne,
        out_shape=[jax.ShapeDtypeStruct((N_DEV, m, n // N_DEV) if out_slabs else (m, n), dt) for dt in out_dtypes],
        out_specs=[out_spec] * no,
        scratch=[pltpu.VMEM((tm, tn), F32)] if nk > 1 else ([pltpu.VMEM((tm, tk), a.dtype)] if keep_t else []),
        grid=(gm, gn, nk), sem=("parallel", "arbitrary" if keep_t else "parallel", "arbitrary"))


@jax.custom_vjp
def _mm(a, w):
    return jnp.dot(a.astype(BF16), w.astype(BF16), preferred_element_type=F32)


def _mm_fwd(a, w):
    return _mm(a, w), (a, w)


def _mm_bwd(res, ct):
    a, w = res
    ctb = ct.astype(BF16)
    da = lax.dot_general(ctb, w.astype(BF16), (((1,), (1,)), ((), ())), preferred_element_type=F32)
    dw = lax.dot_general(a.astype(BF16), ctb, (((0,), (0,)), ((), ())), preferred_element_type=F32)
    return da, dw


_mm.defvjp(_mm_fwd, _mm_bwd)


def _split3(x):
    hi = x.astype(BF16)
    r1 = x - hi.astype(F32)
    mid = r1.astype(BF16)
    lo = (r1 - mid.astype(F32)).astype(BF16)
    return hi, mid, lo


def _head_ones(width):
    r = lax.broadcasted_iota(jnp.int32, (width, width), 0) // HEAD
    c = lax.broadcasted_iota(jnp.int32, (width, width), 1) // HEAD
    return (r == c).astype(BF16)


@jax.custom_vjp
def _segsum(x):
    ones = _head_ones(x.shape[-1])
    out = None
    for piece in _split3(x):
        t = jnp.dot(piece, ones, preferred_element_type=F32)
        out = t if out is None else out + t
    return out


_segsum.defvjp(lambda x: (_segsum(x), None), lambda _, ct: (_segsum(ct),))


def _softplus(z):
    return jnp.maximum(z, 0.0) + jnp.log(1.0 + jnp.exp(-jnp.abs(z)))


def _sigmoid(z):
    return 1.0 / (1.0 + jnp.exp(-z))


def _rms(x, w):
    ms = jnp.mean(x * x, axis=-1, keepdims=True)
    return x * lax.rsqrt(ms + RMS_EPS) * w


def _row(ref, i):
    return ref[pl.ds(i, 1), :]


def _shift_down(x, prev_ref, n, first):
    rolled = pltpu.roll(x, n, 0)
    rows = lax.broadcasted_iota(jnp.int32, x.shape, 0)
    for q in range(n):
        halo = jnp.where(first, 0.0, _row(prev_ref, SUBLANES - n + q))
        rolled = jnp.where(rows == q, halo, rolled)
    return rolled


def _shift_up(x, next_ref, n, last):
    t = x.shape[0]
    rolled = pltpu.roll(x, t - n, 0)
    rows = lax.broadcasted_iota(jnp.int32, x.shape, 0)
    for q in range(n):
        halo = jnp.where(last, 0.0, _row(next_ref, q))
        rolled = jnp.where(rows == t - n + q, halo, rolled)
    return rolled


def _acc_out(ref, val, first):
    @pl.when(first)
    def _():
        ref[...] = val

    @pl.when(jnp.logical_not(first))
    def _():
        ref[...] += val


def _prep_fn(k, plm, w0, a0, kkw, kaw, wd, wi, wg):
    w_log = -_softplus(-(w0 + _mm(jnp.tanh(plm), wd))) - 0.5
    lw = -jnp.exp(w_log)
    a_g = _sigmoid(a0 + _mm(plm, wi))
    g = _mm(_sigmoid(plm), wg)
    kk = k * kkw
    kk = kk / jnp.maximum(jnp.sqrt(_segsum(kk * kk)), L2_EPS)
    k2 = k * (1.0 + (a_g - 1.0) * kaw)
    return lw, k2, -kk, kk * a_g, g


def _post_fn(y, r, k2, v, g, lnw, lnb, rk):
    mu = _segsum(y) * (1.0 / HEAD)
    yc = y - mu
    var = _segsum(yc * yc) * (1.0 / HEAD)
    yn = yc * lax.rsqrt(var + LNX_EPS) * lnw + lnb
    bonus = _segsum(r * k2 * rk) * v
    return (yn + bonus) * g


def _merge_fn(pga, pgb, ba, bb, ya, yb):
    return _sigmoid(pga + ba) * ya + _sigmoid(pgb + bb) * yb


_NN, _NT, _TN = ((2,), (1,)), ((2,), (2,)), ((1,), (1,))


def _dot3(a, b, dims):
    ah = a.astype(BF16)
    al = (a - ah.astype(F32)).astype(BF16)
    bh = b.astype(BF16)
    bl = (b - bh.astype(F32)).astype(BF16)
    dg = lambda p, q: lax.dot_general(p, q, (dims, ((0,), (0,))), preferred_element_type=F32)
    (ca,), (cb_,) = dims
    if a.shape[ca] % LANES:
        return dg(ah, bh) + (dg(ah, bl) + dg(al, bh))
    cross = dg(jnp.concatenate([ah, al], axis=ca), jnp.concatenate([bl, bh], axis=cb_))
    return dg(ah, bh) + cross


@jax.custom_vjp
def _dnn(a, b):
    return _dot3(a, b, _NN)


@jax.custom_vjp
def _dnt(a, b):
    return _dot3(a, b, _NT)


@jax.custom_vjp
def _dtn(a, b):
    return _dot3(a, b, _TN)


_dnn.defvjp(lambda a, b: (_dnn(a, b), (a, b)), lambda res, ct: (_dnt(ct, res[1]), _dtn(res[0], ct)))
_dnt.defvjp(lambda a, b: (_dnt(a, b), (a, b)), lambda res, ct: (_dnn(ct, res[1]), _dtn(ct, res[0])))
_dtn.defvjp(lambda a, b: (_dtn(a, b), (a, b)), lambda res, ct: (_dnt(res[1], ct), _dnn(res[0], ct)))


@jax.custom_vjp
def _unit_lower_inverse(low):
    n = low.shape[-1]
    ri = lax.broadcasted_iota(jnp.int32, low.shape, 1)
    ci = lax.broadcasted_iota(jnp.int32, low.shape, 2)
    inv = (ri == ci).astype(F32) + low
    pw = low
    for _ in range(int(math.log2(n // 2)) - 1):
        pw = _dnn(pw, pw)
        inv = inv + _dnn(inv, pw)
    return inv


def _unit_lower_inverse_bwd(inv, ct):
    return (_dnt(_dtn(inv, ct), inv),)


_unit_lower_inverse.defvjp(lambda low: (_unit_lower_inverse(low),) * 2, _unit_lower_inverse_bwd)


def _chunk_fn(s, r, lw, k, v, a, b):
    np_, c = r.shape[0], r.shape[1]
    c2 = 2 * c
    ri = lax.broadcasted_iota(jnp.int32, (np_, c, c), 1)
    ci = lax.broadcasted_iota(jnp.int32, (np_, c, c), 2)
    tri = (ri >= ci).astype(F32)
    cum = _dnn(tri, lw)
    tot = jnp.sum(lw, axis=1, keepdims=True)
    g_in, g_inv, g_out = jnp.exp(cum), jnp.exp(-cum), jnp.exp(tot - cum)
    lane_head = lax.broadcasted_iota(jnp.int32, (1, 2, 1, LANES), 3) // HEAD
    which = lax.broadcasted_iota(jnp.int32, (1, 2, 1, LANES), 1)
    hmask = (lane_head == which).astype(F32)

    def st(x):
        return (x[:, None] * hmask).reshape(np_, c2, LANES)

    r2, a2 = st(r * g_in), st(a * jnp.exp(cum - lw))
    b2, k2, v2 = st(b * g_inv), st(k * g_inv), st(v)
    bo2, ko2 = st(b * g_out), st(k * g_out)
    r2i = lax.broadcasted_iota(jnp.int32, (np_, c2, c2), 1)
    c2i = lax.broadcasted_iota(jnp.int32, (np_, c2, c2), 2)
    same = (r2i >= c) == (c2i >= c)
    strict = jnp.logical_and(same, r2i > c2i)
    incl = jnp.logical_and(same, r2i >= c2i)
    lab = jnp.where(strict, _dnt(a2, b2), 0.0)
    lak = jnp.where(strict, _dnt(a2, k2), 0.0)
    mrb = jnp.where(incl, _dnt(r2, b2), 0.0)
    mrk = jnp.where(incl, _dnt(r2, k2), 0.0)
    x2 = _dnt(a2, s) + _dnn(lak, v2)
    u2 = _dnn(_unit_lower_inverse(lab), x2)
    y2 = _dnt(r2, s) + _dnn(mrb, u2) + _dnn(mrk, v2)
    y = jnp.sum(y2.reshape(np_, 2, c, LANES), axis=1)
    s_new = s * jnp.exp(tot) + _dtn(u2, bo2) + _dtn(v2, ko2)
    return y, s_new


def _norm_fwd(name, x, add, w, want_sum):
    t, d = x.shape
    tt = _tile(t, (128, 64, 32, 16, 8))
    row = pl.BlockSpec((tt, d), lambda i: (i, 0))
    par = pl.BlockSpec((1, d), lambda i: (0, 0))
    has_add = add is not None

    def body(*refs):
        x_ref = refs[0]
        add_ref = refs[1] if has_add else None
        w_ref = refs[1 + has_add]
        outs = refs[2 + has_add:]
        h = x_ref[...] + add_ref[...] if has_add else x_ref[...]
        if want_sum:
            outs[0][...] = h
        outs[-1][...] = _rms(h, w_ref[...]).astype(BF16)

    out_shape = ([jax.ShapeDtypeStruct((t, d), F32)] if want_sum else []) + [jax.ShapeDtypeStruct((t, d), BF16)]
    return pl.pallas_call(
        body, name=name, out_shape=out_shape, grid=(t // tt,),
        in_specs=[row] + ([row] if has_add else []) + [par],
        out_specs=[row] * len(out_shape),
        compiler_params=_cparams(("arbitrary",)),
    )(*([x] + ([add] if has_add else []) + [w]))


def _norm_bwd(name, xin, dy, dres, w):
    t, d = xin.shape
    tt = _tile(t, (128, 64, 32, 16, 8))
    row = pl.BlockSpec((tt, d), lambda i: (i, 0))
    par = pl.BlockSpec((1, d), lambda i: (0, 0))

    def body(x_ref, dy_ref, dres_ref, w_ref, dx_ref, dxb_ref, dw_ref):
        _, vjp = jax.vjp(_rms, x_ref[...], w_ref[...])
        dx, dw = vjp(dy_ref[...])
        dx = dx + dres_ref[...]
        dx_ref[...] = dx
        dxb_ref[...] = dx.astype(BF16)
        _acc_out(dw_ref, dw, pl.program_id(0) == 0)

    return pl.pallas_call(
        body, name=name,
        out_shape=[jax.ShapeDtypeStruct((t, d), F32), jax.ShapeDtypeStruct((t, d), BF16),
                   jax.ShapeDtypeStruct((1, d), F32)],
        grid=(t // tt,), in_specs=[row, row, row, par], out_specs=[row, row, par],
        compiler_params=_cparams(("arbitrary",)),
    )(xin, dy, dres, w)


def _final(name, h1, md, target, w):
    t, d = h1.shape
    tt = _tile(t, (128, 64, 32, 16, 8))
    row = pl.BlockSpec((tt, d), lambda i: (i, 0))
    par = pl.BlockSpec((1, d), lambda i: (0, 0))
    one = pl.BlockSpec((1, LANES), lambda i: (0, 0))

    def body(h1_ref, md_ref, tg_ref, w_ref, loss_ref, dh_ref, dhb_ref, dw_ref):
        tg = tg_ref[...]

        def f(h, wv):
            err = _rms(h, wv) - tg
            return 0.5 * jnp.sum(jnp.mean(err * err, axis=-1, keepdims=True), axis=0, keepdims=True)

        loss, vjp = jax.vjp(f, h1_ref[...] + md_ref[...], w_ref[...])
        dh, dw = vjp(jnp.ones((1, 1), F32))
        dh_ref[...] = dh
        dhb_ref[...] = dh.astype(BF16)
        first = pl.program_id(0) == 0
        _acc_out(dw_ref, dw, first)
        _acc_out(loss_ref, jnp.broadcast_to(loss, (1, LANES)), first)

    return pl.pallas_call(
        body, name=name,
        out_shape=[jax.ShapeDtypeStruct((1, LANES), F32), jax.ShapeDtypeStruct((t, d), F32),
                   jax.ShapeDtypeStruct((t, d), BF16), jax.ShapeDtypeStruct((1, d), F32)],
        grid=(t // tt,), in_specs=[row, row, row, par], out_specs=[one, row, row, par],
        compiler_params=_cparams(("arbitrary",)),
    )(h1, md, target, w)


def _halo_specs(tt, cb, nrow8, col_of):
    prev = pl.BlockSpec((SUBLANES, cb), lambda i, j: (jnp.maximum(i * (tt // SUBLANES) - 1, 0), col_of(j)))
    nxt = pl.BlockSpec((SUBLANES, cb), lambda i, j: (jnp.minimum((i + 1) * (tt // SUBLANES), nrow8 - 1), col_of(j)))
    return prev, nxt


def _mix_fwd(name, p_all, mu, width, cb):
    t = p_all.shape[0]
    tt = _tile(t, LIGHT_ROWS)
    main = pl.BlockSpec((tt, cb), lambda i, j: (i, j))
    prev, _ = _halo_specs(tt, cb, t // SUBLANES, lambda j: j)
    par = pl.BlockSpec((1, cb), lambda i, j: (0, j))

    def body(p_ref, prev_ref, mu_ref, o_ref):
        p = p_ref[...]
        o_ref[...] = p + (_shift_down(p, prev_ref, 1, pl.program_id(0) == 0) - p) * mu_ref[...]

    return pl.pallas_call(
        body, name=name, out_shape=jax.ShapeDtypeStruct((t, width), F32),
        grid=(t // tt, width // cb), in_specs=[main, prev, par], out_specs=main,
        compiler_params=_cparams(("arbitrary", "arbitrary")),
    )(p_all, p_all, mu)


def _mix_bwd(name, dpm_list, p_all, col0, mu, cb):
    t, width = dpm_list[0].shape
    tt = _tile(t, LIGHT_ROWS)
    n8 = t // SUBLANES
    nl = len(dpm_list)
    main = pl.BlockSpec((tt, cb), lambda j, i: (i, j))
    nxt = pl.BlockSpec((SUBLANES, cb), lambda j, i: (jnp.minimum((i + 1) * (tt // SUBLANES), n8 - 1), j))
    p_main = pl.BlockSpec((tt, cb), lambda j, i: (i, col0 + j))
    p_prev = pl.BlockSpec((SUBLANES, cb), lambda j, i: (jnp.maximum(i * (tt // SUBLANES) - 1, 0), col0 + j))
    par = pl.BlockSpec((1, cb), lambda j, i: (0, j))
    nt_ = t // tt

    def body(*refs):
        d_refs, dn_refs = refs[:nl], refs[nl:2 * nl]
        p_ref, pp_ref, mu_ref, dp_ref, dmu_ref, nx_scr = refs[2 * nl:]
        i = pl.program_id(1)
        dpm = d_refs[0][...]
        nx = dn_refs[0][...]
        for q in range(1, nl):
            dpm = dpm + d_refs[q][...]
            nx = nx + dn_refs[q][...]
        nx_scr[...] = nx
        mu_v = mu_ref[...]
        up = _shift_up(dpm, nx_scr, 1, i == nt_ - 1)
        dp_ref[...] = (dpm * (1.0 - mu_v) + up * mu_v).astype(BF16)
        p = p_ref[...]
        diff = _shift_down(p, pp_ref, 1, i == 0) - p
        _acc_out(dmu_ref, jnp.sum(dpm * diff, axis=0, keepdims=True), i == 0)

    return pl.pallas_call(
        body, name=name,
        out_shape=[jax.ShapeDtypeStruct((t, width), BF16), jax.ShapeDtypeStruct((1, width), F32)],
        grid=(width // cb, nt_),
        in_specs=[main] * nl + [nxt] * nl + [p_main, p_prev, par],
        out_specs=[main, par],
        scratch_shapes=[pltpu.VMEM((SUBLANES, cb), F32)],
        compiler_params=_cparams(("arbitrary", "arbitrary")),
    )(*dpm_list, *dpm_list, p_all, p_all, mu)


def _prep_fwd(name, pm, cfg, w0, a0, kkw, kaw, wd, wi, wg):
    t = pm.shape[0]
    dr, lp, cb = cfg["dr"], cfg["lp"], cfg["cb"]
    tt = _tile(t, (256, 128, 64, 32, 16, 8))
    nj = dr // cb
    kspec = pl.BlockSpec((tt, cb), lambda j, i: (i, nj + j))
    lspec = pl.BlockSpec((tt, lp), lambda j, i: (i, 3 * dr // lp))
    par = pl.BlockSpec((1, cb), lambda j, i: (0, j))
    wspec = pl.BlockSpec((lp, cb), lambda j, i: (0, j))
    out = pl.BlockSpec((tt, cb), lambda j, i: (i, j))

    def body(k_ref, l_ref, w0_ref, a0_ref, kk_ref, ka_ref, wd_ref, wi_ref, wg_ref, *outs):
        vals = _prep_fn(k_ref[...], l_ref[...], w0_ref[...], a0_ref[...], kk_ref[...], ka_ref[...],
                        wd_ref[...], wi_ref[...], wg_ref[...])
        for o_ref, val in zip(outs, vals):
            o_ref[...] = val

    return pl.pallas_call(
        body, name=name, out_shape=[jax.ShapeDtypeStruct((t, dr), F32)] * 5,
        grid=(nj, t // tt), in_specs=[kspec, lspec, par, par, par, par, wspec, wspec, wspec],
        out_specs=[out] * 5, compiler_params=_cparams(("arbitrary", "arbitrary")),
    )(pm, pm, w0, a0, kkw, kaw, wd, wi, wg)


def _prep_bwd(name, pm, cfg, w0, a0, kkw, kaw, wd, wi, wg, cts, dr_parts, dv_parts):
    t = pm.shape[0]
    dr, lp, cb = cfg["dr"], cfg["lp"], cfg["cb"]
    tt = _tile(t, (256, 128, 64, 32, 16, 8))
    nj = dr // cb
    kspec = pl.BlockSpec((tt, cb), lambda j, i: (i, nj + j))
    lspec = pl.BlockSpec((tt, lp), lambda j, i: (i, 3 * dr // lp))
    par = pl.BlockSpec((1, cb), lambda j, i: (0, j))
    wspec = pl.BlockSpec((lp, cb), lambda j, i: (0, j))
    blk = pl.BlockSpec((tt, cb), lambda j, i: (i, j))
    dpl_spec = pl.BlockSpec((None, tt, lp), lambda j, i: (j, i, 0))

    def body(k_ref, l_ref, w0_ref, a0_ref, kk_ref, ka_ref, wd_ref, wi_ref, wg_ref,
             dlw_ref, dk2a_ref, dk2b_ref, da_ref, db_ref, dg_ref, dr0_ref, dr1_ref, dv0_ref, dv1_ref,
             dpr_ref, dpk_ref, dpv_ref, dpl_ref, dw0_ref, da0_ref, dkk_ref, dka_ref, dwd_ref, dwi_ref, dwg_ref):
        _, vjp = jax.vjp(_prep_fn, k_ref[...], l_ref[...], w0_ref[...], a0_ref[...], kk_ref[...], ka_ref[...],
                         wd_ref[...], wi_ref[...], wg_ref[...])
        dk, dpl, dw0, da0, dkk, dka, dwd, dwi, dwg = vjp(
            (dlw_ref[...], dk2a_ref[...] + dk2b_ref[...], da_ref[...], db_ref[...], dg_ref[...]))
        dpr_ref[...] = dr0_ref[...] + dr1_ref[...]
        dpv_ref[...] = dv0_ref[...] + dv1_ref[...]
        dpk_ref[...] = dk
        dpl_ref[...] = dpl
        first = pl.program_id(1) == 0
        for ref, val in ((dw0_ref, dw0), (da0_ref, da0), (dkk_ref, dkk), (dka_ref, dka),
                         (dwd_ref, dwd), (dwi_ref, dwi), (dwg_ref, dwg)):
            _acc_out(ref, val, first)

    out_shape = ([jax.ShapeDtypeStruct((t, dr), F32)] * 3 + [jax.ShapeDtypeStruct((nj, t, lp), F32)]
                 + [jax.ShapeDtypeStruct((1, dr), F32)] * 4 + [jax.ShapeDtypeStruct((lp, dr), F32)] * 3)
    return pl.pallas_call(
        body, name=name, out_shape=out_shape, grid=(nj, t // tt),
        in_specs=[kspec, lspec, par, par, par, par, wspec, wspec, wspec] + [blk] * 10,
        out_specs=[blk] * 3 + [dpl_spec] + [par] * 4 + [wspec] * 3,
        compiler_params=_cparams(("arbitrary", "arbitrary")),
    )(pm, pm, w0, a0, kkw, kaw, wd, wi, wg, *cts, *dr_parts, *dv_parts)


def _post_specs(t, cfg):
    dr, cb = cfg["dr"], cfg["cb"]
    tt = _tile(t, (256, 128, 64, 32, 16, 8))
    nj = dr // cb
    blk = pl.BlockSpec((tt, cb), lambda j, i: (i, j))
    rspec = pl.BlockSpec((tt, cb), lambda j, i: (i, j))
    vspec = pl.BlockSpec((tt, cb), lambda j, i: (i, 2 * nj + j))
    par = pl.BlockSpec((1, cb), lambda j, i: (0, j))
    return tt, nj, blk, rspec, vspec, par


def _post_fwd(name, y, pm, k2, g, lnw, lnb, rk, cfg):
    t = y.shape[0]
    tt, nj, blk, rspec, vspec, par = _post_specs(t, cfg)

    def body(y_ref, r_ref, k_ref, v_ref, g_ref, lw_ref, lb_ref, rk_ref, o_ref):
        o_ref[...] = _post_fn(y_ref[...], r_ref[...], k_ref[...], v_ref[...], g_ref[...],
                              lw_ref[...], lb_ref[...], rk_ref[...]).astype(BF16)

    return pl.pallas_call(
        body, name=name, out_shape=jax.ShapeDtypeStruct((t, cfg["dr"]), BF16), grid=(nj, t // tt),
        in_specs=[blk, rspec, blk, vspec, blk, par, par, par], out_specs=blk,
        compiler_params=_cparams(("arbitrary", "arbitrary")),
    )(y, pm, k2, pm, g, lnw, lnb, rk)


def _post_bwd(name, y, pm, k2, g, lnw, lnb, rk, dout, cfg):
    t = y.shape[0]
    tt, nj, blk, rspec, vspec, par = _post_specs(t, cfg)

    def body(y_ref, r_ref, k_ref, v_ref, g_ref, lw_ref, lb_ref, rk_ref, do_ref,
             dy_ref, dr_ref, dk_ref, dv_ref, dg_ref, dlw_ref, dlb_ref, drk_ref):
        _, vjp = jax.vjp(_post_fn, y_ref[...], r_ref[...], k_ref[...], v_ref[...], g_ref[...],
                         lw_ref[...], lb_ref[...], rk_ref[...])
        dy, dr, dk, dv, dg, dlw, dlb, drk = vjp(do_ref[...])
        for ref, val in ((dy_ref, dy), (dr_ref, dr), (dk_ref, dk), (dv_ref, dv), (dg_ref, dg)):
            ref[...] = val
        first = pl.program_id(1) == 0
        for ref, val in ((dlw_ref, dlw), (dlb_ref, dlb), (drk_ref, drk)):
            _acc_out(ref, val, first)

    dr = cfg["dr"]
    return pl.pallas_call(
        body, name=name,
        out_shape=[jax.ShapeDtypeStruct((t, dr), F32)] * 5 + [jax.ShapeDtypeStruct((1, dr), F32)] * 3,
        grid=(nj, t // tt),
        in_specs=[blk, rspec, blk, vspec, blk, par, par, par, blk],
        out_specs=[blk] * 5 + [par] * 3,
        compiler_params=_cparams(("arbitrary", "arbitrary")),
    )(y, pm, k2, pm, g, lnw, lnb, rk, dout)


def _conv_specs(t, cfg):
    dc, cb = cfg["dc"], cfg["cb"]
    tt = _tile(t, LIGHT_ROWS)
    nj = dc // cb
    c0 = cfg["off_conv"] // cb
    n8 = t // SUBLANES

    def sect(s):
        col = lambda j: c0 + s * nj + j
        main = pl.BlockSpec((tt, cb), lambda j, i: (i, col(j)))
        prev = pl.BlockSpec((SUBLANES, cb), lambda j, i: (jnp.maximum(i * (tt // SUBLANES) - 1, 0), col(j)))
        nxt = pl.BlockSpec((SUBLANES, cb), lambda j, i: (jnp.minimum((i + 1) * (tt // SUBLANES), n8 - 1), col(j)))
        return main, prev, nxt

    blk = pl.BlockSpec((tt, cb), lambda j, i: (i, j))
    wspec = pl.BlockSpec((SUBLANES, cb), lambda j, i: (0, j))
    return tt, nj, n8, sect, blk, wspec


def _conv_fwd(name, p_all, cw8, cfg):
    t = p_all.shape[0]
    tt, nj, n8, sect, blk, wspec = _conv_specs(t, cfg)
    (bm, _, _), (cm, cp, _), (um, up, _) = sect(0), sect(1), sect(2)

    def body(b_ref, c_ref, cp_ref, u_ref, up_ref, w_ref, o_ref, zp_scr):
        first = pl.program_id(1) == 0
        z = c_ref[...] * u_ref[...]
        zp_scr[...] = cp_ref[...] * up_ref[...]
        o = _row(w_ref, 2) * z + _row(w_ref, 1) * _shift_down(z, zp_scr, 1, first) \
            + _row(w_ref, 0) * _shift_down(z, zp_scr, 2, first)
        o_ref[...] = (b_ref[...] * o).astype(BF16)

    return pl.pallas_call(
        body, name=name, out_shape=jax.ShapeDtypeStruct((t, cfg["dc"]), BF16), grid=(nj, t // tt),
        in_specs=[bm, cm, cp, um, up, wspec], out_specs=blk,
        scratch_shapes=[pltpu.VMEM((SUBLANES, blk.block_shape[1]), F32)],
        compiler_params=_cparams(("arbitrary", "arbitrary")),
    )(p_all, p_all, p_all, p_all, p_all, cw8)


def _conv_bwd(name, p_all, cw8, dyb, cfg):
    t = p_all.shape[0]
    tt, nj, n8, sect, blk, wspec = _conv_specs(t, cfg)
    (bm, _, bn), (cm, cp, _), (um, up, _) = sect(0), sect(1), sect(2)
    cb = blk.block_shape[1]
    dnxt = pl.BlockSpec((SUBLANES, cb), lambda j, i: (jnp.minimum((i + 1) * (tt // SUBLANES), n8 - 1), j))
    nt_ = t // tt

    def body(b_ref, bn_ref, c_ref, cp_ref, u_ref, up_ref, w_ref, d_ref, dn_ref,
             db_ref, dc_ref, du_ref, dw_ref, zp_scr, don_scr):
        i = pl.program_id(1)
        first, last = i == 0, i == nt_ - 1
        c, u, b, dy = c_ref[...], u_ref[...], b_ref[...], d_ref[...]
        z = c * u
        zp_scr[...] = cp_ref[...] * up_ref[...]
        z1 = _shift_down(z, zp_scr, 1, first)
        z2 = _shift_down(z, zp_scr, 2, first)
        w0, w1, w2 = _row(w_ref, 0), _row(w_ref, 1), _row(w_ref, 2)
        o = w2 * z + w1 * z1 + w0 * z2
        do = dy * b
        don_scr[...] = dn_ref[...] * bn_ref[...]
        dz = w2 * do + w1 * _shift_up(do, don_scr, 1, last) + w0 * _shift_up(do, don_scr, 2, last)
        db_ref[...] = (dy * o).astype(BF16)
        dc_ref[...] = (dz * u).astype(BF16)
        du_ref[...] = (dz * c).astype(BF16)
        rows = lax.broadcasted_iota(jnp.int32, (SUBLANES, cb), 0)
        s0 = jnp.sum(do * z2, axis=0, keepdims=True)
        s1 = jnp.sum(do * z1, axis=0, keepdims=True)
        s2 = jnp.sum(do * z, axis=0, keepdims=True)
        dw = jnp.where(rows == 0, s0, jnp.where(rows == 1, s1, jnp.where(rows == 2, s2, 0.0)))
        _acc_out(dw_ref, dw, first)

    dc = cfg["dc"]
    return pl.pallas_call(
        body, name=name,
        out_shape=[jax.ShapeDtypeStruct((t, dc), BF16)] * 3 + [jax.ShapeDtypeStruct((SUBLANES, dc), F32)],
        grid=(nj, nt_),
        in_specs=[bm, bn, cm, cp, um, up, wspec, blk, dnxt],
        out_specs=[blk] * 3 + [wspec],
        scratch_shapes=[pltpu.VMEM((SUBLANES, cb), F32), pltpu.VMEM((SUBLANES, cb), F32)],
        compiler_params=_cparams(("arbitrary", "arbitrary")),
    )(p_all, p_all, p_all, p_all, p_all, p_all, cw8, dyb, dyb)


def _merge_specs(t, cfg):
    d, cb = cfg["d"], cfg["cb"]
    tt = _tile(t, LIGHT_ROWS)
    nj = d // cb
    g0 = cfg["off_gate"] // cb
    ga = pl.BlockSpec((tt, cb), lambda j, i: (i, g0 + j))
    gb = pl.BlockSpec((tt, cb), lambda j, i: (i, g0 + nj + j))
    ba = pl.BlockSpec((1, cb), lambda j, i: (0, j))
    bb = pl.BlockSpec((1, cb), lambda j, i: (0, nj + j))
    blk = pl.BlockSpec((tt, cb), lambda j, i: (i, j))
    return tt, nj, ga, gb, ba, bb, blk


def _merge_fwd(name, p_all, bias, ya, yb, cfg):
    t = p_all.shape[0]
    tt, nj, ga, gb, ba, bb, blk = _merge_specs(t, cfg)

    def body(ga_ref, gb_ref, ba_ref, bb_ref, ya_ref, yb_ref, o_ref):
        o_ref[...] = _merge_fn(ga_ref[...], gb_ref[...], ba_ref[...], bb_ref[...],
                               ya_ref[...], yb_ref[...]).astype(BF16)

    return pl.pallas_call(
        body, name=name, out_shape=jax.ShapeDtypeStruct((t, cfg["d"]), BF16), grid=(nj, t // tt),
        in_specs=[ga, gb, ba, bb, blk, blk], out_specs=blk,
        compiler_params=_cparams(("arbitrary", "arbitrary")),
    )(p_all, p_all, bias, bias, ya, yb)


def _merge_bwd(name, p_all, bias, ya, yb, dm, cfg):
    t = p_all.shape[0]
    tt, nj, ga, gb, ba, bb, blk = _merge_specs(t, cfg)

    def body(ga_ref, gb_ref, ba_ref, bb_ref, ya_ref, yb_ref, dm_ref,
             dga_ref, dgb_ref, dya_ref, dyb_ref, dba_ref, dbb_ref):
        _, vjp = jax.vjp(_merge_fn, ga_ref[...], gb_ref[...], ba_ref[...], bb_ref[...], ya_ref[...], yb_ref[...])
        dga, dgb, dba, dbb, dya, dyb = vjp(dm_ref[...])
        for ref, val in ((dga_ref, dga), (dgb_ref, dgb), (dya_ref, dya), (dyb_ref, dyb)):
            ref[...] = val.astype(BF16)
        first = pl.program_id(1) == 0
        _acc_out(dba_ref, dba, first)
        _acc_out(dbb_ref, dbb, first)

    d = cfg["d"]
    par = pl.BlockSpec((1, blk.block_shape[1]), lambda j, i: (0, j))
    return pl.pallas_call(
        body, name=name,
        out_shape=[jax.ShapeDtypeStruct((t, d), BF16)] * 4 + [jax.ShapeDtypeStruct((1, d), F32)] * 2,
        grid=(nj, t // tt),
        in_specs=[ga, gb, ba, bb, blk, blk, blk], out_specs=[blk] * 4 + [par] * 2,
        compiler_params=_cparams(("arbitrary", "arbitrary")),
    )(p_all, p_all, bias, bias, ya, yb, dm)


PAIRS = 8


def _pair_stack(ref, pairs):
    return jnp.stack([ref[:, p * LANES:(p + 1) * LANES] for p in range(pairs)])


def _pair_store(ref, val):
    for p in range(val.shape[0]):
        ref[:, p * LANES:(p + 1) * LANES] = val[p]


def _rec_specs(t, cfg, rev):
    dr = cfg["dr"]
    nc = t // CHUNK
    hp = dr // LANES
    pairs = _tile(hp, (PAIRS, 2, 1))
    ng = hp // pairs
    w = LANES * pairs
    ch = (lambda c: nc - 1 - c) if rev else (lambda c: c)
    slab = pl.BlockSpec((CHUNK, w), lambda h, c: (ch(c), h))
    vspec = pl.BlockSpec((CHUNK, w), lambda h, c: (ch(c), 2 * ng + h))
    sspec = pl.BlockSpec((None, pairs, LANES, LANES), lambda h, c: (ch(c), h, 0, 0))
    first = lambda: jnp.logical_and(pl.program_id(0) == 0, pl.program_id(1) == 0)
    last = lambda: jnp.logical_and(pl.program_id(0) == ng - 1, pl.program_id(1) == nc - 1)
    return nc, hp, pairs, ng, slab, vspec, sspec, first, last


def _rec_fwd(name, pm, lw, k2, a, b, cfg, comm=None):
    t = pm.shape[0]
    nc, hp, pairs, ng, slab, vspec, sspec, first, last = _rec_specs(t, cfg, False)

    def body(r_ref, lw_ref, k_ref, v_ref, a_ref, b_ref, y_ref, s_ref, s_scr):
        @pl.when(pl.program_id(1) == 0)
        def _():
            s_scr[...] = jnp.zeros_like(s_scr)

        s = s_scr[...]
        s_ref[...] = s
        y, s_new = _chunk_fn(s, *[_pair_stack(ref, pairs) for ref in (r_ref, lw_ref, k_ref, v_ref, a_ref, b_ref)])
        _pair_store(y_ref, y)
        s_scr[...] = s_new

    return _hosted_call(
        body, name, comm, first, last, args=[pm, lw, k2, pm, a, b],
        in_specs=[slab, slab, slab, vspec, slab, slab],
        out_shape=[jax.ShapeDtypeStruct((t, cfg["dr"]), F32), jax.ShapeDtypeStruct((nc, hp, LANES, LANES), F32)],
        out_specs=[slab, sspec], scratch=[pltpu.VMEM((pairs, LANES, LANES), F32)], grid=(ng, nc),
        sem=("arbitrary", "arbitrary"))


def _rec_bwd(name, pm, lw, k2, a, b, s_chk, dy, cfg, comm=None):
    t = pm.shape[0]
    nc, hp, pairs, ng, slab, vspec, sspec, first, last = _rec_specs(t, cfg, True)

    def body(r_ref, lw_ref, k_ref, v_ref, a_ref, b_ref, s_ref, dy_ref,
             dr_ref, dlw_ref, dk_ref, dv_ref, da_ref, db_ref, ds_scr):
        @pl.when(pl.program_id(1) == 0)
        def _():
            ds_scr[...] = jnp.zeros_like(ds_scr)

        _, vjp = jax.vjp(_chunk_fn, s_ref[...],
                         *[_pair_stack(ref, pairs) for ref in (r_ref, lw_ref, k_ref, v_ref, a_ref, b_ref)])
        ds, dr, dlw, dk, dv, da, db = vjp((_pair_stack(dy_ref, pairs), ds_scr[...]))
        ds_scr[...] = ds
        for ref, val in ((dr_ref, dr), (dlw_ref, dlw), (dk_ref, dk), (dv_ref, dv), (da_ref, da), (db_ref, db)):
            _pair_store(ref, val)

    return _hosted_call(
        body, name, comm, first, last, args=[pm, lw, k2, pm, a, b, s_chk, dy],
        in_specs=[slab, slab, slab, vspec, slab, slab, sspec, slab],
        out_shape=[jax.ShapeDtypeStruct((t, cfg["dr"]), F32)] * 6, out_specs=[slab] * 6,
        scratch=[pltpu.VMEM((pairs, LANES, LANES), F32)], grid=(ng, nc), sem=("arbitrary", "arbitrary"))


def _comm_call(name, comm):
    n = comm.n
    hbm = pl.BlockSpec(memory_space=pl.ANY)

    def body(*refs):
        comm.start(refs[:n], refs[n:2 * n], refs[2 * n:])
        comm.wait(refs[:n], refs[n:2 * n], refs[2 * n:])

    return pl.pallas_call(body, name=name, out_shape=comm.out_shape, in_specs=[hbm] * n, out_specs=[hbm] * n,
                          scratch_shapes=comm.scratch)(*comm.arrs)


def _all_reduce_small(name, v):
    rows = v.shape[0]
    vm = pl.BlockSpec(memory_space=pltpu.VMEM)

    def body(x_ref, out_ref, buf, send_sems, recv_sems):
        x, y, c = _my_pos()
        me, sibling = (x, y, c), (x, y, 1 - c)
        chips = [(1 - x, y), (x, 1 - y), (1 - x, 1 - y)]

        def copy(k, block, to, src=None):
            px, py, pc = block
            dst = buf.at[4 * px + 2 * py + pc]
            return pltpu.make_async_remote_copy(
                src_ref=dst if src is None else src, dst_ref=dst,
                send_sem=send_sems.at[k], recv_sem=recv_sems.at[k], device_id=to, device_id_type=MESH)

        buf[4 * x + 2 * y + c] = x_ref[...]
        first = [copy(0, me, sibling, src=x_ref)]
        first += [copy(1 + j, me, (*chip, c), src=x_ref) for j, chip in enumerate(chips)]
        for cp in first:
            cp.start()
        passed = [copy(4 + j, (*chip, c), sibling) for j, chip in enumerate(chips)]
        for j, chip in enumerate(chips):
            copy(1 + j, (*chip, c), me).wait_recv()
            passed[j].start()
        copy(0, sibling, me).wait_recv()
        for j, chip in enumerate(chips):
            copy(4 + j, (*chip, 1 - c), me).wait_recv()
        for cp in first + passed:
            cp.wait_send()
        acc = buf[0]
        for d in range(1, N_DEV):
            acc = acc + buf[d]
        out_ref[...] = acc

    return pl.pallas_call(
        body, name=name, out_shape=jax.ShapeDtypeStruct(v.shape, F32),
        in_specs=[vm], out_specs=vm,
        scratch_shapes=[pltpu.VMEM((N_DEV, rows, LANES), F32), pltpu.SemaphoreType.DMA((7,)),
                        pltpu.SemaphoreType.DMA((7,))],
    )(v)


def _pair_sum(name, slabs, got, core):
    _, rows, cols = slabs.shape
    nq = got.shape[0]
    rb = _tile(rows, (256, 128, 64, 32, 16, 8))
    mine = pl.BlockSpec((None, rb, cols), lambda q, j, c_ref: (2 * q + c_ref[0], j, 0))
    blk = pl.BlockSpec((None, rb, cols), lambda q, j, c_ref: (q, j, 0))

    def body(c_ref, a_ref, b_ref, o_ref):
        o_ref[...] = (a_ref[...].astype(F32) + b_ref[...].astype(F32)).astype(o_ref.dtype)

    return pl.pallas_call(
        body, name=name, out_shape=jax.ShapeDtypeStruct(got.shape, got.dtype),
        grid_spec=pltpu.PrefetchScalarGridSpec(num_scalar_prefetch=1, grid=(nq, rows // rb),
                                               in_specs=[mine, blk], out_specs=blk),
        compiler_params=_cparams(("arbitrary", "arbitrary")))(core, slabs, got)


def _adamw(name, w, m, v, g_own, g_recv=None, comm=None):
    rows, cols = w.shape
    nr = g_recv.shape[0] if g_recv is not None else 0
    per_el = 4 * 3 + g_own.dtype.itemsize + (nr * g_recv.dtype.itemsize if nr else 0) + 16
    rb = SUBLANES * 2
    while rb * 2 <= rows and rows % (rb * 2) == 0 and rb * 2 * cols * per_el * 2 <= VMEM_LIMIT // 2:
        rb *= 2
    if rows % rb:
        rb = rows
    blk = pl.BlockSpec((rb, cols), lambda i: (i, 0))
    rblk = pl.BlockSpec((max(nr, 1), rb, cols), lambda i: (0, i, 0))
    has_r = g_recv is not None
    bc1 = 1.0 - ADAM_B1 ** ADAM_STEP
    bc2 = 1.0 - ADAM_B2 ** ADAM_STEP

    def body(*refs):
        w_ref, m_ref, v_ref, go_ref = refs[:4]
        gr_ref = refs[4] if has_r else None
        g_out, d_out, m_out, v_out = refs[4 + has_r:]
        g = go_ref[...].astype(F32)
        if has_r:
            for r in range(nr):
                g = g + gr_ref[r].astype(F32)
        mn = ADAM_B1 * m_ref[...] + (1.0 - ADAM_B1) * g
        vn = ADAM_B2 * v_ref[...] + (1.0 - ADAM_B2) * (g * g)
        m_hat = mn / bc1
        v_hat = vn / bc2
        g_out[...] = g
        d_out[...] = -ADAM_LR * (m_hat / (jnp.sqrt(v_hat) + ADAM_EPS) + ADAM_WD * w_ref[...])
        m_out[...] = mn
        v_out[...] = vn

    steps = rows // rb
    return _hosted_call(
        body, name, comm, lambda: pl.program_id(0) == 0, lambda: pl.program_id(0) == steps - 1,
        args=[w, m, v, g_own] + ([g_recv] if has_r else []), in_specs=[blk] * 4 + ([rblk] if has_r else []),
        out_shape=[jax.ShapeDtypeStruct((rows, cols), F32)] * 4, out_specs=[blk] * 4, scratch=[], grid=(steps,),
        sem=("arbitrary",))


def _round_up(n, q):
    return (n + q - 1) // q * q


def _cols(a8):
    return jnp.transpose(a8, (1, 0, 2)).reshape(a8.shape[1], -1)


def _col_slabs(a):
    r_, c_ = a.shape
    return jnp.transpose(a.reshape(r_, N_DEV, c_ // N_DEV), (1, 0, 2))


def _padded_from_slabs_call(name, slabs, gap_at, gap, total):
    _, rows, c8 = slabs.shape
    tr = _tile(rows, (128, 64, 32, 16))

    def body(s_ref, o_ref):
        o_ref[...] = jnp.zeros_like(o_ref)
        for dd in range(N_DEV):
            lo, hi = dd * c8, (dd + 1) * c8
            if gap and lo <= gap_at < hi:
                cut = gap_at - lo
                if cut:
                    o_ref[:, lo:gap_at] = s_ref[dd, :, :cut]
                o_ref[:, gap_at + gap:hi + gap] = s_ref[dd, :, cut:]
            else:
                off = lo + (gap if lo >= gap_at else 0)
                o_ref[:, off:off + c8] = s_ref[dd]

    return pl.pallas_call(
        body, name=name, out_shape=jax.ShapeDtypeStruct((rows, total), slabs.dtype), grid=(rows // tr,),
        in_specs=[pl.BlockSpec((N_DEV, tr, c8), lambda i: (0, i, 0))], out_specs=pl.BlockSpec((tr, total), lambda i: (i, 0)),
        compiler_params=_cparams(("arbitrary",)))(slabs)


def _slabs_from_padded_call(name, mat, gap_at, gap, c8):
    rows, total = mat.shape
    tr = _tile(rows, (128, 64, 32, 16))

    def body(m_ref, o_ref):
        for dd in range(N_DEV):
            lo, hi = dd * c8, (dd + 1) * c8
            if gap and lo < gap_at < hi:
                cut = gap_at - lo
                o_ref[dd, :, :cut] = m_ref[:, lo:gap_at]
                o_ref[dd, :, cut:] = m_ref[:, gap_at + gap:hi + gap]
            else:
                off = lo + (gap if lo >= gap_at else 0)
                o_ref[dd] = m_ref[:, off:off + c8]

    return pl.pallas_call(
        body, name=name, out_shape=jax.ShapeDtypeStruct((N_DEV, rows, c8), mat.dtype), grid=(rows // tr,),
        in_specs=[pl.BlockSpec((tr, total), lambda i: (i, 0))], out_specs=pl.BlockSpec((N_DEV, tr, c8), lambda i: (0, i, 0)),
        compiler_params=_cparams(("arbitrary",)))(mat)


_MID = ("w_out_a", "w_out_b", "w_out", "w_mlp_up", "w_mlp_down")


def _local_step(x, target, wts, shards, cfg, adam_up):
    dr, dc, d, lp, cb = cfg["dr"], cfg["dc"], cfg["d"], cfg["lp"], cfg["cb"]
    dff = shards["w_mlp_down"].shape[0] * N_DEV
    wmix = 3 * dr + lp
    (xn,) = _norm_fwd("norm_mix_fwd", x, None, wts["norm_mix_w"], False)
    half = d // 2
    (p_top,), (bot8,) = _matmul("mm_in_top", xn[:, :half], wts["w_top"], "nn", [F32],
                                comm=_Comm("gather", [shards["w_in_bot"]]))
    w_bot = _padded_from_slabs_call("relayout_w_bot", bot8, 3 * dr + cfg["lora"], lp - cfg["lora"], cfg["wall"])
    (p_all,), (g_oa, g_ob, g_o) = _matmul(
        "mm_in_bot", xn[:, half:], w_bot, "nn", [F32], epi=lambda r, top: (r + top,), extras=(p_top,),
        comm=_Comm("gather", [shards["w_out_a"], shards["w_out_b"], shards["w_out"]]))
    w_out_a, w_out_b, w_out = _cols(g_oa), _cols(g_ob), g_o.reshape(d, d)
    pm = _mix_fwd("mix_fwd", p_all, wts["mu_pad"], wmix, cb)
    prep_w = (wts["w0"], wts["a0"], wts["k_k"], wts["k_a"], wts["wd"], wts["wi"], wts["wg"])
    lw, k2, a_in, b_in, g = _prep_fwd("prep_fwd", pm, cfg, *prep_w)
    (y_raw, s_chk), (g_u,) = _rec_fwd(
        "rec_fwd", pm, lw, k2, a_in, b_in, cfg, comm=_Comm("gather", [shards["w_mlp_up"]]))
    w_up = _cols(g_u)
    post_w = (wts["lnx_w"], wts["lnx_b"], wts["r_k"])
    ya_in = _post_fwd("post_fwd", y_raw, pm, k2, g, *post_w, cfg)
    (ya,) = _matmul("mm_out_a", ya_in, w_out_a, "nn", [F32])
    yb_in = _conv_fwd("conv_fwd", p_all, wts["conv_w8"], cfg)
    (yb,) = _matmul("mm_out_b", yb_in, w_out_b, "nn", [F32])
    mg = _merge_fwd("merge_fwd", p_all, wts["gate_bias"], ya, yb, cfg)
    (mo,) = _matmul("mm_out", mg, w_out, "nn", [F32])
    h1, hn = _norm_fwd("norm_mlp_fwd", x, mo, wts["norm_mlp_w"], True)
    (u, act), (g_d,) = _matmul("mm_up", hn, w_up, "nn", [F32, BF16],
                               epi=lambda r: (r, jnp.square(jnp.maximum(r, 0.0))),
                               comm=_Comm("gather", [shards["w_mlp_down"]]))
    w_down = g_d.reshape(dff, d)
    (md,) = _matmul("mm_down", act, w_down, "nn", [F32])
    loss, dh2, dh2b, g_norm_final = _final("final", h1, md, target, wts["norm_final_w"])
    (du,) = _matmul("mm_down_dx", dh2b, w_down, "nt", [BF16],
                    epi=lambda r, uu: (r * (2.0 * jnp.maximum(uu, 0.0)),), extras=(u,))
    (g_down,) = _matmul("mm_down_dw", act, dh2b, "tn", [BF16])
    core = lax.axis_index("c").astype(jnp.int32).reshape(1)
    my_chip = 2 * lax.axis_index("x") + lax.axis_index("y")
    me = 2 * my_chip + lax.axis_index("c")
    own, recv = {}, {}

    def chip_own(chip_sum):
        return lax.dynamic_index_in_dim(chip_sum, my_chip, axis=0, keepdims=False)

    down_slabs = g_down.reshape(N_DEV, dff // N_DEV, d)
    (dhn,), (got,) = _matmul("mm_up_dx", du, w_up, "nt", [F32], comm=_Comm("pair", [down_slabs]))
    down_sum = _pair_sum("pair_sum_down", down_slabs, got, core)
    (g_up,) = _matmul("mm_up_dw", hn, du, "tn", [BF16], out_slabs=True)
    own["w_mlp_down"] = chip_own(down_sum)
    dh1, dh1b, g_norm_mlp = _norm_bwd("norm_mlp_bwd", h1, dhn, dh2, wts["norm_mlp_w"])
    (dmg,), (got,) = _matmul("mm_out_dx", dh1b, w_out, "nt", [F32], comm=_Comm("pair", [g_up]))
    up_sum = _pair_sum("pair_sum_up", g_up, got, core)
    own["w_mlp_up"] = chip_own(up_sum)
    (g_out,) = _matmul("mm_out_dw", mg, dh1b, "tn", [BF16])
    dpga, dpgb, dya, dyb, dba, dbb = _merge_bwd("merge_bwd", p_all, wts["gate_bias"], ya, yb, dmg, cfg)
    (dya_in,) = _matmul("mm_out_a_dx", dya, w_out_a, "nt", [F32])
    (g_out_a,) = _matmul("mm_out_a_dw", ya_in, dya, "tn", [BF16], out_slabs=True)
    (dyb_in,) = _matmul("mm_out_b_dx", dyb, w_out_b, "nt", [F32])
    (g_out_b,) = _matmul("mm_out_b_dw", yb_in, dyb, "tn", [BF16], out_slabs=True)
    dpb, dpc, dpu, g_conv8 = _conv_bwd("conv_bwd", p_all, wts["conv_w8"], dyb_in, cfg)
    dy_raw, dr_post, dk_post, dv_post, dg, g_lnw, g_lnb, g_rk = _post_bwd(
        "post_bwd", y_raw, pm, k2, g, *post_w, dya_in, cfg)
    (dr_rec, dlw, dk_rec, dv_rec, da_in, db_in), (recv["w_mlp_up"], recv["w_mlp_down"]) = _rec_bwd(
        "rec_bwd", pm, lw, k2, a_in, b_in, s_chk, dy_raw, cfg, comm=_Comm("chips", [up_sum, down_sum]))
    (dpm_r, dpm_k, dpm_v, dpl, g_w0, g_a0, g_kk, g_ka, g_wd, g_wi, g_wg) = _prep_bwd(
        "prep_bwd", pm, cfg, *prep_w, (dlw, dk_rec, dk_post, da_in, db_in, dg),
        (dr_rec, dr_post), (dv_rec, dv_post))
    mu = wts["mu_pad"]
    nb = dr // cb
    dps, dmus = [], []
    for s, dpm_s in enumerate((dpm_r, dpm_k, dpm_v)):
        dp_s, dmu_s = _mix_bwd("mix_bwd_%d" % s, [dpm_s], p_all, s * nb, mu[:, s * dr:(s + 1) * dr], cb)
        dps.append(dp_s)
        dmus.append(dmu_s)
    dp_l, dmu_l = _mix_bwd("mix_bwd_l", [dpl[j] for j in range(nb)], p_all, 3 * nb, mu[:, 3 * dr:], min(cb, lp))
    tail = [jnp.zeros((x.shape[0], cfg["wall"] - cfg["used"]), BF16)] if cfg["wall"] > cfg["used"] else []
    dp_all = jnp.concatenate(dps + [dp_l, dpb, dpc, dpu, dpga, dpgb] + tail, axis=1)
    ld, li, lora = cfg["ld"], cfg["li"], cfg["lora"]
    g_small = jnp.concatenate([g_wd[:ld], g_wi[ld:ld + li], g_wg[ld + li:lora], g_conv8[:3]], axis=0)
    g_small = jnp.pad(g_small, ((0, cfg["small_rows"] - g_small.shape[0]), (0, 0)))
    direct = dict(w_out_a=g_out_a, w_out_b=g_out_b, w_out=g_out.reshape(N_DEV, d // N_DEV, d),
                  small=_col_slabs(g_small))
    (g_all,), got4 = _matmul("mm_in_dw", xn, dp_all, "tn", [BF16],
                             comm=_Comm("exchange", list(direct.values())))
    for n, slabs, r in zip(direct, direct.values(), got4):
        own[n] = lax.dynamic_index_in_dim(slabs, me, axis=0, keepdims=False)
        recv[n] = r
    in_slabs = _slabs_from_padded_call("relayout_g_in", g_all, 3 * dr + lora, lp - lora,
                                       (cfg["used"] - lp + lora) // N_DEV)
    up_outs, (got,) = _adamw("adamw_w_mlp_up", *adam_up, own["w_mlp_up"], recv["w_mlp_up"],
                             comm=_Comm("pair", [in_slabs]))
    in_sum = _pair_sum("pair_sum_in", in_slabs, got, core)
    (dxn,), (recv["w_in"],) = _matmul("mm_in_dx", dp_all, (wts["w_top"], w_bot), "nt", [F32],
                                      comm=_Comm("chips", [in_sum]))
    own["w_in"] = chip_own(in_sum)
    grad_x, _, g_norm_mix = _norm_bwd("norm_mix_bwd", x, dxn, dh1, wts["norm_mix_w"])
    grads = dict(
        norm_mix_w=g_norm_mix, gate_bias=jnp.concatenate([dba, dbb], axis=1),
        mu_pad=jnp.concatenate(dmus + [dmu_l], axis=1), w0=g_w0, a0=g_a0, k_k=g_kk, k_a=g_ka,
        r_k=g_rk, lnx_w=g_lnw, lnx_b=g_lnb, norm_mlp_w=g_norm_mlp, norm_final_w=g_norm_final)
    return loss, grad_x, grads, own, recv, up_outs


_SMALL = ("norm_mix_w", "gate_bias", "shift_mu", "w0", "a0", "k_k", "k_a", "r_k", "lnx_w", "lnx_b",
          "norm_mlp_w", "norm_final_w")
_ORDER = ("norm_mix_w", "w_in", "gate_bias", "shift_mu", "w0", "w_decay_up", "a0", "w_iclr_up", "w_gate_up",
          "k_k", "k_a", "r_k", "lnx_w", "lnx_b", "w_out_a", "conv_w", "w_out_b", "w_out", "norm_mlp_w",
          "w_mlp_up", "w_mlp_down", "norm_final_w")


def _step(x, target, w, m, v):
    t, d = x.shape[1], x.shape[2]
    dr = w["w0"].shape[-1]
    ld, li, lg = w["w_decay_up"].shape[1], w["w_iclr_up"].shape[1], w["w_gate_up"].shape[1]
    lora = ld + li + lg
    lp = _round_up(lora, LANES)
    dc = w["conv_w"].shape[-1] * N_DEV
    cb = math.gcd(math.gcd(lp, dr), 512)
    used = 3 * dr + lp + 3 * dc + 2 * d
    wall = _round_up(used, 1024 if used > MAX_FULL_K else LANES)
    small_rows = ld + li + lg + 3
    cfg = dict(d=d, dr=dr, dc=dc, lp=lp, cb=cb, off_conv=3 * dr + lp, off_gate=3 * dr + lp + 3 * dc, used=used,
               wall=wall, ld=ld, li=li, lora=lora, small_rows=_round_up(small_rows, SUBLANES))
    x2, tg2 = x[0], target[0]

    small_sh = jnp.concatenate([w["w_decay_up"][0], w["w_iclr_up"][0], w["w_gate_up"][0], w["conv_w"][0]], axis=0)
    small_sh = jnp.pad(small_sh, ((0, _round_up(small_rows, SUBLANES) - small_rows), (0, 0)))
    big = ("w_in",) + _MID
    w_in_b = w["w_in"][0].astype(BF16)
    top8, gsm = _comm_call("gather_weights", _Comm("gather", [w_in_b[:d // 2], small_sh]))
    shards = {n: w[n][0].astype(BF16) for n in _MID}
    shards["w_in_bot"] = w_in_b[d // 2:]
    w_top = _padded_from_slabs_call("relayout_w_top", top8, 3 * dr + lora, lp - lora, wall)
    sm = _cols(gsm)
    lora_full = sm[:lora]

    def lora_pad(lo, hi):
        rows = lax.broadcasted_iota(jnp.int32, (lp, 1), 0)
        full = jnp.pad(lora_full, ((0, lp - lora), (0, 0)))
        return jnp.where(jnp.logical_and(rows >= lo, rows < hi), full, 0.0)

    conv_w8 = jnp.pad(sm[lora:lora + 3], ((0, SUBLANES - 3), (0, 0)))
    mu_pad = jnp.pad(w["shift_mu"], ((0, 0), (0, lp - lora)))
    wts = dict(
        w_top=w_top, wd=lora_pad(0, ld), wi=lora_pad(ld, ld + li), wg=lora_pad(ld + li, lora), conv_w8=conv_w8,
        mu_pad=mu_pad, norm_mix_w=w["norm_mix_w"], gate_bias=w["gate_bias"], w0=w["w0"], a0=w["a0"],
        k_k=w["k_k"], k_a=w["k_a"], r_k=w["r_k"].reshape(1, dr), lnx_w=w["lnx_w"], lnx_b=w["lnx_b"],
        norm_mlp_w=w["norm_mlp_w"], norm_final_w=w["norm_final_w"].reshape(1, d))

    def shard2d(a):
        return a.reshape(-1, a.shape[-1])

    loss, grad_x, gr, own, received, up_outs = _local_step(
        x2, tg2, wts, shards, cfg, tuple(shard2d(d_["w_mlp_up"]) for d_ in (w, m, v)))

    small_g = dict(norm_mix_w=gr["norm_mix_w"], gate_bias=gr["gate_bias"], shift_mu=gr["mu_pad"][:, :3 * dr + lora],
                   w0=gr["w0"], a0=gr["a0"], k_k=gr["k_k"], k_a=gr["k_a"], r_k=gr["r_k"], lnx_w=gr["lnx_w"],
                   lnx_b=gr["lnx_b"], norm_mlp_w=gr["norm_mlp_w"], norm_final_w=gr["norm_final_w"])
    sizes = [small_g[n].size for n in _SMALL]
    total = sum(sizes) + 1
    prow = _round_up(total, LANES * SUBLANES) // LANES

    def pack(parts):
        flat = jnp.concatenate([p.reshape(-1) for p in parts])
        return jnp.pad(flat, (0, prow * LANES - flat.size)).reshape(prow, LANES)

    g_packed = _all_reduce_small("reduce_small", pack([small_g[n] for n in _SMALL] + [loss[0, :1]]))
    one = jnp.zeros((1,), F32)
    packed = [pack([d_[n] for n in _SMALL] + [one]) for d_ in (w, m, v)]
    sm_out = _adamw("adamw_small", *packed, g_packed)
    loss_out = g_packed.reshape(-1)[total - 1]

    def unpack(flat2d):
        flat = flat2d.reshape(-1)
        out, o = {}, 0
        for n, s in zip(_SMALL, sizes):
            out[n] = flat[o:o + s].reshape(w[n].shape)
            o += s
        return out

    res = [unpack(a) for a in sm_out]

    for n in big:
        outs = up_outs if n == "w_mlp_up" else _adamw(
            "adamw_" + n, shard2d(w[n]), shard2d(m[n]), shard2d(v[n]), shard2d(own[n]),
            received[n].reshape(received[n].shape[:1] + shard2d(own[n]).shape))
        for r_, o in zip(res, outs):
            r_[n] = o.reshape(w[n].shape)
    sm_names = ("w_decay_up", "w_iclr_up", "w_gate_up", "conv_w")
    stack = lambda d_: jnp.pad(jnp.concatenate([d_[n][0] for n in sm_names], axis=0),
                               ((0, _round_up(small_rows, SUBLANES) - small_rows), (0, 0)))
    outs = _adamw("adamw_stack", stack(w), stack(m), stack(v), own["small"], received["small"])
    bounds = (0, ld, ld + li, lora, lora + 3)
    for r_, o in zip(res, outs):
        for q, n in enumerate(sm_names):
            r_[n] = o[bounds[q]:bounds[q + 1]].reshape(w[n].shape)

    grad, delta, new_m, new_v = res
    return (loss_out, grad_x[None], *[grad[n] for n in _ORDER], *[delta[n] for n in _ORDER],
            *[new_m[n] for n in _ORDER], *[new_v[n] for n in _ORDER])


def kernel(x, norm_mix_w, w_in, gate_bias, shift_mu, w0, w_decay_up, a0, w_iclr_up, w_gate_up, k_k, k_a, r_k, lnx_w, lnx_b, w_out_a, conv_w, w_out_b, w_out, norm_mlp_w, w_mlp_up, w_mlp_down, norm_final_w, loss_target, m_norm_mix_w, m_w_in, m_gate_bias, m_shift_mu, m_w0, m_w_decay_up, m_a0, m_w_iclr_up, m_w_gate_up, m_k_k, m_k_a, m_r_k, m_lnx_w, m_lnx_b, m_w_out_a, m_conv_w, m_w_out_b, m_w_out, m_norm_mlp_w, m_w_mlp_up, m_w_mlp_down, m_norm_final_w, v_norm_mix_w, v_w_in, v_gate_bias, v_shift_mu, v_w0, v_w_decay_up, v_a0, v_w_iclr_up, v_w_gate_up, v_k_k, v_k_a, v_r_k, v_lnx_w, v_lnx_b, v_w_out_a, v_conv_w, v_w_out_b, v_w_out, v_norm_mlp_w, v_w_mlp_up, v_w_mlp_down, v_norm_final_w):
    w = dict(zip(_ORDER, (norm_mix_w, w_in, gate_bias, shift_mu, w0, w_decay_up, a0, w_iclr_up, w_gate_up, k_k, k_a,
                          r_k, lnx_w, lnx_b, w_out_a, conv_w, w_out_b, w_out, norm_mlp_w, w_mlp_up, w_mlp_down,
                          norm_final_w)))
    m = dict(zip(_ORDER, (m_norm_mix_w, m_w_in, m_gate_bias, m_shift_mu, m_w0, m_w_decay_up, m_a0, m_w_iclr_up,
                          m_w_gate_up, m_k_k, m_k_a, m_r_k, m_lnx_w, m_lnx_b, m_w_out_a, m_conv_w, m_w_out_b,
                          m_w_out, m_norm_mlp_w, m_w_mlp_up, m_w_mlp_down, m_norm_final_w)))
    v = dict(zip(_ORDER, (v_norm_mix_w, v_w_in, v_gate_bias, v_shift_mu, v_w0, v_w_decay_up, v_a0, v_w_iclr_up,
                          v_w_gate_up, v_k_k, v_k_a, v_r_k, v_lnx_w, v_lnx_b, v_w_out_a, v_conv_w, v_w_out_b,
                          v_w_out, v_norm_mlp_w, v_w_mlp_up, v_w_mlp_down, v_norm_final_w)))
    return _step(x, loss_target, w, m, v)
```

```python
import math

import jax
import jax.numpy as jnp
from jax import lax
from jax.experimental import pallas as pl
from jax.experimental.pallas import tpu as pltpu

F32 = jnp.float32
BF16 = jnp.bfloat16
MESH = pl.DeviceIdType.MESH

N_DEV = 8
HEAD = 64
LANES = 128
SUBLANES = 8
CHUNK = 64
RMS_EPS = 1e-5
LNX_EPS = 64e-5
L2_EPS = 1e-12
ADAM_LR = 0.001
ADAM_B1 = 0.9
ADAM_B2 = 0.999
ADAM_EPS = 1e-08
ADAM_WD = 0.01
ADAM_STEP = 10
VMEM_LIMIT = 48 * 1024 * 1024
MAX_FULL_K = 4096
LIGHT_ROWS = (512, 256, 128, 64, 32, 16, 8)


def _cparams(sem):
    return pltpu.CompilerParams(dimension_semantics=sem, vmem_limit_bytes=VMEM_LIMIT)


def _tile(dim, cands):
    for c in cands:
        if c <= dim and dim % c == 0:
            return c
    return dim


def _my_pos():
    return lax.axis_index("x"), lax.axis_index("y"), lax.axis_index("c")


def _peer(pos, r):
    x, y, c = pos
    return (1 - x if r & 4 else x, 1 - y if r & 2 else y, 1 - c if r & 1 else c)


def _slot(pos):
    return 4 * pos[0] + 2 * pos[1] + pos[2]


class _Comm:
    def __init__(self, kind, arrs):
        self.kind, self.arrs, self.n = kind, list(arrs), len(arrs)
        if kind == "gather":
            self.out_shape = [jax.ShapeDtypeStruct((N_DEV,) + a.shape, a.dtype) for a in arrs]
        elif kind == "exchange":
            self.out_shape = [jax.ShapeDtypeStruct((N_DEV - 1,) + a.shape[1:], a.dtype) for a in arrs]
        elif kind == "pair":
            self.out_shape = [jax.ShapeDtypeStruct((N_DEV // 2,) + a.shape[1:], a.dtype) for a in arrs]
        else:
            self.out_shape = [jax.ShapeDtypeStruct((3,) + a.shape[1:], a.dtype) for a in arrs]
        self.scratch = [pltpu.SemaphoreType.DMA((7 * self.n,)), pltpu.SemaphoreType.DMA((7 * self.n,))]
        if kind == "gather":
            self.scratch.append(pltpu.SemaphoreType.DMA((self.n,)))

    def _exchange_copies(self, in_refs, out_refs, sems):
        me = _my_pos()
        x, y, c = me
        cps = []
        for ai in range(self.n):
            if self.kind == "exchange":
                todo = [(in_refs[ai].at[_slot(_peer(me, r))], out_refs[ai].at[r - 1], _peer(me, r), r - 1)
                        for r in range(1, N_DEV)]
            elif self.kind == "pair":
                todo = [(in_refs[ai].at[2 * q + 1 - c], out_refs[ai].at[q], (x, y, 1 - c), q)
                        for q in range(N_DEV // 2)]
            else:
                chips = [(1 - x, y), (x, 1 - y), (1 - x, 1 - y)]
                todo = [(in_refs[ai].at[2 * cx + cy], out_refs[ai].at[j], (cx, cy, c), j)
                        for j, (cx, cy) in enumerate(chips)]
            for src, dst, to, k in todo:
                cps.append(pltpu.make_async_remote_copy(
                    src_ref=src, dst_ref=dst, send_sem=sems[0].at[ai * 7 + k], recv_sem=sems[1].at[ai * 7 + k],
                    device_id=to, device_id_type=MESH))
        return cps

    def _gather_parts(self, in_refs, out_refs, sems):
        x, y, c = _my_pos()
        me, sibling = (x, y, c), (x, y, 1 - c)
        chips = [(1 - x, y), (x, 1 - y), (1 - x, 1 - y)]

        def copy(ai, k, block, to, src=None):
            dst = out_refs[ai].at[_slot(block)]
            return pltpu.make_async_remote_copy(
                src_ref=dst if src is None else src, dst_ref=dst, send_sem=sems[0].at[ai * 7 + k],
                recv_sem=sems[1].at[ai * 7 + k], device_id=to, device_id_type=MESH)

        mine = [pltpu.make_async_copy(in_refs[ai], out_refs[ai].at[_slot(me)], sems[2].at[ai])
                for ai in range(self.n)]
        first = []
        for ai in range(self.n):
            first.append(copy(ai, 0, me, sibling, src=in_refs[ai]))
            first += [copy(ai, 1 + j, me, (*chip, c), src=in_refs[ai]) for j, chip in enumerate(chips)]
        return me, sibling, chips, c, copy, mine, first

    def start(self, in_refs, out_refs, sems):
        if self.kind != "gather":
            for cp in self._exchange_copies(in_refs, out_refs, sems):
                cp.start()
            return
        _, _, _, _, _, mine, first = self._gather_parts(in_refs, out_refs, sems)
        for cp in mine + first:
            cp.start()

    def wait(self, in_refs, out_refs, sems):
        if self.kind != "gather":
            for cp in self._exchange_copies(in_refs, out_refs, sems):
                cp.wait()
            return
        me, sibling, chips, c, copy, mine, first = self._gather_parts(in_refs, out_refs, sems)
        passed = []
        for ai in range(self.n):
            for j, chip in enumerate(chips):
                copy(ai, 1 + j, (*chip, c), me).wait_recv()
                fwd = copy(ai, 4 + j, (*chip, c), sibling)
                fwd.start()
                passed.append(fwd)
        for ai in range(self.n):
            copy(ai, 0, sibling, me).wait_recv()
            for j, chip in enumerate(chips):
                copy(ai, 4 + j, (*chip, 1 - c), me).wait_recv()
        for cp in first + passed:
            cp.wait_send()
        for cp in mine:
            cp.wait()


def _hosted_call(body, name, comm, first, last, *, args, in_specs, out_shape, out_specs, scratch, grid, sem):
    if comm is None:
        return pl.pallas_call(body, name=name, out_shape=out_shape, grid=grid, in_specs=in_specs, out_specs=out_specs,
                              scratch_shapes=scratch, compiler_params=_cparams(sem))(*args)
    ni, no, ns, nc = len(args), len(out_shape), len(scratch), comm.n
    hbm = pl.BlockSpec(memory_space=pl.ANY)

    def hosted(*refs):
        ins, cin = refs[:ni], refs[ni:ni + nc]
        outs, cout = refs[ni + nc:ni + nc + no], refs[ni + nc + no:ni + 2 * nc + no]
        scr, sems = refs[ni + 2 * nc + no:ni + 2 * nc + no + ns], refs[ni + 2 * nc + no + ns:]

        @pl.when(first())
        def _():
            comm.start(cin, cout, sems)

        body(*ins, *outs, *scr)

        @pl.when(last())
        def _():
            comm.wait(cin, cout, sems)

    res = pl.pallas_call(
        hosted, name=name, out_shape=list(out_shape) + comm.out_shape, grid=grid,
        in_specs=list(in_specs) + [hbm] * nc, out_specs=list(out_specs) + [hbm] * nc,
        scratch_shapes=list(scratch) + comm.scratch,
        compiler_params=_cparams(("arbitrary",) * len(grid)))(*args, *comm.arrs)
    return res[:no], res[no:]


_DIMS = {"nn": ((1,), (0,)), "nt": ((1,), (1,)), "tn": ((0,), (0,))}


def _matmul(name, a, b, mode, out_dtypes, epi=None, extras=(), comm=None, out_slabs=False):
    bs = list(b) if isinstance(b, (tuple, list)) else [b]
    if mode == "nn":
        (m, k), n = a.shape, b.shape[1]
    elif mode == "nt":
        (m, k), n = a.shape, sum(x.shape[0] for x in bs)
    else:
        (k, m), n = a.shape, b.shape[1]
    tm = _tile(m, (1024, 512, 256, 128, 64, 32, 16, 8))
    if k <= MAX_FULL_K:
        tk, tn = k, _tile(n // N_DEV if out_slabs else n // len(bs), (512, 256, 128))
    else:
        tk, tn = _tile(k, (2048, 1024, 512, 256, 128)), _tile(n // len(bs), (1024, 512, 256, 128))
    nk = k // tk
    gm, gn = m // tm, n // tn
    a_spec = pl.BlockSpec((tk, tm), lambda i, j, q: (q, i)) if mode == "tn" else pl.BlockSpec((tm, tk), lambda i, j, q: (i, q))
    b_spec = pl.BlockSpec((tn, tk), lambda i, j, q: (j, q)) if mode == "nt" else pl.BlockSpec((tk, tn), lambda i, j, q: (q, j))
    gh = gn // 2
    b_specs = [b_spec] if len(bs) == 1 else [
        pl.BlockSpec((tn, tk), lambda i, j, q: (jnp.minimum(j, gh - 1), q)),
        pl.BlockSpec((tn, tk), lambda i, j, q: (jnp.maximum(j - gh, 0), q))]
    mn_spec = pl.BlockSpec((tm, tn), lambda i, j, q: (i, j))
    per = n // N_DEV // tn if out_slabs else 0
    out_spec = pl.BlockSpec((None, tm, tn), lambda i, j, q: (j // per, i, j % per)) if out_slabs else mn_spec
    ne, no = len(extras), len(out_dtypes)
    dims = (_DIMS[mode], ((), ()))
    keep_t = mode == "tn" and nk == 1 and gn > 1

    def finish(r, extra_refs, out_refs):
        outs = (r,) if epi is None else epi(r, *[e[...] for e in extra_refs])
        for o_ref, o in zip(out_refs, outs):
            o_ref[...] = o.astype(o_ref.dtype)

    def body(a_ref, *rest):
        if len(bs) == 1:
            return step(a_ref, *rest)

        @pl.when(pl.program_id(1) < gh)
        def _():
            step(a_ref, rest[0], *rest[2:])

        @pl.when(pl.program_id(1) >= gh)
        def _():
            step(a_ref, rest[1], *rest[2:])

    def step(a_ref, b_ref, *rest):
        extra_refs, out_refs = rest[:ne], rest[ne:ne + no]
        if keep_t:
            at = rest[ne + no]

            @pl.when(pl.program_id(1) == 0)
            def _():
                at[...] = a_ref[...].T

            part = jnp.dot(at[...], b_ref[...], preferred_element_type=F32)
        else:
            part = lax.dot_general(a_ref[...], b_ref[...], dims, preferred_element_type=F32)
        if nk == 1:
            finish(part, extra_refs, out_refs)
            return
        acc = rest[ne + no]
        q = pl.program_id(2)

        @pl.when(q == 0)
        def _():
            acc[...] = part

        @pl.when(jnp.logical_and(q > 0, q < nk - 1))
        def _():
            acc[...] += part

        @pl.when(q == nk - 1)
        def _():
            finish(acc[...] + part, extra_refs, out_refs)

    def first():
        return jnp.logical_and(jnp.logical_and(pl.program_id(0) == 0, pl.program_id(1) == 0), pl.program_id(2) == 0)

    def last():
        return jnp.logical_and(jnp.logical_and(pl.program_id(0) == gm - 1, pl.program_id(1) == gn - 1),
                               pl.program_id(2) == nk - 1)

    return _hosted_call(
        body, name, comm, first, last,
        args=[a, *bs, *extras], in_specs=[a_spec] + b_specs + [mn_spec] * ne,
        out_shape=[jax.ShapeDtypeStruct((N_DEV, m, n // N_DEV) if out_slabs else (m, n), dt) for dt in out_dtypes],
        out_specs=[out_spec] * no,
        scratch=[pltpu.VMEM((tm, tn), F32)] if nk > 1 else ([pltpu.VMEM((tm, tk), a.dtype)] if keep_t else []),
        grid=(gm, gn, nk), sem=("parallel", "arbitrary" if keep_t else "parallel", "arbitrary"))


@jax.custom_vjp
def _mm(a, w):
    return jnp.dot(a.astype(BF16), w.astype(BF16), preferred_element_type=F32)


def _mm_fwd(a, w):
    return _mm(a, w), (a, w)


def _mm_bwd(res, ct):
    a, w = res
    ctb = ct.astype(BF16)
    da = lax.dot_general(ctb, w.astype(BF16), (((1,), (1,)), ((), ())), preferred_element_type=F32)
    dw = lax.dot_general(a.astype(BF16), ctb, (((0,), (0,)), ((), ())), preferred_element_type=F32)
    return da, dw


_mm.defvjp(_mm_fwd, _mm_bwd)


def _split3(x):
    hi = x.astype(BF16)
    r1 = x - hi.astype(F32)
    mid = r1.astype(BF16)
    lo = (r1 - mid.astype(F32)).astype(BF16)
    return hi, mid, lo


def _head_ones(width):
    r = lax.broadcasted_iota(jnp.int32, (width, width), 0) // HEAD
    c = lax.broadcasted_iota(jnp.int32, (width, width), 1) // HEAD
    return (r == c).astype(BF16)


@jax.custom_vjp
def _segsum(x):
    ones = _head_ones(x.shape[-1])
    out = None
    for piece in _split3(x):
        t = jnp.dot(piece, ones, preferred_element_type=F32)
        out = t if out is None else out + t
    return out


_segsum.defvjp(lambda x: (_segsum(x), None), lambda _, ct: (_segsum(ct),))


def _softplus(z):
    return jnp.maximum(z, 0.0) + jnp.log(1.0 + jnp.exp(-jnp.abs(z)))


def _sigmoid(z):
    return 1.0 / (1.0 + jnp.exp(-z))


def _rms(x, w):
    ms = jnp.mean(x * x, axis=-1, keepdims=True)
    return x * lax.rsqrt(ms + RMS_EPS) * w


def _row(ref, i):
    return ref[pl.ds(i, 1), :]


def _shift_down(x, prev_ref, n, first):
    rolled = pltpu.roll(x, n, 0)
    rows = lax.broadcasted_iota(jnp.int32, x.shape, 0)
    for q in range(n):
        halo = jnp.where(first, 0.0, _row(prev_ref, SUBLANES - n + q))
        rolled = jnp.where(rows == q, halo, rolled)
    return rolled


def _shift_up(x, next_ref, n, last):
    t = x.shape[0]
    rolled = pltpu.roll(x, t - n, 0)
    rows = lax.broadcasted_iota(jnp.int32, x.shape, 0)
    for q in range(n):
        halo = jnp.where(last, 0.0, _row(next_ref, q))
        rolled = jnp.where(rows == t - n + q, halo, rolled)
    return rolled


def _acc_out(ref, val, first):
    @pl.when(first)
    def _():
        ref[...] = val

    @pl.when(jnp.logical_not(first))
    def _():
        ref[...] += val


def _prep_fn(k, plm, w0, a0, kkw, kaw, wd, wi, wg):
    w_log = -_softplus(-(w0 + _mm(jnp.tanh(plm), wd))) - 0.5
    lw = -jnp.exp(w_log)
    a_g = _sigmoid(a0 + _mm(plm, wi))
    g = _mm(_sigmoid(plm), wg)
    kk = k * kkw
    kk = kk / jnp.maximum(jnp.sqrt(_segsum(kk * kk)), L2_EPS)
    k2 = k * (1.0 + (a_g - 1.0) * kaw)
    return lw, k2, -kk, kk * a_g, g


def _post_fn(y, r, k2, v, g, lnw, lnb, rk):
    mu = _segsum(y) * (1.0 / HEAD)
    yc = y - mu
    var = _segsum(yc * yc) * (1.0 / HEAD)
    yn = yc * lax.rsqrt(var + LNX_EPS) * lnw + lnb
    bonus = _segsum(r * k2 * rk) * v
    return (yn + bonus) * g


def _merge_fn(pga, pgb, ba, bb, ya, yb):
    return _sigmoid(pga + ba) * ya + _sigmoid(pgb + bb) * yb


_NN, _NT, _TN = ((2,), (1,)), ((2,), (2,)), ((1,), (1,))


def _dot3(a, b, dims):
    ah = a.astype(BF16)
    al = (a - ah.astype(F32)).astype(BF16)
    bh = b.astype(BF16)
    bl = (b - bh.astype(F32)).astype(BF16)
    dg = lambda p, q: lax.dot_general(p, q, (dims, ((0,), (0,))), preferred_element_type=F32)
    (ca,), (cb_,) = dims
    if a.shape[ca] % LANES:
        return dg(ah, bh) + (dg(ah, bl) + dg(al, bh))
    cross = dg(jnp.concatenate([ah, al], axis=ca), jnp.concatenate([bl, bh], axis=cb_))
    return dg(ah, bh) + cross


@jax.custom_vjp
def _dnn(a, b):
    return _dot3(a, b, _NN)


@jax.custom_vjp
def _dnt(a, b):
    return _dot3(a, b, _NT)


@jax.custom_vjp
def _dtn(a, b):
    return _dot3(a, b, _TN)


_dnn.defvjp(lambda a, b: (_dnn(a, b), (a, b)), lambda res, ct: (_dnt(ct, res[1]), _dtn(res[0], ct)))
_dnt.defvjp(lambda a, b: (_dnt(a, b), (a, b)), lambda res, ct: (_dnn(ct, res[1]), _dtn(ct, res[0])))
_dtn.defvjp(lambda a, b: (_dtn(a, b), (a, b)), lambda res, ct: (_dnt(res[1], ct), _dnn(res[0], ct)))


@jax.custom_vjp
def _unit_lower_inverse(low):
    n = low.shape[-1]
    ri = lax.broadcasted_iota(jnp.int32, low.shape, 1)
    ci = lax.broadcasted_iota(jnp.int32, low.shape, 2)
    inv = (ri == ci).astype(F32) + low
    pw = low
    for _ in range(int(math.log2(n // 2)) - 1):
        pw = _dnn(pw, pw)
        inv = inv + _dnn(inv, pw)
    return inv


def _unit_lower_inverse_bwd(inv, ct):
    return (_dnt(_dtn(inv, ct), inv),)


_unit_lower_inverse.defvjp(lambda low: (_unit_lower_inverse(low),) * 2, _unit_lower_inverse_bwd)


def _chunk_fn(s, r, lw, k, v, a, b):
    np_, c = r.shape[0], r.shape[1]
    c2 = 2 * c
    ri = lax.broadcasted_iota(jnp.int32, (np_, c, c), 1)
    ci = lax.broadcasted_iota(jnp.int32, (np_, c, c), 2)
    tri = (ri >= ci).astype(F32)
    cum = _dnn(tri, lw)
    tot = jnp.sum(lw, axis=1, keepdims=True)
    g_in, g_inv, g_out = jnp.exp(cum), jnp.exp(-cum), jnp.exp(tot - cum)
    lane_head = lax.broadcasted_iota(jnp.int32, (1, 2, 1, LANES), 3) // HEAD
    which = lax.broadcasted_iota(jnp.int32, (1, 2, 1, LANES), 1)
    hmask = (lane_head == which).astype(F32)

    def st(x):
        return (x[:, None] * hmask).reshape(np_, c2, LANES)

    r2, a2 = st(r * g_in), st(a * jnp.exp(cum - lw))
    b2, k2, v2 = st(b * g_inv), st(k * g_inv), st(v)
    bo2, ko2 = st(b * g_out), st(k * g_out)
    r2i = lax.broadcasted_iota(jnp.int32, (np_, c2, c2), 1)
    c2i = lax.broadcasted_iota(jnp.int32, (np_, c2, c2), 2)
    same = (r2i >= c) == (c2i >= c)
    strict = jnp.logical_and(same, r2i > c2i)
    incl = jnp.logical_and(same, r2i >= c2i)
    lab = jnp.where(strict, _dnt(a2, b2), 0.0)
    lak = jnp.where(strict, _dnt(a2, k2), 0.0)
    mrb = jnp.where(incl, _dnt(r2, b2), 0.0)
    mrk = jnp.where(incl, _dnt(r2, k2), 0.0)
    x2 = _dnt(a2, s) + _dnn(lak, v2)
    u2 = _dnn(_unit_lower_inverse(lab), x2)
    y2 = _dnt(r2, s) + _dnn(mrb, u2) + _dnn(mrk, v2)
    y = jnp.sum(y2.reshape(np_, 2, c, LANES), axis=1)
    s_new = s * jnp.exp(tot) + _dtn(u2, bo2) + _dtn(v2, ko2)
    return y, s_new


def _norm_fwd(name, x, add, w, want_sum, comm=None):
    t, d = x.shape
    tt = _tile(t, (128, 64, 32, 16, 8))
    row = pl.BlockSpec((tt, d), lambda i: (i, 0))
    par = pl.BlockSpec((1, d), lambda i: (0, 0))
    has_add = add is not None

    def body(*refs):
        x_ref = refs[0]
        add_ref = refs[1] if has_add else None
        w_ref = refs[1 + has_add]
        outs = refs[2 + has_add:]
        h = x_ref[...] + add_ref[...] if has_add else x_ref[...]
        if want_sum:
            outs[0][...] = h
        outs[-1][...] = _rms(h, w_ref[...]).astype(BF16)

    out_shape = ([jax.ShapeDtypeStruct((t, d), F32)] if want_sum else []) + [jax.ShapeDtypeStruct((t, d), BF16)]
    steps = t // tt
    return _hosted_call(
        body, name, comm, lambda: pl.program_id(0) == 0, lambda: pl.program_id(0) == steps - 1,
        args=[x] + ([add] if has_add else []) + [w], in_specs=[row] + ([row] if has_add else []) + [par],
        out_shape=out_shape, out_specs=[row] * len(out_shape), scratch=[], grid=(steps,), sem=("arbitrary",))


def _norm_bwd(name, xin, dy, dres, w):
    t, d = xin.shape
    tt = _tile(t, (128, 64, 32, 16, 8))
    row = pl.BlockSpec((tt, d), lambda i: (i, 0))
    par = pl.BlockSpec((1, d), lambda i: (0, 0))

    def body(x_ref, dy_ref, dres_ref, w_ref, dx_ref, dxb_ref, dw_ref):
        _, vjp = jax.vjp(_rms, x_ref[...], w_ref[...])
        dx, dw = vjp(dy_ref[...])
        dx = dx + dres_ref[...]
        dx_ref[...] = dx
        dxb_ref[...] = dx.astype(BF16)
        _acc_out(dw_ref, dw, pl.program_id(0) == 0)

    return pl.pallas_call(
        body, name=name,
        out_shape=[jax.ShapeDtypeStruct((t, d), F32), jax.ShapeDtypeStruct((t, d), BF16),
                   jax.ShapeDtypeStruct((1, d), F32)],
        grid=(t // tt,), in_specs=[row, row, row, par], out_specs=[row, row, par],
        compiler_params=_cparams(("arbitrary",)),
    )(xin, dy, dres, w)


def _final(name, h1, md, target, w):
    t, d = h1.shape
    tt = _tile(t, (128, 64, 32, 16, 8))
    row = pl.BlockSpec((tt, d), lambda i: (i, 0))
    par = pl.BlockSpec((1, d), lambda i: (0, 0))
    one = pl.BlockSpec((1, LANES), lambda i: (0, 0))

    def body(h1_ref, md_ref, tg_ref, w_ref, loss_ref, dh_ref, dhb_ref, dw_ref):
        tg = tg_ref[...]

        def f(h, wv):
            err = _rms(h, wv) - tg
            return 0.5 * jnp.sum(jnp.mean(err * err, axis=-1, keepdims=True), axis=0, keepdims=True)

        loss, vjp = jax.vjp(f, h1_ref[...] + md_ref[...], w_ref[...])
        dh, dw = vjp(jnp.ones((1, 1), F32))
        dh_ref[...] = dh
        dhb_ref[...] = dh.astype(BF16)
        first = pl.program_id(0) == 0
        _acc_out(dw_ref, dw, first)
        _acc_out(loss_ref, jnp.broadcast_to(loss, (1, LANES)), first)

    return pl.pallas_call(
        body, name=name,
        out_shape=[jax.ShapeDtypeStruct((1, LANES), F32), jax.ShapeDtypeStruct((t, d), F32),
                   jax.ShapeDtypeStruct((t, d), BF16), jax.ShapeDtypeStruct((1, d), F32)],
        grid=(t // tt,), in_specs=[row, row, row, par], out_specs=[one, row, row, par],
        compiler_params=_cparams(("arbitrary",)),
    )(h1, md, target, w)


def _halo_specs(tt, cb, nrow8, col_of):
    prev = pl.BlockSpec((SUBLANES, cb), lambda i, j: (jnp.maximum(i * (tt // SUBLANES) - 1, 0), col_of(j)))
    nxt = pl.BlockSpec((SUBLANES, cb), lambda i, j: (jnp.minimum((i + 1) * (tt // SUBLANES), nrow8 - 1), col_of(j)))
    return prev, nxt


def _mix_fwd(name, p_all, mu, width, cb):
    t = p_all.shape[0]
    tt = _tile(t, LIGHT_ROWS)
    main = pl.BlockSpec((tt, cb), lambda i, j: (i, j))
    prev, _ = _halo_specs(tt, cb, t // SUBLANES, lambda j: j)
    par = pl.BlockSpec((1, cb), lambda i, j: (0, j))

    def body(p_ref, prev_ref, mu_ref, o_ref):
        p = p_ref[...]
        o_ref[...] = p + (_shift_down(p, prev_ref, 1, pl.program_id(0) == 0) - p) * mu_ref[...]

    return pl.pallas_call(
        body, name=name, out_shape=jax.ShapeDtypeStruct((t, width), F32),
        grid=(t // tt, width // cb), in_specs=[main, prev, par], out_specs=main,
        compiler_params=_cparams(("arbitrary", "arbitrary")),
    )(p_all, p_all, mu)


def _mix_bwd(name, dpm_list, p_all, col0, mu, cb):
    t, width = dpm_list[0].shape
    tt = _tile(t, LIGHT_ROWS)
    n8 = t // SUBLANES
    nl = len(dpm_list)
    main = pl.BlockSpec((tt, cb), lambda j, i: (i, j))
    nxt = pl.BlockSpec((SUBLANES, cb), lambda j, i: (jnp.minimum((i + 1) * (tt // SUBLANES), n8 - 1), j))
    p_main = pl.BlockSpec((tt, cb), lambda j, i: (i, col0 + j))
    p_prev = pl.BlockSpec((SUBLANES, cb), lambda j, i: (jnp.maximum(i * (tt // SUBLANES) - 1, 0), col0 + j))
    par = pl.BlockSpec((1, cb), lambda j, i: (0, j))
    nt_ = t // tt

    def body(*refs):
        d_refs, dn_refs = refs[:nl], refs[nl:2 * nl]
        p_ref, pp_ref, mu_ref, dp_ref, dmu_ref, nx_scr = refs[2 * nl:]
        i = pl.program_id(1)
        dpm = d_refs[0][...]
        nx = dn_refs[0][...]
        for q in range(1, nl):
            dpm = dpm + d_refs[q][...]
            nx = nx + dn_refs[q][...]
        nx_scr[...] = nx
        mu_v = mu_ref[...]
        up = _shift_up(dpm, nx_scr, 1, i == nt_ - 1)
        dp_ref[...] = (dpm * (1.0 - mu_v) + up * mu_v).astype(BF16)
        p = p_ref[...]
        diff = _shift_down(p, pp_ref, 1, i == 0) - p
        _acc_out(dmu_ref, jnp.sum(dpm * diff, axis=0, keepdims=True), i == 0)

    return pl.pallas_call(
        body, name=name,
        out_shape=[jax.ShapeDtypeStruct((t, width), BF16), jax.ShapeDtypeStruct((1, width), F32)],
        grid=(width // cb, nt_),
        in_specs=[main] * nl + [nxt] * nl + [p_main, p_prev, par],
        out_specs=[main, par],
        scratch_shapes=[pltpu.VMEM((SUBLANES, cb), F32)],
        compiler_params=_cparams(("arbitrary", "arbitrary")),
    )(*dpm_list, *dpm_list, p_all, p_all, mu)


def _prep_fwd(name, pm, cfg, w0, a0, kkw, kaw, wd, wi, wg):
    t = pm.shape[0]
    dr, lp, cb = cfg["dr"], cfg["lp"], cfg["cb"]
    tt = _tile(t, (256, 128, 64, 32, 16, 8))
    nj = dr // cb
    kspec = pl.BlockSpec((tt, cb), lambda j, i: (i, nj + j))
    lspec = pl.BlockSpec((tt, lp), lambda j, i: (i, 3 * dr // lp))
    par = pl.BlockSpec((1, cb), lambda j, i: (0, j))
    wspec = pl.BlockSpec((lp, cb), lambda j, i: (0, j))
    out = pl.BlockSpec((tt, cb), lambda j, i: (i, j))

    def body(k_ref, l_ref, w0_ref, a0_ref, kk_ref, ka_ref, wd_ref, wi_ref, wg_ref, *outs):
        vals = _prep_fn(k_ref[...], l_ref[...], w0_ref[...], a0_ref[...], kk_ref[...], ka_ref[...],
                        wd_ref[...], wi_ref[...], wg_ref[...])
        for o_ref, val in zip(outs, vals):
            o_ref[...] = val

    return pl.pallas_call(
        body, name=name, out_shape=[jax.ShapeDtypeStruct((t, dr), F32)] * 5,
        grid=(nj, t // tt), in_specs=[kspec, lspec, par, par, par, par, wspec, wspec, wspec],
        out_specs=[out] * 5, compiler_params=_cparams(("arbitrary", "arbitrary")),
    )(pm, pm, w0, a0, kkw, kaw, wd, wi, wg)


def _prep_bwd(name, pm, cfg, w0, a0, kkw, kaw, wd, wi, wg, cts, dr_parts, dv_parts):
    t = pm.shape[0]
    dr, lp, cb = cfg["dr"], cfg["lp"], cfg["cb"]
    tt = _tile(t, (256, 128, 64, 32, 16, 8))
    nj = dr // cb
    kspec = pl.BlockSpec((tt, cb), lambda j, i: (i, nj + j))
    lspec = pl.BlockSpec((tt, lp), lambda j, i: (i, 3 * dr // lp))
    par = pl.BlockSpec((1, cb), lambda j, i: (0, j))
    wspec = pl.BlockSpec((lp, cb), lambda j, i: (0, j))
    blk = pl.BlockSpec((tt, cb), lambda j, i: (i, j))
    dpl_spec = pl.BlockSpec((None, tt, lp), lambda j, i: (j, i, 0))

    def body(k_ref, l_ref, w0_ref, a0_ref, kk_ref, ka_ref, wd_ref, wi_ref, wg_ref,
             dlw_ref, dk2a_ref, dk2b_ref, da_ref, db_ref, dg_ref, dr0_ref, dr1_ref, dv0_ref, dv1_ref,
             dpr_ref, dpk_ref, dpv_ref, dpl_ref, dw0_ref, da0_ref, dkk_ref, dka_ref, dwd_ref, dwi_ref, dwg_ref):
        _, vjp = jax.vjp(_prep_fn, k_ref[...], l_ref[...], w0_ref[...], a0_ref[...], kk_ref[...], ka_ref[...],
                         wd_ref[...], wi_ref[...], wg_ref[...])
        dk, dpl, dw0, da0, dkk, dka, dwd, dwi, dwg = vjp(
            (dlw_ref[...], dk2a_ref[...] + dk2b_ref[...], da_ref[...], db_ref[...], dg_ref[...]))
        dpr_ref[...] = dr0_ref[...] + dr1_ref[...]
        dpv_ref[...] = dv0_ref[...] + dv1_ref[...]
        dpk_ref[...] = dk
        dpl_ref[...] = dpl
        first = pl.program_id(1) == 0
        for ref, val in ((dw0_ref, dw0), (da0_ref, da0), (dkk_ref, dkk), (dka_ref, dka),
                         (dwd_ref, dwd), (dwi_ref, dwi), (dwg_ref, dwg)):
            _acc_out(ref, val, first)

    out_shape = ([jax.ShapeDtypeStruct((t, dr), F32)] * 3 + [jax.ShapeDtypeStruct((nj, t, lp), F32)]
                 + [jax.ShapeDtypeStruct((1, dr), F32)] * 4 + [jax.ShapeDtypeStruct((lp, dr), F32)] * 3)
    return pl.pallas_call(
        body, name=name, out_shape=out_shape, grid=(nj, t // tt),
        in_specs=[kspec, lspec, par, par, par, par, wspec, wspec, wspec] + [blk] * 10,
        out_specs=[blk] * 3 + [dpl_spec] + [par] * 4 + [wspec] * 3,
        compiler_params=_cparams(("arbitrary", "arbitrary")),
    )(pm, pm, w0, a0, kkw, kaw, wd, wi, wg, *cts, *dr_parts, *dv_parts)


def _post_specs(t, cfg):
    dr, cb = cfg["dr"], cfg["cb"]
    tt = _tile(t, (256, 128, 64, 32, 16, 8))
    nj = dr // cb
    blk = pl.BlockSpec((tt, cb), lambda j, i: (i, j))
    rspec = pl.BlockSpec((tt, cb), lambda j, i: (i, j))
    vspec = pl.BlockSpec((tt, cb), lambda j, i: (i, 2 * nj + j))
    par = pl.BlockSpec((1, cb), lambda j, i: (0, j))
    return tt, nj, blk, rspec, vspec, par


def _post_fwd(name, y, pm, k2, g, lnw, lnb, rk, cfg):
    t = y.shape[0]
    tt, nj, blk, rspec, vspec, par = _post_specs(t, cfg)

    def body(y_ref, r_ref, k_ref, v_ref, g_ref, lw_ref, lb_ref, rk_ref, o_ref):
        o_ref[...] = _post_fn(y_ref[...], r_ref[...], k_ref[...], v_ref[...], g_ref[...],
                              lw_ref[...], lb_ref[...], rk_ref[...]).astype(BF16)

    return pl.pallas_call(
        body, name=name, out_shape=jax.ShapeDtypeStruct((t, cfg["dr"]), BF16), grid=(nj, t // tt),
        in_specs=[blk, rspec, blk, vspec, blk, par, par, par], out_specs=blk,
        compiler_params=_cparams(("arbitrary", "arbitrary")),
    )(y, pm, k2, pm, g, lnw, lnb, rk)


def _post_bwd(name, y, pm, k2, g, lnw, lnb, rk, dout, cfg):
    t = y.shape[0]
    tt, nj, blk, rspec, vspec, par = _post_specs(t, cfg)

    def body(y_ref, r_ref, k_ref, v_ref, g_ref, lw_ref, lb_ref, rk_ref, do_ref,
             dy_ref, dr_ref, dk_ref, dv_ref, dg_ref, dlw_ref, dlb_ref, drk_ref):
        _, vjp = jax.vjp(_post_fn, y_ref[...], r_ref[...], k_ref[...], v_ref[...], g_ref[...],
                         lw_ref[...], lb_ref[...], rk_ref[...])
        dy, dr, dk, dv, dg, dlw, dlb, drk = vjp(do_ref[...])
        for ref, val in ((dy_ref, dy), (dr_ref, dr), (dk_ref, dk), (dv_ref, dv), (dg_ref, dg)):
            ref[...] = val
        first = pl.program_id(1) == 0
        for ref, val in ((dlw_ref, dlw), (dlb_ref, dlb), (drk_ref, drk)):
            _acc_out(ref, val, first)

    dr = cfg["dr"]
    return pl.pallas_call(
        body, name=name,
        out_shape=[jax.ShapeDtypeStruct((t, dr), F32)] * 5 + [jax.ShapeDtypeStruct((1, dr), F32)] * 3,
        grid=(nj, t // tt),
        in_specs=[blk, rspec, blk, vspec, blk, par, par, par, blk],
        out_specs=[blk] * 5 + [par] * 3,
        compiler_params=_cparams(("arbitrary", "arbitrary")),
    )(y, pm, k2, pm, g, lnw, lnb, rk, dout)


def _conv_specs(t, cfg):
    dc, cb = cfg["dc"], cfg["cb"]
    tt = _tile(t, LIGHT_ROWS)
    nj = dc // cb
    c0 = cfg["off_conv"] // cb
    n8 = t // SUBLANES

    def sect(s):
        col = lambda j: c0 + s * nj + j
        main = pl.BlockSpec((tt, cb), lambda j, i: (i, col(j)))
        prev = pl.BlockSpec((SUBLANES, cb), lambda j, i: (jnp.maximum(i * (tt // SUBLANES) - 1, 0), col(j)))
        nxt = pl.BlockSpec((SUBLANES, cb), lambda j, i: (jnp.minimum((i + 1) * (tt // SUBLANES), n8 - 1), col(j)))
        return main, prev, nxt

    blk = pl.BlockSpec((tt, cb), lambda j, i: (i, j))
    wspec = pl.BlockSpec((SUBLANES, cb), lambda j, i: (0, j))
    return tt, nj, n8, sect, blk, wspec


def _conv_fwd(name, p_all, cw8, cfg):
    t = p_all.shape[0]
    tt, nj, n8, sect, blk, wspec = _conv_specs(t, cfg)
    (bm, _, _), (cm, cp, _), (um, up, _) = sect(0), sect(1), sect(2)

    def body(b_ref, c_ref, cp_ref, u_ref, up_ref, w_ref, o_ref, zp_scr):
        first = pl.program_id(1) == 0
        z = c_ref[...] * u_ref[...]
        zp_scr[...] = cp_ref[...] * up_ref[...]
        o = _row(w_ref, 2) * z + _row(w_ref, 1) * _shift_down(z, zp_scr, 1, first) \
            + _row(w_ref, 0) * _shift_down(z, zp_scr, 2, first)
        o_ref[...] = (b_ref[...] * o).astype(BF16)

    return pl.pallas_call(
        body, name=name, out_shape=jax.ShapeDtypeStruct((t, cfg["dc"]), BF16), grid=(nj, t // tt),
        in_specs=[bm, cm, cp, um, up, wspec], out_specs=blk,
        scratch_shapes=[pltpu.VMEM((SUBLANES, blk.block_shape[1]), F32)],
        compiler_params=_cparams(("arbitrary", "arbitrary")),
    )(p_all, p_all, p_all, p_all, p_all, cw8)


def _conv_bwd(name, p_all, cw8, dyb, cfg):
    t = p_all.shape[0]
    tt, nj, n8, sect, blk, wspec = _conv_specs(t, cfg)
    (bm, _, bn), (cm, cp, _), (um, up, _) = sect(0), sect(1), sect(2)
    cb = blk.block_shape[1]
    dnxt = pl.BlockSpec((SUBLANES, cb), lambda j, i: (jnp.minimum((i + 1) * (tt // SUBLANES), n8 - 1), j))
    nt_ = t // tt

    def body(b_ref, bn_ref, c_ref, cp_ref, u_ref, up_ref, w_ref, d_ref, dn_ref,
             db_ref, dc_ref, du_ref, dw_ref, zp_scr, don_scr):
        i = pl.program_id(1)
        first, last = i == 0, i == nt_ - 1
        c, u, b, dy = c_ref[...], u_ref[...], b_ref[...], d_ref[...]
        z = c * u
        zp_scr[...] = cp_ref[...] * up_ref[...]
        z1 = _shift_down(z, zp_scr, 1, first)
        z2 = _shift_down(z, zp_scr, 2, first)
        w0, w1, w2 = _row(w_ref, 0), _row(w_ref, 1), _row(w_ref, 2)
        o = w2 * z + w1 * z1 + w0 * z2
        do = dy * b
        don_scr[...] = dn_ref[...] * bn_ref[...]
        dz = w2 * do + w1 * _shift_up(do, don_scr, 1, last) + w0 * _shift_up(do, don_scr, 2, last)
        db_ref[...] = (dy * o).astype(BF16)
        dc_ref[...] = (dz * u).astype(BF16)
        du_ref[...] = (dz * c).astype(BF16)
        rows = lax.broadcasted_iota(jnp.int32, (SUBLANES, cb), 0)
        s0 = jnp.sum(do * z2, axis=0, keepdims=True)
        s1 = jnp.sum(do * z1, axis=0, keepdims=True)
        s2 = jnp.sum(do * z, axis=0, keepdims=True)
        dw = jnp.where(rows == 0, s0, jnp.where(rows == 1, s1, jnp.where(rows == 2, s2, 0.0)))
        _acc_out(dw_ref, dw, first)

    dc = cfg["dc"]
    return pl.pallas_call(
        body, name=name,
        out_shape=[jax.ShapeDtypeStruct((t, dc), BF16)] * 3 + [jax.ShapeDtypeStruct((SUBLANES, dc), F32)],
        grid=(nj, nt_),
        in_specs=[bm, bn, cm, cp, um, up, wspec, blk, dnxt],
        out_specs=[blk] * 3 + [wspec],
        scratch_shapes=[pltpu.VMEM((SUBLANES, cb), F32), pltpu.VMEM((SUBLANES, cb), F32)],
        compiler_params=_cparams(("arbitrary", "arbitrary")),
    )(p_all, p_all, p_all, p_all, p_all, p_all, cw8, dyb, dyb)


def _merge_specs(t, cfg):
    d, cb = cfg["d"], cfg["cb"]
    tt = _tile(t, LIGHT_ROWS)
    nj = d // cb
    g0 = cfg["off_gate"] // cb
    ga = pl.BlockSpec((tt, cb), lambda j, i: (i, g0 + j))
    gb = pl.BlockSpec((tt, cb), lambda j, i: (i, g0 + nj + j))
    ba = pl.BlockSpec((1, cb), lambda j, i: (0, j))
    bb = pl.BlockSpec((1, cb), lambda j, i: (0, nj + j))
    blk = pl.BlockSpec((tt, cb), lambda j, i: (i, j))
    return tt, nj, ga, gb, ba, bb, blk


def _merge_fwd(name, p_all, bias, ya, yb, cfg):
    t = p_all.shape[0]
    tt, nj, ga, gb, ba, bb, blk = _merge_specs(t, cfg)

    def body(ga_ref, gb_ref, ba_ref, bb_ref, ya_ref, yb_ref, o_ref):
        o_ref[...] = _merge_fn(ga_ref[...], gb_ref[...], ba_ref[...], bb_ref[...],
                               ya_ref[...], yb_ref[...]).astype(BF16)

    return pl.pallas_call(
        body, name=name, out_shape=jax.ShapeDtypeStruct((t, cfg["d"]), BF16), grid=(nj, t // tt),
        in_specs=[ga, gb, ba, bb, blk, blk], out_specs=blk,
        compiler_params=_cparams(("arbitrary", "arbitrary")),
    )(p_all, p_all, bias, bias, ya, yb)


def _merge_bwd(name, p_all, bias, ya, yb, dm, cfg):
    t = p_all.shape[0]
    tt, nj, ga, gb, ba, bb, blk = _merge_specs(t, cfg)

    def body(ga_ref, gb_ref, ba_ref, bb_ref, ya_ref, yb_ref, dm_ref,
             dga_ref, dgb_ref, dya_ref, dyb_ref, dba_ref, dbb_ref):
        _, vjp = jax.vjp(_merge_fn, ga_ref[...], gb_ref[...], ba_ref[...], bb_ref[...], ya_ref[...], yb_ref[...])
        dga, dgb, dba, dbb, dya, dyb = vjp(dm_ref[...])
        for ref, val in ((dga_ref, dga), (dgb_ref, dgb), (dya_ref, dya), (dyb_ref, dyb)):
            ref[...] = val.astype(BF16)
        first = pl.program_id(1) == 0
        _acc_out(dba_ref, dba, first)
        _acc_out(dbb_ref, dbb, first)

    d = cfg["d"]
    par = pl.BlockSpec((1, blk.block_shape[1]), lambda j, i: (0, j))
    return pl.pallas_call(
        body, name=name,
        out_shape=[jax.ShapeDtypeStruct((t, d), BF16)] * 4 + [jax.ShapeDtypeStruct((1, d), F32)] * 2,
        grid=(nj, t // tt),
        in_specs=[ga, gb, ba, bb, blk, blk, blk], out_specs=[blk] * 4 + [par] * 2,
        compiler_params=_cparams(("arbitrary", "arbitrary")),
    )(p_all, p_all, bias, bias, ya, yb, dm)


PAIRS = 8


def _pair_stack(ref, pairs):
    return jnp.stack([ref[:, p * LANES:(p + 1) * LANES] for p in range(pairs)])


def _pair_store(ref, val):
    for p in range(val.shape[0]):
        ref[:, p * LANES:(p + 1) * LANES] = val[p]


def _rec_specs(t, cfg, rev):
    dr = cfg["dr"]
    nc = t // CHUNK
    hp = dr // LANES
    pairs = _tile(hp, (PAIRS, 2, 1))
    ng = hp // pairs
    w = LANES * pairs
    ch = (lambda c: nc - 1 - c) if rev else (lambda c: c)
    slab = pl.BlockSpec((CHUNK, w), lambda h, c: (ch(c), h))
    vspec = pl.BlockSpec((CHUNK, w), lambda h, c: (ch(c), 2 * ng + h))
    sspec = pl.BlockSpec((None, pairs, LANES, LANES), lambda h, c: (ch(c), h, 0, 0))
    first = lambda: jnp.logical_and(pl.program_id(0) == 0, pl.program_id(1) == 0)
    last = lambda: jnp.logical_and(pl.program_id(0) == ng - 1, pl.program_id(1) == nc - 1)
    return nc, hp, pairs, ng, slab, vspec, sspec, first, last


def _rec_fwd(name, pm, lw, k2, a, b, cfg, comm=None):
    t = pm.shape[0]
    nc, hp, pairs, ng, slab, vspec, sspec, first, last = _rec_specs(t, cfg, False)

    def body(r_ref, lw_ref, k_ref, v_ref, a_ref, b_ref, y_ref, s_ref, s_scr):
        @pl.when(pl.program_id(1) == 0)
        def _():
            s_scr[...] = jnp.zeros_like(s_scr)

        s = s_scr[...]
        s_ref[...] = s
        y, s_new = _chunk_fn(s, *[_pair_stack(ref, pairs) for ref in (r_ref, lw_ref, k_ref, v_ref, a_ref, b_ref)])
        _pair_store(y_ref, y)
        s_scr[...] = s_new

    return _hosted_call(
        body, name, comm, first, last, args=[pm, lw, k2, pm, a, b],
        in_specs=[slab, slab, slab, vspec, slab, slab],
        out_shape=[jax.ShapeDtypeStruct((t, cfg["dr"]), F32), jax.ShapeDtypeStruct((nc, hp, LANES, LANES), F32)],
        out_specs=[slab, sspec], scratch=[pltpu.VMEM((pairs, LANES, LANES), F32)], grid=(ng, nc),
        sem=("arbitrary", "arbitrary"))


def _rec_bwd(name, pm, lw, k2, a, b, s_chk, dy, cfg, comm=None):
    t = pm.shape[0]
    nc, hp, pairs, ng, slab, vspec, sspec, first, last = _rec_specs(t, cfg, True)

    def body(r_ref, lw_ref, k_ref, v_ref, a_ref, b_ref, s_ref, dy_ref,
             dr_ref, dlw_ref, dk_ref, dv_ref, da_ref, db_ref, ds_scr):
        @pl.when(pl.program_id(1) == 0)
        def _():
            ds_scr[...] = jnp.zeros_like(ds_scr)

        _, vjp = jax.vjp(_chunk_fn, s_ref[...],
                         *[_pair_stack(ref, pairs) for ref in (r_ref, lw_ref, k_ref, v_ref, a_ref, b_ref)])
        ds, dr, dlw, dk, dv, da, db = vjp((_pair_stack(dy_ref, pairs), ds_scr[...]))
        ds_scr[...] = ds
        for ref, val in ((dr_ref, dr), (dlw_ref, dlw), (dk_ref, dk), (dv_ref, dv), (da_ref, da), (db_ref, db)):
            _pair_store(ref, val)

    return _hosted_call(
        body, name, comm, first, last, args=[pm, lw, k2, pm, a, b, s_chk, dy],
        in_specs=[slab, slab, slab, vspec, slab, slab, sspec, slab],
        out_shape=[jax.ShapeDtypeStruct((t, cfg["dr"]), F32)] * 6, out_specs=[slab] * 6,
        scratch=[pltpu.VMEM((pairs, LANES, LANES), F32)], grid=(ng, nc), sem=("arbitrary", "arbitrary"))


def _comm_call(name, comm):
    n = comm.n
    hbm = pl.BlockSpec(memory_space=pl.ANY)

    def body(*refs):
        comm.start(refs[:n], refs[n:2 * n], refs[2 * n:])
        comm.wait(refs[:n], refs[n:2 * n], refs[2 * n:])

    return pl.pallas_call(body, name=name, out_shape=comm.out_shape, in_specs=[hbm] * n, out_specs=[hbm] * n,
                          scratch_shapes=comm.scratch)(*comm.arrs)


def _all_reduce_small(name, v):
    rows = v.shape[0]
    vm = pl.BlockSpec(memory_space=pltpu.VMEM)

    def body(x_ref, out_ref, buf, send_sems, recv_sems):
        x, y, c = _my_pos()
        me, sibling = (x, y, c), (x, y, 1 - c)
        chips = [(1 - x, y), (x, 1 - y), (1 - x, 1 - y)]

        def copy(k, block, to, src=None):
            px, py, pc = block
            dst = buf.at[4 * px + 2 * py + pc]
            return pltpu.make_async_remote_copy(
                src_ref=dst if src is None else src, dst_ref=dst,
                send_sem=send_sems.at[k], recv_sem=recv_sems.at[k], device_id=to, device_id_type=MESH)

        buf[4 * x + 2 * y + c] = x_ref[...]
        first = [copy(0, me, sibling, src=x_ref)]
        first += [copy(1 + j, me, (*chip, c), src=x_ref) for j, chip in enumerate(chips)]
        for cp in first:
            cp.start()
        passed = [copy(4 + j, (*chip, c), sibling) for j, chip in enumerate(chips)]
        for j, chip in enumerate(chips):
            copy(1 + j, (*chip, c), me).wait_recv()
            passed[j].start()
        copy(0, sibling, me).wait_recv()
        for j, chip in enumerate(chips):
            copy(4 + j, (*chip, 1 - c), me).wait_recv()
        for cp in first + passed:
            cp.wait_send()
        acc = buf[0]
        for d in range(1, N_DEV):
            acc = acc + buf[d]
        out_ref[...] = acc

    return pl.pallas_call(
        body, name=name, out_shape=jax.ShapeDtypeStruct(v.shape, F32),
        in_specs=[vm], out_specs=vm,
        scratch_shapes=[pltpu.VMEM((N_DEV, rows, LANES), F32), pltpu.SemaphoreType.DMA((7,)),
                        pltpu.SemaphoreType.DMA((7,))],
    )(v)


def _pair_sum(name, slabs, got, core):
    _, rows, cols = slabs.shape
    nq = got.shape[0]
    rb = _tile(rows, (256, 128, 64, 32, 16, 8))
    mine = pl.BlockSpec((None, rb, cols), lambda q, j, c_ref: (2 * q + c_ref[0], j, 0))
    blk = pl.BlockSpec((None, rb, cols), lambda q, j, c_ref: (q, j, 0))

    def body(c_ref, a_ref, b_ref, o_ref):
        o_ref[...] = (a_ref[...].astype(F32) + b_ref[...].astype(F32)).astype(o_ref.dtype)

    return pl.pallas_call(
        body, name=name, out_shape=jax.ShapeDtypeStruct(got.shape, got.dtype),
        grid_spec=pltpu.PrefetchScalarGridSpec(num_scalar_prefetch=1, grid=(nq, rows // rb),
                                               in_specs=[mine, blk], out_specs=blk),
        compiler_params=_cparams(("arbitrary", "arbitrary")))(core, slabs, got)


def _adamw(name, w, m, v, g_own, g_recv=None):
    rows, cols = w.shape
    nr = g_recv.shape[0] if g_recv is not None else 0
    per_el = 4 * 3 + g_own.dtype.itemsize + (nr * g_recv.dtype.itemsize if nr else 0) + 16
    rb = SUBLANES * 2
    while rb * 2 <= rows and rows % (rb * 2) == 0 and rb * 2 * cols * per_el * 2 <= VMEM_LIMIT // 2:
        rb *= 2
    if rows % rb:
        rb = rows
    blk = pl.BlockSpec((rb, cols), lambda i: (i, 0))
    rblk = pl.BlockSpec((max(nr, 1), rb, cols), lambda i: (0, i, 0))
    has_r = g_recv is not None
    bc1 = 1.0 - ADAM_B1 ** ADAM_STEP
    bc2 = 1.0 - ADAM_B2 ** ADAM_STEP

    def body(*refs):
        w_ref, m_ref, v_ref, go_ref = refs[:4]
        gr_ref = refs[4] if has_r else None
        g_out, d_out, m_out, v_out = refs[4 + has_r:]
        g = go_ref[...].astype(F32)
        if has_r:
            for r in range(nr):
                g = g + gr_ref[r].astype(F32)
        mn = ADAM_B1 * m_ref[...] + (1.0 - ADAM_B1) * g
        vn = ADAM_B2 * v_ref[...] + (1.0 - ADAM_B2) * (g * g)
        m_hat = mn / bc1
        v_hat = vn / bc2
        g_out[...] = g
        d_out[...] = -ADAM_LR * (m_hat / (jnp.sqrt(v_hat) + ADAM_EPS) + ADAM_WD * w_ref[...])
        m_out[...] = mn
        v_out[...] = vn

    return pl.pallas_call(
        body, name=name, out_shape=[jax.ShapeDtypeStruct((rows, cols), F32)] * 4, grid=(rows // rb,),
        in_specs=[blk] * 4 + ([rblk] if has_r else []), out_specs=[blk] * 4,
        compiler_params=_cparams(("arbitrary",)),
    )(*([w, m, v, g_own] + ([g_recv] if has_r else [])))


def _round_up(n, q):
    return (n + q - 1) // q * q


def _cols(a8):
    return jnp.transpose(a8, (1, 0, 2)).reshape(a8.shape[1], -1)


def _col_slabs(a):
    r_, c_ = a.shape
    return jnp.transpose(a.reshape(r_, N_DEV, c_ // N_DEV), (1, 0, 2))


def _padded_from_slabs_call(name, slabs, gap_at, gap, total):
    _, rows, c8 = slabs.shape
    tr = _tile(rows, (128, 64, 32, 16))

    def body(s_ref, o_ref):
        o_ref[...] = jnp.zeros_like(o_ref)
        for dd in range(N_DEV):
            lo, hi = dd * c8, (dd + 1) * c8
            if gap and lo <= gap_at < hi:
                cut = gap_at - lo
                if cut:
                    o_ref[:, lo:gap_at] = s_ref[dd, :, :cut]
                o_ref[:, gap_at + gap:hi + gap] = s_ref[dd, :, cut:]
            else:
                off = lo + (gap if lo >= gap_at else 0)
                o_ref[:, off:off + c8] = s_ref[dd]

    return pl.pallas_call(
        body, name=name, out_shape=jax.ShapeDtypeStruct((rows, total), slabs.dtype), grid=(rows // tr,),
        in_specs=[pl.BlockSpec((N_DEV, tr, c8), lambda i: (0, i, 0))], out_specs=pl.BlockSpec((tr, total), lambda i: (i, 0)),
        compiler_params=_cparams(("arbitrary",)))(slabs)


def _slabs_from_padded_call(name, mat, gap_at, gap, c8):
    rows, total = mat.shape
    tr = _tile(rows, (128, 64, 32, 16))

    def body(m_ref, o_ref):
        for dd in range(N_DEV):
            lo, hi = dd * c8, (dd + 1) * c8
            if gap and lo < gap_at < hi:
                cut = gap_at - lo
                o_ref[dd, :, :cut] = m_ref[:, lo:gap_at]
                o_ref[dd, :, cut:] = m_ref[:, gap_at + gap:hi + gap]
            else:
                off = lo + (gap if lo >= gap_at else 0)
                o_ref[dd] = m_ref[:, off:off + c8]

    return pl.pallas_call(
        body, name=name, out_shape=jax.ShapeDtypeStruct((N_DEV, rows, c8), mat.dtype), grid=(rows // tr,),
        in_specs=[pl.BlockSpec((tr, total), lambda i: (i, 0))], out_specs=pl.BlockSpec((N_DEV, tr, c8), lambda i: (0, i, 0)),
        compiler_params=_cparams(("arbitrary",)))(mat)


_MID = ("w_out_a", "w_out_b", "w_out", "w_mlp_up", "w_mlp_down")


def _local_step(x, target, wts, shards, cfg):
    dr, dc, d, lp, cb = cfg["dr"], cfg["dc"], cfg["d"], cfg["lp"], cfg["cb"]
    dff = shards["w_mlp_down"].shape[0] * N_DEV
    wmix = 3 * dr + lp
    xn = wts["xn"]
    half = d // 2
    (p_top,), (bot8,) = _matmul("mm_in_top", xn[:, :half], wts["w_top"], "nn", [F32],
                                comm=_Comm("gather", [shards["w_in_bot"]]))
    w_bot = _padded_from_slabs_call("relayout_w_bot", bot8, 3 * dr + cfg["lora"], lp - cfg["lora"], cfg["wall"])
    (p_all,), (g_oa, g_ob, g_o) = _matmul(
        "mm_in_bot", xn[:, half:], w_bot, "nn", [F32], epi=lambda r, top: (r + top,), extras=(p_top,),
        comm=_Comm("gather", [shards["w_out_a"], shards["w_out_b"], shards["w_out"]]))
    w_out_a, w_out_b, w_out = _cols(g_oa), _cols(g_ob), g_o.reshape(d, d)
    pm = _mix_fwd("mix_fwd", p_all, wts["mu_pad"], wmix, cb)
    prep_w = (wts["w0"], wts["a0"], wts["k_k"], wts["k_a"], wts["wd"], wts["wi"], wts["wg"])
    lw, k2, a_in, b_in, g = _prep_fwd("prep_fwd", pm, cfg, *prep_w)
    (y_raw, s_chk), (g_u,) = _rec_fwd(
        "rec_fwd", pm, lw, k2, a_in, b_in, cfg, comm=_Comm("gather", [shards["w_mlp_up"]]))
    w_up = _cols(g_u)
    post_w = (wts["lnx_w"], wts["lnx_b"], wts["r_k"])
    ya_in = _post_fwd("post_fwd", y_raw, pm, k2, g, *post_w, cfg)
    (ya,) = _matmul("mm_out_a", ya_in, w_out_a, "nn", [F32])
    yb_in = _conv_fwd("conv_fwd", p_all, wts["conv_w8"], cfg)
    (yb,) = _matmul("mm_out_b", yb_in, w_out_b, "nn", [F32])
    mg = _merge_fwd("merge_fwd", p_all, wts["gate_bias"], ya, yb, cfg)
    (mo,) = _matmul("mm_out", mg, w_out, "nn", [F32])
    h1, hn = _norm_fwd("norm_mlp_fwd", x, mo, wts["norm_mlp_w"], True)
    (u, act), (g_d,) = _matmul("mm_up", hn, w_up, "nn", [F32, BF16],
                               epi=lambda r: (r, jnp.square(jnp.maximum(r, 0.0))),
                               comm=_Comm("gather", [shards["w_mlp_down"]]))
    w_down = g_d.reshape(dff, d)
    (md,) = _matmul("mm_down", act, w_down, "nn", [F32])
    loss, dh2, dh2b, g_norm_final = _final("final", h1, md, target, wts["norm_final_w"])
    (du,) = _matmul("mm_down_dx", dh2b, w_down, "nt", [BF16],
                    epi=lambda r, uu: (r * (2.0 * jnp.maximum(uu, 0.0)),), extras=(u,))
    (g_down,) = _matmul("mm_down_dw", act, dh2b, "tn", [BF16])
    core = lax.axis_index("c").astype(jnp.int32).reshape(1)
    my_chip = 2 * lax.axis_index("x") + lax.axis_index("y")
    me = 2 * my_chip + lax.axis_index("c")
    own, recv = {}, {}

    def chip_own(chip_sum):
        return lax.dynamic_index_in_dim(chip_sum, my_chip, axis=0, keepdims=False)

    down_slabs = g_down.reshape(N_DEV, dff // N_DEV, d)
    (dhn,), (got,) = _matmul("mm_up_dx", du, w_up, "nt", [F32], comm=_Comm("pair", [down_slabs]))
    down_sum = _pair_sum("pair_sum_down", down_slabs, got, core)
    (g_up,) = _matmul("mm_up_dw", hn, du, "tn", [BF16], out_slabs=True)
    own["w_mlp_down"] = chip_own(down_sum)
    dh1, dh1b, g_norm_mlp = _norm_bwd("norm_mlp_bwd", h1, dhn, dh2, wts["norm_mlp_w"])
    (dmg,), (got,) = _matmul("mm_out_dx", dh1b, w_out, "nt", [F32], comm=_Comm("pair", [g_up]))
    up_sum = _pair_sum("pair_sum_up", g_up, got, core)
    own["w_mlp_up"] = chip_own(up_sum)
    (g_out,) = _matmul("mm_out_dw", mg, dh1b, "tn", [BF16])
    dpga, dpgb, dya, dyb, dba, dbb = _merge_bwd("merge_bwd", p_all, wts["gate_bias"], ya, yb, dmg, cfg)
    (dya_in,) = _matmul("mm_out_a_dx", dya, w_out_a, "nt", [F32])
    (g_out_a,) = _matmul("mm_out_a_dw", ya_in, dya, "tn", [BF16], out_slabs=True)
    (dyb_in,) = _matmul("mm_out_b_dx", dyb, w_out_b, "nt", [F32])
    (g_out_b,) = _matmul("mm_out_b_dw", yb_in, dyb, "tn", [BF16], out_slabs=True)
    dpb, dpc, dpu, g_conv8 = _conv_bwd("conv_bwd", p_all, wts["conv_w8"], dyb_in, cfg)
    dy_raw, dr_post, dk_post, dv_post, dg, g_lnw, g_lnb, g_rk = _post_bwd(
        "post_bwd", y_raw, pm, k2, g, *post_w, dya_in, cfg)
    (dr_rec, dlw, dk_rec, dv_rec, da_in, db_in), (recv["w_mlp_up"], recv["w_mlp_down"]) = _rec_bwd(
        "rec_bwd", pm, lw, k2, a_in, b_in, s_chk, dy_raw, cfg, comm=_Comm("chips", [up_sum, down_sum]))
    (dpm_r, dpm_k, dpm_v, dpl, g_w0, g_a0, g_kk, g_ka, g_wd, g_wi, g_wg) = _prep_bwd(
        "prep_bwd", pm, cfg, *prep_w, (dlw, dk_rec, dk_post, da_in, db_in, dg),
        (dr_rec, dr_post), (dv_rec, dv_post))
    mu = wts["mu_pad"]
    nb = dr // cb
    dps, dmus = [], []
    for s, dpm_s in enumerate((dpm_r, dpm_k, dpm_v)):
        dp_s, dmu_s = _mix_bwd("mix_bwd_%d" % s, [dpm_s], p_all, s * nb, mu[:, s * dr:(s + 1) * dr], cb)
        dps.append(dp_s)
        dmus.append(dmu_s)
    dp_l, dmu_l = _mix_bwd("mix_bwd_l", [dpl[j] for j in range(nb)], p_all, 3 * nb, mu[:, 3 * dr:], min(cb, lp))
    tail = [jnp.zeros((x.shape[0], cfg["wall"] - cfg["used"]), BF16)] if cfg["wall"] > cfg["used"] else []
    dp_all = jnp.concatenate(dps + [dp_l, dpb, dpc, dpu, dpga, dpgb] + tail, axis=1)
    ld, li, lora = cfg["ld"], cfg["li"], cfg["lora"]
    g_small = jnp.concatenate([g_wd[:ld], g_wi[ld:ld + li], g_wg[ld + li:lora], g_conv8[:3]], axis=0)
    g_small = jnp.pad(g_small, ((0, cfg["small_rows"] - g_small.shape[0]), (0, 0)))
    direct = dict(w_out_a=g_out_a, w_out_b=g_out_b, w_out=g_out.reshape(N_DEV, d // N_DEV, d),
                  small=_col_slabs(g_small))
    (g_all,), got4 = _matmul("mm_in_dw", xn, dp_all, "tn", [BF16],
                             comm=_Comm("exchange", list(direct.values())))
    for n, slabs, r in zip(direct, direct.values(), got4):
        own[n] = lax.dynamic_index_in_dim(slabs, me, axis=0, keepdims=False)
        recv[n] = r
    in_slabs = _slabs_from_padded_call("relayout_g_in", g_all, 3 * dr + lora, lp - lora,
                                       (cfg["used"] - lp + lora) // N_DEV)
    (got,) = _comm_call("pair_exchange", _Comm("pair", [in_slabs]))
    in_sum = _pair_sum("pair_sum_in", in_slabs, got, core)
    (dxn,), (recv["w_in"],) = _matmul("mm_in_dx", dp_all, (wts["w_top"], w_bot), "nt", [F32],
                                      comm=_Comm("chips", [in_sum]))
    own["w_in"] = chip_own(in_sum)
    grad_x, _, g_norm_mix = _norm_bwd("norm_mix_bwd", x, dxn, dh1, wts["norm_mix_w"])
    grads = dict(
        norm_mix_w=g_norm_mix, gate_bias=jnp.concatenate([dba, dbb], axis=1),
        mu_pad=jnp.concatenate(dmus + [dmu_l], axis=1), w0=g_w0, a0=g_a0, k_k=g_kk, k_a=g_ka,
        r_k=g_rk, lnx_w=g_lnw, lnx_b=g_lnb, norm_mlp_w=g_norm_mlp, norm_final_w=g_norm_final)
    return loss, grad_x, grads, own, recv


_SMALL = ("norm_mix_w", "gate_bias", "shift_mu", "w0", "a0", "k_k", "k_a", "r_k", "lnx_w", "lnx_b",
          "norm_mlp_w", "norm_final_w")
_ORDER = ("norm_mix_w", "w_in", "gate_bias", "shift_mu", "w0", "w_decay_up", "a0", "w_iclr_up", "w_gate_up",
          "k_k", "k_a", "r_k", "lnx_w", "lnx_b", "w_out_a", "conv_w", "w_out_b", "w_out", "norm_mlp_w",
          "w_mlp_up", "w_mlp_down", "norm_final_w")


def _step(x, target, w, m, v):
    t, d = x.shape[1], x.shape[2]
    dr = w["w0"].shape[-1]
    ld, li, lg = w["w_decay_up"].shape[1], w["w_iclr_up"].shape[1], w["w_gate_up"].shape[1]
    lora = ld + li + lg
    lp = _round_up(lora, LANES)
    dc = w["conv_w"].shape[-1] * N_DEV
    cb = math.gcd(math.gcd(lp, dr), 512)
    used = 3 * dr + lp + 3 * dc + 2 * d
    wall = _round_up(used, 1024 if used > MAX_FULL_K else LANES)
    small_rows = ld + li + lg + 3
    cfg = dict(d=d, dr=dr, dc=dc, lp=lp, cb=cb, off_conv=3 * dr + lp, off_gate=3 * dr + lp + 3 * dc, used=used,
               wall=wall, ld=ld, li=li, lora=lora, small_rows=_round_up(small_rows, SUBLANES))
    x2, tg2 = x[0], target[0]

    small_sh = jnp.concatenate([w["w_decay_up"][0], w["w_iclr_up"][0], w["w_gate_up"][0], w["conv_w"][0]], axis=0)
    small_sh = jnp.pad(small_sh, ((0, _round_up(small_rows, SUBLANES) - small_rows), (0, 0)))
    big = ("w_in",) + _MID
    w_in_b = w["w_in"][0].astype(BF16)
    (xn,), (top8, gsm) = _norm_fwd("norm_mix_fwd", x2, None, w["norm_mix_w"], False,
                                   comm=_Comm("gather", [w_in_b[:d // 2], small_sh]))
    shards = {n: w[n][0].astype(BF16) for n in _MID}
    shards["w_in_bot"] = w_in_b[d // 2:]
    w_top = _padded_from_slabs_call("relayout_w_top", top8, 3 * dr + lora, lp - lora, wall)
    sm = _cols(gsm)
    lora_full = sm[:lora]

    def lora_pad(lo, hi):
        rows = lax.broadcasted_iota(jnp.int32, (lp, 1), 0)
        full = jnp.pad(lora_full, ((0, lp - lora), (0, 0)))
        return jnp.where(jnp.logical_and(rows >= lo, rows < hi), full, 0.0)

    conv_w8 = jnp.pad(sm[lora:lora + 3], ((0, SUBLANES - 3), (0, 0)))
    mu_pad = jnp.pad(w["shift_mu"], ((0, 0), (0, lp - lora)))
    wts = dict(
        w_top=w_top, xn=xn, wd=lora_pad(0, ld), wi=lora_pad(ld, ld + li), wg=lora_pad(ld + li, lora), conv_w8=conv_w8,
        mu_pad=mu_pad, norm_mix_w=w["norm_mix_w"], gate_bias=w["gate_bias"], w0=w["w0"], a0=w["a0"],
        k_k=w["k_k"], k_a=w["k_a"], r_k=w["r_k"].reshape(1, dr), lnx_w=w["lnx_w"], lnx_b=w["lnx_b"],
        norm_mlp_w=w["norm_mlp_w"], norm_final_w=w["norm_final_w"].reshape(1, d))

    loss, grad_x, gr, own, received = _local_step(x2, tg2, wts, shards, cfg)

    small_g = dict(norm_mix_w=gr["norm_mix_w"], gate_bias=gr["gate_bias"], shift_mu=gr["mu_pad"][:, :3 * dr + lora],
                   w0=gr["w0"], a0=gr["a0"], k_k=gr["k_k"], k_a=gr["k_a"], r_k=gr["r_k"], lnx_w=gr["lnx_w"],
                   lnx_b=gr["lnx_b"], norm_mlp_w=gr["norm_mlp_w"], norm_final_w=gr["norm_final_w"])
    sizes = [small_g[n].size for n in _SMALL]
    total = sum(sizes) + 1
    prow = _round_up(total, LANES * SUBLANES) // LANES

    def pack(parts):
        flat = jnp.concatenate([p.reshape(-1) for p in parts])
        return jnp.pad(flat, (0, prow * LANES - flat.size)).reshape(prow, LANES)

    g_packed = _all_reduce_small("reduce_small", pack([small_g[n] for n in _SMALL] + [loss[0, :1]]))
    one = jnp.zeros((1,), F32)
    packed = [pack([d_[n] for n in _SMALL] + [one]) for d_ in (w, m, v)]
    sm_out = _adamw("adamw_small", *packed, g_packed)
    loss_out = g_packed.reshape(-1)[total - 1]

    def unpack(flat2d):
        flat = flat2d.reshape(-1)
        out, o = {}, 0
        for n, s in zip(_SMALL, sizes):
            out[n] = flat[o:o + s].reshape(w[n].shape)
            o += s
        return out

    res = [unpack(a) for a in sm_out]

    def shard2d(a):
        return a.reshape(-1, a.shape[-1])

    for n in big:
        outs = _adamw("adamw_" + n, shard2d(w[n]), shard2d(m[n]), shard2d(v[n]), shard2d(own[n]),
                      received[n].reshape(received[n].shape[:1] + shard2d(own[n]).shape))
        for r_, o in zip(res, outs):
            r_[n] = o.reshape(w[n].shape)
    sm_names = ("w_decay_up", "w_iclr_up", "w_gate_up", "conv_w")
    stack = lambda d_: jnp.pad(jnp.concatenate([d_[n][0] for n in sm_names], axis=0),
                               ((0, _round_up(small_rows, SUBLANES) - small_rows), (0, 0)))
    outs = _adamw("adamw_stack", stack(w), stack(m), stack(v), own["small"], received["small"])
    bounds = (0, ld, ld + li, lora, lora + 3)
    for r_, o in zip(res, outs):
        for q, n in enumerate(sm_names):
            r_[n] = o[bounds[q]:bounds[q + 1]].reshape(w[n].shape)

    grad, delta, new_m, new_v = res
    return (loss_out, grad_x[None], *[grad[n] for n in _ORDER], *[delta[n] for n in _ORDER],
            *[new_m[n] for n in _ORDER], *[new_v[n] for n in _ORDER])


def kernel(x, norm_mix_w, w_in, gate_bias, shift_mu, w0, w_decay_up, a0, w_iclr_up, w_gate_up, k_k, k_a, r_k, lnx_w, lnx_b, w_out_a, conv_w, w_out_b, w_out, norm_mlp_w, w_mlp_up, w_mlp_down, norm_final_w, loss_target, m_norm_mix_w, m_w_in, m_gate_bias, m_shift_mu, m_w0, m_w_decay_up, m_a0, m_w_iclr_up, m_w_gate_up, m_k_k, m_k_a, m_r_k, m_lnx_w, m_lnx_b, m_w_out_a, m_conv_w, m_w_out_b, m_w_out, m_norm_mlp_w, m_w_mlp_up, m_w_mlp_down, m_norm_final_w, v_norm_mix_w, v_w_in, v_gate_bias, v_shift_mu, v_w0, v_w_decay_up, v_a0, v_w_iclr_up, v_w_gate_up, v_k_k, v_k_a, v_r_k, v_lnx_w, v_lnx_b, v_w_out_a, v_conv_w, v_w_out_b, v_w_out, v_norm_mlp_w, v_w_mlp_up, v_w_mlp_down, v_norm_final_w):
    w = dict(zip(_ORDER, (norm_mix_w, w_in, gate_bias, shift_mu, w0, w_decay_up, a0, w_iclr_up, w_gate_up, k_k, k_a,
                          r_k, lnx_w, lnx_b, w_out_a, conv_w, w_out_b, w_out, norm_mlp_w, w_mlp_up, w_mlp_down,
                          norm_final_w)))
    m = dict(zip(_ORDER, (m_norm_mix_w, m_w_in, m_gate_bias, m_shift_mu, m_w0, m_w_decay_up, m_a0, m_w_iclr_up,
                          m_w_gate_up, m_k_k, m_k_a, m_r_k, m_lnx_w, m_lnx_b, m_w_out_a, m_conv_w, m_w_out_b,
                          m_w_out, m_norm_mlp_w, m_w_mlp_up, m_w_mlp_down, m_norm_final_w)))
    v = dict(zip(_ORDER, (v_norm_mix_w, v_w_in, v_gate_bias, v_shift_mu, v_w0, v_w_decay_up, v_a0, v_w_iclr_up,
                          v_w_gate_up, v_k_k, v_k_a, v_r_k, v_lnx_w, v_lnx_b, v_w_out_a, v_conv_w, v_w_out_b,
                          v_w_out, v_norm_mlp_w, v_w_mlp_up, v_w_mlp_down, v_norm_final_w)))
    return _step(x, loss_target, w, m, v)
```
